```python
import jax, jax.numpy as jnp
from jax import lax
import numpy as np

D_MODEL = 1024
BATCH = 8
SEQ = 8192
DEPTH = 1

CONV_WIDTH = 1024
CONV_SIZE = 31
HEAD_DIM = 64
HEADS_PER_GROUP = 8
DILATION_GROUPS = ((128, 1), (512, 4), (2048, 16))
N_GROUPS = len(DILATION_GROUPS)
N_ATT_HEADS = N_GROUPS * HEADS_PER_GROUP
ATT_QKV = N_ATT_HEADS * HEAD_DIM
ATT_OUT = HEADS_PER_GROUP * HEAD_DIM
ROT_DIM = HEAD_DIM // 4
ROPE_THETA = 500000.0
BLOCK = 128
MAX_POS_OFFSET = 4096
EPS = 1e-6
NEG_INF = -1e30

IN_SPLITS = (CONV_WIDTH, CONV_WIDTH, CONV_WIDTH,
             ATT_QKV, ATT_QKV, ATT_QKV, ATT_OUT,
             D_MODEL, D_MODEL)
IN_COLS = sum(IN_SPLITS)

kernel_name = "hybrid_conformer_conv_dilated_attention_gated_merge"


def _rmsnorm(x, g):
    xf = x.astype(jnp.float32)
    y = xf * lax.rsqrt(jnp.mean(xf * xf, axis=-1, keepdims=True) + EPS)
    return (y * g.astype(jnp.float32)).astype(x.dtype)


def _layernorm(x, g, b):
    xf = x.astype(jnp.float32)
    mu = jnp.mean(xf, axis=-1, keepdims=True)
    var = jnp.mean(jnp.square(xf - mu), axis=-1, keepdims=True)
    y = (xf - mu) * lax.rsqrt(var + EPS)
    return (y * g.astype(jnp.float32) + b.astype(jnp.float32)).astype(x.dtype)


def _partial_rope(t, positions):
    half = ROT_DIM // 2
    inv_freq = ROPE_THETA ** (-(jnp.arange(half, dtype=jnp.float32) * 2.0 / ROT_DIM))
    ang = positions.astype(jnp.float32)[..., None] * inv_freq
    cos = jnp.cos(ang)[:, :, None, :]
    sin = jnp.sin(ang)[:, :, None, :]
    tf = t.astype(jnp.float32)
    t1, t2 = tf[..., :half], tf[..., half:ROT_DIM]
    out = jnp.concatenate([t1 * cos - t2 * sin, t2 * cos + t1 * sin, tf[..., ROT_DIM:]], axis=-1)
    return out.astype(t.dtype)


def _dilated_window_group(q, k, v, window, dilation):
    b, s, h, e = q.shape
    L = s // dilation
    w_sub = window // dilation
    nb = -(-L // BLOCK)
    lp = nb * BLOCK

    def to_sub(t):
        return t.reshape(b, L, dilation, h, e).transpose(0, 2, 3, 1, 4)

    qs, ks, vs = to_sub(q), to_sub(k), to_sub(v)
    qs = jnp.pad(qs, ((0, 0), (0, 0), (0, 0), (0, lp - L), (0, 0)))
    ks = jnp.pad(ks, ((0, 0), (0, 0), (0, 0), (BLOCK, lp - L), (0, 0)))
    vs = jnp.pad(vs, ((0, 0), (0, 0), (0, 0), (BLOCK, lp - L), (0, 0)))
    qb = qs.reshape(b, dilation, h, nb, BLOCK, e)

    def band(t):
        prev = t[:, :, :, :lp].reshape(b, dilation, h, nb, BLOCK, e)
        cur = t[:, :, :, BLOCK:].reshape(b, dilation, h, nb, BLOCK, e)
        return jnp.concatenate([prev, cur], axis=-2)

    kb, vb = band(ks), band(vs)
    scores = jnp.einsum('bdhnqe,bdhnke->bdhnqk', qb.astype(jnp.float32),
                        kb.astype(jnp.float32)) * (e ** -0.5)
    qi = jnp.arange(BLOCK)[:, None]
    kj = jnp.arange(2 * BLOCK)[None, :]
    dist = qi + BLOCK - kj
    key_idx = jnp.arange(nb)[:, None, None] * BLOCK - BLOCK + kj[None]
    mask = (dist >= 0) & (dist <= w_sub) & (key_idx >= 0)
    scores = jnp.where(mask, scores, NEG_INF)
    m = jnp.max(scores, axis=-1)
    p = jnp.exp(scores - m[..., None])
    den = jnp.sum(p, axis=-1)
    o = jnp.einsum('bdhnqk,bdhnke->bdhnqe', p, vb.astype(jnp.float32)) / den[..., None]

    def from_sub(t):
        tail = t.shape[5:]
        t = t.reshape((b, dilation, h, lp) + tail)[:, :, :, :L]
        t = jnp.moveaxis(t, 3, 1)
        return t.reshape((b, s, h) + tail)

    return from_sub(o), from_sub(m), from_sub(den)


def _fwd_setup_inputs(seed: int = 0) -> dict:
    key = jax.random.key(seed)
    ks = jax.random.split(key, 16)
    f32 = jnp.float32
    x = jax.random.normal(ks[0], (BATCH, SEQ, D_MODEL), f32)
    c = jax.random.normal(ks[1], (BATCH, D_MODEL), f32)
    positions = (jnp.arange(SEQ, dtype=jnp.int32)[None, :]
                 + jax.random.randint(ks[2], (BATCH, 1), 0, MAX_POS_OFFSET, dtype=jnp.int32))
    norm_g = 1.0 + 0.05 * jax.random.normal(ks[3], (DEPTH, D_MODEL), f32)
    w_ada = 0.5 * D_MODEL ** -0.5 * jax.random.normal(ks[4], (DEPTH, D_MODEL, 3 * D_MODEL), f32)
    b_ada = 0.02 * jax.random.normal(ks[5], (DEPTH, 3 * D_MODEL), f32)
    w_in = D_MODEL ** -0.5 * jax.random.normal(ks[6], (DEPTH, D_MODEL, IN_COLS), f32)
    conv_w = CONV_SIZE ** -0.5 * jax.random.normal(ks[7], (DEPTH, CONV_SIZE, CONV_WIDTH), f32)
    conv_b = 0.02 * jax.random.normal(ks[8], (DEPTH, CONV_WIDTH), f32)
    conv_ln_g = 1.0 + 0.05 * jax.random.normal(ks[9], (DEPTH, CONV_WIDTH), f32)
    conv_ln_b = 0.02 * jax.random.normal(ks[10], (DEPTH, CONV_WIDTH), f32)
    w_conv_out = CONV_WIDTH ** -0.5 * jax.random.normal(ks[11], (DEPTH, CONV_WIDTH, D_MODEL), f32)
    w_att_out = ATT_OUT ** -0.5 * jax.random.normal(ks[12], (DEPTH, ATT_OUT, D_MODEL), f32)
    w_o = D_MODEL ** -0.5 * jax.random.normal(ks[13], (DEPTH, D_MODEL, D_MODEL), f32)
    final_g = 1.0 + 0.05 * jax.random.normal(ks[14], (D_MODEL,), f32)
    return {"x": x, "c": c, "positions": positions, "norm_g": norm_g,
            "w_ada": w_ada, "b_ada": b_ada, "w_in": w_in, "conv_w": conv_w,
            "conv_b": conv_b, "conv_ln_g": conv_ln_g, "conv_ln_b": conv_ln_b,
            "w_conv_out": w_conv_out, "w_att_out": w_att_out, "w_o": w_o,
            "final_g": final_g}


def _fwd_reference(x, c, positions, norm_g, w_ada, b_ada, w_in, conv_w, conv_b, conv_ln_g,
              conv_ln_b, w_conv_out, w_att_out, w_o, final_g):
    b, s, _ = x.shape
    split_idx = np.cumsum(IN_SPLITS)[:-1].tolist()
    for layer in range(DEPTH):
        mod = c @ w_ada[layer] + b_ada[layer]
        shift, scale, gate = [t[:, None, :] for t in jnp.split(mod, 3, axis=-1)]
        h = _rmsnorm(x, norm_g[layer]) * (1.0 + scale) + shift

        proj = h @ w_in[layer]
        (glu_a, glu_b, z_conv, q, k, v, z_att, g_conv, g_att) = jnp.split(proj, split_idx, axis=-1)

        u = glu_a * jax.nn.sigmoid(glu_b)
        u = lax.conv_general_dilated(
            u, conv_w[layer][:, None, :].astype(u.dtype), window_strides=(1,),
            padding=[(CONV_SIZE - 1, 0)], dimension_numbers=('NWC', 'WIO', 'NWC'),
            feature_group_count=CONV_WIDTH) + conv_b[layer]
        u = jax.nn.silu(_layernorm(u, conv_ln_g[layer], conv_ln_b[layer]))
        y_conv = (u * jax.nn.silu(z_conv)) @ w_conv_out[layer]

        q = _partial_rope(q.reshape(b, s, N_ATT_HEADS, HEAD_DIM), positions)
        k = _partial_rope(k.reshape(b, s, N_ATT_HEADS, HEAD_DIM), positions)
        v = v.reshape(b, s, N_ATT_HEADS, HEAD_DIM)
        outs, maxes, dens = [], [], []
        for gi, (window, dilation) in enumerate(DILATION_GROUPS):
            sl = slice(gi * HEADS_PER_GROUP, (gi + 1) * HEADS_PER_GROUP)
            o_g, m_g, d_g = _dilated_window_group(q[:, :, sl], k[:, :, sl], v[:, :, sl],
                                                  window, dilation)
            outs.append(o_g); maxes.append(m_g); dens.append(d_g)
        m_all = jnp.maximum(jnp.maximum(maxes[0], maxes[1]), maxes[2])
        wts = [d_g * jnp.exp(m_g - m_all) for m_g, d_g in zip(maxes, dens)]
        w_sum = wts[0] + wts[1] + wts[2]
        att = (wts[0][..., None] * outs[0] + wts[1][..., None] * outs[1]
               + wts[2][..., None] * outs[2]) / w_sum[..., None]
        att = att.reshape(b, s, ATT_OUT).astype(x.dtype)
        y_att = (att * jax.nn.silu(z_att)) @ w_att_out[layer]

        merged = jax.nn.sigmoid(g_conv) * y_conv + jax.nn.sigmoid(g_att) * y_att
        x = x + gate * (merged @ w_o[layer])
    return _rmsnorm(x, final_g)


import jax as _jax
import jax.numpy as _jnp

TWIN_FORMAT = 'train_step'
FWD_PARAMS = ['x', 'c', 'positions', 'norm_g', 'w_ada', 'b_ada', 'w_in', 'conv_w', 'conv_b', 'conv_ln_g', 'conv_ln_b', 'w_conv_out', 'w_att_out', 'w_o', 'final_g']
TWIN_WEIGHTS = ['norm_g', 'w_ada', 'b_ada', 'w_in', 'conv_w', 'conv_b', 'conv_ln_g', 'conv_ln_b', 'w_conv_out', 'w_att_out', 'w_o', 'final_g']
TWIN_DIFF_INPUT = 'x'
TWIN_INPUTS = ['x', 'c', 'positions', 'norm_g', 'w_ada', 'b_ada', 'w_in', 'conv_w', 'conv_b', 'conv_ln_g', 'conv_ln_b', 'w_conv_out', 'w_att_out', 'w_o', 'final_g', 'loss_target', 'm_norm_g', 'm_w_ada', 'm_b_ada', 'm_w_in', 'm_conv_w', 'm_conv_b', 'm_conv_ln_g', 'm_conv_ln_b', 'm_w_conv_out', 'm_w_att_out', 'm_w_o', 'm_final_g', 'v_norm_g', 'v_w_ada', 'v_b_ada', 'v_w_in', 'v_conv_w', 'v_conv_b', 'v_conv_ln_g', 'v_conv_ln_b', 'v_w_conv_out', 'v_w_att_out', 'v_w_o', 'v_final_g']
TWIN_OUTPUTS = ['loss', 'grad_x', 'grad_norm_g', 'grad_w_ada', 'grad_b_ada', 'grad_w_in', 'grad_conv_w', 'grad_conv_b', 'grad_conv_ln_g', 'grad_conv_ln_b', 'grad_w_conv_out', 'grad_w_att_out', 'grad_w_o', 'grad_final_g', 'delta_norm_g', 'delta_w_ada', 'delta_b_ada', 'delta_w_in', 'delta_conv_w', 'delta_conv_b', 'delta_conv_ln_g', 'delta_conv_ln_b', 'delta_w_conv_out', 'delta_w_att_out', 'delta_w_o', 'delta_final_g', 'new_m_norm_g', 'new_m_w_ada', 'new_m_b_ada', 'new_m_w_in', 'new_m_conv_w', 'new_m_conv_b', 'new_m_conv_ln_g', 'new_m_conv_ln_b', 'new_m_w_conv_out', 'new_m_w_att_out', 'new_m_w_o', 'new_m_final_g', 'new_v_norm_g', 'new_v_w_ada', 'new_v_b_ada', 'new_v_w_in', 'new_v_conv_w', 'new_v_conv_b', 'new_v_conv_ln_g', 'new_v_conv_ln_b', 'new_v_w_conv_out', 'new_v_w_att_out', 'new_v_w_o', 'new_v_final_g']
TWIN_LEAF_KINDS = {'loss': 'loss', 'grad_x': 'grad_x', 'grad_norm_g': 'grad_w', 'grad_w_ada': 'grad_w', 'grad_b_ada': 'grad_w', 'grad_w_in': 'grad_w', 'grad_conv_w': 'grad_w', 'grad_conv_b': 'grad_w', 'grad_conv_ln_g': 'grad_w', 'grad_conv_ln_b': 'grad_w', 'grad_w_conv_out': 'grad_w', 'grad_w_att_out': 'grad_w', 'grad_w_o': 'grad_w', 'grad_final_g': 'grad_w', 'delta_norm_g': 'delta_w', 'delta_w_ada': 'delta_w', 'delta_b_ada': 'delta_w', 'delta_w_in': 'delta_w', 'delta_conv_w': 'delta_w', 'delta_conv_b': 'delta_w', 'delta_conv_ln_g': 'delta_w', 'delta_conv_ln_b': 'delta_w', 'delta_w_conv_out': 'delta_w', 'delta_w_att_out': 'delta_w', 'delta_w_o': 'delta_w', 'delta_final_g': 'delta_w', 'new_m_norm_g': 'new_m', 'new_m_w_ada': 'new_m', 'new_m_b_ada': 'new_m', 'new_m_w_in': 'new_m', 'new_m_conv_w': 'new_m', 'new_m_conv_b': 'new_m', 'new_m_conv_ln_g': 'new_m', 'new_m_conv_ln_b': 'new_m', 'new_m_w_conv_out': 'new_m', 'new_m_w_att_out': 'new_m', 'new_m_w_o': 'new_m', 'new_m_final_g': 'new_m', 'new_v_norm_g': 'new_v', 'new_v_w_ada': 'new_v', 'new_v_b_ada': 'new_v', 'new_v_w_in': 'new_v', 'new_v_conv_w': 'new_v', 'new_v_conv_b': 'new_v', 'new_v_conv_ln_g': 'new_v', 'new_v_conv_ln_b': 'new_v', 'new_v_w_conv_out': 'new_v', 'new_v_w_att_out': 'new_v', 'new_v_w_o': 'new_v', 'new_v_final_g': 'new_v'}


def _forward(args):
    return _fwd_reference(*[args[k] for k in FWD_PARAMS])


def _output_shape():
    def fwd():
        inp = _fwd_setup_inputs(0)
        return _fwd_reference(*[inp[k] for k in FWD_PARAMS])
    out = _jax.eval_shape(fwd)
    return out.shape, out.dtype

N_MICROBATCH = 1
ADAM_LR = 0.001
ADAM_B1 = 0.9
ADAM_B2 = 0.999
ADAM_EPS = 1e-08
ADAM_WD = 0.01
ADAM_STEP = 10
PER_EXAMPLE_BATCH_AXIS = {'x': 0, 'c': 0, 'positions': 0, 'loss_target': 0}
SHARED_INPUTS = []
_WEIGHT_DTYPES = {'norm_g': _jnp.float32, 'w_ada': _jnp.float32, 'b_ada': _jnp.float32, 'w_in': _jnp.float32, 'conv_w': _jnp.float32, 'conv_b': _jnp.float32, 'conv_ln_g': _jnp.float32, 'conv_ln_b': _jnp.float32, 'w_conv_out': _jnp.float32, 'w_att_out': _jnp.float32, 'w_o': _jnp.float32, 'final_g': _jnp.float32}
MOMENT_SCALE = {'norm_g': 6.094189e-02, 'w_ada': 7.011408e-02, 'b_ada': 6.899225e-02, 'w_in': 2.211744e-02, 'conv_w': 3.385772e-02, 'conv_b': 5.395154e-02, 'conv_ln_g': 3.811942e-02, 'conv_ln_b': 3.297290e-02, 'w_conv_out': 3.170817e-02, 'w_att_out': 2.549804e-02, 'w_o': 4.078239e-02, 'final_g': 6.412782e+01}


def _to_microbatches(a, axis):
    t = _jnp.moveaxis(a, axis, 0)
    t = t.reshape((N_MICROBATCH, t.shape[0] // N_MICROBATCH) + t.shape[1:])
    return _jnp.moveaxis(t, 1, axis + 1)


def setup_inputs(seed: int = 0) -> dict:
    inp = _fwd_setup_inputs(seed)
    key = _jax.random.fold_in(_jax.random.key(seed), 7919)
    shape, _ = _output_shape()
    out = dict(inp)
    out["loss_target"] = _jax.random.normal(_jax.random.fold_in(key, 0), shape, _jnp.float32)
    for i, name in enumerate(TWIN_WEIGHTS):
        w = inp[name].astype(_jnp.float32)
        if MOMENT_SCALE is None:
            s = _jnp.sqrt(_jnp.mean(_jnp.square(w)) + 1e-30)
        else:
            s = MOMENT_SCALE[name]
        km, kv = _jax.random.split(_jax.random.fold_in(key, i + 1))
        out[name] = w
        out["m_" + name] = s * _jax.random.normal(km, w.shape, _jnp.float32)
        out["v_" + name] = (s * s) * _jax.random.uniform(kv, w.shape, _jnp.float32, 0.5, 1.5)
    if N_MICROBATCH > 1:
        for name, axis in PER_EXAMPLE_BATCH_AXIS.items():
            out[name] = _to_microbatches(out[name], axis)
    return {'x': out['x'], 'c': out['c'], 'positions': out['positions'], 'norm_g': out['norm_g'], 'w_ada': out['w_ada'], 'b_ada': out['b_ada'], 'w_in': out['w_in'], 'conv_w': out['conv_w'], 'conv_b': out['conv_b'], 'conv_ln_g': out['conv_ln_g'], 'conv_ln_b': out['conv_ln_b'], 'w_conv_out': out['w_conv_out'], 'w_att_out': out['w_att_out'], 'w_o': out['w_o'], 'final_g': out['final_g'], 'loss_target': out['loss_target'], 'm_norm_g': out['m_norm_g'], 'm_w_ada': out['m_w_ada'], 'm_b_ada': out['m_b_ada'], 'm_w_in': out['m_w_in'], 'm_conv_w': out['m_conv_w'], 'm_conv_b': out['m_conv_b'], 'm_conv_ln_g': out['m_conv_ln_g'], 'm_conv_ln_b': out['m_conv_ln_b'], 'm_w_conv_out': out['m_w_conv_out'], 'm_w_att_out': out['m_w_att_out'], 'm_w_o': out['m_w_o'], 'm_final_g': out['m_final_g'], 'v_norm_g': out['v_norm_g'], 'v_w_ada': out['v_w_ada'], 'v_b_ada': out['v_b_ada'], 'v_w_in': out['v_w_in'], 'v_conv_w': out['v_conv_w'], 'v_conv_b': out['v_conv_b'], 'v_conv_ln_g': out['v_conv_ln_g'], 'v_conv_ln_b': out['v_conv_ln_b'], 'v_w_conv_out': out['v_w_conv_out'], 'v_w_att_out': out['v_w_att_out'], 'v_w_o': out['v_w_o'], 'v_final_g': out['v_final_g']}


def _loss(weights, diff, rest, loss_target):
    with _jax.named_scope("forward"):
        args = {**rest, TWIN_DIFF_INPUT: diff, **{k: w.astype(_WEIGHT_DTYPES[k]) for k, w in weights.items()}}
        y = _forward(args)
    with _jax.named_scope("loss_head"):
        err = _jnp.square(y.astype(_jnp.float32) - loss_target)
        return 0.5 * _jnp.sum(_jnp.mean(err, axis=-1)) if err.ndim else 0.5 * err


def _adamw(w, g, m, v):
    m = ADAM_B1 * m + (1.0 - ADAM_B1) * g
    v = ADAM_B2 * v + (1.0 - ADAM_B2) * _jnp.square(g)
    m_hat = m / (1.0 - ADAM_B1 ** ADAM_STEP)
    v_hat = v / (1.0 - ADAM_B2 ** ADAM_STEP)
    delta = -ADAM_LR * (m_hat / (_jnp.sqrt(v_hat) + ADAM_EPS) + ADAM_WD * w)
    return delta, m, v


def reference(x, c, positions, norm_g, w_ada, b_ada, w_in, conv_w, conv_b, conv_ln_g, conv_ln_b, w_conv_out, w_att_out, w_o, final_g, loss_target, m_norm_g, m_w_ada, m_b_ada, m_w_in, m_conv_w, m_conv_b, m_conv_ln_g, m_conv_ln_b, m_w_conv_out, m_w_att_out, m_w_o, m_final_g, v_norm_g, v_w_ada, v_b_ada, v_w_in, v_conv_w, v_conv_b, v_conv_ln_g, v_conv_ln_b, v_w_conv_out, v_w_att_out, v_w_o, v_final_g):
    given = dict(x=x, c=c, positions=positions, norm_g=norm_g, w_ada=w_ada, b_ada=b_ada, w_in=w_in, conv_w=conv_w, conv_b=conv_b, conv_ln_g=conv_ln_g, conv_ln_b=conv_ln_b, w_conv_out=w_conv_out, w_att_out=w_att_out, w_o=w_o, final_g=final_g, loss_target=loss_target, m_norm_g=m_norm_g, m_w_ada=m_w_ada, m_b_ada=m_b_ada, m_w_in=m_w_in, m_conv_w=m_conv_w, m_conv_b=m_conv_b, m_conv_ln_g=m_conv_ln_g, m_conv_ln_b=m_conv_ln_b, m_w_conv_out=m_w_conv_out, m_w_att_out=m_w_att_out, m_w_o=m_w_o, m_final_g=m_final_g, v_norm_g=v_norm_g, v_w_ada=v_w_ada, v_b_ada=v_b_ada, v_w_in=v_w_in, v_conv_w=v_conv_w, v_conv_b=v_conv_b, v_conv_ln_g=v_conv_ln_g, v_conv_ln_b=v_conv_ln_b, v_w_conv_out=v_w_conv_out, v_w_att_out=v_w_att_out, v_w_o=v_w_o, v_final_g=v_final_g)
    weights = {n: given[n] for n in TWIN_WEIGHTS}
    shared = {n: given[n] for n in SHARED_INPUTS}
    per_example = {n: given[n] for n in ['x', 'c', 'positions']}
    grad_fn = _jax.value_and_grad(_loss, argnums=(0, 1))

    def one_microbatch(ex, loss_target):
        ex = dict(ex)
        diff = ex.pop(TWIN_DIFF_INPUT)
        return grad_fn(weights, diff, {**shared, **ex}, loss_target)

    if N_MICROBATCH == 1:
        loss, (grad_w, grad_x) = one_microbatch(per_example, given["loss_target"])
    else:
        def body(carry, xs):
            loss_sum, grad_sum = carry
            l_k, (gw_k, gx_k) = one_microbatch(xs[0], xs[1])
            with _jax.named_scope("update"):
                return (loss_sum + l_k, _jax.tree.map(_jnp.add, grad_sum, gw_k)), gx_k

        init = (_jnp.zeros((), _jnp.float32), _jax.tree.map(_jnp.zeros_like, weights))
        (loss, grad_w), grad_x = _jax.lax.scan(body, init, (per_example, given["loss_target"]))
    with _jax.named_scope("update"):
        delta_w, new_m, new_v = {}, {}, {}
        for n in TWIN_WEIGHTS:
            delta_w[n], new_m[n], new_v[n] = _adamw(weights[n], grad_w[n], given["m_" + n], given["v_" + n])
    return (loss, grad_x, *[grad_w[n] for n in TWIN_WEIGHTS], *[delta_w[n] for n in TWIN_WEIGHTS],
            *[new_m[n] for n in TWIN_WEIGHTS], *[new_v[n] for n in TWIN_WEIGHTS])
```

```python
import functools

import jax
import jax.numpy as jnp
from jax import lax
from jax.experimental import pallas as pl
from jax.experimental.pallas import tpu as pltpu

F32 = jnp.float32
BF16 = jnp.bfloat16
MESH = pl.DeviceIdType.MESH
ANY = pl.BlockSpec(memory_space=pl.ANY)
VM = pl.BlockSpec(memory_space=pltpu.VMEM)

EPS = 1e-6
NEG = -1e30
ATT_W = 512
DILATIONS = (1, 4, 16)
BLK = 128
CONV_K = 31
HALO = 32
ROT_DIM = 16
ROPE_THETA = 500000.0
COL_TILE = 512
LANES = 128
VMEM_LIMIT = 56 * 1024 * 1024

ADAM_LR, ADAM_B1, ADAM_B2, ADAM_EPS, ADAM_WD, ADAM_STEP = 0.001, 0.9, 0.999, 1e-08, 0.01, 10


def _params(sem=None, vmem=VMEM_LIMIT):
    return pltpu.CompilerParams(dimension_semantics=sem, vmem_limit_bytes=vmem)


def _dot(a, b):
    return jnp.dot(a, b, preferred_element_type=F32)


def _dot_nt(a, b):
    return lax.dot_general(a, b, (((1,), (1,)), ((), ())), preferred_element_type=F32)


def _dot_tn(a, b):
    return lax.dot_general(a, b, (((0,), (0,)), ((), ())), preferred_element_type=F32)


def _sig(x):
    return jax.nn.sigmoid(x)


def _dsilu(x, s):
    return s * (1.0 + x * (1.0 - s))


def _my_place():
    return lax.axis_index("x"), lax.axis_index("y"), lax.axis_index("c")


def _allgather_small(x_shard):
    m_per, n = x_shard.shape

    def body(x_ref, out_ref, send_sems, recv_sems, local_sem):
        x, y, c = _my_place()
        me, sibling = (x, y, c), (x, y, 1 - c)
        chips = [(1 - x, y), (x, 1 - y), (1 - x, 1 - y)]

        def rows(px, py, pc):
            return out_ref.at[pl.ds((4 * px + 2 * py + pc) * m_per, m_per), :]

        def copy(k, block, to, src=None):
            return pltpu.make_async_remote_copy(
                src_ref=rows(*block) if src is None else src, dst_ref=rows(*block),
                send_sem=send_sems.at[k], recv_sem=recv_sems.at[k],
                device_id=to, device_id_type=MESH)

        mine = pltpu.make_async_copy(x_ref, rows(*me), local_sem)
        mine.start()
        first = [copy(0, me, sibling, src=x_ref)]
        first += [copy(1 + j, me, (*chip, c), src=x_ref) for j, chip in enumerate(chips)]
        for cp in first:
            cp.start()
        passed = [copy(4 + j, (*chip, c), sibling) for j, chip in enumerate(chips)]
        for j, chip in enumerate(chips):
            copy(1 + j, (*chip, c), me).wait_recv()
            passed[j].start()
        copy(0, sibling, me).wait_recv()
        for j, chip in enumerate(chips):
            copy(4 + j, (*chip, 1 - c), me).wait_recv()
        for cp in first + passed:
            cp.wait_send()
        mine.wait()

    return pl.pallas_call(
        body, name="allgather_small",
        out_shape=jax.ShapeDtypeStruct((8 * m_per, n), x_shard.dtype),
        in_specs=[VM], out_specs=VM,
        scratch_shapes=[pltpu.SemaphoreType.DMA((7,)), pltpu.SemaphoreType.DMA((7,)),
                        pltpu.SemaphoreType.DMA],
    )(x_shard)


def _shard_window(ref, kind, p, n_shards=4):
    r, c = ref.shape
    if kind == "col":
        w = c // n_shards
        return ref.at[:, pl.ds(p * w, w)]
    w = r // n_shards
    return ref.at[pl.ds(p * w, w), :]


def _half_window(ref, kind, hc):
    r, c = ref.shape
    if kind == "col":
        return ref.at[pl.ds(hc * (r // 2), r // 2), :]
    return ref.at[:, pl.ds(hc * (c // 2), c // 2)]


def _allgather_weights(shards, kinds):
    n = len(shards)
    full_shapes = []
    for s, kind in zip(shards, kinds):
        r, c = s.shape
        full_shapes.append((r, 4 * c) if kind == "col" else (4 * r, c))

    def body(*refs):
        w_refs, out_refs = refs[:n], refs[n:2 * n]
        bf_refs = refs[2 * n:3 * n]
        send_sems, recv_sems, local_sems = refs[3 * n:]
        x, y, c = _my_place()
        p = 2 * x + y
        sibling = (x, y, 1 - c)
        chips = [(1 - x, y), (x, 1 - y), (1 - x, 1 - y)]
        locals_ = []
        for k in range(n):
            bf_refs[k][...] = w_refs[k][...].astype(BF16)
            cp = pltpu.make_async_copy(bf_refs[k], _shard_window(out_refs[k], kinds[k], p), local_sems.at[k])
            cp.start()
            locals_.append(cp)

        def landed(k, chip, hc):
            return _half_window(_shard_window(out_refs[k], kinds[k], 2 * chip[0] + chip[1]), kinds[k], hc)

        def ici(j, k, chip_src, to, src=None):
            return pltpu.make_async_remote_copy(
                src_ref=landed(k, chip_src, c) if src is None else src, dst_ref=landed(k, chip_src, c),
                send_sem=send_sems.at[j * n + k], recv_sem=recv_sems.at[j * n + k],
                device_id=to, device_id_type=MESH)

        first = []
        for j, chip in enumerate(chips):
            for k in range(n):
                cp = ici(j, k, (x, y), (*chip, c), src=_half_window(bf_refs[k], kinds[k], c))
                cp.start()
                first.append(cp)
        passed = []
        for j, chip in enumerate(chips):
            for k in range(n):
                ici(j, k, chip, (x, y, c)).wait_recv()
                cp = pltpu.make_async_remote_copy(
                    src_ref=landed(k, chip, c), dst_ref=landed(k, chip, c),
                    send_sem=send_sems.at[(3 + j) * n + k], recv_sem=recv_sems.at[(3 + j) * n + k],
                    device_id=sibling, device_id_type=MESH)
                cp.start()
                passed.append(cp)
        for j, chip in enumerate(chips):
            for k in range(n):
                pltpu.make_async_remote_copy(
                    src_ref=landed(k, chip, 1 - c), dst_ref=landed(k, chip, 1 - c),
                    send_sem=send_sems.at[(3 + j) * n + k], recv_sem=recv_sems.at[(3 + j) * n + k],
                    device_id=sibling, device_id_type=MESH).wait_recv()
        for cp in first + passed:
            cp.wait_send()
        for cp in locals_:
            cp.wait()

    return pl.pallas_call(
        body, name="allgather_weights",
        out_shape=[jax.ShapeDtypeStruct(s, BF16) for s in full_shapes],
        in_specs=[VM] * n, out_specs=[ANY] * n,
        scratch_shapes=[pltpu.VMEM(s.shape, BF16) for s in shards]
        + [pltpu.SemaphoreType.DMA((6 * n,)), pltpu.SemaphoreType.DMA((6 * n,)), pltpu.SemaphoreType.DMA((n,))],
        compiler_params=_params(),
    )(*shards)


def _reduce_pair_exchange(grads, kinds):
    n = len(grads)
    half_shapes = [(g.shape[0] // 2, g.shape[1]) if kind == "col" else (g.shape[0], g.shape[1] // 2)
                   for g, kind in zip(grads, kinds)]

    def body(*refs):
        g_refs, pa_refs = refs[:n], refs[n:2 * n]
        send_sems, recv_sems = refs[2 * n:]
        x, y, c = _my_place()
        cps = []
        for k in range(n):
            cp = pltpu.make_async_remote_copy(
                src_ref=_half_window(g_refs[k], kinds[k], 1 - c), dst_ref=pa_refs[k],
                send_sem=send_sems.at[k], recv_sem=recv_sems.at[k],
                device_id=(x, y, 1 - c), device_id_type=MESH)
            cp.start()
            cps.append(cp)
        for cp in cps:
            cp.wait()

    return pl.pallas_call(
        body, name="reduce_pair_exchange",
        out_shape=[jax.ShapeDtypeStruct(s, F32) for s in half_shapes],
        in_specs=[ANY] * n, out_specs=[ANY] * n,
        scratch_shapes=[pltpu.SemaphoreType.DMA((n,)), pltpu.SemaphoreType.DMA((n,))],
    )(*grads)


def _row_tile(rows, cols, itemsize=4, target=2 * 1024 * 1024, mult=16):
    t = rows
    while t % 2 == 0 and t // 2 >= mult and (t // 2) % mult == 0 and t * cols * itemsize > target:
        t //= 2
    return t


def _reduce_pair_sum(g, pa, kind, c_arr, name):
    hr, hc_ = pa.shape
    tr = _row_tile(hr, hc_)
    nb = hr // tr

    def body(c_ref, g_ref, pa_ref, o_ref):
        o_ref[...] = (g_ref[...] + pa_ref[...]).astype(BF16)

    if kind == "col":
        g_map = lambda i, c_ref: (c_ref[0] * nb + i, 0)
    else:
        g_map = lambda i, c_ref: (i, c_ref[0])
    return pl.pallas_call(
        body, name=name,
        grid_spec=pltpu.PrefetchScalarGridSpec(
            num_scalar_prefetch=1, grid=(nb,),
            in_specs=[pl.BlockSpec((tr, hc_), g_map), pl.BlockSpec((tr, hc_), lambda i, c_ref: (i, 0))],
            out_specs=pl.BlockSpec((tr, hc_), lambda i, c_ref: (i, 0))),
        out_shape=jax.ShapeDtypeStruct((hr, hc_), BF16),
        compiler_params=_params(("parallel",)),
    )(c_arr, g, pa)


def _half_shard_shape(full_shape, kind):
    r, c = full_shape
    return (r // 2, c // 4) if kind == "col" else (r // 4, c // 2)


def _reduce_to_owner(halves, kinds, full_shapes):
    n = len(halves)
    hs = [_half_shard_shape(fs, kind) for fs, kind in zip(full_shapes, kinds)]

    def body(*refs):
        h_refs, rc_refs = refs[:n], refs[n:2 * n]
        send_sems, recv_sems = refs[2 * n:]
        x, y, c = _my_place()
        chips = [(1 - x, y), (x, 1 - y), (1 - x, 1 - y)]
        cps = []
        for j, chip in enumerate(chips):
            pj = 2 * chip[0] + chip[1]
            for k in range(n):
                cp = pltpu.make_async_remote_copy(
                    src_ref=_shard_window(h_refs[k], kinds[k], pj), dst_ref=rc_refs[k].at[j],
                    send_sem=send_sems.at[j * n + k], recv_sem=recv_sems.at[j * n + k],
                    device_id=(*chip, c), device_id_type=MESH)
                cp.start()
                cps.append(cp)
        for cp in cps:
            cp.wait()

    return pl.pallas_call(
        body, name="reduce_to_owner",
        out_shape=[jax.ShapeDtypeStruct((3,) + s, BF16) for s in hs],
        in_specs=[ANY] * n, out_specs=[ANY] * n,
        scratch_shapes=[pltpu.SemaphoreType.DMA((3 * n,)), pltpu.SemaphoreType.DMA((3 * n,))],
    )(*halves)


def _reduce_finish(halves, recvd, kinds, full_shapes):
    n = len(halves)
    hs = [_half_shard_shape(fs, kind) for fs, kind in zip(full_shapes, kinds)]
    shard_shapes = [(fs[0], fs[1] // 4) if kind == "col" else (fs[0] // 4, fs[1])
                    for fs, kind in zip(full_shapes, kinds)]

    def body(*refs):
        h_refs, rc_refs, gs_refs = refs[:n], refs[n:2 * n], refs[2 * n:3 * n]
        own_refs, gh_refs = refs[3 * n:4 * n], refs[4 * n:5 * n]
        in_sems, loc_sems, send_sems, recv_sems = refs[5 * n:]
        x, y, c = _my_place()
        p = 2 * x + y
        loads = []
        for k in range(n):
            cp = pltpu.make_async_copy(_shard_window(h_refs[k], kinds[k], p), own_refs[k], in_sems.at[k])
            cp.start()
            loads.append(cp)
        outs = []
        for k in range(n):
            loads[k].wait()
            gh_refs[k][...] = (own_refs[k][...].astype(F32) + rc_refs[k][0].astype(F32)
                               + rc_refs[k][1].astype(F32) + rc_refs[k][2].astype(F32))
            dst = _half_window(gs_refs[k], kinds[k], c)
            lc = pltpu.make_async_copy(gh_refs[k], dst, loc_sems.at[k])
            lc.start()
            rc = pltpu.make_async_remote_copy(
                src_ref=gh_refs[k], dst_ref=dst, send_sem=send_sems.at[k], recv_sem=recv_sems.at[k],
                device_id=(x, y, 1 - c), device_id_type=MESH)
            rc.start()
            outs.append((lc, rc))
        for k, (lc, rc) in enumerate(outs):
            lc.wait()
            rc.wait_send()
            pltpu.make_async_remote_copy(
                src_ref=gh_refs[k], dst_ref=_half_window(gs_refs[k], kinds[k], 1 - c),
                send_sem=send_sems.at[k], recv_sem=recv_sems.at[k],
                device_id=(x, y, 1 - c), device_id_type=MESH).wait_recv()

    return pl.pallas_call(
        body, name="reduce_finish",
        out_shape=[jax.ShapeDtypeStruct(s, F32) for s in shard_shapes],
        in_specs=[ANY] * n + [VM] * n, out_specs=[ANY] * n,
        scratch_shapes=[pltpu.VMEM(s, BF16) for s in hs] + [pltpu.VMEM(s, F32) for s in hs]
        + [pltpu.SemaphoreType.DMA((n,)) for _ in range(4)],
        compiler_params=_params(),
    )(*halves, *recvd)


def _mod_part(c_all, w_ada_l, b_l):
    def body(c_ref, w_ref, b_ref, o_ref):
        o_ref[...] = _dot(c_ref[...].astype(BF16), w_ref[...].astype(BF16)) + b_ref[...]

    return pl.pallas_call(
        body, name="mod_part", out_shape=jax.ShapeDtypeStruct((8, w_ada_l.shape[1]), F32),
        in_specs=[VM, VM, VM], out_specs=VM, compiler_params=_params(),
    )(c_all, w_ada_l, b_l)


def _sum_devices(parts, m):
    def body(p_ref, o_ref):
        acc = p_ref[0:m, :]
        for d in range(1, 8):
            acc = acc + p_ref[d * m:(d + 1) * m, :]
        o_ref[...] = acc

    return pl.pallas_call(
        body, name="sum_devices", out_shape=jax.ShapeDtypeStruct((m, parts.shape[1]), F32),
        in_specs=[VM], out_specs=VM, compiler_params=_params(),
    )(parts)


def _grad_w_ada(c_all_t, dmod_l):
    d, w = c_all_t.shape[0], dmod_l.shape[1]

    def body(ct_ref, dm_ref, o_ref):
        acc = ct_ref[:, 0:1] * dm_ref[0:1, :]
        for b in range(1, 8):
            acc = acc + ct_ref[:, b:b + 1] * dm_ref[b:b + 1, :]
        o_ref[...] = acc

    return pl.pallas_call(
        body, name="grad_w_ada", out_shape=jax.ShapeDtypeStruct((d, w), F32),
        in_specs=[VM, VM], out_specs=VM, compiler_params=_params(),
    )(c_all_t, dmod_l)


def _adamw(w, g, m, v, name):
    r, c = w.shape
    tr = _row_tile(r, c, target=1024 * 1024, mult=8)
    c1 = 1.0 - ADAM_B1 ** ADAM_STEP
    c2 = 1.0 - ADAM_B2 ** ADAM_STEP

    def body(w_ref, g_ref, m_ref, v_ref, d_ref, nm_ref, nv_ref):
        gv = g_ref[...]
        nm = ADAM_B1 * m_ref[...] + (1.0 - ADAM_B1) * gv
        nv = ADAM_B2 * v_ref[...] + (1.0 - ADAM_B2) * (gv * gv)
        m_hat = nm / c1
        v_hat = nv / c2
        d_ref[...] = -ADAM_LR * (m_hat / (jnp.sqrt(v_hat) + ADAM_EPS) + ADAM_WD * w_ref[...])
        nm_ref[...] = nm
        nv_ref[...] = nv

    spec = pl.BlockSpec((tr, c), lambda i: (i, 0))
    return pl.pallas_call(
        body, name=name, grid=(r // tr,),
        out_shape=[jax.ShapeDtypeStruct((r, c), F32)] * 3,
        in_specs=[spec] * 4, out_specs=[spec] * 3,
        compiler_params=_params(("parallel",)),
    )(w, g, m, v)


def _rope_tables(pos_ref, tab_ref):
    ang = pos_ref[...].astype(F32) * tab_ref[0:1, :]
    cs, sn = jnp.cos(ang), jnp.sin(ang)
    return jnp.where(tab_ref[3:4, :] > 0, cs, 1.0), -sn * tab_ref[1:2, :], sn * tab_ref[2:3, :]


def _deinterleave_store(vals, slab, out_ref, d, ts, dtype):
    if d == 1:
        for s in range(4):
            out_ref[0, :, s * LANES:(s + 1) * LANES] = vals[s].astype(dtype)
        return
    for s in range(4):
        slab[s] = vals[s]
    for r in range(d):
        for s in range(4):
            out_ref[r, :, s * LANES:(s + 1) * LANES] = slab[s, pl.ds(r, ts // d, stride=d), :].astype(dtype)


def _interleave_load(blk_ref, slab, d, ts):
    if d == 1:
        return [blk_ref[0, :, s * LANES:(s + 1) * LANES] for s in range(4)]
    for r in range(d):
        for s in range(4):
            slab[s, pl.ds(r, ts // d, stride=d), :] = blk_ref[r, :, s * LANES:(s + 1) * LANES]
    return [slab[s] for s in range(4)]


def _input_projection(x, pos, modv, norm_g, ropetab, w_in, ts):
    s_len, d_model = x.shape
    ci = w_in.shape[1]
    ct = COL_TILE
    nc, ng = 3 * d_model // ct, (ATT_W + 2 * d_model) // ct
    nct = nc + 9 + ng
    assert ci == nct * ct and s_len % ts == 0

    def body(x_ref, pos_ref, mod_ref, g_ref, tab_ref, w_ref,
             pc_ref, q0_ref, q1_ref, q2_ref, pg_ref, h_ref, h_scr, slab, rc, ra, rb):
        j = pl.program_id(1)

        @pl.when(j == 0)
        def _():
            xv = x_ref[...]
            r = lax.rsqrt(jnp.mean(xv * xv, axis=-1, keepdims=True) + EPS)
            h = (xv * r) * g_ref[...] * (1.0 + mod_ref[1:2, :]) + mod_ref[0:1, :]
            hb = h.astype(BF16)
            h_scr[...] = hb
            h_ref[...] = hb
            c_, a_, b_ = _rope_tables(pos_ref, tab_ref)
            rc[...] = c_
            ra[...] = a_
            rb[...] = b_

        res = _dot(h_scr[...], w_ref[...])

        @pl.when(j < nc)
        def _():
            pc_ref[...] = res

        for gi, (dil, q_ref) in enumerate(zip(DILATIONS, (q0_ref, q1_ref, q2_ref))):
            lo = nc + 3 * gi

            @pl.when((j >= lo) & (j < lo + 3))
            def _(dil=dil, q_ref=q_ref, lo=lo):
                roped = j < lo + 2
                vals = []
                for s in range(4):
                    t = res[:, s * LANES:(s + 1) * LANES]
                    rot = (t * rc[...] + pltpu.roll(t, LANES - 8, 1) * ra[...]
                           + pltpu.roll(t, 8, 1) * rb[...])
                    vals.append(jnp.where(roped, rot, t))
                _deinterleave_store(vals, slab, q_ref, dil, ts, BF16)

        @pl.when(j >= nc + 9)
        def _():
            pg_ref[...] = res

    def w_map(i, j):
        t = j - nc
        qcol = nc + (t % 3) * 3 + t // 3
        return (0, jnp.where((j >= nc) & (j < nc + 9), qcol, j))

    def q_spec(gi, dil):
        return pl.BlockSpec((None, dil, ts // dil, ATT_W),
                            lambda i, j: (jnp.clip(j - nc - 3 * gi, 0, 2), 0, i, 0))

    out_shape = [jax.ShapeDtypeStruct((s_len, 3 * d_model), F32)]
    out_shape += [jax.ShapeDtypeStruct((3, dil, s_len // dil, ATT_W), BF16) for dil in DILATIONS]
    out_shape += [jax.ShapeDtypeStruct((s_len, ATT_W + 2 * d_model), F32),
                  jax.ShapeDtypeStruct((s_len, d_model), BF16)]
    return pl.pallas_call(
        body, name="input_projection", grid=(s_len // ts, nct),
        in_specs=[pl.BlockSpec((ts, d_model), lambda i, j: (i, 0)),
                  pl.BlockSpec((ts, 1), lambda i, j: (i, 0)),
                  pl.BlockSpec((8, d_model), lambda i, j: (0, 0)),
                  pl.BlockSpec((1, d_model), lambda i, j: (0, 0)),
                  pl.BlockSpec((8, LANES), lambda i, j: (0, 0)),
                  pl.BlockSpec((d_model, ct), w_map)],
        out_specs=[pl.BlockSpec((ts, ct), lambda i, j: (i, jnp.clip(j, 0, nc - 1)))]
        + [q_spec(gi, dil) for gi, dil in enumerate(DILATIONS)]
        + [pl.BlockSpec((ts, ct), lambda i, j: (i, jnp.clip(j - nc - 9, 0, ng - 1))),
           pl.BlockSpec((ts, d_model), lambda i, j: (i, 0))],
        out_shape=out_shape,
        scratch_shapes=[pltpu.VMEM((ts, d_model), BF16), pltpu.VMEM((4, ts, LANES), F32),
                        pltpu.VMEM((ts, LANES), F32), pltpu.VMEM((ts, LANES), F32),
                        pltpu.VMEM((ts, LANES), F32)],
        compiler_params=_params(("arbitrary", "arbitrary")),
    )(x, pos, modv, norm_g, ropetab, w_in)


def _layernorm_stats(u1):
    mu = jnp.mean(u1, axis=-1, keepdims=True)
    xc = u1 - mu
    rstd = lax.rsqrt(jnp.mean(xc * xc, axis=-1, keepdims=True) + EPS)
    return xc * rstd, rstd


def _conv_forward(p_conv, conv_w, conv_b, ln_g, ln_b, w_co, ts):
    s_len, d3 = p_conv.shape
    dm = d3 // 3

    def body(p_ref, cw_ref, cb_ref, g_ref, b_ref, w_ref, y_ref, u0_ref, u1_ref, win):
        i = pl.program_id(0)

        @pl.when(i == 0)
        def _():
            win[0:HALO, :] = jnp.zeros((HALO, dm), F32)

        a, b, z = p_ref[:, 0:dm], p_ref[:, dm:2 * dm], p_ref[:, 2 * dm:3 * dm]
        u0 = a * _sig(b)
        win[HALO:HALO + ts, :] = u0
        u0_ref[...] = u0
        acc = jnp.broadcast_to(cb_ref[...], (ts, dm))
        for k in range(CONV_K):
            acc = acc + cw_ref[k:k + 1, :] * win[pl.ds(HALO - (CONV_K - 1) + k, ts), :]
        u1_ref[...] = acc
        xh, _ = _layernorm_stats(acc)
        u2 = xh * g_ref[...] + b_ref[...]
        a_conv = (u2 * _sig(u2)) * (z * _sig(z))
        y_ref[...] = _dot(a_conv.astype(BF16), w_ref[...])
        win[0:HALO, :] = win[ts:ts + HALO, :]

    row = pl.BlockSpec((1, dm), lambda i: (0, 0))
    tile = pl.BlockSpec((ts, dm), lambda i: (i, 0))
    return pl.pallas_call(
        body, name="conv_forward", grid=(s_len // ts,),
        in_specs=[pl.BlockSpec((ts, d3), lambda i: (i, 0)),
                  pl.BlockSpec((HALO, dm), lambda i: (0, 0)), row, row, row,
                  pl.BlockSpec((dm, dm), lambda i: (0, 0))],
        out_specs=[tile, tile, tile],
        out_shape=[jax.ShapeDtypeStruct((s_len, dm), F32)] * 3,
        scratch_shapes=[pltpu.VMEM((ts + HALO, dm), F32)],
        compiler_params=_params(("arbitrary",)),
    )(p_conv, conv_w, conv_b, ln_g, ln_b, w_co)


def _conv_backward(dyc, p_conv, u0, u1, conv_w, ln_g, ln_b, w_co, ts):
    s_len, d3 = p_conv.shape
    dm = d3 // 3
    nt = s_len // ts
    hb = ts // HALO

    def body(dy_ref, p_ref, u0_ref, uh_ref, u1_ref, cw_ref, g_ref, b_ref, w_ref,
             dp_ref, gw_ref, gs_ref, gcw_ref, dwin, uwin):
        i = pl.program_id(0)
        ti = nt - 1 - i

        @pl.when(i == 0)
        def _():
            gw_ref[...] = jnp.zeros_like(gw_ref)
            gs_ref[...] = jnp.zeros_like(gs_ref)
            gcw_ref[...] = jnp.zeros_like(gcw_ref)
            dwin[ts:ts + HALO, :] = jnp.zeros((HALO, dm), F32)

        dy = dy_ref[...]
        a, b, z = p_ref[:, 0:dm], p_ref[:, dm:2 * dm], p_ref[:, 2 * dm:3 * dm]
        d_ac = _dot_nt(dy, w_ref[...])
        xh, rstd = _layernorm_stats(u1_ref[...])
        u2 = xh * g_ref[...] + b_ref[...]
        sg2, sgz = _sig(u2), _sig(z)
        u3, sz = u2 * sg2, z * sgz
        gw_ref[...] += _dot_tn((u3 * sz).astype(BF16), dy)
        d_z = d_ac * u3 * _dsilu(z, sgz)
        d_u2 = d_ac * sz * _dsilu(u2, sg2)
        gs_ref[0:1, :] += jnp.sum(d_u2 * xh, axis=0, keepdims=True)
        gs_ref[1:2, :] += jnp.sum(d_u2, axis=0, keepdims=True)
        dxh = d_u2 * g_ref[...]
        d_u1 = rstd * (dxh - jnp.mean(dxh, axis=-1, keepdims=True)
                       - xh * jnp.mean(dxh * xh, axis=-1, keepdims=True))
        gs_ref[2:3, :] += jnp.sum(d_u1, axis=0, keepdims=True)
        dwin[0:ts, :] = d_u1
        uwin[0:HALO, :] = jnp.where(ti == 0, 0.0, uh_ref[...])
        uwin[HALO:HALO + ts, :] = u0_ref[...]
        d_u0 = jnp.zeros((ts, dm), F32)
        for k in range(CONV_K):
            gcw_ref[k:k + 1, :] += jnp.sum(
                d_u1 * uwin[pl.ds(HALO - (CONV_K - 1) + k, ts), :], axis=0, keepdims=True)
            d_u0 = d_u0 + cw_ref[k:k + 1, :] * dwin[pl.ds(CONV_K - 1 - k, ts), :]
        sgb = _sig(b)
        dp_ref[:, 0:dm] = (d_u0 * sgb).astype(BF16)
        dp_ref[:, dm:2 * dm] = (d_u0 * a * sgb * (1.0 - sgb)).astype(BF16)
        dp_ref[:, 2 * dm:3 * dm] = d_z.astype(BF16)
        dwin[ts:ts + HALO, :] = dwin[0:HALO, :]

    rev = lambda i: (nt - 1 - i, 0)
    row = pl.BlockSpec((1, dm), lambda i: (0, 0))
    tile = pl.BlockSpec((ts, dm), rev)
    return pl.pallas_call(
        body, name="conv_backward", grid=(nt,),
        in_specs=[tile, pl.BlockSpec((ts, d3), rev), tile,
                  pl.BlockSpec((HALO, dm), lambda i: (jnp.maximum((nt - 1 - i) * hb - 1, 0), 0)),
                  tile, pl.BlockSpec((HALO, dm), lambda i: (0, 0)), row, row,
                  pl.BlockSpec((dm, dm), lambda i: (0, 0))],
        out_specs=[pl.BlockSpec((ts, d3), rev), pl.BlockSpec((dm, dm), lambda i: (0, 0)),
                   pl.BlockSpec((8, dm), lambda i: (0, 0)), pl.BlockSpec((HALO, dm), lambda i: (0, 0))],
        out_shape=[jax.ShapeDtypeStruct((s_len, d3), BF16), jax.ShapeDtypeStruct((dm, dm), F32),
                   jax.ShapeDtypeStruct((8, dm), F32), jax.ShapeDtypeStruct((HALO, dm), F32)],
        scratch_shapes=[pltpu.VMEM((ts + HALO, dm), F32), pltpu.VMEM((ts + HALO, dm), F32)],
        compiler_params=_params(("arbitrary",)),
    )(dyc, p_conv, u0, u0, u1, conv_w, ln_g, ln_b, w_co)


def _att_masks():
    lane = lax.broadcasted_iota(jnp.int32, (BLK, BLK), 1)
    row = lax.broadcasted_iota(jnp.int32, (BLK, BLK), 0)
    return lane < 64, lane <= row, lane >= row


def _attention_forward(qkv, seq_len, qt, name):
    s_len = qkv.shape[1]
    nq = qt // BLK
    tiles_per_seq = seq_len // qt

    def body(q_ref, k_ref, v_ref, kh_ref, vh_ref, o_ref, lse_ref):
        first = (pl.program_id(0) % tiles_per_seq) == 0
        head0, m_cur, m_prev = _att_masks()
        for p in range(4):
            sl = slice(p * LANES, (p + 1) * LANES)
            for n in range(nq):
                rows = slice(n * BLK, (n + 1) * BLK)
                q2, kc, vc = q_ref[rows, sl], k_ref[rows, sl], v_ref[rows, sl]
                if n == 0:
                    kp, vp, pmask = kh_ref[:, sl], vh_ref[:, sl], m_prev & jnp.logical_not(first)
                else:
                    prow = slice((n - 1) * BLK, n * BLK)
                    kp, vp, pmask = k_ref[prow, sl], v_ref[prow, sl], m_prev
                outs, lses = [], []
                for h in range(2):
                    hm = head0 if h == 0 else jnp.logical_not(head0)
                    qh = jnp.where(hm, q2, jnp.zeros_like(q2))
                    sc = jnp.where(m_cur, _dot_nt(qh, kc) * 0.125, NEG)
                    sp = jnp.where(pmask, _dot_nt(qh, kp) * 0.125, NEG)
                    m = jnp.maximum(jnp.max(sc, axis=-1, keepdims=True), jnp.max(sp, axis=-1, keepdims=True))
                    pc, pp = jnp.exp(sc - m), jnp.exp(sp - m)
                    den = jnp.sum(pc, axis=-1, keepdims=True) + jnp.sum(pp, axis=-1, keepdims=True)
                    o = _dot(pc.astype(BF16), vc) + _dot(pp.astype(BF16), vp)
                    outs.append(o / den)
                    lses.append(jnp.broadcast_to(m + jnp.log(den), (BLK, LANES)))
                o_ref[rows, sl] = jnp.where(head0, outs[0], outs[1])
                lse_ref[rows, sl] = jnp.where(head0, lses[0], lses[1])

    def which(w):
        return pl.BlockSpec((None, qt, ATT_W), lambda i: (w, i, 0))

    def halo(w):
        return pl.BlockSpec((None, BLK, ATT_W), lambda i: (w, jnp.maximum(i * nq - 1, 0), 0))

    out = pl.BlockSpec((qt, ATT_W), lambda i: (i, 0))
    return pl.pallas_call(
        body, name=name, grid=(s_len // qt,),
        in_specs=[which(0), which(1), which(2), halo(1), halo(2)],
        out_specs=[out, out], out_shape=[jax.ShapeDtypeStruct((s_len, ATT_W), F32)] * 2,
        compiler_params=_params(("parallel",)),
    )(qkv, qkv, qkv, qkv, qkv)


def _attention_backward(qkv, d_att, lse, delta, seq_len, qt, name):
    s_len = qkv.shape[1]
    nq = qt // BLK
    tiles_per_seq = seq_len // qt
    nblk = s_len // BLK

    def body(q_ref, k_ref, v_ref, kh_ref, vh_ref, do_ref, lse_ref, dl_ref,
             qn_ref, don_ref, lsen_ref, dln_ref, dq_ref, dk_ref, dv_ref):
        i = pl.program_id(0)
        first = (i % tiles_per_seq) == 0
        last = (i % tiles_per_seq) == tiles_per_seq - 1
        head0, m_cur, m_prev = _att_masks()

        def pair(q2, do2, lse2, dl2, k2, v2, valid, want_dq, want_dkv):
            dqs, dks, dvs = [], [], []
            for h in range(2):
                hm = head0 if h == 0 else jnp.logical_not(head0)
                qh = jnp.where(hm, q2, jnp.zeros_like(q2))
                doh = jnp.where(hm, do2, jnp.zeros_like(do2))
                s = _dot_nt(qh, k2) * 0.125
                p = jnp.exp(jnp.where(valid, s - lse2[:, h * 64:h * 64 + 1], NEG))
                dp = _dot_nt(doh, v2)
                ds = (p * (dp - dl2[:, h * 64:h * 64 + 1]) * 0.125).astype(BF16)
                if want_dq:
                    dqs.append(_dot(ds, k2))
                if want_dkv:
                    dks.append(_dot_tn(ds, q2))
                    dvs.append(_dot_tn(p.astype(BF16), do2))
            dq = jnp.where(head0, dqs[0], dqs[1]) if want_dq else None
            dk = jnp.where(head0, dks[0], dks[1]) if want_dkv else None
            dv = jnp.where(head0, dvs[0], dvs[1]) if want_dkv else None
            return dq, dk, dv

        for p_ in range(4):
            sl = slice(p_ * LANES, (p_ + 1) * LANES)
            for n in range(nq):
                rows = slice(n * BLK, (n + 1) * BLK)
                q2, do2, lse2, dl2 = q_ref[rows, sl], do_ref[rows, sl], lse_ref[rows, sl], dl_ref[rows, sl]
                dq, dk, dv = pair(q2, do2, lse2, dl2, k_ref[rows, sl], v_ref[rows, sl], m_cur, True, True)
                dk_ref[rows, sl] = dk
                dv_ref[rows, sl] = dv
                if n == 0:
                    dq2, _, _ = pair(q2, do2, lse2, dl2, kh_ref[:, sl], vh_ref[:, sl],
                                     m_prev & jnp.logical_not(first), True, False)
                else:
                    prow = slice((n - 1) * BLK, n * BLK)
                    dq2, dk2, dv2 = pair(q2, do2, lse2, dl2, k_ref[prow, sl], v_ref[prow, sl], m_prev, True, True)
                    dk_ref[prow, sl] += dk2
                    dv_ref[prow, sl] += dv2
                dq_ref[rows, sl] = dq + dq2
            lrow = slice((nq - 1) * BLK, nq * BLK)
            _, dk3, dv3 = pair(qn_ref[:, sl], don_ref[:, sl], lsen_ref[:, sl], dln_ref[:, sl],
                               k_ref[lrow, sl], v_ref[lrow, sl], m_prev & jnp.logical_not(last), False, True)
            dk_ref[lrow, sl] += dk3
            dv_ref[lrow, sl] += dv3

    def which(w):
        return pl.BlockSpec((None, qt, ATT_W), lambda i: (w, i, 0))

    def prev(w):
        return pl.BlockSpec((None, BLK, ATT_W), lambda i: (w, jnp.maximum(i * nq - 1, 0), 0))

    tile = pl.BlockSpec((qt, ATT_W), lambda i: (i, 0))
    nxt = pl.BlockSpec((BLK, ATT_W), lambda i: (jnp.minimum((i + 1) * nq, nblk - 1), 0))
    nxt_q = pl.BlockSpec((None, BLK, ATT_W), lambda i: (0, jnp.minimum((i + 1) * nq, nblk - 1), 0))
    return pl.pallas_call(
        body, name=name, grid=(s_len // qt,),
        in_specs=[which(0), which(1), which(2), prev(1), prev(2), tile, tile, tile, nxt_q, nxt, nxt, nxt],
        out_specs=[tile] * 3, out_shape=[jax.ShapeDtypeStruct((s_len, ATT_W), F32)] * 3,
        compiler_params=_params(("parallel",)),
    )(qkv, qkv, qkv, qkv, qkv, d_att, lse, delta, qkv, d_att, lse, delta)


def _merge_and_head(o_g, lse_g, p_gate, y_conv, x, tgt, modv, final_g, w_ao, w_o, ts):
    s_len, dm = x.shape
    gw = ATT_W + 2 * dm
    nt = s_len // ts

    def body(o0, o1, o2, l0, l1, l2, pg_ref, yc_ref, x_ref, t_ref, mod_ref, fg_ref, wao_ref, wo_ref,
             loss_ref, dx_ref, dyc_ref, dpg_ref, da0, da1, da2, ls0, ls1, ls2, de0, de1, de2,
             gwo_ref, gwao_ref, gs_ref, slab):
        i = pl.program_id(0)

        @pl.when(i == 0)
        def _():
            loss_ref[...] = jnp.zeros_like(loss_ref)
            gwo_ref[...] = jnp.zeros_like(gwo_ref)
            gwao_ref[...] = jnp.zeros_like(gwao_ref)
            gs_ref[...] = jnp.zeros_like(gs_ref)

        os_, ls_ = [], []
        for dil, o_ref, l_ref in zip(DILATIONS, (o0, o1, o2), (l0, l1, l2)):
            os_.append(jnp.concatenate(_interleave_load(o_ref, slab, dil, ts), axis=1))
            ls_.append(jnp.concatenate(_interleave_load(l_ref, slab, dil, ts), axis=1))
        mx = jnp.maximum(jnp.maximum(ls_[0], ls_[1]), ls_[2])
        wts = [jnp.exp(l - mx) for l in ls_]
        wsum = wts[0] + wts[1] + wts[2]
        att = (wts[0] * os_[0] + wts[1] * os_[1] + wts[2] * os_[2]) / wsum
        lse_all = mx + jnp.log(wsum)

        z_att, g_conv, g_att = pg_ref[:, 0:ATT_W], pg_ref[:, ATT_W:ATT_W + dm], pg_ref[:, ATT_W + dm:gw]
        sgz = _sig(z_att)
        sz = z_att * sgz
        a_att = (att * sz).astype(BF16)
        y_att = _dot(a_att, wao_ref[...])
        y_conv = yc_ref[...]
        sgc, sga = _sig(g_conv), _sig(g_att)
        merged = (sgc * y_conv + sga * y_att).astype(BF16)
        mo = _dot(merged, wo_ref[...])
        gate = mod_ref[2:3, :]
        x2 = x_ref[...] + gate * mo
        r = lax.rsqrt(jnp.mean(x2 * x2, axis=-1, keepdims=True) + EPS)
        xr = x2 * r
        err = xr * fg_ref[...] - t_ref[...]
        loss_ref[...] += 0.5 * jnp.sum(jnp.mean(err * err, axis=-1, keepdims=True))
        dy = err * (1.0 / dm)
        gs_ref[0:1, :] += jnp.sum(dy * xr, axis=0, keepdims=True)
        dyg = dy * fg_ref[...]
        d_x2 = r * dyg - xr * (r * jnp.mean(dyg * xr, axis=-1, keepdims=True))
        dx_ref[...] = d_x2
        gs_ref[1:2, :] += jnp.sum(d_x2 * mo, axis=0, keepdims=True)
        d_mo = (d_x2 * gate).astype(BF16)
        d_mg = _dot_nt(d_mo, wo_ref[...])
        gwo_ref[...] += _dot_tn(merged, d_mo)
        dyc_ref[...] = (d_mg * sgc).astype(BF16)
        dpg_ref[:, ATT_W:ATT_W + dm] = (d_mg * y_conv * sgc * (1.0 - sgc)).astype(BF16)
        d_ya = (d_mg * sga).astype(BF16)
        dpg_ref[:, ATT_W + dm:gw] = (d_mg * y_att * sga * (1.0 - sga)).astype(BF16)
        gwao_ref[...] += _dot_tn(a_att, d_ya)
        d_aa = _dot_nt(d_ya, wao_ref[...])
        dpg_ref[:, 0:ATT_W] = (d_aa * att * _dsilu(z_att, sgz)).astype(BF16)
        d_att = d_aa * sz
        ri = lax.broadcasted_iota(jnp.int32, (ATT_W, ATT_W), 0) // 64
        ci = lax.broadcasted_iota(jnp.int32, (ATT_W, ATT_W), 1) // 64
        ones = jnp.where(ri == ci, 1.0, 0.0).astype(BF16)
        prod = d_att * att
        hi = prod.astype(BF16)
        lo = (prod - hi.astype(F32)).astype(BF16)
        delta = _dot(hi, ones) + _dot(lo, ones)
        for val, refs, dt in ((d_att, (da0, da1, da2), BF16), (lse_all, (ls0, ls1, ls2), F32),
                              (delta, (de0, de1, de2), F32)):
            vals = [val[:, s * LANES:(s + 1) * LANES] for s in range(4)]
            for dil, ref in zip(DILATIONS, refs):
                _deinterleave_store(vals, slab, ref, dil, ts, dt)

    def grp(dil):
        return pl.BlockSpec((dil, ts // dil, ATT_W), lambda i: (0, i, 0))

    tile = pl.BlockSpec((ts, dm), lambda i: (i, 0))
    gate_tile = pl.BlockSpec((ts, gw), lambda i: (i, 0))
    const = lambda shp: pl.BlockSpec(shp, lambda i: tuple(0 for _ in shp))
    grp_shape = lambda dt: [jax.ShapeDtypeStruct((dil, s_len // dil, ATT_W), dt) for dil in DILATIONS]
    return pl.pallas_call(
        body, name="merge_and_head", grid=(nt,),
        in_specs=[grp(d) for d in DILATIONS] * 2
        + [gate_tile, tile, tile, tile, const((8, dm)), const((1, dm)), const((ATT_W, dm)), const((dm, dm))],
        out_specs=[const((8, LANES)), tile, tile, gate_tile] + [grp(d) for d in DILATIONS] * 3
        + [const((dm, dm)), const((ATT_W, dm)), const((8, dm))],
        out_shape=[jax.ShapeDtypeStruct((8, LANES), F32), jax.ShapeDtypeStruct((s_len, dm), F32),
                   jax.ShapeDtypeStruct((s_len, dm), BF16), jax.ShapeDtypeStruct((s_len, gw), BF16)]
        + grp_shape(BF16) + grp_shape(F32) + grp_shape(F32)
        + [jax.ShapeDtypeStruct((dm, dm), F32), jax.ShapeDtypeStruct((ATT_W, dm), F32),
           jax.ShapeDtypeStruct((8, dm), F32)],
        scratch_shapes=[pltpu.VMEM((4, ts, LANES), F32)],
        compiler_params=_params(("arbitrary",)),
    )(*o_g, *lse_g, p_gate, y_conv, x, tgt, modv, final_g, w_ao, w_o)


def _qkv_grad_to_tokens(dqkv_g, pos, ropetab, ts):
    s_len = pos.shape[0]

    def body(*refs):
        g_refs, (pos_ref, tab_ref, o_ref, slab) = refs[:9], refs[9:]
        rc, ra, rb = _rope_tables(pos_ref, tab_ref)
        for gi, dil in enumerate(DILATIONS):
            for w in range(3):
                vals = _interleave_load(g_refs[3 * gi + w], slab, dil, ts)
                for s in range(4):
                    t = vals[s]
                    if w < 2:
                        t = t * rc + pltpu.roll(t * ra, 8, 1) + pltpu.roll(t * rb, LANES - 8, 1)
                    col = (w * 3 + gi) * ATT_W + s * LANES
                    o_ref[:, col:col + LANES] = t.astype(BF16)

    return pl.pallas_call(
        body, name="qkv_grad_to_tokens", grid=(s_len // ts,),
        in_specs=[pl.BlockSpec((dil, ts // dil, ATT_W), lambda i: (0, i, 0)) for dil in DILATIONS for _ in range(3)]
        + [pl.BlockSpec((ts, 1), lambda i: (i, 0)), pl.BlockSpec((8, LANES), lambda i: (0, 0))],
        out_specs=pl.BlockSpec((ts, 9 * ATT_W), lambda i: (i, 0)),
        out_shape=jax.ShapeDtypeStruct((s_len, 9 * ATT_W), BF16),
        scratch_shapes=[pltpu.VMEM((4, ts, LANES), F32)],
        compiler_params=_params(("parallel",)),
    )(*[a for grp in dqkv_g for a in grp], pos, ropetab)


def _piece_specs(dm, ct, block_rows, column_tile_first):
    nc, nq, ng = 3 * dm // ct, 9 * ATT_W // ct, (ATT_W + 2 * dm) // ct
    starts = (0, nc, nc + nq)
    counts = (nc, nq, ng)
    specs = []
    for st, cnt in zip(starts, counts):
        def imap(a, b, st=st, cnt=cnt):
            i, j = (b, a) if column_tile_first else (a, b)
            active = (j >= st) & (j < st + cnt)
            return (jnp.where(active, i, 0), jnp.clip(j - st, 0, cnt - 1))
        specs.append(pl.BlockSpec((block_rows, ct), imap))
    return specs, starts, counts


def _pick_piece(j, starts, counts, refs):
    val = refs[0][...]
    for st, cnt, ref in list(zip(starts, counts, refs))[1:]:
        val = jnp.where(j >= st, ref[...], val)
    return val


def _input_grad(dp_conv, dp_qkv, dp_gate, w_in, x, dx_res, modv, norm_g, ts):
    s_len, dm = x.shape
    ct = COL_TILE
    specs, starts, counts = _piece_specs(dm, ct, ts, False)
    nct = starts[2] + counts[2]

    def body(p0, p1, p2, w_ref, x_ref, dxr_ref, mod_ref, g_ref, gx_ref, gs_ref, acc):
        i, j = pl.program_id(0), pl.program_id(1)

        @pl.when((i == 0) & (j == 0))
        def _():
            gs_ref[...] = jnp.zeros_like(gs_ref)

        contrib = _dot_nt(_pick_piece(j, starts, counts, (p0, p1, p2)), w_ref[...])

        @pl.when(j == 0)
        def _():
            acc[...] = contrib

        @pl.when(j > 0)
        def _():
            acc[...] += contrib

        @pl.when(j == nct - 1)
        def _():
            d_h = acc[...]
            xv = x_ref[...]
            r = lax.rsqrt(jnp.mean(xv * xv, axis=-1, keepdims=True) + EPS)
            xr = xv * r
            gs_ref[0:1, :] += jnp.sum(d_h, axis=0, keepdims=True)
            gs_ref[1:2, :] += jnp.sum(d_h * (xr * g_ref[...]), axis=0, keepdims=True)
            d_n = d_h * (1.0 + mod_ref[1:2, :])
            gs_ref[2:3, :] += jnp.sum(d_n * xr, axis=0, keepdims=True)
            dxn = d_n * g_ref[...]
            gx_ref[...] = dxr_ref[...] + r * dxn - xr * (r * jnp.mean(dxn * xr, axis=-1, keepdims=True))

    tile = pl.BlockSpec((ts, dm), lambda i, j: (i, 0))
    return pl.pallas_call(
        body, name="input_grad", grid=(s_len // ts, nct),
        in_specs=specs + [pl.BlockSpec((dm, ct), lambda i, j: (0, j)), tile, tile,
                          pl.BlockSpec((8, dm), lambda i, j: (0, 0)), pl.BlockSpec((1, dm), lambda i, j: (0, 0))],
        out_specs=[tile, pl.BlockSpec((8, dm), lambda i, j: (0, 0))],
        out_shape=[jax.ShapeDtypeStruct((s_len, dm), F32), jax.ShapeDtypeStruct((8, dm), F32)],
        scratch_shapes=[pltpu.VMEM((ts, dm), F32)],
        compiler_params=_params(("arbitrary", "arbitrary")),
    )(dp_conv, dp_qkv, dp_gate, w_in, x, dx_res, modv, norm_g)


def _w_in_grad(h, dp_conv, dp_qkv, dp_gate, ts):
    s_len, dm = h.shape
    ct = COL_TILE
    specs, starts, counts = _piece_specs(dm, ct, ts, True)
    nct = starts[2] + counts[2]

    def body(h_ref, p0, p1, p2, o_ref):
        j, i = pl.program_id(0), pl.program_id(1)
        contrib = _dot_tn(h_ref[...], _pick_piece(j, starts, counts, (p0, p1, p2)))

        @pl.when(i == 0)
        def _():
            o_ref[...] = contrib

        @pl.when(i > 0)
        def _():
            o_ref[...] += contrib

    return pl.pallas_call(
        body, name="w_in_grad", grid=(nct, s_len // ts),
        in_specs=[pl.BlockSpec((ts, dm), lambda j, i: (i, 0))] + specs,
        out_specs=pl.BlockSpec((dm, ct), lambda j, i: (0, j)),
        out_shape=jax.ShapeDtypeStruct((dm, nct * ct), F32),
        compiler_params=_params(("arbitrary", "arbitrary")),
    )(h, dp_conv, dp_qkv, dp_gate)


def _rope_lane_table():
    l64 = jnp.arange(LANES) % 64
    half = ROT_DIM // 2
    inv_freq = ROPE_THETA ** (-(jnp.arange(half, dtype=F32) * 2.0 / ROT_DIM))
    rot = l64 < ROT_DIM
    rows = [jnp.where(rot, inv_freq[l64 % half], 0.0), (l64 < half).astype(F32),
            ((l64 >= half) & rot).astype(F32), rot.astype(F32)]
    return jnp.concatenate([jnp.stack(rows), jnp.zeros((4, LANES), F32)], axis=0)


def _tile_sizes(s_len):
    l_min = s_len // DILATIONS[-1]
    ts_big = min(1024, s_len // 2)
    ts_mid = 256
    ts_head = 256
    qt = min(512, l_min)
    return ts_big, ts_mid, ts_head, qt


def kernel(x, c, positions, norm_g, w_ada, b_ada, w_in, conv_w, conv_b, conv_ln_g, conv_ln_b, w_conv_out, w_att_out, w_o, final_g, loss_target, m_norm_g, m_w_ada, m_b_ada, m_w_in, m_conv_w, m_conv_b, m_conv_ln_g, m_conv_ln_b, m_w_conv_out, m_w_att_out, m_w_o, m_final_g, v_norm_g, v_w_ada, v_b_ada, v_w_in, v_conv_w, v_conv_b, v_conv_ln_g, v_conv_ln_b, v_w_conv_out, v_w_att_out, v_w_o, v_final_g):
    s_len, dm = x.shape[1], x.shape[2]
    ts_big, ts_mid, ts_head, qt = _tile_sizes(s_len)
    xi, yi, cidx = _my_place()
    chip = 2 * xi + yi
    batch = 4 * xi + 2 * yi + cidx
    x2d, tgt = x[0], loss_target[0]
    pos = positions.reshape(s_len, 1)
    wa_l, wi_l, cw_l = w_ada[0], w_in[0], conv_w[0]
    wco_l, wao_l, wo_l = w_conv_out[0], w_att_out[0], w_o[0]
    ada_w = wa_l.shape[1]
    cw_cols = cw_l.shape[1]

    cw_pad = jnp.pad(cw_l, ((0, HALO - CONV_K), (0, 0)))
    small_in = jnp.concatenate([jnp.broadcast_to(c, (8, dm)), cw_pad.reshape(8, dm)], axis=0)
    small = _allgather_small(small_in).reshape(8, 16, dm)
    c_all = small[:, 0, :]
    conv_w_full = jnp.concatenate(
        [small[2 * p, 8:16, :].reshape(HALO, cw_cols) for p in range(4)], axis=1)
    b_l = lax.dynamic_slice(b_ada, (0, chip * ada_w), (1, ada_w))
    mod_parts = _allgather_small(_mod_part(c_all, wa_l, b_l)).reshape(8, 8, ada_w)
    mod_rows = lax.dynamic_index_in_dim(mod_parts, batch, axis=1, keepdims=False)
    mod = jnp.concatenate([mod_rows[2 * p] for p in range(4)], axis=0).reshape(3, dm)
    modv = jnp.concatenate([mod, jnp.zeros((5, dm), F32)], axis=0)

    kinds = ("col", "row", "col", "row")
    w_in_b, w_co_b, w_ao_b, w_o_b = _allgather_weights([wi_l, wco_l, wao_l, wo_l], kinds)

    ropetab = _rope_lane_table()
    p_conv, qkv0, qkv1, qkv2, p_gate, h_b = _input_projection(x2d, pos, modv, norm_g, ropetab, w_in_b, ts_big)
    y_conv, u0, u1 = _conv_forward(p_conv, conv_w_full, conv_b, conv_ln_g, conv_ln_b, w_co_b, ts_mid)
    qkv_flat = [q.reshape(3, s_len, ATT_W) for q in (qkv0, qkv1, qkv2)]
    o_g, lse_g = [], []
    for gi, dil in enumerate(DILATIONS):
        o, l = _attention_forward(qkv_flat[gi], s_len // dil, qt, "attention_forward_%d" % dil)
        o_g.append(o.reshape(dil, s_len // dil, ATT_W))
        lse_g.append(l.reshape(dil, s_len // dil, ATT_W))

    (loss_p, dx_res, dyc, dp_gate, da0, da1, da2, ls0, ls1, ls2, de0, de1, de2,
     g_wo, g_wao, head_sums) = _merge_and_head(o_g, lse_g, p_gate, y_conv, x2d, tgt, modv,
                                               final_g.reshape(1, dm), w_ao_b, w_o_b, ts_head)
    loss = lax.psum(loss_p[0, 0], ("x", "y", "c"))

    dp_conv, g_wco, conv_sums, g_cw = _conv_backward(dyc, p_conv, u0, u1, conv_w_full, conv_ln_g, conv_ln_b,
                                                     w_co_b, ts_mid)
    dqkv_g = []
    for gi, (dil, da, ls, de) in enumerate(zip(DILATIONS, (da0, da1, da2), (ls0, ls1, ls2), (de0, de1, de2))):
        flat = lambda a: a.reshape(s_len, ATT_W)
        dqkv = _attention_backward(qkv_flat[gi], flat(da), flat(ls), flat(de), s_len // dil, qt,
                                   "attention_backward_%d" % dil)
        dqkv_g.append([a.reshape(dil, s_len // dil, ATT_W) for a in dqkv])
    dp_qkv = _qkv_grad_to_tokens(dqkv_g, pos, ropetab, ts_mid)
    grad_x, in_sums = _input_grad(dp_conv, dp_qkv, dp_gate, w_in_b, x2d, dx_res, modv, norm_g, ts_big)
    g_win = _w_in_grad(h_b, dp_conv, dp_qkv, dp_gate, ts_big)

    grads = [g_win, g_wco, g_wao, g_wo]
    full_shapes = [g.shape for g in grads]
    c_arr = jnp.reshape(cidx, (1,)).astype(jnp.int32)
    recv_halves = _reduce_pair_exchange(grads, kinds)
    halves = [_reduce_pair_sum(g, pa, kind, c_arr, "reduce_pair_sum_%d" % k)
              for k, (g, pa, kind) in enumerate(zip(grads, recv_halves, kinds))]
    recvd = _reduce_to_owner(halves, kinds, full_shapes)
    gr_win, gr_wco, gr_wao, gr_wo = _reduce_finish(halves, recvd, kinds, full_shapes)

    rows = [in_sums[2:3], conv_sums[2:3], conv_sums[0:1], conv_sums[1:2], head_sums[0:1],
            in_sums[0:1], in_sums[1:2], head_sums[1:2], g_cw, jnp.zeros((8, dm), F32)]
    part = jnp.concatenate(rows, axis=0)
    gathered = _allgather_small(part)
    tot = _sum_devices(gathered, 48)
    dmod_all = gathered.reshape(8, 48, dm)[:, 5:8, :].reshape(8, 3 * dm)
    dmod_l = lax.dynamic_slice(dmod_all, (0, chip * ada_w), (8, ada_w))
    gr_wada = _grad_w_ada(c_all.T, dmod_l)
    gr_cw = lax.dynamic_slice(tot[8:8 + HALO], (0, chip * cw_cols), (HALO, cw_cols))

    def pack8(ng, cb, lg, lb, fg, ba):
        return jnp.concatenate([ng, cb, lg, lb, fg.reshape(1, dm), ba.reshape(3, dm)], axis=0)

    pad_cw = lambda a: jnp.pad(a[0], ((0, HALO - CONV_K), (0, 0)))
    w8 = pack8(norm_g, conv_b, conv_ln_g, conv_ln_b, final_g, b_ada)
    m8 = pack8(m_norm_g, m_conv_b, m_conv_ln_g, m_conv_ln_b, m_final_g, m_b_ada)
    v8 = pack8(v_norm_g, v_conv_b, v_conv_ln_g, v_conv_ln_b, v_final_g, v_b_ada)
    g8 = tot[0:8]
    upd = {
        "small": _adamw(w8, g8, m8, v8, "adamw_small"),
        "w_ada": _adamw(wa_l, gr_wada, m_w_ada[0], v_w_ada[0], "adamw_w_ada"),
        "w_in": _adamw(wi_l, gr_win, m_w_in[0], v_w_in[0], "adamw_w_in"),
        "conv_w": _adamw(cw_pad, gr_cw, pad_cw(m_conv_w), pad_cw(v_conv_w), "adamw_conv_w"),
        "w_co": _adamw(wco_l, gr_wco, m_w_conv_out[0], v_w_conv_out[0], "adamw_w_conv_out"),
        "w_ao": _adamw(wao_l, gr_wao, m_w_att_out[0], v_w_att_out[0], "adamw_w_att_out"),
        "w_o": _adamw(wo_l, gr_wo, m_w_o[0], v_w_o[0], "adamw_w_o"),
    }

    def family(which):
        if which is None:
            sm = g8
            big = {"w_ada": gr_wada, "w_in": gr_win, "conv_w": gr_cw, "w_co": gr_wco, "w_ao": gr_wao, "w_o": gr_wo}
        else:
            sm = upd["small"][which]
            big = {k: upd[k][which] for k in ("w_ada", "w_in", "conv_w", "w_co", "w_ao", "w_o")}
        return [sm[0:1], big["w_ada"][None], sm[5:8].reshape(1, 3 * dm), big["w_in"][None],
                big["conv_w"][None, :CONV_K], sm[1:2], sm[2:3], sm[3:4], big["w_co"][None],
                big["w_ao"][None], big["w_o"][None], sm[4]]

    return (loss, grad_x[None], *family(None), *family(0), *family(1), *family(2))
```

```python
import functools

import jax
import jax.numpy as jnp
from jax import lax
from jax.experimental import pallas as pl
from jax.experimental.pallas import tpu as pltpu

F32 = jnp.float32
BF16 = jnp.bfloat16
MESH = pl.DeviceIdType.MESH
ANY = pl.BlockSpec(memory_space=pl.ANY)
VM = pl.BlockSpec(memory_space=pltpu.VMEM)

EPS = 1e-6
NEG = -1e30
ATT_W = 512
DILATIONS = (1, 4, 16)
BLK = 128
CONV_K = 31
HALO = 32
CONV_ROWS = 32
ROT_DIM = 16
ROPE_THETA = 500000.0
COL_TILE = 512
LANES = 128
VMEM_LIMIT = 56 * 1024 * 1024

ADAM_LR, ADAM_B1, ADAM_B2, ADAM_EPS, ADAM_WD, ADAM_STEP = 0.001, 0.9, 0.999, 1e-08, 0.01, 10


def _params(sem=None, vmem=VMEM_LIMIT):
    return pltpu.CompilerParams(dimension_semantics=sem, vmem_limit_bytes=vmem)


def _dot(a, b):
    return jnp.dot(a, b, preferred_element_type=F32)


def _dot_nt(a, b):
    return lax.dot_general(a, b, (((1,), (1,)), ((), ())), preferred_element_type=F32)


def _dot_tn(a, b):
    return lax.dot_general(a, b, (((0,), (0,)), ((), ())), preferred_element_type=F32)


def _sig(x):
    return jax.nn.sigmoid(x)


def _dsilu(x, s):
    return s * (1.0 + x * (1.0 - s))


def _my_place():
    return lax.axis_index("x"), lax.axis_index("y"), lax.axis_index("c")


def _allgather_small(x_shard):
    m_per, n = x_shard.shape

    def body(x_ref, out_ref, send_sems, recv_sems, local_sem):
        x, y, c = _my_place()
        me, sibling = (x, y, c), (x, y, 1 - c)
        chips = [(1 - x, y), (x, 1 - y), (1 - x, 1 - y)]

        def rows(px, py, pc):
            return out_ref.at[pl.ds((4 * px + 2 * py + pc) * m_per, m_per), :]

        def copy(k, block, to, src=None):
            return pltpu.make_async_remote_copy(
                src_ref=rows(*block) if src is None else src, dst_ref=rows(*block),
                send_sem=send_sems.at[k], recv_sem=recv_sems.at[k],
                device_id=to, device_id_type=MESH)

        mine = pltpu.make_async_copy(x_ref, rows(*me), local_sem)
        mine.start()
        first = [copy(0, me, sibling, src=x_ref)]
        first += [copy(1 + j, me, (*chip, c), src=x_ref) for j, chip in enumerate(chips)]
        for cp in first:
            cp.start()
        passed = [copy(4 + j, (*chip, c), sibling) for j, chip in enumerate(chips)]
        for j, chip in enumerate(chips):
            copy(1 + j, (*chip, c), me).wait_recv()
            passed[j].start()
        copy(0, sibling, me).wait_recv()
        for j, chip in enumerate(chips):
            copy(4 + j, (*chip, 1 - c), me).wait_recv()
        for cp in first + passed:
            cp.wait_send()
        mine.wait()

    return pl.pallas_call(
        body, name="allgather_small",
        out_shape=jax.ShapeDtypeStruct((8 * m_per, n), x_shard.dtype),
        in_specs=[VM], out_specs=VM,
        scratch_shapes=[pltpu.SemaphoreType.DMA((7,)), pltpu.SemaphoreType.DMA((7,)),
                        pltpu.SemaphoreType.DMA],
    )(x_shard)


def _shard_window(ref, kind, p, n_shards=4):
    r, c = ref.shape
    if kind == "col":
        w = c // n_shards
        return ref.at[:, pl.ds(p * w, w)]
    w = r // n_shards
    return ref.at[pl.ds(p * w, w), :]


def _half_window(ref, kind, hc):
    r, c = ref.shape
    if kind == "col":
        return ref.at[pl.ds(hc * (r // 2), r // 2), :]
    return ref.at[:, pl.ds(hc * (c // 2), c // 2)]


def _allgather_weights(shards, kinds):
    n = len(shards)
    full_shapes = []
    for s, kind in zip(shards, kinds):
        r, c = s.shape
        full_shapes.append((r, 4 * c) if kind == "col" else (4 * r, c))

    def body(*refs):
        w_refs, out_refs = refs[:n], refs[n:2 * n]
        bf_refs = refs[2 * n:3 * n]
        send_sems, recv_sems, local_sems = refs[3 * n:]
        x, y, c = _my_place()
        p = 2 * x + y
        sibling = (x, y, 1 - c)
        chips = [(1 - x, y), (x, 1 - y), (1 - x, 1 - y)]
        locals_ = []
        for k in range(n):
            bf_refs[k][...] = w_refs[k][...].astype(BF16)
            cp = pltpu.make_async_copy(bf_refs[k], _shard_window(out_refs[k], kinds[k], p), local_sems.at[k])
            cp.start()
            locals_.append(cp)

        def landed(k, chip, hc):
            return _half_window(_shard_window(out_refs[k], kinds[k], 2 * chip[0] + chip[1]), kinds[k], hc)

        def ici(j, k, chip_src, to, src=None):
            return pltpu.make_async_remote_copy(
                src_ref=landed(k, chip_src, c) if src is None else src, dst_ref=landed(k, chip_src, c),
                send_sem=send_sems.at[j * n + k], recv_sem=recv_sems.at[j * n + k],
                device_id=to, device_id_type=MESH)

        first = []
        for j, chip in enumerate(chips):
            for k in range(n):
                cp = ici(j, k, (x, y), (*chip, c), src=_half_window(bf_refs[k], kinds[k], c))
                cp.start()
                first.append(cp)
        passed = []
        for j, chip in enumerate(chips):
            for k in range(n):
                ici(j, k, chip, (x, y, c)).wait_recv()
                cp = pltpu.make_async_remote_copy(
                    src_ref=landed(k, chip, c), dst_ref=landed(k, chip, c),
                    send_sem=send_sems.at[(3 + j) * n + k], recv_sem=recv_sems.at[(3 + j) * n + k],
                    device_id=sibling, device_id_type=MESH)
                cp.start()
                passed.append(cp)
        for j, chip in enumerate(chips):
            for k in range(n):
                pltpu.make_async_remote_copy(
                    src_ref=landed(k, chip, 1 - c), dst_ref=landed(k, chip, 1 - c),
                    send_sem=send_sems.at[(3 + j) * n + k], recv_sem=recv_sems.at[(3 + j) * n + k],
                    device_id=sibling, device_id_type=MESH).wait_recv()
        for cp in first + passed:
            cp.wait_send()
        for cp in locals_:
            cp.wait()

    return pl.pallas_call(
        body, name="allgather_weights",
        out_shape=[jax.ShapeDtypeStruct(s, BF16) for s in full_shapes],
        in_specs=[VM] * n, out_specs=[ANY] * n,
        scratch_shapes=[pltpu.VMEM(s.shape, BF16) for s in shards]
        + [pltpu.SemaphoreType.DMA((6 * n,)), pltpu.SemaphoreType.DMA((6 * n,)), pltpu.SemaphoreType.DMA((n,))],
        compiler_params=_params(),
    )(*shards)


def _reduce_pair_exchange(grads, kinds):
    n = len(grads)
    half_shapes = [(g.shape[0] // 2, g.shape[1]) if kind == "col" else (g.shape[0], g.shape[1] // 2)
                   for g, kind in zip(grads, kinds)]

    def body(*refs):
        g_refs, pa_refs = refs[:n], refs[n:2 * n]
        send_sems, recv_sems = refs[2 * n:]
        x, y, c = _my_place()
        cps = []
        for k in range(n):
            cp = pltpu.make_async_remote_copy(
                src_ref=_half_window(g_refs[k], kinds[k], 1 - c), dst_ref=pa_refs[k],
                send_sem=send_sems.at[k], recv_sem=recv_sems.at[k],
                device_id=(x, y, 1 - c), device_id_type=MESH)
            cp.start()
            cps.append(cp)
        for cp in cps:
            cp.wait()

    return pl.pallas_call(
        body, name="reduce_pair_exchange",
        out_shape=[jax.ShapeDtypeStruct(s, F32) for s in half_shapes],
        in_specs=[ANY] * n, out_specs=[ANY] * n,
        scratch_shapes=[pltpu.SemaphoreType.DMA((n,)), pltpu.SemaphoreType.DMA((n,))],
    )(*grads)


def _row_tile(rows, cols, itemsize=4, target=2 * 1024 * 1024, mult=16):
    t = rows
    while t % 2 == 0 and t // 2 >= mult and (t // 2) % mult == 0 and t * cols * itemsize > target:
        t //= 2
    return t


def _reduce_pair_sum(g, pa, kind, c_arr, name):
    hr, hc_ = pa.shape
    tr = _row_tile(hr, hc_)
    nb = hr // tr

    def body(c_ref, g_ref, pa_ref, o_ref):
        o_ref[...] = (g_ref[...] + pa_ref[...]).astype(BF16)

    if kind == "col":
        g_map = lambda i, c_ref: (c_ref[0] * nb + i, 0)
    else:
        g_map = lambda i, c_ref: (i, c_ref[0])
    return pl.pallas_call(
        body, name=name,
        grid_spec=pltpu.PrefetchScalarGridSpec(
            num_scalar_prefetch=1, grid=(nb,),
            in_specs=[pl.BlockSpec((tr, hc_), g_map), pl.BlockSpec((tr, hc_), lambda i, c_ref: (i, 0))],
            out_specs=pl.BlockSpec((tr, hc_), lambda i, c_ref: (i, 0))),
        out_shape=jax.ShapeDtypeStruct((hr, hc_), BF16),
        compiler_params=_params(("parallel",)),
    )(c_arr, g, pa)


def _half_shard_shape(full_shape, kind):
    r, c = full_shape
    return (r // 2, c // 4) if kind == "col" else (r // 4, c // 2)


def _reduce_to_owner(halves, kinds, full_shapes):
    n = len(halves)
    hs = [_half_shard_shape(fs, kind) for fs, kind in zip(full_shapes, kinds)]

    def body(*refs):
        h_refs, rc_refs = refs[:n], refs[n:2 * n]
        send_sems, recv_sems = refs[2 * n:]
        x, y, c = _my_place()
        chips = [(1 - x, y), (x, 1 - y), (1 - x, 1 - y)]
        cps = []
        for j, chip in enumerate(chips):
            pj = 2 * chip[0] + chip[1]
            for k in range(n):
                cp = pltpu.make_async_remote_copy(
                    src_ref=_shard_window(h_refs[k], kinds[k], pj), dst_ref=rc_refs[k].at[j],
                    send_sem=send_sems.at[j * n + k], recv_sem=recv_sems.at[j * n + k],
                    device_id=(*chip, c), device_id_type=MESH)
                cp.start()
                cps.append(cp)
        for cp in cps:
            cp.wait()

    return pl.pallas_call(
        body, name="reduce_to_owner",
        out_shape=[jax.ShapeDtypeStruct((3,) + s, BF16) for s in hs],
        in_specs=[ANY] * n, out_specs=[ANY] * n,
        scratch_shapes=[pltpu.SemaphoreType.DMA((3 * n,)), pltpu.SemaphoreType.DMA((3 * n,))],
    )(*halves)


def _reduce_finish(halves, recvd, kinds, full_shapes):
    n = len(halves)
    hs = [_half_shard_shape(fs, kind) for fs, kind in zip(full_shapes, kinds)]
    shard_shapes = [(fs[0], fs[1] // 4) if kind == "col" else (fs[0] // 4, fs[1])
                    for fs, kind in zip(full_shapes, kinds)]

    def body(*refs):
        h_refs, rc_refs, gs_refs = refs[:n], refs[n:2 * n], refs[2 * n:3 * n]
        own_refs, gh_refs = refs[3 * n:4 * n], refs[4 * n:5 * n]
        in_sems, loc_sems, send_sems, recv_sems = refs[5 * n:]
        x, y, c = _my_place()
        p = 2 * x + y
        loads = []
        for k in range(n):
            cp = pltpu.make_async_copy(_shard_window(h_refs[k], kinds[k], p), own_refs[k], in_sems.at[k])
            cp.start()
            loads.append(cp)
        outs = []
        for k in range(n):
            loads[k].wait()
            gh_refs[k][...] = (own_refs[k][...].astype(F32) + rc_refs[k][0].astype(F32)
                               + rc_refs[k][1].astype(F32) + rc_refs[k][2].astype(F32))
            dst = _half_window(gs_refs[k], kinds[k], c)
            lc = pltpu.make_async_copy(gh_refs[k], dst, loc_sems.at[k])
            lc.start()
            rc = pltpu.make_async_remote_copy(
                src_ref=gh_refs[k], dst_ref=dst, send_sem=send_sems.at[k], recv_sem=recv_sems.at[k],
                device_id=(x, y, 1 - c), device_id_type=MESH)
            rc.start()
            outs.append((lc, rc))
        for k, (lc, rc) in enumerate(outs):
            lc.wait()
            rc.wait_send()
            pltpu.make_async_remote_copy(
                src_ref=gh_refs[k], dst_ref=_half_window(gs_refs[k], kinds[k], 1 - c),
                send_sem=send_sems.at[k], recv_sem=recv_sems.at[k],
                device_id=(x, y, 1 - c), device_id_type=MESH).wait_recv()

    return pl.pallas_call(
        body, name="reduce_finish",
        out_shape=[jax.ShapeDtypeStruct(s, F32) for s in shard_shapes],
        in_specs=[ANY] * n + [VM] * n, out_specs=[ANY] * n,
        scratch_shapes=[pltpu.VMEM(s, BF16) for s in hs] + [pltpu.VMEM(s, F32) for s in hs]
        + [pltpu.SemaphoreType.DMA((n,)) for _ in range(4)],
        compiler_params=_params(),
    )(*halves, *recvd)


def _mod_part(c_all, w_ada_l, b_l):
    def body(c_ref, w_ref, b_ref, o_ref):
        o_ref[...] = _dot(c_ref[...].astype(BF16), w_ref[...].astype(BF16)) + b_ref[...]

    return pl.pallas_call(
        body, name="mod_part", out_shape=jax.ShapeDtypeStruct((8, w_ada_l.shape[1]), F32),
        in_specs=[VM, VM, VM], out_specs=VM, compiler_params=_params(),
    )(c_all, w_ada_l, b_l)


def _sum_devices(parts, m):
    def body(p_ref, o_ref):
        acc = p_ref[0:m, :]
        for d in range(1, 8):
            acc = acc + p_ref[d * m:(d + 1) * m, :]
        o_ref[...] = acc

    return pl.pallas_call(
        body, name="sum_devices", out_shape=jax.ShapeDtypeStruct((m, parts.shape[1]), F32),
        in_specs=[VM], out_specs=VM, compiler_params=_params(),
    )(parts)


def _grad_w_ada(c_all_t, dmod_l):
    d, w = c_all_t.shape[0], dmod_l.shape[1]

    def body(ct_ref, dm_ref, o_ref):
        acc = ct_ref[:, 0:1] * dm_ref[0:1, :]
        for b in range(1, 8):
            acc = acc + ct_ref[:, b:b + 1] * dm_ref[b:b + 1, :]
        o_ref[...] = acc

    return pl.pallas_call(
        body, name="grad_w_ada", out_shape=jax.ShapeDtypeStruct((d, w), F32),
        in_specs=[VM, VM], out_specs=VM, compiler_params=_params(),
    )(c_all_t, dmod_l)


def _adamw(w, g, m, v, name):
    r, c = w.shape
    tr = _row_tile(r, c, target=1024 * 1024, mult=8)
    c1 = 1.0 - ADAM_B1 ** ADAM_STEP
    c2 = 1.0 - ADAM_B2 ** ADAM_STEP

    def body(w_ref, g_ref, m_ref, v_ref, d_ref, nm_ref, nv_ref):
        gv = g_ref[...]
        nm = ADAM_B1 * m_ref[...] + (1.0 - ADAM_B1) * gv
        nv = ADAM_B2 * v_ref[...] + (1.0 - ADAM_B2) * (gv * gv)
        m_hat = nm / c1
        v_hat = nv / c2
        d_ref[...] = -ADAM_LR * (m_hat / (jnp.sqrt(v_hat) + ADAM_EPS) + ADAM_WD * w_ref[...])
        nm_ref[...] = nm
        nv_ref[...] = nv

    spec = pl.BlockSpec((tr, c), lambda i: (i, 0))
    return pl.pallas_call(
        body, name=name, grid=(r // tr,),
        out_shape=[jax.ShapeDtypeStruct((r, c), F32)] * 3,
        in_specs=[spec] * 4, out_specs=[spec] * 3,
        compiler_params=_params(("parallel",)),
    )(w, g, m, v)


def _rope_tables(pos_ref, tab_ref):
    ang = pos_ref[...].astype(F32) * tab_ref[0:1, :]
    cs, sn = jnp.cos(ang), jnp.sin(ang)
    return jnp.where(tab_ref[3:4, :] > 0, cs, 1.0), -sn * tab_ref[1:2, :], sn * tab_ref[2:3, :]


def _deinterleave_store(vals, slab, out_ref, d, ts, dtype):
    if d == 1:
        for s in range(4):
            out_ref[0, :, s * LANES:(s + 1) * LANES] = vals[s].astype(dtype)
        return
    for s in range(4):
        slab[s] = vals[s]
    for r in range(d):
        for s in range(4):
            out_ref[r, :, s * LANES:(s + 1) * LANES] = slab[s, pl.ds(r, ts // d, stride=d), :].astype(dtype)


def _interleave_load(blk_ref, slab, d, ts):
    if d == 1:
        return [blk_ref[0, :, s * LANES:(s + 1) * LANES] for s in range(4)]
    for r in range(d):
        for s in range(4):
            slab[s, pl.ds(r, ts // d, stride=d), :] = blk_ref[r, :, s * LANES:(s + 1) * LANES]
    return [slab[s] for s in range(4)]


def _input_projection(x, pos, modv, norm_g, ropetab, w_in, ts):
    s_len, d_model = x.shape
    ci = w_in.shape[1]
    ct = COL_TILE
    nc, ng = 3 * d_model // ct, (ATT_W + 2 * d_model) // ct
    nct = nc + 9 + ng
    assert ci == nct * ct and s_len % ts == 0

    def body(x_ref, pos_ref, mod_ref, g_ref, tab_ref, w_ref,
             pc_ref, q0_ref, q1_ref, q2_ref, pg_ref, h_ref, h_scr, slab, rc, ra, rb):
        j = pl.program_id(1)

        @pl.when(j == 0)
        def _():
            xv = x_ref[...]
            r = lax.rsqrt(jnp.mean(xv * xv, axis=-1, keepdims=True) + EPS)
            h = (xv * r) * g_ref[...] * (1.0 + mod_ref[1:2, :]) + mod_ref[0:1, :]
            hb = h.astype(BF16)
            h_scr[...] = hb
            h_ref[...] = hb
            c_, a_, b_ = _rope_tables(pos_ref, tab_ref)
            rc[...] = c_
            ra[...] = a_
            rb[...] = b_

        res = _dot(h_scr[...], w_ref[...])

        @pl.when(j < nc)
        def _():
            pc_ref[...] = res

        for gi, (dil, q_ref) in enumerate(zip(DILATIONS, (q0_ref, q1_ref, q2_ref))):
            lo = nc + 3 * gi

            @pl.when((j >= lo) & (j < lo + 3))
            def _(dil=dil, q_ref=q_ref, lo=lo):
                roped = j < lo + 2
                vals = []
                for s in range(4):
                    t = res[:, s * LANES:(s + 1) * LANES]
                    rot = (t * rc[...] + pltpu.roll(t, LANES - 8, 1) * ra[...]
                           + pltpu.roll(t, 8, 1) * rb[...])
                    vals.append(jnp.where(roped, rot, t))
                _deinterleave_store(vals, slab, q_ref, dil, ts, BF16)

        @pl.when(j >= nc + 9)
        def _():
            pg_ref[...] = res

    def w_map(i, j):
        t = j - nc
        qcol = nc + (t % 3) * 3 + t // 3
        return (0, jnp.where((j >= nc) & (j < nc + 9), qcol, j))

    def q_spec(gi, dil):
        return pl.BlockSpec((None, dil, ts // dil, ATT_W),
                            lambda i, j: (jnp.clip(j - nc - 3 * gi, 0, 2), 0, i, 0))

    out_shape = [jax.ShapeDtypeStruct((s_len, 3 * d_model), F32)]
    out_shape += [jax.ShapeDtypeStruct((3, dil, s_len // dil, ATT_W), BF16) for dil in DILATIONS]
    out_shape += [jax.ShapeDtypeStruct((s_len, ATT_W + 2 * d_model), F32),
                  jax.ShapeDtypeStruct((s_len, d_model), BF16)]
    return pl.pallas_call(
        body, name="input_projection", grid=(s_len // ts, nct),
        in_specs=[pl.BlockSpec((ts, d_model), lambda i, j: (i, 0)),
                  pl.BlockSpec((ts, 1), lambda i, j: (i, 0)),
                  pl.BlockSpec((8, d_model), lambda i, j: (0, 0)),
                  pl.BlockSpec((1, d_model), lambda i, j: (0, 0)),
                  pl.BlockSpec((8, LANES), lambda i, j: (0, 0)),
                  pl.BlockSpec((d_model, ct), w_map)],
        out_specs=[pl.BlockSpec((ts, ct), lambda i, j: (i, jnp.clip(j, 0, nc - 1)))]
        + [q_spec(gi, dil) for gi, dil in enumerate(DILATIONS)]
        + [pl.BlockSpec((ts, ct), lambda i, j: (i, jnp.clip(j - nc - 9, 0, ng - 1))),
           pl.BlockSpec((ts, d_model), lambda i, j: (i, 0))],
        out_shape=out_shape,
        scratch_shapes=[pltpu.VMEM((ts, d_model), BF16), pltpu.VMEM((4, ts, LANES), F32),
                        pltpu.VMEM((ts, LANES), F32), pltpu.VMEM((ts, LANES), F32),
                        pltpu.VMEM((ts, LANES), F32)],
        compiler_params=_params(("arbitrary", "arbitrary")),
    )(x, pos, modv, norm_g, ropetab, w_in)


def _layernorm_stats(u1):
    mu = jnp.mean(u1, axis=-1, keepdims=True)
    xc = u1 - mu
    rstd = lax.rsqrt(jnp.mean(xc * xc, axis=-1, keepdims=True) + EPS)
    return xc * rstd, rstd


def _shifted_copies(win, shf, ts):
    rows = ts + HALO - 8
    for b in range(1, 8):
        shf[b - 1, 0:rows, :] = win[pl.ds(b, rows), :]


def _tap(win, shf, off, r0):
    a, b = divmod(off, 8)
    start = 8 * a + r0
    if b == 0:
        return win[start:start + CONV_ROWS, :]
    return shf[b - 1, start:start + CONV_ROWS, :]


def _conv_forward(p_conv, conv_w, conv_b, ln_g, ln_b, w_co, ts):
    s_len, d3 = p_conv.shape
    dm = d3 // 3

    def body(p_ref, cw_ref, cb_ref, g_ref, b_ref, w_ref, y_ref, u0_ref, u1_ref, win, shf):
        i = pl.program_id(0)

        @pl.when(i == 0)
        def _():
            win[0:HALO, :] = jnp.zeros((HALO, dm), F32)

        a, b, z = p_ref[:, 0:dm], p_ref[:, dm:2 * dm], p_ref[:, 2 * dm:3 * dm]
        u0 = a * _sig(b)
        win[HALO:HALO + ts, :] = u0
        u0_ref[...] = u0
        _shifted_copies(win, shf, ts)
        for r0 in range(0, ts, CONV_ROWS):
            acc = jnp.broadcast_to(cb_ref[...], (CONV_ROWS, dm))
            for k in range(CONV_K):
                acc = acc + cw_ref[k:k + 1, :] * _tap(win, shf, HALO - (CONV_K - 1) + k, r0)
            u1_ref[r0:r0 + CONV_ROWS, :] = acc
        xh, _ = _layernorm_stats(u1_ref[...])
        u2 = xh * g_ref[...] + b_ref[...]
        a_conv = (u2 * _sig(u2)) * (z * _sig(z))
        y_ref[...] = _dot(a_conv.astype(BF16), w_ref[...])
        win[0:HALO, :] = win[ts:ts + HALO, :]

    row = pl.BlockSpec((1, dm), lambda i: (0, 0))
    tile = pl.BlockSpec((ts, dm), lambda i: (i, 0))
    return pl.pallas_call(
        body, name="conv_forward", grid=(s_len // ts,),
        in_specs=[pl.BlockSpec((ts, d3), lambda i: (i, 0)),
                  pl.BlockSpec((HALO, dm), lambda i: (0, 0)), row, row, row,
                  pl.BlockSpec((dm, dm), lambda i: (0, 0))],
        out_specs=[tile, tile, tile],
        out_shape=[jax.ShapeDtypeStruct((s_len, dm), F32)] * 3,
        scratch_shapes=[pltpu.VMEM((ts + HALO, dm), F32), pltpu.VMEM((7, ts + HALO - 8, dm), F32)],
        compiler_params=_params(("arbitrary",)),
    )(p_conv, conv_w, conv_b, ln_g, ln_b, w_co)


def _conv_backward(dyc, p_conv, u0, u1, conv_w, ln_g, ln_b, w_co, ts):
    s_len, d3 = p_conv.shape
    dm = d3 // 3
    nt = s_len // ts
    hb = ts // HALO

    def body(dy_ref, p_ref, u0_ref, uh_ref, u1_ref, cw_ref, g_ref, b_ref, w_ref,
             dp_ref, gw_ref, gs_ref, gcw_ref, dwin, uwin, shf):
        i = pl.program_id(0)
        ti = nt - 1 - i

        @pl.when(i == 0)
        def _():
            gw_ref[...] = jnp.zeros_like(gw_ref)
            gs_ref[...] = jnp.zeros_like(gs_ref)
            gcw_ref[...] = jnp.zeros_like(gcw_ref)
            dwin[ts:ts + HALO, :] = jnp.zeros((HALO, dm), F32)

        dy = dy_ref[...]
        z = p_ref[:, 2 * dm:3 * dm]
        d_ac = _dot_nt(dy, w_ref[...])
        xh, rstd = _layernorm_stats(u1_ref[...])
        u2 = xh * g_ref[...] + b_ref[...]
        sg2, sgz = _sig(u2), _sig(z)
        u3, sz = u2 * sg2, z * sgz
        gw_ref[...] += _dot_tn((u3 * sz).astype(BF16), dy)
        d_z = d_ac * u3 * _dsilu(z, sgz)
        d_u2 = d_ac * sz * _dsilu(u2, sg2)
        gs_ref[0:1, :] += jnp.sum(d_u2 * xh, axis=0, keepdims=True)
        gs_ref[1:2, :] += jnp.sum(d_u2, axis=0, keepdims=True)
        dxh = d_u2 * g_ref[...]
        d_u1 = rstd * (dxh - jnp.mean(dxh, axis=-1, keepdims=True)
                       - xh * jnp.mean(dxh * xh, axis=-1, keepdims=True))
        gs_ref[2:3, :] += jnp.sum(d_u1, axis=0, keepdims=True)
        dwin[0:ts, :] = d_u1
        uwin[0:HALO, :] = jnp.where(ti == 0, 0.0, uh_ref[...])
        uwin[HALO:HALO + ts, :] = u0_ref[...]
        dp_ref[:, 2 * dm:3 * dm] = d_z.astype(BF16)
        _shifted_copies(uwin, shf, ts)
        for k in range(CONV_K):
            part = jnp.zeros((CONV_ROWS, dm), F32)
            for r0 in range(0, ts, CONV_ROWS):
                part = part + dwin[r0:r0 + CONV_ROWS, :] * _tap(uwin, shf, HALO - (CONV_K - 1) + k, r0)
            gcw_ref[k:k + 1, :] += jnp.sum(part, axis=0, keepdims=True)
        _shifted_copies(dwin, shf, ts)
        for r0 in range(0, ts, CONV_ROWS):
            d_u0 = jnp.zeros((CONV_ROWS, dm), F32)
            for k in range(CONV_K):
                d_u0 = d_u0 + cw_ref[k:k + 1, :] * _tap(dwin, shf, CONV_K - 1 - k, r0)
            rows = slice(r0, r0 + CONV_ROWS)
            sgb = _sig(p_ref[rows, dm:2 * dm])
            dp_ref[rows, 0:dm] = (d_u0 * sgb).astype(BF16)
            dp_ref[rows, dm:2 * dm] = (d_u0 * p_ref[rows, 0:dm] * sgb * (1.0 - sgb)).astype(BF16)
        dwin[ts:ts + HALO, :] = dwin[0:HALO, :]

    rev = lambda i: (nt - 1 - i, 0)
    row = pl.BlockSpec((1, dm), lambda i: (0, 0))
    tile = pl.BlockSpec((ts, dm), rev)
    return pl.pallas_call(
        body, name="conv_backward", grid=(nt,),
        in_specs=[tile, pl.BlockSpec((ts, d3), rev), tile,
                  pl.BlockSpec((HALO, dm), lambda i: (jnp.maximum((nt - 1 - i) * hb - 1, 0), 0)),
                  tile, pl.BlockSpec((HALO, dm), lambda i: (0, 0)), row, row,
                  pl.BlockSpec((dm, dm), lambda i: (0, 0))],
        out_specs=[pl.BlockSpec((ts, d3), rev), pl.BlockSpec((dm, dm), lambda i: (0, 0)),
                   pl.BlockSpec((8, dm), lambda i: (0, 0)), pl.BlockSpec((HALO, dm), lambda i: (0, 0))],
        out_shape=[jax.ShapeDtypeStruct((s_len, d3), BF16), jax.ShapeDtypeStruct((dm, dm), F32),
                   jax.ShapeDtypeStruct((8, dm), F32), jax.ShapeDtypeStruct((HALO, dm), F32)],
        scratch_shapes=[pltpu.VMEM((ts + HALO, dm), F32), pltpu.VMEM((ts + HALO, dm), F32),
                        pltpu.VMEM((7, ts + HALO - 8, dm), F32)],
        compiler_params=_params(("arbitrary",)),
    )(dyc, p_conv, u0, u0, u1, conv_w, ln_g, ln_b, w_co)


def _att_masks():
    head0 = lax.broadcasted_iota(jnp.int32, (BLK, LANES), 1) < 64
    col = lax.broadcasted_iota(jnp.int32, (BLK, 4 * BLK), 1)
    row = lax.broadcasted_iota(jnp.int32, (BLK, 4 * BLK), 0)
    kj = col % BLK
    prev = jnp.where(col < 2 * BLK, 1, 0)
    band = jnp.where(col < 2 * BLK, kj - row, row - kj)
    return head0, band, prev


def _fill_block_diagonal(dst, slab, src_ref, halo_ref, head0, nq):
    sl = slice(slab * LANES, (slab + 1) * LANES)
    for b in range(nq + 1):
        blk = halo_ref[:, sl] if b == 0 else src_ref[(b - 1) * BLK:b * BLK, sl]
        base = (slab * (nq + 1) + b) * 2 * BLK
        zero = jnp.zeros_like(blk)
        dst[base:base + BLK, :] = jnp.where(head0, blk, zero)
        dst[base + BLK:base + 2 * BLK, :] = jnp.where(head0, zero, blk)


def _attention_forward(qkv, seq_len, qt, name):
    s_len = qkv.shape[1]
    nq = qt // BLK
    tiles_per_seq = seq_len // qt

    def body(q_ref, k_ref, v_ref, kh_ref, vh_ref, o_ref, lse_ref, kbd, vbd):
        first = jnp.where((pl.program_id(0) % tiles_per_seq) == 0, 4 * BLK, 0)
        head0, band, prev = _att_masks()
        band_first = band - prev * first
        for p in range(4):
            _fill_block_diagonal(kbd, p, k_ref, kh_ref, head0, nq)
            _fill_block_diagonal(vbd, p, v_ref, vh_ref, head0, nq)
        units = [(p, n) for p in range(4) for n in range(nq)]

        def keys_of(p, n):
            base = (p * (nq + 1) + n) * 2 * BLK
            return slice(base, base + 4 * BLK)

        def scores(p, n):
            q2 = q_ref[n * BLK:(n + 1) * BLK, p * LANES:(p + 1) * LANES] * 0.125
            return _dot_nt(q2, kbd[keys_of(p, n), :])

        def finish(p, n, o, den, lse):
            rows, sl = slice(n * BLK, (n + 1) * BLK), slice(p * LANES, (p + 1) * LANES)
            o_ref[rows, sl] = o / jnp.where(head0, den[0], den[1])
            lse_ref[rows, sl] = jnp.where(head0, lse[0], lse[1])

        s_next = scores(*units[0])
        pending = None
        for ui, (p, n) in enumerate(units):
            s = s_next
            if ui + 1 < len(units):
                s_next = scores(*units[ui + 1])
            s = jnp.where((band_first if n == 0 else band) >= 0, s, NEG)
            grp = [s[:, g * BLK:(g + 1) * BLK] for g in range(4)]
            ps, den, lse = [None] * 4, [], []
            for h in range(2):
                m = jnp.max(jnp.maximum(grp[h], grp[2 + h]), axis=-1, keepdims=True)
                ps[h], ps[2 + h] = jnp.exp(grp[h] - m), jnp.exp(grp[2 + h] - m)
                dn = jnp.sum(ps[h] + ps[2 + h], axis=-1, keepdims=True)
                den.append(dn)
                lse.append(m + jnp.log(dn))
            pmat = jnp.concatenate([x.astype(BF16) for x in ps], axis=1)
            o = _dot(pmat, vbd[keys_of(p, n), :])
            if pending is not None:
                finish(*pending)
            pending = (p, n, o, den, lse)
        finish(*pending)

    def which(w):
        return pl.BlockSpec((None, qt, ATT_W), lambda i: (w, i, 0))

    def halo(w):
        return pl.BlockSpec((None, BLK, ATT_W), lambda i: (w, jnp.maximum(i * nq - 1, 0), 0))

    out = pl.BlockSpec((qt, ATT_W), lambda i: (i, 0))
    bd = pltpu.VMEM((4 * (nq + 1) * 2 * BLK, LANES), BF16)
    return pl.pallas_call(
        body, name=name, grid=(s_len // qt,),
        in_specs=[which(0), which(1), which(2), halo(1), halo(2)],
        out_specs=[out, out], out_shape=[jax.ShapeDtypeStruct((s_len, ATT_W), F32)] * 2,
        scratch_shapes=[bd, bd],
        compiler_params=_params(("parallel",)),
    )(qkv, qkv, qkv, qkv, qkv)


def _attention_backward(qkv, d_att, lse, delta, seq_len, qt, name):
    s_len = qkv.shape[1]
    nq = qt // BLK
    tiles_per_seq = seq_len // qt
    nblk = s_len // BLK

    def body(q_ref, k_ref, v_ref, kh_ref, vh_ref, do_ref, lse_ref, dl_ref,
             qn_ref, don_ref, lsen_ref, dln_ref, dq_ref, dk_ref, dv_ref, kbd, vbd):
        i = pl.program_id(0)
        first = jnp.where((i % tiles_per_seq) == 0, 4 * BLK, 0)
        last = jnp.where((i % tiles_per_seq) == tiles_per_seq - 1, 4 * BLK, 0)
        head0, band, prev = _att_masks()
        band_first = band - prev * first
        band_tail = band[:, 0:2 * BLK] - last
        for p in range(4):
            _fill_block_diagonal(kbd, p, k_ref, kh_ref, head0, nq)
            _fill_block_diagonal(vbd, p, v_ref, vh_ref, head0, nq)
        units = [(p, n) for p in range(4) for n in range(nq + 1)]

        def stage_a(p, n):
            sl = slice(p * LANES, (p + 1) * LANES)
            base = (p * (nq + 1) + n) * 2 * BLK
            if n < nq:
                rows = slice(n * BLK, (n + 1) * BLK)
                q2, do2, lse2, dl2 = q_ref[rows, sl], do_ref[rows, sl], lse_ref[rows, sl], dl_ref[rows, sl]
                keys = slice(base, base + 4 * BLK)
            else:
                q2, do2, lse2, dl2 = qn_ref[:, sl], don_ref[:, sl], lsen_ref[:, sl], dln_ref[:, sl]
                keys = slice(base, base + 2 * BLK)
            s = _dot_nt(q2 * 0.125, kbd[keys, :])
            dp = _dot_nt(do2, vbd[keys, :])
            return q2, do2, lse2, dl2, keys, s, dp

        def stage_b(n, lse2, dl2, s, dp):
            mask = band_tail if n == nq else (band_first if n == 0 else band)
            ps, dss = [], []
            for g in range(s.shape[1] // BLK):
                h = g % 2
                cols = slice(g * BLK, (g + 1) * BLK)
                pg = jnp.exp(jnp.where(mask[:, cols] >= 0, s[:, cols] - lse2[:, h * 64:h * 64 + 1], NEG))
                ps.append(pg.astype(BF16))
                dss.append((pg * (dp[:, cols] - dl2[:, h * 64:h * 64 + 1]) * 0.125).astype(BF16))
            return jnp.concatenate(ps, axis=1), jnp.concatenate(dss, axis=1)

        def heads(r, g):
            return jnp.where(head0, r[g * BLK:(g + 1) * BLK, :], r[(g + 1) * BLK:(g + 2) * BLK, :])

        a_next = stage_a(*units[0])
        carry = None
        for ui, (p, n) in enumerate(units):
            q2, do2, lse2, dl2, keys, s, dp = a_next
            if ui + 1 < len(units):
                a_next = stage_a(*units[ui + 1])
            pmat, dsmat = stage_b(n, lse2, dl2, s, dp)
            sl = slice(p * LANES, (p + 1) * LANES)
            if n < nq:
                dq_ref[n * BLK:(n + 1) * BLK, sl] = _dot(dsmat, kbd[keys, :])
            dkbd = _dot_tn(dsmat, q2)
            dvbd = _dot_tn(pmat, do2)
            if n > 0:
                prow = slice((n - 1) * BLK, n * BLK)
                dk_ref[prow, sl] = carry[0] + heads(dkbd, 0)
                dv_ref[prow, sl] = carry[1] + heads(dvbd, 0)
            if n < nq:
                carry = (heads(dkbd, 2), heads(dvbd, 2))

    def which(w):
        return pl.BlockSpec((None, qt, ATT_W), lambda i: (w, i, 0))

    def prev(w):
        return pl.BlockSpec((None, BLK, ATT_W), lambda i: (w, jnp.maximum(i * nq - 1, 0), 0))

    tile = pl.BlockSpec((qt, ATT_W), lambda i: (i, 0))
    nxt = pl.BlockSpec((BLK, ATT_W), lambda i: (jnp.minimum((i + 1) * nq, nblk - 1), 0))
    nxt_q = pl.BlockSpec((None, BLK, ATT_W), lambda i: (0, jnp.minimum((i + 1) * nq, nblk - 1), 0))
    return pl.pallas_call(
        body, name=name, grid=(s_len // qt,),
        in_specs=[which(0), which(1), which(2), prev(1), prev(2), tile, tile, tile, nxt_q, nxt, nxt, nxt],
        out_specs=[tile] * 3, out_shape=[jax.ShapeDtypeStruct((s_len, ATT_W), F32)] * 3,
        scratch_shapes=[pltpu.VMEM((4 * (nq + 1) * 2 * BLK, LANES), BF16)] * 2,
        compiler_params=_params(("parallel",)),
    )(qkv, qkv, qkv, qkv, qkv, d_att, lse, delta, qkv, d_att, lse, delta)


def _merge_and_head(o_g, lse_g, p_gate, y_conv, x, tgt, modv, final_g, w_ao, w_o, ts):
    s_len, dm = x.shape
    gw = ATT_W + 2 * dm
    nt = s_len // ts

    def body(o0, o1, o2, l0, l1, l2, pg_ref, yc_ref, x_ref, t_ref, mod_ref, fg_ref, wao_ref, wo_ref,
             loss_ref, dx_ref, dyc_ref, dpg_ref, da0, da1, da2, ls0, ls1, ls2, de0, de1, de2,
             gwo_ref, gwao_ref, gs_ref, slab):
        i = pl.program_id(0)

        @pl.when(i == 0)
        def _():
            loss_ref[...] = jnp.zeros_like(loss_ref)
            gwo_ref[...] = jnp.zeros_like(gwo_ref)
            gwao_ref[...] = jnp.zeros_like(gwao_ref)
            gs_ref[...] = jnp.zeros_like(gs_ref)

        os_, ls_ = [], []
        for dil, o_ref, l_ref in zip(DILATIONS, (o0, o1, o2), (l0, l1, l2)):
            os_.append(jnp.concatenate(_interleave_load(o_ref, slab, dil, ts), axis=1))
            ls_.append(jnp.concatenate(_interleave_load(l_ref, slab, dil, ts), axis=1))
        mx = jnp.maximum(jnp.maximum(ls_[0], ls_[1]), ls_[2])
        wts = [jnp.exp(l - mx) for l in ls_]
        wsum = wts[0] + wts[1] + wts[2]
        att = (wts[0] * os_[0] + wts[1] * os_[1] + wts[2] * os_[2]) / wsum
        lse_all = mx + jnp.log(wsum)

        z_att, g_conv, g_att = pg_ref[:, 0:ATT_W], pg_ref[:, ATT_W:ATT_W + dm], pg_ref[:, ATT_W + dm:gw]
        sgz = _sig(z_att)
        sz = z_att * sgz
        a_att = (att * sz).astype(BF16)
        y_att = _dot(a_att, wao_ref[...])
        y_conv = yc_ref[...]
        sgc, sga = _sig(g_conv), _sig(g_att)
        merged = (sgc * y_conv + sga * y_att).astype(BF16)
        mo = _dot(merged, wo_ref[...])
        gate = mod_ref[2:3, :]
        x2 = x_ref[...] + gate * mo
        r = lax.rsqrt(jnp.mean(x2 * x2, axis=-1, keepdims=True) + EPS)
        xr = x2 * r
        err = xr * fg_ref[...] - t_ref[...]
        loss_ref[...] += 0.5 * jnp.sum(jnp.mean(err * err, axis=-1, keepdims=True))
        dy = err * (1.0 / dm)
        gs_ref[0:1, :] += jnp.sum(dy * xr, axis=0, keepdims=True)
        dyg = dy * fg_ref[...]
        d_x2 = r * dyg - xr * (r * jnp.mean(dyg * xr, axis=-1, keepdims=True))
        dx_ref[...] = d_x2
        gs_ref[1:2, :] += jnp.sum(d_x2 * mo, axis=0, keepdims=True)
        d_mo = (d_x2 * gate).astype(BF16)
        d_mg = _dot_nt(d_mo, wo_ref[...])
        gwo_ref[...] += _dot_tn(merged, d_mo)
        dyc_ref[...] = (d_mg * sgc).astype(BF16)
        dpg_ref[:, ATT_W:ATT_W + dm] = (d_mg * y_conv * sgc * (1.0 - sgc)).astype(BF16)
        d_ya = (d_mg * sga).astype(BF16)
        dpg_ref[:, ATT_W + dm:gw] = (d_mg * y_att * sga * (1.0 - sga)).astype(BF16)
        gwao_ref[...] += _dot_tn(a_att, d_ya)
        d_aa = _dot_nt(d_ya, wao_ref[...])
        dpg_ref[:, 0:ATT_W] = (d_aa * att * _dsilu(z_att, sgz)).astype(BF16)
        d_att = d_aa * sz
        ri = lax.broadcasted_iota(jnp.int32, (ATT_W, ATT_W), 0) // 64
        ci = lax.broadcasted_iota(jnp.int32, (ATT_W, ATT_W), 1) // 64
        ones = jnp.where(ri == ci, 1.0, 0.0).astype(BF16)
        prod = d_att * att
        hi = prod.astype(BF16)
        lo = (prod - hi.astype(F32)).astype(BF16)
        delta = _dot(hi, ones) + _dot(lo, ones)
        for val, refs, dt in ((d_att, (da0, da1, da2), BF16), (lse_all, (ls0, ls1, ls2), F32),
                              (delta, (de0, de1, de2), F32)):
            vals = [val[:, s * LANES:(s + 1) * LANES] for s in range(4)]
            for dil, ref in zip(DILATIONS, refs):
                _deinterleave_store(vals, slab, ref, dil, ts, dt)

    def grp(dil):
        return pl.BlockSpec((dil, ts // dil, ATT_W), lambda i: (0, i, 0))

    tile = pl.BlockSpec((ts, dm), lambda i: (i, 0))
    gate_tile = pl.BlockSpec((ts, gw), lambda i: (i, 0))
    const = lambda shp: pl.BlockSpec(shp, lambda i: tuple(0 for _ in shp))
    grp_shape = lambda dt: [jax.ShapeDtypeStruct((dil, s_len // dil, ATT_W), dt) for dil in DILATIONS]
    return pl.pallas_call(
        body, name="merge_and_head", grid=(nt,),
        in_specs=[grp(d) for d in DILATIONS] * 2
        + [gate_tile, tile, tile, tile, const((8, dm)), const((1, dm)), const((ATT_W, dm)), const((dm, dm))],
        out_specs=[const((8, LANES)), tile, tile, gate_tile] + [grp(d) for d in DILATIONS] * 3
        + [const((dm, dm)), const((ATT_W, dm)), const((8, dm))],
        out_shape=[jax.ShapeDtypeStruct((8, LANES), F32), jax.ShapeDtypeStruct((s_len, dm), F32),
                   jax.ShapeDtypeStruct((s_len, dm), BF16), jax.ShapeDtypeStruct((s_len, gw), BF16)]
        + grp_shape(BF16) + grp_shape(F32) + grp_shape(F32)
        + [jax.ShapeDtypeStruct((dm, dm), F32), jax.ShapeDtypeStruct((ATT_W, dm), F32),
           jax.ShapeDtypeStruct((8, dm), F32)],
        scratch_shapes=[pltpu.VMEM((4, ts, LANES), F32)],
        compiler_params=_params(("arbitrary",)),
    )(*o_g, *lse_g, p_gate, y_conv, x, tgt, modv, final_g, w_ao, w_o)


def _qkv_grad_to_tokens(dqkv_g, pos, ropetab, ts):
    s_len = pos.shape[0]

    def body(*refs):
        g_refs, (pos_ref, tab_ref, o_ref, slab) = refs[:9], refs[9:]
        rc, ra, rb = _rope_tables(pos_ref, tab_ref)
        for gi, dil in enumerate(DILATIONS):
            for w in range(3):
                vals = _interleave_load(g_refs[3 * gi + w], slab, dil, ts)
                for s in range(4):
                    t = vals[s]
                    if w < 2:
                        t = t * rc + pltpu.roll(t * ra, 8, 1) + pltpu.roll(t * rb, LANES - 8, 1)
                    col = (w * 3 + gi) * ATT_W + s * LANES
                    o_ref[:, col:col + LANES] = t.astype(BF16)

    return pl.pallas_call(
        body, name="qkv_grad_to_tokens", grid=(s_len // ts,),
        in_specs=[pl.BlockSpec((dil, ts // dil, ATT_W), lambda i: (0, i, 0)) for dil in DILATIONS for _ in range(3)]
        + [pl.BlockSpec((ts, 1), lambda i: (i, 0)), pl.BlockSpec((8, LANES), lambda i: (0, 0))],
        out_specs=pl.BlockSpec((ts, 9 * ATT_W), lambda i: (i, 0)),
        out_shape=jax.ShapeDtypeStruct((s_len, 9 * ATT_W), BF16),
        scratch_shapes=[pltpu.VMEM((4, ts, LANES), F32)],
        compiler_params=_params(("parallel",)),
    )(*[a for grp in dqkv_g for a in grp], pos, ropetab)


def _piece_specs(dm, ct, block_rows, column_tile_first):
    nc, nq, ng = 3 * dm // ct, 9 * ATT_W // ct, (ATT_W + 2 * dm) // ct
    starts = (0, nc, nc + nq)
    counts = (nc, nq, ng)
    specs = []
    for st, cnt in zip(starts, counts):
        def imap(a, b, st=st, cnt=cnt):
            i, j = (b, a) if column_tile_first else (a, b)
            active = (j >= st) & (j < st + cnt)
            return (jnp.where(active, i, 0), jnp.clip(j - st, 0, cnt - 1))
        specs.append(pl.BlockSpec((block_rows, ct), imap))
    return specs, starts, counts


def _pick_piece(j, starts, counts, refs):
    val = refs[0][...]
    for st, cnt, ref in list(zip(starts, counts, refs))[1:]:
        val = jnp.where(j >= st, ref[...], val)
    return val


def _input_grad(dp_conv, dp_qkv, dp_gate, w_in, x, dx_res, modv, norm_g, ts):
    s_len, dm = x.shape
    ct = COL_TILE
    specs, starts, counts = _piece_specs(dm, ct, ts, False)
    nct = starts[2] + counts[2]

    def body(p0, p1, p2, w_ref, x_ref, dxr_ref, mod_ref, g_ref, gx_ref, gs_ref, acc):
        i, j = pl.program_id(0), pl.program_id(1)

        @pl.when((i == 0) & (j == 0))
        def _():
            gs_ref[...] = jnp.zeros_like(gs_ref)

        contrib = _dot_nt(_pick_piece(j, starts, counts, (p0, p1, p2)), w_ref[...])

        @pl.when(j == 0)
        def _():
            acc[...] = contrib

        @pl.when(j > 0)
        def _():
            acc[...] += contrib

        @pl.when(j == nct - 1)
        def _():
            d_h = acc[...]
            xv = x_ref[...]
            r = lax.rsqrt(jnp.mean(xv * xv, axis=-1, keepdims=True) + EPS)
            xr = xv * r
            gs_ref[0:1, :] += jnp.sum(d_h, axis=0, keepdims=True)
            gs_ref[1:2, :] += jnp.sum(d_h * (xr * g_ref[...]), axis=0, keepdims=True)
            d_n = d_h * (1.0 + mod_ref[1:2, :])
            gs_ref[2:3, :] += jnp.sum(d_n * xr, axis=0, keepdims=True)
            dxn = d_n * g_ref[...]
            gx_ref[...] = dxr_ref[...] + r * dxn - xr * (r * jnp.mean(dxn * xr, axis=-1, keepdims=True))

    tile = pl.BlockSpec((ts, dm), lambda i, j: (i, 0))
    return pl.pallas_call(
        body, name="input_grad", grid=(s_len // ts, nct),
        in_specs=specs + [pl.BlockSpec((dm, ct), lambda i, j: (0, j)), tile, tile,
                          pl.BlockSpec((8, dm), lambda i, j: (0, 0)), pl.BlockSpec((1, dm), lambda i, j: (0, 0))],
        out_specs=[tile, pl.BlockSpec((8, dm), lambda i, j: (0, 0))],
        out_shape=[jax.ShapeDtypeStruct((s_len, dm), F32), jax.ShapeDtypeStruct((8, dm), F32)],
        scratch_shapes=[pltpu.VMEM((ts, dm), F32)],
        compiler_params=_params(("arbitrary", "arbitrary")),
    )(dp_conv, dp_qkv, dp_gate, w_in, x, dx_res, modv, norm_g)


def _w_in_grad(h, dp_conv, dp_qkv, dp_gate, ts):
    s_len, dm = h.shape
    ct = COL_TILE
    specs, starts, counts = _piece_specs(dm, ct, ts, True)
    nct = starts[2] + counts[2]

    def body(h_ref, p0, p1, p2, o_ref):
        j, i = pl.program_id(0), pl.program_id(1)
        contrib = _dot_tn(h_ref[...], _pick_piece(j, starts, counts, (p0, p1, p2)))

        @pl.when(i == 0)
        def _():
            o_ref[...] = contrib

        @pl.when(i > 0)
        def _():
            o_ref[...] += contrib

    return pl.pallas_call(
        body, name="w_in_grad", grid=(nct, s_len // ts),
        in_specs=[pl.BlockSpec((ts, dm), lambda j, i: (i, 0))] + specs,
        out_specs=pl.BlockSpec((dm, ct), lambda j, i: (0, j)),
        out_shape=jax.ShapeDtypeStruct((dm, nct * ct), F32),
        compiler_params=_params(("arbitrary", "arbitrary")),
    )(h, dp_conv, dp_qkv, dp_gate)


def _rope_lane_table():
    l64 = jnp.arange(LANES) % 64
    half = ROT_DIM // 2
    inv_freq = ROPE_THETA ** (-(jnp.arange(half, dtype=F32) * 2.0 / ROT_DIM))
    rot = l64 < ROT_DIM
    rows = [jnp.where(rot, inv_freq[l64 % half], 0.0), (l64 < half).astype(F32),
            ((l64 >= half) & rot).astype(F32), rot.astype(F32)]
    return jnp.concatenate([jnp.stack(rows), jnp.zeros((4, LANES), F32)], axis=0)


def _tile_sizes(s_len):
    l_min = s_len // DILATIONS[-1]
    ts_big = min(1024, s_len // 2)
    ts_mid = 256
    ts_head = 256
    qt = min(512, l_min)
    return ts_big, ts_mid, ts_head, qt


def kernel(x, c, positions, norm_g, w_ada, b_ada, w_in, conv_w, conv_b, conv_ln_g, conv_ln_b, w_conv_out, w_att_out, w_o, final_g, loss_target, m_norm_g, m_w_ada, m_b_ada, m_w_in, m_conv_w, m_conv_b, m_conv_ln_g, m_conv_ln_b, m_w_conv_out, m_w_att_out, m_w_o, m_final_g, v_norm_g, v_w_ada, v_b_ada, v_w_in, v_conv_w, v_conv_b, v_conv_ln_g, v_conv_ln_b, v_w_conv_out, v_w_att_out, v_w_o, v_final_g):
    s_len, dm = x.shape[1], x.shape[2]
    ts_big, ts_mid, ts_head, qt = _tile_sizes(s_len)
    xi, yi, cidx = _my_place()
    chip = 2 * xi + yi
    batch = 4 * xi + 2 * yi + cidx
    x2d, tgt = x[0], loss_target[0]
    pos = positions.reshape(s_len, 1)
    wa_l, wi_l, cw_l = w_ada[0], w_in[0], conv_w[0]
    wco_l, wao_l, wo_l = w_conv_out[0], w_att_out[0], w_o[0]
    ada_w = wa_l.shape[1]
    cw_cols = cw_l.shape[1]

    cw_pad = jnp.pad(cw_l, ((0, HALO - CONV_K), (0, 0)))
    small_in = jnp.concatenate([jnp.broadcast_to(c, (8, dm)), cw_pad.reshape(8, dm)], axis=0)
    small = _allgather_small(small_in).reshape(8, 16, dm)
    c_all = small[:, 0, :]
    conv_w_full = jnp.concatenate(
        [small[2 * p, 8:16, :].reshape(HALO, cw_cols) for p in range(4)], axis=1)
    b_l = lax.dynamic_slice(b_ada, (0, chip * ada_w), (1, ada_w))
    mod_parts = _allgather_small(_mod_part(c_all, wa_l, b_l)).reshape(8, 8, ada_w)
    mod_rows = lax.dynamic_index_in_dim(mod_parts, batch, axis=1, keepdims=False)
    mod = jnp.concatenate([mod_rows[2 * p] for p in range(4)], axis=0).reshape(3, dm)
    modv = jnp.concatenate([mod, jnp.zeros((5, dm), F32)], axis=0)

    kinds = ("col", "row", "col", "row")
    w_in_b, w_co_b, w_ao_b, w_o_b = _allgather_weights([wi_l, wco_l, wao_l, wo_l], kinds)

    ropetab = _rope_lane_table()
    p_conv, qkv0, qkv1, qkv2, p_gate, h_b = _input_projection(x2d, pos, modv, norm_g, ropetab, w_in_b, ts_big)
    y_conv, u0, u1 = _conv_forward(p_conv, conv_w_full, conv_b, conv_ln_g, conv_ln_b, w_co_b, ts_mid)
    qkv_flat = [q.reshape(3, s_len, ATT_W) for q in (qkv0, qkv1, qkv2)]
    o_g, lse_g = [], []
    for gi, dil in enumerate(DILATIONS):
        o, l = _attention_forward(qkv_flat[gi], s_len // dil, qt, "attention_forward_%d" % dil)
        o_g.append(o.reshape(dil, s_len // dil, ATT_W))
        lse_g.append(l.reshape(dil, s_len // dil, ATT_W))

    (loss_p, dx_res, dyc, dp_gate, da0, da1, da2, ls0, ls1, ls2, de0, de1, de2,
     g_wo, g_wao, head_sums) = _merge_and_head(o_g, lse_g, p_gate, y_conv, x2d, tgt, modv,
                                               final_g.reshape(1, dm), w_ao_b, w_o_b, ts_head)
    loss = lax.psum(loss_p[0, 0], ("x", "y", "c"))

    dp_conv, g_wco, conv_sums, g_cw = _conv_backward(dyc, p_conv, u0, u1, conv_w_full, conv_ln_g, conv_ln_b,
                                                     w_co_b, ts_mid)
    dqkv_g = []
    for gi, (dil, da, ls, de) in enumerate(zip(DILATIONS, (da0, da1, da2), (ls0, ls1, ls2), (de0, de1, de2))):
        flat = lambda a: a.reshape(s_len, ATT_W)
        dqkv = _attention_backward(qkv_flat[gi], flat(da), flat(ls), flat(de), s_len // dil, qt,
                                   "attention_backward_%d" % dil)
        dqkv_g.append([a.reshape(dil, s_len // dil, ATT_W) for a in dqkv])
    dp_qkv = _qkv_grad_to_tokens(dqkv_g, pos, ropetab, ts_mid)
    grad_x, in_sums = _input_grad(dp_conv, dp_qkv, dp_gate, w_in_b, x2d, dx_res, modv, norm_g, ts_big)
    g_win = _w_in_grad(h_b, dp_conv, dp_qkv, dp_gate, ts_big)

    grads = [g_win, g_wco, g_wao, g_wo]
    full_shapes = [g.shape for g in grads]
    c_arr = jnp.reshape(cidx, (1,)).astype(jnp.int32)
    recv_halves = _reduce_pair_exchange(grads, kinds)
    halves = [_reduce_pair_sum(g, pa, kind, c_arr, "reduce_pair_sum_%d" % k)
              for k, (g, pa, kind) in enumerate(zip(grads, recv_halves, kinds))]
    recvd = _reduce_to_owner(halves, kinds, full_shapes)
    gr_win, gr_wco, gr_wao, gr_wo = _reduce_finish(halves, recvd, kinds, full_shapes)

    rows = [in_sums[2:3], conv_sums[2:3], conv_sums[0:1], conv_sums[1:2], head_sums[0:1],
            in_sums[0:1], in_sums[1:2], head_sums[1:2], g_cw, jnp.zeros((8, dm), F32)]
    part = jnp.concatenate(rows, axis=0)
    gathered = _allgather_small(part)
    tot = _sum_devices(gathered, 48)
    dmod_all = gathered.reshape(8, 48, dm)[:, 5:8, :].reshape(8, 3 * dm)
    dmod_l = lax.dynamic_slice(dmod_all, (0, chip * ada_w), (8, ada_w))
    gr_wada = _grad_w_ada(c_all.T, dmod_l)
    gr_cw = lax.dynamic_slice(tot[8:8 + HALO], (0, chip * cw_cols), (HALO, cw_cols))

    def pack8(ng, cb, lg, lb, fg, ba):
        return jnp.concatenate([ng, cb, lg, lb, fg.reshape(1, dm), ba.reshape(3, dm)], axis=0)

    pad_cw = lambda a: jnp.pad(a[0], ((0, HALO - CONV_K), (0, 0)))
    w8 = pack8(norm_g, conv_b, conv_ln_g, conv_ln_b, final_g, b_ada)
    m8 = pack8(m_norm_g, m_conv_b, m_conv_ln_g, m_conv_ln_b, m_final_g, m_b_ada)
    v8 = pack8(v_norm_g, v_conv_b, v_conv_ln_g, v_conv_ln_b, v_final_g, v_b_ada)
    g8 = tot[0:8]
    upd = {
        "small": _adamw(w8, g8, m8, v8, "adamw_small"),
        "w_ada": _adamw(wa_l, gr_wada, m_w_ada[0], v_w_ada[0], "adamw_w_ada"),
        "w_in": _adamw(wi_l, gr_win, m_w_in[0], v_w_in[0], "adamw_w_in"),
        "conv_w": _adamw(cw_pad, gr_cw, pad_cw(m_conv_w), pad_cw(v_conv_w), "adamw_conv_w"),
        "w_co": _adamw(wco_l, gr_wco, m_w_conv_out[0], v_w_conv_out[0], "adamw_w_conv_out"),
        "w_ao": _adamw(wao_l, gr_wao, m_w_att_out[0], v_w_att_out[0], "adamw_w_att_out"),
        "w_o": _adamw(wo_l, gr_wo, m_w_o[0], v_w_o[0], "adamw_w_o"),
    }

    def family(which):
        if which is None:
            sm = g8
            big = {"w_ada": gr_wada, "w_in": gr_win, "conv_w": gr_cw, "w_co": gr_wco, "w_ao": gr_wao, "w_o": gr_wo}
        else:
            sm = upd["small"][which]
            big = {k: upd[k][which] for k in ("w_ada", "w_in", "conv_w", "w_co", "w_ao", "w_o")}
        return [sm[0:1], big["w_ada"][None], sm[5:8].reshape(1, 3 * dm), big["w_in"][None],
                big["conv_w"][None, :CONV_K], sm[1:2], sm[2:3], sm[3:4], big["w_co"][None],
                big["w_ao"][None], big["w_o"][None], sm[4]]

    return (loss, grad_x[None], *family(None), *family(0), *family(1), *family(2))
```

```python
import jax
import jax.numpy as jnp
from jax import lax
from jax.experimental import pallas as pl
from jax.experimental.pallas import tpu as pltpu

F32 = jnp.float32
BF16 = jnp.bfloat16
MESH = pl.DeviceIdType.MESH
ANY = pl.BlockSpec(memory_space=pl.ANY)
VM = pl.BlockSpec(memory_space=pltpu.VMEM)

EPS = 1e-6
NEG = -1e30
ATT_W = 512
DILATIONS = (1, 4, 16)
BLK = 128
CONV_K = 31
HALO = 32
CONV_ROWS = 16
ROT_DIM = 16
ROPE_THETA = 500000.0
COL_TILE = 512
LANES = 128
VMEM_LIMIT = 56 * 1024 * 1024

ADAM_LR, ADAM_B1, ADAM_B2, ADAM_EPS, ADAM_WD, ADAM_STEP = 0.001, 0.9, 0.999, 1e-08, 0.01, 10


def _params(sem=None, vmem=VMEM_LIMIT):
    return pltpu.CompilerParams(dimension_semantics=sem, vmem_limit_bytes=vmem)


def _dot(a, b):
    return jnp.dot(a, b, preferred_element_type=F32)


def _dot_nt(a, b):
    return lax.dot_general(a, b, (((1,), (1,)), ((), ())), preferred_element_type=F32)


def _dot_tn(a, b):
    return lax.dot_general(a, b, (((0,), (0,)), ((), ())), preferred_element_type=F32)


def _sig(x):
    return jax.nn.sigmoid(x)


def _dsilu(x, s):
    return s * (1.0 + x * (1.0 - s))


def _my_place():
    return lax.axis_index("x"), lax.axis_index("y"), lax.axis_index("c")


def _allgather_small(x_shard):
    m_per, n = x_shard.shape

    def body(x_ref, out_ref, send_sems, recv_sems, local_sem):
        x, y, c = _my_place()
        me, sibling = (x, y, c), (x, y, 1 - c)
        chips = [(1 - x, y), (x, 1 - y), (1 - x, 1 - y)]

        def rows(px, py, pc):
            return out_ref.at[pl.ds((4 * px + 2 * py + pc) * m_per, m_per), :]

        def copy(k, block, to, src=None):
            return pltpu.make_async_remote_copy(
                src_ref=rows(*block) if src is None else src, dst_ref=rows(*block),
                send_sem=send_sems.at[k], recv_sem=recv_sems.at[k],
                device_id=to, device_id_type=MESH)

        mine = pltpu.make_async_copy(x_ref, rows(*me), local_sem)
        mine.start()
        first = [copy(0, me, sibling, src=x_ref)]
        first += [copy(1 + j, me, (*chip, c), src=x_ref) for j, chip in enumerate(chips)]
        for cp in first:
            cp.start()
        passed = [copy(4 + j, (*chip, c), sibling) for j, chip in enumerate(chips)]
        for j, chip in enumerate(chips):
            copy(1 + j, (*chip, c), me).wait_recv()
            passed[j].start()
        copy(0, sibling, me).wait_recv()
        for j, chip in enumerate(chips):
            copy(4 + j, (*chip, 1 - c), me).wait_recv()
        for cp in first + passed:
            cp.wait_send()
        mine.wait()

    return pl.pallas_call(
        body, name="allgather_small",
        out_shape=jax.ShapeDtypeStruct((8 * m_per, n), x_shard.dtype),
        in_specs=[VM], out_specs=VM,
        scratch_shapes=[pltpu.SemaphoreType.DMA((7,)), pltpu.SemaphoreType.DMA((7,)),
                        pltpu.SemaphoreType.DMA],
    )(x_shard)


def _shard_window(ref, kind, p, n_shards=4):
    r, c = ref.shape
    if kind == "col":
        w = c // n_shards
        return ref.at[:, pl.ds(p * w, w)]
    w = r // n_shards
    return ref.at[pl.ds(p * w, w), :]


def _half_window(ref, kind, hc):
    r, c = ref.shape
    if kind == "col":
        return ref.at[pl.ds(hc * (r // 2), r // 2), :]
    return ref.at[:, pl.ds(hc * (c // 2), c // 2)]


def _allgather_weights(shards, kinds):
    n = len(shards)
    full_shapes = []
    for s, kind in zip(shards, kinds):
        r, c = s.shape
        full_shapes.append((r, 4 * c) if kind == "col" else (4 * r, c))

    def body(*refs):
        w_refs, out_refs = refs[:n], refs[n:2 * n]
        bf_refs = refs[2 * n:3 * n]
        send_sems, recv_sems, local_sems = refs[3 * n:]
        x, y, c = _my_place()
        p = 2 * x + y
        sibling = (x, y, 1 - c)
        chips = [(1 - x, y), (x, 1 - y), (1 - x, 1 - y)]
        locals_ = []
        for k in range(n):
            bf_refs[k][...] = w_refs[k][...].astype(BF16)
            cp = pltpu.make_async_copy(bf_refs[k], _shard_window(out_refs[k], kinds[k], p), local_sems.at[k])
            cp.start()
            locals_.append(cp)

        def landed(k, chip, hc):
            return _half_window(_shard_window(out_refs[k], kinds[k], 2 * chip[0] + chip[1]), kinds[k], hc)

        def ici(j, k, chip_src, to, src=None):
            return pltpu.make_async_remote_copy(
                src_ref=landed(k, chip_src, c) if src is None else src, dst_ref=landed(k, chip_src, c),
                send_sem=send_sems.at[j * n + k], recv_sem=recv_sems.at[j * n + k],
                device_id=to, device_id_type=MESH)

        first = []
        for j, chip in enumerate(chips):
            for k in range(n):
                cp = ici(j, k, (x, y), (*chip, c), src=_half_window(bf_refs[k], kinds[k], c))
                cp.start()
                first.append(cp)
        passed = []
        for j, chip in enumerate(chips):
            for k in range(n):
                ici(j, k, chip, (x, y, c)).wait_recv()
                cp = pltpu.make_async_remote_copy(
                    src_ref=landed(k, chip, c), dst_ref=landed(k, chip, c),
                    send_sem=send_sems.at[(3 + j) * n + k], recv_sem=recv_sems.at[(3 + j) * n + k],
                    device_id=sibling, device_id_type=MESH)
                cp.start()
                passed.append(cp)
        for j, chip in enumerate(chips):
            for k in range(n):
                pltpu.make_async_remote_copy(
                    src_ref=landed(k, chip, 1 - c), dst_ref=landed(k, chip, 1 - c),
                    send_sem=send_sems.at[(3 + j) * n + k], recv_sem=recv_sems.at[(3 + j) * n + k],
                    device_id=sibling, device_id_type=MESH).wait_recv()
        for cp in first + passed:
            cp.wait_send()
        for cp in locals_:
            cp.wait()

    return pl.pallas_call(
        body, name="allgather_weights",
        out_shape=[jax.ShapeDtypeStruct(s, BF16) for s in full_shapes],
        in_specs=[VM] * n, out_specs=[ANY] * n,
        scratch_shapes=[pltpu.VMEM(s.shape, BF16) for s in shards]
        + [pltpu.SemaphoreType.DMA((6 * n,)), pltpu.SemaphoreType.DMA((6 * n,)), pltpu.SemaphoreType.DMA((n,))],
        compiler_params=_params(),
    )(*shards)


def _reduce_pair_exchange(grads, kinds):
    n = len(grads)
    half_shapes = [(g.shape[0] // 2, g.shape[1]) if kind == "col" else (g.shape[0], g.shape[1] // 2)
                   for g, kind in zip(grads, kinds)]

    def body(*refs):
        g_refs, pa_refs = refs[:n], refs[n:2 * n]
        send_sems, recv_sems = refs[2 * n:]
        x, y, c = _my_place()
        cps = []
        for k in range(n):
            cp = pltpu.make_async_remote_copy(
                src_ref=_half_window(g_refs[k], kinds[k], 1 - c), dst_ref=pa_refs[k],
                send_sem=send_sems.at[k], recv_sem=recv_sems.at[k],
                device_id=(x, y, 1 - c), device_id_type=MESH)
            cp.start()
            cps.append(cp)
        for cp in cps:
            cp.wait()

    return pl.pallas_call(
        body, name="reduce_pair_exchange",
        out_shape=[jax.ShapeDtypeStruct(s, F32) for s in half_shapes],
        in_specs=[ANY] * n, out_specs=[ANY] * n,
        scratch_shapes=[pltpu.SemaphoreType.DMA((n,)), pltpu.SemaphoreType.DMA((n,))],
    )(*grads)


def _row_tile(rows, cols, itemsize=4, target=2 * 1024 * 1024, mult=16):
    t = rows
    while t % 2 == 0 and t // 2 >= mult and (t // 2) % mult == 0 and t * cols * itemsize > target:
        t //= 2
    return t


def _reduce_pair_sum(g, pa, kind, c_arr, name):
    hr, hc_ = pa.shape
    tr = _row_tile(hr, hc_)
    nb = hr // tr

    def body(c_ref, g_ref, pa_ref, o_ref):
        o_ref[...] = (g_ref[...] + pa_ref[...]).astype(BF16)

    if kind == "col":
        g_map = lambda i, c_ref: (c_ref[0] * nb + i, 0)
    else:
        g_map = lambda i, c_ref: (i, c_ref[0])
    return pl.pallas_call(
        body, name=name,
        grid_spec=pltpu.PrefetchScalarGridSpec(
            num_scalar_prefetch=1, grid=(nb,),
            in_specs=[pl.BlockSpec((tr, hc_), g_map), pl.BlockSpec((tr, hc_), lambda i, c_ref: (i, 0))],
            out_specs=pl.BlockSpec((tr, hc_), lambda i, c_ref: (i, 0))),
        out_shape=jax.ShapeDtypeStruct((hr, hc_), BF16),
        compiler_params=_params(("parallel",)),
    )(c_arr, g, pa)


def _half_shard_shape(full_shape, kind):
    r, c = full_shape
    return (r // 2, c // 4) if kind == "col" else (r // 4, c // 2)


def _reduce_to_owner(halves, kinds, full_shapes):
    n = len(halves)
    hs = [_half_shard_shape(fs, kind) for fs, kind in zip(full_shapes, kinds)]

    def body(*refs):
        h_refs, rc_refs = refs[:n], refs[n:2 * n]
        send_sems, recv_sems = refs[2 * n:]
        x, y, c = _my_place()
        chips = [(1 - x, y), (x, 1 - y), (1 - x, 1 - y)]
        cps = []
        for j, chip in enumerate(chips):
            pj = 2 * chip[0] + chip[1]
            for k in range(n):
                cp = pltpu.make_async_remote_copy(
                    src_ref=_shard_window(h_refs[k], kinds[k], pj), dst_ref=rc_refs[k].at[j],
                    send_sem=send_sems.at[j * n + k], recv_sem=recv_sems.at[j * n + k],
                    device_id=(*chip, c), device_id_type=MESH)
                cp.start()
                cps.append(cp)
        for cp in cps:
            cp.wait()

    return pl.pallas_call(
        body, name="reduce_to_owner",
        out_shape=[jax.ShapeDtypeStruct((3,) + s, BF16) for s in hs],
        in_specs=[ANY] * n, out_specs=[ANY] * n,
        scratch_shapes=[pltpu.SemaphoreType.DMA((3 * n,)), pltpu.SemaphoreType.DMA((3 * n,))],
    )(*halves)


def _reduce_finish(halves, recvd, kinds, full_shapes):
    n = len(halves)
    hs = [_half_shard_shape(fs, kind) for fs, kind in zip(full_shapes, kinds)]
    shard_shapes = [(fs[0], fs[1] // 4) if kind == "col" else (fs[0] // 4, fs[1])
                    for fs, kind in zip(full_shapes, kinds)]

    def body(*refs):
        h_refs, rc_refs, gs_refs = refs[:n], refs[n:2 * n], refs[2 * n:3 * n]
        own_refs, gh_refs = refs[3 * n:4 * n], refs[4 * n:5 * n]
        in_sems, loc_sems, send_sems, recv_sems = refs[5 * n:]
        x, y, c = _my_place()
        p = 2 * x + y
        loads = []
        for k in range(n):
            cp = pltpu.make_async_copy(_shard_window(h_refs[k], kinds[k], p), own_refs[k], in_sems.at[k])
            cp.start()
            loads.append(cp)
        outs = []
        for k in range(n):
            loads[k].wait()
            gh_refs[k][...] = (own_refs[k][...].astype(F32) + rc_refs[k][0].astype(F32)
                               + rc_refs[k][1].astype(F32) + rc_refs[k][2].astype(F32))
            dst = _half_window(gs_refs[k], kinds[k], c)
            lc = pltpu.make_async_copy(gh_refs[k], dst, loc_sems.at[k])
            lc.start()
            rc = pltpu.make_async_remote_copy(
                src_ref=gh_refs[k], dst_ref=dst, send_sem=send_sems.at[k], recv_sem=recv_sems.at[k],
                device_id=(x, y, 1 - c), device_id_type=MESH)
            rc.start()
            outs.append((lc, rc))
        for k, (lc, rc) in enumerate(outs):
            lc.wait()
            rc.wait_send()
            pltpu.make_async_remote_copy(
                src_ref=gh_refs[k], dst_ref=_half_window(gs_refs[k], kinds[k], 1 - c),
                send_sem=send_sems.at[k], recv_sem=recv_sems.at[k],
                device_id=(x, y, 1 - c), device_id_type=MESH).wait_recv()

    return pl.pallas_call(
        body, name="reduce_finish",
        out_shape=[jax.ShapeDtypeStruct(s, F32) for s in shard_shapes],
        in_specs=[ANY] * n + [VM] * n, out_specs=[ANY] * n,
        scratch_shapes=[pltpu.VMEM(s, BF16) for s in hs] + [pltpu.VMEM(s, F32) for s in hs]
        + [pltpu.SemaphoreType.DMA((n,)) for _ in range(4)],
        compiler_params=_params(),
    )(*halves, *recvd)


def _mod_part(c_all, w_ada_l, b_l):
    def body(c_ref, w_ref, b_ref, o_ref):
        o_ref[...] = _dot(c_ref[...].astype(BF16), w_ref[...].astype(BF16)) + b_ref[...]

    return pl.pallas_call(
        body, name="mod_part", out_shape=jax.ShapeDtypeStruct((8, w_ada_l.shape[1]), F32),
        in_specs=[VM, VM, VM], out_specs=VM, compiler_params=_params(),
    )(c_all, w_ada_l, b_l)


def _sum_devices(parts, m):
    def body(p_ref, o_ref):
        acc = p_ref[0:m, :]
        for d in range(1, 8):
            acc = acc + p_ref[d * m:(d + 1) * m, :]
        o_ref[...] = acc

    return pl.pallas_call(
        body, name="sum_devices", out_shape=jax.ShapeDtypeStruct((m, parts.shape[1]), F32),
        in_specs=[VM], out_specs=VM, compiler_params=_params(),
    )(parts)


def _grad_w_ada(c_all_t, dmod_l):
    d, w = c_all_t.shape[0], dmod_l.shape[1]

    def body(ct_ref, dm_ref, o_ref):
        acc = ct_ref[:, 0:1] * dm_ref[0:1, :]
        for b in range(1, 8):
            acc = acc + ct_ref[:, b:b + 1] * dm_ref[b:b + 1, :]
        o_ref[...] = acc

    return pl.pallas_call(
        body, name="grad_w_ada", out_shape=jax.ShapeDtypeStruct((d, w), F32),
        in_specs=[VM, VM], out_specs=VM, compiler_params=_params(),
    )(c_all_t, dmod_l)


def _adamw(w, g, m, v, name):
    r, c = w.shape
    tr = _row_tile(r, c, target=1024 * 1024, mult=8)
    c1 = 1.0 - ADAM_B1 ** ADAM_STEP
    c2 = 1.0 - ADAM_B2 ** ADAM_STEP

    def body(w_ref, g_ref, m_ref, v_ref, d_ref, nm_ref, nv_ref):
        gv = g_ref[...]
        nm = ADAM_B1 * m_ref[...] + (1.0 - ADAM_B1) * gv
        nv = ADAM_B2 * v_ref[...] + (1.0 - ADAM_B2) * (gv * gv)
        m_hat = nm / c1
        v_hat = nv / c2
        d_ref[...] = -ADAM_LR * (m_hat / (jnp.sqrt(v_hat) + ADAM_EPS) + ADAM_WD * w_ref[...])
        nm_ref[...] = nm
        nv_ref[...] = nv

    spec = pl.BlockSpec((tr, c), lambda i: (i, 0))
    return pl.pallas_call(
        body, name=name, grid=(r // tr,),
        out_shape=[jax.ShapeDtypeStruct((r, c), F32)] * 3,
        in_specs=[spec] * 4, out_specs=[spec] * 3,
        compiler_params=_params(("parallel",)),
    )(w, g, m, v)


def _rope_tables(pos_ref, tab_ref):
    ang = pos_ref[...].astype(F32) * tab_ref[0:1, :]
    cs, sn = jnp.cos(ang), jnp.sin(ang)
    return jnp.where(tab_ref[3:4, :] > 0, cs, 1.0), -sn * tab_ref[1:2, :], sn * tab_ref[2:3, :]


def _deinterleave_store(vals, slab, out_ref, d, ts, dtype):
    if d == 1:
        for s in range(4):
            out_ref[0, :, s * LANES:(s + 1) * LANES] = vals[s].astype(dtype)
        return
    for s in range(4):
        slab[s] = vals[s]
    for r in range(d):
        for s in range(4):
            out_ref[r, :, s * LANES:(s + 1) * LANES] = slab[s, pl.ds(r, ts // d, stride=d), :].astype(dtype)


def _interleave_load(blk_ref, slab, d, ts):
    if d == 1:
        return [blk_ref[0, :, s * LANES:(s + 1) * LANES] for s in range(4)]
    for r in range(d):
        for s in range(4):
            slab[s, pl.ds(r, ts // d, stride=d), :] = blk_ref[r, :, s * LANES:(s + 1) * LANES]
    return [slab[s] for s in range(4)]


def _input_projection(x, pos, modv, norm_g, ropetab, w_in, ts):
    s_len, d_model = x.shape
    ci = w_in.shape[1]
    ct = COL_TILE
    nc, ng = 3 * d_model // ct, (ATT_W + 2 * d_model) // ct
    nct = nc + 9 + ng
    assert ci == nct * ct and s_len % ts == 0

    def body(x_ref, pos_ref, mod_ref, g_ref, tab_ref, w_ref,
             pc_ref, q0_ref, q1_ref, q2_ref, pg_ref, ht_ref, h_scr, slab, rc, ra, rb):
        j = pl.program_id(1)

        @pl.when(j == 0)
        def _():
            xv = x_ref[...]
            r = lax.rsqrt(jnp.mean(xv * xv, axis=-1, keepdims=True) + EPS)
            h = (xv * r) * g_ref[...] * (1.0 + mod_ref[1:2, :]) + mod_ref[0:1, :]
            hb = h.astype(BF16)
            h_scr[...] = hb
            ht_ref[...] = h.T.astype(BF16)
            c_, a_, b_ = _rope_tables(pos_ref, tab_ref)
            rc[...] = c_
            ra[...] = a_
            rb[...] = b_

        res = _dot(h_scr[...], w_ref[...])

        @pl.when(j < nc)
        def _():
            pc_ref[...] = res

        for gi, (dil, q_ref) in enumerate(zip(DILATIONS, (q0_ref, q1_ref, q2_ref))):
            lo = nc + 3 * gi

            @pl.when((j >= lo) & (j < lo + 2))
            def _(dil=dil, q_ref=q_ref):
                vals = []
                for s in range(4):
                    t = res[:, s * LANES:(s + 1) * LANES]
                    vals.append(t * rc[...] + pltpu.roll(t, LANES - 8, 1) * ra[...]
                                + pltpu.roll(t, 8, 1) * rb[...])
                _deinterleave_store(vals, slab, q_ref, dil, ts, BF16)

            @pl.when(j == lo + 2)
            def _(dil=dil, q_ref=q_ref):
                vals = [res[:, s * LANES:(s + 1) * LANES] for s in range(4)]
                _deinterleave_store(vals, slab, q_ref, dil, ts, BF16)

        @pl.when(j >= nc + 9)
        def _():
            pg_ref[...] = res

    def w_map(i, j):
        t = j - nc
        qcol = nc + (t % 3) * 3 + t // 3
        return (0, jnp.where((j >= nc) & (j < nc + 9), qcol, j))

    def q_spec(gi, dil):
        return pl.BlockSpec((None, dil, ts // dil, ATT_W),
                            lambda i, j: (jnp.clip(j - nc - 3 * gi, 0, 2), 0, i, 0))

    out_shape = [jax.ShapeDtypeStruct((s_len, 3 * d_model), F32)]
    out_shape += [jax.ShapeDtypeStruct((3, dil, s_len // dil, ATT_W), BF16) for dil in DILATIONS]
    out_shape += [jax.ShapeDtypeStruct((s_len, ATT_W + 2 * d_model), F32),
                  jax.ShapeDtypeStruct((d_model, s_len), BF16)]
    return pl.pallas_call(
        body, name="input_projection", grid=(s_len // ts, nct),
        in_specs=[pl.BlockSpec((ts, d_model), lambda i, j: (i, 0)),
                  pl.BlockSpec((ts, 1), lambda i, j: (i, 0)),
                  pl.BlockSpec((8, d_model), lambda i, j: (0, 0)),
                  pl.BlockSpec((1, d_model), lambda i, j: (0, 0)),
                  pl.BlockSpec((8, LANES), lambda i, j: (0, 0)),
                  pl.BlockSpec((d_model, ct), w_map)],
        out_specs=[pl.BlockSpec((ts, ct), lambda i, j: (i, jnp.clip(j, 0, nc - 1)))]
        + [q_spec(gi, dil) for gi, dil in enumerate(DILATIONS)]
        + [pl.BlockSpec((ts, ct), lambda i, j: (i, jnp.clip(j - nc - 9, 0, ng - 1))),
           pl.BlockSpec((d_model, ts), lambda i, j: (0, i))],
        out_shape=out_shape,
        scratch_shapes=[pltpu.VMEM((ts, d_model), BF16), pltpu.VMEM((4, ts, LANES), F32),
                        pltpu.VMEM((ts, LANES), F32), pltpu.VMEM((ts, LANES), F32),
                        pltpu.VMEM((ts, LANES), F32)],
        compiler_params=_params(("arbitrary", "arbitrary")),
    )(x, pos, modv, norm_g, ropetab, w_in)


def _layernorm_stats(u1):
    mu = jnp.mean(u1, axis=-1, keepdims=True)
    xc = u1 - mu
    rstd = lax.rsqrt(jnp.mean(xc * xc, axis=-1, keepdims=True) + EPS)
    return xc * rstd, rstd


def _shifted_copies(win, shf, ts):
    rows = ts + HALO - 8
    for b in range(1, 8):
        shf[b - 1, 0:rows, :] = win[pl.ds(b, rows), :]


def _tap(win, shf, off, r0):
    a, b = divmod(off, 8)
    start = 8 * a + r0
    if b == 0:
        return win[start:start + CONV_ROWS, :]
    return shf[b - 1, start:start + CONV_ROWS, :]


def _conv_forward(p_conv, conv_w, conv_b, ln_g, ln_b, w_co, ts):
    s_len, d3 = p_conv.shape
    dm = d3 // 3

    def body(p_ref, cw_ref, cb_ref, g_ref, b_ref, w_ref, y_ref, u0_ref, u1_ref, win, shf):
        i = pl.program_id(0)

        @pl.when(i == 0)
        def _():
            win[0:HALO, :] = jnp.zeros((HALO, dm), F32)

        a, b, z = p_ref[:, 0:dm], p_ref[:, dm:2 * dm], p_ref[:, 2 * dm:3 * dm]
        u0 = a * _sig(b)
        win[HALO:HALO + ts, :] = u0
        u0_ref[...] = u0
        _shifted_copies(win, shf, ts)
        for r0 in range(0, ts, CONV_ROWS):
            acc = jnp.broadcast_to(cb_ref[...], (CONV_ROWS, dm))
            for k in range(CONV_K):
                acc = acc + cw_ref[k:k + 1, :] * _tap(win, shf, HALO - (CONV_K - 1) + k, r0)
            u1_ref[r0:r0 + CONV_ROWS, :] = acc
        xh, _ = _layernorm_stats(u1_ref[...])
        u2 = xh * g_ref[...] + b_ref[...]
        a_conv = (u2 * _sig(u2)) * (z * _sig(z))
        y_ref[...] = _dot(a_conv.astype(BF16), w_ref[...])
        win[0:HALO, :] = win[ts:ts + HALO, :]

    row = pl.BlockSpec((1, dm), lambda i: (0, 0))
    tile = pl.BlockSpec((ts, dm), lambda i: (i, 0))
    return pl.pallas_call(
        body, name="conv_forward", grid=(s_len // ts,),
        in_specs=[pl.BlockSpec((ts, d3), lambda i: (i, 0)),
                  pl.BlockSpec((HALO, dm), lambda i: (0, 0)), row, row, row,
                  pl.BlockSpec((dm, dm), lambda i: (0, 0))],
        out_specs=[tile, tile, tile],
        out_shape=[jax.ShapeDtypeStruct((s_len, dm), F32)] * 3,
        scratch_shapes=[pltpu.VMEM((ts + HALO, dm), F32), pltpu.VMEM((7, ts + HALO - 8, dm), F32)],
        compiler_params=_params(("arbitrary",)),
    )(p_conv, conv_w, conv_b, ln_g, ln_b, w_co)


def _conv_backward(dp, dyc, p_conv, u0, u1, conv_w, ln_g, ln_b, w_co, ts):
    s_len, d3 = p_conv.shape
    dm = d3 // 3
    nt = s_len // ts
    hb = ts // HALO

    def body(dp_in, dy_ref, p_ref, u0_ref, uh_ref, u1_ref, cw_ref, g_ref, b_ref, w_ref,
             dp_ref, gw_ref, gs_ref, gcw_ref, dwin, uwin, shf):
        del dp_in
        i = pl.program_id(0)
        ti = nt - 1 - i

        @pl.when(i == 0)
        def _():
            gw_ref[...] = jnp.zeros_like(gw_ref)
            gs_ref[...] = jnp.zeros_like(gs_ref)
            gcw_ref[...] = jnp.zeros_like(gcw_ref)
            dwin[ts:ts + HALO, :] = jnp.zeros((HALO, dm), F32)

        dy = dy_ref[...]
        z = p_ref[:, 2 * dm:3 * dm]
        d_ac = _dot_nt(dy, w_ref[...])
        xh, rstd = _layernorm_stats(u1_ref[...])
        u2 = xh * g_ref[...] + b_ref[...]
        sg2, sgz = _sig(u2), _sig(z)
        u3, sz = u2 * sg2, z * sgz
        gw_ref[...] += _dot_tn((u3 * sz).astype(BF16), dy)
        d_z = d_ac * u3 * _dsilu(z, sgz)
        d_u2 = d_ac * sz * _dsilu(u2, sg2)
        gs_ref[0:1, :] += jnp.sum(d_u2 * xh, axis=0, keepdims=True)
        gs_ref[1:2, :] += jnp.sum(d_u2, axis=0, keepdims=True)
        dxh = d_u2 * g_ref[...]
        d_u1 = rstd * (dxh - jnp.mean(dxh, axis=-1, keepdims=True)
                       - xh * jnp.mean(dxh * xh, axis=-1, keepdims=True))
        gs_ref[2:3, :] += jnp.sum(d_u1, axis=0, keepdims=True)
        dwin[0:ts, :] = d_u1
        uwin[0:HALO, :] = jnp.where(ti == 0, 0.0, uh_ref[...])
        uwin[HALO:HALO + ts, :] = u0_ref[...]
        dp_ref[:, 2 * dm:3 * dm] = d_z.astype(BF16)
        _shifted_copies(uwin, shf, ts)
        for k in range(CONV_K):
            part = jnp.zeros((CONV_ROWS, dm), F32)
            for r0 in range(0, ts, CONV_ROWS):
                part = part + dwin[r0:r0 + CONV_ROWS, :] * _tap(uwin, shf, HALO - (CONV_K - 1) + k, r0)
            gcw_ref[k:k + 1, :] += jnp.sum(part, axis=0, keepdims=True)
        _shifted_copies(dwin, shf, ts)
        for r0 in range(0, ts, CONV_ROWS):
            d_u0 = jnp.zeros((CONV_ROWS, dm), F32)
            for k in range(CONV_K):
                d_u0 = d_u0 + cw_ref[k:k + 1, :] * _tap(dwin, shf, CONV_K - 1 - k, r0)
            rows = slice(r0, r0 + CONV_ROWS)
            sgb = _sig(p_ref[rows, dm:2 * dm])
            dp_ref[rows, 0:dm] = (d_u0 * sgb).astype(BF16)
            dp_ref[rows, dm:2 * dm] = (d_u0 * p_ref[rows, 0:dm] * sgb * (1.0 - sgb)).astype(BF16)
        dwin[ts:ts + HALO, :] = dwin[0:HALO, :]

    rev = lambda i: (nt - 1 - i, 0)
    row = pl.BlockSpec((1, dm), lambda i: (0, 0))
    tile = pl.BlockSpec((ts, dm), rev)
    return pl.pallas_call(
        body, name="conv_backward", grid=(nt,),
        in_specs=[ANY, tile, pl.BlockSpec((ts, d3), rev), tile,
                  pl.BlockSpec((HALO, dm), lambda i: (jnp.maximum((nt - 1 - i) * hb - 1, 0), 0)),
                  tile, pl.BlockSpec((HALO, dm), lambda i: (0, 0)), row, row,
                  pl.BlockSpec((dm, dm), lambda i: (0, 0))],
        out_specs=[pl.BlockSpec((ts, d3), rev), pl.BlockSpec((dm, dm), lambda i: (0, 0)),
                   pl.BlockSpec((8, dm), lambda i: (0, 0)), pl.BlockSpec((HALO, dm), lambda i: (0, 0))],
        out_shape=[jax.ShapeDtypeStruct(dp.shape, BF16), jax.ShapeDtypeStruct((dm, dm), F32),
                   jax.ShapeDtypeStruct((8, dm), F32), jax.ShapeDtypeStruct((HALO, dm), F32)],
        input_output_aliases={0: 0},
        scratch_shapes=[pltpu.VMEM((ts + HALO, dm), F32), pltpu.VMEM((ts + HALO, dm), F32),
                        pltpu.VMEM((7, ts + HALO - 8, dm), F32)],
        compiler_params=_params(("arbitrary",)),
    )(dp, dyc, p_conv, u0, u0, u1, conv_w, ln_g, ln_b, w_co)


def _att_masks():
    head0 = lax.broadcasted_iota(jnp.int32, (BLK, LANES), 1) < 64
    col = lax.broadcasted_iota(jnp.int32, (BLK, 4 * BLK), 1)
    row = lax.broadcasted_iota(jnp.int32, (BLK, 4 * BLK), 0)
    kj = col % BLK
    prev = jnp.where(col < 2 * BLK, 1, 0)
    band = jnp.where(col < 2 * BLK, kj - row, row - kj)
    return head0, band, prev


def _fill_block_diagonal(dst, slab, src_ref, halo_ref, head0, nq):
    sl = slice(slab * LANES, (slab + 1) * LANES)
    for b in range(nq + 1):
        blk = halo_ref[:, sl] if b == 0 else src_ref[(b - 1) * BLK:b * BLK, sl]
        base = (slab * (nq + 1) + b) * 2 * BLK
        zero = jnp.zeros_like(blk)
        dst[base:base + BLK, :] = jnp.where(head0, blk, zero)
        dst[base + BLK:base + 2 * BLK, :] = jnp.where(head0, zero, blk)


def _attention_forward(qkv, seq_len, qt, name):
    s_len = qkv.shape[1]
    nq = qt // BLK
    tiles_per_seq = seq_len // qt

    def body(q_ref, k_ref, v_ref, kh_ref, vh_ref, o_ref, lse_ref, kbd, vbd):
        first = jnp.where((pl.program_id(0) % tiles_per_seq) == 0, 4 * BLK, 0)
        head0, band, prev = _att_masks()
        band_first = band - prev * first
        for p in range(4):
            _fill_block_diagonal(kbd, p, k_ref, kh_ref, head0, nq)
            _fill_block_diagonal(vbd, p, v_ref, vh_ref, head0, nq)
        units = [(p, n) for p in range(4) for n in range(nq)]

        def keys_of(p, n):
            base = (p * (nq + 1) + n) * 2 * BLK
            return slice(base, base + 4 * BLK)

        def scores(p, n):
            q2 = q_ref[n * BLK:(n + 1) * BLK, p * LANES:(p + 1) * LANES] * 0.125
            return _dot_nt(q2, kbd[keys_of(p, n), :])

        def finish(p, n, o, den, lse):
            rows, sl = slice(n * BLK, (n + 1) * BLK), slice(p * LANES, (p + 1) * LANES)
            o_ref[rows, sl] = o / jnp.where(head0, den[0], den[1])
            lse_ref[rows, sl] = jnp.where(head0, lse[0], lse[1])

        s_next = scores(*units[0])
        pending = None
        for ui, (p, n) in enumerate(units):
            s = s_next
            if ui + 1 < len(units):
                s_next = scores(*units[ui + 1])
            s = jnp.where((band_first if n == 0 else band) >= 0, s, NEG)
            grp = [s[:, g * BLK:(g + 1) * BLK] for g in range(4)]
            ps, den, lse = [None] * 4, [], []
            for h in range(2):
                m = jnp.max(jnp.maximum(grp[h], grp[2 + h]), axis=-1, keepdims=True)
                ps[h], ps[2 + h] = jnp.exp(grp[h] - m), jnp.exp(grp[2 + h] - m)
                dn = jnp.sum(ps[h] + ps[2 + h], axis=-1, keepdims=True)
                den.append(dn)
                lse.append(m + jnp.log(dn))
            pmat = jnp.concatenate([x.astype(BF16) for x in ps], axis=1)
            o = _dot(pmat, vbd[keys_of(p, n), :])
            if pending is not None:
                finish(*pending)
            pending = (p, n, o, den, lse)
        finish(*pending)

    def which(w):
        return pl.BlockSpec((None, qt, ATT_W), lambda i: (w, i, 0))

    def halo(w):
        return pl.BlockSpec((None, BLK, ATT_W), lambda i: (w, jnp.maximum(i * nq - 1, 0), 0))

    out = pl.BlockSpec((qt, ATT_W), lambda i: (i, 0))
    bd = pltpu.VMEM((4 * (nq + 1) * 2 * BLK, LANES), BF16)
    return pl.pallas_call(
        body, name=name, grid=(s_len // qt,),
        in_specs=[which(0), which(1), which(2), halo(1), halo(2)],
        out_specs=[out, out], out_shape=[jax.ShapeDtypeStruct((s_len, ATT_W), F32)] * 2,
        scratch_shapes=[bd, bd],
        compiler_params=_params(("parallel",)),
    )(qkv, qkv, qkv, qkv, qkv)


def _attention_backward(qkv, d_att, lse, delta, seq_len, qt, name):
    s_len = qkv.shape[1]
    nq = qt // BLK
    tiles_per_seq = seq_len // qt
    nblk = s_len // BLK

    def body(q_ref, k_ref, v_ref, kh_ref, vh_ref, do_ref, lse_ref, dl_ref,
             qn_ref, don_ref, lsen_ref, dln_ref, dqkv_ref, kbd, vbd):
        i = pl.program_id(0)
        first = jnp.where((i % tiles_per_seq) == 0, 4 * BLK, 0)
        last = jnp.where((i % tiles_per_seq) == tiles_per_seq - 1, 4 * BLK, 0)
        head0, band, prev = _att_masks()
        band_first = band - prev * first
        band_tail = band[:, 0:2 * BLK] - last
        for p in range(4):
            _fill_block_diagonal(kbd, p, k_ref, kh_ref, head0, nq)
            _fill_block_diagonal(vbd, p, v_ref, vh_ref, head0, nq)
        units = [(p, n) for p in range(4) for n in range(nq + 1)]

        def stage_a(p, n):
            sl = slice(p * LANES, (p + 1) * LANES)
            base = (p * (nq + 1) + n) * 2 * BLK
            if n < nq:
                rows = slice(n * BLK, (n + 1) * BLK)
                q2, do2, lse2, dl2 = q_ref[rows, sl], do_ref[rows, sl], lse_ref[rows, sl], dl_ref[rows, sl]
                keys = slice(base, base + 4 * BLK)
            else:
                q2, do2, lse2, dl2 = qn_ref[:, sl], don_ref[:, sl], lsen_ref[:, sl], dln_ref[:, sl]
                keys = slice(base, base + 2 * BLK)
            s = _dot_nt(q2 * 0.125, kbd[keys, :])
            dp = _dot_nt(do2, vbd[keys, :])
            return q2, do2, lse2, dl2, keys, s, dp

        def stage_b(n, lse2, dl2, s, dp):
            mask = band_tail if n == nq else (band_first if n == 0 else band)
            ps, dss = [], []
            for g in range(s.shape[1] // BLK):
                h = g % 2
                cols = slice(g * BLK, (g + 1) * BLK)
                pg = jnp.exp(jnp.where(mask[:, cols] >= 0, s[:, cols] - lse2[:, h * 64:h * 64 + 1], NEG))
                ps.append(pg.astype(BF16))
                dss.append((pg * (dp[:, cols] - dl2[:, h * 64:h * 64 + 1]) * 0.125).astype(BF16))
            return jnp.concatenate(ps, axis=1), jnp.concatenate(dss, axis=1)

        def heads(r, g):
            return jnp.where(head0, r[g * BLK:(g + 1) * BLK, :], r[(g + 1) * BLK:(g + 2) * BLK, :])

        a_next = stage_a(*units[0])
        carry = None
        for ui, (p, n) in enumerate(units):
            q2, do2, lse2, dl2, keys, s, dp = a_next
            if ui + 1 < len(units):
                a_next = stage_a(*units[ui + 1])
            pmat, dsmat = stage_b(n, lse2, dl2, s, dp)
            sl = slice(p * LANES, (p + 1) * LANES)
            if n < nq:
                dqkv_ref[0, n * BLK:(n + 1) * BLK, sl] = _dot(dsmat, kbd[keys, :])
            dkbd = _dot_tn(dsmat, q2)
            dvbd = _dot_tn(pmat, do2)
            if n > 0:
                prow = slice((n - 1) * BLK, n * BLK)
                dqkv_ref[1, prow, sl] = carry[0] + heads(dkbd, 0)
                dqkv_ref[2, prow, sl] = carry[1] + heads(dvbd, 0)
            if n < nq:
                carry = (heads(dkbd, 2), heads(dvbd, 2))

    def which(w):
        return pl.BlockSpec((None, qt, ATT_W), lambda i: (w, i, 0))

    def prev(w):
        return pl.BlockSpec((None, BLK, ATT_W), lambda i: (w, jnp.maximum(i * nq - 1, 0), 0))

    tile = pl.BlockSpec((qt, ATT_W), lambda i: (i, 0))
    nxt = pl.BlockSpec((BLK, ATT_W), lambda i: (jnp.minimum((i + 1) * nq, nblk - 1), 0))
    nxt_q = pl.BlockSpec((None, BLK, ATT_W), lambda i: (0, jnp.minimum((i + 1) * nq, nblk - 1), 0))
    return pl.pallas_call(
        body, name=name, grid=(s_len // qt,),
        in_specs=[which(0), which(1), which(2), prev(1), prev(2), tile, tile, tile, nxt_q, nxt, nxt, nxt],
        out_specs=pl.BlockSpec((3, qt, ATT_W), lambda i: (0, i, 0)),
        out_shape=jax.ShapeDtypeStruct((3, s_len, ATT_W), F32),
        scratch_shapes=[pltpu.VMEM((4 * (nq + 1) * 2 * BLK, LANES), BF16)] * 2,
        compiler_params=_params(("parallel",)),
    )(qkv, qkv, qkv, qkv, qkv, d_att, lse, delta, qkv, d_att, lse, delta)


def _merge_and_head(o_g, lse_g, p_gate, y_conv, x, tgt, modv, final_g, w_ao, w_o, ts):
    s_len, dm = x.shape
    gw = ATT_W + 2 * dm
    nt = s_len // ts
    gate_off = 3 * dm + 9 * ATT_W
    assert gate_off % gw == 0

    def body(o0, o1, o2, l0, l1, l2, pg_ref, yc_ref, x_ref, t_ref, mod_ref, fg_ref, wao_ref, wo_ref,
             loss_ref, dx_ref, dyc_ref, dpg_ref, da0, da1, da2, ls0, ls1, ls2, de0, de1, de2,
             gwo_ref, gwao_ref, gs_ref, slab):
        i = pl.program_id(0)

        @pl.when(i == 0)
        def _():
            loss_ref[...] = jnp.zeros_like(loss_ref)
            gwo_ref[...] = jnp.zeros_like(gwo_ref)
            gwao_ref[...] = jnp.zeros_like(gwao_ref)
            gs_ref[...] = jnp.zeros_like(gs_ref)

        os_, ls_ = [], []
        for dil, o_ref, l_ref in zip(DILATIONS, (o0, o1, o2), (l0, l1, l2)):
            os_.append(jnp.concatenate(_interleave_load(o_ref, slab, dil, ts), axis=1))
            ls_.append(jnp.concatenate(_interleave_load(l_ref, slab, dil, ts), axis=1))
        mx = jnp.maximum(jnp.maximum(ls_[0], ls_[1]), ls_[2])
        wts = [jnp.exp(l - mx) for l in ls_]
        wsum = wts[0] + wts[1] + wts[2]
        att = (wts[0] * os_[0] + wts[1] * os_[1] + wts[2] * os_[2]) / wsum
        lse_all = mx + jnp.log(wsum)

        z_att, g_conv, g_att = pg_ref[:, 0:ATT_W], pg_ref[:, ATT_W:ATT_W + dm], pg_ref[:, ATT_W + dm:gw]
        sgz = _sig(z_att)
        sz = z_att * sgz
        a_att = (att * sz).astype(BF16)
        y_att = _dot(a_att, wao_ref[...])
        y_conv = yc_ref[...]
        sgc, sga = _sig(g_conv), _sig(g_att)
        merged = (sgc * y_conv + sga * y_att).astype(BF16)
        mo = _dot(merged, wo_ref[...])
        gate = mod_ref[2:3, :]
        x2 = x_ref[...] + gate * mo
        r = lax.rsqrt(jnp.mean(x2 * x2, axis=-1, keepdims=True) + EPS)
        xr = x2 * r
        err = xr * fg_ref[...] - t_ref[...]
        loss_ref[...] += 0.5 * jnp.sum(jnp.mean(err * err, axis=-1, keepdims=True))
        dy = err * (1.0 / dm)
        gs_ref[0:1, :] += jnp.sum(dy * xr, axis=0, keepdims=True)
        dyg = dy * fg_ref[...]
        d_x2 = r * dyg - xr * (r * jnp.mean(dyg * xr, axis=-1, keepdims=True))
        dx_ref[...] = d_x2
        gs_ref[1:2, :] += jnp.sum(d_x2 * mo, axis=0, keepdims=True)
        d_mo = (d_x2 * gate).astype(BF16)
        d_mg = _dot_nt(d_mo, wo_ref[...])
        gwo_ref[...] += _dot_tn(merged, d_mo)
        dyc_ref[...] = (d_mg * sgc).astype(BF16)
        dpg_ref[:, ATT_W:ATT_W + dm] = (d_mg * y_conv * sgc * (1.0 - sgc)).astype(BF16)
        d_ya = (d_mg * sga).astype(BF16)
        dpg_ref[:, ATT_W + dm:gw] = (d_mg * y_att * sga * (1.0 - sga)).astype(BF16)
        gwao_ref[...] += _dot_tn(a_att, d_ya)
        d_aa = _dot_nt(d_ya, wao_ref[...])
        dpg_ref[:, 0:ATT_W] = (d_aa * att * _dsilu(z_att, sgz)).astype(BF16)
        d_att = d_aa * sz
        ri = lax.broadcasted_iota(jnp.int32, (ATT_W, ATT_W), 0) // 64
        ci = lax.broadcasted_iota(jnp.int32, (ATT_W, ATT_W), 1) // 64
        ones = jnp.where(ri == ci, 1.0, 0.0).astype(BF16)
        prod = d_att * att
        hi = prod.astype(BF16)
        lo = (prod - hi.astype(F32)).astype(BF16)
        delta = _dot(hi, ones) + _dot(lo, ones)
        for val, refs, dt in ((d_att, (da0, da1, da2), BF16), (lse_all, (ls0, ls1, ls2), F32),
                              (delta, (de0, de1, de2), F32)):
            vals = [val[:, s * LANES:(s + 1) * LANES] for s in range(4)]
            for dil, ref in zip(DILATIONS, refs):
                _deinterleave_store(vals, slab, ref, dil, ts, dt)

    def grp(dil):
        return pl.BlockSpec((dil, ts // dil, ATT_W), lambda i: (0, i, 0))

    tile = pl.BlockSpec((ts, dm), lambda i: (i, 0))
    gate_tile = pl.BlockSpec((ts, gw), lambda i: (i, 0))
    const = lambda shp: pl.BlockSpec(shp, lambda i: tuple(0 for _ in shp))
    grp_shape = lambda dt: [jax.ShapeDtypeStruct((dil, s_len // dil, ATT_W), dt) for dil in DILATIONS]
    return pl.pallas_call(
        body, name="merge_and_head", grid=(nt,),
        in_specs=[grp(d) for d in DILATIONS] * 2
        + [gate_tile, tile, tile, tile, const((8, dm)), const((1, dm)), const((ATT_W, dm)), const((dm, dm))],
        out_specs=[const((8, LANES)), tile, tile, pl.BlockSpec((ts, gw), lambda i: (i, gate_off // gw))]
        + [grp(d) for d in DILATIONS] * 3
        + [const((dm, dm)), const((ATT_W, dm)), const((8, dm))],
        out_shape=[jax.ShapeDtypeStruct((8, LANES), F32), jax.ShapeDtypeStruct((s_len, dm), F32),
                   jax.ShapeDtypeStruct((s_len, dm), BF16), jax.ShapeDtypeStruct((s_len, gate_off + gw), BF16)]
        + grp_shape(BF16) + grp_shape(F32) + grp_shape(F32)
        + [jax.ShapeDtypeStruct((dm, dm), F32), jax.ShapeDtypeStruct((ATT_W, dm), F32),
           jax.ShapeDtypeStruct((8, dm), F32)],
        scratch_shapes=[pltpu.VMEM((4, ts, LANES), F32)],
        compiler_params=_params(("arbitrary",)),
    )(*o_g, *lse_g, p_gate, y_conv, x, tgt, modv, final_g, w_ao, w_o)


def _qkv_grad_to_tokens(dp, dqkv_g, pos, ropetab, ts):
    s_len, dm = pos.shape[0], (dp.shape[1] - 10 * ATT_W) // 5
    qw = 3 * ATT_W
    assert (3 * dm) % qw == 0

    def body(dp_in, g0, g1, g2, pos_ref, tab_ref, o_ref, slab):
        del dp_in
        roped = pl.program_id(1) < 2
        rc, ra, rb = _rope_tables(pos_ref, tab_ref)
        for gi, (dil, g_ref) in enumerate(zip(DILATIONS, (g0, g1, g2))):
            vals = _interleave_load(g_ref, slab, dil, ts)
            for s in range(4):
                t = vals[s]
                rot = t * rc + pltpu.roll(t * ra, 8, 1) + pltpu.roll(t * rb, LANES - 8, 1)
                col = gi * ATT_W + s * LANES
                o_ref[:, col:col + LANES] = jnp.where(roped, rot, t).astype(BF16)

    return pl.pallas_call(
        body, name="qkv_grad_to_tokens", grid=(s_len // ts, 3),
        in_specs=[ANY] + [pl.BlockSpec((None, dil, ts // dil, ATT_W), lambda i, w: (w, 0, i, 0)) for dil in DILATIONS]
        + [pl.BlockSpec((ts, 1), lambda i, w: (i, 0)), pl.BlockSpec((8, LANES), lambda i, w: (0, 0))],
        out_specs=pl.BlockSpec((ts, qw), lambda i, w: (i, 3 * dm // qw + w)),
        out_shape=jax.ShapeDtypeStruct(dp.shape, BF16),
        input_output_aliases={0: 0},
        scratch_shapes=[pltpu.VMEM((4, ts, LANES), F32)],
        compiler_params=_params(("parallel", "arbitrary")),
    )(dp, *dqkv_g, pos, ropetab)


def _wide_col_tile(cols):
    return 2 * COL_TILE if cols % (2 * COL_TILE) == 0 else COL_TILE


def _input_grad(dp, w_in, x, dx_res, modv, norm_g, ts):
    s_len, dm = x.shape
    ct = _wide_col_tile(dp.shape[1])
    nct = dp.shape[1] // ct

    def body(p_ref, w_ref, x_ref, dxr_ref, mod_ref, g_ref, gx_ref, gs_ref, acc):
        i, j = pl.program_id(0), pl.program_id(1)

        @pl.when((i == 0) & (j == 0))
        def _():
            gs_ref[...] = jnp.zeros_like(gs_ref)

        @pl.when(j == 0)
        def _():
            acc[...] = jnp.zeros_like(acc)

        acc[...] += _dot_nt(p_ref[...], w_ref[...])

        @pl.when(j == nct - 1)
        def _():
            d_h = acc[...]
            xv = x_ref[...]
            r = lax.rsqrt(jnp.mean(xv * xv, axis=-1, keepdims=True) + EPS)
            xr = xv * r
            gs_ref[0:1, :] += jnp.sum(d_h, axis=0, keepdims=True)
            gs_ref[1:2, :] += jnp.sum(d_h * (xr * g_ref[...]), axis=0, keepdims=True)
            d_n = d_h * (1.0 + mod_ref[1:2, :])
            gs_ref[2:3, :] += jnp.sum(d_n * xr, axis=0, keepdims=True)
            dxn = d_n * g_ref[...]
            gx_ref[...] = dxr_ref[...] + r * dxn - xr * (r * jnp.mean(dxn * xr, axis=-1, keepdims=True))

    tile = pl.BlockSpec((ts, dm), lambda i, j: (i, 0))
    return pl.pallas_call(
        body, name="input_grad", grid=(s_len // ts, nct),
        in_specs=[pl.BlockSpec((ts, ct), lambda i, j: (i, j)), pl.BlockSpec((dm, ct), lambda i, j: (0, j)), tile, tile,
                  pl.BlockSpec((8, dm), lambda i, j: (0, 0)), pl.BlockSpec((1, dm), lambda i, j: (0, 0))],
        out_specs=[tile, pl.BlockSpec((8, dm), lambda i, j: (0, 0))],
        out_shape=[jax.ShapeDtypeStruct((s_len, dm), F32), jax.ShapeDtypeStruct((8, dm), F32)],
        scratch_shapes=[pltpu.VMEM((ts, dm), F32)],
        compiler_params=_params(("arbitrary", "arbitrary")),
    )(dp, w_in, x, dx_res, modv, norm_g)


def _w_in_grad(h_t, dp, ts):
    dm, s_len = h_t.shape
    ct = _wide_col_tile(dp.shape[1])

    def body(h_ref, p_ref, o_ref):
        @pl.when(pl.program_id(1) == 0)
        def _():
            o_ref[...] = jnp.zeros_like(o_ref)

        o_ref[...] += _dot(h_ref[...], p_ref[...])

    return pl.pallas_call(
        body, name="w_in_grad", grid=(dp.shape[1] // ct, s_len // ts),
        in_specs=[pl.BlockSpec((dm, ts), lambda j, i: (0, i)), pl.BlockSpec((ts, ct), lambda j, i: (i, j))],
        out_specs=pl.BlockSpec((dm, ct), lambda j, i: (0, j)),
        out_shape=jax.ShapeDtypeStruct((dm, dp.shape[1]), F32),
        compiler_params=_params(("arbitrary", "arbitrary")),
    )(h_t, dp)


def _rope_lane_table():
    l64 = jnp.arange(LANES) % 64
    half = ROT_DIM // 2
    inv_freq = ROPE_THETA ** (-(jnp.arange(half, dtype=F32) * 2.0 / ROT_DIM))
    rot = l64 < ROT_DIM
    rows = [jnp.where(rot, inv_freq[l64 % half], 0.0), (l64 < half).astype(F32),
            ((l64 >= half) & rot).astype(F32), rot.astype(F32)]
    return jnp.concatenate([jnp.stack(rows), jnp.zeros((4, LANES), F32)], axis=0)


def _tile_sizes(s_len):
    l_min = s_len // DILATIONS[-1]
    ts_big = min(1024, s_len // 2)
    ts_mid = 256
    ts_head = 256
    qt = min(512, l_min)
    return ts_big, ts_mid, ts_head, qt


def kernel(x, c, positions, norm_g, w_ada, b_ada, w_in, conv_w, conv_b, conv_ln_g, conv_ln_b, w_conv_out, w_att_out, w_o, final_g, loss_target, m_norm_g, m_w_ada, m_b_ada, m_w_in, m_conv_w, m_conv_b, m_conv_ln_g, m_conv_ln_b, m_w_conv_out, m_w_att_out, m_w_o, m_final_g, v_norm_g, v_w_ada, v_b_ada, v_w_in, v_conv_w, v_conv_b, v_conv_ln_g, v_conv_ln_b, v_w_conv_out, v_w_att_out, v_w_o, v_final_g):
    s_len, dm = x.shape[1], x.shape[2]
    ts_big, ts_mid, ts_head, qt = _tile_sizes(s_len)
    xi, yi, cidx = _my_place()
    chip = 2 * xi + yi
    batch = 4 * xi + 2 * yi + cidx
    x2d, tgt = x[0], loss_target[0]
    pos = positions.reshape(s_len, 1)
    wa_l, wi_l, cw_l = w_ada[0], w_in[0], conv_w[0]
    wco_l, wao_l, wo_l = w_conv_out[0], w_att_out[0], w_o[0]
    ada_w = wa_l.shape[1]
    cw_cols = cw_l.shape[1]

    cw_pad = jnp.pad(cw_l, ((0, HALO - CONV_K), (0, 0)))
    small_in = jnp.concatenate([jnp.broadcast_to(c, (8, dm)), cw_pad.reshape(8, dm)], axis=0)
    small = _allgather_small(small_in).reshape(8, 16, dm)
    c_all = small[:, 0, :]
    conv_w_full = jnp.concatenate(
        [small[2 * p, 8:16, :].reshape(HALO, cw_cols) for p in range(4)], axis=1)
    b_l = lax.dynamic_slice(b_ada, (0, chip * ada_w), (1, ada_w))
    mod_parts = _allgather_small(_mod_part(c_all, wa_l, b_l)).reshape(8, 8, ada_w)
    mod_rows = lax.dynamic_index_in_dim(mod_parts, batch, axis=1, keepdims=False)
    mod = jnp.concatenate([mod_rows[2 * p] for p in range(4)], axis=0).reshape(3, dm)
    modv = jnp.concatenate([mod, jnp.zeros((5, dm), F32)], axis=0)

    kinds = ("col", "row", "col", "row")
    w_in_b, w_co_b, w_ao_b, w_o_b = _allgather_weights([wi_l, wco_l, wao_l, wo_l], kinds)

    ropetab = _rope_lane_table()
    p_conv, qkv0, qkv1, qkv2, p_gate, h_t = _input_projection(x2d, pos, modv, norm_g, ropetab, w_in_b, ts_big)
    y_conv, u0, u1 = _conv_forward(p_conv, conv_w_full, conv_b, conv_ln_g, conv_ln_b, w_co_b, ts_mid)
    qkv_flat = [q.reshape(3, s_len, ATT_W) for q in (qkv0, qkv1, qkv2)]
    o_g, lse_g = [], []
    for gi, dil in enumerate(DILATIONS):
        o, l = _attention_forward(qkv_flat[gi], s_len // dil, qt, "attention_forward_%d" % dil)
        o_g.append(o.reshape(dil, s_len // dil, ATT_W))
        lse_g.append(l.reshape(dil, s_len // dil, ATT_W))

    (loss_p, dx_res, dyc, dp, da0, da1, da2, ls0, ls1, ls2, de0, de1, de2,
     g_wo, g_wao, head_sums) = _merge_and_head(o_g, lse_g, p_gate, y_conv, x2d, tgt, modv,
                                               final_g.reshape(1, dm), w_ao_b, w_o_b, ts_head)
    loss = lax.psum(loss_p[0, 0], ("x", "y", "c"))

    dp, g_wco, conv_sums, g_cw = _conv_backward(dp, dyc, p_conv, u0, u1, conv_w_full, conv_ln_g, conv_ln_b,
                                                w_co_b, ts_mid)
    dqkv_g = []
    for gi, (dil, da, ls, de) in enumerate(zip(DILATIONS, (da0, da1, da2), (ls0, ls1, ls2), (de0, de1, de2))):
        flat = lambda a: a.reshape(s_len, ATT_W)
        dqkv = _attention_backward(qkv_flat[gi], flat(da), flat(ls), flat(de), s_len // dil, qt,
                                   "attention_backward_%d" % dil)
        dqkv_g.append(dqkv.reshape(3, dil, s_len // dil, ATT_W))
    dp = _qkv_grad_to_tokens(dp, dqkv_g, pos, ropetab, ts_mid)
    grad_x, in_sums = _input_grad(dp, w_in_b, x2d, dx_res, modv, norm_g, ts_big)
    g_win = _w_in_grad(h_t, dp, ts_big)

    grads = [g_win, g_wco, g_wao, g_wo]
    full_shapes = [g.shape for g in grads]
    c_arr = jnp.reshape(cidx, (1,)).astype(jnp.int32)
    recv_halves = _reduce_pair_exchange(grads, kinds)
    halves = [_reduce_pair_sum(g, pa, kind, c_arr, "reduce_pair_sum_%d" % k)
              for k, (g, pa, kind) in enumerate(zip(grads, recv_halves, kinds))]
    recvd = _reduce_to_owner(halves, kinds, full_shapes)
    gr_win, gr_wco, gr_wao, gr_wo = _reduce_finish(halves, recvd, kinds, full_shapes)

    rows = [in_sums[2:3], conv_sums[2:3], conv_sums[0:1], conv_sums[1:2], head_sums[0:1],
            in_sums[0:1], in_sums[1:2], head_sums[1:2], g_cw, jnp.zeros((8, dm), F32)]
    part = jnp.concatenate(rows, axis=0)
    gathered = _allgather_small(part)
    tot = _sum_devices(gathered, 48)
    dmod_all = gathered.reshape(8, 48, dm)[:, 5:8, :].reshape(8, 3 * dm)
    dmod_l = lax.dynamic_slice(dmod_all, (0, chip * ada_w), (8, ada_w))
    gr_wada = _grad_w_ada(c_all.T, dmod_l)
    gr_cw = lax.dynamic_slice(tot[8:8 + HALO], (0, chip * cw_cols), (HALO, cw_cols))

    def pack8(ng, cb, lg, lb, fg, ba):
        return jnp.concatenate([ng, cb, lg, lb, fg.reshape(1, dm), ba.reshape(3, dm)], axis=0)

    pad_cw = lambda a: jnp.pad(a[0], ((0, HALO - CONV_K), (0, 0)))
    w8 = pack8(norm_g, conv_b, conv_ln_g, conv_ln_b, final_g, b_ada)
    m8 = pack8(m_norm_g, m_conv_b, m_conv_ln_g, m_conv_ln_b, m_final_g, m_b_ada)
    v8 = pack8(v_norm_g, v_conv_b, v_conv_ln_g, v_conv_ln_b, v_final_g, v_b_ada)
    g8 = tot[0:8]
    upd = {
        "small": _adamw(w8, g8, m8, v8, "adamw_small"),
        "w_ada": _adamw(wa_l, gr_wada, m_w_ada[0], v_w_ada[0], "adamw_w_ada"),
        "w_in": _adamw(wi_l, gr_win, m_w_in[0], v_w_in[0], "adamw_w_in"),
        "conv_w": _adamw(cw_pad, gr_cw, pad_cw(m_conv_w), pad_cw(v_conv_w), "adamw_conv_w"),
        "w_co": _adamw(wco_l, gr_wco, m_w_conv_out[0], v_w_conv_out[0], "adamw_w_conv_out"),
        "w_ao": _adamw(wao_l, gr_wao, m_w_att_out[0], v_w_att_out[0], "adamw_w_att_out"),
        "w_o": _adamw(wo_l, gr_wo, m_w_o[0], v_w_o[0], "adamw_w_o"),
    }

    def family(which):
        if which is None:
            sm = g8
            big = {"w_ada": gr_wada, "w_in": gr_win, "conv_w": gr_cw, "w_co": gr_wco, "w_ao": gr_wao, "w_o": gr_wo}
        else:
            sm = upd["small"][which]
            big = {k: upd[k][which] for k in ("w_ada", "w_in", "conv_w", "w_co", "w_ao", "w_o")}
        return [sm[0:1], big["w_ada"][None], sm[5:8].reshape(1, 3 * dm), big["w_in"][None],
                big["conv_w"][None, :CONV_K], sm[1:2], sm[2:3], sm[3:4], big["w_co"][None],
                big["w_ao"][None], big["w_o"][None], sm[4]]

    return (loss, grad_x[None], *family(None), *family(0), *family(1), *family(2))
```

```python
import jax
import jax.numpy as jnp
from jax import lax
from jax.experimental import pallas as pl
from jax.experimental.pallas import tpu as pltpu

F32 = jnp.float32
BF16 = jnp.bfloat16
MESH = pl.DeviceIdType.MESH
ANY = pl.BlockSpec(memory_space=pl.ANY)
VM = pl.BlockSpec(memory_space=pltpu.VMEM)

EPS = 1e-6
NEG = -1e30
ATT_W = 512
DILATIONS = (1, 4, 16)
BLK = 128
CONV_K = 31
HALO = 32
CONV_ROWS_FWD = 32
CONV_ROWS_BWD = 16
ROT_DIM = 16
ROPE_THETA = 500000.0
COL_TILE = 512
LANES = 128
VMEM_LIMIT = 56 * 1024 * 1024

ADAM_LR, ADAM_B1, ADAM_B2, ADAM_EPS, ADAM_WD, ADAM_STEP = 0.001, 0.9, 0.999, 1e-08, 0.01, 10


def _params(sem=None, vmem=VMEM_LIMIT):
    return pltpu.CompilerParams(dimension_semantics=sem, vmem_limit_bytes=vmem)


def _dot(a, b):
    return jnp.dot(a, b, preferred_element_type=F32)


def _dot_nt(a, b):
    return lax.dot_general(a, b, (((1,), (1,)), ((), ())), preferred_element_type=F32)


def _dot_tn(a, b):
    return lax.dot_general(a, b, (((0,), (0,)), ((), ())), preferred_element_type=F32)


def _sig(x):
    return jax.nn.sigmoid(x)


def _dsilu(x, s):
    return s * (1.0 + x * (1.0 - s))


def _my_place():
    return lax.axis_index("x"), lax.axis_index("y"), lax.axis_index("c")


def _allgather_small(x_shard):
    m_per, n = x_shard.shape

    def body(x_ref, out_ref, send_sems, recv_sems, local_sem):
        x, y, c = _my_place()
        me, sibling = (x, y, c), (x, y, 1 - c)
        chips = [(1 - x, y), (x, 1 - y), (1 - x, 1 - y)]

        def rows(px, py, pc):
            return out_ref.at[pl.ds((4 * px + 2 * py + pc) * m_per, m_per), :]

        def copy(k, block, to, src=None):
            return pltpu.make_async_remote_copy(
                src_ref=rows(*block) if src is None else src, dst_ref=rows(*block),
                send_sem=send_sems.at[k], recv_sem=recv_sems.at[k],
                device_id=to, device_id_type=MESH)

        mine = pltpu.make_async_copy(x_ref, rows(*me), local_sem)
        mine.start()
        first = [copy(0, me, sibling, src=x_ref)]
        first += [copy(1 + j, me, (*chip, c), src=x_ref) for j, chip in enumerate(chips)]
        for cp in first:
            cp.start()
        passed = [copy(4 + j, (*chip, c), sibling) for j, chip in enumerate(chips)]
        for j, chip in enumerate(chips):
            copy(1 + j, (*chip, c), me).wait_recv()
            passed[j].start()
        copy(0, sibling, me).wait_recv()
        for j, chip in enumerate(chips):
            copy(4 + j, (*chip, 1 - c), me).wait_recv()
        for cp in first + passed:
            cp.wait_send()
        mine.wait()

    return pl.pallas_call(
        body, name="allgather_small",
        out_shape=jax.ShapeDtypeStruct((8 * m_per, n), x_shard.dtype),
        in_specs=[VM], out_specs=VM,
        scratch_shapes=[pltpu.SemaphoreType.DMA((7,)), pltpu.SemaphoreType.DMA((7,)),
                        pltpu.SemaphoreType.DMA],
    )(x_shard)


def _shard_window(ref, kind, p, n_shards=4):
    r, c = ref.shape
    if kind == "col":
        w = c // n_shards
        return ref.at[:, pl.ds(p * w, w)]
    w = r // n_shards
    return ref.at[pl.ds(p * w, w), :]


def _half_window(ref, kind, hc):
    r, c = ref.shape
    if kind == "col":
        return ref.at[pl.ds(hc * (r // 2), r // 2), :]
    return ref.at[:, pl.ds(hc * (c // 2), c // 2)]


def _allgather_weights(shards, kinds):
    n = len(shards)
    full_shapes = []
    for s, kind in zip(shards, kinds):
        r, c = s.shape
        full_shapes.append((r, 4 * c) if kind == "col" else (4 * r, c))

    def body(*refs):
        w_refs, out_refs = refs[:n], refs[n:2 * n]
        bf_refs = refs[2 * n:3 * n]
        send_sems, recv_sems, local_sems = refs[3 * n:]
        x, y, c = _my_place()
        p = 2 * x + y
        sibling = (x, y, 1 - c)
        chips = [(1 - x, y), (x, 1 - y), (1 - x, 1 - y)]
        locals_ = []
        for k in range(n):
            bf_refs[k][...] = w_refs[k][...].astype(BF16)
            cp = pltpu.make_async_copy(bf_refs[k], _shard_window(out_refs[k], kinds[k], p), local_sems.at[k])
            cp.start()
            locals_.append(cp)

        def landed(k, chip, hc):
            return _half_window(_shard_window(out_refs[k], kinds[k], 2 * chip[0] + chip[1]), kinds[k], hc)

        def ici(j, k, chip_src, to, src=None):
            return pltpu.make_async_remote_copy(
                src_ref=landed(k, chip_src, c) if src is None else src, dst_ref=landed(k, chip_src, c),
                send_sem=send_sems.at[j * n + k], recv_sem=recv_sems.at[j * n + k],
                device_id=to, device_id_type=MESH)

        first = []
        for j, chip in enumerate(chips):
            for k in range(n):
                cp = ici(j, k, (x, y), (*chip, c), src=_half_window(bf_refs[k], kinds[k], c))
                cp.start()
                first.append(cp)
        passed = []
        for j, chip in enumerate(chips):
            for k in range(n):
                ici(j, k, chip, (x, y, c)).wait_recv()
                cp = pltpu.make_async_remote_copy(
                    src_ref=landed(k, chip, c), dst_ref=landed(k, chip, c),
                    send_sem=send_sems.at[(3 + j) * n + k], recv_sem=recv_sems.at[(3 + j) * n + k],
                    device_id=sibling, device_id_type=MESH)
                cp.start()
                passed.append(cp)
        for j, chip in enumerate(chips):
            for k in range(n):
                pltpu.make_async_remote_copy(
                    src_ref=landed(k, chip, 1 - c), dst_ref=landed(k, chip, 1 - c),
                    send_sem=send_sems.at[(3 + j) * n + k], recv_sem=recv_sems.at[(3 + j) * n + k],
                    device_id=sibling, device_id_type=MESH).wait_recv()
        for cp in first + passed:
            cp.wait_send()
        for cp in locals_:
            cp.wait()

    return pl.pallas_call(
        body, name="allgather_weights",
        out_shape=[jax.ShapeDtypeStruct(s, BF16) for s in full_shapes],
        in_specs=[VM] * n, out_specs=[ANY] * n,
        scratch_shapes=[pltpu.VMEM(s.shape, BF16) for s in shards]
        + [pltpu.SemaphoreType.DMA((6 * n,)), pltpu.SemaphoreType.DMA((6 * n,)), pltpu.SemaphoreType.DMA((n,))],
        compiler_params=_params(),
    )(*shards)


def _reduce_pair_exchange(grads, kinds):
    n = len(grads)
    half_shapes = [(g.shape[0] // 2, g.shape[1]) if kind == "col" else (g.shape[0], g.shape[1] // 2)
                   for g, kind in zip(grads, kinds)]

    def body(*refs):
        g_refs, pa_refs = refs[:n], refs[n:2 * n]
        send_sems, recv_sems = refs[2 * n:]
        x, y, c = _my_place()
        cps = []
        for k in range(n):
            cp = pltpu.make_async_remote_copy(
                src_ref=_half_window(g_refs[k], kinds[k], 1 - c), dst_ref=pa_refs[k],
                send_sem=send_sems.at[k], recv_sem=recv_sems.at[k],
                device_id=(x, y, 1 - c), device_id_type=MESH)
            cp.start()
            cps.append(cp)
        for cp in cps:
            cp.wait()

    return pl.pallas_call(
        body, name="reduce_pair_exchange",
        out_shape=[jax.ShapeDtypeStruct(s, F32) for s in half_shapes],
        in_specs=[ANY] * n, out_specs=[ANY] * n,
        scratch_shapes=[pltpu.SemaphoreType.DMA((n,)), pltpu.SemaphoreType.DMA((n,))],
    )(*grads)


def _row_tile(rows, cols, itemsize=4, target=2 * 1024 * 1024, mult=16):
    t = rows
    while t % 2 == 0 and t // 2 >= mult and (t // 2) % mult == 0 and t * cols * itemsize > target:
        t //= 2
    return t


def _reduce_pair_sum(g, pa, kind, c_arr, name):
    hr, hc_ = pa.shape
    tr = _row_tile(hr, hc_)
    nb = hr // tr

    def body(c_ref, g_ref, pa_ref, o_ref):
        o_ref[...] = (g_ref[...] + pa_ref[...]).astype(BF16)

    if kind == "col":
        g_map = lambda i, c_ref: (c_ref[0] * nb + i, 0)
    else:
        g_map = lambda i, c_ref: (i, c_ref[0])
    return pl.pallas_call(
        body, name=name,
        grid_spec=pltpu.PrefetchScalarGridSpec(
            num_scalar_prefetch=1, grid=(nb,),
            in_specs=[pl.BlockSpec((tr, hc_), g_map), pl.BlockSpec((tr, hc_), lambda i, c_ref: (i, 0))],
            out_specs=pl.BlockSpec((tr, hc_), lambda i, c_ref: (i, 0))),
        out_shape=jax.ShapeDtypeStruct((hr, hc_), BF16),
        compiler_params=_params(("parallel",)),
    )(c_arr, g, pa)


def _half_shard_shape(full_shape, kind):
    r, c = full_shape
    return (r // 2, c // 4) if kind == "col" else (r // 4, c // 2)


def _reduce_to_owner(halves, kinds, full_shapes):
    n = len(halves)
    hs = [_half_shard_shape(fs, kind) for fs, kind in zip(full_shapes, kinds)]

    def body(*refs):
        h_refs, rc_refs = refs[:n], refs[n:2 * n]
        send_sems, recv_sems = refs[2 * n:]
        x, y, c = _my_place()
        chips = [(1 - x, y), (x, 1 - y), (1 - x, 1 - y)]
        cps = []
        for j, chip in enumerate(chips):
            pj = 2 * chip[0] + chip[1]
            for k in range(n):
                cp = pltpu.make_async_remote_copy(
                    src_ref=_shard_window(h_refs[k], kinds[k], pj), dst_ref=rc_refs[k].at[j],
                    send_sem=send_sems.at[j * n + k], recv_sem=recv_sems.at[j * n + k],
                    device_id=(*chip, c), device_id_type=MESH)
                cp.start()
                cps.append(cp)
        for cp in cps:
            cp.wait()

    return pl.pallas_call(
        body, name="reduce_to_owner",
        out_shape=[jax.ShapeDtypeStruct((3,) + s, BF16) for s in hs],
        in_specs=[ANY] * n, out_specs=[ANY] * n,
        scratch_shapes=[pltpu.SemaphoreType.DMA((3 * n,)), pltpu.SemaphoreType.DMA((3 * n,))],
    )(*halves)


def _reduce_finish(halves, recvd, kinds, full_shapes):
    n = len(halves)
    hs = [_half_shard_shape(fs, kind) for fs, kind in zip(full_shapes, kinds)]
    shard_shapes = [(fs[0], fs[1] // 4) if kind == "col" else (fs[0] // 4, fs[1])
                    for fs, kind in zip(full_shapes, kinds)]

    def body(*refs):
        h_refs, rc_refs, gs_refs = refs[:n], refs[n:2 * n], refs[2 * n:3 * n]
        own_refs, gh_refs = refs[3 * n:4 * n], refs[4 * n:5 * n]
        in_sems, loc_sems, send_sems, recv_sems = refs[5 * n:]
        x, y, c = _my_place()
        p = 2 * x + y
        loads = []
        for k in range(n):
            cp = pltpu.make_async_copy(_shard_window(h_refs[k], kinds[k], p), own_refs[k], in_sems.at[k])
            cp.start()
            loads.append(cp)
        outs = []
        for k in range(n):
            loads[k].wait()
            gh_refs[k][...] = (own_refs[k][...].astype(F32) + rc_refs[k][0].astype(F32)
                               + rc_refs[k][1].astype(F32) + rc_refs[k][2].astype(F32))
            dst = _half_window(gs_refs[k], kinds[k], c)
            lc = pltpu.make_async_copy(gh_refs[k], dst, loc_sems.at[k])
            lc.start()
            rc = pltpu.make_async_remote_copy(
                src_ref=gh_refs[k], dst_ref=dst, send_sem=send_sems.at[k], recv_sem=recv_sems.at[k],
                device_id=(x, y, 1 - c), device_id_type=MESH)
            rc.start()
            outs.append((lc, rc))
        for k, (lc, rc) in enumerate(outs):
            lc.wait()
            rc.wait_send()
            pltpu.make_async_remote_copy(
                src_ref=gh_refs[k], dst_ref=_half_window(gs_refs[k], kinds[k], 1 - c),
                send_sem=send_sems.at[k], recv_sem=recv_sems.at[k],
                device_id=(x, y, 1 - c), device_id_type=MESH).wait_recv()

    return pl.pallas_call(
        body, name="reduce_finish",
        out_shape=[jax.ShapeDtypeStruct(s, F32) for s in shard_shapes],
        in_specs=[ANY] * n + [VM] * n, out_specs=[ANY] * n,
        scratch_shapes=[pltpu.VMEM(s, BF16) for s in hs] + [pltpu.VMEM(s, F32) for s in hs]
        + [pltpu.SemaphoreType.DMA((n,)) for _ in range(4)],
        compiler_params=_params(),
    )(*halves, *recvd)


def _mod_part(c_all, w_ada_l, b_l):
    def body(c_ref, w_ref, b_ref, o_ref):
        o_ref[...] = _dot(c_ref[...].astype(BF16), w_ref[...].astype(BF16)) + b_ref[...]

    return pl.pallas_call(
        body, name="mod_part", out_shape=jax.ShapeDtypeStruct((8, w_ada_l.shape[1]), F32),
        in_specs=[VM, VM, VM], out_specs=VM, compiler_params=_params(),
    )(c_all, w_ada_l, b_l)


def _sum_devices(parts, m):
    def body(p_ref, o_ref):
        acc = p_ref[0:m, :]
        for d in range(1, 8):
            acc = acc + p_ref[d * m:(d + 1) * m, :]
        o_ref[...] = acc

    return pl.pallas_call(
        body, name="sum_devices", out_shape=jax.ShapeDtypeStruct((m, parts.shape[1]), F32),
        in_specs=[VM], out_specs=VM, compiler_params=_params(),
    )(parts)


def _grad_w_ada(c_all_t, dmod_l):
    d, w = c_all_t.shape[0], dmod_l.shape[1]

    def body(ct_ref, dm_ref, o_ref):
        acc = ct_ref[:, 0:1] * dm_ref[0:1, :]
        for b in range(1, 8):
            acc = acc + ct_ref[:, b:b + 1] * dm_ref[b:b + 1, :]
        o_ref[...] = acc

    return pl.pallas_call(
        body, name="grad_w_ada", out_shape=jax.ShapeDtypeStruct((d, w), F32),
        in_specs=[VM, VM], out_specs=VM, compiler_params=_params(),
    )(c_all_t, dmod_l)


def _adamw(w, g, m, v, name):
    r, c = w.shape
    tr = _row_tile(r, c, target=1024 * 1024, mult=8)
    c1 = 1.0 - ADAM_B1 ** ADAM_STEP
    c2 = 1.0 - ADAM_B2 ** ADAM_STEP

    def body(w_ref, g_ref, m_ref, v_ref, d_ref, nm_ref, nv_ref):
        gv = g_ref[...]
        nm = ADAM_B1 * m_ref[...] + (1.0 - ADAM_B1) * gv
        nv = ADAM_B2 * v_ref[...] + (1.0 - ADAM_B2) * (gv * gv)
        m_hat = nm / c1
        v_hat = nv / c2
        d_ref[...] = -ADAM_LR * (m_hat / (jnp.sqrt(v_hat) + ADAM_EPS) + ADAM_WD * w_ref[...])
        nm_ref[...] = nm
        nv_ref[...] = nv

    spec = pl.BlockSpec((tr, c), lambda i: (i, 0))
    return pl.pallas_call(
        body, name=name, grid=(r // tr,),
        out_shape=[jax.ShapeDtypeStruct((r, c), F32)] * 3,
        in_specs=[spec] * 4, out_specs=[spec] * 3,
        compiler_params=_params(("parallel",)),
    )(w, g, m, v)


def _rope_tables(pos_ref, tab_ref):
    ang = pos_ref[...].astype(F32) * tab_ref[0:1, :]
    cs, sn = jnp.cos(ang), jnp.sin(ang)
    return jnp.where(tab_ref[3:4, :] > 0, cs, 1.0), -sn * tab_ref[1:2, :], sn * tab_ref[2:3, :]


def _deinterleave_store(vals, slab, out_ref, d, ts, dtype):
    if d == 1:
        for s in range(4):
            out_ref[0, :, s * LANES:(s + 1) * LANES] = vals[s].astype(dtype)
        return
    for s in range(4):
        slab[s] = vals[s]
    for r in range(d):
        for s in range(4):
            out_ref[r, :, s * LANES:(s + 1) * LANES] = slab[s, pl.ds(r, ts // d, stride=d), :].astype(dtype)


def _interleave_load(blk_ref, slab, d, ts):
    if d == 1:
        return [blk_ref[0, :, s * LANES:(s + 1) * LANES] for s in range(4)]
    for r in range(d):
        for s in range(4):
            slab[s, pl.ds(r, ts // d, stride=d), :] = blk_ref[r, :, s * LANES:(s + 1) * LANES]
    return [slab[s] for s in range(4)]


def _input_projection(x, modv, norm_g, w_in, ts):
    s_len, d_model = x.shape
    ci = w_in.shape[1]
    ct = COL_TILE
    nc, ng = 3 * d_model // ct, (ATT_W + 2 * d_model) // ct
    assert ci == (nc + 9 + ng) * ct and s_len % ts == 0

    def body(x_ref, mod_ref, g_ref, w_ref, pc_ref, pg_ref, h_ref, ht_ref):
        j = pl.program_id(1)

        @pl.when(j == 0)
        def _():
            xv = x_ref[...]
            r = lax.rsqrt(jnp.mean(xv * xv, axis=-1, keepdims=True) + EPS)
            h = (xv * r) * g_ref[...] * (1.0 + mod_ref[1:2, :]) + mod_ref[0:1, :]
            h_ref[...] = h.astype(BF16)
            ht_ref[...] = h.T.astype(BF16)

        res = _dot(h_ref[...], w_ref[...])

        @pl.when(j < nc)
        def _():
            pc_ref[...] = res

        @pl.when(j >= nc)
        def _():
            pg_ref[...] = res

    return pl.pallas_call(
        body, name="input_projection", grid=(s_len // ts, nc + ng),
        in_specs=[pl.BlockSpec((ts, d_model), lambda i, j: (i, 0)),
                  pl.BlockSpec((8, d_model), lambda i, j: (0, 0)),
                  pl.BlockSpec((1, d_model), lambda i, j: (0, 0)),
                  pl.BlockSpec((d_model, ct), lambda i, j: (0, jnp.where(j < nc, j, j + 9)))],
        out_specs=[pl.BlockSpec((ts, ct), lambda i, j: (i, jnp.clip(j, 0, nc - 1))),
                   pl.BlockSpec((ts, ct), lambda i, j: (i, jnp.clip(j - nc, 0, ng - 1))),
                   pl.BlockSpec((ts, d_model), lambda i, j: (i, 0)),
                   pl.BlockSpec((d_model, ts), lambda i, j: (0, i))],
        out_shape=[jax.ShapeDtypeStruct((s_len, 3 * d_model), F32),
                   jax.ShapeDtypeStruct((s_len, ATT_W + 2 * d_model), F32),
                   jax.ShapeDtypeStruct((s_len, d_model), BF16),
                   jax.ShapeDtypeStruct((d_model, s_len), BF16)],
        compiler_params=_params(("arbitrary", "arbitrary")),
    )(x, modv, norm_g, w_in)


def _qkv_projection(h, pos, ropetab, w_in, gi, ts):
    s_len, d_model = h.shape
    dil = DILATIONS[gi]
    nc = 3 * d_model // COL_TILE
    hr = ts // 2
    assert COL_TILE == ATT_W and hr % (16 * dil) == 0

    def body(h_ref, pos_ref, tab_ref, wq_ref, wk_ref, wv_ref, o_ref, slab):
        rc, ra, rb = _rope_tables(pos_ref, tab_ref)
        w_refs = (wq_ref, wk_ref, wv_ref)
        units = [(w, half) for w in range(3) for half in range(2)]

        def matmul(w, half):
            return _dot(h_ref[half * hr:(half + 1) * hr, :], w_refs[w][...])

        def finish(w, half, res):
            rows = slice(half * hr, (half + 1) * hr)
            vals = []
            for s in range(4):
                t = res[:, s * LANES:(s + 1) * LANES]
                if w < 2:
                    t = (t * rc[rows, :] + pltpu.roll(t, LANES - 8, 1) * ra[rows, :]
                         + pltpu.roll(t, 8, 1) * rb[rows, :])
                vals.append(t)
            out = o_ref.at[w, :, half * (hr // dil):(half + 1) * (hr // dil), :]
            _deinterleave_store(vals, slab.at[half], out, dil, hr, BF16)

        res_next = matmul(*units[0])
        for ui, (w, half) in enumerate(units):
            res = res_next
            if ui + 1 < len(units):
                res_next = matmul(*units[ui + 1])
            finish(w, half, res)

    def w_spec(w):
        return pl.BlockSpec((d_model, COL_TILE), lambda i: (0, nc + 3 * w + gi))

    return pl.pallas_call(
        body, name="qkv_projection_%d" % dil, grid=(s_len // ts,),
        in_specs=[pl.BlockSpec((ts, d_model), lambda i: (i, 0)), pl.BlockSpec((ts, 1), lambda i: (i, 0)),
                  pl.BlockSpec((8, LANES), lambda i: (0, 0)), w_spec(0), w_spec(1), w_spec(2)],
        out_specs=pl.BlockSpec((3, dil, ts // dil, ATT_W), lambda i: (0, 0, i, 0)),
        out_shape=jax.ShapeDtypeStruct((3, dil, s_len // dil, ATT_W), BF16),
        scratch_shapes=[pltpu.VMEM((2, 4, hr, LANES), F32)],
        compiler_params=_params(("parallel",)),
    )(h, pos, ropetab, w_in, w_in, w_in)


def _layernorm_stats(u1):
    mu = jnp.mean(u1, axis=-1, keepdims=True)
    xc = u1 - mu
    rstd = lax.rsqrt(jnp.mean(xc * xc, axis=-1, keepdims=True) + EPS)
    return xc * rstd, rstd


def _shifted_copies(win, shf, ts):
    rows = ts + HALO - 8
    for b in range(1, 8):
        shf[b - 1, 0:rows, :] = win[pl.ds(b, rows), :]


def _tap(win, shf, off, r0, rows):
    a, b = divmod(off, 8)
    start = 8 * a + r0
    if b == 0:
        return win[start:start + rows, :]
    return shf[b - 1, start:start + rows, :]


def _conv_forward(p_conv, conv_w, conv_b, ln_g, ln_b, w_co, ts):
    s_len, d3 = p_conv.shape
    dm = d3 // 3

    def body(p_ref, cw_ref, cb_ref, g_ref, b_ref, w_ref, y_ref, u0_ref, u1_ref, win, shf):
        i = pl.program_id(0)

        @pl.when(i == 0)
        def _():
            win[0:HALO, :] = jnp.zeros((HALO, dm), F32)

        a, b, z = p_ref[:, 0:dm], p_ref[:, dm:2 * dm], p_ref[:, 2 * dm:3 * dm]
        u0 = a * _sig(b)
        win[HALO:HALO + ts, :] = u0
        u0_ref[...] = u0
        _shifted_copies(win, shf, ts)
        for r0 in range(0, ts, CONV_ROWS_FWD):
            acc = jnp.broadcast_to(cb_ref[...], (CONV_ROWS_FWD, dm))
            for k in range(CONV_K):
                acc = acc + cw_ref[k:k + 1, :] * _tap(win, shf, HALO - (CONV_K - 1) + k, r0, CONV_ROWS_FWD)
            u1_ref[r0:r0 + CONV_ROWS_FWD, :] = acc
        xh, _ = _layernorm_stats(u1_ref[...])
        u2 = xh * g_ref[...] + b_ref[...]
        a_conv = (u2 * _sig(u2)) * (z * _sig(z))
        y_ref[...] = _dot(a_conv.astype(BF16), w_ref[...])
        win[0:HALO, :] = win[ts:ts + HALO, :]

    row = pl.BlockSpec((1, dm), lambda i: (0, 0))
    tile = pl.BlockSpec((ts, dm), lambda i: (i, 0))
    return pl.pallas_call(
        body, name="conv_forward", grid=(s_len // ts,),
        in_specs=[pl.BlockSpec((ts, d3), lambda i: (i, 0)),
                  pl.BlockSpec((HALO, dm), lambda i: (0, 0)), row, row, row,
                  pl.BlockSpec((dm, dm), lambda i: (0, 0))],
        out_specs=[tile, tile, tile],
        out_shape=[jax.ShapeDtypeStruct((s_len, dm), F32)] * 3,
        scratch_shapes=[pltpu.VMEM((ts + HALO, dm), F32), pltpu.VMEM((7, ts + HALO - 8, dm), F32)],
        compiler_params=_params(("arbitrary",)),
    )(p_conv, conv_w, conv_b, ln_g, ln_b, w_co)


def _conv_backward(dp, dyc, p_conv, u0, u1, conv_w, ln_g, ln_b, w_co, ts):
    s_len, d3 = p_conv.shape
    dm = d3 // 3
    nt = s_len // ts
    hb = ts // HALO

    def body(dp_in, dy_ref, p_ref, u0_ref, uh_ref, u1_ref, cw_ref, g_ref, b_ref, w_ref,
             dp_ref, gw_ref, gs_ref, gcw_ref, dwin, uwin, shf):
        del dp_in
        i = pl.program_id(0)
        ti = nt - 1 - i

        @pl.when(i == 0)
        def _():
            gw_ref[...] = jnp.zeros_like(gw_ref)
            gs_ref[...] = jnp.zeros_like(gs_ref)
            gcw_ref[...] = jnp.zeros_like(gcw_ref)
            dwin[ts:ts + HALO, :] = jnp.zeros((HALO, dm), F32)

        dy = dy_ref[...]
        z = p_ref[:, 2 * dm:3 * dm]
        d_ac = _dot_nt(dy, w_ref[...])
        xh, rstd = _layernorm_stats(u1_ref[...])
        u2 = xh * g_ref[...] + b_ref[...]
        sg2, sgz = _sig(u2), _sig(z)
        u3, sz = u2 * sg2, z * sgz
        gw_ref[...] += _dot_tn((u3 * sz).astype(BF16), dy)
        d_z = d_ac * u3 * _dsilu(z, sgz)
        d_u2 = d_ac * sz * _dsilu(u2, sg2)
        gs_ref[0:1, :] += jnp.sum(d_u2 * xh, axis=0, keepdims=True)
        gs_ref[1:2, :] += jnp.sum(d_u2, axis=0, keepdims=True)
        dxh = d_u2 * g_ref[...]
        d_u1 = rstd * (dxh - jnp.mean(dxh, axis=-1, keepdims=True)
                       - xh * jnp.mean(dxh * xh, axis=-1, keepdims=True))
        gs_ref[2:3, :] += jnp.sum(d_u1, axis=0, keepdims=True)
        dwin[0:ts, :] = d_u1
        uwin[0:HALO, :] = jnp.where(ti == 0, 0.0, uh_ref[...])
        uwin[HALO:HALO + ts, :] = u0_ref[...]
        dp_ref[:, 2 * dm:3 * dm] = d_z.astype(BF16)
        _shifted_copies(uwin, shf, ts)
        for k in range(CONV_K):
            part = jnp.zeros((CONV_ROWS_BWD, dm), F32)
            for r0 in range(0, ts, CONV_ROWS_BWD):
                part = part + dwin[r0:r0 + CONV_ROWS_BWD, :] * _tap(uwin, shf, HALO - (CONV_K - 1) + k, r0,
                                                                    CONV_ROWS_BWD)
            gcw_ref[k:k + 1, :] += jnp.sum(part, axis=0, keepdims=True)
        _shifted_copies(dwin, shf, ts)
        for r0 in range(0, ts, CONV_ROWS_BWD):
            d_u0 = jnp.zeros((CONV_ROWS_BWD, dm), F32)
            for k in range(CONV_K):
                d_u0 = d_u0 + cw_ref[k:k + 1, :] * _tap(dwin, shf, CONV_K - 1 - k, r0, CONV_ROWS_BWD)
            rows = slice(r0, r0 + CONV_ROWS_BWD)
            sgb = _sig(p_ref[rows, dm:2 * dm])
            dp_ref[rows, 0:dm] = (d_u0 * sgb).astype(BF16)
            dp_ref[rows, dm:2 * dm] = (d_u0 * p_ref[rows, 0:dm] * sgb * (1.0 - sgb)).astype(BF16)
        dwin[ts:ts + HALO, :] = dwin[0:HALO, :]

    rev = lambda i: (nt - 1 - i, 0)
    row = pl.BlockSpec((1, dm), lambda i: (0, 0))
    tile = pl.BlockSpec((ts, dm), rev)
    return pl.pallas_call(
        body, name="conv_backward", grid=(nt,),
        in_specs=[ANY, tile, pl.BlockSpec((ts, d3), rev), tile,
                  pl.BlockSpec((HALO, dm), lambda i: (jnp.maximum((nt - 1 - i) * hb - 1, 0), 0)),
                  tile, pl.BlockSpec((HALO, dm), lambda i: (0, 0)), row, row,
                  pl.BlockSpec((dm, dm), lambda i: (0, 0))],
        out_specs=[pl.BlockSpec((ts, d3), rev), pl.BlockSpec((dm, dm), lambda i: (0, 0)),
                   pl.BlockSpec((8, dm), lambda i: (0, 0)), pl.BlockSpec((HALO, dm), lambda i: (0, 0))],
        out_shape=[jax.ShapeDtypeStruct(dp.shape, BF16), jax.ShapeDtypeStruct((dm, dm), F32),
                   jax.ShapeDtypeStruct((8, dm), F32), jax.ShapeDtypeStruct((HALO, dm), F32)],
        input_output_aliases={0: 0},
        scratch_shapes=[pltpu.VMEM((ts + HALO, dm), F32), pltpu.VMEM((ts + HALO, dm), F32),
                        pltpu.VMEM((7, ts + HALO - 8, dm), F32)],
        compiler_params=_params(("arbitrary",)),
    )(dp, dyc, p_conv, u0, u0, u1, conv_w, ln_g, ln_b, w_co)


def _att_masks():
    head0 = lax.broadcasted_iota(jnp.int32, (BLK, LANES), 1) < 64
    col = lax.broadcasted_iota(jnp.int32, (BLK, 4 * BLK), 1)
    row = lax.broadcasted_iota(jnp.int32, (BLK, 4 * BLK), 0)
    kj = col % BLK
    prev = jnp.where(col < 2 * BLK, 1, 0)
    band = jnp.where(col < 2 * BLK, kj - row, row - kj)
    return head0, band, prev


def _fill_block_diagonal(dst, slab, src_ref, halo_ref, head0, nq):
    sl = slice(slab * LANES, (slab + 1) * LANES)
    for b in range(nq + 1):
        blk = halo_ref[:, sl] if b == 0 else src_ref[(b - 1) * BLK:b * BLK, sl]
        base = (slab * (nq + 1) + b) * 2 * BLK
        zero = jnp.zeros_like(blk)
        dst[base:base + BLK, :] = jnp.where(head0, blk, zero)
        dst[base + BLK:base + 2 * BLK, :] = jnp.where(head0, zero, blk)


def _attention_forward(qkv, seq_len, qt, name):
    s_len = qkv.shape[1]
    nq = qt // BLK
    tiles_per_seq = seq_len // qt

    def body(q_ref, k_ref, v_ref, kh_ref, vh_ref, o_ref, lse_ref, kbd, vbd):
        first = jnp.where((pl.program_id(0) % tiles_per_seq) == 0, 4 * BLK, 0)
        head0, band, prev = _att_masks()
        band_first = band - prev * first
        for p in range(4):
            _fill_block_diagonal(kbd, p, k_ref, kh_ref, head0, nq)
            _fill_block_diagonal(vbd, p, v_ref, vh_ref, head0, nq)
        units = [(p, n) for p in range(4) for n in range(nq)]

        def keys_of(p, n):
            base = (p * (nq + 1) + n) * 2 * BLK
            return slice(base, base + 4 * BLK)

        def scores(p, n):
            q2 = q_ref[n * BLK:(n + 1) * BLK, p * LANES:(p + 1) * LANES] * 0.125
            return _dot_nt(q2, kbd[keys_of(p, n), :])

        def finish(p, n, o, den, lse):
            rows, sl = slice(n * BLK, (n + 1) * BLK), slice(p * LANES, (p + 1) * LANES)
            o_ref[rows, sl] = o / jnp.where(head0, den[0], den[1])
            lse_ref[rows, sl] = jnp.where(head0, lse[0], lse[1])

        s_next = scores(*units[0])
        pending = None
        for ui, (p, n) in enumerate(units):
            s = s_next
            if ui + 1 < len(units):
                s_next = scores(*units[ui + 1])
            s = jnp.where((band_first if n == 0 else band) >= 0, s, NEG)
            grp = [s[:, g * BLK:(g + 1) * BLK] for g in range(4)]
            ps, den, lse = [None] * 4, [], []
            for h in range(2):
                m = jnp.max(jnp.maximum(grp[h], grp[2 + h]), axis=-1, keepdims=True)
                ps[h], ps[2 + h] = jnp.exp(grp[h] - m), jnp.exp(grp[2 + h] - m)
                dn = jnp.sum(ps[h] + ps[2 + h], axis=-1, keepdims=True)
                den.append(dn)
                lse.append(m + jnp.log(dn))
            pmat = jnp.concatenate([x.astype(BF16) for x in ps], axis=1)
            o = _dot(pmat, vbd[keys_of(p, n), :])
            if pending is not None:
                finish(*pending)
            pending = (p, n, o, den, lse)
        finish(*pending)

    def which(w):
        return pl.BlockSpec((None, qt, ATT_W), lambda i: (w, i, 0))

    def halo(w):
        return pl.BlockSpec((None, BLK, ATT_W), lambda i: (w, jnp.maximum(i * nq - 1, 0), 0))

    out = pl.BlockSpec((qt, ATT_W), lambda i: (i, 0))
    bd = pltpu.VMEM((4 * (nq + 1) * 2 * BLK, LANES), BF16)
    return pl.pallas_call(
        body, name=name, grid=(s_len // qt,),
        in_specs=[which(0), which(1), which(2), halo(1), halo(2)],
        out_specs=[out, out], out_shape=[jax.ShapeDtypeStruct((s_len, ATT_W), F32)] * 2,
        scratch_shapes=[bd, bd],
        compiler_params=_params(("parallel",)),
    )(qkv, qkv, qkv, qkv, qkv)


def _attention_backward(qkv, d_att, lse, delta, seq_len, qt, name):
    s_len = qkv.shape[1]
    nq = qt // BLK
    tiles_per_seq = seq_len // qt
    nblk = s_len // BLK

    def body(q_ref, k_ref, v_ref, kh_ref, vh_ref, do_ref, lse_ref, dl_ref,
             qn_ref, don_ref, lsen_ref, dln_ref, dqkv_ref, kbd, vbd):
        i = pl.program_id(0)
        first = jnp.where((i % tiles_per_seq) == 0, 4 * BLK, 0)
        last = jnp.where((i % tiles_per_seq) == tiles_per_seq - 1, 4 * BLK, 0)
        head0, band, prev = _att_masks()
        band_first = band - prev * first
        band_tail = band[:, 0:2 * BLK] - last
        for p in range(4):
            _fill_block_diagonal(kbd, p, k_ref, kh_ref, head0, nq)
            _fill_block_diagonal(vbd, p, v_ref, vh_ref, head0, nq)
        units = [(p, n) for p in range(4) for n in range(nq + 1)]

        def stage_a(p, n):
            sl = slice(p * LANES, (p + 1) * LANES)
            base = (p * (nq + 1) + n) * 2 * BLK
            if n < nq:
                rows = slice(n * BLK, (n + 1) * BLK)
                q2, do2, lse2, dl2 = q_ref[rows, sl], do_ref[rows, sl], lse_ref[rows, sl], dl_ref[rows, sl]
                keys = slice(base, base + 4 * BLK)
            else:
                q2, do2, lse2, dl2 = qn_ref[:, sl], don_ref[:, sl], lsen_ref[:, sl], dln_ref[:, sl]
                keys = slice(base, base + 2 * BLK)
            s = _dot_nt(q2 * 0.125, kbd[keys, :])
            dp = _dot_nt(do2, vbd[keys, :])
            return q2, do2, lse2, dl2, keys, s, dp

        def stage_b(n, lse2, dl2, s, dp):
            mask = band_tail if n == nq else (band_first if n == 0 else band)
            ps, dss = [], []
            for g in range(s.shape[1] // BLK):
                h = g % 2
                cols = slice(g * BLK, (g + 1) * BLK)
                pg = jnp.exp(jnp.where(mask[:, cols] >= 0, s[:, cols] - lse2[:, h * 64:h * 64 + 1], NEG))
                ps.append(pg.astype(BF16))
                dss.append((pg * (dp[:, cols] - dl2[:, h * 64:h * 64 + 1]) * 0.125).astype(BF16))
            return jnp.concatenate(ps, axis=1), jnp.concatenate(dss, axis=1)

        def heads(r, g):
            return jnp.where(head0, r[g * BLK:(g + 1) * BLK, :], r[(g + 1) * BLK:(g + 2) * BLK, :])

        a_next = stage_a(*units[0])
        carry = None
        for ui, (p, n) in enumerate(units):
            q2, do2, lse2, dl2, keys, s, dp = a_next
            if ui + 1 < len(units):
                a_next = stage_a(*units[ui + 1])
            pmat, dsmat = stage_b(n, lse2, dl2, s, dp)
            sl = slice(p * LANES, (p + 1) * LANES)
            if n < nq:
                dqkv_ref[0, n * BLK:(n + 1) * BLK, sl] = _dot(dsmat, kbd[keys, :])
            dkbd = _dot_tn(dsmat, q2)
            dvbd = _dot_tn(pmat, do2)
            if n > 0:
                prow = slice((n - 1) * BLK, n * BLK)
                dqkv_ref[1, prow, sl] = carry[0] + heads(dkbd, 0)
                dqkv_ref[2, prow, sl] = carry[1] + heads(dvbd, 0)
            if n < nq:
                carry = (heads(dkbd, 2), heads(dvbd, 2))

    def which(w):
        return pl.BlockSpec((None, qt, ATT_W), lambda i: (w, i, 0))

    def prev(w):
        return pl.BlockSpec((None, BLK, ATT_W), lambda i: (w, jnp.maximum(i * nq - 1, 0), 0))

    tile = pl.BlockSpec((qt, ATT_W), lambda i: (i, 0))
    nxt = pl.BlockSpec((BLK, ATT_W), lambda i: (jnp.minimum((i + 1) * nq, nblk - 1), 0))
    nxt_q = pl.BlockSpec((None, BLK, ATT_W), lambda i: (0, jnp.minimum((i + 1) * nq, nblk - 1), 0))
    return pl.pallas_call(
        body, name=name, grid=(s_len // qt,),
        in_specs=[which(0), which(1), which(2), prev(1), prev(2), tile, tile, tile, nxt_q, nxt, nxt, nxt],
        out_specs=pl.BlockSpec((3, qt, ATT_W), lambda i: (0, i, 0)),
        out_shape=jax.ShapeDtypeStruct((3, s_len, ATT_W), F32),
        scratch_shapes=[pltpu.VMEM((4 * (nq + 1) * 2 * BLK, LANES), BF16)] * 2,
        compiler_params=_params(("parallel",)),
    )(qkv, qkv, qkv, qkv, qkv, d_att, lse, delta, qkv, d_att, lse, delta)


def _merge_and_head(o_g, lse_g, p_gate, y_conv, x, tgt, modv, final_g, w_ao, w_o, ts):
    s_len, dm = x.shape
    gw = ATT_W + 2 * dm
    nt = s_len // ts
    gate_off = 3 * dm + 9 * ATT_W
    assert gate_off % gw == 0

    def body(o0, o1, o2, l0, l1, l2, pg_ref, yc_ref, x_ref, t_ref, mod_ref, fg_ref, wao_ref, wo_ref,
             loss_ref, dx_ref, dyc_ref, dpg_ref, da0, da1, da2, ls0, ls1, ls2, de0, de1, de2,
             gwo_ref, gwao_ref, gs_ref, slab):
        i = pl.program_id(0)

        @pl.when(i == 0)
        def _():
            loss_ref[...] = jnp.zeros_like(loss_ref)
            gwo_ref[...] = jnp.zeros_like(gwo_ref)
            gwao_ref[...] = jnp.zeros_like(gwao_ref)
            gs_ref[...] = jnp.zeros_like(gs_ref)

        os_, ls_ = [], []
        for dil, o_ref, l_ref in zip(DILATIONS, (o0, o1, o2), (l0, l1, l2)):
            os_.append(jnp.concatenate(_interleave_load(o_ref, slab, dil, ts), axis=1))
            ls_.append(jnp.concatenate(_interleave_load(l_ref, slab, dil, ts), axis=1))
        mx = jnp.maximum(jnp.maximum(ls_[0], ls_[1]), ls_[2])
        wts = [jnp.exp(l - mx) for l in ls_]
        wsum = wts[0] + wts[1] + wts[2]
        att = (wts[0] * os_[0] + wts[1] * os_[1] + wts[2] * os_[2]) / wsum
        lse_all = mx + jnp.log(wsum)

        z_att, g_conv, g_att = pg_ref[:, 0:ATT_W], pg_ref[:, ATT_W:ATT_W + dm], pg_ref[:, ATT_W + dm:gw]
        sgz = _sig(z_att)
        sz = z_att * sgz
        a_att = (att * sz).astype(BF16)
        y_att = _dot(a_att, wao_ref[...])
        y_conv = yc_ref[...]
        sgc, sga = _sig(g_conv), _sig(g_att)
        merged = (sgc * y_conv + sga * y_att).astype(BF16)
        mo = _dot(merged, wo_ref[...])
        gate = mod_ref[2:3, :]
        x2 = x_ref[...] + gate * mo
        r = lax.rsqrt(jnp.mean(x2 * x2, axis=-1, keepdims=True) + EPS)
        xr = x2 * r
        err = xr * fg_ref[...] - t_ref[...]
        loss_ref[...] += 0.5 * jnp.sum(jnp.mean(err * err, axis=-1, keepdims=True))
        dy = err * (1.0 / dm)
        gs_ref[0:1, :] += jnp.sum(dy * xr, axis=0, keepdims=True)
        dyg = dy * fg_ref[...]
        d_x2 = r * dyg - xr * (r * jnp.mean(dyg * xr, axis=-1, keepdims=True))
        dx_ref[...] = d_x2
        gs_ref[1:2, :] += jnp.sum(d_x2 * mo, axis=0, keepdims=True)
        d_mo = (d_x2 * gate).astype(BF16)
        d_mg = _dot_nt(d_mo, wo_ref[...])
        gwo_ref[...] += _dot_tn(merged, d_mo)
        dyc_ref[...] = (d_mg * sgc).astype(BF16)
        dpg_ref[:, ATT_W:ATT_W + dm] = (d_mg * y_conv * sgc * (1.0 - sgc)).astype(BF16)
        d_ya = (d_mg * sga).astype(BF16)
        dpg_ref[:, ATT_W + dm:gw] = (d_mg * y_att * sga * (1.0 - sga)).astype(BF16)
        gwao_ref[...] += _dot_tn(a_att, d_ya)
        d_aa = _dot_nt(d_ya, wao_ref[...])
        dpg_ref[:, 0:ATT_W] = (d_aa * att * _dsilu(z_att, sgz)).astype(BF16)
        d_att = d_aa * sz
        ri = lax.broadcasted_iota(jnp.int32, (ATT_W, ATT_W), 0) // 64
        ci = lax.broadcasted_iota(jnp.int32, (ATT_W, ATT_W), 1) // 64
        ones = jnp.where(ri == ci, 1.0, 0.0).astype(BF16)
        prod = d_att * att
        hi = prod.astype(BF16)
        lo = (prod - hi.astype(F32)).astype(BF16)
        delta = _dot(hi, ones) + _dot(lo, ones)
        for val, refs, dt in ((d_att, (da0, da1, da2), BF16), (lse_all, (ls0, ls1, ls2), F32),
                              (delta, (de0, de1, de2), F32)):
            vals = [val[:, s * LANES:(s + 1) * LANES] for s in range(4)]
            for dil, ref in zip(DILATIONS, refs):
                _deinterleave_store(vals, slab, ref, dil, ts, dt)

    def grp(dil):
        return pl.BlockSpec((dil, ts // dil, ATT_W), lambda i: (0, i, 0))

    tile = pl.BlockSpec((ts, dm), lambda i: (i, 0))
    gate_tile = pl.BlockSpec((ts, gw), lambda i: (i, 0))
    const = lambda shp: pl.BlockSpec(shp, lambda i: tuple(0 for _ in shp))
    grp_shape = lambda dt: [jax.ShapeDtypeStruct((dil, s_len // dil, ATT_W), dt) for dil in DILATIONS]
    return pl.pallas_call(
        body, name="merge_and_head", grid=(nt,),
        in_specs=[grp(d) for d in DILATIONS] * 2
        + [gate_tile, tile, tile, tile, const((8, dm)), const((1, dm)), const((ATT_W, dm)), const((dm, dm))],
        out_specs=[const((8, LANES)), tile, tile, pl.BlockSpec((ts, gw), lambda i: (i, gate_off // gw))]
        + [grp(d) for d in DILATIONS] * 3
        + [const((dm, dm)), const((ATT_W, dm)), const((8, dm))],
        out_shape=[jax.ShapeDtypeStruct((8, LANES), F32), jax.ShapeDtypeStruct((s_len, dm), F32),
                   jax.ShapeDtypeStruct((s_len, dm), BF16), jax.ShapeDtypeStruct((s_len, gate_off + gw), BF16)]
        + grp_shape(BF16) + grp_shape(F32) + grp_shape(F32)
        + [jax.ShapeDtypeStruct((dm, dm), F32), jax.ShapeDtypeStruct((ATT_W, dm), F32),
           jax.ShapeDtypeStruct((8, dm), F32)],
        scratch_shapes=[pltpu.VMEM((4, ts, LANES), F32)],
        compiler_params=_params(("arbitrary",)),
    )(*o_g, *lse_g, p_gate, y_conv, x, tgt, modv, final_g, w_ao, w_o)


def _qkv_grad_to_tokens(dp, dqkv_g, pos, ropetab, ts):
    s_len, dm = pos.shape[0], (dp.shape[1] - 10 * ATT_W) // 5
    qw = 3 * ATT_W
    assert (3 * dm) % qw == 0

    def body(dp_in, g0, g1, g2, pos_ref, tab_ref, o_ref, slab, rc, ra, rb):
        del dp_in
        w = pl.program_id(1)

        @pl.when(w == 0)
        def _():
            c_, a_, b_ = _rope_tables(pos_ref, tab_ref)
            rc[...] = c_
            ra[...] = a_
            rb[...] = b_

        def emit(roped):
            for gi, (dil, g_ref) in enumerate(zip(DILATIONS, (g0, g1, g2))):
                vals = _interleave_load(g_ref, slab, dil, ts)
                for s in range(4):
                    t = vals[s]
                    if roped:
                        t = t * rc[...] + pltpu.roll(t * ra[...], 8, 1) + pltpu.roll(t * rb[...], LANES - 8, 1)
                    col = gi * ATT_W + s * LANES
                    o_ref[:, col:col + LANES] = t.astype(BF16)

        pl.when(w < 2)(lambda: emit(True))
        pl.when(w == 2)(lambda: emit(False))

    return pl.pallas_call(
        body, name="qkv_grad_to_tokens", grid=(s_len // ts, 3),
        in_specs=[ANY] + [pl.BlockSpec((None, dil, ts // dil, ATT_W), lambda i, w: (w, 0, i, 0)) for dil in DILATIONS]
        + [pl.BlockSpec((ts, 1), lambda i, w: (i, 0)), pl.BlockSpec((8, LANES), lambda i, w: (0, 0))],
        out_specs=pl.BlockSpec((ts, qw), lambda i, w: (i, 3 * dm // qw + w)),
        out_shape=jax.ShapeDtypeStruct(dp.shape, BF16),
        input_output_aliases={0: 0},
        scratch_shapes=[pltpu.VMEM((4, ts, LANES), F32)] + [pltpu.VMEM((ts, LANES), F32)] * 3,
        compiler_params=_params(("parallel", "arbitrary")),
    )(dp, *dqkv_g, pos, ropetab)


def _wide_col_tile(cols):
    return 2 * COL_TILE if cols % (2 * COL_TILE) == 0 else COL_TILE


def _input_grad(dp, w_in, x, dx_res, modv, norm_g, ts):
    s_len, dm = x.shape
    ct = _wide_col_tile(dp.shape[1])
    nct = dp.shape[1] // ct

    def body(p_ref, w_ref, x_ref, dxr_ref, mod_ref, g_ref, gx_ref, gs_ref, acc):
        i, j = pl.program_id(0), pl.program_id(1)

        @pl.when((i == 0) & (j == 0))
        def _():
            gs_ref[...] = jnp.zeros_like(gs_ref)

        @pl.when(j == 0)
        def _():
            acc[...] = jnp.zeros_like(acc)

        acc[...] += _dot_nt(p_ref[...], w_ref[...])

        @pl.when(j == nct - 1)
        def _():
            d_h = acc[...]
            xv = x_ref[...]
            r = lax.rsqrt(jnp.mean(xv * xv, axis=-1, keepdims=True) + EPS)
            xr = xv * r
            gs_ref[0:1, :] += jnp.sum(d_h, axis=0, keepdims=True)
            gs_ref[1:2, :] += jnp.sum(d_h * (xr * g_ref[...]), axis=0, keepdims=True)
            d_n = d_h * (1.0 + mod_ref[1:2, :])
            gs_ref[2:3, :] += jnp.sum(d_n * xr, axis=0, keepdims=True)
            dxn = d_n * g_ref[...]
            gx_ref[...] = dxr_ref[...] + r * dxn - xr * (r * jnp.mean(dxn * xr, axis=-1, keepdims=True))

    tile = pl.BlockSpec((ts, dm), lambda i, j: (i, 0))
    return pl.pallas_call(
        body, name="input_grad", grid=(s_len // ts, nct),
        in_specs=[pl.BlockSpec((ts, ct), lambda i, j: (i, j)), pl.BlockSpec((dm, ct), lambda i, j: (0, j)), tile, tile,
                  pl.BlockSpec((8, dm), lambda i, j: (0, 0)), pl.BlockSpec((1, dm), lambda i, j: (0, 0))],
        out_specs=[tile, pl.BlockSpec((8, dm), lambda i, j: (0, 0))],
        out_shape=[jax.ShapeDtypeStruct((s_len, dm), F32), jax.ShapeDtypeStruct((8, dm), F32)],
        scratch_shapes=[pltpu.VMEM((ts, dm), F32)],
        compiler_params=_params(("arbitrary", "arbitrary")),
    )(dp, w_in, x, dx_res, modv, norm_g)


def _w_in_grad(h_t, dp, ts):
    dm, s_len = h_t.shape
    ct = _wide_col_tile(dp.shape[1])

    def body(h_ref, p_ref, o_ref):
        @pl.when(pl.program_id(1) == 0)
        def _():
            o_ref[...] = jnp.zeros_like(o_ref)

        o_ref[...] += _dot(h_ref[...], p_ref[...])

    return pl.pallas_call(
        body, name="w_in_grad", grid=(dp.shape[1] // ct, s_len // ts),
        in_specs=[pl.BlockSpec((dm, ts), lambda j, i: (0, i)), pl.BlockSpec((ts, ct), lambda j, i: (i, j))],
        out_specs=pl.BlockSpec((dm, ct), lambda j, i: (0, j)),
        out_shape=jax.ShapeDtypeStruct((dm, dp.shape[1]), F32),
        compiler_params=_params(("arbitrary", "arbitrary")),
    )(h_t, dp)


def _rope_lane_table():
    l64 = jnp.arange(LANES) % 64
    half = ROT_DIM // 2
    inv_freq = ROPE_THETA ** (-(jnp.arange(half, dtype=F32) * 2.0 / ROT_DIM))
    rot = l64 < ROT_DIM
    rows = [jnp.where(rot, inv_freq[l64 % half], 0.0), (l64 < half).astype(F32),
            ((l64 >= half) & rot).astype(F32), rot.astype(F32)]
    return jnp.concatenate([jnp.stack(rows), jnp.zeros((4, LANES), F32)], axis=0)


def _tile_sizes(s_len):
    l_min = s_len // DILATIONS[-1]
    ts_big = min(1024, s_len // 2)
    ts_mid = 256
    ts_head = 256
    qt = min(512, l_min)
    return ts_big, ts_mid, ts_head, qt


def kernel(x, c, positions, norm_g, w_ada, b_ada, w_in, conv_w, conv_b, conv_ln_g, conv_ln_b, w_conv_out, w_att_out, w_o, final_g, loss_target, m_norm_g, m_w_ada, m_b_ada, m_w_in, m_conv_w, m_conv_b, m_conv_ln_g, m_conv_ln_b, m_w_conv_out, m_w_att_out, m_w_o, m_final_g, v_norm_g, v_w_ada, v_b_ada, v_w_in, v_conv_w, v_conv_b, v_conv_ln_g, v_conv_ln_b, v_w_conv_out, v_w_att_out, v_w_o, v_final_g):
    s_len, dm = x.shape[1], x.shape[2]
    ts_big, ts_mid, ts_head, qt = _tile_sizes(s_len)
    xi, yi, cidx = _my_place()
    chip = 2 * xi + yi
    batch = 4 * xi + 2 * yi + cidx
    x2d, tgt = x[0], loss_target[0]
    pos = positions.reshape(s_len, 1)
    wa_l, wi_l, cw_l = w_ada[0], w_in[0], conv_w[0]
    wco_l, wao_l, wo_l = w_conv_out[0], w_att_out[0], w_o[0]
    ada_w = wa_l.shape[1]
    cw_cols = cw_l.shape[1]

    cw_pad = jnp.pad(cw_l, ((0, HALO - CONV_K), (0, 0)))
    small_in = jnp.concatenate([jnp.broadcast_to(c, (8, dm)), cw_pad.reshape(8, dm)], axis=0)
    small = _allgather_small(small_in).reshape(8, 16, dm)
    c_all = small[:, 0, :]
    conv_w_full = jnp.concatenate(
        [small[2 * p, 8:16, :].reshape(HALO, cw_cols) for p in range(4)], axis=1)
    b_l = lax.dynamic_slice(b_ada, (0, chip * ada_w), (1, ada_w))
    mod_parts = _allgather_small(_mod_part(c_all, wa_l, b_l)).reshape(8, 8, ada_w)
    mod_rows = lax.dynamic_index_in_dim(mod_parts, batch, axis=1, keepdims=False)
    mod = jnp.concatenate([mod_rows[2 * p] for p in range(4)], axis=0).reshape(3, dm)
    modv = jnp.concatenate([mod, jnp.zeros((5, dm), F32)], axis=0)

    kinds = ("col", "row", "col", "row")
    w_in_b, w_co_b, w_ao_b, w_o_b = _allgather_weights([wi_l, wco_l, wao_l, wo_l], kinds)

    ropetab = _rope_lane_table()
    p_conv, p_gate, h_b, h_t = _input_projection(x2d, modv, norm_g, w_in_b, ts_big)
    qkv_g = [_qkv_projection(h_b, pos, ropetab, w_in_b, gi, ts_big) for gi in range(3)]
    y_conv, u0, u1 = _conv_forward(p_conv, conv_w_full, conv_b, conv_ln_g, conv_ln_b, w_co_b, ts_mid)
    qkv_flat = [q.reshape(3, s_len, ATT_W) for q in qkv_g]
    o_g, lse_g = [], []
    for gi, dil in enumerate(DILATIONS):
        o, l = _attention_forward(qkv_flat[gi], s_len // dil, qt, "attention_forward_%d" % dil)
        o_g.append(o.reshape(dil, s_len // dil, ATT_W))
        lse_g.append(l.reshape(dil, s_len // dil, ATT_W))

    (loss_p, dx_res, dyc, dp, da0, da1, da2, ls0, ls1, ls2, de0, de1, de2,
     g_wo, g_wao, head_sums) = _merge_and_head(o_g, lse_g, p_gate, y_conv, x2d, tgt, modv,
                                               final_g.reshape(1, dm), w_ao_b, w_o_b, ts_head)
    loss = lax.psum(loss_p[0, 0], ("x", "y", "c"))

    dp, g_wco, conv_sums, g_cw = _conv_backward(dp, dyc, p_conv, u0, u1, conv_w_full, conv_ln_g, conv_ln_b,
                                                w_co_b, ts_mid)
    dqkv_g = []
    for gi, (dil, da, ls, de) in enumerate(zip(DILATIONS, (da0, da1, da2), (ls0, ls1, ls2), (de0, de1, de2))):
        flat = lambda a: a.reshape(s_len, ATT_W)
        dqkv = _attention_backward(qkv_flat[gi], flat(da), flat(ls), flat(de), s_len // dil, qt,
                                   "attention_backward_%d" % dil)
        dqkv_g.append(dqkv.reshape(3, dil, s_len // dil, ATT_W))
    dp = _qkv_grad_to_tokens(dp, dqkv_g, pos, ropetab, 2 * ts_mid)
    grad_x, in_sums = _input_grad(dp, w_in_b, x2d, dx_res, modv, norm_g, ts_big)
    g_win = _w_in_grad(h_t, dp, ts_big)

    grads = [g_win, g_wco, g_wao, g_wo]
    full_shapes = [g.shape for g in grads]
    c_arr = jnp.reshape(cidx, (1,)).astype(jnp.int32)
    recv_halves = _reduce_pair_exchange(grads, kinds)
    halves = [_reduce_pair_sum(g, pa, kind, c_arr, "reduce_pair_sum_%d" % k)
              for k, (g, pa, kind) in enumerate(zip(grads, recv_halves, kinds))]
    recvd = _reduce_to_owner(halves, kinds, full_shapes)
    gr_win, gr_wco, gr_wao, gr_wo = _reduce_finish(halves, recvd, kinds, full_shapes)

    rows = [in_sums[2:3], conv_sums[2:3], conv_sums[0:1], conv_sums[1:2], head_sums[0:1],
            in_sums[0:1], in_sums[1:2], head_sums[1:2], g_cw, jnp.zeros((8, dm), F32)]
    part = jnp.concatenate(rows, axis=0)
    gathered = _allgather_small(part)
    tot = _sum_devices(gathered, 48)
    dmod_all = gathered.reshape(8, 48, dm)[:, 5:8, :].reshape(8, 3 * dm)
    dmod_l = lax.dynamic_slice(dmod_all, (0, chip * ada_w), (8, ada_w))
    gr_wada = _grad_w_ada(c_all.T, dmod_l)
    gr_cw = lax.dynamic_slice(tot[8:8 + HALO], (0, chip * cw_cols), (HALO, cw_cols))

    def pack8(ng, cb, lg, lb, fg, ba):
        return jnp.concatenate([ng, cb, lg, lb, fg.reshape(1, dm), ba.reshape(3, dm)], axis=0)

    pad_cw = lambda a: jnp.pad(a[0], ((0, HALO - CONV_K), (0, 0)))
    w8 = pack8(norm_g, conv_b, conv_ln_g, conv_ln_b, final_g, b_ada)
    m8 = pack8(m_norm_g, m_conv_b, m_conv_ln_g, m_conv_ln_b, m_final_g, m_b_ada)
    v8 = pack8(v_norm_g, v_conv_b, v_conv_ln_g, v_conv_ln_b, v_final_g, v_b_ada)
    g8 = tot[0:8]
    upd = {
        "small": _adamw(w8, g8, m8, v8, "adamw_small"),
        "w_ada": _adamw(wa_l, gr_wada, m_w_ada[0], v_w_ada[0], "adamw_w_ada"),
        "w_in": _adamw(wi_l, gr_win, m_w_in[0], v_w_in[0], "adamw_w_in"),
        "conv_w": _adamw(cw_pad, gr_cw, pad_cw(m_conv_w), pad_cw(v_conv_w), "adamw_conv_w"),
        "w_co": _adamw(wco_l, gr_wco, m_w_conv_out[0], v_w_conv_out[0], "adamw_w_conv_out"),
        "w_ao": _adamw(wao_l, gr_wao, m_w_att_out[0], v_w_att_out[0], "adamw_w_att_out"),
        "w_o": _adamw(wo_l, gr_wo, m_w_o[0], v_w_o[0], "adamw_w_o"),
    }

    def family(which):
        if which is None:
            sm = g8
            big = {"w_ada": gr_wada, "w_in": gr_win, "conv_w": gr_cw, "w_co": gr_wco, "w_ao": gr_wao, "w_o": gr_wo}
        else:
            sm = upd["small"][which]
            big = {k: upd[k][which] for k in ("w_ada", "w_in", "conv_w", "w_co", "w_ao", "w_o")}
        return [sm[0:1], big["w_ada"][None], sm[5:8].reshape(1, 3 * dm), big["w_in"][None],
                big["conv_w"][None, :CONV_K], sm[1:2], sm[2:3], sm[3:4], big["w_co"][None],
                big["w_ao"][None], big["w_o"][None], sm[4]]

    return (loss, grad_x[None], *family(None), *family(0), *family(1), *family(2))
```

```python
import jax
import jax.numpy as jnp
from jax import lax
from jax.experimental import pallas as pl
from jax.experimental.pallas import tpu as pltpu

F32 = jnp.float32
BF16 = jnp.bfloat16
MESH = pl.DeviceIdType.MESH
ANY = pl.BlockSpec(memory_space=pl.ANY)
VM = pl.BlockSpec(memory_space=pltpu.VMEM)

EPS = 1e-6
NEG = -1e30
ATT_W = 512
DILATIONS = (1, 4, 16)
BLK = 128
CONV_K = 31
HALO = 32
CONV_ROWS_FWD = 32
CONV_ROWS_BWD = 16
ROT_DIM = 16
ROPE_THETA = 500000.0
COL_TILE = 512
LANES = 128
VMEM_LIMIT = 56 * 1024 * 1024

ADAM_LR, ADAM_B1, ADAM_B2, ADAM_EPS, ADAM_WD, ADAM_STEP = 0.001, 0.9, 0.999, 1e-08, 0.01, 10


def _params(sem=None, vmem=VMEM_LIMIT):
    return pltpu.CompilerParams(dimension_semantics=sem, vmem_limit_bytes=vmem)


def _dot(a, b):
    return jnp.dot(a, b, preferred_element_type=F32)


def _dot_nt(a, b):
    return lax.dot_general(a, b, (((1,), (1,)), ((), ())), preferred_element_type=F32)


def _dot_tn(a, b):
    return lax.dot_general(a, b, (((0,), (0,)), ((), ())), preferred_element_type=F32)


def _sig(x):
    return jax.nn.sigmoid(x)


def _dsilu(x, s):
    return s * (1.0 + x * (1.0 - s))


def _my_place():
    return lax.axis_index("x"), lax.axis_index("y"), lax.axis_index("c")


def _allgather_small(x_shard):
    m_per, n = x_shard.shape

    def body(x_ref, out_ref, send_sems, recv_sems, local_sem):
        x, y, c = _my_place()
        me, sibling = (x, y, c), (x, y, 1 - c)
        chips = [(1 - x, y), (x, 1 - y), (1 - x, 1 - y)]

        def rows(px, py, pc):
            return out_ref.at[pl.ds((4 * px + 2 * py + pc) * m_per, m_per), :]

        def copy(k, block, to, src=None):
            return pltpu.make_async_remote_copy(
                src_ref=rows(*block) if src is None else src, dst_ref=rows(*block),
                send_sem=send_sems.at[k], recv_sem=recv_sems.at[k],
                device_id=to, device_id_type=MESH)

        mine = pltpu.make_async_copy(x_ref, rows(*me), local_sem)
        mine.start()
        first = [copy(0, me, sibling, src=x_ref)]
        first += [copy(1 + j, me, (*chip, c), src=x_ref) for j, chip in enumerate(chips)]
        for cp in first:
            cp.start()
        passed = [copy(4 + j, (*chip, c), sibling) for j, chip in enumerate(chips)]
        for j, chip in enumerate(chips):
            copy(1 + j, (*chip, c), me).wait_recv()
            passed[j].start()
        copy(0, sibling, me).wait_recv()
        for j, chip in enumerate(chips):
            copy(4 + j, (*chip, 1 - c), me).wait_recv()
        for cp in first + passed:
            cp.wait_send()
        mine.wait()

    return pl.pallas_call(
        body, name="allgather_small",
        out_shape=jax.ShapeDtypeStruct((8 * m_per, n), x_shard.dtype),
        in_specs=[VM], out_specs=VM,
        scratch_shapes=[pltpu.SemaphoreType.DMA((7,)), pltpu.SemaphoreType.DMA((7,)),
                        pltpu.SemaphoreType.DMA],
    )(x_shard)


def _shard_window(ref, kind, p, n_shards=4):
    r, c = ref.shape
    if kind == "col":
        w = c // n_shards
        return ref.at[:, pl.ds(p * w, w)]
    w = r // n_shards
    return ref.at[pl.ds(p * w, w), :]


def _half_window(ref, kind, hc):
    r, c = ref.shape
    if kind == "col":
        return ref.at[pl.ds(hc * (r // 2), r // 2), :]
    return ref.at[:, pl.ds(hc * (c // 2), c // 2)]


def _allgather_weights(shards, kinds):
    n = len(shards)
    full_shapes = []
    for s, kind in zip(shards, kinds):
        r, c = s.shape
        full_shapes.append((r, 4 * c) if kind == "col" else (4 * r, c))

    def body(*refs):
        w_refs, out_refs = refs[:n], refs[n:2 * n]
        bf_refs = refs[2 * n:3 * n]
        send_sems, recv_sems, local_sems = refs[3 * n:]
        x, y, c = _my_place()
        p = 2 * x + y
        sibling = (x, y, 1 - c)
        chips = [(1 - x, y), (x, 1 - y), (1 - x, 1 - y)]
        locals_ = []
        for k in range(n):
            bf_refs[k][...] = w_refs[k][...].astype(BF16)
            cp = pltpu.make_async_copy(bf_refs[k], _shard_window(out_refs[k], kinds[k], p), local_sems.at[k])
            cp.start()
            locals_.append(cp)

        def landed(k, chip, hc):
            return _half_window(_shard_window(out_refs[k], kinds[k], 2 * chip[0] + chip[1]), kinds[k], hc)

        def ici(j, k, chip_src, to, src=None):
            return pltpu.make_async_remote_copy(
                src_ref=landed(k, chip_src, c) if src is None else src, dst_ref=landed(k, chip_src, c),
                send_sem=send_sems.at[j * n + k], recv_sem=recv_sems.at[j * n + k],
                device_id=to, device_id_type=MESH)

        first = []
        for j, chip in enumerate(chips):
            for k in range(n):
                cp = ici(j, k, (x, y), (*chip, c), src=_half_window(bf_refs[k], kinds[k], c))
                cp.start()
                first.append(cp)
        passed = []
        for j, chip in enumerate(chips):
            for k in range(n):
                ici(j, k, chip, (x, y, c)).wait_recv()
                cp = pltpu.make_async_remote_copy(
                    src_ref=landed(k, chip, c), dst_ref=landed(k, chip, c),
                    send_sem=send_sems.at[(3 + j) * n + k], recv_sem=recv_sems.at[(3 + j) * n + k],
                    device_id=sibling, device_id_type=MESH)
                cp.start()
                passed.append(cp)
        for j, chip in enumerate(chips):
            for k in range(n):
                pltpu.make_async_remote_copy(
                    src_ref=landed(k, chip, 1 - c), dst_ref=landed(k, chip, 1 - c),
                    send_sem=send_sems.at[(3 + j) * n + k], recv_sem=recv_sems.at[(3 + j) * n + k],
                    device_id=sibling, device_id_type=MESH).wait_recv()
        for cp in first + passed:
            cp.wait_send()
        for cp in locals_:
            cp.wait()

    return pl.pallas_call(
        body, name="allgather_weights",
        out_shape=[jax.ShapeDtypeStruct(s, BF16) for s in full_shapes],
        in_specs=[VM] * n, out_specs=[ANY] * n,
        scratch_shapes=[pltpu.VMEM(s.shape, BF16) for s in shards]
        + [pltpu.SemaphoreType.DMA((6 * n,)), pltpu.SemaphoreType.DMA((6 * n,)), pltpu.SemaphoreType.DMA((n,))],
        compiler_params=_params(),
    )(*shards)


def _reduce_pair_exchange(grads, kinds):
    n = len(grads)
    half_shapes = [(g.shape[0] // 2, g.shape[1]) if kind == "col" else (g.shape[0], g.shape[1] // 2)
                   for g, kind in zip(grads, kinds)]

    def body(*refs):
        g_refs, pa_refs = refs[:n], refs[n:2 * n]
        send_sems, recv_sems = refs[2 * n:]
        x, y, c = _my_place()
        cps = []
        for k in range(n):
            cp = pltpu.make_async_remote_copy(
                src_ref=_half_window(g_refs[k], kinds[k], 1 - c), dst_ref=pa_refs[k],
                send_sem=send_sems.at[k], recv_sem=recv_sems.at[k],
                device_id=(x, y, 1 - c), device_id_type=MESH)
            cp.start()
            cps.append(cp)
        for cp in cps:
            cp.wait()

    return pl.pallas_call(
        body, name="reduce_pair_exchange",
        out_shape=[jax.ShapeDtypeStruct(s, F32) for s in half_shapes],
        in_specs=[ANY] * n, out_specs=[ANY] * n,
        scratch_shapes=[pltpu.SemaphoreType.DMA((n,)), pltpu.SemaphoreType.DMA((n,))],
    )(*grads)


def _row_tile(rows, cols, itemsize=4, target=2 * 1024 * 1024, mult=16):
    t = rows
    while t % 2 == 0 and t // 2 >= mult and (t // 2) % mult == 0 and t * cols * itemsize > target:
        t //= 2
    return t


def _reduce_pair_sum(g, pa, kind, c_arr, name):
    hr, hc_ = pa.shape
    tr = _row_tile(hr, hc_)
    nb = hr // tr

    def body(c_ref, g_ref, pa_ref, o_ref):
        o_ref[...] = (g_ref[...] + pa_ref[...]).astype(BF16)

    if kind == "col":
        g_map = lambda i, c_ref: (c_ref[0] * nb + i, 0)
    else:
        g_map = lambda i, c_ref: (i, c_ref[0])
    return pl.pallas_call(
        body, name=name,
        grid_spec=pltpu.PrefetchScalarGridSpec(
            num_scalar_prefetch=1, grid=(nb,),
            in_specs=[pl.BlockSpec((tr, hc_), g_map), pl.BlockSpec((tr, hc_), lambda i, c_ref: (i, 0))],
            out_specs=pl.BlockSpec((tr, hc_), lambda i, c_ref: (i, 0))),
        out_shape=jax.ShapeDtypeStruct((hr, hc_), BF16),
        compiler_params=_params(("parallel",)),
    )(c_arr, g, pa)


def _half_shard_shape(full_shape, kind):
    r, c = full_shape
    return (r // 2, c // 4) if kind == "col" else (r // 4, c // 2)


def _to_owner_copies(h_refs, land_refs, send_sems, recv_sems, kinds):
    n = len(h_refs)
    x, y, c = _my_place()
    chips = [(1 - x, y), (x, 1 - y), (1 - x, 1 - y)]
    cps = []
    for j, chip in enumerate(chips):
        pj = 2 * chip[0] + chip[1]
        for k in range(n):
            cps.append(pltpu.make_async_remote_copy(
                src_ref=_shard_window(h_refs[k], kinds[k], pj), dst_ref=land_refs[k].at[j],
                send_sem=send_sems.at[j * n + k], recv_sem=recv_sems.at[j * n + k],
                device_id=(*chip, c), device_id_type=MESH))
    return cps


def _reduce_to_owner_start(halves, kinds, full_shapes):
    n = len(halves)
    hs = [_half_shard_shape(fs, kind) for fs, kind in zip(full_shapes, kinds)]
    hbm = pl.BlockSpec(memory_space=pltpu.HBM)
    sem = pl.BlockSpec(memory_space=pltpu.SEMAPHORE)

    def body(*refs):
        h_refs, land_refs = refs[:n], refs[n:2 * n]
        send_sems, recv_sems = refs[2 * n:2 * n + 2]
        token = refs[-1]
        for cp in _to_owner_copies(h_refs, land_refs, send_sems, recv_sems, kinds):
            cp.start()
        token[...] = jnp.zeros_like(token)

    lands = [pltpu.with_memory_space_constraint(lax.empty((3,) + s, BF16), pltpu.HBM) for s in hs]
    out = pl.pallas_call(
        body, name="reduce_to_owner_start",
        out_shape=[pltpu.SemaphoreType.DMA((3 * n,)), pltpu.SemaphoreType.DMA((3 * n,))]
        + [pltpu.HBM(h.shape, h.dtype) for h in halves] + [pltpu.HBM((3,) + s, BF16) for s in hs]
        + [jax.ShapeDtypeStruct((8, LANES), F32)],
        in_specs=[hbm] * (2 * n), out_specs=[sem, sem] + [hbm] * (2 * n) + [VM],
        input_output_aliases={k: 2 + k for k in range(2 * n)},
        compiler_params=pltpu.CompilerParams(has_side_effects=pltpu.SideEffectType.DATAFLOW_SIDE_EFFECTING),
    )(*[pltpu.with_memory_space_constraint(h, pltpu.HBM) for h in halves], *lands)
    return out[0], out[1], out[2:2 + n], out[2 + n:2 + 2 * n], out[-1]


def _reduce_to_owner_wait(send_sems, recv_sems, halves, lands, after, kinds):
    n = len(halves)
    hbm = pl.BlockSpec(memory_space=pltpu.HBM)
    sem = pl.BlockSpec(memory_space=pltpu.SEMAPHORE)

    def body(*refs):
        h_refs, land_refs = refs[:n], refs[n:2 * n]
        send_s, recv_s = refs[2 * n:2 * n + 2]
        for cp in _to_owner_copies(h_refs, land_refs, send_s, recv_s, kinds):
            cp.wait_send()
            cp.wait_recv()

    out = pl.pallas_call(
        body, name="reduce_to_owner_wait",
        out_shape=[pltpu.HBM(h.shape, h.dtype) for h in halves] + [pltpu.HBM(l.shape, l.dtype) for l in lands],
        in_specs=[hbm] * (2 * n) + [sem, sem, ANY], out_specs=[hbm] * (2 * n),
        input_output_aliases={k: k for k in range(2 * n)},
        compiler_params=pltpu.CompilerParams(has_side_effects=pltpu.SideEffectType.DATAFLOW_SIDE_EFFECTING),
    )(*halves, *lands, send_sems, recv_sems, after)
    return out[:n], out[n:]


def _reduce_finish(halves, recvd, kinds, full_shapes):
    n = len(halves)
    hs = [_half_shard_shape(fs, kind) for fs, kind in zip(full_shapes, kinds)]
    shard_shapes = [(fs[0], fs[1] // 4) if kind == "col" else (fs[0] // 4, fs[1])
                    for fs, kind in zip(full_shapes, kinds)]

    def body(*refs):
        h_refs, rc_refs, gs_refs = refs[:n], refs[n:2 * n], refs[2 * n:3 * n]
        own_refs, gh_refs = refs[3 * n:4 * n], refs[4 * n:5 * n]
        in_sems, loc_sems, send_sems, recv_sems = refs[5 * n:]
        x, y, c = _my_place()
        p = 2 * x + y
        loads = []
        for k in range(n):
            cp = pltpu.make_async_copy(_shard_window(h_refs[k], kinds[k], p), own_refs[k], in_sems.at[k])
            cp.start()
            loads.append(cp)
        outs = []
        for k in range(n):
            loads[k].wait()
            gh_refs[k][...] = (own_refs[k][...].astype(F32) + rc_refs[k][0].astype(F32)
                               + rc_refs[k][1].astype(F32) + rc_refs[k][2].astype(F32))
            dst = _half_window(gs_refs[k], kinds[k], c)
            lc = pltpu.make_async_copy(gh_refs[k], dst, loc_sems.at[k])
            lc.start()
            rc = pltpu.make_async_remote_copy(
                src_ref=gh_refs[k], dst_ref=dst, send_sem=send_sems.at[k], recv_sem=recv_sems.at[k],
                device_id=(x, y, 1 - c), device_id_type=MESH)
            rc.start()
            outs.append((lc, rc))
        for k, (lc, rc) in enumerate(outs):
            lc.wait()
            rc.wait_send()
            pltpu.make_async_remote_copy(
                src_ref=gh_refs[k], dst_ref=_half_window(gs_refs[k], kinds[k], 1 - c),
                send_sem=send_sems.at[k], recv_sem=recv_sems.at[k],
                device_id=(x, y, 1 - c), device_id_type=MESH).wait_recv()

    return pl.pallas_call(
        body, name="reduce_finish",
        out_shape=[jax.ShapeDtypeStruct(s, F32) for s in shard_shapes],
        in_specs=[ANY] * n + [VM] * n, out_specs=[ANY] * n,
        scratch_shapes=[pltpu.VMEM(s, BF16) for s in hs] + [pltpu.VMEM(s, F32) for s in hs]
        + [pltpu.SemaphoreType.DMA((n,)) for _ in range(4)],
        compiler_params=_params(),
    )(*halves, *recvd)


def _mod_part(c_all, w_ada_l, b_l):
    def body(c_ref, w_ref, b_ref, o_ref):
        o_ref[...] = _dot(c_ref[...].astype(BF16), w_ref[...].astype(BF16)) + b_ref[...]

    return pl.pallas_call(
        body, name="mod_part", out_shape=jax.ShapeDtypeStruct((8, w_ada_l.shape[1]), F32),
        in_specs=[VM, VM, VM], out_specs=VM, compiler_params=_params(),
    )(c_all, w_ada_l, b_l)


def _sum_devices(parts, m):
    def body(p_ref, o_ref):
        acc = p_ref[0:m, :]
        for d in range(1, 8):
            acc = acc + p_ref[d * m:(d + 1) * m, :]
        o_ref[...] = acc

    return pl.pallas_call(
        body, name="sum_devices", out_shape=jax.ShapeDtypeStruct((m, parts.shape[1]), F32),
        in_specs=[VM], out_specs=VM, compiler_params=_params(),
    )(parts)


def _grad_w_ada(c_all_t, dmod_l):
    d, w = c_all_t.shape[0], dmod_l.shape[1]

    def body(ct_ref, dm_ref, o_ref):
        acc = ct_ref[:, 0:1] * dm_ref[0:1, :]
        for b in range(1, 8):
            acc = acc + ct_ref[:, b:b + 1] * dm_ref[b:b + 1, :]
        o_ref[...] = acc

    return pl.pallas_call(
        body, name="grad_w_ada", out_shape=jax.ShapeDtypeStruct((d, w), F32),
        in_specs=[VM, VM], out_specs=VM, compiler_params=_params(),
    )(c_all_t, dmod_l)


def _adamw(w, g, m, v, name):
    r, c = w.shape
    tr = _row_tile(r, c, target=1024 * 1024, mult=8)
    c1 = 1.0 - ADAM_B1 ** ADAM_STEP
    c2 = 1.0 - ADAM_B2 ** ADAM_STEP

    def body(w_ref, g_ref, m_ref, v_ref, d_ref, nm_ref, nv_ref):
        gv = g_ref[...]
        nm = ADAM_B1 * m_ref[...] + (1.0 - ADAM_B1) * gv
        nv = ADAM_B2 * v_ref[...] + (1.0 - ADAM_B2) * (gv * gv)
        m_hat = nm / c1
        v_hat = nv / c2
        d_ref[...] = -ADAM_LR * (m_hat / (jnp.sqrt(v_hat) + ADAM_EPS) + ADAM_WD * w_ref[...])
        nm_ref[...] = nm
        nv_ref[...] = nv

    spec = pl.BlockSpec((tr, c), lambda i: (i, 0))
    return pl.pallas_call(
        body, name=name, grid=(r // tr,),
        out_shape=[jax.ShapeDtypeStruct((r, c), F32)] * 3,
        in_specs=[spec] * 4, out_specs=[spec] * 3,
        compiler_params=_params(("parallel",)),
    )(w, g, m, v)


def _rope_tables(pos_ref, tab_ref):
    ang = pos_ref[...].astype(F32) * tab_ref[0:1, :]
    cs, sn = jnp.cos(ang), jnp.sin(ang)
    return jnp.where(tab_ref[3:4, :] > 0, cs, 1.0), -sn * tab_ref[1:2, :], sn * tab_ref[2:3, :]


def _deinterleave_store(vals, slab, out_ref, d, ts, dtype):
    if d == 1:
        for s in range(4):
            out_ref[0, :, s * LANES:(s + 1) * LANES] = vals[s].astype(dtype)
        return
    for s in range(4):
        slab[s] = vals[s]
    for r in range(d):
        for s in range(4):
            out_ref[r, :, s * LANES:(s + 1) * LANES] = slab[s, pl.ds(r, ts // d, stride=d), :].astype(dtype)


def _interleave_load(blk_ref, slab, d, ts):
    if d == 1:
        return [blk_ref[0, :, s * LANES:(s + 1) * LANES] for s in range(4)]
    for r in range(d):
        for s in range(4):
            slab[s, pl.ds(r, ts // d, stride=d), :] = blk_ref[r, :, s * LANES:(s + 1) * LANES]
    return [slab[s] for s in range(4)]


def _input_projection(x, modv, norm_g, w_in, ts):
    s_len, d_model = x.shape
    ci = w_in.shape[1]
    ct = COL_TILE
    nc, ng = 3 * d_model // ct, (ATT_W + 2 * d_model) // ct
    assert ci == (nc + 9 + ng) * ct and s_len % ts == 0

    def body(x_ref, mod_ref, g_ref, w_ref, pc_ref, pg_ref, h_ref, ht_ref):
        j = pl.program_id(1)

        @pl.when(j == 0)
        def _():
            xv = x_ref[...]
            r = lax.rsqrt(jnp.mean(xv * xv, axis=-1, keepdims=True) + EPS)
            h = (xv * r) * g_ref[...] * (1.0 + mod_ref[1:2, :]) + mod_ref[0:1, :]
            h_ref[...] = h.astype(BF16)
            ht_ref[...] = h.T.astype(BF16)

        res = _dot(h_ref[...], w_ref[...])

        @pl.when(j < nc)
        def _():
            pc_ref[...] = res

        @pl.when(j >= nc)
        def _():
            pg_ref[...] = res

    return pl.pallas_call(
        body, name="input_projection", grid=(s_len // ts, nc + ng),
        in_specs=[pl.BlockSpec((ts, d_model), lambda i, j: (i, 0)),
                  pl.BlockSpec((8, d_model), lambda i, j: (0, 0)),
                  pl.BlockSpec((1, d_model), lambda i, j: (0, 0)),
                  pl.BlockSpec((d_model, ct), lambda i, j: (0, jnp.where(j < nc, j, j + 9)))],
        out_specs=[pl.BlockSpec((ts, ct), lambda i, j: (i, jnp.clip(j, 0, nc - 1))),
                   pl.BlockSpec((ts, ct), lambda i, j: (i, jnp.clip(j - nc, 0, ng - 1))),
                   pl.BlockSpec((ts, d_model), lambda i, j: (i, 0)),
                   pl.BlockSpec((d_model, ts), lambda i, j: (0, i))],
        out_shape=[jax.ShapeDtypeStruct((s_len, 3 * d_model), F32),
                   jax.ShapeDtypeStruct((s_len, ATT_W + 2 * d_model), F32),
                   jax.ShapeDtypeStruct((s_len, d_model), BF16),
                   jax.ShapeDtypeStruct((d_model, s_len), BF16)],
        compiler_params=_params(("arbitrary", "arbitrary")),
    )(x, modv, norm_g, w_in)


def _qkv_projection(h, pos, ropetab, w_in, gi, ts):
    s_len, d_model = h.shape
    dil = DILATIONS[gi]
    nc = 3 * d_model // COL_TILE
    hr = ts // 2
    assert COL_TILE == ATT_W and hr % (16 * dil) == 0

    def body(h_ref, pos_ref, tab_ref, wq_ref, wk_ref, wv_ref, o_ref, slab):
        rc, ra, rb = _rope_tables(pos_ref, tab_ref)
        w_refs = (wq_ref, wk_ref, wv_ref)
        units = [(w, half) for w in range(3) for half in range(2)]

        def matmul(w, half):
            return _dot(h_ref[half * hr:(half + 1) * hr, :], w_refs[w][...])

        def finish(w, half, res):
            rows = slice(half * hr, (half + 1) * hr)
            vals = []
            for s in range(4):
                t = res[:, s * LANES:(s + 1) * LANES]
                if w < 2:
                    t = (t * rc[rows, :] + pltpu.roll(t, LANES - 8, 1) * ra[rows, :]
                         + pltpu.roll(t, 8, 1) * rb[rows, :])
                vals.append(t)
            out = o_ref.at[w, :, half * (hr // dil):(half + 1) * (hr // dil), :]
            _deinterleave_store(vals, slab.at[half], out, dil, hr, BF16)

        res_next = matmul(*units[0])
        for ui, (w, half) in enumerate(units):
            res = res_next
            if ui + 1 < len(units):
                res_next = matmul(*units[ui + 1])
            finish(w, half, res)

    def w_spec(w):
        return pl.BlockSpec((d_model, COL_TILE), lambda i: (0, nc + 3 * w + gi))

    return pl.pallas_call(
        body, name="qkv_projection_%d" % dil, grid=(s_len // ts,),
        in_specs=[pl.BlockSpec((ts, d_model), lambda i: (i, 0)), pl.BlockSpec((ts, 1), lambda i: (i, 0)),
                  pl.BlockSpec((8, LANES), lambda i: (0, 0)), w_spec(0), w_spec(1), w_spec(2)],
        out_specs=pl.BlockSpec((3, dil, ts // dil, ATT_W), lambda i: (0, 0, i, 0)),
        out_shape=jax.ShapeDtypeStruct((3, dil, s_len // dil, ATT_W), BF16),
        scratch_shapes=[pltpu.VMEM((2, 4, hr, LANES), F32)],
        compiler_params=_params(("parallel",)),
    )(h, pos, ropetab, w_in, w_in, w_in)


def _layernorm_stats(u1):
    mu = jnp.mean(u1, axis=-1, keepdims=True)
    xc = u1 - mu
    rstd = lax.rsqrt(jnp.mean(xc * xc, axis=-1, keepdims=True) + EPS)
    return xc * rstd, rstd


def _shifted_copies(win, shf, ts):
    rows = ts + HALO - 8
    for b in range(1, 8):
        shf[b - 1, 0:rows, :] = win[pl.ds(b, rows), :]


def _tap(win, shf, off, r0, rows):
    a, b = divmod(off, 8)
    start = 8 * a + r0
    if b == 0:
        return win[start:start + rows, :]
    return shf[b - 1, start:start + rows, :]


def _conv_forward(p_conv, conv_w, conv_b, ln_g, ln_b, w_co, ts):
    s_len, d3 = p_conv.shape
    dm = d3 // 3

    def body(p_ref, cw_ref, cb_ref, g_ref, b_ref, w_ref, y_ref, u0_ref, u1_ref, win, shf):
        i = pl.program_id(0)

        @pl.when(i == 0)
        def _():
            win[0:HALO, :] = jnp.zeros((HALO, dm), F32)

        a, b, z = p_ref[:, 0:dm], p_ref[:, dm:2 * dm], p_ref[:, 2 * dm:3 * dm]
        u0 = a * _sig(b)
        win[HALO:HALO + ts, :] = u0
        u0_ref[...] = u0
        _shifted_copies(win, shf, ts)
        for r0 in range(0, ts, CONV_ROWS_FWD):
            acc = jnp.broadcast_to(cb_ref[...], (CONV_ROWS_FWD, dm))
            for k in range(CONV_K):
                acc = acc + cw_ref[k:k + 1, :] * _tap(win, shf, HALO - (CONV_K - 1) + k, r0, CONV_ROWS_FWD)
            u1_ref[r0:r0 + CONV_ROWS_FWD, :] = acc
        xh, _ = _layernorm_stats(u1_ref[...])
        u2 = xh * g_ref[...] + b_ref[...]
        a_conv = (u2 * _sig(u2)) * (z * _sig(z))
        y_ref[...] = _dot(a_conv.astype(BF16), w_ref[...])
        win[0:HALO, :] = win[ts:ts + HALO, :]

    row = pl.BlockSpec((1, dm), lambda i: (0, 0))
    tile = pl.BlockSpec((ts, dm), lambda i: (i, 0))
    return pl.pallas_call(
        body, name="conv_forward", grid=(s_len // ts,),
        in_specs=[pl.BlockSpec((ts, d3), lambda i: (i, 0)),
                  pl.BlockSpec((HALO, dm), lambda i: (0, 0)), row, row, row,
                  pl.BlockSpec((dm, dm), lambda i: (0, 0))],
        out_specs=[tile, tile, tile],
        out_shape=[jax.ShapeDtypeStruct((s_len, dm), F32)] * 3,
        scratch_shapes=[pltpu.VMEM((ts + HALO, dm), F32), pltpu.VMEM((7, ts + HALO - 8, dm), F32)],
        compiler_params=_params(("arbitrary",)),
    )(p_conv, conv_w, conv_b, ln_g, ln_b, w_co)


def _conv_backward(dp, dyc, p_conv, u0, u1, conv_w, ln_g, ln_b, w_co, ts):
    s_len, d3 = p_conv.shape
    dm = d3 // 3
    nt = s_len // ts
    hb = ts // HALO

    def body(dp_in, dy_ref, p_ref, u0_ref, uh_ref, u1_ref, cw_ref, g_ref, b_ref, w_ref,
             dp_ref, gw_ref, gs_ref, gcw_ref, dwin, uwin, shf):
        del dp_in
        i = pl.program_id(0)
        ti = nt - 1 - i

        @pl.when(i == 0)
        def _():
            gw_ref[...] = jnp.zeros_like(gw_ref)
            gs_ref[...] = jnp.zeros_like(gs_ref)
            gcw_ref[...] = jnp.zeros_like(gcw_ref)
            dwin[ts:ts + HALO, :] = jnp.zeros((HALO, dm), F32)

        dy = dy_ref[...]
        z = p_ref[:, 2 * dm:3 * dm]
        d_ac = _dot_nt(dy, w_ref[...])
        xh, rstd = _layernorm_stats(u1_ref[...])
        u2 = xh * g_ref[...] + b_ref[...]
        sg2, sgz = _sig(u2), _sig(z)
        u3, sz = u2 * sg2, z * sgz
        gw_ref[...] += _dot_tn((u3 * sz).astype(BF16), dy)
        d_z = d_ac * u3 * _dsilu(z, sgz)
        d_u2 = d_ac * sz * _dsilu(u2, sg2)
        gs_ref[0:1, :] += jnp.sum(d_u2 * xh, axis=0, keepdims=True)
        gs_ref[1:2, :] += jnp.sum(d_u2, axis=0, keepdims=True)
        dxh = d_u2 * g_ref[...]
        d_u1 = rstd * (dxh - jnp.mean(dxh, axis=-1, keepdims=True)
                       - xh * jnp.mean(dxh * xh, axis=-1, keepdims=True))
        gs_ref[2:3, :] += jnp.sum(d_u1, axis=0, keepdims=True)
        dwin[0:ts, :] = d_u1
        uwin[0:HALO, :] = jnp.where(ti == 0, 0.0, uh_ref[...])
        uwin[HALO:HALO + ts, :] = u0_ref[...]
        dp_ref[:, 2 * dm:3 * dm] = d_z.astype(BF16)
        _shifted_copies(uwin, shf, ts)
        for k in range(CONV_K):
            part = jnp.zeros((CONV_ROWS_BWD, dm), F32)
            for r0 in range(0, ts, CONV_ROWS_BWD):
                part = part + dwin[r0:r0 + CONV_ROWS_BWD, :] * _tap(uwin, shf, HALO - (CONV_K - 1) + k, r0,
                                                                    CONV_ROWS_BWD)
            gcw_ref[k:k + 1, :] += jnp.sum(part, axis=0, keepdims=True)
        _shifted_copies(dwin, shf, ts)
        for r0 in range(0, ts, CONV_ROWS_BWD):
            d_u0 = jnp.zeros((CONV_ROWS_BWD, dm), F32)
            for k in range(CONV_K):
                d_u0 = d_u0 + cw_ref[k:k + 1, :] * _tap(dwin, shf, CONV_K - 1 - k, r0, CONV_ROWS_BWD)
            rows = slice(r0, r0 + CONV_ROWS_BWD)
            sgb = _sig(p_ref[rows, dm:2 * dm])
            dp_ref[rows, 0:dm] = (d_u0 * sgb).astype(BF16)
            dp_ref[rows, dm:2 * dm] = (d_u0 * p_ref[rows, 0:dm] * sgb * (1.0 - sgb)).astype(BF16)
        dwin[ts:ts + HALO, :] = dwin[0:HALO, :]

    rev = lambda i: (nt - 1 - i, 0)
    row = pl.BlockSpec((1, dm), lambda i: (0, 0))
    tile = pl.BlockSpec((ts, dm), rev)
    return pl.pallas_call(
        body, name="conv_backward", grid=(nt,),
        in_specs=[ANY, tile, pl.BlockSpec((ts, d3), rev), tile,
                  pl.BlockSpec((HALO, dm), lambda i: (jnp.maximum((nt - 1 - i) * hb - 1, 0), 0)),
                  tile, pl.BlockSpec((HALO, dm), lambda i: (0, 0)), row, row,
                  pl.BlockSpec((dm, dm), lambda i: (0, 0))],
        out_specs=[pl.BlockSpec((ts, d3), rev), pl.BlockSpec((dm, dm), lambda i: (0, 0)),
                   pl.BlockSpec((8, dm), lambda i: (0, 0)), pl.BlockSpec((HALO, dm), lambda i: (0, 0))],
        out_shape=[jax.ShapeDtypeStruct(dp.shape, BF16), jax.ShapeDtypeStruct((dm, dm), F32),
                   jax.ShapeDtypeStruct((8, dm), F32), jax.ShapeDtypeStruct((HALO, dm), F32)],
        input_output_aliases={0: 0},
        scratch_shapes=[pltpu.VMEM((ts + HALO, dm), F32), pltpu.VMEM((ts + HALO, dm), F32),
                        pltpu.VMEM((7, ts + HALO - 8, dm), F32)],
        compiler_params=_params(("arbitrary",)),
    )(dp, dyc, p_conv, u0, u0, u1, conv_w, ln_g, ln_b, w_co)


def _att_masks():
    head0 = lax.broadcasted_iota(jnp.int32, (BLK, LANES), 1) < 64
    col = lax.broadcasted_iota(jnp.int32, (BLK, 4 * BLK), 1)
    row = lax.broadcasted_iota(jnp.int32, (BLK, 4 * BLK), 0)
    kj = col % BLK
    prev = jnp.where(col < 2 * BLK, 1, 0)
    band = jnp.where(col < 2 * BLK, kj - row, row - kj)
    return head0, band, prev


def _fill_block_diagonal(dst, slab, src_ref, halo_ref, head0, nq):
    sl = slice(slab * LANES, (slab + 1) * LANES)
    for b in range(nq + 1):
        blk = halo_ref[:, sl] if b == 0 else src_ref[(b - 1) * BLK:b * BLK, sl]
        base = (slab * (nq + 1) + b) * 2 * BLK
        zero = jnp.zeros_like(blk)
        dst[base:base + BLK, :] = jnp.where(head0, blk, zero)
        dst[base + BLK:base + 2 * BLK, :] = jnp.where(head0, zero, blk)


def _attention_forward(qkv, seq_len, qt, name):
    s_len = qkv.shape[1]
    nq = qt // BLK
    tiles_per_seq = seq_len // qt

    def body(q_ref, k_ref, v_ref, kh_ref, vh_ref, o_ref, lse_ref, kbd, vbd):
        first = jnp.where((pl.program_id(0) % tiles_per_seq) == 0, 4 * BLK, 0)
        head0, band, prev = _att_masks()
        band_first = band - prev * first
        for p in range(4):
            _fill_block_diagonal(kbd, p, k_ref, kh_ref, head0, nq)
            _fill_block_diagonal(vbd, p, v_ref, vh_ref, head0, nq)
        units = [(p, n) for p in range(4) for n in range(nq)]

        def keys_of(p, n):
            base = (p * (nq + 1) + n) * 2 * BLK
            return slice(base, base + 4 * BLK)

        def scores(p, n):
            q2 = q_ref[n * BLK:(n + 1) * BLK, p * LANES:(p + 1) * LANES] * 0.125
            return _dot_nt(q2, kbd[keys_of(p, n), :])

        def finish(p, n, o, den, lse):
            rows, sl = slice(n * BLK, (n + 1) * BLK), slice(p * LANES, (p + 1) * LANES)
            o_ref[rows, sl] = o / jnp.where(head0, den[0], den[1])
            lse_ref[rows, sl] = jnp.where(head0, lse[0], lse[1])

        s_next = scores(*units[0])
        pending = None
        for ui, (p, n) in enumerate(units):
            s = s_next
            if ui + 1 < len(units):
                s_next = scores(*units[ui + 1])
            s = jnp.where((band_first if n == 0 else band) >= 0, s, NEG)
            grp = [s[:, g * BLK:(g + 1) * BLK] for g in range(4)]
            ps, den, lse = [None] * 4, [], []
            for h in range(2):
                m = jnp.max(jnp.maximum(grp[h], grp[2 + h]), axis=-1, keepdims=True)
                ps[h], ps[2 + h] = jnp.exp(grp[h] - m), jnp.exp(grp[2 + h] - m)
                dn = jnp.sum(ps[h] + ps[2 + h], axis=-1, keepdims=True)
                den.append(dn)
                lse.append(m + jnp.log(dn))
            pmat = jnp.concatenate([x.astype(BF16) for x in ps], axis=1)
            o = _dot(pmat, vbd[keys_of(p, n), :])
            if pending is not None:
                finish(*pending)
            pending = (p, n, o, den, lse)
        finish(*pending)

    def which(w):
        return pl.BlockSpec((None, qt, ATT_W), lambda i: (w, i, 0))

    def halo(w):
        return pl.BlockSpec((None, BLK, ATT_W), lambda i: (w, jnp.maximum(i * nq - 1, 0), 0))

    out = pl.BlockSpec((qt, ATT_W), lambda i: (i, 0))
    bd = pltpu.VMEM((4 * (nq + 1) * 2 * BLK, LANES), BF16)
    return pl.pallas_call(
        body, name=name, grid=(s_len // qt,),
        in_specs=[which(0), which(1), which(2), halo(1), halo(2)],
        out_specs=[out, out], out_shape=[jax.ShapeDtypeStruct((s_len, ATT_W), F32)] * 2,
        scratch_shapes=[bd, bd],
        compiler_params=_params(("parallel",)),
    )(qkv, qkv, qkv, qkv, qkv)


def _attention_backward(qkv, d_att, lse, delta, seq_len, qt, name):
    s_len = qkv.shape[1]
    nq = qt // BLK
    tiles_per_seq = seq_len // qt
    nblk = s_len // BLK

    def body(q_ref, k_ref, v_ref, kh_ref, vh_ref, do_ref, lse_ref, dl_ref,
             qn_ref, don_ref, lsen_ref, dln_ref, dqkv_ref, kbd, vbd):
        i = pl.program_id(0)
        first = jnp.where((i % tiles_per_seq) == 0, 4 * BLK, 0)
        last = jnp.where((i % tiles_per_seq) == tiles_per_seq - 1, 4 * BLK, 0)
        head0, band, prev = _att_masks()
        band_first = band - prev * first
        band_tail = band[:, 0:2 * BLK] - last
        for p in range(4):
            _fill_block_diagonal(kbd, p, k_ref, kh_ref, head0, nq)
            _fill_block_diagonal(vbd, p, v_ref, vh_ref, head0, nq)
        units = [(p, n) for p in range(4) for n in range(nq + 1)]

        def stage_a(p, n):
            sl = slice(p * LANES, (p + 1) * LANES)
            base = (p * (nq + 1) + n) * 2 * BLK
            if n < nq:
                rows = slice(n * BLK, (n + 1) * BLK)
                q2, do2, lse2, dl2 = q_ref[rows, sl], do_ref[rows, sl], lse_ref[rows, sl], dl_ref[rows, sl]
                keys = slice(base, base + 4 * BLK)
            else:
                q2, do2, lse2, dl2 = qn_ref[:, sl], don_ref[:, sl], lsen_ref[:, sl], dln_ref[:, sl]
                keys = slice(base, base + 2 * BLK)
            s = _dot_nt(q2 * 0.125, kbd[keys, :])
            dp = _dot_nt(do2, vbd[keys, :])
            return q2, do2, lse2, dl2, keys, s, dp

        def stage_b(n, lse2, dl2, s, dp):
            mask = band_tail if n == nq else (band_first if n == 0 else band)
            ps, dss = [], []
            for g in range(s.shape[1] // BLK):
                h = g % 2
                cols = slice(g * BLK, (g + 1) * BLK)
                pg = jnp.exp(jnp.where(mask[:, cols] >= 0, s[:, cols] - lse2[:, h * 64:h * 64 + 1], NEG))
                ps.append(pg.astype(BF16))
                dss.append((pg * (dp[:, cols] - dl2[:, h * 64:h * 64 + 1]) * 0.125).astype(BF16))
            return jnp.concatenate(ps, axis=1), jnp.concatenate(dss, axis=1)

        def heads(r, g):
            return jnp.where(head0, r[g * BLK:(g + 1) * BLK, :], r[(g + 1) * BLK:(g + 2) * BLK, :])

        a_next = stage_a(*units[0])
        carry = None
        for ui, (p, n) in enumerate(units):
            q2, do2, lse2, dl2, keys, s, dp = a_next
            if ui + 1 < len(units):
                a_next = stage_a(*units[ui + 1])
            pmat, dsmat = stage_b(n, lse2, dl2, s, dp)
            sl = slice(p * LANES, (p + 1) * LANES)
            if n < nq:
                dqkv_ref[0, n * BLK:(n + 1) * BLK, sl] = _dot(dsmat, kbd[keys, :])
            dkbd = _dot_tn(dsmat, q2)
            dvbd = _dot_tn(pmat, do2)
            if n > 0:
                prow = slice((n - 1) * BLK, n * BLK)
                dqkv_ref[1, prow, sl] = carry[0] + heads(dkbd, 0)
                dqkv_ref[2, prow, sl] = carry[1] + heads(dvbd, 0)
            if n < nq:
                carry = (heads(dkbd, 2), heads(dvbd, 2))

    def which(w):
        return pl.BlockSpec((None, qt, ATT_W), lambda i: (w, i, 0))

    def prev(w):
        return pl.BlockSpec((None, BLK, ATT_W), lambda i: (w, jnp.maximum(i * nq - 1, 0), 0))

    tile = pl.BlockSpec((qt, ATT_W), lambda i: (i, 0))
    nxt = pl.BlockSpec((BLK, ATT_W), lambda i: (jnp.minimum((i + 1) * nq, nblk - 1), 0))
    nxt_q = pl.BlockSpec((None, BLK, ATT_W), lambda i: (0, jnp.minimum((i + 1) * nq, nblk - 1), 0))
    return pl.pallas_call(
        body, name=name, grid=(s_len // qt,),
        in_specs=[which(0), which(1), which(2), prev(1), prev(2), tile, tile, tile, nxt_q, nxt, nxt, nxt],
        out_specs=pl.BlockSpec((3, qt, ATT_W), lambda i: (0, i, 0)),
        out_shape=jax.ShapeDtypeStruct((3, s_len, ATT_W), F32),
        scratch_shapes=[pltpu.VMEM((4 * (nq + 1) * 2 * BLK, LANES), BF16)] * 2,
        compiler_params=_params(("parallel",)),
    )(qkv, qkv, qkv, qkv, qkv, d_att, lse, delta, qkv, d_att, lse, delta)


def _merge_and_head(o_g, lse_g, p_gate, y_conv, x, tgt, modv, final_g, w_ao, w_o, ts):
    s_len, dm = x.shape
    gw = ATT_W + 2 * dm
    nt = s_len // ts
    gate_off = 3 * dm + 9 * ATT_W
    assert gate_off % gw == 0

    def body(o0, o1, o2, l0, l1, l2, pg_ref, yc_ref, x_ref, t_ref, mod_ref, fg_ref, wao_ref, wo_ref,
             loss_ref, dx_ref, dyc_ref, dpg_ref, da0, da1, da2, ls0, ls1, ls2, de0, de1, de2,
             gwo_ref, gwao_ref, gs_ref, slab):
        i = pl.program_id(0)

        @pl.when(i == 0)
        def _():
            loss_ref[...] = jnp.zeros_like(loss_ref)
            gwo_ref[...] = jnp.zeros_like(gwo_ref)
            gwao_ref[...] = jnp.zeros_like(gwao_ref)
            gs_ref[...] = jnp.zeros_like(gs_ref)

        os_, ls_ = [], []
        for dil, o_ref, l_ref in zip(DILATIONS, (o0, o1, o2), (l0, l1, l2)):
            os_.append(jnp.concatenate(_interleave_load(o_ref, slab, dil, ts), axis=1))
            ls_.append(jnp.concatenate(_interleave_load(l_ref, slab, dil, ts), axis=1))
        mx = jnp.maximum(jnp.maximum(ls_[0], ls_[1]), ls_[2])
        wts = [jnp.exp(l - mx) for l in ls_]
        wsum = wts[0] + wts[1] + wts[2]
        att = (wts[0] * os_[0] + wts[1] * os_[1] + wts[2] * os_[2]) / wsum
        lse_all = mx + jnp.log(wsum)

        z_att, g_conv, g_att = pg_ref[:, 0:ATT_W], pg_ref[:, ATT_W:ATT_W + dm], pg_ref[:, ATT_W + dm:gw]
        sgz = _sig(z_att)
        sz = z_att * sgz
        a_att = (att * sz).astype(BF16)
        y_att = _dot(a_att, wao_ref[...])
        y_conv = yc_ref[...]
        sgc, sga = _sig(g_conv), _sig(g_att)
        merged = (sgc * y_conv + sga * y_att).astype(BF16)
        mo = _dot(merged, wo_ref[...])
        gate = mod_ref[2:3, :]
        x2 = x_ref[...] + gate * mo
        r = lax.rsqrt(jnp.mean(x2 * x2, axis=-1, keepdims=True) + EPS)
        xr = x2 * r
        err = xr * fg_ref[...] - t_ref[...]
        loss_ref[...] += 0.5 * jnp.sum(jnp.mean(err * err, axis=-1, keepdims=True))
        dy = err * (1.0 / dm)
        gs_ref[0:1, :] += jnp.sum(dy * xr, axis=0, keepdims=True)
        dyg = dy * fg_ref[...]
        d_x2 = r * dyg - xr * (r * jnp.mean(dyg * xr, axis=-1, keepdims=True))
        dx_ref[...] = d_x2
        gs_ref[1:2, :] += jnp.sum(d_x2 * mo, axis=0, keepdims=True)
        d_mo = (d_x2 * gate).astype(BF16)
        d_mg = _dot_nt(d_mo, wo_ref[...])
        gwo_ref[...] += _dot_tn(merged, d_mo)
        dyc_ref[...] = (d_mg * sgc).astype(BF16)
        dpg_ref[:, ATT_W:ATT_W + dm] = (d_mg * y_conv * sgc * (1.0 - sgc)).astype(BF16)
        d_ya = (d_mg * sga).astype(BF16)
        dpg_ref[:, ATT_W + dm:gw] = (d_mg * y_att * sga * (1.0 - sga)).astype(BF16)
        gwao_ref[...] += _dot_tn(a_att, d_ya)
        d_aa = _dot_nt(d_ya, wao_ref[...])
        dpg_ref[:, 0:ATT_W] = (d_aa * att * _dsilu(z_att, sgz)).astype(BF16)
        d_att = d_aa * sz
        ri = lax.broadcasted_iota(jnp.int32, (ATT_W, ATT_W), 0) // 64
        ci = lax.broadcasted_iota(jnp.int32, (ATT_W, ATT_W), 1) // 64
        ones = jnp.where(ri == ci, 1.0, 0.0).astype(BF16)
        prod = d_att * att
        hi = prod.astype(BF16)
        lo = (prod - hi.astype(F32)).astype(BF16)
        delta = _dot(hi, ones) + _dot(lo, ones)
        for val, refs, dt in ((d_att, (da0, da1, da2), BF16), (lse_all, (ls0, ls1, ls2), F32),
                              (delta, (de0, de1, de2), F32)):
            vals = [val[:, s * LANES:(s + 1) * LANES] for s in range(4)]
            for dil, ref in zip(DILATIONS, refs):
                _deinterleave_store(vals, slab, ref, dil, ts, dt)

    def grp(dil):
        return pl.BlockSpec((dil, ts // dil, ATT_W), lambda i: (0, i, 0))

    tile = pl.BlockSpec((ts, dm), lambda i: (i, 0))
    gate_tile = pl.BlockSpec((ts, gw), lambda i: (i, 0))
    const = lambda shp: pl.BlockSpec(shp, lambda i: tuple(0 for _ in shp))
    grp_shape = lambda dt: [jax.ShapeDtypeStruct((dil, s_len // dil, ATT_W), dt) for dil in DILATIONS]
    return pl.pallas_call(
        body, name="merge_and_head", grid=(nt,),
        in_specs=[grp(d) for d in DILATIONS] * 2
        + [gate_tile, tile, tile, tile, const((8, dm)), const((1, dm)), const((ATT_W, dm)), const((dm, dm))],
        out_specs=[const((8, LANES)), tile, tile, pl.BlockSpec((ts, gw), lambda i: (i, gate_off // gw))]
        + [grp(d) for d in DILATIONS] * 3
        + [const((dm, dm)), const((ATT_W, dm)), const((8, dm))],
        out_shape=[jax.ShapeDtypeStruct((8, LANES), F32), jax.ShapeDtypeStruct((s_len, dm), F32),
                   jax.ShapeDtypeStruct((s_len, dm), BF16), jax.ShapeDtypeStruct((s_len, gate_off + gw), BF16)]
        + grp_shape(BF16) + grp_shape(F32) + grp_shape(F32)
        + [jax.ShapeDtypeStruct((dm, dm), F32), jax.ShapeDtypeStruct((ATT_W, dm), F32),
           jax.ShapeDtypeStruct((8, dm), F32)],
        scratch_shapes=[pltpu.VMEM((4, ts, LANES), F32)],
        compiler_params=_params(("arbitrary",)),
    )(*o_g, *lse_g, p_gate, y_conv, x, tgt, modv, final_g, w_ao, w_o)


def _qkv_grad_to_tokens(dp, dqkv_g, pos, ropetab, ts):
    s_len, dm = pos.shape[0], (dp.shape[1] - 10 * ATT_W) // 5
    qw = 3 * ATT_W
    assert (3 * dm) % qw == 0

    def body(dp_in, g0, g1, g2, pos_ref, tab_ref, o_ref, slab, rc, ra, rb):
        del dp_in
        w = pl.program_id(1)

        @pl.when(w == 0)
        def _():
            c_, a_, b_ = _rope_tables(pos_ref, tab_ref)
            rc[...] = c_
            ra[...] = a_
            rb[...] = b_

        def emit(roped):
            for gi, (dil, g_ref) in enumerate(zip(DILATIONS, (g0, g1, g2))):
                vals = _interleave_load(g_ref, slab, dil, ts)
                for s in range(4):
                    t = vals[s]
                    if roped:
                        t = t * rc[...] + pltpu.roll(t * ra[...], 8, 1) + pltpu.roll(t * rb[...], LANES - 8, 1)
                    col = gi * ATT_W + s * LANES
                    o_ref[:, col:col + LANES] = t.astype(BF16)

        pl.when(w < 2)(lambda: emit(True))
        pl.when(w == 2)(lambda: emit(False))

    return pl.pallas_call(
        body, name="qkv_grad_to_tokens", grid=(s_len // ts, 3),
        in_specs=[ANY] + [pl.BlockSpec((None, dil, ts // dil, ATT_W), lambda i, w: (w, 0, i, 0)) for dil in DILATIONS]
        + [pl.BlockSpec((ts, 1), lambda i, w: (i, 0)), pl.BlockSpec((8, LANES), lambda i, w: (0, 0))],
        out_specs=pl.BlockSpec((ts, qw), lambda i, w: (i, 3 * dm // qw + w)),
        out_shape=jax.ShapeDtypeStruct(dp.shape, BF16),
        input_output_aliases={0: 0},
        scratch_shapes=[pltpu.VMEM((4, ts, LANES), F32)] + [pltpu.VMEM((ts, LANES), F32)] * 3,
        compiler_params=_params(("parallel", "arbitrary")),
    )(dp, *dqkv_g, pos, ropetab)


def _wide_col_tile(cols):
    return 2 * COL_TILE if cols % (2 * COL_TILE) == 0 else COL_TILE


def _input_grad(dp, w_in, x, dx_res, modv, norm_g, ts):
    s_len, dm = x.shape
    ct = _wide_col_tile(dp.shape[1])
    nct = dp.shape[1] // ct

    def body(p_ref, w_ref, x_ref, dxr_ref, mod_ref, g_ref, gx_ref, gs_ref, acc):
        i, j = pl.program_id(0), pl.program_id(1)

        @pl.when((i == 0) & (j == 0))
        def _():
            gs_ref[...] = jnp.zeros_like(gs_ref)

        @pl.when(j == 0)
        def _():
            acc[...] = jnp.zeros_like(acc)

        acc[...] += _dot_nt(p_ref[...], w_ref[...])

        @pl.when(j == nct - 1)
        def _():
            d_h = acc[...]
            xv = x_ref[...]
            r = lax.rsqrt(jnp.mean(xv * xv, axis=-1, keepdims=True) + EPS)
            xr = xv * r
            gs_ref[0:1, :] += jnp.sum(d_h, axis=0, keepdims=True)
            gs_ref[1:2, :] += jnp.sum(d_h * (xr * g_ref[...]), axis=0, keepdims=True)
            d_n = d_h * (1.0 + mod_ref[1:2, :])
            gs_ref[2:3, :] += jnp.sum(d_n * xr, axis=0, keepdims=True)
            dxn = d_n * g_ref[...]
            gx_ref[...] = dxr_ref[...] + r * dxn - xr * (r * jnp.mean(dxn * xr, axis=-1, keepdims=True))

    tile = pl.BlockSpec((ts, dm), lambda i, j: (i, 0))
    return pl.pallas_call(
        body, name="input_grad", grid=(s_len // ts, nct),
        in_specs=[pl.BlockSpec((ts, ct), lambda i, j: (i, j)), pl.BlockSpec((dm, ct), lambda i, j: (0, j)), tile, tile,
                  pl.BlockSpec((8, dm), lambda i, j: (0, 0)), pl.BlockSpec((1, dm), lambda i, j: (0, 0))],
        out_specs=[tile, pl.BlockSpec((8, dm), lambda i, j: (0, 0))],
        out_shape=[jax.ShapeDtypeStruct((s_len, dm), F32), jax.ShapeDtypeStruct((8, dm), F32)],
        scratch_shapes=[pltpu.VMEM((ts, dm), F32)],
        compiler_params=_params(("arbitrary", "arbitrary")),
    )(dp, w_in, x, dx_res, modv, norm_g)


def _w_in_grad(h_t, dp, ts):
    dm, s_len = h_t.shape
    ct = _wide_col_tile(dp.shape[1])

    def body(h_ref, p_ref, o_ref):
        @pl.when(pl.program_id(1) == 0)
        def _():
            o_ref[...] = jnp.zeros_like(o_ref)

        o_ref[...] += _dot(h_ref[...], p_ref[...])

    return pl.pallas_call(
        body, name="w_in_grad", grid=(dp.shape[1] // ct, s_len // ts),
        in_specs=[pl.BlockSpec((dm, ts), lambda j, i: (0, i)), pl.BlockSpec((ts, ct), lambda j, i: (i, j))],
        out_specs=pl.BlockSpec((dm, ct), lambda j, i: (0, j)),
        out_shape=jax.ShapeDtypeStruct((dm, dp.shape[1]), F32),
        compiler_params=_params(("arbitrary", "arbitrary")),
    )(h_t, dp)


def _rope_lane_table():
    l64 = jnp.arange(LANES) % 64
    half = ROT_DIM // 2
    inv_freq = ROPE_THETA ** (-(jnp.arange(half, dtype=F32) * 2.0 / ROT_DIM))
    rot = l64 < ROT_DIM
    rows = [jnp.where(rot, inv_freq[l64 % half], 0.0), (l64 < half).astype(F32),
            ((l64 >= half) & rot).astype(F32), rot.astype(F32)]
    return jnp.concatenate([jnp.stack(rows), jnp.zeros((4, LANES), F32)], axis=0)


def _tile_sizes(s_len):
    l_min = s_len // DILATIONS[-1]
    ts_big = min(1024, s_len // 2)
    ts_mid = 256
    ts_head = 256
    qt = min(512, l_min)
    return ts_big, ts_mid, ts_head, qt


def kernel(x, c, positions, norm_g, w_ada, b_ada, w_in, conv_w, conv_b, conv_ln_g, conv_ln_b, w_conv_out, w_att_out, w_o, final_g, loss_target, m_norm_g, m_w_ada, m_b_ada, m_w_in, m_conv_w, m_conv_b, m_conv_ln_g, m_conv_ln_b, m_w_conv_out, m_w_att_out, m_w_o, m_final_g, v_norm_g, v_w_ada, v_b_ada, v_w_in, v_conv_w, v_conv_b, v_conv_ln_g, v_conv_ln_b, v_w_conv_out, v_w_att_out, v_w_o, v_final_g):
    s_len, dm = x.shape[1], x.shape[2]
    ts_big, ts_mid, ts_head, qt = _tile_sizes(s_len)
    xi, yi, cidx = _my_place()
    chip = 2 * xi + yi
    batch = 4 * xi + 2 * yi + cidx
    x2d, tgt = x[0], loss_target[0]
    pos = positions.reshape(s_len, 1)
    wa_l, wi_l, cw_l = w_ada[0], w_in[0], conv_w[0]
    wco_l, wao_l, wo_l = w_conv_out[0], w_att_out[0], w_o[0]
    ada_w = wa_l.shape[1]
    cw_cols = cw_l.shape[1]

    cw_pad = jnp.pad(cw_l, ((0, HALO - CONV_K), (0, 0)))
    small_in = jnp.concatenate([jnp.broadcast_to(c, (8, dm)), cw_pad.reshape(8, dm)], axis=0)
    small = _allgather_small(small_in).reshape(8, 16, dm)
    c_all = small[:, 0, :]
    conv_w_full = jnp.concatenate(
        [small[2 * p, 8:16, :].reshape(HALO, cw_cols) for p in range(4)], axis=1)
    b_l = lax.dynamic_slice(b_ada, (0, chip * ada_w), (1, ada_w))
    mod_parts = _allgather_small(_mod_part(c_all, wa_l, b_l)).reshape(8, 8, ada_w)
    mod_rows = lax.dynamic_index_in_dim(mod_parts, batch, axis=1, keepdims=False)
    mod = jnp.concatenate([mod_rows[2 * p] for p in range(4)], axis=0).reshape(3, dm)
    modv = jnp.concatenate([mod, jnp.zeros((5, dm), F32)], axis=0)

    kinds = ("col", "row", "col", "row")
    w_in_b, w_co_b, w_ao_b, w_o_b = _allgather_weights([wi_l, wco_l, wao_l, wo_l], kinds)

    ropetab = _rope_lane_table()
    p_conv, p_gate, h_b, h_t = _input_projection(x2d, modv, norm_g, w_in_b, ts_big)
    qkv_g = [_qkv_projection(h_b, pos, ropetab, w_in_b, gi, ts_big) for gi in range(3)]
    y_conv, u0, u1 = _conv_forward(p_conv, conv_w_full, conv_b, conv_ln_g, conv_ln_b, w_co_b, ts_mid)
    qkv_flat = [q.reshape(3, s_len, ATT_W) for q in qkv_g]
    o_g, lse_g = [], []
    for gi, dil in enumerate(DILATIONS):
        o, l = _attention_forward(qkv_flat[gi], s_len // dil, qt, "attention_forward_%d" % dil)
        o_g.append(o.reshape(dil, s_len // dil, ATT_W))
        lse_g.append(l.reshape(dil, s_len // dil, ATT_W))

    (loss_p, dx_res, dyc, dp, da0, da1, da2, ls0, ls1, ls2, de0, de1, de2,
     g_wo, g_wao, head_sums) = _merge_and_head(o_g, lse_g, p_gate, y_conv, x2d, tgt, modv,
                                               final_g.reshape(1, dm), w_ao_b, w_o_b, ts_head)
    loss = lax.psum(loss_p[0, 0], ("x", "y", "c"))

    dp, g_wco, conv_sums, g_cw = _conv_backward(dp, dyc, p_conv, u0, u1, conv_w_full, conv_ln_g, conv_ln_b,
                                                w_co_b, ts_mid)
    dqkv_g = []
    for gi, (dil, da, ls, de) in enumerate(zip(DILATIONS, (da0, da1, da2), (ls0, ls1, ls2), (de0, de1, de2))):
        flat = lambda a: a.reshape(s_len, ATT_W)
        dqkv = _attention_backward(qkv_flat[gi], flat(da), flat(ls), flat(de), s_len // dil, qt,
                                   "attention_backward_%d" % dil)
        dqkv_g.append(dqkv.reshape(3, dil, s_len // dil, ATT_W))
    dp = _qkv_grad_to_tokens(dp, dqkv_g, pos, ropetab, 2 * ts_mid)
    g_win = _w_in_grad(h_t, dp, ts_big)

    grads = [g_win, g_wco, g_wao, g_wo]
    full_shapes = [g.shape for g in grads]
    c_arr = jnp.reshape(cidx, (1,)).astype(jnp.int32)
    recv_halves = _reduce_pair_exchange(grads, kinds)
    halves = [_reduce_pair_sum(g, pa, kind, c_arr, "reduce_pair_sum_%d" % k)
              for k, (g, pa, kind) in enumerate(zip(grads, recv_halves, kinds))]
    send_sems, recv_sems, halves, lands, token = _reduce_to_owner_start(halves, kinds, full_shapes)
    grad_x, in_sums = _input_grad(dp, w_in_b, x2d, dx_res, modv + token[0, 0], norm_g, ts_big)
    halves, recvd = _reduce_to_owner_wait(send_sems, recv_sems, halves, lands, in_sums, kinds)
    gr_win, gr_wco, gr_wao, gr_wo = _reduce_finish(halves, recvd, kinds, full_shapes)

    rows = [in_sums[2:3], conv_sums[2:3], conv_sums[0:1], conv_sums[1:2], head_sums[0:1],
            in_sums[0:1], in_sums[1:2], head_sums[1:2], g_cw, jnp.zeros((8, dm), F32)]
    part = jnp.concatenate(rows, axis=0)
    gathered = _allgather_small(part)
    tot = _sum_devices(gathered, 48)
    dmod_all = gathered.reshape(8, 48, dm)[:, 5:8, :].reshape(8, 3 * dm)
    dmod_l = lax.dynamic_slice(dmod_all, (0, chip * ada_w), (8, ada_w))
    gr_wada = _grad_w_ada(c_all.T, dmod_l)
    gr_cw = lax.dynamic_slice(tot[8:8 + HALO], (0, chip * cw_cols), (HALO, cw_cols))

    def pack8(ng, cb, lg, lb, fg, ba):
        return jnp.concatenate([ng, cb, lg, lb, fg.reshape(1, dm), ba.reshape(3, dm)], axis=0)

    pad_cw = lambda a: jnp.pad(a[0], ((0, HALO - CONV_K), (0, 0)))
    w8 = pack8(norm_g, conv_b, conv_ln_g, conv_ln_b, final_g, b_ada)
    m8 = pack8(m_norm_g, m_conv_b, m_conv_ln_g, m_conv_ln_b, m_final_g, m_b_ada)
    v8 = pack8(v_norm_g, v_conv_b, v_conv_ln_g, v_conv_ln_b, v_final_g, v_b_ada)
    g8 = tot[0:8]
    upd = {
        "small": _adamw(w8, g8, m8, v8, "adamw_small"),
        "w_ada": _adamw(wa_l, gr_wada, m_w_ada[0], v_w_ada[0], "adamw_w_ada"),
        "w_in": _adamw(wi_l, gr_win, m_w_in[0], v_w_in[0], "adamw_w_in"),
        "conv_w": _adamw(cw_pad, gr_cw, pad_cw(m_conv_w), pad_cw(v_conv_w), "adamw_conv_w"),
        "w_co": _adamw(wco_l, gr_wco, m_w_conv_out[0], v_w_conv_out[0], "adamw_w_conv_out"),
        "w_ao": _adamw(wao_l, gr_wao, m_w_att_out[0], v_w_att_out[0], "adamw_w_att_out"),
        "w_o": _adamw(wo_l, gr_wo, m_w_o[0], v_w_o[0], "adamw_w_o"),
    }

    def family(which):
        if which is None:
            sm = g8
            big = {"w_ada": gr_wada, "w_in": gr_win, "conv_w": gr_cw, "w_co": gr_wco, "w_ao": gr_wao, "w_o": gr_wo}
        else:
            sm = upd["small"][which]
            big = {k: upd[k][which] for k in ("w_ada", "w_in", "conv_w", "w_co", "w_ao", "w_o")}
        return [sm[0:1], big["w_ada"][None], sm[5:8].reshape(1, 3 * dm), big["w_in"][None],
                big["conv_w"][None, :CONV_K], sm[1:2], sm[2:3], sm[3:4], big["w_co"][None],
                big["w_ao"][None], big["w_o"][None], sm[4]]

    return (loss, grad_x[None], *family(None), *family(0), *family(1), *family(2))
```

```python
import jax
import jax.numpy as jnp
from jax import lax
from jax.experimental import pallas as pl
from jax.experimental.pallas import tpu as pltpu

F32 = jnp.float32
BF16 = jnp.bfloat16
MESH = pl.DeviceIdType.MESH
ANY = pl.BlockSpec(memory_space=pl.ANY)
VM = pl.BlockSpec(memory_space=pltpu.VMEM)

EPS = 1e-6
NEG = -1e30
ATT_W = 512
DILATIONS = (1, 4, 16)
BLK = 128
CONV_K = 31
HALO = 32
CONV_ROWS_FWD = 32
CONV_ROWS_BWD = 16
ROT_DIM = 16
ROPE_THETA = 500000.0
COL_TILE = 512
LANES = 128
VMEM_LIMIT = 56 * 1024 * 1024

ADAM_LR, ADAM_B1, ADAM_B2, ADAM_EPS, ADAM_WD, ADAM_STEP = 0.001, 0.9, 0.999, 1e-08, 0.01, 10


def _params(sem=None, vmem=VMEM_LIMIT):
    return pltpu.CompilerParams(dimension_semantics=sem, vmem_limit_bytes=vmem)


def _dot(a, b):
    return jnp.dot(a, b, preferred_element_type=F32)


def _dot_nt(a, b):
    return lax.dot_general(a, b, (((1,), (1,)), ((), ())), preferred_element_type=F32)


def _dot_tn(a, b):
    return lax.dot_general(a, b, (((0,), (0,)), ((), ())), preferred_element_type=F32)


def _sig(x):
    return jax.nn.sigmoid(x)


def _dsilu(x, s):
    return s * (1.0 + x * (1.0 - s))


def _my_place():
    return lax.axis_index("x"), lax.axis_index("y"), lax.axis_index("c")


def _allgather_small(x_shard):
    m_per, n = x_shard.shape

    def body(x_ref, out_ref, send_sems, recv_sems, local_sem):
        x, y, c = _my_place()
        me, sibling = (x, y, c), (x, y, 1 - c)
        chips = [(1 - x, y), (x, 1 - y), (1 - x, 1 - y)]

        def rows(px, py, pc):
            return out_ref.at[pl.ds((4 * px + 2 * py + pc) * m_per, m_per), :]

        def copy(k, block, to, src=None):
            return pltpu.make_async_remote_copy(
                src_ref=rows(*block) if src is None else src, dst_ref=rows(*block),
                send_sem=send_sems.at[k], recv_sem=recv_sems.at[k],
                device_id=to, device_id_type=MESH)

        mine = pltpu.make_async_copy(x_ref, rows(*me), local_sem)
        mine.start()
        first = [copy(0, me, sibling, src=x_ref)]
        first += [copy(1 + j, me, (*chip, c), src=x_ref) for j, chip in enumerate(chips)]
        for cp in first:
            cp.start()
        passed = [copy(4 + j, (*chip, c), sibling) for j, chip in enumerate(chips)]
        for j, chip in enumerate(chips):
            copy(1 + j, (*chip, c), me).wait_recv()
            passed[j].start()
        copy(0, sibling, me).wait_recv()
        for j, chip in enumerate(chips):
            copy(4 + j, (*chip, 1 - c), me).wait_recv()
        for cp in first + passed:
            cp.wait_send()
        mine.wait()

    return pl.pallas_call(
        body, name="allgather_small",
        out_shape=jax.ShapeDtypeStruct((8 * m_per, n), x_shard.dtype),
        in_specs=[VM], out_specs=VM,
        scratch_shapes=[pltpu.SemaphoreType.DMA((7,)), pltpu.SemaphoreType.DMA((7,)),
                        pltpu.SemaphoreType.DMA],
    )(x_shard)


def _shard_window(ref, kind, p, n_shards=4):
    r, c = ref.shape
    if kind == "col":
        w = c // n_shards
        return ref.at[:, pl.ds(p * w, w)]
    w = r // n_shards
    return ref.at[pl.ds(p * w, w), :]


def _half_window(ref, kind, hc):
    r, c = ref.shape
    if kind == "col":
        return ref.at[pl.ds(hc * (r // 2), r // 2), :]
    return ref.at[:, pl.ds(hc * (c // 2), c // 2)]


def _landed(ref, kind, chip, hc):
    return _half_window(_shard_window(ref, kind, 2 * chip[0] + chip[1]), kind, hc)


def _cast_weights(shards, kinds):
    n = len(shards)
    full_shapes = [(s.shape[0], 4 * s.shape[1]) if kind == "col" else (4 * s.shape[0], s.shape[1])
                   for s, kind in zip(shards, kinds)]

    def body(*refs):
        w_refs, out_refs, bf_refs, sems = refs[:n], refs[n:2 * n], refs[2 * n:3 * n], refs[3 * n]
        x, y, _ = _my_place()
        cps = []
        for k in range(n):
            bf_refs[k][...] = w_refs[k][...].astype(BF16)
            cp = pltpu.make_async_copy(bf_refs[k], _shard_window(out_refs[k], kinds[k], 2 * x + y), sems.at[k])
            cp.start()
            cps.append(cp)
        for cp in cps:
            cp.wait()

    return pl.pallas_call(
        body, name="cast_weights",
        out_shape=[jax.ShapeDtypeStruct(s, BF16) for s in full_shapes],
        in_specs=[VM] * n, out_specs=[ANY] * n,
        scratch_shapes=[pltpu.VMEM(s.shape, BF16) for s in shards] + [pltpu.SemaphoreType.DMA((n,))],
        compiler_params=_params(),
    )(*shards)


def _gather_copies(refs, kinds, sems_a, sems_b):
    x, y, c = _my_place()
    chips = [(1 - x, y), (x, 1 - y), (1 - x, 1 - y)]
    set_a, set_b = [], []
    for j, chip in enumerate(chips):
        for k in range(len(refs)):
            if refs[k] is None:
                continue
            for flip in ((0,) if k == 0 else (0, 1)):
                win = _landed(refs[k], kinds[k], (x, y), c)
                sems, idx = (sems_a, j) if k == 0 else (sems_b, ((k - 1) * 3 + j) * 2 + flip)
                mine = _landed(refs[k], kinds[k], chip, (c + flip) % 2)
                (set_a if k == 0 else set_b).append((
                    pltpu.make_async_remote_copy(
                        src_ref=win, dst_ref=win, send_sem=sems[0].at[idx], recv_sem=sems[1].at[idx],
                        device_id=(*chip, (c + flip) % 2), device_id_type=MESH),
                    pltpu.make_async_remote_copy(
                        src_ref=mine, dst_ref=mine, send_sem=sems[0].at[idx], recv_sem=sems[1].at[idx],
                        device_id=(*chip, (c + flip) % 2), device_id_type=MESH)))
    return set_a, set_b


_SPLIT = dict(has_side_effects=pltpu.SideEffectType.DATAFLOW_SIDE_EFFECTING)


def _gather_weights_start(fulls, kinds):
    n = len(fulls)
    hbm = pl.BlockSpec(memory_space=pltpu.HBM)
    sem = pl.BlockSpec(memory_space=pltpu.SEMAPHORE)
    nb = (n - 1) * 6

    def body(*refs):
        in_refs = refs[:n]
        sa, ra, sb, rb = refs[n:n + 4]
        token = refs[-1]
        set_a, set_b = _gather_copies(in_refs, kinds, (sa, ra), (sb, rb))
        for out_cp, _ in set_a + set_b:
            out_cp.start()
        token[...] = jnp.zeros_like(token)

    out = pl.pallas_call(
        body, name="gather_weights_start",
        out_shape=[pltpu.SemaphoreType.DMA((3,)), pltpu.SemaphoreType.DMA((3,)),
                   pltpu.SemaphoreType.DMA((nb,)), pltpu.SemaphoreType.DMA((nb,))]
        + [pltpu.HBM(f.shape, f.dtype) for f in fulls] + [jax.ShapeDtypeStruct((8, LANES), F32)],
        in_specs=[hbm] * n, out_specs=[sem] * 4 + [hbm] * n + [VM],
        input_output_aliases={k: 4 + k for k in range(n)},
        compiler_params=pltpu.CompilerParams(**_SPLIT),
    )(*[pltpu.with_memory_space_constraint(f, pltpu.HBM) for f in fulls])
    return (out[0], out[1]), (out[2], out[3]), out[4:4 + n], out[-1]


def _gather_weights_wait(fulls, kinds, sems, which, after, name):
    n = len(fulls)
    hbm = pl.BlockSpec(memory_space=pltpu.HBM)
    sem = pl.BlockSpec(memory_space=pltpu.SEMAPHORE)
    keep = [0] if which == 0 else list(range(1, n))

    def body(*refs):
        m = len(keep)
        full_refs = [None] * n
        for pos_, k in enumerate(keep):
            full_refs[k] = refs[pos_]
        s_ref, r_ref = refs[m:m + 2]
        if which == 0:
            sets = _gather_copies([full_refs[0]], kinds[:1], (s_ref, r_ref), None)[0]
        else:
            sets = _gather_copies([None] + [full_refs[k] for k in keep], kinds, None, (s_ref, r_ref))[1]
        for out_cp, in_cp in sets:
            out_cp.wait_send()
            in_cp.wait_recv()

    out = pl.pallas_call(
        body, name=name,
        out_shape=[pltpu.HBM(fulls[k].shape, fulls[k].dtype) for k in keep],
        in_specs=[hbm] * len(keep) + [sem, sem, ANY], out_specs=[hbm] * len(keep),
        input_output_aliases={i: i for i in range(len(keep))},
        compiler_params=pltpu.CompilerParams(**_SPLIT),
    )(*[fulls[k] for k in keep], sems[0], sems[1], after)
    return list(out)


def _pass_to_sibling(w_full, kind):
    def body(w_in_ref, w_ref, send_sems, recv_sems):
        del w_in_ref
        x, y, c = _my_place()
        chips = [(1 - x, y), (x, 1 - y), (1 - x, 1 - y)]
        cps = []
        for j, chip in enumerate(chips):
            win = _landed(w_ref, kind, chip, c)
            cp = pltpu.make_async_remote_copy(
                src_ref=win, dst_ref=win, send_sem=send_sems.at[j], recv_sem=recv_sems.at[j],
                device_id=(x, y, 1 - c), device_id_type=MESH)
            cp.start()
            cps.append(cp)
        for j, chip in enumerate(chips):
            theirs = _landed(w_ref, kind, chip, 1 - c)
            pltpu.make_async_remote_copy(
                src_ref=theirs, dst_ref=theirs, send_sem=send_sems.at[j], recv_sem=recv_sems.at[j],
                device_id=(x, y, 1 - c), device_id_type=MESH).wait_recv()
        for cp in cps:
            cp.wait_send()

    return pl.pallas_call(
        body, name="pass_to_sibling",
        out_shape=jax.ShapeDtypeStruct(w_full.shape, w_full.dtype),
        in_specs=[ANY], out_specs=ANY, input_output_aliases={0: 0},
        scratch_shapes=[pltpu.SemaphoreType.DMA((3,)), pltpu.SemaphoreType.DMA((3,))],
    )(w_full)


def _reduce_pair_exchange(grads, kinds):
    n = len(grads)
    half_shapes = [(g.shape[0] // 2, g.shape[1]) if kind == "col" else (g.shape[0], g.shape[1] // 2)
                   for g, kind in zip(grads, kinds)]

    def body(*refs):
        g_refs, pa_refs = refs[:n], refs[n:2 * n]
        send_sems, recv_sems = refs[2 * n:]
        x, y, c = _my_place()
        cps = []
        for k in range(n):
            cp = pltpu.make_async_remote_copy(
                src_ref=_half_window(g_refs[k], kinds[k], 1 - c), dst_ref=pa_refs[k],
                send_sem=send_sems.at[k], recv_sem=recv_sems.at[k],
                device_id=(x, y, 1 - c), device_id_type=MESH)
            cp.start()
            cps.append(cp)
        for cp in cps:
            cp.wait()

    return pl.pallas_call(
        body, name="reduce_pair_exchange",
        out_shape=[jax.ShapeDtypeStruct(s, F32) for s in half_shapes],
        in_specs=[ANY] * n, out_specs=[ANY] * n,
        scratch_shapes=[pltpu.SemaphoreType.DMA((n,)), pltpu.SemaphoreType.DMA((n,))],
    )(*grads)


def _row_tile(rows, cols, itemsize=4, target=2 * 1024 * 1024, mult=16):
    t = rows
    while t % 2 == 0 and t // 2 >= mult and (t // 2) % mult == 0 and t * cols * itemsize > target:
        t //= 2
    return t


def _reduce_pair_sum(g, pa, kind, c_arr, name):
    hr, hc_ = pa.shape
    tr = _row_tile(hr, hc_)
    nb = hr // tr

    def body(c_ref, g_ref, pa_ref, o_ref):
        o_ref[...] = (g_ref[...] + pa_ref[...]).astype(BF16)

    if kind == "col":
        g_map = lambda i, c_ref: (c_ref[0] * nb + i, 0)
    else:
        g_map = lambda i, c_ref: (i, c_ref[0])
    return pl.pallas_call(
        body, name=name,
        grid_spec=pltpu.PrefetchScalarGridSpec(
            num_scalar_prefetch=1, grid=(nb,),
            in_specs=[pl.BlockSpec((tr, hc_), g_map), pl.BlockSpec((tr, hc_), lambda i, c_ref: (i, 0))],
            out_specs=pl.BlockSpec((tr, hc_), lambda i, c_ref: (i, 0))),
        out_shape=jax.ShapeDtypeStruct((hr, hc_), BF16),
        compiler_params=_params(("parallel",)),
    )(c_arr, g, pa)


def _half_shard_shape(full_shape, kind):
    r, c = full_shape
    return (r // 2, c // 4) if kind == "col" else (r // 4, c // 2)


def _to_owner_copies(h_refs, land_refs, send_sems, recv_sems, kinds):
    n = len(h_refs)
    x, y, c = _my_place()
    chips = [(1 - x, y), (x, 1 - y), (1 - x, 1 - y)]
    cps = []
    for j, chip in enumerate(chips):
        pj = 2 * chip[0] + chip[1]
        for k in range(n):
            cps.append(pltpu.make_async_remote_copy(
                src_ref=_shard_window(h_refs[k], kinds[k], pj), dst_ref=land_refs[k].at[j],
                send_sem=send_sems.at[j * n + k], recv_sem=recv_sems.at[j * n + k],
                device_id=(*chip, c), device_id_type=MESH))
    return cps


def _reduce_to_owner_start(halves, kinds, full_shapes):
    n = len(halves)
    hs = [_half_shard_shape(fs, kind) for fs, kind in zip(full_shapes, kinds)]
    hbm = pl.BlockSpec(memory_space=pltpu.HBM)
    sem = pl.BlockSpec(memory_space=pltpu.SEMAPHORE)

    def body(*refs):
        h_refs, land_refs = refs[:n], refs[n:2 * n]
        send_sems, recv_sems = refs[2 * n:2 * n + 2]
        token = refs[-1]
        for cp in _to_owner_copies(h_refs, land_refs, send_sems, recv_sems, kinds):
            cp.start()
        token[...] = jnp.zeros_like(token)

    lands = [pltpu.with_memory_space_constraint(lax.empty((3,) + s, BF16), pltpu.HBM) for s in hs]
    out = pl.pallas_call(
        body, name="reduce_to_owner_start",
        out_shape=[pltpu.SemaphoreType.DMA((3 * n,)), pltpu.SemaphoreType.DMA((3 * n,))]
        + [pltpu.HBM(h.shape, h.dtype) for h in halves] + [pltpu.HBM((3,) + s, BF16) for s in hs]
        + [jax.ShapeDtypeStruct((8, LANES), F32)],
        in_specs=[hbm] * (2 * n), out_specs=[sem, sem] + [hbm] * (2 * n) + [VM],
        input_output_aliases={k: 2 + k for k in range(2 * n)},
        compiler_params=pltpu.CompilerParams(has_side_effects=pltpu.SideEffectType.DATAFLOW_SIDE_EFFECTING),
    )(*[pltpu.with_memory_space_constraint(h, pltpu.HBM) for h in halves], *lands)
    return out[0], out[1], out[2:2 + n], out[2 + n:2 + 2 * n], out[-1]


def _reduce_to_owner_wait(send_sems, recv_sems, halves, lands, after, kinds):
    n = len(halves)
    hbm = pl.BlockSpec(memory_space=pltpu.HBM)
    sem = pl.BlockSpec(memory_space=pltpu.SEMAPHORE)

    def body(*refs):
        h_refs, land_refs = refs[:n], refs[n:2 * n]
        send_s, recv_s = refs[2 * n:2 * n + 2]
        for cp in _to_owner_copies(h_refs, land_refs, send_s, recv_s, kinds):
            cp.wait_send()
            cp.wait_recv()

    out = pl.pallas_call(
        body, name="reduce_to_owner_wait",
        out_shape=[pltpu.HBM(h.shape, h.dtype) for h in halves] + [pltpu.HBM(l.shape, l.dtype) for l in lands],
        in_specs=[hbm] * (2 * n) + [sem, sem, ANY], out_specs=[hbm] * (2 * n),
        input_output_aliases={k: k for k in range(2 * n)},
        compiler_params=pltpu.CompilerParams(has_side_effects=pltpu.SideEffectType.DATAFLOW_SIDE_EFFECTING),
    )(*halves, *lands, send_sems, recv_sems, after)
    return out[:n], out[n:]


def _reduce_finish(halves, recvd, kinds, full_shapes):
    n = len(halves)
    hs = [_half_shard_shape(fs, kind) for fs, kind in zip(full_shapes, kinds)]
    shard_shapes = [(fs[0], fs[1] // 4) if kind == "col" else (fs[0] // 4, fs[1])
                    for fs, kind in zip(full_shapes, kinds)]

    def body(*refs):
        h_refs, rc_refs, gs_refs = refs[:n], refs[n:2 * n], refs[2 * n:3 * n]
        own_refs, gh_refs = refs[3 * n:4 * n], refs[4 * n:5 * n]
        in_sems, loc_sems, send_sems, recv_sems = refs[5 * n:]
        x, y, c = _my_place()
        p = 2 * x + y
        loads = []
        for k in range(n):
            cp = pltpu.make_async_copy(_shard_window(h_refs[k], kinds[k], p), own_refs[k], in_sems.at[k])
            cp.start()
            loads.append(cp)
        outs = []
        for k in range(n):
            loads[k].wait()
            gh_refs[k][...] = (own_refs[k][...].astype(F32) + rc_refs[k][0].astype(F32)
                               + rc_refs[k][1].astype(F32) + rc_refs[k][2].astype(F32))
            dst = _half_window(gs_refs[k], kinds[k], c)
            lc = pltpu.make_async_copy(gh_refs[k], dst, loc_sems.at[k])
            lc.start()
            rc = pltpu.make_async_remote_copy(
                src_ref=gh_refs[k], dst_ref=dst, send_sem=send_sems.at[k], recv_sem=recv_sems.at[k],
                device_id=(x, y, 1 - c), device_id_type=MESH)
            rc.start()
            outs.append((lc, rc))
        for k, (lc, rc) in enumerate(outs):
            lc.wait()
            rc.wait_send()
            pltpu.make_async_remote_copy(
                src_ref=gh_refs[k], dst_ref=_half_window(gs_refs[k], kinds[k], 1 - c),
                send_sem=send_sems.at[k], recv_sem=recv_sems.at[k],
                device_id=(x, y, 1 - c), device_id_type=MESH).wait_recv()

    return pl.pallas_call(
        body, name="reduce_finish",
        out_shape=[jax.ShapeDtypeStruct(s, F32) for s in shard_shapes],
        in_specs=[ANY] * n + [VM] * n, out_specs=[ANY] * n,
        scratch_shapes=[pltpu.VMEM(s, BF16) for s in hs] + [pltpu.VMEM(s, F32) for s in hs]
        + [pltpu.SemaphoreType.DMA((n,)) for _ in range(4)],
        compiler_params=_params(),
    )(*halves, *recvd)


def _mod_part(c_all, w_ada_l, b_l):
    def body(c_ref, w_ref, b_ref, o_ref):
        o_ref[...] = _dot(c_ref[...].astype(BF16), w_ref[...].astype(BF16)) + b_ref[...]

    return pl.pallas_call(
        body, name="mod_part", out_shape=jax.ShapeDtypeStruct((8, w_ada_l.shape[1]), F32),
        in_specs=[VM, VM, VM], out_specs=VM, compiler_params=_params(),
    )(c_all, w_ada_l, b_l)


def _sum_devices(parts, m):
    def body(p_ref, o_ref):
        acc = p_ref[0:m, :]
        for d in range(1, 8):
            acc = acc + p_ref[d * m:(d + 1) * m, :]
        o_ref[...] = acc

    return pl.pallas_call(
        body, name="sum_devices", out_shape=jax.ShapeDtypeStruct((m, parts.shape[1]), F32),
        in_specs=[VM], out_specs=VM, compiler_params=_params(),
    )(parts)


def _grad_w_ada(c_all_t, dmod_l):
    d, w = c_all_t.shape[0], dmod_l.shape[1]

    def body(ct_ref, dm_ref, o_ref):
        acc = ct_ref[:, 0:1] * dm_ref[0:1, :]
        for b in range(1, 8):
            acc = acc + ct_ref[:, b:b + 1] * dm_ref[b:b + 1, :]
        o_ref[...] = acc

    return pl.pallas_call(
        body, name="grad_w_ada", out_shape=jax.ShapeDtypeStruct((d, w), F32),
        in_specs=[VM, VM], out_specs=VM, compiler_params=_params(),
    )(c_all_t, dmod_l)


def _adamw(w, g, m, v, name):
    r, c = w.shape
    tr = _row_tile(r, c, target=1024 * 1024, mult=8)
    c1 = 1.0 - ADAM_B1 ** ADAM_STEP
    c2 = 1.0 - ADAM_B2 ** ADAM_STEP

    def body(w_ref, g_ref, m_ref, v_ref, d_ref, nm_ref, nv_ref):
        gv = g_ref[...]
        nm = ADAM_B1 * m_ref[...] + (1.0 - ADAM_B1) * gv
        nv = ADAM_B2 * v_ref[...] + (1.0 - ADAM_B2) * (gv * gv)
        m_hat = nm / c1
        v_hat = nv / c2
        d_ref[...] = -ADAM_LR * (m_hat / (jnp.sqrt(v_hat) + ADAM_EPS) + ADAM_WD * w_ref[...])
        nm_ref[...] = nm
        nv_ref[...] = nv

    spec = pl.BlockSpec((tr, c), lambda i: (i, 0))
    return pl.pallas_call(
        body, name=name, grid=(r // tr,),
        out_shape=[jax.ShapeDtypeStruct((r, c), F32)] * 3,
        in_specs=[spec] * 4, out_specs=[spec] * 3,
        compiler_params=_params(("parallel",)),
    )(w, g, m, v)


def _rope_tables(pos_ref, tab_ref):
    ang = pos_ref[...].astype(F32) * tab_ref[0:1, :]
    cs, sn = jnp.cos(ang), jnp.sin(ang)
    return jnp.where(tab_ref[3:4, :] > 0, cs, 1.0), -sn * tab_ref[1:2, :], sn * tab_ref[2:3, :]


def _deinterleave_store(vals, slab, out_ref, d, ts, dtype):
    if d == 1:
        for s in range(4):
            out_ref[0, :, s * LANES:(s + 1) * LANES] = vals[s].astype(dtype)
        return
    for s in range(4):
        slab[s] = vals[s]
    for r in range(d):
        for s in range(4):
            out_ref[r, :, s * LANES:(s + 1) * LANES] = slab[s, pl.ds(r, ts // d, stride=d), :].astype(dtype)


def _interleave_load(blk_ref, slab, d, ts):
    if d == 1:
        return [blk_ref[0, :, s * LANES:(s + 1) * LANES] for s in range(4)]
    for r in range(d):
        for s in range(4):
            slab[s, pl.ds(r, ts // d, stride=d), :] = blk_ref[r, :, s * LANES:(s + 1) * LANES]
    return [slab[s] for s in range(4)]


def _input_projection(x, modv, norm_g, w_in, ts):
    s_len, d_model = x.shape
    ci = w_in.shape[1]
    ct = COL_TILE
    nc, ng = 3 * d_model // ct, (ATT_W + 2 * d_model) // ct
    assert ci == (nc + 9 + ng) * ct and s_len % ts == 0

    def body(x_ref, mod_ref, g_ref, w_ref, pc_ref, pg_ref, h_ref, ht_ref):
        j = pl.program_id(1)

        @pl.when(j == 0)
        def _():
            xv = x_ref[...]
            r = lax.rsqrt(jnp.mean(xv * xv, axis=-1, keepdims=True) + EPS)
            h = (xv * r) * g_ref[...] * (1.0 + mod_ref[1:2, :]) + mod_ref[0:1, :]
            h_ref[...] = h.astype(BF16)
            ht_ref[...] = h.T.astype(BF16)

        res = _dot(h_ref[...], w_ref[...])

        @pl.when(j < nc)
        def _():
            pc_ref[...] = res

        @pl.when(j >= nc)
        def _():
            pg_ref[...] = res

    return pl.pallas_call(
        body, name="input_projection", grid=(s_len // ts, nc + ng),
        in_specs=[pl.BlockSpec((ts, d_model), lambda i, j: (i, 0)),
                  pl.BlockSpec((8, d_model), lambda i, j: (0, 0)),
                  pl.BlockSpec((1, d_model), lambda i, j: (0, 0)),
                  pl.BlockSpec((d_model, ct), lambda i, j: (0, jnp.where(j < nc, j, j + 9)))],
        out_specs=[pl.BlockSpec((ts, ct), lambda i, j: (i, jnp.clip(j, 0, nc - 1))),
                   pl.BlockSpec((ts, ct), lambda i, j: (i, jnp.clip(j - nc, 0, ng - 1))),
                   pl.BlockSpec((ts, d_model), lambda i, j: (i, 0)),
                   pl.BlockSpec((d_model, ts), lambda i, j: (0, i))],
        out_shape=[jax.ShapeDtypeStruct((s_len, 3 * d_model), F32),
                   jax.ShapeDtypeStruct((s_len, ATT_W + 2 * d_model), F32),
                   jax.ShapeDtypeStruct((s_len, d_model), BF16),
                   jax.ShapeDtypeStruct((d_model, s_len), BF16)],
        compiler_params=_params(("arbitrary", "arbitrary")),
    )(x, modv, norm_g, w_in)


def _qkv_projection(h, pos, ropetab, w_in, gi, ts):
    s_len, d_model = h.shape
    dil = DILATIONS[gi]
    nc = 3 * d_model // COL_TILE
    hr = ts // 2
    assert COL_TILE == ATT_W and hr % (16 * dil) == 0

    def body(h_ref, pos_ref, tab_ref, wq_ref, wk_ref, wv_ref, o_ref, slab):
        rc, ra, rb = _rope_tables(pos_ref, tab_ref)
        w_refs = (wq_ref, wk_ref, wv_ref)
        units = [(w, half) for w in range(3) for half in range(2)]

        def matmul(w, half):
            return _dot(h_ref[half * hr:(half + 1) * hr, :], w_refs[w][...])

        def finish(w, half, res):
            rows = slice(half * hr, (half + 1) * hr)
            vals = []
            for s in range(4):
                t = res[:, s * LANES:(s + 1) * LANES]
                if w < 2:
                    t = (t * rc[rows, :] + pltpu.roll(t, LANES - 8, 1) * ra[rows, :]
                         + pltpu.roll(t, 8, 1) * rb[rows, :])
                vals.append(t)
            out = o_ref.at[w, :, half * (hr // dil):(half + 1) * (hr // dil), :]
            _deinterleave_store(vals, slab.at[half], out, dil, hr, BF16)

        res_next = matmul(*units[0])
        for ui, (w, half) in enumerate(units):
            res = res_next
            if ui + 1 < len(units):
                res_next = matmul(*units[ui + 1])
            finish(w, half, res)

    def w_spec(w):
        return pl.BlockSpec((d_model, COL_TILE), lambda i: (0, nc + 3 * w + gi))

    return pl.pallas_call(
        body, name="qkv_projection_%d" % dil, grid=(s_len // ts,),
        in_specs=[pl.BlockSpec((ts, d_model), lambda i: (i, 0)), pl.BlockSpec((ts, 1), lambda i: (i, 0)),
                  pl.BlockSpec((8, LANES), lambda i: (0, 0)), w_spec(0), w_spec(1), w_spec(2)],
        out_specs=pl.BlockSpec((3, dil, ts // dil, ATT_W), lambda i: (0, 0, i, 0)),
        out_shape=jax.ShapeDtypeStruct((3, dil, s_len // dil, ATT_W), BF16),
        scratch_shapes=[pltpu.VMEM((2, 4, hr, LANES), F32)],
        compiler_params=_params(("parallel",)),
    )(h, pos, ropetab, w_in, w_in, w_in)


def _layernorm_stats(u1):
    mu = jnp.mean(u1, axis=-1, keepdims=True)
    xc = u1 - mu
    rstd = lax.rsqrt(jnp.mean(xc * xc, axis=-1, keepdims=True) + EPS)
    return xc * rstd, rstd


def _shifted_copies(win, shf, ts):
    rows = ts + HALO - 8
    for b in range(1, 8):
        shf[b - 1, 0:rows, :] = win[pl.ds(b, rows), :]


def _tap(win, shf, off, r0, rows):
    a, b = divmod(off, 8)
    start = 8 * a + r0
    if b == 0:
        return win[start:start + rows, :]
    return shf[b - 1, start:start + rows, :]


def _conv_forward(p_conv, conv_w, conv_b, ln_g, ln_b, w_co, ts):
    s_len, d3 = p_conv.shape
    dm = d3 // 3

    def body(p_ref, cw_ref, cb_ref, g_ref, b_ref, w_ref, y_ref, u0_ref, u1_ref, win, shf):
        i = pl.program_id(0)

        @pl.when(i == 0)
        def _():
            win[0:HALO, :] = jnp.zeros((HALO, dm), F32)

        a, b, z = p_ref[:, 0:dm], p_ref[:, dm:2 * dm], p_ref[:, 2 * dm:3 * dm]
        u0 = a * _sig(b)
        win[HALO:HALO + ts, :] = u0
        u0_ref[...] = u0
        _shifted_copies(win, shf, ts)
        for r0 in range(0, ts, CONV_ROWS_FWD):
            acc = jnp.broadcast_to(cb_ref[...], (CONV_ROWS_FWD, dm))
            for k in range(CONV_K):
                acc = acc + cw_ref[k:k + 1, :] * _tap(win, shf, HALO - (CONV_K - 1) + k, r0, CONV_ROWS_FWD)
            u1_ref[r0:r0 + CONV_ROWS_FWD, :] = acc
        xh, _ = _layernorm_stats(u1_ref[...])
        u2 = xh * g_ref[...] + b_ref[...]
        a_conv = (u2 * _sig(u2)) * (z * _sig(z))
        y_ref[...] = _dot(a_conv.astype(BF16), w_ref[...])
        win[0:HALO, :] = win[ts:ts + HALO, :]

    row = pl.BlockSpec((1, dm), lambda i: (0, 0))
    tile = pl.BlockSpec((ts, dm), lambda i: (i, 0))
    return pl.pallas_call(
        body, name="conv_forward", grid=(s_len // ts,),
        in_specs=[pl.BlockSpec((ts, d3), lambda i: (i, 0)),
                  pl.BlockSpec((HALO, dm), lambda i: (0, 0)), row, row, row,
                  pl.BlockSpec((dm, dm), lambda i: (0, 0))],
        out_specs=[tile, tile, tile],
        out_shape=[jax.ShapeDtypeStruct((s_len, dm), F32)] * 3,
        scratch_shapes=[pltpu.VMEM((ts + HALO, dm), F32), pltpu.VMEM((7, ts + HALO - 8, dm), F32)],
        compiler_params=_params(("arbitrary",)),
    )(p_conv, conv_w, conv_b, ln_g, ln_b, w_co)


def _conv_backward(dp, dyc, p_conv, u0, u1, conv_w, ln_g, ln_b, w_co, ts):
    s_len, d3 = p_conv.shape
    dm = d3 // 3
    nt = s_len // ts
    hb = ts // HALO

    def body(dp_in, dy_ref, p_ref, u0_ref, uh_ref, u1_ref, cw_ref, g_ref, b_ref, w_ref,
             dp_ref, gw_ref, gs_ref, gcw_ref, dwin, uwin, shf):
        del dp_in
        i = pl.program_id(0)
        ti = nt - 1 - i

        @pl.when(i == 0)
        def _():
            gw_ref[...] = jnp.zeros_like(gw_ref)
            gs_ref[...] = jnp.zeros_like(gs_ref)
            gcw_ref[...] = jnp.zeros_like(gcw_ref)
            dwin[ts:ts + HALO, :] = jnp.zeros((HALO, dm), F32)

        dy = dy_ref[...]
        z = p_ref[:, 2 * dm:3 * dm]
        d_ac = _dot_nt(dy, w_ref[...])
        xh, rstd = _layernorm_stats(u1_ref[...])
        u2 = xh * g_ref[...] + b_ref[...]
        sg2, sgz = _sig(u2), _sig(z)
        u3, sz = u2 * sg2, z * sgz
        gw_ref[...] += _dot_tn((u3 * sz).astype(BF16), dy)
        d_z = d_ac * u3 * _dsilu(z, sgz)
        d_u2 = d_ac * sz * _dsilu(u2, sg2)
        gs_ref[0:1, :] += jnp.sum(d_u2 * xh, axis=0, keepdims=True)
        gs_ref[1:2, :] += jnp.sum(d_u2, axis=0, keepdims=True)
        dxh = d_u2 * g_ref[...]
        d_u1 = rstd * (dxh - jnp.mean(dxh, axis=-1, keepdims=True)
                       - xh * jnp.mean(dxh * xh, axis=-1, keepdims=True))
        gs_ref[2:3, :] += jnp.sum(d_u1, axis=0, keepdims=True)
        dwin[0:ts, :] = d_u1
        uwin[0:HALO, :] = jnp.where(ti == 0, 0.0, uh_ref[...])
        uwin[HALO:HALO + ts, :] = u0_ref[...]
        dp_ref[:, 2 * dm:3 * dm] = d_z.astype(BF16)
        _shifted_copies(uwin, shf, ts)
        for k in range(CONV_K):
            part = jnp.zeros((CONV_ROWS_BWD, dm), F32)
            for r0 in range(0, ts, CONV_ROWS_BWD):
                part = part + dwin[r0:r0 + CONV_ROWS_BWD, :] * _tap(uwin, shf, HALO - (CONV_K - 1) + k, r0,
                                                                    CONV_ROWS_BWD)
            gcw_ref[k:k + 1, :] += jnp.sum(part, axis=0, keepdims=True)
        _shifted_copies(dwin, shf, ts)
        for r0 in range(0, ts, CONV_ROWS_BWD):
            d_u0 = jnp.zeros((CONV_ROWS_BWD, dm), F32)
            for k in range(CONV_K):
                d_u0 = d_u0 + cw_ref[k:k + 1, :] * _tap(dwin, shf, CONV_K - 1 - k, r0, CONV_ROWS_BWD)
            rows = slice(r0, r0 + CONV_ROWS_BWD)
            sgb = _sig(p_ref[rows, dm:2 * dm])
            dp_ref[rows, 0:dm] = (d_u0 * sgb).astype(BF16)
            dp_ref[rows, dm:2 * dm] = (d_u0 * p_ref[rows, 0:dm] * sgb * (1.0 - sgb)).astype(BF16)
        dwin[ts:ts + HALO, :] = dwin[0:HALO, :]

    rev = lambda i: (nt - 1 - i, 0)
    row = pl.BlockSpec((1, dm), lambda i: (0, 0))
    tile = pl.BlockSpec((ts, dm), rev)
    return pl.pallas_call(
        body, name="conv_backward", grid=(nt,),
        in_specs=[ANY, tile, pl.BlockSpec((ts, d3), rev), tile,
                  pl.BlockSpec((HALO, dm), lambda i: (jnp.maximum((nt - 1 - i) * hb - 1, 0), 0)),
                  tile, pl.BlockSpec((HALO, dm), lambda i: (0, 0)), row, row,
                  pl.BlockSpec((dm, dm), lambda i: (0, 0))],
        out_specs=[pl.BlockSpec((ts, d3), rev), pl.BlockSpec((dm, dm), lambda i: (0, 0)),
                   pl.BlockSpec((8, dm), lambda i: (0, 0)), pl.BlockSpec((HALO, dm), lambda i: (0, 0))],
        out_shape=[jax.ShapeDtypeStruct(dp.shape, BF16), jax.ShapeDtypeStruct((dm, dm), F32),
                   jax.ShapeDtypeStruct((8, dm), F32), jax.ShapeDtypeStruct((HALO, dm), F32)],
        input_output_aliases={0: 0},
        scratch_shapes=[pltpu.VMEM((ts + HALO, dm), F32), pltpu.VMEM((ts + HALO, dm), F32),
                        pltpu.VMEM((7, ts + HALO - 8, dm), F32)],
        compiler_params=_params(("arbitrary",)),
    )(dp, dyc, p_conv, u0, u0, u1, conv_w, ln_g, ln_b, w_co)


def _att_masks():
    head0 = lax.broadcasted_iota(jnp.int32, (BLK, LANES), 1) < 64
    col = lax.broadcasted_iota(jnp.int32, (BLK, 4 * BLK), 1)
    row = lax.broadcasted_iota(jnp.int32, (BLK, 4 * BLK), 0)
    kj = col % BLK
    prev = jnp.where(col < 2 * BLK, 1, 0)
    band = jnp.where(col < 2 * BLK, kj - row, row - kj)
    return head0, band, prev


def _fill_block_diagonal(dst, slab, src_ref, halo_ref, head0, nq):
    sl = slice(slab * LANES, (slab + 1) * LANES)
    for b in range(nq + 1):
        blk = halo_ref[:, sl] if b == 0 else src_ref[(b - 1) * BLK:b * BLK, sl]
        base = (slab * (nq + 1) + b) * 2 * BLK
        zero = jnp.zeros_like(blk)
        dst[base:base + BLK, :] = jnp.where(head0, blk, zero)
        dst[base + BLK:base + 2 * BLK, :] = jnp.where(head0, zero, blk)


def _attention_forward(qkv, seq_len, qt, name):
    s_len = qkv.shape[1]
    nq = qt // BLK
    tiles_per_seq = seq_len // qt

    def body(q_ref, k_ref, v_ref, kh_ref, vh_ref, o_ref, lse_ref, kbd, vbd):
        first = jnp.where((pl.program_id(0) % tiles_per_seq) == 0, 4 * BLK, 0)
        head0, band, prev = _att_masks()
        band_first = band - prev * first
        for p in range(4):
            _fill_block_diagonal(kbd, p, k_ref, kh_ref, head0, nq)
            _fill_block_diagonal(vbd, p, v_ref, vh_ref, head0, nq)
        units = [(p, n) for p in range(4) for n in range(nq)]

        def keys_of(p, n):
            base = (p * (nq + 1) + n) * 2 * BLK
            return slice(base, base + 4 * BLK)

        def scores(p, n):
            q2 = q_ref[n * BLK:(n + 1) * BLK, p * LANES:(p + 1) * LANES] * 0.125
            return _dot_nt(q2, kbd[keys_of(p, n), :])

        def finish(p, n, o, den, lse):
            rows, sl = slice(n * BLK, (n + 1) * BLK), slice(p * LANES, (p + 1) * LANES)
            o_ref[rows, sl] = o / jnp.where(head0, den[0], den[1])
            lse_ref[rows, sl] = jnp.where(head0, lse[0], lse[1])

        s_next = scores(*units[0])
        pending = None
        for ui, (p, n) in enumerate(units):
            s = s_next
            if ui + 1 < len(units):
                s_next = scores(*units[ui + 1])
            s = jnp.where((band_first if n == 0 else band) >= 0, s, NEG)
            grp = [s[:, g * BLK:(g + 1) * BLK] for g in range(4)]
            ps, den, lse = [None] * 4, [], []
            for h in range(2):
                m = jnp.max(jnp.maximum(grp[h], grp[2 + h]), axis=-1, keepdims=True)
                ps[h], ps[2 + h] = jnp.exp(grp[h] - m), jnp.exp(grp[2 + h] - m)
                dn = jnp.sum(ps[h] + ps[2 + h], axis=-1, keepdims=True)
                den.append(dn)
                lse.append(m + jnp.log(dn))
            pmat = jnp.concatenate([x.astype(BF16) for x in ps], axis=1)
            o = _dot(pmat, vbd[keys_of(p, n), :])
            if pending is not None:
                finish(*pending)
            pending = (p, n, o, den, lse)
        finish(*pending)

    def which(w):
        return pl.BlockSpec((None, qt, ATT_W), lambda i: (w, i, 0))

    def halo(w):
        return pl.BlockSpec((None, BLK, ATT_W), lambda i: (w, jnp.maximum(i * nq - 1, 0), 0))

    out = pl.BlockSpec((qt, ATT_W), lambda i: (i, 0))
    bd = pltpu.VMEM((4 * (nq + 1) * 2 * BLK, LANES), BF16)
    return pl.pallas_call(
        body, name=name, grid=(s_len // qt,),
        in_specs=[which(0), which(1), which(2), halo(1), halo(2)],
        out_specs=[out, out], out_shape=[jax.ShapeDtypeStruct((s_len, ATT_W), F32)] * 2,
        scratch_shapes=[bd, bd],
        compiler_params=_params(("parallel",)),
    )(qkv, qkv, qkv, qkv, qkv)


def _attention_backward(qkv, d_att, lse, delta, seq_len, qt, name):
    s_len = qkv.shape[1]
    nq = qt // BLK
    tiles_per_seq = seq_len // qt
    nblk = s_len // BLK

    def body(q_ref, k_ref, v_ref, kh_ref, vh_ref, do_ref, lse_ref, dl_ref,
             qn_ref, don_ref, lsen_ref, dln_ref, dqkv_ref, kbd, vbd):
        i = pl.program_id(0)
        first = jnp.where((i % tiles_per_seq) == 0, 4 * BLK, 0)
        last = jnp.where((i % tiles_per_seq) == tiles_per_seq - 1, 4 * BLK, 0)
        head0, band, prev = _att_masks()
        band_first = band - prev * first
        band_tail = band[:, 0:2 * BLK] - last
        for p in range(4):
            _fill_block_diagonal(kbd, p, k_ref, kh_ref, head0, nq)
            _fill_block_diagonal(vbd, p, v_ref, vh_ref, head0, nq)
        units = [(p, n) for p in range(4) for n in range(nq + 1)]

        def stage_a(p, n):
            sl = slice(p * LANES, (p + 1) * LANES)
            base = (p * (nq + 1) + n) * 2 * BLK
            if n < nq:
                rows = slice(n * BLK, (n + 1) * BLK)
                q2, do2, lse2, dl2 = q_ref[rows, sl], do_ref[rows, sl], lse_ref[rows, sl], dl_ref[rows, sl]
                keys = slice(base, base + 4 * BLK)
            else:
                q2, do2, lse2, dl2 = qn_ref[:, sl], don_ref[:, sl], lsen_ref[:, sl], dln_ref[:, sl]
                keys = slice(base, base + 2 * BLK)
            s = _dot_nt(q2 * 0.125, kbd[keys, :])
            dp = _dot_nt(do2, vbd[keys, :])
            return q2, do2, lse2, dl2, keys, s, dp

        def stage_b(n, lse2, dl2, s, dp):
            mask = band_tail if n == nq else (band_first if n == 0 else band)
            ps, dss = [], []
            for g in range(s.shape[1] // BLK):
                h = g % 2
                cols = slice(g * BLK, (g + 1) * BLK)
                pg = jnp.exp(jnp.where(mask[:, cols] >= 0, s[:, cols] - lse2[:, h * 64:h * 64 + 1], NEG))
                ps.append(pg.astype(BF16))
                dss.append((pg * (dp[:, cols] - dl2[:, h * 64:h * 64 + 1]) * 0.125).astype(BF16))
            return jnp.concatenate(ps, axis=1), jnp.concatenate(dss, axis=1)

        def heads(r, g):
            return jnp.where(head0, r[g * BLK:(g + 1) * BLK, :], r[(g + 1) * BLK:(g + 2) * BLK, :])

        a_next = stage_a(*units[0])
        carry = None
        for ui, (p, n) in enumerate(units):
            q2, do2, lse2, dl2, keys, s, dp = a_next
            if ui + 1 < len(units):
                a_next = stage_a(*units[ui + 1])
            pmat, dsmat = stage_b(n, lse2, dl2, s, dp)
            sl = slice(p * LANES, (p + 1) * LANES)
            if n < nq:
                dqkv_ref[0, n * BLK:(n + 1) * BLK, sl] = _dot(dsmat, kbd[keys, :])
            dkbd = _dot_tn(dsmat, q2)
            dvbd = _dot_tn(pmat, do2)
            if n > 0:
                prow = slice((n - 1) * BLK, n * BLK)
                dqkv_ref[1, prow, sl] = carry[0] + heads(dkbd, 0)
                dqkv_ref[2, prow, sl] = carry[1] + heads(dvbd, 0)
            if n < nq:
                carry = (heads(dkbd, 2), heads(dvbd, 2))

    def which(w):
        return pl.BlockSpec((None, qt, ATT_W), lambda i: (w, i, 0))

    def prev(w):
        return pl.BlockSpec((None, BLK, ATT_W), lambda i: (w, jnp.maximum(i * nq - 1, 0), 0))

    tile = pl.BlockSpec((qt, ATT_W), lambda i: (i, 0))
    nxt = pl.BlockSpec((BLK, ATT_W), lambda i: (jnp.minimum((i + 1) * nq, nblk - 1), 0))
    nxt_q = pl.BlockSpec((None, BLK, ATT_W), lambda i: (0, jnp.minimum((i + 1) * nq, nblk - 1), 0))
    return pl.pallas_call(
        body, name=name, grid=(s_len // qt,),
        in_specs=[which(0), which(1), which(2), prev(1), prev(2), tile, tile, tile, nxt_q, nxt, nxt, nxt],
        out_specs=pl.BlockSpec((3, qt, ATT_W), lambda i: (0, i, 0)),
        out_shape=jax.ShapeDtypeStruct((3, s_len, ATT_W), F32),
        scratch_shapes=[pltpu.VMEM((4 * (nq + 1) * 2 * BLK, LANES), BF16)] * 2,
        compiler_params=_params(("parallel",)),
    )(qkv, qkv, qkv, qkv, qkv, d_att, lse, delta, qkv, d_att, lse, delta)


def _merge_and_head(o_g, lse_g, p_gate, y_conv, x, tgt, modv, final_g, w_ao, w_o, ts):
    s_len, dm = x.shape
    gw = ATT_W + 2 * dm
    nt = s_len // ts
    gate_off = 3 * dm + 9 * ATT_W
    assert gate_off % gw == 0

    def body(o0, o1, o2, l0, l1, l2, pg_ref, yc_ref, x_ref, t_ref, mod_ref, fg_ref, wao_ref, wo_ref,
             loss_ref, dx_ref, dyc_ref, dpg_ref, da0, da1, da2, ls0, ls1, ls2, de0, de1, de2,
             gwo_ref, gwao_ref, gs_ref, slab):
        i = pl.program_id(0)

        @pl.when(i == 0)
        def _():
            loss_ref[...] = jnp.zeros_like(loss_ref)
            gwo_ref[...] = jnp.zeros_like(gwo_ref)
            gwao_ref[...] = jnp.zeros_like(gwao_ref)
            gs_ref[...] = jnp.zeros_like(gs_ref)

        os_, ls_ = [], []
        for dil, o_ref, l_ref in zip(DILATIONS, (o0, o1, o2), (l0, l1, l2)):
            os_.append(jnp.concatenate(_interleave_load(o_ref, slab, dil, ts), axis=1))
            ls_.append(jnp.concatenate(_interleave_load(l_ref, slab, dil, ts), axis=1))
        mx = jnp.maximum(jnp.maximum(ls_[0], ls_[1]), ls_[2])
        wts = [jnp.exp(l - mx) for l in ls_]
        wsum = wts[0] + wts[1] + wts[2]
        att = (wts[0] * os_[0] + wts[1] * os_[1] + wts[2] * os_[2]) / wsum
        lse_all = mx + jnp.log(wsum)

        z_att, g_conv, g_att = pg_ref[:, 0:ATT_W], pg_ref[:, ATT_W:ATT_W + dm], pg_ref[:, ATT_W + dm:gw]
        sgz = _sig(z_att)
        sz = z_att * sgz
        a_att = (att * sz).astype(BF16)
        y_att = _dot(a_att, wao_ref[...])
        y_conv = yc_ref[...]
        sgc, sga = _sig(g_conv), _sig(g_att)
        merged = (sgc * y_conv + sga * y_att).astype(BF16)
        mo = _dot(merged, wo_ref[...])
        gate = mod_ref[2:3, :]
        x2 = x_ref[...] + gate * mo
        r = lax.rsqrt(jnp.mean(x2 * x2, axis=-1, keepdims=True) + EPS)
        xr = x2 * r
        err = xr * fg_ref[...] - t_ref[...]
        loss_ref[...] += 0.5 * jnp.sum(jnp.mean(err * err, axis=-1, keepdims=True))
        dy = err * (1.0 / dm)
        gs_ref[0:1, :] += jnp.sum(dy * xr, axis=0, keepdims=True)
        dyg = dy * fg_ref[...]
        d_x2 = r * dyg - xr * (r * jnp.mean(dyg * xr, axis=-1, keepdims=True))
        dx_ref[...] = d_x2
        gs_ref[1:2, :] += jnp.sum(d_x2 * mo, axis=0, keepdims=True)
        d_mo = (d_x2 * gate).astype(BF16)
        d_mg = _dot_nt(d_mo, wo_ref[...])
        gwo_ref[...] += _dot_tn(merged, d_mo)
        dyc_ref[...] = (d_mg * sgc).astype(BF16)
        dpg_ref[:, ATT_W:ATT_W + dm] = (d_mg * y_conv * sgc * (1.0 - sgc)).astype(BF16)
        d_ya = (d_mg * sga).astype(BF16)
        dpg_ref[:, ATT_W + dm:gw] = (d_mg * y_att * sga * (1.0 - sga)).astype(BF16)
        gwao_ref[...] += _dot_tn(a_att, d_ya)
        d_aa = _dot_nt(d_ya, wao_ref[...])
        dpg_ref[:, 0:ATT_W] = (d_aa * att * _dsilu(z_att, sgz)).astype(BF16)
        d_att = d_aa * sz
        ri = lax.broadcasted_iota(jnp.int32, (ATT_W, ATT_W), 0) // 64
        ci = lax.broadcasted_iota(jnp.int32, (ATT_W, ATT_W), 1) // 64
        ones = jnp.where(ri == ci, 1.0, 0.0).astype(BF16)
        prod = d_att * att
        hi = prod.astype(BF16)
        lo = (prod - hi.astype(F32)).astype(BF16)
        delta = _dot(hi, ones) + _dot(lo, ones)
        for val, refs, dt in ((d_att, (da0, da1, da2), BF16), (lse_all, (ls0, ls1, ls2), F32),
                              (delta, (de0, de1, de2), F32)):
            vals = [val[:, s * LANES:(s + 1) * LANES] for s in range(4)]
            for dil, ref in zip(DILATIONS, refs):
                _deinterleave_store(vals, slab, ref, dil, ts, dt)

    def grp(dil):
        return pl.BlockSpec((dil, ts // dil, ATT_W), lambda i: (0, i, 0))

    tile = pl.BlockSpec((ts, dm), lambda i: (i, 0))
    gate_tile = pl.BlockSpec((ts, gw), lambda i: (i, 0))
    const = lambda shp: pl.BlockSpec(shp, lambda i: tuple(0 for _ in shp))
    grp_shape = lambda dt: [jax.ShapeDtypeStruct((dil, s_len // dil, ATT_W), dt) for dil in DILATIONS]
    return pl.pallas_call(
        body, name="merge_and_head", grid=(nt,),
        in_specs=[grp(d) for d in DILATIONS] * 2
        + [gate_tile, tile, tile, tile, const((8, dm)), const((1, dm)), const((ATT_W, dm)), const((dm, dm))],
        out_specs=[const((8, LANES)), tile, tile, pl.BlockSpec((ts, gw), lambda i: (i, gate_off // gw))]
        + [grp(d) for d in DILATIONS] * 3
        + [const((dm, dm)), const((ATT_W, dm)), const((8, dm))],
        out_shape=[jax.ShapeDtypeStruct((8, LANES), F32), jax.ShapeDtypeStruct((s_len, dm), F32),
                   jax.ShapeDtypeStruct((s_len, dm), BF16), jax.ShapeDtypeStruct((s_len, gate_off + gw), BF16)]
        + grp_shape(BF16) + grp_shape(F32) + grp_shape(F32)
        + [jax.ShapeDtypeStruct((dm, dm), F32), jax.ShapeDtypeStruct((ATT_W, dm), F32),
           jax.ShapeDtypeStruct((8, dm), F32)],
        scratch_shapes=[pltpu.VMEM((4, ts, LANES), F32)],
        compiler_params=_params(("arbitrary",)),
    )(*o_g, *lse_g, p_gate, y_conv, x, tgt, modv, final_g, w_ao, w_o)


def _qkv_grad_to_tokens(dp, dqkv_g, pos, ropetab, ts):
    s_len, dm = pos.shape[0], (dp.shape[1] - 10 * ATT_W) // 5
    qw = 3 * ATT_W
    assert (3 * dm) % qw == 0

    def body(dp_in, g0, g1, g2, pos_ref, tab_ref, o_ref, slab, rc, ra, rb):
        del dp_in
        w = pl.program_id(1)

        @pl.when(w == 0)
        def _():
            c_, a_, b_ = _rope_tables(pos_ref, tab_ref)
            rc[...] = c_
            ra[...] = a_
            rb[...] = b_

        def emit(roped):
            for gi, (dil, g_ref) in enumerate(zip(DILATIONS, (g0, g1, g2))):
                vals = _interleave_load(g_ref, slab, dil, ts)
                for s in range(4):
                    t = vals[s]
                    if roped:
                        t = t * rc[...] + pltpu.roll(t * ra[...], 8, 1) + pltpu.roll(t * rb[...], LANES - 8, 1)
                    col = gi * ATT_W + s * LANES
                    o_ref[:, col:col + LANES] = t.astype(BF16)

        pl.when(w < 2)(lambda: emit(True))
        pl.when(w == 2)(lambda: emit(False))

    return pl.pallas_call(
        body, name="qkv_grad_to_tokens", grid=(s_len // ts, 3),
        in_specs=[ANY] + [pl.BlockSpec((None, dil, ts // dil, ATT_W), lambda i, w: (w, 0, i, 0)) for dil in DILATIONS]
        + [pl.BlockSpec((ts, 1), lambda i, w: (i, 0)), pl.BlockSpec((8, LANES), lambda i, w: (0, 0))],
        out_specs=pl.BlockSpec((ts, qw), lambda i, w: (i, 3 * dm // qw + w)),
        out_shape=jax.ShapeDtypeStruct(dp.shape, BF16),
        input_output_aliases={0: 0},
        scratch_shapes=[pltpu.VMEM((4, ts, LANES), F32)] + [pltpu.VMEM((ts, LANES), F32)] * 3,
        compiler_params=_params(("parallel", "arbitrary")),
    )(dp, *dqkv_g, pos, ropetab)


def _wide_col_tile(cols):
    for width in (5 * COL_TILE, 2 * COL_TILE):
        if cols % width == 0:
            return width
    return COL_TILE


def _input_grad(dp, w_in, x, dx_res, modv, norm_g, ts):
    s_len, dm = x.shape
    ct = _wide_col_tile(dp.shape[1])
    nct = dp.shape[1] // ct

    def body(p_ref, w_ref, x_ref, dxr_ref, mod_ref, g_ref, gx_ref, gs_ref, acc):
        i, j = pl.program_id(0), pl.program_id(1)

        @pl.when((i == 0) & (j == 0))
        def _():
            gs_ref[...] = jnp.zeros_like(gs_ref)

        @pl.when(j == 0)
        def _():
            acc[...] = jnp.zeros_like(acc)

        acc[...] += _dot_nt(p_ref[...], w_ref[...])

        @pl.when(j == nct - 1)
        def _():
            d_h = acc[...]
            xv = x_ref[...]
            r = lax.rsqrt(jnp.mean(xv * xv, axis=-1, keepdims=True) + EPS)
            xr = xv * r
            gs_ref[0:1, :] += jnp.sum(d_h, axis=0, keepdims=True)
            gs_ref[1:2, :] += jnp.sum(d_h * (xr * g_ref[...]), axis=0, keepdims=True)
            d_n = d_h * (1.0 + mod_ref[1:2, :])
            gs_ref[2:3, :] += jnp.sum(d_n * xr, axis=0, keepdims=True)
            dxn = d_n * g_ref[...]
            gx_ref[...] = dxr_ref[...] + r * dxn - xr * (r * jnp.mean(dxn * xr, axis=-1, keepdims=True))

    tile = pl.BlockSpec((ts, dm), lambda i, j: (i, 0))
    return pl.pallas_call(
        body, name="input_grad", grid=(s_len // ts, nct),
        in_specs=[pl.BlockSpec((ts, ct), lambda i, j: (i, j)), pl.BlockSpec((dm, ct), lambda i, j: (0, j)), tile, tile,
                  pl.BlockSpec((8, dm), lambda i, j: (0, 0)), pl.BlockSpec((1, dm), lambda i, j: (0, 0))],
        out_specs=[tile, pl.BlockSpec((8, dm), lambda i, j: (0, 0))],
        out_shape=[jax.ShapeDtypeStruct((s_len, dm), F32), jax.ShapeDtypeStruct((8, dm), F32)],
        scratch_shapes=[pltpu.VMEM((ts, dm), F32)],
        compiler_params=_params(("arbitrary", "arbitrary")),
    )(dp, w_in, x, dx_res, modv, norm_g)


def _w_in_grad(h_t, dp, ts):
    dm, s_len = h_t.shape
    ct = _wide_col_tile(dp.shape[1])

    def body(h_ref, p_ref, o_ref):
        @pl.when(pl.program_id(1) == 0)
        def _():
            o_ref[...] = jnp.zeros_like(o_ref)

        o_ref[...] += _dot(h_ref[...], p_ref[...])

    return pl.pallas_call(
        body, name="w_in_grad", grid=(dp.shape[1] // ct, s_len // ts),
        in_specs=[pl.BlockSpec((dm, ts), lambda j, i: (0, i)), pl.BlockSpec((ts, ct), lambda j, i: (i, j))],
        out_specs=pl.BlockSpec((dm, ct), lambda j, i: (0, j)),
        out_shape=jax.ShapeDtypeStruct((dm, dp.shape[1]), F32),
        compiler_params=_params(("arbitrary", "arbitrary")),
    )(h_t, dp)


def _rope_lane_table():
    l64 = jnp.arange(LANES) % 64
    half = ROT_DIM // 2
    inv_freq = ROPE_THETA ** (-(jnp.arange(half, dtype=F32) * 2.0 / ROT_DIM))
    rot = l64 < ROT_DIM
    rows = [jnp.where(rot, inv_freq[l64 % half], 0.0), (l64 < half).astype(F32),
            ((l64 >= half) & rot).astype(F32), rot.astype(F32)]
    return jnp.concatenate([jnp.stack(rows), jnp.zeros((4, LANES), F32)], axis=0)


def _tile_sizes(s_len):
    l_min = s_len // DILATIONS[-1]
    ts_big = min(1024, s_len // 2)
    ts_mid = 256
    ts_head = 256
    qt = min(512, l_min)
    return ts_big, ts_mid, ts_head, qt


def kernel(x, c, positions, norm_g, w_ada, b_ada, w_in, conv_w, conv_b, conv_ln_g, conv_ln_b, w_conv_out, w_att_out, w_o, final_g, loss_target, m_norm_g, m_w_ada, m_b_ada, m_w_in, m_conv_w, m_conv_b, m_conv_ln_g, m_conv_ln_b, m_w_conv_out, m_w_att_out, m_w_o, m_final_g, v_norm_g, v_w_ada, v_b_ada, v_w_in, v_conv_w, v_conv_b, v_conv_ln_g, v_conv_ln_b, v_w_conv_out, v_w_att_out, v_w_o, v_final_g):
    s_len, dm = x.shape[1], x.shape[2]
    ts_big, ts_mid, ts_head, qt = _tile_sizes(s_len)
    xi, yi, cidx = _my_place()
    chip = 2 * xi + yi
    batch = 4 * xi + 2 * yi + cidx
    x2d, tgt = x[0], loss_target[0]
    pos = positions.reshape(s_len, 1)
    wa_l, wi_l, cw_l = w_ada[0], w_in[0], conv_w[0]
    wco_l, wao_l, wo_l = w_conv_out[0], w_att_out[0], w_o[0]
    ada_w = wa_l.shape[1]
    cw_cols = cw_l.shape[1]

    kinds = ("col", "row", "col", "row")
    sems_a, sems_b, w_bufs, token = _gather_weights_start(_cast_weights([wi_l, wco_l, wao_l, wo_l], kinds), kinds)

    cw_pad = jnp.pad(cw_l, ((0, HALO - CONV_K), (0, 0)))
    small_in = jnp.concatenate([jnp.broadcast_to(c, (8, dm)), cw_pad.reshape(8, dm)], axis=0)
    small = _allgather_small(small_in + token[0, 0]).reshape(8, 16, dm)
    c_all = small[:, 0, :]
    conv_w_full = jnp.concatenate(
        [small[2 * p, 8:16, :].reshape(HALO, cw_cols) for p in range(4)], axis=1)
    b_l = lax.dynamic_slice(b_ada, (0, chip * ada_w), (1, ada_w))
    mod_parts = _allgather_small(_mod_part(c_all, wa_l, b_l)).reshape(8, 8, ada_w)
    mod_rows = lax.dynamic_index_in_dim(mod_parts, batch, axis=1, keepdims=False)
    mod = jnp.concatenate([mod_rows[2 * p] for p in range(4)], axis=0).reshape(3, dm)
    modv = jnp.concatenate([mod, jnp.zeros((5, dm), F32)], axis=0)

    (w_in_b,) = _gather_weights_wait(w_bufs, kinds, sems_a, 0, modv, "gather_weights_wait_w_in")
    w_in_b = _pass_to_sibling(w_in_b, kinds[0])

    ropetab = _rope_lane_table()
    p_conv, p_gate, h_b, h_t = _input_projection(x2d, modv, norm_g, w_in_b, ts_big)
    qkv_g = [_qkv_projection(h_b, pos, ropetab, w_in_b, gi, ts_big) for gi in range(3)]
    w_co_b, w_ao_b, w_o_b = _gather_weights_wait(w_bufs, kinds, sems_b, 1, qkv_g[2], "gather_weights_wait_rest")
    y_conv, u0, u1 = _conv_forward(p_conv, conv_w_full, conv_b, conv_ln_g, conv_ln_b, w_co_b, ts_mid)
    qkv_flat = [q.reshape(3, s_len, ATT_W) for q in qkv_g]
    o_g, lse_g = [], []
    for gi, dil in enumerate(DILATIONS):
        o, l = _attention_forward(qkv_flat[gi], s_len // dil, qt, "attention_forward_%d" % dil)
        o_g.append(o.reshape(dil, s_len // dil, ATT_W))
        lse_g.append(l.reshape(dil, s_len // dil, ATT_W))

    (loss_p, dx_res, dyc, dp, da0, da1, da2, ls0, ls1, ls2, de0, de1, de2,
     g_wo, g_wao, head_sums) = _merge_and_head(o_g, lse_g, p_gate, y_conv, x2d, tgt, modv,
                                               final_g.reshape(1, dm), w_ao_b, w_o_b, ts_head)
    loss = lax.psum(loss_p[0, 0], ("x", "y", "c"))

    dp, g_wco, conv_sums, g_cw = _conv_backward(dp, dyc, p_conv, u0, u1, conv_w_full, conv_ln_g, conv_ln_b,
                                                w_co_b, ts_mid)
    dqkv_g = []
    for gi, (dil, da, ls, de) in enumerate(zip(DILATIONS, (da0, da1, da2), (ls0, ls1, ls2), (de0, de1, de2))):
        flat = lambda a: a.reshape(s_len, ATT_W)
        dqkv = _attention_backward(qkv_flat[gi], flat(da), flat(ls), flat(de), s_len // dil, qt,
                                   "attention_backward_%d" % dil)
        dqkv_g.append(dqkv.reshape(3, dil, s_len // dil, ATT_W))
    dp = _qkv_grad_to_tokens(dp, dqkv_g, pos, ropetab, 2 * ts_mid)
    g_win = _w_in_grad(h_t, dp, ts_big)

    grads = [g_win, g_wco, g_wao, g_wo]
    full_shapes = [g.shape for g in grads]
    c_arr = jnp.reshape(cidx, (1,)).astype(jnp.int32)
    recv_halves = _reduce_pair_exchange(grads, kinds)
    halves = [_reduce_pair_sum(g, pa, kind, c_arr, "reduce_pair_sum_%d" % k)
              for k, (g, pa, kind) in enumerate(zip(grads, recv_halves, kinds))]
    send_sems, recv_sems, halves, lands, token = _reduce_to_owner_start(halves, kinds, full_shapes)
    grad_x, in_sums = _input_grad(dp, w_in_b, x2d, dx_res, modv + token[0, 0], norm_g, ts_big // 2)
    halves, recvd = _reduce_to_owner_wait(send_sems, recv_sems, halves, lands, in_sums, kinds)
    gr_win, gr_wco, gr_wao, gr_wo = _reduce_finish(halves, recvd, kinds, full_shapes)

    rows = [in_sums[2:3], conv_sums[2:3], conv_sums[0:1], conv_sums[1:2], head_sums[0:1],
            in_sums[0:1], in_sums[1:2], head_sums[1:2], g_cw, jnp.zeros((8, dm), F32)]
    part = jnp.concatenate(rows, axis=0)
    gathered = _allgather_small(part)
    tot = _sum_devices(gathered, 48)
    dmod_all = gathered.reshape(8, 48, dm)[:, 5:8, :].reshape(8, 3 * dm)
    dmod_l = lax.dynamic_slice(dmod_all, (0, chip * ada_w), (8, ada_w))
    gr_wada = _grad_w_ada(c_all.T, dmod_l)
    gr_cw = lax.dynamic_slice(tot[8:8 + HALO], (0, chip * cw_cols), (HALO, cw_cols))

    def pack8(ng, cb, lg, lb, fg, ba):
        return jnp.concatenate([ng, cb, lg, lb, fg.reshape(1, dm), ba.reshape(3, dm)], axis=0)

    pad_cw = lambda a: jnp.pad(a[0], ((0, HALO - CONV_K), (0, 0)))
    w8 = pack8(norm_g, conv_b, conv_ln_g, conv_ln_b, final_g, b_ada)
    m8 = pack8(m_norm_g, m_conv_b, m_conv_ln_g, m_conv_ln_b, m_final_g, m_b_ada)
    v8 = pack8(v_norm_g, v_conv_b, v_conv_ln_g, v_conv_ln_b, v_final_g, v_b_ada)
    g8 = tot[0:8]
    upd = {
        "small": _adamw(w8, g8, m8, v8, "adamw_small"),
        "w_ada": _adamw(wa_l, gr_wada, m_w_ada[0], v_w_ada[0], "adamw_w_ada"),
        "w_in": _adamw(wi_l, gr_win, m_w_in[0], v_w_in[0], "adamw_w_in"),
        "conv_w": _adamw(cw_pad, gr_cw, pad_cw(m_conv_w), pad_cw(v_conv_w), "adamw_conv_w"),
        "w_co": _adamw(wco_l, gr_wco, m_w_conv_out[0], v_w_conv_out[0], "adamw_w_conv_out"),
        "w_ao": _adamw(wao_l, gr_wao, m_w_att_out[0], v_w_att_out[0], "adamw_w_att_out"),
        "w_o": _adamw(wo_l, gr_wo, m_w_o[0], v_w_o[0], "adamw_w_o"),
    }

    def family(which):
        if which is None:
            sm = g8
            big = {"w_ada": gr_wada, "w_in": gr_win, "conv_w": gr_cw, "w_co": gr_wco, "w_ao": gr_wao, "w_o": gr_wo}
        else:
            sm = upd["small"][which]
            big = {k: upd[k][which] for k in ("w_ada", "w_in", "conv_w", "w_co", "w_ao", "w_o")}
        return [sm[0:1], big["w_ada"][None], sm[5:8].reshape(1, 3 * dm), big["w_in"][None],
                big["conv_w"][None, :CONV_K], sm[1:2], sm[2:3], sm[3:4], big["w_co"][None],
                big["w_ao"][None], big["w_o"][None], sm[4]]

    return (loss, grad_x[None], *family(None), *family(0), *family(1), *family(2))
```

```python
import jax
import jax.numpy as jnp
from jax import lax
from jax.experimental import pallas as pl
from jax.experimental.pallas import tpu as pltpu

F32 = jnp.float32
BF16 = jnp.bfloat16
MESH = pl.DeviceIdType.MESH
ANY = pl.BlockSpec(memory_space=pl.ANY)
VM = pl.BlockSpec(memory_space=pltpu.VMEM)

EPS = 1e-6
NEG = -1e30
ATT_W = 512
DILATIONS = (1, 4, 16)
BLK = 128
CONV_K = 31
HALO = 32
CONV_ROWS_FWD = 32
CONV_ROWS_BWD = 16
ROT_DIM = 16
ROPE_THETA = 500000.0
COL_TILE = 512
LANES = 128
VMEM_LIMIT = 56 * 1024 * 1024

ADAM_LR, ADAM_B1, ADAM_B2, ADAM_EPS, ADAM_WD, ADAM_STEP = 0.001, 0.9, 0.999, 1e-08, 0.01, 10


def _params(sem=None, vmem=VMEM_LIMIT):
    return pltpu.CompilerParams(dimension_semantics=sem, vmem_limit_bytes=vmem)


def _dot(a, b):
    return jnp.dot(a, b, preferred_element_type=F32)


def _dot_nt(a, b):
    return lax.dot_general(a, b, (((1,), (1,)), ((), ())), preferred_element_type=F32)


def _dot_tn(a, b):
    return lax.dot_general(a, b, (((0,), (0,)), ((), ())), preferred_element_type=F32)


def _sig(x):
    return jax.nn.sigmoid(x)


def _dsilu(x, s):
    return s * (1.0 + x * (1.0 - s))


def _my_place():
    return lax.axis_index("x"), lax.axis_index("y"), lax.axis_index("c")


def _allgather_small(x_shard):
    m_per, n = x_shard.shape

    def body(x_ref, out_ref, send_sems, recv_sems, local_sem):
        x, y, c = _my_place()
        me, sibling = (x, y, c), (x, y, 1 - c)
        chips = [(1 - x, y), (x, 1 - y), (1 - x, 1 - y)]

        def rows(px, py, pc):
            return out_ref.at[pl.ds((4 * px + 2 * py + pc) * m_per, m_per), :]

        def copy(k, block, to, src=None):
            return pltpu.make_async_remote_copy(
                src_ref=rows(*block) if src is None else src, dst_ref=rows(*block),
                send_sem=send_sems.at[k], recv_sem=recv_sems.at[k],
                device_id=to, device_id_type=MESH)

        mine = pltpu.make_async_copy(x_ref, rows(*me), local_sem)
        mine.start()
        first = [copy(0, me, sibling, src=x_ref)]
        first += [copy(1 + j, me, (*chip, c), src=x_ref) for j, chip in enumerate(chips)]
        for cp in first:
            cp.start()
        passed = [copy(4 + j, (*chip, c), sibling) for j, chip in enumerate(chips)]
        for j, chip in enumerate(chips):
            copy(1 + j, (*chip, c), me).wait_recv()
            passed[j].start()
        copy(0, sibling, me).wait_recv()
        for j, chip in enumerate(chips):
            copy(4 + j, (*chip, 1 - c), me).wait_recv()
        for cp in first + passed:
            cp.wait_send()
        mine.wait()

    return pl.pallas_call(
        body, name="allgather_small",
        out_shape=jax.ShapeDtypeStruct((8 * m_per, n), x_shard.dtype),
        in_specs=[VM], out_specs=VM,
        scratch_shapes=[pltpu.SemaphoreType.DMA((7,)), pltpu.SemaphoreType.DMA((7,)),
                        pltpu.SemaphoreType.DMA],
    )(x_shard)


def _shard_window(ref, kind, p, n_shards=4):
    r, c = ref.shape
    if kind == "col":
        w = c // n_shards
        return ref.at[:, pl.ds(p * w, w)]
    w = r // n_shards
    return ref.at[pl.ds(p * w, w), :]


def _half_window(ref, kind, hc):
    r, c = ref.shape
    if kind == "col":
        return ref.at[pl.ds(hc * (r // 2), r // 2), :]
    return ref.at[:, pl.ds(hc * (c // 2), c // 2)]


def _landed(ref, kind, chip, hc):
    return _half_window(_shard_window(ref, kind, 2 * chip[0] + chip[1]), kind, hc)


def _cast_weights(shards, kinds):
    n = len(shards)
    full_shapes = [(s.shape[0], 4 * s.shape[1]) if kind == "col" else (4 * s.shape[0], s.shape[1])
                   for s, kind in zip(shards, kinds)]

    def body(*refs):
        w_refs, out_refs, bf_refs, sems = refs[:n], refs[n:2 * n], refs[2 * n:3 * n], refs[3 * n]
        x, y, _ = _my_place()
        cps = []
        for k in range(n):
            bf_refs[k][...] = w_refs[k][...].astype(BF16)
            cp = pltpu.make_async_copy(bf_refs[k], _shard_window(out_refs[k], kinds[k], 2 * x + y), sems.at[k])
            cp.start()
            cps.append(cp)
        for cp in cps:
            cp.wait()

    return pl.pallas_call(
        body, name="cast_weights",
        out_shape=[jax.ShapeDtypeStruct(s, BF16) for s in full_shapes],
        in_specs=[VM] * n, out_specs=[ANY] * n,
        scratch_shapes=[pltpu.VMEM(s.shape, BF16) for s in shards] + [pltpu.SemaphoreType.DMA((n,))],
        compiler_params=_params(),
    )(*shards)


def _gather_copies(refs, kinds, sems_a, sems_b):
    x, y, c = _my_place()
    chips = [(1 - x, y), (x, 1 - y), (1 - x, 1 - y)]
    set_a, set_b = [], []
    for j, chip in enumerate(chips):
        for k in range(len(refs)):
            if refs[k] is None:
                continue
            for flip in ((0,) if k == 0 else (0, 1)):
                win = _landed(refs[k], kinds[k], (x, y), c)
                sems, idx = (sems_a, j) if k == 0 else (sems_b, ((k - 1) * 3 + j) * 2 + flip)
                mine = _landed(refs[k], kinds[k], chip, (c + flip) % 2)
                (set_a if k == 0 else set_b).append((
                    pltpu.make_async_remote_copy(
                        src_ref=win, dst_ref=win, send_sem=sems[0].at[idx], recv_sem=sems[1].at[idx],
                        device_id=(*chip, (c + flip) % 2), device_id_type=MESH),
                    pltpu.make_async_remote_copy(
                        src_ref=mine, dst_ref=mine, send_sem=sems[0].at[idx], recv_sem=sems[1].at[idx],
                        device_id=(*chip, (c + flip) % 2), device_id_type=MESH)))
    return set_a, set_b


_SPLIT = dict(has_side_effects=pltpu.SideEffectType.DATAFLOW_SIDE_EFFECTING)


def _gather_weights_start(fulls, kinds, after):
    n = len(fulls)
    hbm = pl.BlockSpec(memory_space=pltpu.HBM)
    sem = pl.BlockSpec(memory_space=pltpu.SEMAPHORE)
    nb = (n - 1) * 6

    def body(*refs):
        in_refs = refs[:n]
        sa, ra, sb, rb = refs[n + 1:n + 5]
        token = refs[-1]
        set_a, set_b = _gather_copies(in_refs, kinds, (sa, ra), (sb, rb))
        for out_cp, _ in set_a + set_b:
            out_cp.start()
        token[...] = jnp.zeros_like(token)

    out = pl.pallas_call(
        body, name="gather_weights_start",
        out_shape=[pltpu.SemaphoreType.DMA((3,)), pltpu.SemaphoreType.DMA((3,)),
                   pltpu.SemaphoreType.DMA((nb,)), pltpu.SemaphoreType.DMA((nb,))]
        + [pltpu.HBM(f.shape, f.dtype) for f in fulls] + [jax.ShapeDtypeStruct((8, LANES), F32)],
        in_specs=[hbm] * n + [ANY], out_specs=[sem] * 4 + [hbm] * n + [VM],
        input_output_aliases={k: 4 + k for k in range(n)},
        compiler_params=pltpu.CompilerParams(**_SPLIT),
    )(*[pltpu.with_memory_space_constraint(f, pltpu.HBM) for f in fulls], after)
    return (out[0], out[1]), (out[2], out[3]), out[4:4 + n], out[-1]


def _gather_weights_wait(fulls, kinds, sems, which, after, name):
    n = len(fulls)
    hbm = pl.BlockSpec(memory_space=pltpu.HBM)
    sem = pl.BlockSpec(memory_space=pltpu.SEMAPHORE)
    keep = [0] if which == 0 else list(range(1, n))

    def body(*refs):
        m = len(keep)
        full_refs = [None] * n
        for pos_, k in enumerate(keep):
            full_refs[k] = refs[pos_]
        s_ref, r_ref = refs[m:m + 2]
        if which == 0:
            sets = _gather_copies([full_refs[0]], kinds[:1], (s_ref, r_ref), None)[0]
        else:
            sets = _gather_copies([None] + [full_refs[k] for k in keep], kinds, None, (s_ref, r_ref))[1]
        for out_cp, in_cp in sets:
            out_cp.wait_send()
            in_cp.wait_recv()

    out = pl.pallas_call(
        body, name=name,
        out_shape=[pltpu.HBM(fulls[k].shape, fulls[k].dtype) for k in keep],
        in_specs=[hbm] * len(keep) + [sem, sem, ANY], out_specs=[hbm] * len(keep),
        input_output_aliases={i: i for i in range(len(keep))},
        compiler_params=pltpu.CompilerParams(**_SPLIT),
    )(*[fulls[k] for k in keep], sems[0], sems[1], after)
    return list(out)


def _pass_to_sibling(w_full, kind):
    def body(w_in_ref, w_ref, send_sems, recv_sems):
        del w_in_ref
        x, y, c = _my_place()
        chips = [(1 - x, y), (x, 1 - y), (1 - x, 1 - y)]
        cps = []
        for j, chip in enumerate(chips):
            win = _landed(w_ref, kind, chip, c)
            cp = pltpu.make_async_remote_copy(
                src_ref=win, dst_ref=win, send_sem=send_sems.at[j], recv_sem=recv_sems.at[j],
                device_id=(x, y, 1 - c), device_id_type=MESH)
            cp.start()
            cps.append(cp)
        for j, chip in enumerate(chips):
            theirs = _landed(w_ref, kind, chip, 1 - c)
            pltpu.make_async_remote_copy(
                src_ref=theirs, dst_ref=theirs, send_sem=send_sems.at[j], recv_sem=recv_sems.at[j],
                device_id=(x, y, 1 - c), device_id_type=MESH).wait_recv()
        for cp in cps:
            cp.wait_send()

    return pl.pallas_call(
        body, name="pass_to_sibling",
        out_shape=jax.ShapeDtypeStruct(w_full.shape, w_full.dtype),
        in_specs=[ANY], out_specs=ANY, input_output_aliases={0: 0},
        scratch_shapes=[pltpu.SemaphoreType.DMA((3,)), pltpu.SemaphoreType.DMA((3,))],
    )(w_full)


def _reduce_pair_exchange(grads, kinds):
    n = len(grads)
    half_shapes = [(g.shape[0] // 2, g.shape[1]) if kind == "col" else (g.shape[0], g.shape[1] // 2)
                   for g, kind in zip(grads, kinds)]

    def body(*refs):
        g_refs, pa_refs = refs[:n], refs[n:2 * n]
        send_sems, recv_sems = refs[2 * n:]
        x, y, c = _my_place()
        cps = []
        for k in range(n):
            cp = pltpu.make_async_remote_copy(
                src_ref=_half_window(g_refs[k], kinds[k], 1 - c), dst_ref=pa_refs[k],
                send_sem=send_sems.at[k], recv_sem=recv_sems.at[k],
                device_id=(x, y, 1 - c), device_id_type=MESH)
            cp.start()
            cps.append(cp)
        for cp in cps:
            cp.wait()

    return pl.pallas_call(
        body, name="reduce_pair_exchange",
        out_shape=[jax.ShapeDtypeStruct(s, F32) for s in half_shapes],
        in_specs=[ANY] * n, out_specs=[ANY] * n,
        scratch_shapes=[pltpu.SemaphoreType.DMA((n,)), pltpu.SemaphoreType.DMA((n,))],
    )(*grads)


def _row_tile(rows, cols, itemsize=4, target=2 * 1024 * 1024, mult=16):
    t = rows
    while t % 2 == 0 and t // 2 >= mult and (t // 2) % mult == 0 and t * cols * itemsize > target:
        t //= 2
    return t


def _reduce_pair_sum(g, pa, kind, c_arr, name):
    hr, hc_ = pa.shape
    tr = _row_tile(hr, hc_)
    nb = hr // tr

    def body(c_ref, g_ref, pa_ref, o_ref):
        o_ref[...] = (g_ref[...] + pa_ref[...]).astype(BF16)

    if kind == "col":
        g_map = lambda i, c_ref: (c_ref[0] * nb + i, 0)
    else:
        g_map = lambda i, c_ref: (i, c_ref[0])
    return pl.pallas_call(
        body, name=name,
        grid_spec=pltpu.PrefetchScalarGridSpec(
            num_scalar_prefetch=1, grid=(nb,),
            in_specs=[pl.BlockSpec((tr, hc_), g_map), pl.BlockSpec((tr, hc_), lambda i, c_ref: (i, 0))],
            out_specs=pl.BlockSpec((tr, hc_), lambda i, c_ref: (i, 0))),
        out_shape=jax.ShapeDtypeStruct((hr, hc_), BF16),
        compiler_params=_params(("parallel",)),
    )(c_arr, g, pa)


def _half_shard_shape(full_shape, kind):
    r, c = full_shape
    return (r // 2, c // 4) if kind == "col" else (r // 4, c // 2)


def _to_owner_copies(h_refs, land_refs, send_sems, recv_sems, kinds):
    n = len(h_refs)
    x, y, c = _my_place()
    chips = [(1 - x, y), (x, 1 - y), (1 - x, 1 - y)]
    cps = []
    for j, chip in enumerate(chips):
        pj = 2 * chip[0] + chip[1]
        for k in range(n):
            cps.append(pltpu.make_async_remote_copy(
                src_ref=_shard_window(h_refs[k], kinds[k], pj), dst_ref=land_refs[k].at[j],
                send_sem=send_sems.at[j * n + k], recv_sem=recv_sems.at[j * n + k],
                device_id=(*chip, c), device_id_type=MESH))
    return cps


def _reduce_to_owner_start(halves, kinds, full_shapes):
    n = len(halves)
    hs = [_half_shard_shape(fs, kind) for fs, kind in zip(full_shapes, kinds)]
    hbm = pl.BlockSpec(memory_space=pltpu.HBM)
    sem = pl.BlockSpec(memory_space=pltpu.SEMAPHORE)

    def body(*refs):
        h_refs, land_refs = refs[:n], refs[n:2 * n]
        send_sems, recv_sems = refs[2 * n:2 * n + 2]
        token = refs[-1]
        for cp in _to_owner_copies(h_refs, land_refs, send_sems, recv_sems, kinds):
            cp.start()
        token[...] = jnp.zeros_like(token)

    lands = [pltpu.with_memory_space_constraint(lax.empty((3,) + s, BF16), pltpu.HBM) for s in hs]
    out = pl.pallas_call(
        body, name="reduce_to_owner_start",
        out_shape=[pltpu.SemaphoreType.DMA((3 * n,)), pltpu.SemaphoreType.DMA((3 * n,))]
        + [pltpu.HBM(h.shape, h.dtype) for h in halves] + [pltpu.HBM((3,) + s, BF16) for s in hs]
        + [jax.ShapeDtypeStruct((8, LANES), F32)],
        in_specs=[hbm] * (2 * n), out_specs=[sem, sem] + [hbm] * (2 * n) + [VM],
        input_output_aliases={k: 2 + k for k in range(2 * n)},
        compiler_params=pltpu.CompilerParams(has_side_effects=pltpu.SideEffectType.DATAFLOW_SIDE_EFFECTING),
    )(*[pltpu.with_memory_space_constraint(h, pltpu.HBM) for h in halves], *lands)
    return out[0], out[1], out[2:2 + n], out[2 + n:2 + 2 * n], out[-1]


def _reduce_to_owner_wait(send_sems, recv_sems, halves, lands, after, kinds):
    n = len(halves)
    hbm = pl.BlockSpec(memory_space=pltpu.HBM)
    sem = pl.BlockSpec(memory_space=pltpu.SEMAPHORE)

    def body(*refs):
        h_refs, land_refs = refs[:n], refs[n:2 * n]
        send_s, recv_s = refs[2 * n:2 * n + 2]
        for cp in _to_owner_copies(h_refs, land_refs, send_s, recv_s, kinds):
            cp.wait_send()
            cp.wait_recv()

    out = pl.pallas_call(
        body, name="reduce_to_owner_wait",
        out_shape=[pltpu.HBM(h.shape, h.dtype) for h in halves] + [pltpu.HBM(l.shape, l.dtype) for l in lands],
        in_specs=[hbm] * (2 * n) + [sem, sem, ANY], out_specs=[hbm] * (2 * n),
        input_output_aliases={k: k for k in range(2 * n)},
        compiler_params=pltpu.CompilerParams(has_side_effects=pltpu.SideEffectType.DATAFLOW_SIDE_EFFECTING),
    )(*halves, *lands, send_sems, recv_sems, after)
    return out[:n], out[n:]


def _reduce_finish(halves, recvd, kinds, full_shapes):
    n = len(halves)
    hs = [_half_shard_shape(fs, kind) for fs, kind in zip(full_shapes, kinds)]
    shard_shapes = [(fs[0], fs[1] // 4) if kind == "col" else (fs[0] // 4, fs[1])
                    for fs, kind in zip(full_shapes, kinds)]

    def body(*refs):
        h_refs, rc_refs, gs_refs = refs[:n], refs[n:2 * n], refs[2 * n:3 * n]
        own_refs, gh_refs = refs[3 * n:4 * n], refs[4 * n:5 * n]
        in_sems, loc_sems, send_sems, recv_sems = refs[5 * n:]
        x, y, c = _my_place()
        p = 2 * x + y
        loads = []
        for k in range(n):
            cp = pltpu.make_async_copy(_shard_window(h_refs[k], kinds[k], p), own_refs[k], in_sems.at[k])
            cp.start()
            loads.append(cp)
        outs = []
        for k in range(n):
            loads[k].wait()
            gh_refs[k][...] = (own_refs[k][...].astype(F32) + rc_refs[k][0].astype(F32)
                               + rc_refs[k][1].astype(F32) + rc_refs[k][2].astype(F32))
            dst = _half_window(gs_refs[k], kinds[k], c)
            lc = pltpu.make_async_copy(gh_refs[k], dst, loc_sems.at[k])
            lc.start()
            rc = pltpu.make_async_remote_copy(
                src_ref=gh_refs[k], dst_ref=dst, send_sem=send_sems.at[k], recv_sem=recv_sems.at[k],
                device_id=(x, y, 1 - c), device_id_type=MESH)
            rc.start()
            outs.append((lc, rc))
        for k, (lc, rc) in enumerate(outs):
            lc.wait()
            rc.wait_send()
            pltpu.make_async_remote_copy(
                src_ref=gh_refs[k], dst_ref=_half_window(gs_refs[k], kinds[k], 1 - c),
                send_sem=send_sems.at[k], recv_sem=recv_sems.at[k],
                device_id=(x, y, 1 - c), device_id_type=MESH).wait_recv()

    return pl.pallas_call(
        body, name="reduce_finish",
        out_shape=[jax.ShapeDtypeStruct(s, F32) for s in shard_shapes],
        in_specs=[ANY] * n + [VM] * n, out_specs=[ANY] * n,
        scratch_shapes=[pltpu.VMEM(s, BF16) for s in hs] + [pltpu.VMEM(s, F32) for s in hs]
        + [pltpu.SemaphoreType.DMA((n,)) for _ in range(4)],
        compiler_params=_params(),
    )(*halves, *recvd)


def _mod_part(c_all, w_ada_l, b_l):
    def body(c_ref, w_ref, b_ref, o_ref):
        o_ref[...] = _dot(c_ref[...].astype(BF16), w_ref[...].astype(BF16)) + b_ref[...]

    return pl.pallas_call(
        body, name="mod_part", out_shape=jax.ShapeDtypeStruct((8, w_ada_l.shape[1]), F32),
        in_specs=[VM, VM, VM], out_specs=VM, compiler_params=_params(),
    )(c_all, w_ada_l, b_l)


def _sum_devices(parts, m):
    def body(p_ref, o_ref):
        acc = p_ref[0:m, :]
        for d in range(1, 8):
            acc = acc + p_ref[d * m:(d + 1) * m, :]
        o_ref[...] = acc

    return pl.pallas_call(
        body, name="sum_devices", out_shape=jax.ShapeDtypeStruct((m, parts.shape[1]), F32),
        in_specs=[VM], out_specs=VM, compiler_params=_params(),
    )(parts)


def _grad_w_ada(c_all_t, dmod_l):
    d, w = c_all_t.shape[0], dmod_l.shape[1]

    def body(ct_ref, dm_ref, o_ref):
        acc = ct_ref[:, 0:1] * dm_ref[0:1, :]
        for b in range(1, 8):
            acc = acc + ct_ref[:, b:b + 1] * dm_ref[b:b + 1, :]
        o_ref[...] = acc

    return pl.pallas_call(
        body, name="grad_w_ada", out_shape=jax.ShapeDtypeStruct((d, w), F32),
        in_specs=[VM, VM], out_specs=VM, compiler_params=_params(),
    )(c_all_t, dmod_l)


def _adamw(w, g, m, v, name):
    r, c = w.shape
    tr = _row_tile(r, c, target=1024 * 1024, mult=8)
    c1 = 1.0 - ADAM_B1 ** ADAM_STEP
    c2 = 1.0 - ADAM_B2 ** ADAM_STEP

    def body(w_ref, g_ref, m_ref, v_ref, d_ref, nm_ref, nv_ref):
        gv = g_ref[...]
        nm = ADAM_B1 * m_ref[...] + (1.0 - ADAM_B1) * gv
        nv = ADAM_B2 * v_ref[...] + (1.0 - ADAM_B2) * (gv * gv)
        m_hat = nm / c1
        v_hat = nv / c2
        d_ref[...] = -ADAM_LR * (m_hat / (jnp.sqrt(v_hat) + ADAM_EPS) + ADAM_WD * w_ref[...])
        nm_ref[...] = nm
        nv_ref[...] = nv

    spec = pl.BlockSpec((tr, c), lambda i: (i, 0))
    return pl.pallas_call(
        body, name=name, grid=(r // tr,),
        out_shape=[jax.ShapeDtypeStruct((r, c), F32)] * 3,
        in_specs=[spec] * 4, out_specs=[spec] * 3,
        compiler_params=_params(("parallel",)),
    )(w, g, m, v)


def _rope_tables(pos_ref, tab_ref):
    ang = pos_ref[...].astype(F32) * tab_ref[0:1, :]
    cs, sn = jnp.cos(ang), jnp.sin(ang)
    return jnp.where(tab_ref[3:4, :] > 0, cs, 1.0), -sn * tab_ref[1:2, :], sn * tab_ref[2:3, :]


def _deinterleave_store(vals, slab, out_ref, d, ts, dtype):
    if d == 1:
        for s in range(4):
            out_ref[0, :, s * LANES:(s + 1) * LANES] = vals[s].astype(dtype)
        return
    for s in range(4):
        slab[s] = vals[s]
    for r in range(d):
        for s in range(4):
            out_ref[r, :, s * LANES:(s + 1) * LANES] = slab[s, pl.ds(r, ts // d, stride=d), :].astype(dtype)


def _interleave_load(blk_ref, slab, d, ts):
    if d == 1:
        return [blk_ref[0, :, s * LANES:(s + 1) * LANES] for s in range(4)]
    for r in range(d):
        for s in range(4):
            slab[s, pl.ds(r, ts // d, stride=d), :] = blk_ref[r, :, s * LANES:(s + 1) * LANES]
    return [slab[s] for s in range(4)]


def _input_projection(x, modv, norm_g, w_in, ts):
    s_len, d_model = x.shape
    ci = w_in.shape[1]
    ct = COL_TILE
    nc, ng = 3 * d_model // ct, (ATT_W + 2 * d_model) // ct
    assert ci == (nc + 9 + ng) * ct and s_len % ts == 0

    def body(x_ref, mod_ref, g_ref, w_ref, pc_ref, pg_ref, h_ref, ht_ref):
        j = pl.program_id(1)

        @pl.when(j == 0)
        def _():
            xv = x_ref[...]
            r = lax.rsqrt(jnp.mean(xv * xv, axis=-1, keepdims=True) + EPS)
            h = (xv * r) * g_ref[...] * (1.0 + mod_ref[1:2, :]) + mod_ref[0:1, :]
            h_ref[...] = h.astype(BF16)
            ht_ref[...] = h.T.astype(BF16)

        res = _dot(h_ref[...], w_ref[...])

        @pl.when(j < nc)
        def _():
            pc_ref[...] = res

        @pl.when(j >= nc)
        def _():
            pg_ref[...] = res

    return pl.pallas_call(
        body, name="input_projection", grid=(s_len // ts, nc + ng),
        in_specs=[pl.BlockSpec((ts, d_model), lambda i, j: (i, 0)),
                  pl.BlockSpec((8, d_model), lambda i, j: (0, 0)),
                  pl.BlockSpec((1, d_model), lambda i, j: (0, 0)),
                  pl.BlockSpec((d_model, ct), lambda i, j: (0, jnp.where(j < nc, j, j + 9)))],
        out_specs=[pl.BlockSpec((ts, ct), lambda i, j: (i, jnp.clip(j, 0, nc - 1))),
                   pl.BlockSpec((ts, ct), lambda i, j: (i, jnp.clip(j - nc, 0, ng - 1))),
                   pl.BlockSpec((ts, d_model), lambda i, j: (i, 0)),
                   pl.BlockSpec((d_model, ts), lambda i, j: (0, i))],
        out_shape=[jax.ShapeDtypeStruct((s_len, 3 * d_model), F32),
                   jax.ShapeDtypeStruct((s_len, ATT_W + 2 * d_model), F32),
                   jax.ShapeDtypeStruct((s_len, d_model), BF16),
                   jax.ShapeDtypeStruct((d_model, s_len), BF16)],
        compiler_params=_params(("arbitrary", "arbitrary")),
    )(x, modv, norm_g, w_in)


def _qkv_projection(h, pos, ropetab, w_in, gi, ts):
    s_len, d_model = h.shape
    dil = DILATIONS[gi]
    nc = 3 * d_model // COL_TILE
    hr = ts // 2
    assert COL_TILE == ATT_W and hr % (16 * dil) == 0

    def body(h_ref, pos_ref, tab_ref, wq_ref, wk_ref, wv_ref, o_ref, slab):
        rc, ra, rb = _rope_tables(pos_ref, tab_ref)
        w_refs = (wq_ref, wk_ref, wv_ref)
        units = [(w, half) for w in range(3) for half in range(2)]

        def matmul(w, half):
            return _dot(h_ref[half * hr:(half + 1) * hr, :], w_refs[w][...])

        def finish(w, half, res):
            rows = slice(half * hr, (half + 1) * hr)
            vals = []
            for s in range(4):
                t = res[:, s * LANES:(s + 1) * LANES]
                if w < 2:
                    t = (t * rc[rows, :] + pltpu.roll(t, LANES - 8, 1) * ra[rows, :]
                         + pltpu.roll(t, 8, 1) * rb[rows, :])
                vals.append(t)
            out = o_ref.at[w, :, half * (hr // dil):(half + 1) * (hr // dil), :]
            _deinterleave_store(vals, slab.at[half], out, dil, hr, BF16)

        res_next = matmul(*units[0])
        for ui, (w, half) in enumerate(units):
            res = res_next
            if ui + 1 < len(units):
                res_next = matmul(*units[ui + 1])
            finish(w, half, res)

    def w_spec(w):
        return pl.BlockSpec((d_model, COL_TILE), lambda i: (0, nc + 3 * w + gi))

    return pl.pallas_call(
        body, name="qkv_projection_%d" % dil, grid=(s_len // ts,),
        in_specs=[pl.BlockSpec((ts, d_model), lambda i: (i, 0)), pl.BlockSpec((ts, 1), lambda i: (i, 0)),
                  pl.BlockSpec((8, LANES), lambda i: (0, 0)), w_spec(0), w_spec(1), w_spec(2)],
        out_specs=pl.BlockSpec((3, dil, ts // dil, ATT_W), lambda i: (0, 0, i, 0)),
        out_shape=jax.ShapeDtypeStruct((3, dil, s_len // dil, ATT_W), BF16),
        scratch_shapes=[pltpu.VMEM((2, 4, hr, LANES), F32)],
        compiler_params=_params(("parallel",)),
    )(h, pos, ropetab, w_in, w_in, w_in)


def _layernorm_stats(u1):
    mu = jnp.mean(u1, axis=-1, keepdims=True)
    xc = u1 - mu
    rstd = lax.rsqrt(jnp.mean(xc * xc, axis=-1, keepdims=True) + EPS)
    return xc * rstd, rstd


def _shifted_copies(win, shf, ts):
    rows = ts + HALO - 8
    for b in range(1, 8):
        shf[b - 1, 0:rows, :] = win[pl.ds(b, rows), :]


def _tap(win, shf, off, r0, rows):
    a, b = divmod(off, 8)
    start = 8 * a + r0
    if b == 0:
        return win[start:start + rows, :]
    return shf[b - 1, start:start + rows, :]


def _conv_forward(p_conv, conv_w, conv_b, ln_g, ln_b, w_co, ts):
    s_len, d3 = p_conv.shape
    dm = d3 // 3

    def body(p_ref, cw_ref, cb_ref, g_ref, b_ref, w_ref, y_ref, u0_ref, u1_ref, win, shf):
        i = pl.program_id(0)

        @pl.when(i == 0)
        def _():
            win[0:HALO, :] = jnp.zeros((HALO, dm), F32)

        a, b, z = p_ref[:, 0:dm], p_ref[:, dm:2 * dm], p_ref[:, 2 * dm:3 * dm]
        u0 = a * _sig(b)
        win[HALO:HALO + ts, :] = u0
        u0_ref[...] = u0
        _shifted_copies(win, shf, ts)
        for r0 in range(0, ts, CONV_ROWS_FWD):
            acc = jnp.broadcast_to(cb_ref[...], (CONV_ROWS_FWD, dm))
            for k in range(CONV_K):
                acc = acc + cw_ref[k:k + 1, :] * _tap(win, shf, HALO - (CONV_K - 1) + k, r0, CONV_ROWS_FWD)
            u1_ref[r0:r0 + CONV_ROWS_FWD, :] = acc
        xh, _ = _layernorm_stats(u1_ref[...])
        u2 = xh * g_ref[...] + b_ref[...]
        a_conv = (u2 * _sig(u2)) * (z * _sig(z))
        y_ref[...] = _dot(a_conv.astype(BF16), w_ref[...])
        win[0:HALO, :] = win[ts:ts + HALO, :]

    row = pl.BlockSpec((1, dm), lambda i: (0, 0))
    tile = pl.BlockSpec((ts, dm), lambda i: (i, 0))
    return pl.pallas_call(
        body, name="conv_forward", grid=(s_len // ts,),
        in_specs=[pl.BlockSpec((ts, d3), lambda i: (i, 0)),
                  pl.BlockSpec((HALO, dm), lambda i: (0, 0)), row, row, row,
                  pl.BlockSpec((dm, dm), lambda i: (0, 0))],
        out_specs=[tile, tile, tile],
        out_shape=[jax.ShapeDtypeStruct((s_len, dm), F32)] * 3,
        scratch_shapes=[pltpu.VMEM((ts + HALO, dm), F32), pltpu.VMEM((7, ts + HALO - 8, dm), F32)],
        compiler_params=_params(("arbitrary",)),
    )(p_conv, conv_w, conv_b, ln_g, ln_b, w_co)


def _conv_backward(dp, dyc, p_conv, u0, u1, conv_w, ln_g, ln_b, w_co, ts):
    s_len, d3 = p_conv.shape
    dm = d3 // 3
    nt = s_len // ts
    hb = ts // HALO

    def body(dp_in, dy_ref, p_ref, u0_ref, uh_ref, u1_ref, cw_ref, g_ref, b_ref, w_ref,
             dp_ref, gw_ref, gs_ref, gcw_ref, dwin, uwin, shf):
        del dp_in
        i = pl.program_id(0)
        ti = nt - 1 - i

        @pl.when(i == 0)
        def _():
            gw_ref[...] = jnp.zeros_like(gw_ref)
            gs_ref[...] = jnp.zeros_like(gs_ref)
            gcw_ref[...] = jnp.zeros_like(gcw_ref)
            dwin[ts:ts + HALO, :] = jnp.zeros((HALO, dm), F32)

        dy = dy_ref[...]
        z = p_ref[:, 2 * dm:3 * dm]
        d_ac = _dot_nt(dy, w_ref[...])
        xh, rstd = _layernorm_stats(u1_ref[...])
        u2 = xh * g_ref[...] + b_ref[...]
        sg2, sgz = _sig(u2), _sig(z)
        u3, sz = u2 * sg2, z * sgz
        gw_ref[...] += _dot_tn((u3 * sz).astype(BF16), dy)
        d_z = d_ac * u3 * _dsilu(z, sgz)
        d_u2 = d_ac * sz * _dsilu(u2, sg2)
        gs_ref[0:1, :] += jnp.sum(d_u2 * xh, axis=0, keepdims=True)
        gs_ref[1:2, :] += jnp.sum(d_u2, axis=0, keepdims=True)
        dxh = d_u2 * g_ref[...]
        d_u1 = rstd * (dxh - jnp.mean(dxh, axis=-1, keepdims=True)
                       - xh * jnp.mean(dxh * xh, axis=-1, keepdims=True))
        gs_ref[2:3, :] += jnp.sum(d_u1, axis=0, keepdims=True)
        dwin[0:ts, :] = d_u1
        uwin[0:HALO, :] = jnp.where(ti == 0, 0.0, uh_ref[...])
        uwin[HALO:HALO + ts, :] = u0_ref[...]
        dp_ref[:, 2 * dm:3 * dm] = d_z.astype(BF16)
        _shifted_copies(uwin, shf, ts)
        for k in range(CONV_K):
            part = jnp.zeros((CONV_ROWS_BWD, dm), F32)
            for r0 in range(0, ts, CONV_ROWS_BWD):
                part = part + dwin[r0:r0 + CONV_ROWS_BWD, :] * _tap(uwin, shf, HALO - (CONV_K - 1) + k, r0,
                                                                    CONV_ROWS_BWD)
            gcw_ref[k:k + 1, :] += jnp.sum(part, axis=0, keepdims=True)
        _shifted_copies(dwin, shf, ts)
        for r0 in range(0, ts, CONV_ROWS_BWD):
            d_u0 = jnp.zeros((CONV_ROWS_BWD, dm), F32)
            for k in range(CONV_K):
                d_u0 = d_u0 + cw_ref[k:k + 1, :] * _tap(dwin, shf, CONV_K - 1 - k, r0, CONV_ROWS_BWD)
            rows = slice(r0, r0 + CONV_ROWS_BWD)
            sgb = _sig(p_ref[rows, dm:2 * dm])
            dp_ref[rows, 0:dm] = (d_u0 * sgb).astype(BF16)
            dp_ref[rows, dm:2 * dm] = (d_u0 * p_ref[rows, 0:dm] * sgb * (1.0 - sgb)).astype(BF16)
        dwin[ts:ts + HALO, :] = dwin[0:HALO, :]

    rev = lambda i: (nt - 1 - i, 0)
    row = pl.BlockSpec((1, dm), lambda i: (0, 0))
    tile = pl.BlockSpec((ts, dm), rev)
    return pl.pallas_call(
        body, name="conv_backward", grid=(nt,),
        in_specs=[ANY, tile, pl.BlockSpec((ts, d3), rev), tile,
                  pl.BlockSpec((HALO, dm), lambda i: (jnp.maximum((nt - 1 - i) * hb - 1, 0), 0)),
                  tile, pl.BlockSpec((HALO, dm), lambda i: (0, 0)), row, row,
                  pl.BlockSpec((dm, dm), lambda i: (0, 0))],
        out_specs=[pl.BlockSpec((ts, d3), rev), pl.BlockSpec((dm, dm), lambda i: (0, 0)),
                   pl.BlockSpec((8, dm), lambda i: (0, 0)), pl.BlockSpec((HALO, dm), lambda i: (0, 0))],
        out_shape=[jax.ShapeDtypeStruct(dp.shape, BF16), jax.ShapeDtypeStruct((dm, dm), F32),
                   jax.ShapeDtypeStruct((8, dm), F32), jax.ShapeDtypeStruct((HALO, dm), F32)],
        input_output_aliases={0: 0},
        scratch_shapes=[pltpu.VMEM((ts + HALO, dm), F32), pltpu.VMEM((ts + HALO, dm), F32),
                        pltpu.VMEM((7, ts + HALO - 8, dm), F32)],
        compiler_params=_params(("arbitrary",)),
    )(dp, dyc, p_conv, u0, u0, u1, conv_w, ln_g, ln_b, w_co)


def _att_masks():
    head0 = lax.broadcasted_iota(jnp.int32, (BLK, LANES), 1) < 64
    col = lax.broadcasted_iota(jnp.int32, (BLK, 4 * BLK), 1)
    row = lax.broadcasted_iota(jnp.int32, (BLK, 4 * BLK), 0)
    kj = col % BLK
    prev = jnp.where(col < 2 * BLK, 1, 0)
    band = jnp.where(col < 2 * BLK, kj - row, row - kj)
    return head0, band, prev


def _fill_block_diagonal(dst, slab, src_ref, halo_ref, head0, nq):
    sl = slice(slab * LANES, (slab + 1) * LANES)
    for b in range(nq + 1):
        blk = halo_ref[:, sl] if b == 0 else src_ref[(b - 1) * BLK:b * BLK, sl]
        base = (slab * (nq + 1) + b) * 2 * BLK
        zero = jnp.zeros_like(blk)
        dst[base:base + BLK, :] = jnp.where(head0, blk, zero)
        dst[base + BLK:base + 2 * BLK, :] = jnp.where(head0, zero, blk)


def _attention_forward(qkv, seq_len, qt, name):
    s_len = qkv.shape[1]
    nq = qt // BLK
    tiles_per_seq = seq_len // qt

    def body(q_ref, k_ref, v_ref, kh_ref, vh_ref, o_ref, lse_ref, kbd, vbd):
        first = jnp.where((pl.program_id(0) % tiles_per_seq) == 0, 4 * BLK, 0)
        head0, band, prev = _att_masks()
        band_first = band - prev * first
        for p in range(4):
            _fill_block_diagonal(kbd, p, k_ref, kh_ref, head0, nq)
            _fill_block_diagonal(vbd, p, v_ref, vh_ref, head0, nq)
        units = [(p, n) for p in range(4) for n in range(nq)]

        def keys_of(p, n):
            base = (p * (nq + 1) + n) * 2 * BLK
            return slice(base, base + 4 * BLK)

        def scores(p, n):
            q2 = q_ref[n * BLK:(n + 1) * BLK, p * LANES:(p + 1) * LANES] * 0.125
            return _dot_nt(q2, kbd[keys_of(p, n), :])

        def finish(p, n, o, den, lse):
            rows, sl = slice(n * BLK, (n + 1) * BLK), slice(p * LANES, (p + 1) * LANES)
            o_ref[rows, sl] = o / jnp.where(head0, den[0], den[1])
            lse_ref[rows, sl] = jnp.where(head0, lse[0], lse[1])

        s_next = scores(*units[0])
        pending = None
        for ui, (p, n) in enumerate(units):
            s = s_next
            if ui + 1 < len(units):
                s_next = scores(*units[ui + 1])
            s = jnp.where((band_first if n == 0 else band) >= 0, s, NEG)
            grp = [s[:, g * BLK:(g + 1) * BLK] for g in range(4)]
            ps, den, lse = [None] * 4, [], []
            for h in range(2):
                m = jnp.max(jnp.maximum(grp[h], grp[2 + h]), axis=-1, keepdims=True)
                ps[h], ps[2 + h] = jnp.exp(grp[h] - m), jnp.exp(grp[2 + h] - m)
                dn = jnp.sum(ps[h] + ps[2 + h], axis=-1, keepdims=True)
                den.append(dn)
                lse.append(m + jnp.log(dn))
            pmat = jnp.concatenate([x.astype(BF16) for x in ps], axis=1)
            o = _dot(pmat, vbd[keys_of(p, n), :])
            if pending is not None:
                finish(*pending)
            pending = (p, n, o, den, lse)
        finish(*pending)

    def which(w):
        return pl.BlockSpec((None, qt, ATT_W), lambda i: (w, i, 0))

    def halo(w):
        return pl.BlockSpec((None, BLK, ATT_W), lambda i: (w, jnp.maximum(i * nq - 1, 0), 0))

    out = pl.BlockSpec((qt, ATT_W), lambda i: (i, 0))
    bd = pltpu.VMEM((4 * (nq + 1) * 2 * BLK, LANES), BF16)
    return pl.pallas_call(
        body, name=name, grid=(s_len // qt,),
        in_specs=[which(0), which(1), which(2), halo(1), halo(2)],
        out_specs=[out, out], out_shape=[jax.ShapeDtypeStruct((s_len, ATT_W), F32)] * 2,
        scratch_shapes=[bd, bd],
        compiler_params=_params(("parallel",)),
    )(qkv, qkv, qkv, qkv, qkv)


def _attention_backward(qkv, d_att, lse, delta, seq_len, qt, name):
    s_len = qkv.shape[1]
    nq = qt // BLK
    tiles_per_seq = seq_len // qt
    nblk = s_len // BLK

    def body(q_ref, k_ref, v_ref, kh_ref, vh_ref, do_ref, lse_ref, dl_ref,
             qn_ref, don_ref, lsen_ref, dln_ref, dqkv_ref, kbd, vbd):
        i = pl.program_id(0)
        first = jnp.where((i % tiles_per_seq) == 0, 4 * BLK, 0)
        last = jnp.where((i % tiles_per_seq) == tiles_per_seq - 1, 4 * BLK, 0)
        head0, band, prev = _att_masks()
        band_first = band - prev * first
        band_tail = band[:, 0:2 * BLK] - last
        for p in range(4):
            _fill_block_diagonal(kbd, p, k_ref, kh_ref, head0, nq)
            _fill_block_diagonal(vbd, p, v_ref, vh_ref, head0, nq)
        units = [(p, n) for p in range(4) for n in range(nq + 1)]

        def stage_a(p, n):
            sl = slice(p * LANES, (p + 1) * LANES)
            base = (p * (nq + 1) + n) * 2 * BLK
            if n < nq:
                rows = slice(n * BLK, (n + 1) * BLK)
                q2, do2, lse2, dl2 = q_ref[rows, sl], do_ref[rows, sl], lse_ref[rows, sl], dl_ref[rows, sl]
                keys = slice(base, base + 4 * BLK)
            else:
                q2, do2, lse2, dl2 = qn_ref[:, sl], don_ref[:, sl], lsen_ref[:, sl], dln_ref[:, sl]
                keys = slice(base, base + 2 * BLK)
            s = _dot_nt(q2 * 0.125, kbd[keys, :])
            dp = _dot_nt(do2, vbd[keys, :])
            return q2, do2, lse2, dl2, keys, s, dp

        def stage_b(n, lse2, dl2, s, dp):
            mask = band_tail if n == nq else (band_first if n == 0 else band)
            ps, dss = [], []
            for g in range(s.shape[1] // BLK):
                h = g % 2
                cols = slice(g * BLK, (g + 1) * BLK)
                pg = jnp.exp(jnp.where(mask[:, cols] >= 0, s[:, cols] - lse2[:, h * 64:h * 64 + 1], NEG))
                ps.append(pg.astype(BF16))
                dss.append((pg * (dp[:, cols] - dl2[:, h * 64:h * 64 + 1]) * 0.125).astype(BF16))
            return jnp.concatenate(ps, axis=1), jnp.concatenate(dss, axis=1)

        def heads(r, g):
            return jnp.where(head0, r[g * BLK:(g + 1) * BLK, :], r[(g + 1) * BLK:(g + 2) * BLK, :])

        a_next = stage_a(*units[0])
        carry = None
        for ui, (p, n) in enumerate(units):
            q2, do2, lse2, dl2, keys, s, dp = a_next
            if ui + 1 < len(units):
                a_next = stage_a(*units[ui + 1])
            pmat, dsmat = stage_b(n, lse2, dl2, s, dp)
            sl = slice(p * LANES, (p + 1) * LANES)
            if n < nq:
                dqkv_ref[0, n * BLK:(n + 1) * BLK, sl] = _dot(dsmat, kbd[keys, :])
            dkbd = _dot_tn(dsmat, q2)
            dvbd = _dot_tn(pmat, do2)
            if n > 0:
                prow = slice((n - 1) * BLK, n * BLK)
                dqkv_ref[1, prow, sl] = carry[0] + heads(dkbd, 0)
                dqkv_ref[2, prow, sl] = carry[1] + heads(dvbd, 0)
            if n < nq:
                carry = (heads(dkbd, 2), heads(dvbd, 2))

    def which(w):
        return pl.BlockSpec((None, qt, ATT_W), lambda i: (w, i, 0))

    def prev(w):
        return pl.BlockSpec((None, BLK, ATT_W), lambda i: (w, jnp.maximum(i * nq - 1, 0), 0))

    tile = pl.BlockSpec((qt, ATT_W), lambda i: (i, 0))
    nxt = pl.BlockSpec((BLK, ATT_W), lambda i: (jnp.minimum((i + 1) * nq, nblk - 1), 0))
    nxt_q = pl.BlockSpec((None, BLK, ATT_W), lambda i: (0, jnp.minimum((i + 1) * nq, nblk - 1), 0))
    return pl.pallas_call(
        body, name=name, grid=(s_len // qt,),
        in_specs=[which(0), which(1), which(2), prev(1), prev(2), tile, tile, tile, nxt_q, nxt, nxt, nxt],
        out_specs=pl.BlockSpec((3, qt, ATT_W), lambda i: (0, i, 0)),
        out_shape=jax.ShapeDtypeStruct((3, s_len, ATT_W), F32),
        scratch_shapes=[pltpu.VMEM((4 * (nq + 1) * 2 * BLK, LANES), BF16)] * 2,
        compiler_params=_params(("parallel",)),
    )(qkv, qkv, qkv, qkv, qkv, d_att, lse, delta, qkv, d_att, lse, delta)


def _merge_and_head(o_g, lse_g, p_gate, y_conv, x, tgt, modv, final_g, w_ao, w_o, ts):
    s_len, dm = x.shape
    gw = ATT_W + 2 * dm
    nt = s_len // ts
    gate_off = 3 * dm + 9 * ATT_W
    assert gate_off % gw == 0

    def body(o0, o1, o2, l0, l1, l2, pg_ref, yc_ref, x_ref, t_ref, mod_ref, fg_ref, wao_ref, wo_ref,
             loss_ref, dx_ref, dyc_ref, dpg_ref, da0, da1, da2, ls0, ls1, ls2, de0, de1, de2,
             gwo_ref, gwao_ref, gs_ref, slab):
        i = pl.program_id(0)

        @pl.when(i == 0)
        def _():
            loss_ref[...] = jnp.zeros_like(loss_ref)
            gwo_ref[...] = jnp.zeros_like(gwo_ref)
            gwao_ref[...] = jnp.zeros_like(gwao_ref)
            gs_ref[...] = jnp.zeros_like(gs_ref)

        os_, ls_ = [], []
        for dil, o_ref, l_ref in zip(DILATIONS, (o0, o1, o2), (l0, l1, l2)):
            os_.append(jnp.concatenate(_interleave_load(o_ref, slab, dil, ts), axis=1))
            ls_.append(jnp.concatenate(_interleave_load(l_ref, slab, dil, ts), axis=1))
        mx = jnp.maximum(jnp.maximum(ls_[0], ls_[1]), ls_[2])
        wts = [jnp.exp(l - mx) for l in ls_]
        wsum = wts[0] + wts[1] + wts[2]
        att = (wts[0] * os_[0] + wts[1] * os_[1] + wts[2] * os_[2]) / wsum
        lse_all = mx + jnp.log(wsum)

        z_att, g_conv, g_att = pg_ref[:, 0:ATT_W], pg_ref[:, ATT_W:ATT_W + dm], pg_ref[:, ATT_W + dm:gw]
        sgz = _sig(z_att)
        sz = z_att * sgz
        a_att = (att * sz).astype(BF16)
        y_att = _dot(a_att, wao_ref[...])
        y_conv = yc_ref[...]
        sgc, sga = _sig(g_conv), _sig(g_att)
        merged = (sgc * y_conv + sga * y_att).astype(BF16)
        mo = _dot(merged, wo_ref[...])
        gate = mod_ref[2:3, :]
        x2 = x_ref[...] + gate * mo
        r = lax.rsqrt(jnp.mean(x2 * x2, axis=-1, keepdims=True) + EPS)
        xr = x2 * r
        err = xr * fg_ref[...] - t_ref[...]
        loss_ref[...] += 0.5 * jnp.sum(jnp.mean(err * err, axis=-1, keepdims=True))
        dy = err * (1.0 / dm)
        gs_ref[0:1, :] += jnp.sum(dy * xr, axis=0, keepdims=True)
        dyg = dy * fg_ref[...]
        d_x2 = r * dyg - xr * (r * jnp.mean(dyg * xr, axis=-1, keepdims=True))
        dx_ref[...] = d_x2
        gs_ref[1:2, :] += jnp.sum(d_x2 * mo, axis=0, keepdims=True)
        d_mo = (d_x2 * gate).astype(BF16)
        d_mg = _dot_nt(d_mo, wo_ref[...])
        gwo_ref[...] += _dot_tn(merged, d_mo)
        dyc_ref[...] = (d_mg * sgc).astype(BF16)
        dpg_ref[:, ATT_W:ATT_W + dm] = (d_mg * y_conv * sgc * (1.0 - sgc)).astype(BF16)
        d_ya = (d_mg * sga).astype(BF16)
        dpg_ref[:, ATT_W + dm:gw] = (d_mg * y_att * sga * (1.0 - sga)).astype(BF16)
        gwao_ref[...] += _dot_tn(a_att, d_ya)
        d_aa = _dot_nt(d_ya, wao_ref[...])
        dpg_ref[:, 0:ATT_W] = (d_aa * att * _dsilu(z_att, sgz)).astype(BF16)
        d_att = d_aa * sz
        ri = lax.broadcasted_iota(jnp.int32, (ATT_W, ATT_W), 0) // 64
        ci = lax.broadcasted_iota(jnp.int32, (ATT_W, ATT_W), 1) // 64
        ones = jnp.where(ri == ci, 1.0, 0.0).astype(BF16)
        prod = d_att * att
        hi = prod.astype(BF16)
        lo = (prod - hi.astype(F32)).astype(BF16)
        delta = _dot(hi, ones) + _dot(lo, ones)
        for val, refs, dt in ((d_att, (da0, da1, da2), BF16), (lse_all, (ls0, ls1, ls2), F32),
                              (delta, (de0, de1, de2), F32)):
            vals = [val[:, s * LANES:(s + 1) * LANES] for s in range(4)]
            for dil, ref in zip(DILATIONS, refs):
                _deinterleave_store(vals, slab, ref, dil, ts, dt)

    def grp(dil):
        return pl.BlockSpec((dil, ts // dil, ATT_W), lambda i: (0, i, 0))

    tile = pl.BlockSpec((ts, dm), lambda i: (i, 0))
    gate_tile = pl.BlockSpec((ts, gw), lambda i: (i, 0))
    const = lambda shp: pl.BlockSpec(shp, lambda i: tuple(0 for _ in shp))
    grp_shape = lambda dt: [jax.ShapeDtypeStruct((dil, s_len // dil, ATT_W), dt) for dil in DILATIONS]
    return pl.pallas_call(
        body, name="merge_and_head", grid=(nt,),
        in_specs=[grp(d) for d in DILATIONS] * 2
        + [gate_tile, tile, tile, tile, const((8, dm)), const((1, dm)), const((ATT_W, dm)), const((dm, dm))],
        out_specs=[const((8, LANES)), tile, tile, pl.BlockSpec((ts, gw), lambda i: (i, gate_off // gw))]
        + [grp(d) for d in DILATIONS] * 3
        + [const((dm, dm)), const((ATT_W, dm)), const((8, dm))],
        out_shape=[jax.ShapeDtypeStruct((8, LANES), F32), jax.ShapeDtypeStruct((s_len, dm), F32),
                   jax.ShapeDtypeStruct((s_len, dm), BF16), jax.ShapeDtypeStruct((s_len, gate_off + gw), BF16)]
        + grp_shape(BF16) + grp_shape(F32) + grp_shape(F32)
        + [jax.ShapeDtypeStruct((dm, dm), F32), jax.ShapeDtypeStruct((ATT_W, dm), F32),
           jax.ShapeDtypeStruct((8, dm), F32)],
        scratch_shapes=[pltpu.VMEM((4, ts, LANES), F32)],
        compiler_params=_params(("arbitrary",)),
    )(*o_g, *lse_g, p_gate, y_conv, x, tgt, modv, final_g, w_ao, w_o)


def _qkv_grad_to_tokens(dp, dqkv_g, pos, ropetab, ts):
    s_len, dm = pos.shape[0], (dp.shape[1] - 10 * ATT_W) // 5
    qw = 3 * ATT_W
    assert (3 * dm) % qw == 0

    def body(dp_in, g0, g1, g2, pos_ref, tab_ref, o_ref, slab, rc, ra, rb):
        del dp_in
        w = pl.program_id(1)

        @pl.when(w == 0)
        def _():
            c_, a_, b_ = _rope_tables(pos_ref, tab_ref)
            rc[...] = c_
            ra[...] = a_
            rb[...] = b_

        def emit(roped):
            for gi, (dil, g_ref) in enumerate(zip(DILATIONS, (g0, g1, g2))):
                vals = _interleave_load(g_ref, slab, dil, ts)
                for s in range(4):
                    t = vals[s]
                    if roped:
                        t = t * rc[...] + pltpu.roll(t * ra[...], 8, 1) + pltpu.roll(t * rb[...], LANES - 8, 1)
                    col = gi * ATT_W + s * LANES
                    o_ref[:, col:col + LANES] = t.astype(BF16)

        pl.when(w < 2)(lambda: emit(True))
        pl.when(w == 2)(lambda: emit(False))

    return pl.pallas_call(
        body, name="qkv_grad_to_tokens", grid=(s_len // ts, 3),
        in_specs=[ANY] + [pl.BlockSpec((None, dil, ts // dil, ATT_W), lambda i, w: (w, 0, i, 0)) for dil in DILATIONS]
        + [pl.BlockSpec((ts, 1), lambda i, w: (i, 0)), pl.BlockSpec((8, LANES), lambda i, w: (0, 0))],
        out_specs=pl.BlockSpec((ts, qw), lambda i, w: (i, 3 * dm // qw + w)),
        out_shape=jax.ShapeDtypeStruct(dp.shape, BF16),
        input_output_aliases={0: 0},
        scratch_shapes=[pltpu.VMEM((4, ts, LANES), F32)] + [pltpu.VMEM((ts, LANES), F32)] * 3,
        compiler_params=_params(("parallel", "arbitrary")),
    )(dp, *dqkv_g, pos, ropetab)


def _wide_col_tile(cols):
    for width in (5 * COL_TILE, 2 * COL_TILE):
        if cols % width == 0:
            return width
    return COL_TILE


def _input_grad(dp, w_in, x, dx_res, modv, norm_g, ts):
    s_len, dm = x.shape
    ct = 2 * COL_TILE if dp.shape[1] % (2 * COL_TILE) == 0 else COL_TILE
    nct = dp.shape[1] // ct

    def body(p_ref, w_ref, x_ref, dxr_ref, mod_ref, g_ref, gx_ref, gs_ref, acc):
        i, j = pl.program_id(0), pl.program_id(1)

        @pl.when((i == 0) & (j == 0))
        def _():
            gs_ref[...] = jnp.zeros_like(gs_ref)

        @pl.when(j == 0)
        def _():
            acc[...] = jnp.zeros_like(acc)

        acc[...] += _dot_nt(p_ref[...], w_ref[...])

        @pl.when(j == nct - 1)
        def _():
            d_h = acc[...]
            xv = x_ref[...]
            r = lax.rsqrt(jnp.mean(xv * xv, axis=-1, keepdims=True) + EPS)
            xr = xv * r
            gs_ref[0:1, :] += jnp.sum(d_h, axis=0, keepdims=True)
            gs_ref[1:2, :] += jnp.sum(d_h * (xr * g_ref[...]), axis=0, keepdims=True)
            d_n = d_h * (1.0 + mod_ref[1:2, :])
            gs_ref[2:3, :] += jnp.sum(d_n * xr, axis=0, keepdims=True)
            dxn = d_n * g_ref[...]
            gx_ref[...] = dxr_ref[...] + r * dxn - xr * (r * jnp.mean(dxn * xr, axis=-1, keepdims=True))

    tile = pl.BlockSpec((ts, dm), lambda i, j: (i, 0))
    return pl.pallas_call(
        body, name="input_grad", grid=(s_len // ts, nct),
        in_specs=[pl.BlockSpec((ts, ct), lambda i, j: (i, j)), pl.BlockSpec((dm, ct), lambda i, j: (0, j)), tile, tile,
                  pl.BlockSpec((8, dm), lambda i, j: (0, 0)), pl.BlockSpec((1, dm), lambda i, j: (0, 0))],
        out_specs=[tile, pl.BlockSpec((8, dm), lambda i, j: (0, 0))],
        out_shape=[jax.ShapeDtypeStruct((s_len, dm), F32), jax.ShapeDtypeStruct((8, dm), F32)],
        scratch_shapes=[pltpu.VMEM((ts, dm), F32)],
        compiler_params=_params(("arbitrary", "arbitrary")),
    )(dp, w_in, x, dx_res, modv, norm_g)


def _w_in_grad(h_t, dp, ts):
    dm, s_len = h_t.shape
    ct = _wide_col_tile(dp.shape[1])

    def body(h_ref, p_ref, o_ref):
        @pl.when(pl.program_id(1) == 0)
        def _():
            o_ref[...] = jnp.zeros_like(o_ref)

        o_ref[...] += _dot(h_ref[...], p_ref[...])

    return pl.pallas_call(
        body, name="w_in_grad", grid=(dp.shape[1] // ct, s_len // ts),
        in_specs=[pl.BlockSpec((dm, ts), lambda j, i: (0, i)), pl.BlockSpec((ts, ct), lambda j, i: (i, j))],
        out_specs=pl.BlockSpec((dm, ct), lambda j, i: (0, j)),
        out_shape=jax.ShapeDtypeStruct((dm, dp.shape[1]), F32),
        compiler_params=_params(("arbitrary", "arbitrary")),
    )(h_t, dp)


def _rope_lane_table():
    l64 = jnp.arange(LANES) % 64
    half = ROT_DIM // 2
    inv_freq = ROPE_THETA ** (-(jnp.arange(half, dtype=F32) * 2.0 / ROT_DIM))
    rot = l64 < ROT_DIM
    rows = [jnp.where(rot, inv_freq[l64 % half], 0.0), (l64 < half).astype(F32),
            ((l64 >= half) & rot).astype(F32), rot.astype(F32)]
    return jnp.concatenate([jnp.stack(rows), jnp.zeros((4, LANES), F32)], axis=0)


def _tile_sizes(s_len):
    l_min = s_len // DILATIONS[-1]
    ts_big = min(1024, s_len // 2)
    ts_mid = 256
    ts_head = 256
    qt = min(512, l_min)
    return ts_big, ts_mid, ts_head, qt


def kernel(x, c, positions, norm_g, w_ada, b_ada, w_in, conv_w, conv_b, conv_ln_g, conv_ln_b, w_conv_out, w_att_out, w_o, final_g, loss_target, m_norm_g, m_w_ada, m_b_ada, m_w_in, m_conv_w, m_conv_b, m_conv_ln_g, m_conv_ln_b, m_w_conv_out, m_w_att_out, m_w_o, m_final_g, v_norm_g, v_w_ada, v_b_ada, v_w_in, v_conv_w, v_conv_b, v_conv_ln_g, v_conv_ln_b, v_w_conv_out, v_w_att_out, v_w_o, v_final_g):
    s_len, dm = x.shape[1], x.shape[2]
    ts_big, ts_mid, ts_head, qt = _tile_sizes(s_len)
    xi, yi, cidx = _my_place()
    chip = 2 * xi + yi
    batch = 4 * xi + 2 * yi + cidx
    x2d, tgt = x[0], loss_target[0]
    pos = positions.reshape(s_len, 1)
    wa_l, wi_l, cw_l = w_ada[0], w_in[0], conv_w[0]
    wco_l, wao_l, wo_l = w_conv_out[0], w_att_out[0], w_o[0]
    ada_w = wa_l.shape[1]
    cw_cols = cw_l.shape[1]

    kinds = ("col", "row", "col", "row")
    w_bufs = _cast_weights([wi_l, wco_l, wao_l, wo_l], kinds)

    cw_pad = jnp.pad(cw_l, ((0, HALO - CONV_K), (0, 0)))
    small_in = jnp.concatenate([jnp.broadcast_to(c, (8, dm)), cw_pad.reshape(8, dm)], axis=0)
    small = _allgather_small(small_in).reshape(8, 16, dm)
    c_all = small[:, 0, :]
    conv_w_full = jnp.concatenate(
        [small[2 * p, 8:16, :].reshape(HALO, cw_cols) for p in range(4)], axis=1)
    b_l = lax.dynamic_slice(b_ada, (0, chip * ada_w), (1, ada_w))
    mod_parts = _allgather_small(_mod_part(c_all, wa_l, b_l)).reshape(8, 8, ada_w)
    mod_rows = lax.dynamic_index_in_dim(mod_parts, batch, axis=1, keepdims=False)
    mod = jnp.concatenate([mod_rows[2 * p] for p in range(4)], axis=0).reshape(3, dm)
    modv = jnp.concatenate([mod, jnp.zeros((5, dm), F32)], axis=0)

    sems_a, sems_b, w_bufs, token = _gather_weights_start(w_bufs, kinds, modv)
    (w_in_b,) = _gather_weights_wait(w_bufs, kinds, sems_a, 0, token, "gather_weights_wait_w_in")
    w_in_b = _pass_to_sibling(w_in_b, kinds[0])

    ropetab = _rope_lane_table()
    p_conv, p_gate, h_b, h_t = _input_projection(x2d, modv, norm_g, w_in_b, ts_big)
    qkv_g = [_qkv_projection(h_b, pos, ropetab, w_in_b, gi, ts_big) for gi in range(3)]
    w_co_b, w_ao_b, w_o_b = _gather_weights_wait(w_bufs, kinds, sems_b, 1, qkv_g[2], "gather_weights_wait_rest")
    y_conv, u0, u1 = _conv_forward(p_conv, conv_w_full, conv_b, conv_ln_g, conv_ln_b, w_co_b, ts_mid)
    qkv_flat = [q.reshape(3, s_len, ATT_W) for q in qkv_g]
    o_g, lse_g = [], []
    for gi, dil in enumerate(DILATIONS):
        o, l = _attention_forward(qkv_flat[gi], s_len // dil, qt, "attention_forward_%d" % dil)
        o_g.append(o.reshape(dil, s_len // dil, ATT_W))
        lse_g.append(l.reshape(dil, s_len // dil, ATT_W))

    (loss_p, dx_res, dyc, dp, da0, da1, da2, ls0, ls1, ls2, de0, de1, de2,
     g_wo, g_wao, head_sums) = _merge_and_head(o_g, lse_g, p_gate, y_conv, x2d, tgt, modv,
                                               final_g.reshape(1, dm), w_ao_b, w_o_b, ts_head)
    loss = lax.psum(loss_p[0, 0], ("x", "y", "c"))

    dp, g_wco, conv_sums, g_cw = _conv_backward(dp, dyc, p_conv, u0, u1, conv_w_full, conv_ln_g, conv_ln_b,
                                                w_co_b, ts_mid)
    dqkv_g = []
    for gi, (dil, da, ls, de) in enumerate(zip(DILATIONS, (da0, da1, da2), (ls0, ls1, ls2), (de0, de1, de2))):
        flat = lambda a: a.reshape(s_len, ATT_W)
        dqkv = _attention_backward(qkv_flat[gi], flat(da), flat(ls), flat(de), s_len // dil, qt,
                                   "attention_backward_%d" % dil)
        dqkv_g.append(dqkv.reshape(3, dil, s_len // dil, ATT_W))
    dp = _qkv_grad_to_tokens(dp, dqkv_g, pos, ropetab, 2 * ts_mid)
    g_win = _w_in_grad(h_t, dp, ts_big)

    grads = [g_win, g_wco, g_wao, g_wo]
    full_shapes = [g.shape for g in grads]
    c_arr = jnp.reshape(cidx, (1,)).astype(jnp.int32)
    recv_halves = _reduce_pair_exchange(grads, kinds)
    halves = [_reduce_pair_sum(g, pa, kind, c_arr, "reduce_pair_sum_%d" % k)
              for k, (g, pa, kind) in enumerate(zip(grads, recv_halves, kinds))]
    send_sems, recv_sems, halves, lands, token = _reduce_to_owner_start(halves, kinds, full_shapes)
    grad_x, in_sums = _input_grad(dp, w_in_b, x2d, dx_res, modv + token[0, 0], norm_g, ts_big)
    halves, recvd = _reduce_to_owner_wait(send_sems, recv_sems, halves, lands, in_sums, kinds)
    gr_win, gr_wco, gr_wao, gr_wo = _reduce_finish(halves, recvd, kinds, full_shapes)

    rows = [in_sums[2:3], conv_sums[2:3], conv_sums[0:1], conv_sums[1:2], head_sums[0:1],
            in_sums[0:1], in_sums[1:2], head_sums[1:2], g_cw, jnp.zeros((8, dm), F32)]
    part = jnp.concatenate(rows, axis=0)
    gathered = _allgather_small(part)
    tot = _sum_devices(gathered, 48)
    dmod_all = gathered.reshape(8, 48, dm)[:, 5:8, :].reshape(8, 3 * dm)
    dmod_l = lax.dynamic_slice(dmod_all, (0, chip * ada_w), (8, ada_w))
    gr_wada = _grad_w_ada(c_all.T, dmod_l)
    gr_cw = lax.dynamic_slice(tot[8:8 + HALO], (0, chip * cw_cols), (HALO, cw_cols))

    def pack8(ng, cb, lg, lb, fg, ba):
        return jnp.concatenate([ng, cb, lg, lb, fg.reshape(1, dm), ba.reshape(3, dm)], axis=0)

    pad_cw = lambda a: jnp.pad(a[0], ((0, HALO - CONV_K), (0, 0)))
    w8 = pack8(norm_g, conv_b, conv_ln_g, conv_ln_b, final_g, b_ada)
    m8 = pack8(m_norm_g, m_conv_b, m_conv_ln_g, m_conv_ln_b, m_final_g, m_b_ada)
    v8 = pack8(v_norm_g, v_conv_b, v_conv_ln_g, v_conv_ln_b, v_final_g, v_b_ada)
    g8 = tot[0:8]
    upd = {
        "small": _adamw(w8, g8, m8, v8, "adamw_small"),
        "w_ada": _adamw(wa_l, gr_wada, m_w_ada[0], v_w_ada[0], "adamw_w_ada"),
        "w_in": _adamw(wi_l, gr_win, m_w_in[0], v_w_in[0], "adamw_w_in"),
        "conv_w": _adamw(cw_pad, gr_cw, pad_cw(m_conv_w), pad_cw(v_conv_w), "adamw_conv_w"),
        "w_co": _adamw(wco_l, gr_wco, m_w_conv_out[0], v_w_conv_out[0], "adamw_w_conv_out"),
        "w_ao": _adamw(wao_l, gr_wao, m_w_att_out[0], v_w_att_out[0], "adamw_w_att_out"),
        "w_o": _adamw(wo_l, gr_wo, m_w_o[0], v_w_o[0], "adamw_w_o"),
    }

    def family(which):
        if which is None:
            sm = g8
            big = {"w_ada": gr_wada, "w_in": gr_win, "conv_w": gr_cw, "w_co": gr_wco, "w_ao": gr_wao, "w_o": gr_wo}
        else:
            sm = upd["small"][which]
            big = {k: upd[k][which] for k in ("w_ada", "w_in", "conv_w", "w_co", "w_ao", "w_o")}
        return [sm[0:1], big["w_ada"][None], sm[5:8].reshape(1, 3 * dm), big["w_in"][None],
                big["conv_w"][None, :CONV_K], sm[1:2], sm[2:3], sm[3:4], big["w_co"][None],
                big["w_ao"][None], big["w_o"][None], sm[4]]

    return (loss, grad_x[None], *family(None), *family(0), *family(1), *family(2))
```

```python
import jax
import jax.numpy as jnp
from jax import lax
from jax.experimental import pallas as pl
from jax.experimental.pallas import tpu as pltpu

F32 = jnp.float32
BF16 = jnp.bfloat16
MESH = pl.DeviceIdType.MESH
ANY = pl.BlockSpec(memory_space=pl.ANY)
VM = pl.BlockSpec(memory_space=pltpu.VMEM)

EPS = 1e-6
NEG = -1e30
ATT_W = 512
DILATIONS = (1, 4, 16)
BLK = 128
CONV_K = 31
HALO = 32
CONV_ROWS_FWD = 32
CONV_ROWS_BWD = 16
ROT_DIM = 16
ROPE_THETA = 500000.0
COL_TILE = 512
LANES = 128
VMEM_LIMIT = 56 * 1024 * 1024

ADAM_LR, ADAM_B1, ADAM_B2, ADAM_EPS, ADAM_WD, ADAM_STEP = 0.001, 0.9, 0.999, 1e-08, 0.01, 10


def _params(sem=None, vmem=VMEM_LIMIT):
    return pltpu.CompilerParams(dimension_semantics=sem, vmem_limit_bytes=vmem)


def _dot(a, b):
    return jnp.dot(a, b, preferred_element_type=F32)


def _dot_nt(a, b):
    return lax.dot_general(a, b, (((1,), (1,)), ((), ())), preferred_element_type=F32)


def _dot_tn(a, b):
    return lax.dot_general(a, b, (((0,), (0,)), ((), ())), preferred_element_type=F32)


def _sig(x):
    return jax.nn.sigmoid(x)


def _dsilu(x, s):
    return s * (1.0 + x * (1.0 - s))


def _my_place():
    return lax.axis_index("x"), lax.axis_index("y"), lax.axis_index("c")


def _allgather_small(x_shard):
    m_per, n = x_shard.shape

    def body(x_ref, out_ref, send_sems, recv_sems, local_sem):
        x, y, c = _my_place()
        me, sibling = (x, y, c), (x, y, 1 - c)
        chips = [(1 - x, y), (x, 1 - y), (1 - x, 1 - y)]

        def rows(px, py, pc):
            return out_ref.at[pl.ds((4 * px + 2 * py + pc) * m_per, m_per), :]

        def copy(k, block, to, src=None):
            return pltpu.make_async_remote_copy(
                src_ref=rows(*block) if src is None else src, dst_ref=rows(*block),
                send_sem=send_sems.at[k], recv_sem=recv_sems.at[k],
                device_id=to, device_id_type=MESH)

        mine = pltpu.make_async_copy(x_ref, rows(*me), local_sem)
        mine.start()
        first = [copy(0, me, sibling, src=x_ref)]
        first += [copy(1 + j, me, (*chip, c), src=x_ref) for j, chip in enumerate(chips)]
        for cp in first:
            cp.start()
        passed = [copy(4 + j, (*chip, c), sibling) for j, chip in enumerate(chips)]
        for j, chip in enumerate(chips):
            copy(1 + j, (*chip, c), me).wait_recv()
            passed[j].start()
        copy(0, sibling, me).wait_recv()
        for j, chip in enumerate(chips):
            copy(4 + j, (*chip, 1 - c), me).wait_recv()
        for cp in first + passed:
            cp.wait_send()
        mine.wait()

    return pl.pallas_call(
        body, name="allgather_small",
        out_shape=jax.ShapeDtypeStruct((8 * m_per, n), x_shard.dtype),
        in_specs=[VM], out_specs=VM,
        scratch_shapes=[pltpu.SemaphoreType.DMA((7,)), pltpu.SemaphoreType.DMA((7,)),
                        pltpu.SemaphoreType.DMA],
    )(x_shard)


def _shard_window(ref, kind, p, n_shards=4):
    r, c = ref.shape
    if kind == "col":
        w = c // n_shards
        return ref.at[:, pl.ds(p * w, w)]
    w = r // n_shards
    return ref.at[pl.ds(p * w, w), :]


def _half_window(ref, kind, hc):
    r, c = ref.shape
    if kind == "col":
        return ref.at[pl.ds(hc * (r // 2), r // 2), :]
    return ref.at[:, pl.ds(hc * (c // 2), c // 2)]


def _landed(ref, kind, chip, hc):
    return _half_window(_shard_window(ref, kind, 2 * chip[0] + chip[1]), kind, hc)


def _cast_weights(shards, kinds):
    n = len(shards)
    full_shapes = [(s.shape[0], 4 * s.shape[1]) if kind == "col" else (4 * s.shape[0], s.shape[1])
                   for s, kind in zip(shards, kinds)]

    def body(*refs):
        w_refs, out_refs, bf_refs, sems = refs[:n], refs[n:2 * n], refs[2 * n:3 * n], refs[3 * n]
        x, y, _ = _my_place()
        cps = []
        for k in range(n):
            bf_refs[k][...] = w_refs[k][...].astype(BF16)
            cp = pltpu.make_async_copy(bf_refs[k], _shard_window(out_refs[k], kinds[k], 2 * x + y), sems.at[k])
            cp.start()
            cps.append(cp)
        for cp in cps:
            cp.wait()

    return pl.pallas_call(
        body, name="cast_weights",
        out_shape=[jax.ShapeDtypeStruct(s, BF16) for s in full_shapes],
        in_specs=[VM] * n, out_specs=[ANY] * n,
        scratch_shapes=[pltpu.VMEM(s.shape, BF16) for s in shards] + [pltpu.SemaphoreType.DMA((n,))],
        compiler_params=_params(),
    )(*shards)


def _gather_copies(refs, kinds, sems_a, sems_b):
    x, y, c = _my_place()
    chips = [(1 - x, y), (x, 1 - y), (1 - x, 1 - y)]
    set_a, set_b = [], []
    for j, chip in enumerate(chips):
        for k in range(len(refs)):
            if refs[k] is None:
                continue
            for flip in ((0,) if k == 0 else (0, 1)):
                win = _landed(refs[k], kinds[k], (x, y), c)
                sems, idx = (sems_a, j) if k == 0 else (sems_b, ((k - 1) * 3 + j) * 2 + flip)
                mine = _landed(refs[k], kinds[k], chip, (c + flip) % 2)
                (set_a if k == 0 else set_b).append((
                    pltpu.make_async_remote_copy(
                        src_ref=win, dst_ref=win, send_sem=sems[0].at[idx], recv_sem=sems[1].at[idx],
                        device_id=(*chip, (c + flip) % 2), device_id_type=MESH),
                    pltpu.make_async_remote_copy(
                        src_ref=mine, dst_ref=mine, send_sem=sems[0].at[idx], recv_sem=sems[1].at[idx],
                        device_id=(*chip, (c + flip) % 2), device_id_type=MESH)))
    return set_a, set_b


_SPLIT = dict(has_side_effects=pltpu.SideEffectType.DATAFLOW_SIDE_EFFECTING)


def _gather_weights_start(fulls, kinds, after):
    n = len(fulls)
    hbm = pl.BlockSpec(memory_space=pltpu.HBM)
    sem = pl.BlockSpec(memory_space=pltpu.SEMAPHORE)
    nb = (n - 1) * 6

    def body(*refs):
        in_refs = refs[:n]
        sa, ra, sb, rb = refs[n + 1:n + 5]
        token = refs[-1]
        set_a, set_b = _gather_copies(in_refs, kinds, (sa, ra), (sb, rb))
        for out_cp, _ in set_a + set_b:
            out_cp.start()
        token[...] = jnp.zeros_like(token)

    out = pl.pallas_call(
        body, name="gather_weights_start",
        out_shape=[pltpu.SemaphoreType.DMA((3,)), pltpu.SemaphoreType.DMA((3,)),
                   pltpu.SemaphoreType.DMA((nb,)), pltpu.SemaphoreType.DMA((nb,))]
        + [pltpu.HBM(f.shape, f.dtype) for f in fulls] + [jax.ShapeDtypeStruct((8, LANES), F32)],
        in_specs=[hbm] * n + [ANY], out_specs=[sem] * 4 + [hbm] * n + [VM],
        input_output_aliases={k: 4 + k for k in range(n)},
        compiler_params=pltpu.CompilerParams(**_SPLIT),
    )(*[pltpu.with_memory_space_constraint(f, pltpu.HBM) for f in fulls], after)
    return (out[0], out[1]), (out[2], out[3]), out[4:4 + n], out[-1]


def _gather_weights_wait(fulls, kinds, sems, which, after, name):
    n = len(fulls)
    hbm = pl.BlockSpec(memory_space=pltpu.HBM)
    sem = pl.BlockSpec(memory_space=pltpu.SEMAPHORE)
    keep = [0] if which == 0 else list(range(1, n))

    def body(*refs):
        m = len(keep)
        full_refs = [None] * n
        for pos_, k in enumerate(keep):
            full_refs[k] = refs[pos_]
        s_ref, r_ref = refs[m:m + 2]
        if which == 0:
            sets = _gather_copies([full_refs[0]], kinds[:1], (s_ref, r_ref), None)[0]
        else:
            sets = _gather_copies([None] + [full_refs[k] for k in keep], kinds, None, (s_ref, r_ref))[1]
        for out_cp, in_cp in sets:
            out_cp.wait_send()
            in_cp.wait_recv()

    out = pl.pallas_call(
        body, name=name,
        out_shape=[pltpu.HBM(fulls[k].shape, fulls[k].dtype) for k in keep],
        in_specs=[hbm] * len(keep) + [sem, sem, ANY], out_specs=[hbm] * len(keep),
        input_output_aliases={i: i for i in range(len(keep))},
        compiler_params=pltpu.CompilerParams(**_SPLIT),
    )(*[fulls[k] for k in keep], sems[0], sems[1], after)
    return list(out)


def _pass_to_sibling(w_full, kind):
    def body(w_in_ref, w_ref, send_sems, recv_sems):
        del w_in_ref
        x, y, c = _my_place()
        chips = [(1 - x, y), (x, 1 - y), (1 - x, 1 - y)]
        cps = []
        for j, chip in enumerate(chips):
            win = _landed(w_ref, kind, chip, c)
            cp = pltpu.make_async_remote_copy(
                src_ref=win, dst_ref=win, send_sem=send_sems.at[j], recv_sem=recv_sems.at[j],
                device_id=(x, y, 1 - c), device_id_type=MESH)
            cp.start()
            cps.append(cp)
        for j, chip in enumerate(chips):
            theirs = _landed(w_ref, kind, chip, 1 - c)
            pltpu.make_async_remote_copy(
                src_ref=theirs, dst_ref=theirs, send_sem=send_sems.at[j], recv_sem=recv_sems.at[j],
                device_id=(x, y, 1 - c), device_id_type=MESH).wait_recv()
        for cp in cps:
            cp.wait_send()

    return pl.pallas_call(
        body, name="pass_to_sibling",
        out_shape=jax.ShapeDtypeStruct(w_full.shape, w_full.dtype),
        in_specs=[ANY], out_specs=ANY, input_output_aliases={0: 0},
        scratch_shapes=[pltpu.SemaphoreType.DMA((3,)), pltpu.SemaphoreType.DMA((3,))],
    )(w_full)


def _reduce_pair_exchange(grads, kinds):
    n = len(grads)
    half_shapes = [(g.shape[0] // 2, g.shape[1]) if kind == "col" else (g.shape[0], g.shape[1] // 2)
                   for g, kind in zip(grads, kinds)]

    def body(*refs):
        g_refs, pa_refs = refs[:n], refs[n:2 * n]
        send_sems, recv_sems = refs[2 * n:]
        x, y, c = _my_place()
        cps = []
        for k in range(n):
            cp = pltpu.make_async_remote_copy(
                src_ref=_half_window(g_refs[k], kinds[k], 1 - c), dst_ref=pa_refs[k],
                send_sem=send_sems.at[k], recv_sem=recv_sems.at[k],
                device_id=(x, y, 1 - c), device_id_type=MESH)
            cp.start()
            cps.append(cp)
        for cp in cps:
            cp.wait()

    return pl.pallas_call(
        body, name="reduce_pair_exchange",
        out_shape=[jax.ShapeDtypeStruct(s, F32) for s in half_shapes],
        in_specs=[ANY] * n, out_specs=[ANY] * n,
        scratch_shapes=[pltpu.SemaphoreType.DMA((n,)), pltpu.SemaphoreType.DMA((n,))],
    )(*grads)


def _row_tile(rows, cols, itemsize=4, target=2 * 1024 * 1024, mult=16):
    t = rows
    while t % 2 == 0 and t // 2 >= mult and (t // 2) % mult == 0 and t * cols * itemsize > target:
        t //= 2
    return t


def _reduce_pair_sum(g, pa, kind, c_arr, name):
    hr, hc_ = pa.shape
    tr = _row_tile(hr, hc_)
    nb = hr // tr

    def body(c_ref, g_ref, pa_ref, o_ref):
        o_ref[...] = (g_ref[...] + pa_ref[...]).astype(BF16)

    if kind == "col":
        g_map = lambda i, c_ref: (c_ref[0] * nb + i, 0)
    else:
        g_map = lambda i, c_ref: (i, c_ref[0])
    return pl.pallas_call(
        body, name=name,
        grid_spec=pltpu.PrefetchScalarGridSpec(
            num_scalar_prefetch=1, grid=(nb,),
            in_specs=[pl.BlockSpec((tr, hc_), g_map), pl.BlockSpec((tr, hc_), lambda i, c_ref: (i, 0))],
            out_specs=pl.BlockSpec((tr, hc_), lambda i, c_ref: (i, 0))),
        out_shape=jax.ShapeDtypeStruct((hr, hc_), BF16),
        compiler_params=_params(("parallel",)),
    )(c_arr, g, pa)


def _half_shard_shape(full_shape, kind):
    r, c = full_shape
    return (r // 2, c // 4) if kind == "col" else (r // 4, c // 2)


def _to_owner_copies(h_refs, land_refs, send_sems, recv_sems, kinds):
    n = len(h_refs)
    x, y, c = _my_place()
    chips = [(1 - x, y), (x, 1 - y), (1 - x, 1 - y)]
    cps = []
    for j, chip in enumerate(chips):
        pj = 2 * chip[0] + chip[1]
        for k in range(n):
            cps.append(pltpu.make_async_remote_copy(
                src_ref=_shard_window(h_refs[k], kinds[k], pj), dst_ref=land_refs[k].at[j],
                send_sem=send_sems.at[j * n + k], recv_sem=recv_sems.at[j * n + k],
                device_id=(*chip, c), device_id_type=MESH))
    return cps


def _reduce_to_owner_start(halves, kinds, full_shapes):
    n = len(halves)
    hs = [_half_shard_shape(fs, kind) for fs, kind in zip(full_shapes, kinds)]
    hbm = pl.BlockSpec(memory_space=pltpu.HBM)
    sem = pl.BlockSpec(memory_space=pltpu.SEMAPHORE)

    def body(*refs):
        h_refs, land_refs = refs[:n], refs[n:2 * n]
        send_sems, recv_sems = refs[2 * n:2 * n + 2]
        token = refs[-1]
        for cp in _to_owner_copies(h_refs, land_refs, send_sems, recv_sems, kinds):
            cp.start()
        token[...] = jnp.zeros_like(token)

    lands = [pltpu.with_memory_space_constraint(lax.empty((3,) + s, BF16), pltpu.HBM) for s in hs]
    out = pl.pallas_call(
        body, name="reduce_to_owner_start",
        out_shape=[pltpu.SemaphoreType.DMA((3 * n,)), pltpu.SemaphoreType.DMA((3 * n,))]
        + [pltpu.HBM(h.shape, h.dtype) for h in halves] + [pltpu.HBM((3,) + s, BF16) for s in hs]
        + [jax.ShapeDtypeStruct((8, LANES), F32)],
        in_specs=[hbm] * (2 * n), out_specs=[sem, sem] + [hbm] * (2 * n) + [VM],
        input_output_aliases={k: 2 + k for k in range(2 * n)},
        compiler_params=pltpu.CompilerParams(has_side_effects=pltpu.SideEffectType.DATAFLOW_SIDE_EFFECTING),
    )(*[pltpu.with_memory_space_constraint(h, pltpu.HBM) for h in halves], *lands)
    return out[0], out[1], out[2:2 + n], out[2 + n:2 + 2 * n], out[-1]


def _reduce_to_owner_wait(send_sems, recv_sems, halves, lands, after, kinds):
    n = len(halves)
    hbm = pl.BlockSpec(memory_space=pltpu.HBM)
    sem = pl.BlockSpec(memory_space=pltpu.SEMAPHORE)

    def body(*refs):
        h_refs, land_refs = refs[:n], refs[n:2 * n]
        send_s, recv_s = refs[2 * n:2 * n + 2]
        for cp in _to_owner_copies(h_refs, land_refs, send_s, recv_s, kinds):
            cp.wait_send()
            cp.wait_recv()

    out = pl.pallas_call(
        body, name="reduce_to_owner_wait",
        out_shape=[pltpu.HBM(h.shape, h.dtype) for h in halves] + [pltpu.HBM(l.shape, l.dtype) for l in lands],
        in_specs=[hbm] * (2 * n) + [sem, sem, ANY], out_specs=[hbm] * (2 * n),
        input_output_aliases={k: k for k in range(2 * n)},
        compiler_params=pltpu.CompilerParams(has_side_effects=pltpu.SideEffectType.DATAFLOW_SIDE_EFFECTING),
    )(*halves, *lands, send_sems, recv_sems, after)
    return out[:n], out[n:]


def _reduce_finish(halves, recvd, kinds, full_shapes):
    n = len(halves)
    hs = [_half_shard_shape(fs, kind) for fs, kind in zip(full_shapes, kinds)]
    shard_shapes = [(fs[0], fs[1] // 4) if kind == "col" else (fs[0] // 4, fs[1])
                    for fs, kind in zip(full_shapes, kinds)]

    def body(*refs):
        h_refs, rc_refs, gs_refs = refs[:n], refs[n:2 * n], refs[2 * n:3 * n]
        own_refs, gh_refs = refs[3 * n:4 * n], refs[4 * n:5 * n]
        in_sems, loc_sems, send_sems, recv_sems = refs[5 * n:]
        x, y, c = _my_place()
        p = 2 * x + y
        loads = []
        for k in range(n):
            cp = pltpu.make_async_copy(_shard_window(h_refs[k], kinds[k], p), own_refs[k], in_sems.at[k])
            cp.start()
            loads.append(cp)
        outs = []
        for k in range(n):
            loads[k].wait()
            gh_refs[k][...] = (own_refs[k][...].astype(F32) + rc_refs[k][0].astype(F32)
                               + rc_refs[k][1].astype(F32) + rc_refs[k][2].astype(F32))
            dst = _half_window(gs_refs[k], kinds[k], c)
            lc = pltpu.make_async_copy(gh_refs[k], dst, loc_sems.at[k])
            lc.start()
            rc = pltpu.make_async_remote_copy(
                src_ref=gh_refs[k], dst_ref=dst, send_sem=send_sems.at[k], recv_sem=recv_sems.at[k],
                device_id=(x, y, 1 - c), device_id_type=MESH)
            rc.start()
            outs.append((lc, rc))
        for k, (lc, rc) in enumerate(outs):
            lc.wait()
            rc.wait_send()
            pltpu.make_async_remote_copy(
                src_ref=gh_refs[k], dst_ref=_half_window(gs_refs[k], kinds[k], 1 - c),
                send_sem=send_sems.at[k], recv_sem=recv_sems.at[k],
                device_id=(x, y, 1 - c), device_id_type=MESH).wait_recv()

    return pl.pallas_call(
        body, name="reduce_finish",
        out_shape=[jax.ShapeDtypeStruct(s, F32) for s in shard_shapes],
        in_specs=[ANY] * n + [VM] * n, out_specs=[ANY] * n,
        scratch_shapes=[pltpu.VMEM(s, BF16) for s in hs] + [pltpu.VMEM(s, F32) for s in hs]
        + [pltpu.SemaphoreType.DMA((n,)) for _ in range(4)],
        compiler_params=_params(),
    )(*halves, *recvd)


def _mod_part(c_all, w_ada_l, b_l):
    def body(c_ref, w_ref, b_ref, o_ref):
        o_ref[...] = _dot(c_ref[...].astype(BF16), w_ref[...].astype(BF16)) + b_ref[...]

    return pl.pallas_call(
        body, name="mod_part", out_shape=jax.ShapeDtypeStruct((8, w_ada_l.shape[1]), F32),
        in_specs=[VM, VM, VM], out_specs=VM, compiler_params=_params(),
    )(c_all, w_ada_l, b_l)


def _sum_devices(parts, m):
    def body(p_ref, o_ref):
        acc = p_ref[0:m, :]
        for d in range(1, 8):
            acc = acc + p_ref[d * m:(d + 1) * m, :]
        o_ref[...] = acc

    return pl.pallas_call(
        body, name="sum_devices", out_shape=jax.ShapeDtypeStruct((m, parts.shape[1]), F32),
        in_specs=[VM], out_specs=VM, compiler_params=_params(),
    )(parts)


def _grad_w_ada(c_all_t, dmod_l):
    d, w = c_all_t.shape[0], dmod_l.shape[1]

    def body(ct_ref, dm_ref, o_ref):
        acc = ct_ref[:, 0:1] * dm_ref[0:1, :]
        for b in range(1, 8):
            acc = acc + ct_ref[:, b:b + 1] * dm_ref[b:b + 1, :]
        o_ref[...] = acc

    return pl.pallas_call(
        body, name="grad_w_ada", out_shape=jax.ShapeDtypeStruct((d, w), F32),
        in_specs=[VM, VM], out_specs=VM, compiler_params=_params(),
    )(c_all_t, dmod_l)


def _adamw_math(w, g, m, v):
    nm = ADAM_B1 * m + (1.0 - ADAM_B1) * g
    nv = ADAM_B2 * v + (1.0 - ADAM_B2) * (g * g)
    m_hat = nm / (1.0 - ADAM_B1 ** ADAM_STEP)
    v_hat = nv / (1.0 - ADAM_B2 ** ADAM_STEP)
    return -ADAM_LR * (m_hat / (jnp.sqrt(v_hat) + ADAM_EPS) + ADAM_WD * w), nm, nv


def _adamw_small(ws, g8, ms, vs):
    shapes = [jax.ShapeDtypeStruct(w.shape, F32) for w in ws]

    def body(*refs):
        w_refs, g_ref, m_refs, v_refs, outs = refs[0:6], refs[6], refs[7:13], refs[13:19], refs[19:]
        for idx in range(6):
            if idx < 5:
                g = g_ref[idx:idx + 1, :]
            else:
                g = jnp.concatenate([g_ref[5:6, :], g_ref[6:7, :], g_ref[7:8, :]], axis=1)
            res = (g,) + _adamw_math(w_refs[idx][...], g, m_refs[idx][...], v_refs[idx][...])
            for fam in range(4):
                outs[6 * fam + idx][...] = res[fam]

    return pl.pallas_call(
        body, name="adamw_small", out_shape=shapes * 4,
        in_specs=[VM] * 19, out_specs=[VM] * 24, compiler_params=_params(),
    )(*ws, g8, *ms, *vs)


def _adamw(w, g, m, v, name):
    r, c = w.shape
    tr = _row_tile(r, c, target=1024 * 1024, mult=8)

    def body(w_ref, g_ref, m_ref, v_ref, d_ref, nm_ref, nv_ref):
        d_ref[...], nm_ref[...], nv_ref[...] = _adamw_math(w_ref[...], g_ref[...], m_ref[...], v_ref[...])

    spec = pl.BlockSpec((tr, c), lambda i: (i, 0))
    return pl.pallas_call(
        body, name=name, grid=(r // tr,),
        out_shape=[jax.ShapeDtypeStruct((r, c), F32)] * 3,
        in_specs=[spec] * 4, out_specs=[spec] * 3,
        compiler_params=_params(("parallel",)),
    )(w, g, m, v)


def _rope_coefficients(pos, ropetab, ts):
    s_len = pos.shape[0]

    def body(pos_ref, tab_ref, o_ref):
        ang = pos_ref[...].astype(F32) * tab_ref[0:1, :]
        cs, sn = jnp.cos(ang), jnp.sin(ang)
        o_ref[0] = jnp.where(tab_ref[3:4, :] > 0, cs, 1.0)
        o_ref[1] = -sn * tab_ref[1:2, :]
        o_ref[2] = sn * tab_ref[2:3, :]

    return pl.pallas_call(
        body, name="rope_coefficients", grid=(s_len // ts,),
        in_specs=[pl.BlockSpec((ts, 1), lambda i: (i, 0)), pl.BlockSpec((8, LANES), lambda i: (0, 0))],
        out_specs=pl.BlockSpec((3, ts, LANES), lambda i: (0, i, 0)),
        out_shape=jax.ShapeDtypeStruct((3, s_len, LANES), F32),
        compiler_params=_params(("parallel",)),
    )(pos, ropetab)


def _deinterleave_store(vals, slab, out_ref, d, ts, dtype):
    if d == 1:
        for s in range(4):
            out_ref[0, :, s * LANES:(s + 1) * LANES] = vals[s].astype(dtype)
        return
    for s in range(4):
        slab[s] = vals[s]
    for r in range(d):
        for s in range(4):
            out_ref[r, :, s * LANES:(s + 1) * LANES] = slab[s, pl.ds(r, ts // d, stride=d), :].astype(dtype)


def _interleave_load(blk_ref, slab, d, ts):
    if d == 1:
        return [blk_ref[0, :, s * LANES:(s + 1) * LANES] for s in range(4)]
    for r in range(d):
        for s in range(4):
            slab[s, pl.ds(r, ts // d, stride=d), :] = blk_ref[r, :, s * LANES:(s + 1) * LANES]
    return [slab[s] for s in range(4)]


def _input_projection(x, modv, norm_g, w_in, ts):
    s_len, d_model = x.shape
    ci = w_in.shape[1]
    ct = COL_TILE
    nc, ng = 3 * d_model // ct, (ATT_W + 2 * d_model) // ct
    assert ci == (nc + 9 + ng) * ct and s_len % ts == 0

    def body(x_ref, mod_ref, g_ref, w_ref, pc_ref, pg_ref, h_ref, ht_ref):
        j = pl.program_id(1)

        @pl.when(j == 0)
        def _():
            xv = x_ref[...]
            r = lax.rsqrt(jnp.mean(xv * xv, axis=-1, keepdims=True) + EPS)
            h = (xv * r) * g_ref[...] * (1.0 + mod_ref[1:2, :]) + mod_ref[0:1, :]
            h_ref[...] = h.astype(BF16)
            ht_ref[...] = h.T.astype(BF16)

        res = _dot(h_ref[...], w_ref[...])

        @pl.when(j < nc)
        def _():
            pc_ref[...] = res

        @pl.when(j >= nc)
        def _():
            pg_ref[...] = res

    return pl.pallas_call(
        body, name="input_projection", grid=(s_len // ts, nc + ng),
        in_specs=[pl.BlockSpec((ts, d_model), lambda i, j: (i, 0)),
                  pl.BlockSpec((8, d_model), lambda i, j: (0, 0)),
                  pl.BlockSpec((1, d_model), lambda i, j: (0, 0)),
                  pl.BlockSpec((d_model, ct), lambda i, j: (0, jnp.where(j < nc, j, j + 9)))],
        out_specs=[pl.BlockSpec((ts, ct), lambda i, j: (i, jnp.clip(j, 0, nc - 1))),
                   pl.BlockSpec((ts, ct), lambda i, j: (i, jnp.clip(j - nc, 0, ng - 1))),
                   pl.BlockSpec((ts, d_model), lambda i, j: (i, 0)),
                   pl.BlockSpec((d_model, ts), lambda i, j: (0, i))],
        out_shape=[jax.ShapeDtypeStruct((s_len, 3 * d_model), F32),
                   jax.ShapeDtypeStruct((s_len, ATT_W + 2 * d_model), F32),
                   jax.ShapeDtypeStruct((s_len, d_model), BF16),
                   jax.ShapeDtypeStruct((d_model, s_len), BF16)],
        compiler_params=_params(("arbitrary", "arbitrary")),
    )(x, modv, norm_g, w_in)


def _qkv_projection(h, rope, w_in, gi, ts):
    s_len, d_model = h.shape
    dil = DILATIONS[gi]
    nc = 3 * d_model // COL_TILE
    hr = ts // 2
    assert COL_TILE == ATT_W and hr % (16 * dil) == 0

    def body(h_ref, rope_ref, wq_ref, wk_ref, wv_ref, o_ref, slab):
        rc, ra, rb = rope_ref.at[0], rope_ref.at[1], rope_ref.at[2]
        w_refs = (wq_ref, wk_ref, wv_ref)
        units = [(w, half) for w in range(3) for half in range(2)]

        def matmul(w, half):
            return _dot(h_ref[half * hr:(half + 1) * hr, :], w_refs[w][...])

        def finish(w, half, res):
            rows = slice(half * hr, (half + 1) * hr)
            vals = []
            for s in range(4):
                t = res[:, s * LANES:(s + 1) * LANES]
                if w < 2:
                    t = (t * rc[rows, :] + pltpu.roll(t, LANES - 8, 1) * ra[rows, :]
                         + pltpu.roll(t, 8, 1) * rb[rows, :])
                vals.append(t)
            out = o_ref.at[w, :, half * (hr // dil):(half + 1) * (hr // dil), :]
            _deinterleave_store(vals, slab.at[half], out, dil, hr, BF16)

        res_next = matmul(*units[0])
        for ui, (w, half) in enumerate(units):
            res = res_next
            if ui + 1 < len(units):
                res_next = matmul(*units[ui + 1])
            finish(w, half, res)

    def w_spec(w):
        return pl.BlockSpec((d_model, COL_TILE), lambda i: (0, nc + 3 * w + gi))

    return pl.pallas_call(
        body, name="qkv_projection_%d" % dil, grid=(s_len // ts,),
        in_specs=[pl.BlockSpec((ts, d_model), lambda i: (i, 0)), pl.BlockSpec((3, ts, LANES), lambda i: (0, i, 0)),
                  w_spec(0), w_spec(1), w_spec(2)],
        out_specs=pl.BlockSpec((3, dil, ts // dil, ATT_W), lambda i: (0, 0, i, 0)),
        out_shape=jax.ShapeDtypeStruct((3, dil, s_len // dil, ATT_W), BF16),
        scratch_shapes=[pltpu.VMEM((2, 4, hr, LANES), F32)],
        compiler_params=_params(("parallel",)),
    )(h, rope, w_in, w_in, w_in)


def _layernorm_stats(u1):
    mu = jnp.mean(u1, axis=-1, keepdims=True)
    xc = u1 - mu
    rstd = lax.rsqrt(jnp.mean(xc * xc, axis=-1, keepdims=True) + EPS)
    return xc * rstd, rstd


def _shifted_copies(win, shf, ts):
    rows = ts + HALO - 8
    for b in range(1, 8):
        shf[b - 1, 0:rows, :] = win[pl.ds(b, rows), :]


def _tap(win, shf, off, r0, rows):
    a, b = divmod(off, 8)
    start = 8 * a + r0
    if b == 0:
        return win[start:start + rows, :]
    return shf[b - 1, start:start + rows, :]


def _conv_forward(p_conv, conv_w, conv_b, ln_g, ln_b, w_co, ts):
    s_len, d3 = p_conv.shape
    dm = d3 // 3

    def body(p_ref, cw_ref, cb_ref, g_ref, b_ref, w_ref, y_ref, u0_ref, u1_ref, win, shf):
        i = pl.program_id(0)

        @pl.when(i == 0)
        def _():
            win[0:HALO, :] = jnp.zeros((HALO, dm), F32)

        a, b, z = p_ref[:, 0:dm], p_ref[:, dm:2 * dm], p_ref[:, 2 * dm:3 * dm]
        u0 = a * _sig(b)
        win[HALO:HALO + ts, :] = u0
        u0_ref[...] = u0
        _shifted_copies(win, shf, ts)
        for r0 in range(0, ts, CONV_ROWS_FWD):
            acc = jnp.broadcast_to(cb_ref[...], (CONV_ROWS_FWD, dm))
            for k in range(CONV_K):
                acc = acc + cw_ref[k:k + 1, :] * _tap(win, shf, HALO - (CONV_K - 1) + k, r0, CONV_ROWS_FWD)
            u1_ref[r0:r0 + CONV_ROWS_FWD, :] = acc
        xh, _ = _layernorm_stats(u1_ref[...])
        u2 = xh * g_ref[...] + b_ref[...]
        a_conv = (u2 * _sig(u2)) * (z * _sig(z))
        y_ref[...] = _dot(a_conv.astype(BF16), w_ref[...])
        win[0:HALO, :] = win[ts:ts + HALO, :]

    row = pl.BlockSpec((1, dm), lambda i: (0, 0))
    tile = pl.BlockSpec((ts, dm), lambda i: (i, 0))
    return pl.pallas_call(
        body, name="conv_forward", grid=(s_len // ts,),
        in_specs=[pl.BlockSpec((ts, d3), lambda i: (i, 0)),
                  pl.BlockSpec((HALO, dm), lambda i: (0, 0)), row, row, row,
                  pl.BlockSpec((dm, dm), lambda i: (0, 0))],
        out_specs=[tile, tile, tile],
        out_shape=[jax.ShapeDtypeStruct((s_len, dm), F32)] * 3,
        scratch_shapes=[pltpu.VMEM((ts + HALO, dm), F32), pltpu.VMEM((7, ts + HALO - 8, dm), F32)],
        compiler_params=_params(("arbitrary",)),
    )(p_conv, conv_w, conv_b, ln_g, ln_b, w_co)


def _conv_backward(dp, dyc, p_conv, u0, u1, conv_w, ln_g, ln_b, w_co, ts):
    s_len, d3 = p_conv.shape
    dm = d3 // 3
    nt = s_len // ts
    hb = ts // HALO

    def body(dp_in, dy_ref, p_ref, u0_ref, uh_ref, u1_ref, cw_ref, g_ref, b_ref, w_ref,
             dp_ref, gw_ref, gs_ref, gcw_ref, dwin, uwin, shf):
        del dp_in
        i = pl.program_id(0)
        ti = nt - 1 - i

        @pl.when(i == 0)
        def _():
            gw_ref[...] = jnp.zeros_like(gw_ref)
            gs_ref[...] = jnp.zeros_like(gs_ref)
            gcw_ref[...] = jnp.zeros_like(gcw_ref)
            dwin[ts:ts + HALO, :] = jnp.zeros((HALO, dm), F32)

        dy = dy_ref[...]
        z = p_ref[:, 2 * dm:3 * dm]
        d_ac = _dot_nt(dy, w_ref[...])
        xh, rstd = _layernorm_stats(u1_ref[...])
        u2 = xh * g_ref[...] + b_ref[...]
        sg2, sgz = _sig(u2), _sig(z)
        u3, sz = u2 * sg2, z * sgz
        gw_ref[...] += _dot_tn((u3 * sz).astype(BF16), dy)
        d_z = d_ac * u3 * _dsilu(z, sgz)
        d_u2 = d_ac * sz * _dsilu(u2, sg2)
        gs_ref[0:1, :] += jnp.sum(d_u2 * xh, axis=0, keepdims=True)
        gs_ref[1:2, :] += jnp.sum(d_u2, axis=0, keepdims=True)
        dxh = d_u2 * g_ref[...]
        d_u1 = rstd * (dxh - jnp.mean(dxh, axis=-1, keepdims=True)
                       - xh * jnp.mean(dxh * xh, axis=-1, keepdims=True))
        gs_ref[2:3, :] += jnp.sum(d_u1, axis=0, keepdims=True)
        dwin[0:ts, :] = d_u1
        uwin[0:HALO, :] = jnp.where(ti == 0, 0.0, uh_ref[...])
        uwin[HALO:HALO + ts, :] = u0_ref[...]
        dp_ref[:, 2 * dm:3 * dm] = d_z.astype(BF16)
        _shifted_copies(uwin, shf, ts)
        for k in range(CONV_K):
            part = jnp.zeros((CONV_ROWS_BWD, dm), F32)
            for r0 in range(0, ts, CONV_ROWS_BWD):
                part = part + dwin[r0:r0 + CONV_ROWS_BWD, :] * _tap(uwin, shf, HALO - (CONV_K - 1) + k, r0,
                                                                    CONV_ROWS_BWD)
            gcw_ref[k:k + 1, :] += jnp.sum(part, axis=0, keepdims=True)
        _shifted_copies(dwin, shf, ts)
        for r0 in range(0, ts, CONV_ROWS_BWD):
            d_u0 = jnp.zeros((CONV_ROWS_BWD, dm), F32)
            for k in range(CONV_K):
                d_u0 = d_u0 + cw_ref[k:k + 1, :] * _tap(dwin, shf, CONV_K - 1 - k, r0, CONV_ROWS_BWD)
            rows = slice(r0, r0 + CONV_ROWS_BWD)
            sgb = _sig(p_ref[rows, dm:2 * dm])
            dp_ref[rows, 0:dm] = (d_u0 * sgb).astype(BF16)
            dp_ref[rows, dm:2 * dm] = (d_u0 * p_ref[rows, 0:dm] * sgb * (1.0 - sgb)).astype(BF16)
        dwin[ts:ts + HALO, :] = dwin[0:HALO, :]

    rev = lambda i: (nt - 1 - i, 0)
    row = pl.BlockSpec((1, dm), lambda i: (0, 0))
    tile = pl.BlockSpec((ts, dm), rev)
    return pl.pallas_call(
        body, name="conv_backward", grid=(nt,),
        in_specs=[ANY, tile, pl.BlockSpec((ts, d3), rev), tile,
                  pl.BlockSpec((HALO, dm), lambda i: (jnp.maximum((nt - 1 - i) * hb - 1, 0), 0)),
                  tile, pl.BlockSpec((HALO, dm), lambda i: (0, 0)), row, row,
                  pl.BlockSpec((dm, dm), lambda i: (0, 0))],
        out_specs=[pl.BlockSpec((ts, d3), rev), pl.BlockSpec((dm, dm), lambda i: (0, 0)),
                   pl.BlockSpec((8, dm), lambda i: (0, 0)), pl.BlockSpec((HALO, dm), lambda i: (0, 0))],
        out_shape=[jax.ShapeDtypeStruct(dp.shape, BF16), jax.ShapeDtypeStruct((dm, dm), F32),
                   jax.ShapeDtypeStruct((8, dm), F32), jax.ShapeDtypeStruct((HALO, dm), F32)],
        input_output_aliases={0: 0},
        scratch_shapes=[pltpu.VMEM((ts + HALO, dm), F32), pltpu.VMEM((ts + HALO, dm), F32),
                        pltpu.VMEM((7, ts + HALO - 8, dm), F32)],
        compiler_params=_params(("arbitrary",)),
    )(dp, dyc, p_conv, u0, u0, u1, conv_w, ln_g, ln_b, w_co)


def _att_masks():
    head0 = lax.broadcasted_iota(jnp.int32, (BLK, LANES), 1) < 64
    col = lax.broadcasted_iota(jnp.int32, (BLK, 4 * BLK), 1)
    row = lax.broadcasted_iota(jnp.int32, (BLK, 4 * BLK), 0)
    kj = col % BLK
    prev = jnp.where(col < 2 * BLK, 1, 0)
    band = jnp.where(col < 2 * BLK, kj - row, row - kj)
    return head0, band, prev


def _fill_block_diagonal(dst, slab, src_ref, halo_ref, head0, nq):
    sl = slice(slab * LANES, (slab + 1) * LANES)
    for b in range(nq + 1):
        blk = halo_ref[:, sl] if b == 0 else src_ref[(b - 1) * BLK:b * BLK, sl]
        base = (slab * (nq + 1) + b) * 2 * BLK
        zero = jnp.zeros_like(blk)
        dst[base:base + BLK, :] = jnp.where(head0, blk, zero)
        dst[base + BLK:base + 2 * BLK, :] = jnp.where(head0, zero, blk)


def _attention_forward(qkv, seq_len, qt, name):
    s_len = qkv.shape[1]
    nq = qt // BLK
    tiles_per_seq = seq_len // qt

    def body(q_ref, k_ref, v_ref, kh_ref, vh_ref, o_ref, lse_ref, kbd, vbd):
        first = jnp.where((pl.program_id(0) % tiles_per_seq) == 0, 4 * BLK, 0)
        head0, band, prev = _att_masks()
        band_first = band - prev * first
        for p in range(4):
            _fill_block_diagonal(kbd, p, k_ref, kh_ref, head0, nq)
            _fill_block_diagonal(vbd, p, v_ref, vh_ref, head0, nq)
        units = [(p, n) for p in range(4) for n in range(nq)]

        def keys_of(p, n):
            base = (p * (nq + 1) + n) * 2 * BLK
            return slice(base, base + 4 * BLK)

        def scores(p, n):
            q2 = q_ref[n * BLK:(n + 1) * BLK, p * LANES:(p + 1) * LANES] * 0.125
            return _dot_nt(q2, kbd[keys_of(p, n), :])

        def finish(p, n, o, den, lse):
            rows, sl = slice(n * BLK, (n + 1) * BLK), slice(p * LANES, (p + 1) * LANES)
            o_ref[rows, sl] = o / jnp.where(head0, den[0], den[1])
            lse_ref[rows, sl] = jnp.where(head0, lse[0], lse[1])

        s_next = scores(*units[0])
        pending = None
        for ui, (p, n) in enumerate(units):
            s = s_next
            if ui + 1 < len(units):
                s_next = scores(*units[ui + 1])
            s = jnp.where((band_first if n == 0 else band) >= 0, s, NEG)
            grp = [s[:, g * BLK:(g + 1) * BLK] for g in range(4)]
            ps, den, lse = [None] * 4, [], []
            for h in range(2):
                m = jnp.max(jnp.maximum(grp[h], grp[2 + h]), axis=-1, keepdims=True)
                ps[h], ps[2 + h] = jnp.exp(grp[h] - m), jnp.exp(grp[2 + h] - m)
                dn = jnp.sum(ps[h] + ps[2 + h], axis=-1, keepdims=True)
                den.append(dn)
                lse.append(m + jnp.log(dn))
            pmat = jnp.concatenate([x.astype(BF16) for x in ps], axis=1)
            o = _dot(pmat, vbd[keys_of(p, n), :])
            if pending is not None:
                finish(*pending)
            pending = (p, n, o, den, lse)
        finish(*pending)

    def which(w):
        return pl.BlockSpec((None, qt, ATT_W), lambda i: (w, i, 0))

    def halo(w):
        return pl.BlockSpec((None, BLK, ATT_W), lambda i: (w, jnp.maximum(i * nq - 1, 0), 0))

    out = pl.BlockSpec((qt, ATT_W), lambda i: (i, 0))
    bd = pltpu.VMEM((4 * (nq + 1) * 2 * BLK, LANES), BF16)
    return pl.pallas_call(
        body, name=name, grid=(s_len // qt,),
        in_specs=[which(0), which(1), which(2), halo(1), halo(2)],
        out_specs=[out, out], out_shape=[jax.ShapeDtypeStruct((s_len, ATT_W), F32)] * 2,
        scratch_shapes=[bd, bd],
        compiler_params=_params(("parallel",)),
    )(qkv, qkv, qkv, qkv, qkv)


def _attention_backward(qkv, d_att, lse, delta, seq_len, qt, name):
    s_len = qkv.shape[1]
    nq = qt // BLK
    tiles_per_seq = seq_len // qt
    nblk = s_len // BLK

    def body(q_ref, k_ref, v_ref, kh_ref, vh_ref, do_ref, lse_ref, dl_ref,
             qn_ref, don_ref, lsen_ref, dln_ref, dqkv_ref, kbd, vbd):
        i = pl.program_id(0)
        first = jnp.where((i % tiles_per_seq) == 0, 4 * BLK, 0)
        last = jnp.where((i % tiles_per_seq) == tiles_per_seq - 1, 4 * BLK, 0)
        head0, band, prev = _att_masks()
        band_first = band - prev * first
        band_tail = band[:, 0:2 * BLK] - last
        for p in range(4):
            _fill_block_diagonal(kbd, p, k_ref, kh_ref, head0, nq)
            _fill_block_diagonal(vbd, p, v_ref, vh_ref, head0, nq)
        units = [(p, n) for p in range(4) for n in range(nq + 1)]

        def stage_a(p, n):
            sl = slice(p * LANES, (p + 1) * LANES)
            base = (p * (nq + 1) + n) * 2 * BLK
            if n < nq:
                rows = slice(n * BLK, (n + 1) * BLK)
                q2, do2, lse2, dl2 = q_ref[rows, sl], do_ref[rows, sl], lse_ref[rows, sl], dl_ref[rows, sl]
                keys = slice(base, base + 4 * BLK)
            else:
                q2, do2, lse2, dl2 = qn_ref[:, sl], don_ref[:, sl], lsen_ref[:, sl], dln_ref[:, sl]
                keys = slice(base, base + 2 * BLK)
            s = _dot_nt(q2 * 0.125, kbd[keys, :])
            dp = _dot_nt(do2, vbd[keys, :])
            return q2, do2, lse2, dl2, keys, s, dp

        def stage_b(n, lse2, dl2, s, dp):
            mask = band_tail if n == nq else (band_first if n == 0 else band)
            ps, dss = [], []
            for g in range(s.shape[1] // BLK):
                h = g % 2
                cols = slice(g * BLK, (g + 1) * BLK)
                pg = jnp.exp(jnp.where(mask[:, cols] >= 0, s[:, cols] - lse2[:, h * 64:h * 64 + 1], NEG))
                ps.append(pg.astype(BF16))
                dss.append((pg * (dp[:, cols] - dl2[:, h * 64:h * 64 + 1]) * 0.125).astype(BF16))
            return jnp.concatenate(ps, axis=1), jnp.concatenate(dss, axis=1)

        def heads(r, g):
            return jnp.where(head0, r[g * BLK:(g + 1) * BLK, :], r[(g + 1) * BLK:(g + 2) * BLK, :])

        a_next = stage_a(*units[0])
        carry = None
        for ui, (p, n) in enumerate(units):
            q2, do2, lse2, dl2, keys, s, dp = a_next
            if ui + 1 < len(units):
                a_next = stage_a(*units[ui + 1])
            pmat, dsmat = stage_b(n, lse2, dl2, s, dp)
            sl = slice(p * LANES, (p + 1) * LANES)
            if n < nq:
                dqkv_ref[0, n * BLK:(n + 1) * BLK, sl] = _dot(dsmat, kbd[keys, :])
            dkbd = _dot_tn(dsmat, q2)
            dvbd = _dot_tn(pmat, do2)
            if n > 0:
                prow = slice((n - 1) * BLK, n * BLK)
                dqkv_ref[1, prow, sl] = carry[0] + heads(dkbd, 0)
                dqkv_ref[2, prow, sl] = carry[1] + heads(dvbd, 0)
            if n < nq:
                carry = (heads(dkbd, 2), heads(dvbd, 2))

    def which(w):
        return pl.BlockSpec((None, qt, ATT_W), lambda i: (w, i, 0))

    def prev(w):
        return pl.BlockSpec((None, BLK, ATT_W), lambda i: (w, jnp.maximum(i * nq - 1, 0), 0))

    tile = pl.BlockSpec((qt, ATT_W), lambda i: (i, 0))
    nxt = pl.BlockSpec((BLK, ATT_W), lambda i: (jnp.minimum((i + 1) * nq, nblk - 1), 0))
    nxt_q = pl.BlockSpec((None, BLK, ATT_W), lambda i: (0, jnp.minimum((i + 1) * nq, nblk - 1), 0))
    return pl.pallas_call(
        body, name=name, grid=(s_len // qt,),
        in_specs=[which(0), which(1), which(2), prev(1), prev(2), tile, tile, tile, nxt_q, nxt, nxt, nxt],
        out_specs=pl.BlockSpec((3, qt, ATT_W), lambda i: (0, i, 0)),
        out_shape=jax.ShapeDtypeStruct((3, s_len, ATT_W), F32),
        scratch_shapes=[pltpu.VMEM((4 * (nq + 1) * 2 * BLK, LANES), BF16)] * 2,
        compiler_params=_params(("parallel",)),
    )(qkv, qkv, qkv, qkv, qkv, d_att, lse, delta, qkv, d_att, lse, delta)


def _merge_and_head(o_g, lse_g, p_gate, y_conv, x, tgt, modv, final_g, w_ao, w_o, ts):
    s_len, dm = x.shape
    gw = ATT_W + 2 * dm
    nt = s_len // ts
    gate_off = 3 * dm + 9 * ATT_W
    assert gate_off % gw == 0

    def body(o0, o1, o2, l0, l1, l2, pg_ref, yc_ref, x_ref, t_ref, mod_ref, fg_ref, wao_ref, wo_ref,
             loss_ref, dx_ref, dyc_ref, dpg_ref, da0, da1, da2, ls0, ls1, ls2, de0, de1, de2,
             gwo_ref, gwao_ref, gs_ref, slab):
        i = pl.program_id(0)

        @pl.when(i == 0)
        def _():
            loss_ref[...] = jnp.zeros_like(loss_ref)
            gwo_ref[...] = jnp.zeros_like(gwo_ref)
            gwao_ref[...] = jnp.zeros_like(gwao_ref)
            gs_ref[...] = jnp.zeros_like(gs_ref)

        os_, ls_ = [], []
        for dil, o_ref, l_ref in zip(DILATIONS, (o0, o1, o2), (l0, l1, l2)):
            os_.append(jnp.concatenate(_interleave_load(o_ref, slab, dil, ts), axis=1))
            ls_.append(jnp.concatenate(_interleave_load(l_ref, slab, dil, ts), axis=1))
        mx = jnp.maximum(jnp.maximum(ls_[0], ls_[1]), ls_[2])
        wts = [jnp.exp(l - mx) for l in ls_]
        wsum = wts[0] + wts[1] + wts[2]
        att = (wts[0] * os_[0] + wts[1] * os_[1] + wts[2] * os_[2]) / wsum
        lse_all = mx + jnp.log(wsum)

        z_att, g_conv, g_att = pg_ref[:, 0:ATT_W], pg_ref[:, ATT_W:ATT_W + dm], pg_ref[:, ATT_W + dm:gw]
        sgz = _sig(z_att)
        sz = z_att * sgz
        a_att = (att * sz).astype(BF16)
        y_att = _dot(a_att, wao_ref[...])
        y_conv = yc_ref[...]
        sgc, sga = _sig(g_conv), _sig(g_att)
        merged = (sgc * y_conv + sga * y_att).astype(BF16)
        mo = _dot(merged, wo_ref[...])
        gate = mod_ref[2:3, :]
        x2 = x_ref[...] + gate * mo
        r = lax.rsqrt(jnp.mean(x2 * x2, axis=-1, keepdims=True) + EPS)
        xr = x2 * r
        err = xr * fg_ref[...] - t_ref[...]
        loss_ref[...] += 0.5 * jnp.sum(jnp.mean(err * err, axis=-1, keepdims=True))
        dy = err * (1.0 / dm)
        gs_ref[0:1, :] += jnp.sum(dy * xr, axis=0, keepdims=True)
        dyg = dy * fg_ref[...]
        d_x2 = r * dyg - xr * (r * jnp.mean(dyg * xr, axis=-1, keepdims=True))
        dx_ref[...] = d_x2
        gs_ref[1:2, :] += jnp.sum(d_x2 * mo, axis=0, keepdims=True)
        d_mo = (d_x2 * gate).astype(BF16)
        d_mg = _dot_nt(d_mo, wo_ref[...])
        gwo_ref[...] += _dot_tn(merged, d_mo)
        dyc_ref[...] = (d_mg * sgc).astype(BF16)
        dpg_ref[:, ATT_W:ATT_W + dm] = (d_mg * y_conv * sgc * (1.0 - sgc)).astype(BF16)
        d_ya = (d_mg * sga).astype(BF16)
        dpg_ref[:, ATT_W + dm:gw] = (d_mg * y_att * sga * (1.0 - sga)).astype(BF16)
        gwao_ref[...] += _dot_tn(a_att, d_ya)
        d_aa = _dot_nt(d_ya, wao_ref[...])
        dpg_ref[:, 0:ATT_W] = (d_aa * att * _dsilu(z_att, sgz)).astype(BF16)
        d_att = d_aa * sz
        ri = lax.broadcasted_iota(jnp.int32, (ATT_W, ATT_W), 0) // 64
        ci = lax.broadcasted_iota(jnp.int32, (ATT_W, ATT_W), 1) // 64
        ones = jnp.where(ri == ci, 1.0, 0.0).astype(BF16)
        prod = d_att * att
        hi = prod.astype(BF16)
        lo = (prod - hi.astype(F32)).astype(BF16)
        delta = _dot(hi, ones) + _dot(lo, ones)
        for val, refs, dt in ((d_att, (da0, da1, da2), BF16), (lse_all, (ls0, ls1, ls2), F32),
                              (delta, (de0, de1, de2), F32)):
            vals = [val[:, s * LANES:(s + 1) * LANES] for s in range(4)]
            for dil, ref in zip(DILATIONS, refs):
                _deinterleave_store(vals, slab, ref, dil, ts, dt)

    def grp(dil):
        return pl.BlockSpec((dil, ts // dil, ATT_W), lambda i: (0, i, 0))

    tile = pl.BlockSpec((ts, dm), lambda i: (i, 0))
    gate_tile = pl.BlockSpec((ts, gw), lambda i: (i, 0))
    const = lambda shp: pl.BlockSpec(shp, lambda i: tuple(0 for _ in shp))
    grp_shape = lambda dt: [jax.ShapeDtypeStruct((dil, s_len // dil, ATT_W), dt) for dil in DILATIONS]
    return pl.pallas_call(
        body, name="merge_and_head", grid=(nt,),
        in_specs=[grp(d) for d in DILATIONS] * 2
        + [gate_tile, tile, tile, tile, const((8, dm)), const((1, dm)), const((ATT_W, dm)), const((dm, dm))],
        out_specs=[const((8, LANES)), tile, tile, pl.BlockSpec((ts, gw), lambda i: (i, gate_off // gw))]
        + [grp(d) for d in DILATIONS] * 3
        + [const((dm, dm)), const((ATT_W, dm)), const((8, dm))],
        out_shape=[jax.ShapeDtypeStruct((8, LANES), F32), jax.ShapeDtypeStruct((s_len, dm), F32),
                   jax.ShapeDtypeStruct((s_len, dm), BF16), jax.ShapeDtypeStruct((s_len, gate_off + gw), BF16)]
        + grp_shape(BF16) + grp_shape(F32) + grp_shape(F32)
        + [jax.ShapeDtypeStruct((dm, dm), F32), jax.ShapeDtypeStruct((ATT_W, dm), F32),
           jax.ShapeDtypeStruct((8, dm), F32)],
        scratch_shapes=[pltpu.VMEM((4, ts, LANES), F32)],
        compiler_params=_params(("arbitrary",)),
    )(*o_g, *lse_g, p_gate, y_conv, x, tgt, modv, final_g, w_ao, w_o)


def _qkv_grad_to_tokens(dp, dqkv_g, rope, ts):
    s_len, dm = rope.shape[1], (dp.shape[1] - 10 * ATT_W) // 5
    qw = 3 * ATT_W
    assert (3 * dm) % qw == 0

    def body(dp_in, g0, g1, g2, rope_ref, o_ref, slab):
        del dp_in
        w = pl.program_id(1)
        rc, ra, rb = rope_ref.at[0], rope_ref.at[1], rope_ref.at[2]

        def emit(roped):
            for gi, (dil, g_ref) in enumerate(zip(DILATIONS, (g0, g1, g2))):
                vals = _interleave_load(g_ref, slab, dil, ts)
                for s in range(4):
                    t = vals[s]
                    if roped:
                        t = t * rc[...] + pltpu.roll(t * ra[...], 8, 1) + pltpu.roll(t * rb[...], LANES - 8, 1)
                    col = gi * ATT_W + s * LANES
                    o_ref[:, col:col + LANES] = t.astype(BF16)

        pl.when(w < 2)(lambda: emit(True))
        pl.when(w == 2)(lambda: emit(False))

    return pl.pallas_call(
        body, name="qkv_grad_to_tokens", grid=(s_len // ts, 3),
        in_specs=[ANY] + [pl.BlockSpec((None, dil, ts // dil, ATT_W), lambda i, w: (w, 0, i, 0)) for dil in DILATIONS]
        + [pl.BlockSpec((3, ts, LANES), lambda i, w: (0, i, 0))],
        out_specs=pl.BlockSpec((ts, qw), lambda i, w: (i, 3 * dm // qw + w)),
        out_shape=jax.ShapeDtypeStruct(dp.shape, BF16),
        input_output_aliases={0: 0},
        scratch_shapes=[pltpu.VMEM((4, ts, LANES), F32)],
        compiler_params=_params(("parallel", "arbitrary")),
    )(dp, *dqkv_g, rope)


def _wide_col_tile(cols):
    for width in (5 * COL_TILE, 2 * COL_TILE):
        if cols % width == 0:
            return width
    return COL_TILE


def _input_grad(dp, w_in, x, dx_res, modv, norm_g, ts):
    s_len, dm = x.shape
    ct = next(w * COL_TILE for w in (4, 2, 1) if dp.shape[1] % (w * COL_TILE) == 0)
    nct = dp.shape[1] // ct

    def body(p_ref, w_ref, x_ref, dxr_ref, mod_ref, g_ref, gx_ref, gs_ref, acc):
        i, j = pl.program_id(0), pl.program_id(1)

        @pl.when((i == 0) & (j == 0))
        def _():
            gs_ref[...] = jnp.zeros_like(gs_ref)

        @pl.when(j == 0)
        def _():
            acc[...] = jnp.zeros_like(acc)

        acc[...] += _dot_nt(p_ref[...], w_ref[...])

        @pl.when(j == nct - 1)
        def _():
            d_h = acc[...]
            xv = x_ref[...]
            r = lax.rsqrt(jnp.mean(xv * xv, axis=-1, keepdims=True) + EPS)
            xr = xv * r
            gs_ref[0:1, :] += jnp.sum(d_h, axis=0, keepdims=True)
            gs_ref[1:2, :] += jnp.sum(d_h * (xr * g_ref[...]), axis=0, keepdims=True)
            d_n = d_h * (1.0 + mod_ref[1:2, :])
            gs_ref[2:3, :] += jnp.sum(d_n * xr, axis=0, keepdims=True)
            dxn = d_n * g_ref[...]
            gx_ref[...] = dxr_ref[...] + r * dxn - xr * (r * jnp.mean(dxn * xr, axis=-1, keepdims=True))

    tile = pl.BlockSpec((ts, dm), lambda i, j: (i, 0))
    return pl.pallas_call(
        body, name="input_grad", grid=(s_len // ts, nct),
        in_specs=[pl.BlockSpec((ts, ct), lambda i, j: (i, j)), pl.BlockSpec((dm, ct), lambda i, j: (0, j)), tile, tile,
                  pl.BlockSpec((8, dm), lambda i, j: (0, 0)), pl.BlockSpec((1, dm), lambda i, j: (0, 0))],
        out_specs=[tile, pl.BlockSpec((8, dm), lambda i, j: (0, 0))],
        out_shape=[jax.ShapeDtypeStruct((s_len, dm), F32), jax.ShapeDtypeStruct((8, dm), F32)],
        scratch_shapes=[pltpu.VMEM((ts, dm), F32)],
        compiler_params=_params(("arbitrary", "arbitrary")),
    )(dp, w_in, x, dx_res, modv, norm_g)


def _w_in_grad(h_t, dp, ts):
    dm, s_len = h_t.shape
    ct = _wide_col_tile(dp.shape[1])

    def body(h_ref, p_ref, o_ref):
        @pl.when(pl.program_id(1) == 0)
        def _():
            o_ref[...] = jnp.zeros_like(o_ref)

        o_ref[...] += _dot(h_ref[...], p_ref[...])

    return pl.pallas_call(
        body, name="w_in_grad", grid=(dp.shape[1] // ct, s_len // ts),
        in_specs=[pl.BlockSpec((dm, ts), lambda j, i: (0, i)), pl.BlockSpec((ts, ct), lambda j, i: (i, j))],
        out_specs=pl.BlockSpec((dm, ct), lambda j, i: (0, j)),
        out_shape=jax.ShapeDtypeStruct((dm, dp.shape[1]), F32),
        compiler_params=_params(("arbitrary", "arbitrary")),
    )(h_t, dp)


def _rope_lane_table():
    l64 = jnp.arange(LANES) % 64
    half = ROT_DIM // 2
    inv_freq = ROPE_THETA ** (-(jnp.arange(half, dtype=F32) * 2.0 / ROT_DIM))
    rot = l64 < ROT_DIM
    rows = [jnp.where(rot, inv_freq[l64 % half], 0.0), (l64 < half).astype(F32),
            ((l64 >= half) & rot).astype(F32), rot.astype(F32)]
    return jnp.concatenate([jnp.stack(rows), jnp.zeros((4, LANES), F32)], axis=0)


def _tile_sizes(s_len):
    ts_big = min(1024, s_len // 2)
    ts_mid = 256
    ts_head = 256
    qt = [min(1024, s_len // dil) for dil in DILATIONS]
    return ts_big, ts_mid, ts_head, qt


def kernel(x, c, positions, norm_g, w_ada, b_ada, w_in, conv_w, conv_b, conv_ln_g, conv_ln_b, w_conv_out, w_att_out, w_o, final_g, loss_target, m_norm_g, m_w_ada, m_b_ada, m_w_in, m_conv_w, m_conv_b, m_conv_ln_g, m_conv_ln_b, m_w_conv_out, m_w_att_out, m_w_o, m_final_g, v_norm_g, v_w_ada, v_b_ada, v_w_in, v_conv_w, v_conv_b, v_conv_ln_g, v_conv_ln_b, v_w_conv_out, v_w_att_out, v_w_o, v_final_g):
    s_len, dm = x.shape[1], x.shape[2]
    ts_big, ts_mid, ts_head, qt = _tile_sizes(s_len)
    xi, yi, cidx = _my_place()
    chip = 2 * xi + yi
    batch = 4 * xi + 2 * yi + cidx
    x2d, tgt = x[0], loss_target[0]
    pos = positions.reshape(s_len, 1)
    wa_l, wi_l, cw_l = w_ada[0], w_in[0], conv_w[0]
    wco_l, wao_l, wo_l = w_conv_out[0], w_att_out[0], w_o[0]
    ada_w = wa_l.shape[1]
    cw_cols = cw_l.shape[1]

    kinds = ("col", "row", "col", "row")
    w_bufs = _cast_weights([wi_l, wco_l, wao_l, wo_l], kinds)

    cw_pad = jnp.pad(cw_l, ((0, HALO - CONV_K), (0, 0)))
    small_in = jnp.concatenate([jnp.broadcast_to(c, (8, dm)), cw_pad.reshape(8, dm)], axis=0)
    small = _allgather_small(small_in).reshape(8, 16, dm)
    c_all = small[:, 0, :]
    conv_w_full = jnp.concatenate(
        [small[2 * p, 8:16, :].reshape(HALO, cw_cols) for p in range(4)], axis=1)
    b_l = lax.dynamic_slice(b_ada, (0, chip * ada_w), (1, ada_w))
    mod_parts = _allgather_small(_mod_part(c_all, wa_l, b_l)).reshape(8, 8, ada_w)
    mod_rows = lax.dynamic_index_in_dim(mod_parts, batch, axis=1, keepdims=False)
    mod = jnp.concatenate([mod_rows[2 * p] for p in range(4)], axis=0).reshape(3, dm)
    modv = jnp.concatenate([mod, jnp.zeros((5, dm), F32)], axis=0)

    sems_a, sems_b, w_bufs, token = _gather_weights_start(w_bufs, kinds, modv)
    (w_in_b,) = _gather_weights_wait(w_bufs, kinds, sems_a, 0, token, "gather_weights_wait_w_in")
    w_in_b = _pass_to_sibling(w_in_b, kinds[0])

    rope = _rope_coefficients(pos, _rope_lane_table(), ts_big)
    p_conv, p_gate, h_b, h_t = _input_projection(x2d, modv, norm_g, w_in_b, ts_big)
    qkv_g = [_qkv_projection(h_b, rope, w_in_b, gi, ts_big) for gi in range(3)]
    w_co_b, w_ao_b, w_o_b = _gather_weights_wait(w_bufs, kinds, sems_b, 1, qkv_g[2], "gather_weights_wait_rest")
    y_conv, u0, u1 = _conv_forward(p_conv, conv_w_full, conv_b, conv_ln_g, conv_ln_b, w_co_b, ts_mid)
    qkv_flat = [q.reshape(3, s_len, ATT_W) for q in qkv_g]
    o_g, lse_g = [], []
    for gi, dil in enumerate(DILATIONS):
        o, l = _attention_forward(qkv_flat[gi], s_len // dil, qt[gi], "attention_forward_%d" % dil)
        o_g.append(o.reshape(dil, s_len // dil, ATT_W))
        lse_g.append(l.reshape(dil, s_len // dil, ATT_W))

    (loss_p, dx_res, dyc, dp, da0, da1, da2, ls0, ls1, ls2, de0, de1, de2,
     g_wo, g_wao, head_sums) = _merge_and_head(o_g, lse_g, p_gate, y_conv, x2d, tgt, modv,
                                               final_g.reshape(1, dm), w_ao_b, w_o_b, ts_head)
    loss = lax.psum(loss_p[0, 0], ("x", "y", "c"))

    dp, g_wco, conv_sums, g_cw = _conv_backward(dp, dyc, p_conv, u0, u1, conv_w_full, conv_ln_g, conv_ln_b,
                                                w_co_b, ts_mid)
    dqkv_g = []
    for gi, (dil, da, ls, de) in enumerate(zip(DILATIONS, (da0, da1, da2), (ls0, ls1, ls2), (de0, de1, de2))):
        flat = lambda a: a.reshape(s_len, ATT_W)
        dqkv = _attention_backward(qkv_flat[gi], flat(da), flat(ls), flat(de), s_len // dil, qt[gi],
                                   "attention_backward_%d" % dil)
        dqkv_g.append(dqkv.reshape(3, dil, s_len // dil, ATT_W))
    dp = _qkv_grad_to_tokens(dp, dqkv_g, rope, 2 * ts_mid)
    g_win = _w_in_grad(h_t, dp, ts_big)

    grads = [g_win, g_wco, g_wao, g_wo]
    full_shapes = [g.shape for g in grads]
    c_arr = jnp.reshape(cidx, (1,)).astype(jnp.int32)
    recv_halves = _reduce_pair_exchange(grads, kinds)
    halves = [_reduce_pair_sum(g, pa, kind, c_arr, "reduce_pair_sum_%d" % k)
              for k, (g, pa, kind) in enumerate(zip(grads, recv_halves, kinds))]
    send_sems, recv_sems, halves, lands, token = _reduce_to_owner_start(halves, kinds, full_shapes)
    grad_x, in_sums = _input_grad(dp, w_in_b, x2d, dx_res, modv + token[0, 0], norm_g, ts_big)
    halves, recvd = _reduce_to_owner_wait(send_sems, recv_sems, halves, lands, in_sums, kinds)
    gr_win, gr_wco, gr_wao, gr_wo = _reduce_finish(halves, recvd, kinds, full_shapes)

    rows = [in_sums[2:3], conv_sums[2:3], conv_sums[0:1], conv_sums[1:2], head_sums[0:1],
            in_sums[0:1], in_sums[1:2], head_sums[1:2], g_cw, jnp.zeros((8, dm), F32)]
    part = jnp.concatenate(rows, axis=0)
    gathered = _allgather_small(part)
    tot = _sum_devices(gathered, 48)
    dmod_all = gathered.reshape(8, 48, dm)[:, 5:8, :].reshape(8, 3 * dm)
    dmod_l = lax.dynamic_slice(dmod_all, (0, chip * ada_w), (8, ada_w))
    gr_wada = _grad_w_ada(c_all.T, dmod_l)
    gr_cw = lax.dynamic_slice(tot[8:8 + HALO], (0, chip * cw_cols), (HALO, cw_cols))

    pad_cw = lambda a: jnp.pad(a[0], ((0, HALO - CONV_K), (0, 0)))
    row = lambda a: a.reshape(1, dm)
    small_upd = _adamw_small(
        [norm_g, conv_b, conv_ln_g, conv_ln_b, row(final_g), b_ada], tot[0:8],
        [m_norm_g, m_conv_b, m_conv_ln_g, m_conv_ln_b, row(m_final_g), m_b_ada],
        [v_norm_g, v_conv_b, v_conv_ln_g, v_conv_ln_b, row(v_final_g), v_b_ada])
    upd = {
        "w_ada": _adamw(wa_l, gr_wada, m_w_ada[0], v_w_ada[0], "adamw_w_ada"),
        "w_in": _adamw(wi_l, gr_win, m_w_in[0], v_w_in[0], "adamw_w_in"),
        "conv_w": _adamw(cw_pad, gr_cw, pad_cw(m_conv_w), pad_cw(v_conv_w), "adamw_conv_w"),
        "w_co": _adamw(wco_l, gr_wco, m_w_conv_out[0], v_w_conv_out[0], "adamw_w_conv_out"),
        "w_ao": _adamw(wao_l, gr_wao, m_w_att_out[0], v_w_att_out[0], "adamw_w_att_out"),
        "w_o": _adamw(wo_l, gr_wo, m_w_o[0], v_w_o[0], "adamw_w_o"),
    }

    def family(which):
        if which is None:
            sm = small_upd[0:6]
            big = {"w_ada": gr_wada, "w_in": gr_win, "conv_w": gr_cw, "w_co": gr_wco, "w_ao": gr_wao, "w_o": gr_wo}
        else:
            sm = small_upd[6 * (which + 1):6 * (which + 2)]
            big = {k: upd[k][which] for k in ("w_ada", "w_in", "conv_w", "w_co", "w_ao", "w_o")}
        return [sm[0], big["w_ada"][None], sm[5], big["w_in"][None],
                big["conv_w"][None, :CONV_K], sm[1], sm[2], sm[3], big["w_co"][None],
                big["w_ao"][None], big["w_o"][None], sm[4].reshape(dm)]

    return (loss, grad_x[None], *family(None), *family(0), *family(1), *family(2))
```

```python
import jax
import jax.numpy as jnp
from jax import lax
from jax.experimental import pallas as pl
from jax.experimental.pallas import tpu as pltpu

F32 = jnp.float32
BF16 = jnp.bfloat16
MESH = pl.DeviceIdType.MESH
ANY = pl.BlockSpec(memory_space=pl.ANY)
VM = pl.BlockSpec(memory_space=pltpu.VMEM)

EPS = 1e-6
NEG = -1e30
ATT_W = 512
DILATIONS = (1, 4, 16)
BLK = 128
CONV_K = 31
HALO = 32
CONV_ROWS_FWD = 32
CONV_ROWS_BWD = 16
ROT_DIM = 16
ROPE_THETA = 500000.0
COL_TILE = 512
LANES = 128
VMEM_LIMIT = 56 * 1024 * 1024

ADAM_LR, ADAM_B1, ADAM_B2, ADAM_EPS, ADAM_WD, ADAM_STEP = 0.001, 0.9, 0.999, 1e-08, 0.01, 10


def _params(sem=None, vmem=VMEM_LIMIT):
    return pltpu.CompilerParams(dimension_semantics=sem, vmem_limit_bytes=vmem)


def _dot(a, b):
    return jnp.dot(a, b, preferred_element_type=F32)


def _dot_nt(a, b):
    return lax.dot_general(a, b, (((1,), (1,)), ((), ())), preferred_element_type=F32)


def _dot_tn(a, b):
    return lax.dot_general(a, b, (((0,), (0,)), ((), ())), preferred_element_type=F32)


def _sig(x):
    return jax.nn.sigmoid(x)


def _dsilu(x, s):
    return s * (1.0 + x * (1.0 - s))


def _my_place():
    return lax.axis_index("x"), lax.axis_index("y"), lax.axis_index("c")


def _allgather_small(x_shard):
    m_per, n = x_shard.shape

    def body(x_ref, out_ref, send_sems, recv_sems, local_sem):
        x, y, c = _my_place()
        me, sibling = (x, y, c), (x, y, 1 - c)
        chips = [(1 - x, y), (x, 1 - y), (1 - x, 1 - y)]

        def rows(px, py, pc):
            return out_ref.at[pl.ds((4 * px + 2 * py + pc) * m_per, m_per), :]

        def copy(k, block, to, src=None):
            return pltpu.make_async_remote_copy(
                src_ref=rows(*block) if src is None else src, dst_ref=rows(*block),
                send_sem=send_sems.at[k], recv_sem=recv_sems.at[k],
                device_id=to, device_id_type=MESH)

        mine = pltpu.make_async_copy(x_ref, rows(*me), local_sem)
        mine.start()
        first = [copy(0, me, sibling, src=x_ref)]
        first += [copy(1 + j, me, (*chip, c), src=x_ref) for j, chip in enumerate(chips)]
        for cp in first:
            cp.start()
        passed = [copy(4 + j, (*chip, c), sibling) for j, chip in enumerate(chips)]
        for j, chip in enumerate(chips):
            copy(1 + j, (*chip, c), me).wait_recv()
            passed[j].start()
        copy(0, sibling, me).wait_recv()
        for j, chip in enumerate(chips):
            copy(4 + j, (*chip, 1 - c), me).wait_recv()
        for cp in first + passed:
            cp.wait_send()
        mine.wait()

    return pl.pallas_call(
        body, name="allgather_small",
        out_shape=jax.ShapeDtypeStruct((8 * m_per, n), x_shard.dtype),
        in_specs=[VM], out_specs=VM,
        scratch_shapes=[pltpu.SemaphoreType.DMA((7,)), pltpu.SemaphoreType.DMA((7,)),
                        pltpu.SemaphoreType.DMA],
    )(x_shard)


def _shard_window(ref, kind, p, n_shards=4):
    r, c = ref.shape
    if kind == "col":
        w = c // n_shards
        return ref.at[:, pl.ds(p * w, w)]
    w = r // n_shards
    return ref.at[pl.ds(p * w, w), :]


def _half_window(ref, kind, hc):
    r, c = ref.shape
    if kind == "col":
        return ref.at[pl.ds(hc * (r // 2), r // 2), :]
    return ref.at[:, pl.ds(hc * (c // 2), c // 2)]


def _landed(ref, kind, chip, hc):
    return _half_window(_shard_window(ref, kind, 2 * chip[0] + chip[1]), kind, hc)


def _cast_weights(shards, kinds):
    n = len(shards)
    full_shapes = [(s.shape[0], 4 * s.shape[1]) if kind == "col" else (4 * s.shape[0], s.shape[1])
                   for s, kind in zip(shards, kinds)]

    def body(*refs):
        w_refs, out_refs, bf_refs, sems = refs[:n], refs[n:2 * n], refs[2 * n:3 * n], refs[3 * n]
        x, y, _ = _my_place()
        cps = []
        for k in range(n):
            bf_refs[k][...] = w_refs[k][...].astype(BF16)
            cp = pltpu.make_async_copy(bf_refs[k], _shard_window(out_refs[k], kinds[k], 2 * x + y), sems.at[k])
            cp.start()
            cps.append(cp)
        for cp in cps:
            cp.wait()

    return pl.pallas_call(
        body, name="cast_weights",
        out_shape=[jax.ShapeDtypeStruct(s, BF16) for s in full_shapes],
        in_specs=[VM] * n, out_specs=[ANY] * n,
        scratch_shapes=[pltpu.VMEM(s.shape, BF16) for s in shards] + [pltpu.SemaphoreType.DMA((n,))],
        compiler_params=_params(),
    )(*shards)


def _gather_copies(refs, kinds, sems_a, sems_b):
    x, y, c = _my_place()
    chips = [(1 - x, y), (x, 1 - y), (1 - x, 1 - y)]
    set_a, set_b = [], []
    for j, chip in enumerate(chips):
        for k in range(len(refs)):
            if refs[k] is None:
                continue
            for flip in ((0,) if k == 0 else (0, 1)):
                win = _landed(refs[k], kinds[k], (x, y), c)
                sems, idx = (sems_a, j) if k == 0 else (sems_b, ((k - 1) * 3 + j) * 2 + flip)
                mine = _landed(refs[k], kinds[k], chip, (c + flip) % 2)
                (set_a if k == 0 else set_b).append((
                    pltpu.make_async_remote_copy(
                        src_ref=win, dst_ref=win, send_sem=sems[0].at[idx], recv_sem=sems[1].at[idx],
                        device_id=(*chip, (c + flip) % 2), device_id_type=MESH),
                    pltpu.make_async_remote_copy(
                        src_ref=mine, dst_ref=mine, send_sem=sems[0].at[idx], recv_sem=sems[1].at[idx],
                        device_id=(*chip, (c + flip) % 2), device_id_type=MESH)))
    return set_a, set_b


_SPLIT = dict(has_side_effects=pltpu.SideEffectType.DATAFLOW_SIDE_EFFECTING)


def _gather_weights_start(fulls, kinds, after):
    n = len(fulls)
    hbm = pl.BlockSpec(memory_space=pltpu.HBM)
    sem = pl.BlockSpec(memory_space=pltpu.SEMAPHORE)
    nb = (n - 1) * 6

    def body(*refs):
        in_refs = refs[:n]
        sa, ra, sb, rb = refs[n + 1:n + 5]
        token = refs[-1]
        set_a, set_b = _gather_copies(in_refs, kinds, (sa, ra), (sb, rb))
        for out_cp, _ in set_a + set_b:
            out_cp.start()
        token[...] = jnp.zeros_like(token)

    out = pl.pallas_call(
        body, name="gather_weights_start",
        out_shape=[pltpu.SemaphoreType.DMA((3,)), pltpu.SemaphoreType.DMA((3,)),
                   pltpu.SemaphoreType.DMA((nb,)), pltpu.SemaphoreType.DMA((nb,))]
        + [pltpu.HBM(f.shape, f.dtype) for f in fulls] + [jax.ShapeDtypeStruct((8, LANES), F32)],
        in_specs=[hbm] * n + [ANY], out_specs=[sem] * 4 + [hbm] * n + [VM],
        input_output_aliases={k: 4 + k for k in range(n)},
        compiler_params=pltpu.CompilerParams(**_SPLIT),
    )(*[pltpu.with_memory_space_constraint(f, pltpu.HBM) for f in fulls], after)
    return (out[0], out[1]), (out[2], out[3]), out[4:4 + n], out[-1]


def _gather_weights_wait(fulls, kinds, sems, which, after, name):
    n = len(fulls)
    hbm = pl.BlockSpec(memory_space=pltpu.HBM)
    sem = pl.BlockSpec(memory_space=pltpu.SEMAPHORE)
    keep = [0] if which == 0 else list(range(1, n))

    def body(*refs):
        m = len(keep)
        full_refs = [None] * n
        for pos_, k in enumerate(keep):
            full_refs[k] = refs[pos_]
        s_ref, r_ref = refs[m:m + 2]
        if which == 0:
            sets = _gather_copies([full_refs[0]], kinds[:1], (s_ref, r_ref), None)[0]
        else:
            sets = _gather_copies([None] + [full_refs[k] for k in keep], kinds, None, (s_ref, r_ref))[1]
        for out_cp, in_cp in sets:
            out_cp.wait_send()
            in_cp.wait_recv()

    out = pl.pallas_call(
        body, name=name,
        out_shape=[pltpu.HBM(fulls[k].shape, fulls[k].dtype) for k in keep],
        in_specs=[hbm] * len(keep) + [sem, sem] + [ANY] * len(after), out_specs=[hbm] * len(keep),
        input_output_aliases={i: i for i in range(len(keep))},
        compiler_params=pltpu.CompilerParams(**_SPLIT),
    )(*[fulls[k] for k in keep], sems[0], sems[1], *after)
    return list(out)


def _pass_to_sibling(w_full, kind):
    def body(w_in_ref, w_ref, send_sems, recv_sems):
        del w_in_ref
        x, y, c = _my_place()
        chips = [(1 - x, y), (x, 1 - y), (1 - x, 1 - y)]
        cps = []
        for j, chip in enumerate(chips):
            win = _landed(w_ref, kind, chip, c)
            cp = pltpu.make_async_remote_copy(
                src_ref=win, dst_ref=win, send_sem=send_sems.at[j], recv_sem=recv_sems.at[j],
                device_id=(x, y, 1 - c), device_id_type=MESH)
            cp.start()
            cps.append(cp)
        for j, chip in enumerate(chips):
            theirs = _landed(w_ref, kind, chip, 1 - c)
            pltpu.make_async_remote_copy(
                src_ref=theirs, dst_ref=theirs, send_sem=send_sems.at[j], recv_sem=recv_sems.at[j],
                device_id=(x, y, 1 - c), device_id_type=MESH).wait_recv()
        for cp in cps:
            cp.wait_send()

    return pl.pallas_call(
        body, name="pass_to_sibling",
        out_shape=jax.ShapeDtypeStruct(w_full.shape, w_full.dtype),
        in_specs=[ANY], out_specs=ANY, input_output_aliases={0: 0},
        scratch_shapes=[pltpu.SemaphoreType.DMA((3,)), pltpu.SemaphoreType.DMA((3,))],
    )(w_full)


def _reduce_pair_exchange(grads, kinds):
    n = len(grads)
    half_shapes = [(g.shape[0] // 2, g.shape[1]) if kind == "col" else (g.shape[0], g.shape[1] // 2)
                   for g, kind in zip(grads, kinds)]

    def body(*refs):
        g_refs, pa_refs = refs[:n], refs[n:2 * n]
        send_sems, recv_sems = refs[2 * n:]
        x, y, c = _my_place()
        cps = []
        for k in range(n):
            cp = pltpu.make_async_remote_copy(
                src_ref=_half_window(g_refs[k], kinds[k], 1 - c), dst_ref=pa_refs[k],
                send_sem=send_sems.at[k], recv_sem=recv_sems.at[k],
                device_id=(x, y, 1 - c), device_id_type=MESH)
            cp.start()
            cps.append(cp)
        for cp in cps:
            cp.wait()

    return pl.pallas_call(
        body, name="reduce_pair_exchange",
        out_shape=[jax.ShapeDtypeStruct(s, F32) for s in half_shapes],
        in_specs=[ANY] * n, out_specs=[ANY] * n,
        scratch_shapes=[pltpu.SemaphoreType.DMA((n,)), pltpu.SemaphoreType.DMA((n,))],
    )(*grads)


def _row_tile(rows, cols, itemsize=4, target=2 * 1024 * 1024, mult=16):
    t = rows
    while t % 2 == 0 and t // 2 >= mult and (t // 2) % mult == 0 and t * cols * itemsize > target:
        t //= 2
    return t


def _reduce_pair_sum(g, pa, kind, c_arr, name):
    hr, hc_ = pa.shape
    tr = _row_tile(hr, hc_)
    nb = hr // tr

    def body(c_ref, g_ref, pa_ref, o_ref):
        o_ref[...] = (g_ref[...] + pa_ref[...]).astype(BF16)

    if kind == "col":
        g_map = lambda i, c_ref: (c_ref[0] * nb + i, 0)
    else:
        g_map = lambda i, c_ref: (i, c_ref[0])
    return pl.pallas_call(
        body, name=name,
        grid_spec=pltpu.PrefetchScalarGridSpec(
            num_scalar_prefetch=1, grid=(nb,),
            in_specs=[pl.BlockSpec((tr, hc_), g_map), pl.BlockSpec((tr, hc_), lambda i, c_ref: (i, 0))],
            out_specs=pl.BlockSpec((tr, hc_), lambda i, c_ref: (i, 0))),
        out_shape=jax.ShapeDtypeStruct((hr, hc_), BF16),
        compiler_params=_params(("parallel",)),
    )(c_arr, g, pa)


def _half_shard_shape(full_shape, kind):
    r, c = full_shape
    return (r // 2, c // 4) if kind == "col" else (r // 4, c // 2)


def _to_owner_copies(h_refs, land_refs, send_sems, recv_sems, kinds):
    n = len(h_refs)
    x, y, c = _my_place()
    chips = [(1 - x, y), (x, 1 - y), (1 - x, 1 - y)]
    cps = []
    for j, chip in enumerate(chips):
        pj = 2 * chip[0] + chip[1]
        for k in range(n):
            cps.append(pltpu.make_async_remote_copy(
                src_ref=_shard_window(h_refs[k], kinds[k], pj), dst_ref=land_refs[k].at[j],
                send_sem=send_sems.at[j * n + k], recv_sem=recv_sems.at[j * n + k],
                device_id=(*chip, c), device_id_type=MESH))
    return cps


def _reduce_to_owner_start(halves, kinds, full_shapes):
    n = len(halves)
    hs = [_half_shard_shape(fs, kind) for fs, kind in zip(full_shapes, kinds)]
    hbm = pl.BlockSpec(memory_space=pltpu.HBM)
    sem = pl.BlockSpec(memory_space=pltpu.SEMAPHORE)

    def body(*refs):
        h_refs, land_refs = refs[:n], refs[n:2 * n]
        send_sems, recv_sems = refs[2 * n:2 * n + 2]
        token = refs[-1]
        for cp in _to_owner_copies(h_refs, land_refs, send_sems, recv_sems, kinds):
            cp.start()
        token[...] = jnp.zeros_like(token)

    lands = [pltpu.with_memory_space_constraint(lax.empty((3,) + s, BF16), pltpu.HBM) for s in hs]
    out = pl.pallas_call(
        body, name="reduce_to_owner_start",
        out_shape=[pltpu.SemaphoreType.DMA((3 * n,)), pltpu.SemaphoreType.DMA((3 * n,))]
        + [pltpu.HBM(h.shape, h.dtype) for h in halves] + [pltpu.HBM((3,) + s, BF16) for s in hs]
        + [jax.ShapeDtypeStruct((8, LANES), F32)],
        in_specs=[hbm] * (2 * n), out_specs=[sem, sem] + [hbm] * (2 * n) + [VM],
        input_output_aliases={k: 2 + k for k in range(2 * n)},
        compiler_params=pltpu.CompilerParams(has_side_effects=pltpu.SideEffectType.DATAFLOW_SIDE_EFFECTING),
    )(*[pltpu.with_memory_space_constraint(h, pltpu.HBM) for h in halves], *lands)
    return out[0], out[1], out[2:2 + n], out[2 + n:2 + 2 * n], out[-1]


def _reduce_to_owner_wait(send_sems, recv_sems, halves, lands, after, kinds):
    n = len(halves)
    hbm = pl.BlockSpec(memory_space=pltpu.HBM)
    sem = pl.BlockSpec(memory_space=pltpu.SEMAPHORE)

    def body(*refs):
        h_refs, land_refs = refs[:n], refs[n:2 * n]
        send_s, recv_s = refs[2 * n:2 * n + 2]
        for cp in _to_owner_copies(h_refs, land_refs, send_s, recv_s, kinds):
            cp.wait_send()
            cp.wait_recv()

    out = pl.pallas_call(
        body, name="reduce_to_owner_wait",
        out_shape=[pltpu.HBM(h.shape, h.dtype) for h in halves] + [pltpu.HBM(l.shape, l.dtype) for l in lands],
        in_specs=[hbm] * (2 * n) + [sem, sem, ANY], out_specs=[hbm] * (2 * n),
        input_output_aliases={k: k for k in range(2 * n)},
        compiler_params=pltpu.CompilerParams(has_side_effects=pltpu.SideEffectType.DATAFLOW_SIDE_EFFECTING),
    )(*halves, *lands, send_sems, recv_sems, after)
    return out[:n], out[n:]


def _reduce_finish(halves, recvd, kinds, full_shapes):
    n = len(halves)
    hs = [_half_shard_shape(fs, kind) for fs, kind in zip(full_shapes, kinds)]
    shard_shapes = [(fs[0], fs[1] // 4) if kind == "col" else (fs[0] // 4, fs[1])
                    for fs, kind in zip(full_shapes, kinds)]

    def body(*refs):
        h_refs, rc_refs, gs_refs = refs[:n], refs[n:2 * n], refs[2 * n:3 * n]
        own_refs, gh_refs = refs[3 * n:4 * n], refs[4 * n:5 * n]
        in_sems, loc_sems, send_sems, recv_sems = refs[5 * n:]
        x, y, c = _my_place()
        p = 2 * x + y
        loads = []
        for k in range(n):
            cp = pltpu.make_async_copy(_shard_window(h_refs[k], kinds[k], p), own_refs[k], in_sems.at[k])
            cp.start()
            loads.append(cp)
        outs = []
        for k in range(n):
            loads[k].wait()
            gh_refs[k][...] = (own_refs[k][...].astype(F32) + rc_refs[k][0].astype(F32)
                               + rc_refs[k][1].astype(F32) + rc_refs[k][2].astype(F32))
            dst = _half_window(gs_refs[k], kinds[k], c)
            lc = pltpu.make_async_copy(gh_refs[k], dst, loc_sems.at[k])
            lc.start()
            rc = pltpu.make_async_remote_copy(
                src_ref=gh_refs[k], dst_ref=dst, send_sem=send_sems.at[k], recv_sem=recv_sems.at[k],
                device_id=(x, y, 1 - c), device_id_type=MESH)
            rc.start()
            outs.append((lc, rc))
        for k, (lc, rc) in enumerate(outs):
            lc.wait()
            rc.wait_send()
            pltpu.make_async_remote_copy(
                src_ref=gh_refs[k], dst_ref=_half_window(gs_refs[k], kinds[k], 1 - c),
                send_sem=send_sems.at[k], recv_sem=recv_sems.at[k],
                device_id=(x, y, 1 - c), device_id_type=MESH).wait_recv()

    return pl.pallas_call(
        body, name="reduce_finish",
        out_shape=[jax.ShapeDtypeStruct(s, F32) for s in shard_shapes],
        in_specs=[ANY] * n + [VM] * n, out_specs=[ANY] * n,
        scratch_shapes=[pltpu.VMEM(s, BF16) for s in hs] + [pltpu.VMEM(s, F32) for s in hs]
        + [pltpu.SemaphoreType.DMA((n,)) for _ in range(4)],
        compiler_params=_params(),
    )(*halves, *recvd)


def _mod_part(c_all, w_ada_l, b_l):
    def body(c_ref, w_ref, b_ref, o_ref):
        o_ref[...] = _dot(c_ref[...].astype(BF16), w_ref[...].astype(BF16)) + b_ref[...]

    return pl.pallas_call(
        body, name="mod_part", out_shape=jax.ShapeDtypeStruct((8, w_ada_l.shape[1]), F32),
        in_specs=[VM, VM, VM], out_specs=VM, compiler_params=_params(),
    )(c_all, w_ada_l, b_l)


def _sum_devices(parts, m):
    def body(p_ref, o_ref):
        acc = p_ref[0:m, :]
        for d in range(1, 8):
            acc = acc + p_ref[d * m:(d + 1) * m, :]
        o_ref[...] = acc

    return pl.pallas_call(
        body, name="sum_devices", out_shape=jax.ShapeDtypeStruct((m, parts.shape[1]), F32),
        in_specs=[VM], out_specs=VM, compiler_params=_params(),
    )(parts)


def _grad_w_ada(c_all_t, dmod_l):
    d, w = c_all_t.shape[0], dmod_l.shape[1]

    def body(ct_ref, dm_ref, o_ref):
        acc = ct_ref[:, 0:1] * dm_ref[0:1, :]
        for b in range(1, 8):
            acc = acc + ct_ref[:, b:b + 1] * dm_ref[b:b + 1, :]
        o_ref[...] = acc

    return pl.pallas_call(
        body, name="grad_w_ada", out_shape=jax.ShapeDtypeStruct((d, w), F32),
        in_specs=[VM, VM], out_specs=VM, compiler_params=_params(),
    )(c_all_t, dmod_l)


def _adamw_math(w, g, m, v):
    nm = ADAM_B1 * m + (1.0 - ADAM_B1) * g
    nv = ADAM_B2 * v + (1.0 - ADAM_B2) * (g * g)
    m_hat = nm / (1.0 - ADAM_B1 ** ADAM_STEP)
    v_hat = nv / (1.0 - ADAM_B2 ** ADAM_STEP)
    return -ADAM_LR * (m_hat / (jnp.sqrt(v_hat) + ADAM_EPS) + ADAM_WD * w), nm, nv


def _adamw_small(ws, g8, ms, vs):
    shapes = [jax.ShapeDtypeStruct(w.shape, F32) for w in ws]

    def body(*refs):
        w_refs, g_ref, m_refs, v_refs, outs = refs[0:6], refs[6], refs[7:13], refs[13:19], refs[19:]
        for idx in range(6):
            if idx < 5:
                g = g_ref[idx:idx + 1, :]
            else:
                g = jnp.concatenate([g_ref[5:6, :], g_ref[6:7, :], g_ref[7:8, :]], axis=1)
            res = (g,) + _adamw_math(w_refs[idx][...], g, m_refs[idx][...], v_refs[idx][...])
            for fam in range(4):
                outs[6 * fam + idx][...] = res[fam]

    return pl.pallas_call(
        body, name="adamw_small", out_shape=shapes * 4,
        in_specs=[VM] * 19, out_specs=[VM] * 24, compiler_params=_params(),
    )(*ws, g8, *ms, *vs)


def _adamw(w, g, m, v, name):
    r, c = w.shape
    tr = _row_tile(r, c, target=1024 * 1024, mult=8)

    def body(w_ref, g_ref, m_ref, v_ref, d_ref, nm_ref, nv_ref):
        d_ref[...], nm_ref[...], nv_ref[...] = _adamw_math(w_ref[...], g_ref[...], m_ref[...], v_ref[...])

    spec = pl.BlockSpec((tr, c), lambda i: (i, 0))
    return pl.pallas_call(
        body, name=name, grid=(r // tr,),
        out_shape=[jax.ShapeDtypeStruct((r, c), F32)] * 3,
        in_specs=[spec] * 4, out_specs=[spec] * 3,
        compiler_params=_params(("parallel",)),
    )(w, g, m, v)


def _rope_coefficients(pos, ropetab, ts):
    s_len = pos.shape[0]

    def body(pos_ref, tab_ref, o_ref):
        ang = pos_ref[...].astype(F32) * tab_ref[0:1, :]
        cs, sn = jnp.cos(ang), jnp.sin(ang)
        o_ref[0] = jnp.where(tab_ref[3:4, :] > 0, cs, 1.0)
        o_ref[1] = -sn * tab_ref[1:2, :]
        o_ref[2] = sn * tab_ref[2:3, :]

    return pl.pallas_call(
        body, name="rope_coefficients", grid=(s_len // ts,),
        in_specs=[pl.BlockSpec((ts, 1), lambda i: (i, 0)), pl.BlockSpec((8, LANES), lambda i: (0, 0))],
        out_specs=pl.BlockSpec((3, ts, LANES), lambda i: (0, i, 0)),
        out_shape=jax.ShapeDtypeStruct((3, s_len, LANES), F32),
        compiler_params=_params(("parallel",)),
    )(pos, ropetab)


def _deinterleave_store(vals, slab, out_ref, d, ts, dtype):
    if d == 1:
        for s in range(4):
            out_ref[0, :, s * LANES:(s + 1) * LANES] = vals[s].astype(dtype)
        return
    for s in range(4):
        slab[s] = vals[s]
    for r in range(d):
        for s in range(4):
            out_ref[r, :, s * LANES:(s + 1) * LANES] = slab[s, pl.ds(r, ts // d, stride=d), :].astype(dtype)


def _interleave_load(blk_ref, slab, d, ts):
    if d == 1:
        return [blk_ref[0, :, s * LANES:(s + 1) * LANES] for s in range(4)]
    for r in range(d):
        for s in range(4):
            slab[s, pl.ds(r, ts // d, stride=d), :] = blk_ref[r, :, s * LANES:(s + 1) * LANES]
    return [slab[s] for s in range(4)]


def _norm_modulate(x, modv, norm_g, ts):
    s_len, d_model = x.shape

    def body(x_ref, mod_ref, g_ref, h_ref, ht_ref):
        xv = x_ref[...]
        r = lax.rsqrt(jnp.mean(xv * xv, axis=-1, keepdims=True) + EPS)
        h = (xv * r) * g_ref[...] * (1.0 + mod_ref[1:2, :]) + mod_ref[0:1, :]
        h_ref[...] = h.astype(BF16)
        ht_ref[...] = h.T.astype(BF16)

    return pl.pallas_call(
        body, name="norm_modulate", grid=(s_len // ts,),
        in_specs=[pl.BlockSpec((ts, d_model), lambda i: (i, 0)), pl.BlockSpec((8, d_model), lambda i: (0, 0)),
                  pl.BlockSpec((1, d_model), lambda i: (0, 0))],
        out_specs=[pl.BlockSpec((ts, d_model), lambda i: (i, 0)), pl.BlockSpec((d_model, ts), lambda i: (0, i))],
        out_shape=[jax.ShapeDtypeStruct((s_len, d_model), BF16), jax.ShapeDtypeStruct((d_model, s_len), BF16)],
        compiler_params=_params(("parallel",)),
    )(x, modv, norm_g)


def _plain_projection(h, w_in, first_tile, n_tiles, ts, name):
    s_len, d_model = h.shape
    ct = COL_TILE

    def body(h_ref, w_ref, o_ref):
        o_ref[...] = _dot(h_ref[...], w_ref[...])

    return pl.pallas_call(
        body, name=name, grid=(s_len // ts, n_tiles),
        in_specs=[pl.BlockSpec((ts, d_model), lambda i, j: (i, 0)),
                  pl.BlockSpec((d_model, ct), lambda i, j: (0, first_tile + j))],
        out_specs=pl.BlockSpec((ts, ct), lambda i, j: (i, j)),
        out_shape=jax.ShapeDtypeStruct((s_len, n_tiles * ct), F32),
        compiler_params=_params(("parallel", "arbitrary")),
    )(h, w_in)


def _qkv_projection(h, rope, w_in, gi, ts):
    s_len, d_model = h.shape
    dil = DILATIONS[gi]
    nc = 3 * d_model // COL_TILE
    hr = ts // 2
    assert COL_TILE == ATT_W and hr % (16 * dil) == 0

    def body(h_ref, rope_ref, wq_ref, wk_ref, wv_ref, o_ref, slab):
        rc, ra, rb = rope_ref.at[0], rope_ref.at[1], rope_ref.at[2]
        w_refs = (wq_ref, wk_ref, wv_ref)
        units = [(w, half) for w in range(3) for half in range(2)]

        def matmul(w, half):
            return _dot(h_ref[half * hr:(half + 1) * hr, :], w_refs[w][...])

        def finish(w, half, res):
            rows = slice(half * hr, (half + 1) * hr)
            vals = []
            for s in range(4):
                t = res[:, s * LANES:(s + 1) * LANES]
                if w < 2:
                    t = (t * rc[rows, :] + pltpu.roll(t, LANES - 8, 1) * ra[rows, :]
                         + pltpu.roll(t, 8, 1) * rb[rows, :])
                vals.append(t)
            out = o_ref.at[w, :, half * (hr // dil):(half + 1) * (hr // dil), :]
            _deinterleave_store(vals, slab.at[half], out, dil, hr, BF16)

        res_next = matmul(*units[0])
        for ui, (w, half) in enumerate(units):
            res = res_next
            if ui + 1 < len(units):
                res_next = matmul(*units[ui + 1])
            finish(w, half, res)

    def w_spec(w):
        return pl.BlockSpec((d_model, COL_TILE), lambda i: (0, nc + 3 * w + gi))

    return pl.pallas_call(
        body, name="qkv_projection_%d" % dil, grid=(s_len // ts,),
        in_specs=[pl.BlockSpec((ts, d_model), lambda i: (i, 0)), pl.BlockSpec((3, ts, LANES), lambda i: (0, i, 0)),
                  w_spec(0), w_spec(1), w_spec(2)],
        out_specs=pl.BlockSpec((3, dil, ts // dil, ATT_W), lambda i: (0, 0, i, 0)),
        out_shape=jax.ShapeDtypeStruct((3, dil, s_len // dil, ATT_W), BF16),
        scratch_shapes=[pltpu.VMEM((2, 4, hr, LANES), F32)],
        compiler_params=_params(("parallel",)),
    )(h, rope, w_in, w_in, w_in)


def _layernorm_stats(u1):
    mu = jnp.mean(u1, axis=-1, keepdims=True)
    xc = u1 - mu
    rstd = lax.rsqrt(jnp.mean(xc * xc, axis=-1, keepdims=True) + EPS)
    return xc * rstd, rstd


def _shifted_copies(win, shf, ts):
    rows = ts + HALO - 8
    for b in range(1, 8):
        shf[b - 1, 0:rows, :] = win[pl.ds(b, rows), :]


def _tap(win, shf, off, r0, rows):
    a, b = divmod(off, 8)
    start = 8 * a + r0
    if b == 0:
        return win[start:start + rows, :]
    return shf[b - 1, start:start + rows, :]


def _conv_forward(p_conv, conv_w, conv_b, ln_g, ln_b, w_co, ts):
    s_len, d3 = p_conv.shape
    dm = d3 // 3

    def body(p_ref, cw_ref, cb_ref, g_ref, b_ref, w_ref, y_ref, u0_ref, u1_ref, win, shf):
        i = pl.program_id(0)

        @pl.when(i == 0)
        def _():
            win[0:HALO, :] = jnp.zeros((HALO, dm), F32)

        a, b, z = p_ref[:, 0:dm], p_ref[:, dm:2 * dm], p_ref[:, 2 * dm:3 * dm]
        u0 = a * _sig(b)
        win[HALO:HALO + ts, :] = u0
        u0_ref[...] = u0
        _shifted_copies(win, shf, ts)
        for r0 in range(0, ts, CONV_ROWS_FWD):
            acc = jnp.broadcast_to(cb_ref[...], (CONV_ROWS_FWD, dm))
            for k in range(CONV_K):
                acc = acc + cw_ref[k:k + 1, :] * _tap(win, shf, HALO - (CONV_K - 1) + k, r0, CONV_ROWS_FWD)
            u1_ref[r0:r0 + CONV_ROWS_FWD, :] = acc
        xh, _ = _layernorm_stats(u1_ref[...])
        u2 = xh * g_ref[...] + b_ref[...]
        a_conv = (u2 * _sig(u2)) * (z * _sig(z))
        y_ref[...] = _dot(a_conv.astype(BF16), w_ref[...])
        win[0:HALO, :] = win[ts:ts + HALO, :]

    row = pl.BlockSpec((1, dm), lambda i: (0, 0))
    tile = pl.BlockSpec((ts, dm), lambda i: (i, 0))
    return pl.pallas_call(
        body, name="conv_forward", grid=(s_len // ts,),
        in_specs=[pl.BlockSpec((ts, d3), lambda i: (i, 0)),
                  pl.BlockSpec((HALO, dm), lambda i: (0, 0)), row, row, row,
                  pl.BlockSpec((dm, dm), lambda i: (0, 0))],
        out_specs=[tile, tile, tile],
        out_shape=[jax.ShapeDtypeStruct((s_len, dm), F32)] * 3,
        scratch_shapes=[pltpu.VMEM((ts + HALO, dm), F32), pltpu.VMEM((7, ts + HALO - 8, dm), F32)],
        compiler_params=_params(("arbitrary",)),
    )(p_conv, conv_w, conv_b, ln_g, ln_b, w_co)


def _conv_backward(dp, dyc, p_conv, u0, u1, conv_w, ln_g, ln_b, w_co, ts):
    s_len, d3 = p_conv.shape
    dm = d3 // 3
    nt = s_len // ts
    hb = ts // HALO

    def body(dp_in, dy_ref, p_ref, u0_ref, uh_ref, u1_ref, cw_ref, g_ref, b_ref, w_ref,
             dp_ref, gw_ref, gs_ref, gcw_ref, dwin, uwin, shf):
        del dp_in
        i = pl.program_id(0)
        ti = nt - 1 - i

        @pl.when(i == 0)
        def _():
            gw_ref[...] = jnp.zeros_like(gw_ref)
            gs_ref[...] = jnp.zeros_like(gs_ref)
            gcw_ref[...] = jnp.zeros_like(gcw_ref)
            dwin[ts:ts + HALO, :] = jnp.zeros((HALO, dm), F32)

        dy = dy_ref[...]
        z = p_ref[:, 2 * dm:3 * dm]
        d_ac = _dot_nt(dy, w_ref[...])
        xh, rstd = _layernorm_stats(u1_ref[...])
        u2 = xh * g_ref[...] + b_ref[...]
        sg2, sgz = _sig(u2), _sig(z)
        u3, sz = u2 * sg2, z * sgz
        gw_ref[...] += _dot_tn((u3 * sz).astype(BF16), dy)
        d_z = d_ac * u3 * _dsilu(z, sgz)
        d_u2 = d_ac * sz * _dsilu(u2, sg2)
        gs_ref[0:1, :] += jnp.sum(d_u2 * xh, axis=0, keepdims=True)
        gs_ref[1:2, :] += jnp.sum(d_u2, axis=0, keepdims=True)
        dxh = d_u2 * g_ref[...]
        d_u1 = rstd * (dxh - jnp.mean(dxh, axis=-1, keepdims=True)
                       - xh * jnp.mean(dxh * xh, axis=-1, keepdims=True))
        gs_ref[2:3, :] += jnp.sum(d_u1, axis=0, keepdims=True)
        dwin[0:ts, :] = d_u1
        uwin[0:HALO, :] = jnp.where(ti == 0, 0.0, uh_ref[...])
        uwin[HALO:HALO + ts, :] = u0_ref[...]
        dp_ref[:, 2 * dm:3 * dm] = d_z.astype(BF16)
        _shifted_copies(uwin, shf, ts)
        for k in range(CONV_K):
            part = jnp.zeros((CONV_ROWS_BWD, dm), F32)
            for r0 in range(0, ts, CONV_ROWS_BWD):
                part = part + dwin[r0:r0 + CONV_ROWS_BWD, :] * _tap(uwin, shf, HALO - (CONV_K - 1) + k, r0,
                                                                    CONV_ROWS_BWD)
            gcw_ref[k:k + 1, :] += jnp.sum(part, axis=0, keepdims=True)
        _shifted_copies(dwin, shf, ts)
        for r0 in range(0, ts, CONV_ROWS_BWD):
            d_u0 = jnp.zeros((CONV_ROWS_BWD, dm), F32)
            for k in range(CONV_K):
                d_u0 = d_u0 + cw_ref[k:k + 1, :] * _tap(dwin, shf, CONV_K - 1 - k, r0, CONV_ROWS_BWD)
            rows = slice(r0, r0 + CONV_ROWS_BWD)
            sgb = _sig(p_ref[rows, dm:2 * dm])
            dp_ref[rows, 0:dm] = (d_u0 * sgb).astype(BF16)
            dp_ref[rows, dm:2 * dm] = (d_u0 * p_ref[rows, 0:dm] * sgb * (1.0 - sgb)).astype(BF16)
        dwin[ts:ts + HALO, :] = dwin[0:HALO, :]

    rev = lambda i: (nt - 1 - i, 0)
    row = pl.BlockSpec((1, dm), lambda i: (0, 0))
    tile = pl.BlockSpec((ts, dm), rev)
    return pl.pallas_call(
        body, name="conv_backward", grid=(nt,),
        in_specs=[ANY, tile, pl.BlockSpec((ts, d3), rev), tile,
                  pl.BlockSpec((HALO, dm), lambda i: (jnp.maximum((nt - 1 - i) * hb - 1, 0), 0)),
                  tile, pl.BlockSpec((HALO, dm), lambda i: (0, 0)), row, row,
                  pl.BlockSpec((dm, dm), lambda i: (0, 0))],
        out_specs=[pl.BlockSpec((ts, d3), rev), pl.BlockSpec((dm, dm), lambda i: (0, 0)),
                   pl.BlockSpec((8, dm), lambda i: (0, 0)), pl.BlockSpec((HALO, dm), lambda i: (0, 0))],
        out_shape=[jax.ShapeDtypeStruct(dp.shape, BF16), jax.ShapeDtypeStruct((dm, dm), F32),
                   jax.ShapeDtypeStruct((8, dm), F32), jax.ShapeDtypeStruct((HALO, dm), F32)],
        input_output_aliases={0: 0},
        scratch_shapes=[pltpu.VMEM((ts + HALO, dm), F32), pltpu.VMEM((ts + HALO, dm), F32),
                        pltpu.VMEM((7, ts + HALO - 8, dm), F32)],
        compiler_params=_params(("arbitrary",)),
    )(dp, dyc, p_conv, u0, u0, u1, conv_w, ln_g, ln_b, w_co)


def _att_masks():
    head0 = lax.broadcasted_iota(jnp.int32, (BLK, LANES), 1) < 64
    col = lax.broadcasted_iota(jnp.int32, (BLK, 4 * BLK), 1)
    row = lax.broadcasted_iota(jnp.int32, (BLK, 4 * BLK), 0)
    kj = col % BLK
    prev = jnp.where(col < 2 * BLK, 1, 0)
    band = jnp.where(col < 2 * BLK, kj - row, row - kj)
    return head0, band, prev


def _fill_block_diagonal(dst, slab, src_ref, halo_ref, head0, nq):
    sl = slice(slab * LANES, (slab + 1) * LANES)
    for b in range(nq + 1):
        blk = halo_ref[:, sl] if b == 0 else src_ref[(b - 1) * BLK:b * BLK, sl]
        base = (slab * (nq + 1) + b) * 2 * BLK
        zero = jnp.zeros_like(blk)
        dst[base:base + BLK, :] = jnp.where(head0, blk, zero)
        dst[base + BLK:base + 2 * BLK, :] = jnp.where(head0, zero, blk)


def _attention_forward(qkv, seq_len, qt, name):
    s_len = qkv.shape[1]
    nq = qt // BLK
    tiles_per_seq = seq_len // qt

    def body(q_ref, k_ref, v_ref, kh_ref, vh_ref, o_ref, lse_ref, kbd, vbd):
        first = jnp.where((pl.program_id(0) % tiles_per_seq) == 0, 4 * BLK, 0)
        head0, band, prev = _att_masks()
        band_first = band - prev * first
        for p in range(4):
            _fill_block_diagonal(kbd, p, k_ref, kh_ref, head0, nq)
            _fill_block_diagonal(vbd, p, v_ref, vh_ref, head0, nq)
        units = [(p, n) for p in range(4) for n in range(nq)]

        def keys_of(p, n):
            base = (p * (nq + 1) + n) * 2 * BLK
            return slice(base, base + 4 * BLK)

        def scores(p, n):
            q2 = q_ref[n * BLK:(n + 1) * BLK, p * LANES:(p + 1) * LANES] * 0.125
            return _dot_nt(q2, kbd[keys_of(p, n), :])

        def finish(p, n, o, den, lse):
            rows, sl = slice(n * BLK, (n + 1) * BLK), slice(p * LANES, (p + 1) * LANES)
            o_ref[rows, sl] = o / jnp.where(head0, den[0], den[1])
            lse_ref[rows, sl] = jnp.where(head0, lse[0], lse[1])

        s_next = scores(*units[0])
        pending = None
        for ui, (p, n) in enumerate(units):
            s = s_next
            if ui + 1 < len(units):
                s_next = scores(*units[ui + 1])
            s = jnp.where((band_first if n == 0 else band) >= 0, s, NEG)
            grp = [s[:, g * BLK:(g + 1) * BLK] for g in range(4)]
            ps, den, lse = [None] * 4, [], []
            for h in range(2):
                m = jnp.max(jnp.maximum(grp[h], grp[2 + h]), axis=-1, keepdims=True)
                ps[h], ps[2 + h] = jnp.exp(grp[h] - m), jnp.exp(grp[2 + h] - m)
                dn = jnp.sum(ps[h] + ps[2 + h], axis=-1, keepdims=True)
                den.append(dn)
                lse.append(m + jnp.log(dn))
            pmat = jnp.concatenate([x.astype(BF16) for x in ps], axis=1)
            o = _dot(pmat, vbd[keys_of(p, n), :])
            if pending is not None:
                finish(*pending)
            pending = (p, n, o, den, lse)
        finish(*pending)

    def which(w):
        return pl.BlockSpec((None, qt, ATT_W), lambda i: (w, i, 0))

    def halo(w):
        return pl.BlockSpec((None, BLK, ATT_W), lambda i: (w, jnp.maximum(i * nq - 1, 0), 0))

    out = pl.BlockSpec((qt, ATT_W), lambda i: (i, 0))
    bd = pltpu.VMEM((4 * (nq + 1) * 2 * BLK, LANES), BF16)
    return pl.pallas_call(
        body, name=name, grid=(s_len // qt,),
        in_specs=[which(0), which(1), which(2), halo(1), halo(2)],
        out_specs=[out, out], out_shape=[jax.ShapeDtypeStruct((s_len, ATT_W), F32)] * 2,
        scratch_shapes=[bd, bd],
        compiler_params=_params(("parallel",)),
    )(qkv, qkv, qkv, qkv, qkv)


def _attention_backward(qkv, d_att, lse, delta, seq_len, qt, name):
    s_len = qkv.shape[1]
    nq = qt // BLK
    tiles_per_seq = seq_len // qt
    nblk = s_len // BLK

    def body(q_ref, k_ref, v_ref, kh_ref, vh_ref, do_ref, lse_ref, dl_ref,
             qn_ref, don_ref, lsen_ref, dln_ref, dqkv_ref, kbd, vbd):
        i = pl.program_id(0)
        first = jnp.where((i % tiles_per_seq) == 0, 4 * BLK, 0)
        last = jnp.where((i % tiles_per_seq) == tiles_per_seq - 1, 4 * BLK, 0)
        head0, band, prev = _att_masks()
        band_first = band - prev * first
        band_tail = band[:, 0:2 * BLK] - last
        for p in range(4):
            _fill_block_diagonal(kbd, p, k_ref, kh_ref, head0, nq)
            _fill_block_diagonal(vbd, p, v_ref, vh_ref, head0, nq)
        units = [(p, n) for p in range(4) for n in range(nq + 1)]

        def stage_a(p, n):
            sl = slice(p * LANES, (p + 1) * LANES)
            base = (p * (nq + 1) + n) * 2 * BLK
            if n < nq:
                rows = slice(n * BLK, (n + 1) * BLK)
                q2, do2, lse2, dl2 = q_ref[rows, sl], do_ref[rows, sl], lse_ref[rows, sl], dl_ref[rows, sl]
                keys = slice(base, base + 4 * BLK)
            else:
                q2, do2, lse2, dl2 = qn_ref[:, sl], don_ref[:, sl], lsen_ref[:, sl], dln_ref[:, sl]
                keys = slice(base, base + 2 * BLK)
            s = _dot_nt(q2 * 0.125, kbd[keys, :])
            dp = _dot_nt(do2, vbd[keys, :])
            return q2, do2, lse2, dl2, keys, s, dp

        def stage_b(n, lse2, dl2, s, dp):
            mask = band_tail if n == nq else (band_first if n == 0 else band)
            ps, dss = [], []
            for g in range(s.shape[1] // BLK):
                h = g % 2
                cols = slice(g * BLK, (g + 1) * BLK)
                pg = jnp.exp(jnp.where(mask[:, cols] >= 0, s[:, cols] - lse2[:, h * 64:h * 64 + 1], NEG))
                ps.append(pg.astype(BF16))
                dss.append((pg * (dp[:, cols] - dl2[:, h * 64:h * 64 + 1]) * 0.125).astype(BF16))
            return jnp.concatenate(ps, axis=1), jnp.concatenate(dss, axis=1)

        def heads(r, g):
            return jnp.where(head0, r[g * BLK:(g + 1) * BLK, :], r[(g + 1) * BLK:(g + 2) * BLK, :])

        a_next = stage_a(*units[0])
        carry = None
        for ui, (p, n) in enumerate(units):
            q2, do2, lse2, dl2, keys, s, dp = a_next
            if ui + 1 < len(units):
                a_next = stage_a(*units[ui + 1])
            pmat, dsmat = stage_b(n, lse2, dl2, s, dp)
            sl = slice(p * LANES, (p + 1) * LANES)
            if n < nq:
                dqkv_ref[0, n * BLK:(n + 1) * BLK, sl] = _dot(dsmat, kbd[keys, :])
            dkbd = _dot_tn(dsmat, q2)
            dvbd = _dot_tn(pmat, do2)
            if n > 0:
                prow = slice((n - 1) * BLK, n * BLK)
                dqkv_ref[1, prow, sl] = carry[0] + heads(dkbd, 0)
                dqkv_ref[2, prow, sl] = carry[1] + heads(dvbd, 0)
            if n < nq:
                carry = (heads(dkbd, 2), heads(dvbd, 2))

    def which(w):
        return pl.BlockSpec((None, qt, ATT_W), lambda i: (w, i, 0))

    def prev(w):
        return pl.BlockSpec((None, BLK, ATT_W), lambda i: (w, jnp.maximum(i * nq - 1, 0), 0))

    tile = pl.BlockSpec((qt, ATT_W), lambda i: (i, 0))
    nxt = pl.BlockSpec((BLK, ATT_W), lambda i: (jnp.minimum((i + 1) * nq, nblk - 1), 0))
    nxt_q = pl.BlockSpec((None, BLK, ATT_W), lambda i: (0, jnp.minimum((i + 1) * nq, nblk - 1), 0))
    return pl.pallas_call(
        body, name=name, grid=(s_len // qt,),
        in_specs=[which(0), which(1), which(2), prev(1), prev(2), tile, tile, tile, nxt_q, nxt, nxt, nxt],
        out_specs=pl.BlockSpec((3, qt, ATT_W), lambda i: (0, i, 0)),
        out_shape=jax.ShapeDtypeStruct((3, s_len, ATT_W), F32),
        scratch_shapes=[pltpu.VMEM((4 * (nq + 1) * 2 * BLK, LANES), BF16)] * 2,
        compiler_params=_params(("parallel",)),
    )(qkv, qkv, qkv, qkv, qkv, d_att, lse, delta, qkv, d_att, lse, delta)


def _merge_and_head(o_g, lse_g, p_gate, y_conv, x, tgt, modv, final_g, w_ao, w_o, ts):
    s_len, dm = x.shape
    gw = ATT_W + 2 * dm
    nt = s_len // ts
    gate_off = 3 * dm + 9 * ATT_W
    assert gate_off % gw == 0

    def body(o0, o1, o2, l0, l1, l2, pg_ref, yc_ref, x_ref, t_ref, mod_ref, fg_ref, wao_ref, wo_ref,
             loss_ref, dx_ref, dyc_ref, dpg_ref, da0, da1, da2, ls0, ls1, ls2, de0, de1, de2,
             gwo_ref, gwao_ref, gs_ref, slab):
        i = pl.program_id(0)

        @pl.when(i == 0)
        def _():
            loss_ref[...] = jnp.zeros_like(loss_ref)
            gwo_ref[...] = jnp.zeros_like(gwo_ref)
            gwao_ref[...] = jnp.zeros_like(gwao_ref)
            gs_ref[...] = jnp.zeros_like(gs_ref)

        os_, ls_ = [], []
        for dil, o_ref, l_ref in zip(DILATIONS, (o0, o1, o2), (l0, l1, l2)):
            os_.append(jnp.concatenate(_interleave_load(o_ref, slab, dil, ts), axis=1))
            ls_.append(jnp.concatenate(_interleave_load(l_ref, slab, dil, ts), axis=1))
        mx = jnp.maximum(jnp.maximum(ls_[0], ls_[1]), ls_[2])
        wts = [jnp.exp(l - mx) for l in ls_]
        wsum = wts[0] + wts[1] + wts[2]
        att = (wts[0] * os_[0] + wts[1] * os_[1] + wts[2] * os_[2]) / wsum
        lse_all = mx + jnp.log(wsum)

        z_att, g_conv, g_att = pg_ref[:, 0:ATT_W], pg_ref[:, ATT_W:ATT_W + dm], pg_ref[:, ATT_W + dm:gw]
        sgz = _sig(z_att)
        sz = z_att * sgz
        a_att = (att * sz).astype(BF16)
        y_att = _dot(a_att, wao_ref[...])
        y_conv = yc_ref[...]
        sgc, sga = _sig(g_conv), _sig(g_att)
        merged = (sgc * y_conv + sga * y_att).astype(BF16)
        mo = _dot(merged, wo_ref[...])
        gate = mod_ref[2:3, :]
        x2 = x_ref[...] + gate * mo
        r = lax.rsqrt(jnp.mean(x2 * x2, axis=-1, keepdims=True) + EPS)
        xr = x2 * r
        err = xr * fg_ref[...] - t_ref[...]
        loss_ref[...] += 0.5 * jnp.sum(jnp.mean(err * err, axis=-1, keepdims=True))
        dy = err * (1.0 / dm)
        gs_ref[0:1, :] += jnp.sum(dy * xr, axis=0, keepdims=True)
        dyg = dy * fg_ref[...]
        d_x2 = r * dyg - xr * (r * jnp.mean(dyg * xr, axis=-1, keepdims=True))
        dx_ref[...] = d_x2
        gs_ref[1:2, :] += jnp.sum(d_x2 * mo, axis=0, keepdims=True)
        d_mo = (d_x2 * gate).astype(BF16)
        d_mg = _dot_nt(d_mo, wo_ref[...])
        gwo_ref[...] += _dot_tn(merged, d_mo)
        dyc_ref[...] = (d_mg * sgc).astype(BF16)
        dpg_ref[:, ATT_W:ATT_W + dm] = (d_mg * y_conv * sgc * (1.0 - sgc)).astype(BF16)
        d_ya = (d_mg * sga).astype(BF16)
        dpg_ref[:, ATT_W + dm:gw] = (d_mg * y_att * sga * (1.0 - sga)).astype(BF16)
        gwao_ref[...] += _dot_tn(a_att, d_ya)
        d_aa = _dot_nt(d_ya, wao_ref[...])
        dpg_ref[:, 0:ATT_W] = (d_aa * att * _dsilu(z_att, sgz)).astype(BF16)
        d_att = d_aa * sz
        ri = lax.broadcasted_iota(jnp.int32, (ATT_W, ATT_W), 0) // 64
        ci = lax.broadcasted_iota(jnp.int32, (ATT_W, ATT_W), 1) // 64
        ones = jnp.where(ri == ci, 1.0, 0.0).astype(BF16)
        prod = d_att * att
        hi = prod.astype(BF16)
        lo = (prod - hi.astype(F32)).astype(BF16)
        delta = _dot(hi, ones) + _dot(lo, ones)
        for val, refs, dt in ((d_att, (da0, da1, da2), BF16), (lse_all, (ls0, ls1, ls2), F32),
                              (delta, (de0, de1, de2), F32)):
            vals = [val[:, s * LANES:(s + 1) * LANES] for s in range(4)]
            for dil, ref in zip(DILATIONS, refs):
                _deinterleave_store(vals, slab, ref, dil, ts, dt)

    def grp(dil):
        return pl.BlockSpec((dil, ts // dil, ATT_W), lambda i: (0, i, 0))

    tile = pl.BlockSpec((ts, dm), lambda i: (i, 0))
    gate_tile = pl.BlockSpec((ts, gw), lambda i: (i, 0))
    const = lambda shp: pl.BlockSpec(shp, lambda i: tuple(0 for _ in shp))
    grp_shape = lambda dt: [jax.ShapeDtypeStruct((dil, s_len // dil, ATT_W), dt) for dil in DILATIONS]
    return pl.pallas_call(
        body, name="merge_and_head", grid=(nt,),
        in_specs=[grp(d) for d in DILATIONS] * 2
        + [gate_tile, tile, tile, tile, const((8, dm)), const((1, dm)), const((ATT_W, dm)), const((dm, dm))],
        out_specs=[const((8, LANES)), tile, tile, pl.BlockSpec((ts, gw), lambda i: (i, gate_off // gw))]
        + [grp(d) for d in DILATIONS] * 3
        + [const((dm, dm)), const((ATT_W, dm)), const((8, dm))],
        out_shape=[jax.ShapeDtypeStruct((8, LANES), F32), jax.ShapeDtypeStruct((s_len, dm), F32),
                   jax.ShapeDtypeStruct((s_len, dm), BF16), jax.ShapeDtypeStruct((s_len, gate_off + gw), BF16)]
        + grp_shape(BF16) + grp_shape(F32) + grp_shape(F32)
        + [jax.ShapeDtypeStruct((dm, dm), F32), jax.ShapeDtypeStruct((ATT_W, dm), F32),
           jax.ShapeDtypeStruct((8, dm), F32)],
        scratch_shapes=[pltpu.VMEM((4, ts, LANES), F32)],
        compiler_params=_params(("arbitrary",)),
    )(*o_g, *lse_g, p_gate, y_conv, x, tgt, modv, final_g, w_ao, w_o)


def _qkv_grad_to_tokens(dp, dqkv_g, rope, ts):
    s_len, dm = rope.shape[1], (dp.shape[1] - 10 * ATT_W) // 5
    qw = 3 * ATT_W
    assert (3 * dm) % qw == 0

    def body(dp_in, g0, g1, g2, rope_ref, o_ref, slab):
        del dp_in
        w = pl.program_id(1)
        rc, ra, rb = rope_ref.at[0], rope_ref.at[1], rope_ref.at[2]

        def emit(roped):
            for gi, (dil, g_ref) in enumerate(zip(DILATIONS, (g0, g1, g2))):
                vals = _interleave_load(g_ref, slab, dil, ts)
                for s in range(4):
                    t = vals[s]
                    if roped:
                        t = t * rc[...] + pltpu.roll(t * ra[...], 8, 1) + pltpu.roll(t * rb[...], LANES - 8, 1)
                    col = gi * ATT_W + s * LANES
                    o_ref[:, col:col + LANES] = t.astype(BF16)

        pl.when(w < 2)(lambda: emit(True))
        pl.when(w == 2)(lambda: emit(False))

    return pl.pallas_call(
        body, name="qkv_grad_to_tokens", grid=(s_len // ts, 3),
        in_specs=[ANY] + [pl.BlockSpec((None, dil, ts // dil, ATT_W), lambda i, w: (w, 0, i, 0)) for dil in DILATIONS]
        + [pl.BlockSpec((3, ts, LANES), lambda i, w: (0, i, 0))],
        out_specs=pl.BlockSpec((ts, qw), lambda i, w: (i, 3 * dm // qw + w)),
        out_shape=jax.ShapeDtypeStruct(dp.shape, BF16),
        input_output_aliases={0: 0},
        scratch_shapes=[pltpu.VMEM((4, ts, LANES), F32)],
        compiler_params=_params(("parallel", "arbitrary")),
    )(dp, *dqkv_g, rope)


def _wide_col_tile(cols):
    for width in (5 * COL_TILE, 2 * COL_TILE):
        if cols % width == 0:
            return width
    return COL_TILE


def _input_grad(dp, w_in, x, dx_res, modv, norm_g, ts):
    s_len, dm = x.shape
    ct = next(w * COL_TILE for w in (4, 2, 1) if dp.shape[1] % (w * COL_TILE) == 0)
    nct = dp.shape[1] // ct

    def body(p_ref, w_ref, x_ref, dxr_ref, mod_ref, g_ref, gx_ref, gs_ref, acc):
        i, j = pl.program_id(0), pl.program_id(1)

        @pl.when((i == 0) & (j == 0))
        def _():
            gs_ref[...] = jnp.zeros_like(gs_ref)

        @pl.when(j == 0)
        def _():
            acc[...] = jnp.zeros_like(acc)

        acc[...] += _dot_nt(p_ref[...], w_ref[...])

        @pl.when(j == nct - 1)
        def _():
            d_h = acc[...]
            xv = x_ref[...]
            r = lax.rsqrt(jnp.mean(xv * xv, axis=-1, keepdims=True) + EPS)
            xr = xv * r
            gs_ref[0:1, :] += jnp.sum(d_h, axis=0, keepdims=True)
            gs_ref[1:2, :] += jnp.sum(d_h * (xr * g_ref[...]), axis=0, keepdims=True)
            d_n = d_h * (1.0 + mod_ref[1:2, :])
            gs_ref[2:3, :] += jnp.sum(d_n * xr, axis=0, keepdims=True)
            dxn = d_n * g_ref[...]
            gx_ref[...] = dxr_ref[...] + r * dxn - xr * (r * jnp.mean(dxn * xr, axis=-1, keepdims=True))

    tile = pl.BlockSpec((ts, dm), lambda i, j: (i, 0))
    return pl.pallas_call(
        body, name="input_grad", grid=(s_len // ts, nct),
        in_specs=[pl.BlockSpec((ts, ct), lambda i, j: (i, j)), pl.BlockSpec((dm, ct), lambda i, j: (0, j)), tile, tile,
                  pl.BlockSpec((8, dm), lambda i, j: (0, 0)), pl.BlockSpec((1, dm), lambda i, j: (0, 0))],
        out_specs=[tile, pl.BlockSpec((8, dm), lambda i, j: (0, 0))],
        out_shape=[jax.ShapeDtypeStruct((s_len, dm), F32), jax.ShapeDtypeStruct((8, dm), F32)],
        scratch_shapes=[pltpu.VMEM((ts, dm), F32)],
        compiler_params=_params(("arbitrary", "arbitrary")),
    )(dp, w_in, x, dx_res, modv, norm_g)


def _w_in_grad(h_t, dp, ts):
    dm, s_len = h_t.shape
    ct = _wide_col_tile(dp.shape[1])

    def body(h_ref, p_ref, o_ref):
        @pl.when(pl.program_id(1) == 0)
        def _():
            o_ref[...] = jnp.zeros_like(o_ref)

        o_ref[...] += _dot(h_ref[...], p_ref[...])

    return pl.pallas_call(
        body, name="w_in_grad", grid=(dp.shape[1] // ct, s_len // ts),
        in_specs=[pl.BlockSpec((dm, ts), lambda j, i: (0, i)), pl.BlockSpec((ts, ct), lambda j, i: (i, j))],
        out_specs=pl.BlockSpec((dm, ct), lambda j, i: (0, j)),
        out_shape=jax.ShapeDtypeStruct((dm, dp.shape[1]), F32),
        compiler_params=_params(("arbitrary", "arbitrary")),
    )(h_t, dp)


def _rope_lane_table():
    l64 = jnp.arange(LANES) % 64
    half = ROT_DIM // 2
    inv_freq = ROPE_THETA ** (-(jnp.arange(half, dtype=F32) * 2.0 / ROT_DIM))
    rot = l64 < ROT_DIM
    rows = [jnp.where(rot, inv_freq[l64 % half], 0.0), (l64 < half).astype(F32),
            ((l64 >= half) & rot).astype(F32), rot.astype(F32)]
    return jnp.concatenate([jnp.stack(rows), jnp.zeros((4, LANES), F32)], axis=0)


def _tile_sizes(s_len):
    ts_big = min(1024, s_len // 2)
    ts_mid = 256
    ts_head = 256
    qt = [min(1024, s_len // dil) for dil in DILATIONS]
    return ts_big, ts_mid, ts_head, qt


def kernel(x, c, positions, norm_g, w_ada, b_ada, w_in, conv_w, conv_b, conv_ln_g, conv_ln_b, w_conv_out, w_att_out, w_o, final_g, loss_target, m_norm_g, m_w_ada, m_b_ada, m_w_in, m_conv_w, m_conv_b, m_conv_ln_g, m_conv_ln_b, m_w_conv_out, m_w_att_out, m_w_o, m_final_g, v_norm_g, v_w_ada, v_b_ada, v_w_in, v_conv_w, v_conv_b, v_conv_ln_g, v_conv_ln_b, v_w_conv_out, v_w_att_out, v_w_o, v_final_g):
    s_len, dm = x.shape[1], x.shape[2]
    ts_big, ts_mid, ts_head, qt = _tile_sizes(s_len)
    xi, yi, cidx = _my_place()
    chip = 2 * xi + yi
    batch = 4 * xi + 2 * yi + cidx
    x2d, tgt = x[0], loss_target[0]
    pos = positions.reshape(s_len, 1)
    wa_l, wi_l, cw_l = w_ada[0], w_in[0], conv_w[0]
    wco_l, wao_l, wo_l = w_conv_out[0], w_att_out[0], w_o[0]
    ada_w = wa_l.shape[1]
    cw_cols = cw_l.shape[1]

    kinds = ("col", "row", "col", "row")
    w_bufs = _cast_weights([wi_l, wco_l, wao_l, wo_l], kinds)

    cw_pad = jnp.pad(cw_l, ((0, HALO - CONV_K), (0, 0)))
    small_in = jnp.concatenate([jnp.broadcast_to(c, (8, dm)), cw_pad.reshape(8, dm)], axis=0)
    small = _allgather_small(small_in).reshape(8, 16, dm)
    c_all = small[:, 0, :]
    conv_w_full = jnp.concatenate(
        [small[2 * p, 8:16, :].reshape(HALO, cw_cols) for p in range(4)], axis=1)
    b_l = lax.dynamic_slice(b_ada, (0, chip * ada_w), (1, ada_w))
    mod_parts = _allgather_small(_mod_part(c_all, wa_l, b_l)).reshape(8, 8, ada_w)
    mod_rows = lax.dynamic_index_in_dim(mod_parts, batch, axis=1, keepdims=False)
    mod = jnp.concatenate([mod_rows[2 * p] for p in range(4)], axis=0).reshape(3, dm)
    modv = jnp.concatenate([mod, jnp.zeros((5, dm), F32)], axis=0)

    sems_a, sems_b, w_bufs, token = _gather_weights_start(w_bufs, kinds, modv)
    rope = _rope_coefficients(pos, _rope_lane_table() + token, ts_big)
    h_b, h_t = _norm_modulate(x2d, modv + token[0, 0], norm_g, ts_big)
    (w_in_b,) = _gather_weights_wait(w_bufs, kinds, sems_a, 0, (rope, h_b), "gather_weights_wait_w_in")
    w_in_b = _pass_to_sibling(w_in_b, kinds[0])

    n_conv, n_gate = 3 * dm // COL_TILE, (ATT_W + 2 * dm) // COL_TILE
    p_conv = _plain_projection(h_b, w_in_b, 0, n_conv, ts_big, "conv_projection")
    p_gate = _plain_projection(h_b, w_in_b, n_conv + 9, n_gate, ts_big, "gate_projection")
    qkv_g = [_qkv_projection(h_b, rope, w_in_b, gi, ts_big) for gi in range(3)]
    w_co_b, w_ao_b, w_o_b = _gather_weights_wait(w_bufs, kinds, sems_b, 1, (qkv_g[2], p_conv, p_gate),
                                                  "gather_weights_wait_rest")
    y_conv, u0, u1 = _conv_forward(p_conv, conv_w_full, conv_b, conv_ln_g, conv_ln_b, w_co_b, ts_mid)
    qkv_flat = [q.reshape(3, s_len, ATT_W) for q in qkv_g]
    o_g, lse_g = [], []
    for gi, dil in enumerate(DILATIONS):
        o, l = _attention_forward(qkv_flat[gi], s_len // dil, qt[gi], "attention_forward_%d" % dil)
        o_g.append(o.reshape(dil, s_len // dil, ATT_W))
        lse_g.append(l.reshape(dil, s_len // dil, ATT_W))

    (loss_p, dx_res, dyc, dp, da0, da1, da2, ls0, ls1, ls2, de0, de1, de2,
     g_wo, g_wao, head_sums) = _merge_and_head(o_g, lse_g, p_gate, y_conv, x2d, tgt, modv,
                                               final_g.reshape(1, dm), w_ao_b, w_o_b, ts_head)
    loss = lax.psum(loss_p[0, 0], ("x", "y", "c"))

    dp, g_wco, conv_sums, g_cw = _conv_backward(dp, dyc, p_conv, u0, u1, conv_w_full, conv_ln_g, conv_ln_b,
                                                w_co_b, ts_mid)
    dqkv_g = []
    for gi, (dil, da, ls, de) in enumerate(zip(DILATIONS, (da0, da1, da2), (ls0, ls1, ls2), (de0, de1, de2))):
        flat = lambda a: a.reshape(s_len, ATT_W)
        dqkv = _attention_backward(qkv_flat[gi], flat(da), flat(ls), flat(de), s_len // dil, qt[gi],
                                   "attention_backward_%d" % dil)
        dqkv_g.append(dqkv.reshape(3, dil, s_len // dil, ATT_W))
    dp = _qkv_grad_to_tokens(dp, dqkv_g, rope, 2 * ts_mid)
    g_win = _w_in_grad(h_t, dp, ts_big)

    grads = [g_win, g_wco, g_wao, g_wo]
    full_shapes = [g.shape for g in grads]
    c_arr = jnp.reshape(cidx, (1,)).astype(jnp.int32)
    recv_halves = _reduce_pair_exchange(grads, kinds)
    halves = [_reduce_pair_sum(g, pa, kind, c_arr, "reduce_pair_sum_%d" % k)
              for k, (g, pa, kind) in enumerate(zip(grads, recv_halves, kinds))]
    send_sems, recv_sems, halves, lands, token = _reduce_to_owner_start(halves, kinds, full_shapes)
    grad_x, in_sums = _input_grad(dp, w_in_b, x2d, dx_res, modv + token[0, 0], norm_g, ts_big)
    halves, recvd = _reduce_to_owner_wait(send_sems, recv_sems, halves, lands, in_sums, kinds)
    gr_win, gr_wco, gr_wao, gr_wo = _reduce_finish(halves, recvd, kinds, full_shapes)

    rows = [in_sums[2:3], conv_sums[2:3], conv_sums[0:1], conv_sums[1:2], head_sums[0:1],
            in_sums[0:1], in_sums[1:2], head_sums[1:2], g_cw, jnp.zeros((8, dm), F32)]
    part = jnp.concatenate(rows, axis=0)
    gathered = _allgather_small(part)
    tot = _sum_devices(gathered, 48)
    dmod_all = gathered.reshape(8, 48, dm)[:, 5:8, :].reshape(8, 3 * dm)
    dmod_l = lax.dynamic_slice(dmod_all, (0, chip * ada_w), (8, ada_w))
    gr_wada = _grad_w_ada(c_all.T, dmod_l)
    gr_cw = lax.dynamic_slice(tot[8:8 + HALO], (0, chip * cw_cols), (HALO, cw_cols))

    pad_cw = lambda a: jnp.pad(a[0], ((0, HALO - CONV_K), (0, 0)))
    row = lambda a: a.reshape(1, dm)
    small_upd = _adamw_small(
        [norm_g, conv_b, conv_ln_g, conv_ln_b, row(final_g), b_ada], tot[0:8],
        [m_norm_g, m_conv_b, m_conv_ln_g, m_conv_ln_b, row(m_final_g), m_b_ada],
        [v_norm_g, v_conv_b, v_conv_ln_g, v_conv_ln_b, row(v_final_g), v_b_ada])
    upd = {
        "w_ada": _adamw(wa_l, gr_wada, m_w_ada[0], v_w_ada[0], "adamw_w_ada"),
        "w_in": _adamw(wi_l, gr_win, m_w_in[0], v_w_in[0], "adamw_w_in"),
        "conv_w": _adamw(cw_pad, gr_cw, pad_cw(m_conv_w), pad_cw(v_conv_w), "adamw_conv_w"),
        "w_co": _adamw(wco_l, gr_wco, m_w_conv_out[0], v_w_conv_out[0], "adamw_w_conv_out"),
        "w_ao": _adamw(wao_l, gr_wao, m_w_att_out[0], v_w_att_out[0], "adamw_w_att_out"),
        "w_o": _adamw(wo_l, gr_wo, m_w_o[0], v_w_o[0], "adamw_w_o"),
    }

    def family(which):
        if which is None:
            sm = small_upd[0:6]
            big = {"w_ada": gr_wada, "w_in": gr_win, "conv_w": gr_cw, "w_co": gr_wco, "w_ao": gr_wao, "w_o": gr_wo}
        else:
            sm = small_upd[6 * (which + 1):6 * (which + 2)]
            big = {k: upd[k][which] for k in ("w_ada", "w_in", "conv_w", "w_co", "w_ao", "w_o")}
        return [sm[0], big["w_ada"][None], sm[5], big["w_in"][None],
                big["conv_w"][None, :CONV_K], sm[1], sm[2], sm[3], big["w_co"][None],
                big["w_ao"][None], big["w_o"][None], sm[4].reshape(dm)]

    return (loss, grad_x[None], *family(None), *family(0), *family(1), *family(2))
```

```python
import jax
import jax.numpy as jnp
from jax import lax
from jax.experimental import pallas as pl
from jax.experimental.pallas import tpu as pltpu

F32 = jnp.float32
BF16 = jnp.bfloat16
MESH = pl.DeviceIdType.MESH
ANY = pl.BlockSpec(memory_space=pl.ANY)
VM = pl.BlockSpec(memory_space=pltpu.VMEM)

EPS = 1e-6
NEG = -1e30
ATT_W = 512
DILATIONS = (1, 4, 16)
BLK = 128
CONV_K = 31
HALO = 32
CONV_ROWS_FWD = 32
CONV_ROWS_BWD = 16
ROT_DIM = 16
ROPE_THETA = 500000.0
COL_TILE = 512
LANES = 128
VMEM_LIMIT = 56 * 1024 * 1024

ADAM_LR, ADAM_B1, ADAM_B2, ADAM_EPS, ADAM_WD, ADAM_STEP = 0.001, 0.9, 0.999, 1e-08, 0.01, 10


def _params(sem=None, vmem=VMEM_LIMIT):
    return pltpu.CompilerParams(dimension_semantics=sem, vmem_limit_bytes=vmem)


def _dot(a, b):
    return jnp.dot(a, b, preferred_element_type=F32)


def _dot_nt(a, b):
    return lax.dot_general(a, b, (((1,), (1,)), ((), ())), preferred_element_type=F32)


def _dot_tn(a, b):
    return lax.dot_general(a, b, (((0,), (0,)), ((), ())), preferred_element_type=F32)


def _sig(x):
    return jax.nn.sigmoid(x)


def _dsilu(x, s):
    return s * (1.0 + x * (1.0 - s))


def _my_place():
    return lax.axis_index("x"), lax.axis_index("y"), lax.axis_index("c")


def _allgather_small(x_shard):
    m_per, n = x_shard.shape

    def body(x_ref, out_ref, send_sems, recv_sems, local_sem):
        x, y, c = _my_place()
        me, sibling = (x, y, c), (x, y, 1 - c)
        chips = [(1 - x, y), (x, 1 - y), (1 - x, 1 - y)]

        def rows(px, py, pc):
            return out_ref.at[pl.ds((4 * px + 2 * py + pc) * m_per, m_per), :]

        def copy(k, block, to, src=None):
            return pltpu.make_async_remote_copy(
                src_ref=rows(*block) if src is None else src, dst_ref=rows(*block),
                send_sem=send_sems.at[k], recv_sem=recv_sems.at[k],
                device_id=to, device_id_type=MESH)

        mine = pltpu.make_async_copy(x_ref, rows(*me), local_sem)
        mine.start()
        first = [copy(0, me, sibling, src=x_ref)]
        first += [copy(1 + j, me, (*chip, c), src=x_ref) for j, chip in enumerate(chips)]
        for cp in first:
            cp.start()
        passed = [copy(4 + j, (*chip, c), sibling) for j, chip in enumerate(chips)]
        for j, chip in enumerate(chips):
            copy(1 + j, (*chip, c), me).wait_recv()
            passed[j].start()
        copy(0, sibling, me).wait_recv()
        for j, chip in enumerate(chips):
            copy(4 + j, (*chip, 1 - c), me).wait_recv()
        for cp in first + passed:
            cp.wait_send()
        mine.wait()

    return pl.pallas_call(
        body, name="allgather_small",
        out_shape=jax.ShapeDtypeStruct((8 * m_per, n), x_shard.dtype),
        in_specs=[VM], out_specs=VM,
        scratch_shapes=[pltpu.SemaphoreType.DMA((7,)), pltpu.SemaphoreType.DMA((7,)),
                        pltpu.SemaphoreType.DMA],
    )(x_shard)


def _shard_window(ref, kind, p, n_shards=4):
    r, c = ref.shape
    if kind == "col":
        w = c // n_shards
        return ref.at[:, pl.ds(p * w, w)]
    w = r // n_shards
    return ref.at[pl.ds(p * w, w), :]


def _half_window(ref, kind, hc):
    r, c = ref.shape
    if kind == "col":
        return ref.at[pl.ds(hc * (r // 2), r // 2), :]
    return ref.at[:, pl.ds(hc * (c // 2), c // 2)]


def _landed(ref, kind, chip, hc):
    return _half_window(_shard_window(ref, kind, 2 * chip[0] + chip[1]), kind, hc)


def _cast_weights(shards, kinds):
    n = len(shards)
    full_shapes = [(s.shape[0], 4 * s.shape[1]) if kind == "col" else (4 * s.shape[0], s.shape[1])
                   for s, kind in zip(shards, kinds)]

    def body(*refs):
        w_refs, out_refs, bf_refs, sems = refs[:n], refs[n:2 * n], refs[2 * n:3 * n], refs[3 * n]
        x, y, _ = _my_place()
        cps = []
        for k in range(n):
            bf_refs[k][...] = w_refs[k][...].astype(BF16)
            cp = pltpu.make_async_copy(bf_refs[k], _shard_window(out_refs[k], kinds[k], 2 * x + y), sems.at[k])
            cp.start()
            cps.append(cp)
        for cp in cps:
            cp.wait()

    return pl.pallas_call(
        body, name="cast_weights",
        out_shape=[jax.ShapeDtypeStruct(s, BF16) for s in full_shapes],
        in_specs=[VM] * n, out_specs=[ANY] * n,
        scratch_shapes=[pltpu.VMEM(s.shape, BF16) for s in shards] + [pltpu.SemaphoreType.DMA((n,))],
        compiler_params=_params(),
    )(*shards)


def _gather_copies(refs, kinds, sems_a, sems_b):
    x, y, c = _my_place()
    chips = [(1 - x, y), (x, 1 - y), (1 - x, 1 - y)]
    set_a, set_b = [], []
    for j, chip in enumerate(chips):
        for k in range(len(refs)):
            if refs[k] is None:
                continue
            for flip in ((0,) if k == 0 else (0, 1)):
                win = _landed(refs[k], kinds[k], (x, y), c)
                sems, idx = (sems_a, j) if k == 0 else (sems_b, ((k - 1) * 3 + j) * 2 + flip)
                mine = _landed(refs[k], kinds[k], chip, (c + flip) % 2)
                (set_a if k == 0 else set_b).append((
                    pltpu.make_async_remote_copy(
                        src_ref=win, dst_ref=win, send_sem=sems[0].at[idx], recv_sem=sems[1].at[idx],
                        device_id=(*chip, (c + flip) % 2), device_id_type=MESH),
                    pltpu.make_async_remote_copy(
                        src_ref=mine, dst_ref=mine, send_sem=sems[0].at[idx], recv_sem=sems[1].at[idx],
                        device_id=(*chip, (c + flip) % 2), device_id_type=MESH)))
    return set_a, set_b


_SPLIT = dict(has_side_effects=pltpu.SideEffectType.DATAFLOW_SIDE_EFFECTING)


def _gather_weights_start(fulls, kinds, after):
    n = len(fulls)
    hbm = pl.BlockSpec(memory_space=pltpu.HBM)
    sem = pl.BlockSpec(memory_space=pltpu.SEMAPHORE)
    nb = (n - 1) * 6

    def body(*refs):
        in_refs = refs[:n]
        sa, ra, sb, rb = refs[n + 1:n + 5]
        token = refs[-1]
        set_a, set_b = _gather_copies(in_refs, kinds, (sa, ra), (sb, rb))
        for out_cp, _ in set_a + set_b:
            out_cp.start()
        token[...] = jnp.zeros_like(token)

    out = pl.pallas_call(
        body, name="gather_weights_start",
        out_shape=[pltpu.SemaphoreType.DMA((3,)), pltpu.SemaphoreType.DMA((3,)),
                   pltpu.SemaphoreType.DMA((nb,)), pltpu.SemaphoreType.DMA((nb,))]
        + [pltpu.HBM(f.shape, f.dtype) for f in fulls] + [jax.ShapeDtypeStruct((8, LANES), F32)],
        in_specs=[hbm] * n + [ANY], out_specs=[sem] * 4 + [hbm] * n + [VM],
        input_output_aliases={k: 4 + k for k in range(n)},
        compiler_params=pltpu.CompilerParams(**_SPLIT),
    )(*[pltpu.with_memory_space_constraint(f, pltpu.HBM) for f in fulls], after)
    return (out[0], out[1]), (out[2], out[3]), out[4:4 + n], out[-1]


def _gather_weights_wait(fulls, kinds, sems, which, after, name):
    n = len(fulls)
    hbm = pl.BlockSpec(memory_space=pltpu.HBM)
    sem = pl.BlockSpec(memory_space=pltpu.SEMAPHORE)
    keep = [0] if which == 0 else list(range(1, n))

    def body(*refs):
        m = len(keep)
        full_refs = [None] * n
        for pos_, k in enumerate(keep):
            full_refs[k] = refs[pos_]
        s_ref, r_ref = refs[m:m + 2]
        if which == 0:
            sets = _gather_copies([full_refs[0]], kinds[:1], (s_ref, r_ref), None)[0]
        else:
            sets = _gather_copies([None] + [full_refs[k] for k in keep], kinds, None, (s_ref, r_ref))[1]
        for out_cp, in_cp in sets:
            out_cp.wait_send()
            in_cp.wait_recv()

    out = pl.pallas_call(
        body, name=name,
        out_shape=[pltpu.HBM(fulls[k].shape, fulls[k].dtype) for k in keep],
        in_specs=[hbm] * len(keep) + [sem, sem] + [ANY] * len(after), out_specs=[hbm] * len(keep),
        input_output_aliases={i: i for i in range(len(keep))},
        compiler_params=pltpu.CompilerParams(**_SPLIT),
    )(*[fulls[k] for k in keep], sems[0], sems[1], *after)
    return list(out)


def _pass_to_sibling(w_full, kind):
    def body(w_in_ref, w_ref, send_sems, recv_sems):
        del w_in_ref
        x, y, c = _my_place()
        chips = [(1 - x, y), (x, 1 - y), (1 - x, 1 - y)]
        cps = []
        for j, chip in enumerate(chips):
            win = _landed(w_ref, kind, chip, c)
            cp = pltpu.make_async_remote_copy(
                src_ref=win, dst_ref=win, send_sem=send_sems.at[j], recv_sem=recv_sems.at[j],
                device_id=(x, y, 1 - c), device_id_type=MESH)
            cp.start()
            cps.append(cp)
        for j, chip in enumerate(chips):
            theirs = _landed(w_ref, kind, chip, 1 - c)
            pltpu.make_async_remote_copy(
                src_ref=theirs, dst_ref=theirs, send_sem=send_sems.at[j], recv_sem=recv_sems.at[j],
                device_id=(x, y, 1 - c), device_id_type=MESH).wait_recv()
        for cp in cps:
            cp.wait_send()

    return pl.pallas_call(
        body, name="pass_to_sibling",
        out_shape=jax.ShapeDtypeStruct(w_full.shape, w_full.dtype),
        in_specs=[ANY], out_specs=ANY, input_output_aliases={0: 0},
        scratch_shapes=[pltpu.SemaphoreType.DMA((3,)), pltpu.SemaphoreType.DMA((3,))],
    )(w_full)


def _reduce_pair_exchange(grads, kinds):
    n = len(grads)
    half_shapes = [(g.shape[0] // 2, g.shape[1]) if kind == "col" else (g.shape[0], g.shape[1] // 2)
                   for g, kind in zip(grads, kinds)]

    def body(*refs):
        g_refs, pa_refs = refs[:n], refs[n:2 * n]
        send_sems, recv_sems = refs[2 * n:]
        x, y, c = _my_place()
        cps = []
        for k in range(n):
            cp = pltpu.make_async_remote_copy(
                src_ref=_half_window(g_refs[k], kinds[k], 1 - c), dst_ref=pa_refs[k],
                send_sem=send_sems.at[k], recv_sem=recv_sems.at[k],
                device_id=(x, y, 1 - c), device_id_type=MESH)
            cp.start()
            cps.append(cp)
        for cp in cps:
            cp.wait()

    return pl.pallas_call(
        body, name="reduce_pair_exchange",
        out_shape=[jax.ShapeDtypeStruct(s, F32) for s in half_shapes],
        in_specs=[ANY] * n, out_specs=[ANY] * n,
        scratch_shapes=[pltpu.SemaphoreType.DMA((n,)), pltpu.SemaphoreType.DMA((n,))],
    )(*grads)


def _row_tile(rows, cols, itemsize=4, target=2 * 1024 * 1024, mult=16):
    t = rows
    while t % 2 == 0 and t // 2 >= mult and (t // 2) % mult == 0 and t * cols * itemsize > target:
        t //= 2
    return t


def _reduce_pair_sum(g, pa, kind, c_arr, name):
    hr, hc_ = pa.shape
    tr = _row_tile(hr, hc_)
    nb = hr // tr

    def body(c_ref, g_ref, pa_ref, o_ref):
        o_ref[...] = (g_ref[...] + pa_ref[...]).astype(BF16)

    if kind == "col":
        g_map = lambda i, c_ref: (c_ref[0] * nb + i, 0)
    else:
        g_map = lambda i, c_ref: (i, c_ref[0])
    return pl.pallas_call(
        body, name=name,
        grid_spec=pltpu.PrefetchScalarGridSpec(
            num_scalar_prefetch=1, grid=(nb,),
            in_specs=[pl.BlockSpec((tr, hc_), g_map), pl.BlockSpec((tr, hc_), lambda i, c_ref: (i, 0))],
            out_specs=pl.BlockSpec((tr, hc_), lambda i, c_ref: (i, 0))),
        out_shape=jax.ShapeDtypeStruct((hr, hc_), BF16),
        compiler_params=_params(("parallel",)),
    )(c_arr, g, pa)


def _half_shard_shape(full_shape, kind):
    r, c = full_shape
    return (r // 2, c // 4) if kind == "col" else (r // 4, c // 2)


def _to_owner_copies(h_refs, land_refs, send_sems, recv_sems, kinds):
    n = len(h_refs)
    x, y, c = _my_place()
    chips = [(1 - x, y), (x, 1 - y), (1 - x, 1 - y)]
    cps = []
    for j, chip in enumerate(chips):
        pj = 2 * chip[0] + chip[1]
        for k in range(n):
            cps.append(pltpu.make_async_remote_copy(
                src_ref=_shard_window(h_refs[k], kinds[k], pj), dst_ref=land_refs[k].at[j],
                send_sem=send_sems.at[j * n + k], recv_sem=recv_sems.at[j * n + k],
                device_id=(*chip, c), device_id_type=MESH))
    return cps


def _reduce_to_owner_start(halves, kinds, full_shapes):
    n = len(halves)
    hs = [_half_shard_shape(fs, kind) for fs, kind in zip(full_shapes, kinds)]
    hbm = pl.BlockSpec(memory_space=pltpu.HBM)
    sem = pl.BlockSpec(memory_space=pltpu.SEMAPHORE)

    def body(*refs):
        h_refs, land_refs = refs[:n], refs[n:2 * n]
        send_sems, recv_sems = refs[2 * n:2 * n + 2]
        token = refs[-1]
        for cp in _to_owner_copies(h_refs, land_refs, send_sems, recv_sems, kinds):
            cp.start()
        token[...] = jnp.zeros_like(token)

    lands = [pltpu.with_memory_space_constraint(lax.empty((3,) + s, BF16), pltpu.HBM) for s in hs]
    out = pl.pallas_call(
        body, name="reduce_to_owner_start",
        out_shape=[pltpu.SemaphoreType.DMA((3 * n,)), pltpu.SemaphoreType.DMA((3 * n,))]
        + [pltpu.HBM(h.shape, h.dtype) for h in halves] + [pltpu.HBM((3,) + s, BF16) for s in hs]
        + [jax.ShapeDtypeStruct((8, LANES), F32)],
        in_specs=[hbm] * (2 * n), out_specs=[sem, sem] + [hbm] * (2 * n) + [VM],
        input_output_aliases={k: 2 + k for k in range(2 * n)},
        compiler_params=pltpu.CompilerParams(has_side_effects=pltpu.SideEffectType.DATAFLOW_SIDE_EFFECTING),
    )(*[pltpu.with_memory_space_constraint(h, pltpu.HBM) for h in halves], *lands)
    return out[0], out[1], out[2:2 + n], out[2 + n:2 + 2 * n], out[-1]


def _reduce_to_owner_wait(send_sems, recv_sems, halves, lands, after, kinds):
    n = len(halves)
    hbm = pl.BlockSpec(memory_space=pltpu.HBM)
    sem = pl.BlockSpec(memory_space=pltpu.SEMAPHORE)

    def body(*refs):
        h_refs, land_refs = refs[:n], refs[n:2 * n]
        send_s, recv_s = refs[2 * n:2 * n + 2]
        for cp in _to_owner_copies(h_refs, land_refs, send_s, recv_s, kinds):
            cp.wait_send()
            cp.wait_recv()

    out = pl.pallas_call(
        body, name="reduce_to_owner_wait",
        out_shape=[pltpu.HBM(h.shape, h.dtype) for h in halves] + [pltpu.HBM(l.shape, l.dtype) for l in lands],
        in_specs=[hbm] * (2 * n) + [sem, sem, ANY], out_specs=[hbm] * (2 * n),
        input_output_aliases={k: k for k in range(2 * n)},
        compiler_params=pltpu.CompilerParams(has_side_effects=pltpu.SideEffectType.DATAFLOW_SIDE_EFFECTING),
    )(*halves, *lands, send_sems, recv_sems, after)
    return out[:n], out[n:]


def _reduce_finish(halves, recvd, kinds, full_shapes):
    n = len(halves)
    hs = [_half_shard_shape(fs, kind) for fs, kind in zip(full_shapes, kinds)]
    shard_shapes = [(fs[0], fs[1] // 4) if kind == "col" else (fs[0] // 4, fs[1])
                    for fs, kind in zip(full_shapes, kinds)]

    def body(*refs):
        h_refs, rc_refs, gs_refs = refs[:n], refs[n:2 * n], refs[2 * n:3 * n]
        own_refs, gh_refs = refs[3 * n:4 * n], refs[4 * n:5 * n]
        in_sems, loc_sems, send_sems, recv_sems = refs[5 * n:]
        x, y, c = _my_place()
        p = 2 * x + y
        loads = []
        for k in range(n):
            cp = pltpu.make_async_copy(_shard_window(h_refs[k], kinds[k], p), own_refs[k], in_sems.at[k])
            cp.start()
            loads.append(cp)
        outs = []
        for k in range(n):
            loads[k].wait()
            gh_refs[k][...] = (own_refs[k][...].astype(F32) + rc_refs[k][0].astype(F32)
                               + rc_refs[k][1].astype(F32) + rc_refs[k][2].astype(F32))
            dst = _half_window(gs_refs[k], kinds[k], c)
            lc = pltpu.make_async_copy(gh_refs[k], dst, loc_sems.at[k])
            lc.start()
            rc = pltpu.make_async_remote_copy(
                src_ref=gh_refs[k], dst_ref=dst, send_sem=send_sems.at[k], recv_sem=recv_sems.at[k],
                device_id=(x, y, 1 - c), device_id_type=MESH)
            rc.start()
            outs.append((lc, rc))
        for k, (lc, rc) in enumerate(outs):
            lc.wait()
            rc.wait_send()
            pltpu.make_async_remote_copy(
                src_ref=gh_refs[k], dst_ref=_half_window(gs_refs[k], kinds[k], 1 - c),
                send_sem=send_sems.at[k], recv_sem=recv_sems.at[k],
                device_id=(x, y, 1 - c), device_id_type=MESH).wait_recv()

    return pl.pallas_call(
        body, name="reduce_finish",
        out_shape=[jax.ShapeDtypeStruct(s, F32) for s in shard_shapes],
        in_specs=[ANY] * n + [VM] * n, out_specs=[ANY] * n,
        scratch_shapes=[pltpu.VMEM(s, BF16) for s in hs] + [pltpu.VMEM(s, F32) for s in hs]
        + [pltpu.SemaphoreType.DMA((n,)) for _ in range(4)],
        compiler_params=_params(),
    )(*halves, *recvd)


def _mod_part(c_all, w_ada_l, b_l):
    def body(c_ref, w_ref, b_ref, o_ref):
        o_ref[...] = _dot(c_ref[...].astype(BF16), w_ref[...].astype(BF16)) + b_ref[...]

    return pl.pallas_call(
        body, name="mod_part", out_shape=jax.ShapeDtypeStruct((8, w_ada_l.shape[1]), F32),
        in_specs=[VM, VM, VM], out_specs=VM, compiler_params=_params(),
    )(c_all, w_ada_l, b_l)


def _sum_devices(parts, m):
    def body(p_ref, o_ref):
        acc = p_ref[0:m, :]
        for d in range(1, 8):
            acc = acc + p_ref[d * m:(d + 1) * m, :]
        o_ref[...] = acc

    return pl.pallas_call(
        body, name="sum_devices", out_shape=jax.ShapeDtypeStruct((m, parts.shape[1]), F32),
        in_specs=[VM], out_specs=VM, compiler_params=_params(),
    )(parts)


def _grad_w_ada(c_all_t, dmod_l):
    d, w = c_all_t.shape[0], dmod_l.shape[1]

    def body(ct_ref, dm_ref, o_ref):
        acc = ct_ref[:, 0:1] * dm_ref[0:1, :]
        for b in range(1, 8):
            acc = acc + ct_ref[:, b:b + 1] * dm_ref[b:b + 1, :]
        o_ref[...] = acc

    return pl.pallas_call(
        body, name="grad_w_ada", out_shape=jax.ShapeDtypeStruct((d, w), F32),
        in_specs=[VM, VM], out_specs=VM, compiler_params=_params(),
    )(c_all_t, dmod_l)


def _adamw_math(w, g, m, v):
    nm = ADAM_B1 * m + (1.0 - ADAM_B1) * g
    nv = ADAM_B2 * v + (1.0 - ADAM_B2) * (g * g)
    m_hat = nm / (1.0 - ADAM_B1 ** ADAM_STEP)
    v_hat = nv / (1.0 - ADAM_B2 ** ADAM_STEP)
    return -ADAM_LR * (m_hat / (jnp.sqrt(v_hat) + ADAM_EPS) + ADAM_WD * w), nm, nv


def _adamw_small(ws, g8, ms, vs):
    shapes = [jax.ShapeDtypeStruct(w.shape, F32) for w in ws]

    def body(*refs):
        w_refs, g_ref, m_refs, v_refs, outs = refs[0:6], refs[6], refs[7:13], refs[13:19], refs[19:]
        for idx in range(6):
            if idx < 5:
                g = g_ref[idx:idx + 1, :]
            else:
                g = jnp.concatenate([g_ref[5:6, :], g_ref[6:7, :], g_ref[7:8, :]], axis=1)
            res = (g,) + _adamw_math(w_refs[idx][...], g, m_refs[idx][...], v_refs[idx][...])
            for fam in range(4):
                outs[6 * fam + idx][...] = res[fam]

    return pl.pallas_call(
        body, name="adamw_small", out_shape=shapes * 4,
        in_specs=[VM] * 19, out_specs=[VM] * 24, compiler_params=_params(),
    )(*ws, g8, *ms, *vs)


def _adamw(w, g, m, v, name):
    r, c = w.shape
    tr = _row_tile(r, c, target=1024 * 1024, mult=8)

    def body(w_ref, g_ref, m_ref, v_ref, d_ref, nm_ref, nv_ref):
        d_ref[...], nm_ref[...], nv_ref[...] = _adamw_math(w_ref[...], g_ref[...], m_ref[...], v_ref[...])

    spec = pl.BlockSpec((tr, c), lambda i: (i, 0))
    return pl.pallas_call(
        body, name=name, grid=(r // tr,),
        out_shape=[jax.ShapeDtypeStruct((r, c), F32)] * 3,
        in_specs=[spec] * 4, out_specs=[spec] * 3,
        compiler_params=_params(("parallel",)),
    )(w, g, m, v)


def _rope_coefficients(pos, ropetab, ts):
    s_len = pos.shape[0]

    def body(pos_ref, tab_ref, o_ref):
        ang = pos_ref[...].astype(F32) * tab_ref[0:1, :]
        cs, sn = jnp.cos(ang), jnp.sin(ang)
        o_ref[0] = jnp.where(tab_ref[3:4, :] > 0, cs, 1.0)
        o_ref[1] = -sn * tab_ref[1:2, :]
        o_ref[2] = sn * tab_ref[2:3, :]

    return pl.pallas_call(
        body, name="rope_coefficients", grid=(s_len // ts,),
        in_specs=[pl.BlockSpec((ts, 1), lambda i: (i, 0)), pl.BlockSpec((8, LANES), lambda i: (0, 0))],
        out_specs=pl.BlockSpec((3, ts, LANES), lambda i: (0, i, 0)),
        out_shape=jax.ShapeDtypeStruct((3, s_len, LANES), F32),
        compiler_params=_params(("parallel",)),
    )(pos, ropetab)


def _deinterleave_store(vals, slab, out_ref, d, ts, dtype):
    if d == 1:
        for s in range(4):
            out_ref[0, :, s * LANES:(s + 1) * LANES] = vals[s].astype(dtype)
        return
    for s in range(4):
        slab[s] = vals[s]
    for r in range(d):
        for s in range(4):
            out_ref[r, :, s * LANES:(s + 1) * LANES] = slab[s, pl.ds(r, ts // d, stride=d), :].astype(dtype)


def _interleave_load(blk_ref, slab, d, ts):
    if d == 1:
        return [blk_ref[0, :, s * LANES:(s + 1) * LANES] for s in range(4)]
    for r in range(d):
        for s in range(4):
            slab[s, pl.ds(r, ts // d, stride=d), :] = blk_ref[r, :, s * LANES:(s + 1) * LANES]
    return [slab[s] for s in range(4)]


def _norm_modulate(x, modv, norm_g, ts):
    s_len, d_model = x.shape

    def body(x_ref, mod_ref, g_ref, h_ref, ht_ref):
        xv = x_ref[...]
        r = lax.rsqrt(jnp.mean(xv * xv, axis=-1, keepdims=True) + EPS)
        h = (xv * r) * g_ref[...] * (1.0 + mod_ref[1:2, :]) + mod_ref[0:1, :]
        h_ref[...] = h.astype(BF16)
        ht_ref[...] = h.T.astype(BF16)

    return pl.pallas_call(
        body, name="norm_modulate", grid=(s_len // ts,),
        in_specs=[pl.BlockSpec((ts, d_model), lambda i: (i, 0)), pl.BlockSpec((8, d_model), lambda i: (0, 0)),
                  pl.BlockSpec((1, d_model), lambda i: (0, 0))],
        out_specs=[pl.BlockSpec((ts, d_model), lambda i: (i, 0)), pl.BlockSpec((d_model, ts), lambda i: (0, i))],
        out_shape=[jax.ShapeDtypeStruct((s_len, d_model), BF16), jax.ShapeDtypeStruct((d_model, s_len), BF16)],
        compiler_params=_params(("parallel",)),
    )(x, modv, norm_g)


def _plain_projection(h, w_in, first_tile, n_tiles, ts, name):
    s_len, d_model = h.shape
    ct = COL_TILE

    def body(h_ref, w_ref, o_ref):
        o_ref[...] = _dot(h_ref[...], w_ref[...]).astype(BF16)

    return pl.pallas_call(
        body, name=name, grid=(s_len // ts, n_tiles),
        in_specs=[pl.BlockSpec((ts, d_model), lambda i, j: (i, 0)),
                  pl.BlockSpec((d_model, ct), lambda i, j: (0, first_tile + j))],
        out_specs=pl.BlockSpec((ts, ct), lambda i, j: (i, j)),
        out_shape=jax.ShapeDtypeStruct((s_len, n_tiles * ct), BF16),
        compiler_params=_params(("parallel", "arbitrary")),
    )(h, w_in)


def _qkv_projection(h, rope, w_in, gi, ts):
    s_len, d_model = h.shape
    dil = DILATIONS[gi]
    nc = 3 * d_model // COL_TILE
    hr = ts // 2
    assert COL_TILE == ATT_W and hr % (16 * dil) == 0

    def body(h_ref, rope_ref, wq_ref, wk_ref, wv_ref, o_ref, slab):
        rc, ra, rb = rope_ref.at[0], rope_ref.at[1], rope_ref.at[2]
        w_refs = (wq_ref, wk_ref, wv_ref)
        units = [(w, half) for w in range(3) for half in range(2)]

        def matmul(w, half):
            return _dot(h_ref[half * hr:(half + 1) * hr, :], w_refs[w][...])

        def finish(w, half, res):
            rows = slice(half * hr, (half + 1) * hr)
            vals = []
            for s in range(4):
                t = res[:, s * LANES:(s + 1) * LANES]
                if w < 2:
                    t = (t * rc[rows, :] + pltpu.roll(t, LANES - 8, 1) * ra[rows, :]
                         + pltpu.roll(t, 8, 1) * rb[rows, :])
                vals.append(t)
            out = o_ref.at[w, :, half * (hr // dil):(half + 1) * (hr // dil), :]
            _deinterleave_store(vals, slab.at[half], out, dil, hr, BF16)

        res_next = matmul(*units[0])
        for ui, (w, half) in enumerate(units):
            res = res_next
            if ui + 1 < len(units):
                res_next = matmul(*units[ui + 1])
            finish(w, half, res)

    def w_spec(w):
        return pl.BlockSpec((d_model, COL_TILE), lambda i: (0, nc + 3 * w + gi))

    return pl.pallas_call(
        body, name="qkv_projection_%d" % dil, grid=(s_len // ts,),
        in_specs=[pl.BlockSpec((ts, d_model), lambda i: (i, 0)), pl.BlockSpec((3, ts, LANES), lambda i: (0, i, 0)),
                  w_spec(0), w_spec(1), w_spec(2)],
        out_specs=pl.BlockSpec((3, dil, ts // dil, ATT_W), lambda i: (0, 0, i, 0)),
        out_shape=jax.ShapeDtypeStruct((3, dil, s_len // dil, ATT_W), BF16),
        scratch_shapes=[pltpu.VMEM((2, 4, hr, LANES), F32)],
        compiler_params=_params(("parallel",)),
    )(h, rope, w_in, w_in, w_in)


def _layernorm_stats(u1):
    mu = jnp.mean(u1, axis=-1, keepdims=True)
    xc = u1 - mu
    rstd = lax.rsqrt(jnp.mean(xc * xc, axis=-1, keepdims=True) + EPS)
    return xc * rstd, rstd


def _shifted_copies(win, shf, ts):
    rows = ts + HALO - 8
    for b in range(1, 8):
        shf[b - 1, 0:rows, :] = win[pl.ds(b, rows), :]


def _tap(win, shf, off, r0, rows):
    a, b = divmod(off, 8)
    start = 8 * a + r0
    if b == 0:
        return win[start:start + rows, :]
    return shf[b - 1, start:start + rows, :]


def _conv_forward(p_conv, conv_w, conv_b, ln_g, ln_b, w_co, ts):
    s_len, d3 = p_conv.shape
    dm = d3 // 3

    def body(p_ref, cw_ref, cb_ref, g_ref, b_ref, w_ref, y_ref, u0_ref, u1_ref, win, shf):
        i = pl.program_id(0)

        @pl.when(i == 0)
        def _():
            win[0:HALO, :] = jnp.zeros((HALO, dm), F32)

        a, b, z = (p_ref[:, k * dm:(k + 1) * dm].astype(F32) for k in range(3))
        u0 = a * _sig(b)
        win[HALO:HALO + ts, :] = u0
        u0_ref[...] = u0
        _shifted_copies(win, shf, ts)
        for r0 in range(0, ts, CONV_ROWS_FWD):
            acc = jnp.broadcast_to(cb_ref[...], (CONV_ROWS_FWD, dm))
            for k in range(CONV_K):
                acc = acc + cw_ref[k:k + 1, :] * _tap(win, shf, HALO - (CONV_K - 1) + k, r0, CONV_ROWS_FWD)
            u1_ref[r0:r0 + CONV_ROWS_FWD, :] = acc
        xh, _ = _layernorm_stats(u1_ref[...])
        u2 = xh * g_ref[...] + b_ref[...]
        a_conv = (u2 * _sig(u2)) * (z * _sig(z))
        y_ref[...] = _dot(a_conv.astype(BF16), w_ref[...])
        win[0:HALO, :] = win[ts:ts + HALO, :]

    row = pl.BlockSpec((1, dm), lambda i: (0, 0))
    tile = pl.BlockSpec((ts, dm), lambda i: (i, 0))
    return pl.pallas_call(
        body, name="conv_forward", grid=(s_len // ts,),
        in_specs=[pl.BlockSpec((ts, d3), lambda i: (i, 0)),
                  pl.BlockSpec((HALO, dm), lambda i: (0, 0)), row, row, row,
                  pl.BlockSpec((dm, dm), lambda i: (0, 0))],
        out_specs=[tile, tile, tile],
        out_shape=[jax.ShapeDtypeStruct((s_len, dm), F32)] * 3,
        scratch_shapes=[pltpu.VMEM((ts + HALO, dm), F32), pltpu.VMEM((7, ts + HALO - 8, dm), F32)],
        compiler_params=_params(("arbitrary",)),
    )(p_conv, conv_w, conv_b, ln_g, ln_b, w_co)


def _conv_backward(dp, dyc, p_conv, u0, u1, conv_w, ln_g, ln_b, w_co, ts):
    s_len, d3 = p_conv.shape
    dm = d3 // 3
    nt = s_len // ts

    def body(dp_in, dy_ref, p_ref, u0_ref, u1_ref, cw_ref, g_ref, b_ref, w_ref,
             dp_ref, gw_ref, gs_ref, gcw_ref, dwin, shf):
        del dp_in
        i = pl.program_id(0)

        @pl.when(i == 0)
        def _():
            gw_ref[...] = jnp.zeros_like(gw_ref)
            gs_ref[...] = jnp.zeros_like(gs_ref)
            gcw_ref[...] = jnp.zeros_like(gcw_ref)
            dwin[ts:ts + HALO, :] = jnp.zeros((HALO, dm), F32)

        dy = dy_ref[...]
        z = p_ref[:, 2 * dm:3 * dm].astype(F32)
        d_ac = _dot_nt(dy, w_ref[...])
        xh, rstd = _layernorm_stats(u1_ref[...])
        u2 = xh * g_ref[...] + b_ref[...]
        sg2, sgz = _sig(u2), _sig(z)
        u3, sz = u2 * sg2, z * sgz
        gw_ref[...] += _dot_tn((u3 * sz).astype(BF16), dy)
        d_z = d_ac * u3 * _dsilu(z, sgz)
        d_u2 = d_ac * sz * _dsilu(u2, sg2)
        gs_ref[0:1, :] += jnp.sum(d_u2 * xh, axis=0, keepdims=True)
        gs_ref[1:2, :] += jnp.sum(d_u2, axis=0, keepdims=True)
        dxh = d_u2 * g_ref[...]
        d_u1 = rstd * (dxh - jnp.mean(dxh, axis=-1, keepdims=True)
                       - xh * jnp.mean(dxh * xh, axis=-1, keepdims=True))
        gs_ref[2:3, :] += jnp.sum(d_u1, axis=0, keepdims=True)
        dwin[0:ts, :] = d_u1
        dp_ref[:, 2 * dm:3 * dm] = d_z.astype(BF16)
        _shifted_copies(dwin, shf, ts)
        for k in range(CONV_K):
            part = jnp.zeros((CONV_ROWS_BWD, dm), F32)
            for r0 in range(0, ts, CONV_ROWS_BWD):
                part = part + u0_ref[r0:r0 + CONV_ROWS_BWD, :] * _tap(dwin, shf, CONV_K - 1 - k, r0, CONV_ROWS_BWD)
            gcw_ref[k:k + 1, :] += jnp.sum(part, axis=0, keepdims=True)
        for r0 in range(0, ts, CONV_ROWS_BWD):
            d_u0 = jnp.zeros((CONV_ROWS_BWD, dm), F32)
            for k in range(CONV_K):
                d_u0 = d_u0 + cw_ref[k:k + 1, :] * _tap(dwin, shf, CONV_K - 1 - k, r0, CONV_ROWS_BWD)
            rows = slice(r0, r0 + CONV_ROWS_BWD)
            sgb = _sig(p_ref[rows, dm:2 * dm].astype(F32))
            dp_ref[rows, 0:dm] = (d_u0 * sgb).astype(BF16)
            dp_ref[rows, dm:2 * dm] = (d_u0 * p_ref[rows, 0:dm].astype(F32) * sgb * (1.0 - sgb)).astype(BF16)
        dwin[ts:ts + HALO, :] = dwin[0:HALO, :]

    rev = lambda i: (nt - 1 - i, 0)
    row = pl.BlockSpec((1, dm), lambda i: (0, 0))
    tile = pl.BlockSpec((ts, dm), rev)
    return pl.pallas_call(
        body, name="conv_backward", grid=(nt,),
        in_specs=[ANY, tile, pl.BlockSpec((ts, d3), rev), tile,
                  tile, pl.BlockSpec((HALO, dm), lambda i: (0, 0)), row, row,
                  pl.BlockSpec((dm, dm), lambda i: (0, 0))],
        out_specs=[pl.BlockSpec((ts, d3), rev), pl.BlockSpec((dm, dm), lambda i: (0, 0)),
                   pl.BlockSpec((8, dm), lambda i: (0, 0)), pl.BlockSpec((HALO, dm), lambda i: (0, 0))],
        out_shape=[jax.ShapeDtypeStruct(dp.shape, BF16), jax.ShapeDtypeStruct((dm, dm), F32),
                   jax.ShapeDtypeStruct((8, dm), F32), jax.ShapeDtypeStruct((HALO, dm), F32)],
        input_output_aliases={0: 0},
        scratch_shapes=[pltpu.VMEM((ts + HALO, dm), F32), pltpu.VMEM((7, ts + HALO - 8, dm), F32)],
        compiler_params=_params(("arbitrary",)),
    )(dp, dyc, p_conv, u0, u1, conv_w, ln_g, ln_b, w_co)


def _att_masks():
    head0 = lax.broadcasted_iota(jnp.int32, (BLK, LANES), 1) < 64
    col = lax.broadcasted_iota(jnp.int32, (BLK, 4 * BLK), 1)
    row = lax.broadcasted_iota(jnp.int32, (BLK, 4 * BLK), 0)
    kj = col % BLK
    prev = jnp.where(col < 2 * BLK, 1, 0)
    band = jnp.where(col < 2 * BLK, kj - row, row - kj)
    return head0, band, prev


def _fill_block_diagonal(dst, slab, src_ref, halo_ref, head0, nq):
    sl = slice(slab * LANES, (slab + 1) * LANES)
    for b in range(nq + 1):
        blk = halo_ref[:, sl] if b == 0 else src_ref[(b - 1) * BLK:b * BLK, sl]
        base = (slab * (nq + 1) + b) * 2 * BLK
        zero = jnp.zeros_like(blk)
        dst[base:base + BLK, :] = jnp.where(head0, blk, zero)
        dst[base + BLK:base + 2 * BLK, :] = jnp.where(head0, zero, blk)


def _attention_forward(qkv, seq_len, qt, name):
    s_len = qkv.shape[1]
    nq = qt // BLK
    tiles_per_seq = seq_len // qt

    def body(q_ref, k_ref, v_ref, kh_ref, vh_ref, o_ref, lse_ref, kbd, vbd):
        first = jnp.where((pl.program_id(0) % tiles_per_seq) == 0, 4 * BLK, 0)
        head0, band, prev = _att_masks()
        band_first = band - prev * first
        for p in range(4):
            _fill_block_diagonal(kbd, p, k_ref, kh_ref, head0, nq)
            _fill_block_diagonal(vbd, p, v_ref, vh_ref, head0, nq)
        units = [(p, n) for p in range(4) for n in range(nq)]

        def keys_of(p, n):
            base = (p * (nq + 1) + n) * 2 * BLK
            return slice(base, base + 4 * BLK)

        def scores(p, n):
            q2 = q_ref[n * BLK:(n + 1) * BLK, p * LANES:(p + 1) * LANES] * 0.125
            return _dot_nt(q2, kbd[keys_of(p, n), :])

        def finish(p, n, o, den, lse):
            rows, sl = slice(n * BLK, (n + 1) * BLK), slice(p * LANES, (p + 1) * LANES)
            o_ref[rows, sl] = o / jnp.where(head0, den[0], den[1])
            lse_ref[rows, sl] = jnp.where(head0, lse[0], lse[1])

        s_next = scores(*units[0])
        pending = None
        for ui, (p, n) in enumerate(units):
            s = s_next
            if ui + 1 < len(units):
                s_next = scores(*units[ui + 1])
            s = jnp.where((band_first if n == 0 else band) >= 0, s, NEG)
            grp = [s[:, g * BLK:(g + 1) * BLK] for g in range(4)]
            ps, den, lse = [None] * 4, [], []
            for h in range(2):
                m = jnp.max(jnp.maximum(grp[h], grp[2 + h]), axis=-1, keepdims=True)
                ps[h], ps[2 + h] = jnp.exp(grp[h] - m), jnp.exp(grp[2 + h] - m)
                dn = jnp.sum(ps[h] + ps[2 + h], axis=-1, keepdims=True)
                den.append(dn)
                lse.append(m + jnp.log(dn))
            pmat = jnp.concatenate([x.astype(BF16) for x in ps], axis=1)
            o = _dot(pmat, vbd[keys_of(p, n), :])
            if pending is not None:
                finish(*pending)
            pending = (p, n, o, den, lse)
        finish(*pending)

    def which(w):
        return pl.BlockSpec((None, qt, ATT_W), lambda i: (w, i, 0))

    def halo(w):
        return pl.BlockSpec((None, BLK, ATT_W), lambda i: (w, jnp.maximum(i * nq - 1, 0), 0))

    out = pl.BlockSpec((qt, ATT_W), lambda i: (i, 0))
    bd = pltpu.VMEM((4 * (nq + 1) * 2 * BLK, LANES), BF16)
    return pl.pallas_call(
        body, name=name, grid=(s_len // qt,),
        in_specs=[which(0), which(1), which(2), halo(1), halo(2)],
        out_specs=[out, out], out_shape=[jax.ShapeDtypeStruct((s_len, ATT_W), F32)] * 2,
        scratch_shapes=[bd, bd],
        compiler_params=_params(("parallel",)),
    )(qkv, qkv, qkv, qkv, qkv)


def _attention_backward(qkv, d_att, lse, delta, seq_len, qt, name):
    s_len = qkv.shape[1]
    nq = qt // BLK
    tiles_per_seq = seq_len // qt
    nblk = s_len // BLK

    def body(q_ref, k_ref, v_ref, kh_ref, vh_ref, do_ref, lse_ref, dl_ref,
             qn_ref, don_ref, lsen_ref, dln_ref, dqkv_ref, kbd, vbd):
        i = pl.program_id(0)
        first = jnp.where((i % tiles_per_seq) == 0, 4 * BLK, 0)
        last = jnp.where((i % tiles_per_seq) == tiles_per_seq - 1, 4 * BLK, 0)
        head0, band, prev = _att_masks()
        band_first = band - prev * first
        band_tail = band[:, 0:2 * BLK] - last
        for p in range(4):
            _fill_block_diagonal(kbd, p, k_ref, kh_ref, head0, nq)
            _fill_block_diagonal(vbd, p, v_ref, vh_ref, head0, nq)
        units = [(p, n) for p in range(4) for n in range(nq + 1)]

        def stage_a(p, n):
            sl = slice(p * LANES, (p + 1) * LANES)
            base = (p * (nq + 1) + n) * 2 * BLK
            if n < nq:
                rows = slice(n * BLK, (n + 1) * BLK)
                q2, do2, lse2, dl2 = q_ref[rows, sl], do_ref[rows, sl], lse_ref[rows, sl], dl_ref[rows, sl]
                keys = slice(base, base + 4 * BLK)
            else:
                q2, do2, lse2, dl2 = qn_ref[:, sl], don_ref[:, sl], lsen_ref[:, sl], dln_ref[:, sl]
                keys = slice(base, base + 2 * BLK)
            s = _dot_nt(q2 * 0.125, kbd[keys, :])
            dp = _dot_nt(do2, vbd[keys, :])
            return q2, do2, lse2, dl2, keys, s, dp

        def stage_b(n, lse2, dl2, s, dp):
            mask = band_tail if n == nq else (band_first if n == 0 else band)
            ps, dss = [], []
            for g in range(s.shape[1] // BLK):
                h = g % 2
                cols = slice(g * BLK, (g + 1) * BLK)
                pg = jnp.exp(jnp.where(mask[:, cols] >= 0, s[:, cols] - lse2[:, h * 64:h * 64 + 1], NEG))
                ps.append(pg.astype(BF16))
                dss.append((pg * (dp[:, cols] - dl2[:, h * 64:h * 64 + 1]) * 0.125).astype(BF16))
            return jnp.concatenate(ps, axis=1), jnp.concatenate(dss, axis=1)

        def heads(r, g):
            return jnp.where(head0, r[g * BLK:(g + 1) * BLK, :], r[(g + 1) * BLK:(g + 2) * BLK, :])

        a_next = stage_a(*units[0])
        carry = None
        for ui, (p, n) in enumerate(units):
            q2, do2, lse2, dl2, keys, s, dp = a_next
            if ui + 1 < len(units):
                a_next = stage_a(*units[ui + 1])
            pmat, dsmat = stage_b(n, lse2, dl2, s, dp)
            sl = slice(p * LANES, (p + 1) * LANES)
            if n < nq:
                dqkv_ref[0, n * BLK:(n + 1) * BLK, sl] = _dot(dsmat, kbd[keys, :])
            dkbd = _dot_tn(dsmat, q2)
            dvbd = _dot_tn(pmat, do2)
            if n > 0:
                prow = slice((n - 1) * BLK, n * BLK)
                dqkv_ref[1, prow, sl] = carry[0] + heads(dkbd, 0)
                dqkv_ref[2, prow, sl] = carry[1] + heads(dvbd, 0)
            if n < nq:
                carry = (heads(dkbd, 2), heads(dvbd, 2))

    def which(w):
        return pl.BlockSpec((None, qt, ATT_W), lambda i: (w, i, 0))

    def prev(w):
        return pl.BlockSpec((None, BLK, ATT_W), lambda i: (w, jnp.maximum(i * nq - 1, 0), 0))

    tile = pl.BlockSpec((qt, ATT_W), lambda i: (i, 0))
    nxt = pl.BlockSpec((BLK, ATT_W), lambda i: (jnp.minimum((i + 1) * nq, nblk - 1), 0))
    nxt_q = pl.BlockSpec((None, BLK, ATT_W), lambda i: (0, jnp.minimum((i + 1) * nq, nblk - 1), 0))
    return pl.pallas_call(
        body, name=name, grid=(s_len // qt,),
        in_specs=[which(0), which(1), which(2), prev(1), prev(2), tile, tile, tile, nxt_q, nxt, nxt, nxt],
        out_specs=pl.BlockSpec((3, qt, ATT_W), lambda i: (0, i, 0)),
        out_shape=jax.ShapeDtypeStruct((3, s_len, ATT_W), F32),
        scratch_shapes=[pltpu.VMEM((4 * (nq + 1) * 2 * BLK, LANES), BF16)] * 2,
        compiler_params=_params(("parallel",)),
    )(qkv, qkv, qkv, qkv, qkv, d_att, lse, delta, qkv, d_att, lse, delta)


def _merge_and_head(o_g, lse_g, p_gate, y_conv, x, tgt, modv, final_g, w_ao, w_o, ts):
    s_len, dm = x.shape
    gw = ATT_W + 2 * dm
    nt = s_len // ts
    gate_off = 3 * dm + 9 * ATT_W
    assert gate_off % gw == 0

    def body(o0, o1, o2, l0, l1, l2, pg_ref, yc_ref, x_ref, t_ref, mod_ref, fg_ref, wao_ref, wo_ref,
             loss_ref, dx_ref, dyc_ref, dpg_ref, da0, da1, da2, ls0, ls1, ls2, de0, de1, de2,
             gwo_ref, gwao_ref, gs_ref, slab):
        i = pl.program_id(0)

        @pl.when(i == 0)
        def _():
            loss_ref[...] = jnp.zeros_like(loss_ref)
            gwo_ref[...] = jnp.zeros_like(gwo_ref)
            gwao_ref[...] = jnp.zeros_like(gwao_ref)
            gs_ref[...] = jnp.zeros_like(gs_ref)

        os_, ls_ = [], []
        for dil, o_ref, l_ref in zip(DILATIONS, (o0, o1, o2), (l0, l1, l2)):
            os_.append(jnp.concatenate(_interleave_load(o_ref, slab, dil, ts), axis=1))
            ls_.append(jnp.concatenate(_interleave_load(l_ref, slab, dil, ts), axis=1))
        mx = jnp.maximum(jnp.maximum(ls_[0], ls_[1]), ls_[2])
        wts = [jnp.exp(l - mx) for l in ls_]
        wsum = wts[0] + wts[1] + wts[2]
        att = (wts[0] * os_[0] + wts[1] * os_[1] + wts[2] * os_[2]) / wsum
        lse_all = mx + jnp.log(wsum)

        z_att, g_conv, g_att = (pg_ref[:, lo:hi].astype(F32)
                                for lo, hi in ((0, ATT_W), (ATT_W, ATT_W + dm), (ATT_W + dm, gw)))
        sgz = _sig(z_att)
        sz = z_att * sgz
        a_att = (att * sz).astype(BF16)
        y_att = _dot(a_att, wao_ref[...])
        y_conv = yc_ref[...]
        sgc, sga = _sig(g_conv), _sig(g_att)
        merged = (sgc * y_conv + sga * y_att).astype(BF16)
        mo = _dot(merged, wo_ref[...])
        gate = mod_ref[2:3, :]
        x2 = x_ref[...] + gate * mo
        r = lax.rsqrt(jnp.mean(x2 * x2, axis=-1, keepdims=True) + EPS)
        xr = x2 * r
        err = xr * fg_ref[...] - t_ref[...]
        loss_ref[...] += 0.5 * jnp.sum(jnp.mean(err * err, axis=-1, keepdims=True))
        dy = err * (1.0 / dm)
        gs_ref[0:1, :] += jnp.sum(dy * xr, axis=0, keepdims=True)
        dyg = dy * fg_ref[...]
        d_x2 = r * dyg - xr * (r * jnp.mean(dyg * xr, axis=-1, keepdims=True))
        dx_ref[...] = d_x2
        gs_ref[1:2, :] += jnp.sum(d_x2 * mo, axis=0, keepdims=True)
        d_mo = (d_x2 * gate).astype(BF16)
        d_mg = _dot_nt(d_mo, wo_ref[...])
        gwo_ref[...] += _dot_tn(merged, d_mo)
        dyc_ref[...] = (d_mg * sgc).astype(BF16)
        dpg_ref[:, ATT_W:ATT_W + dm] = (d_mg * y_conv * sgc * (1.0 - sgc)).astype(BF16)
        d_ya = (d_mg * sga).astype(BF16)
        dpg_ref[:, ATT_W + dm:gw] = (d_mg * y_att * sga * (1.0 - sga)).astype(BF16)
        gwao_ref[...] += _dot_tn(a_att, d_ya)
        d_aa = _dot_nt(d_ya, wao_ref[...])
        dpg_ref[:, 0:ATT_W] = (d_aa * att * _dsilu(z_att, sgz)).astype(BF16)
        d_att = d_aa * sz
        ri = lax.broadcasted_iota(jnp.int32, (ATT_W, ATT_W), 0) // 64
        ci = lax.broadcasted_iota(jnp.int32, (ATT_W, ATT_W), 1) // 64
        ones = jnp.where(ri == ci, 1.0, 0.0).astype(BF16)
        prod = d_att * att
        hi = prod.astype(BF16)
        lo = (prod - hi.astype(F32)).astype(BF16)
        delta = _dot(hi, ones) + _dot(lo, ones)
        for val, refs, dt in ((d_att, (da0, da1, da2), BF16), (lse_all, (ls0, ls1, ls2), F32),
                              (delta, (de0, de1, de2), F32)):
            vals = [val[:, s * LANES:(s + 1) * LANES] for s in range(4)]
            for dil, ref in zip(DILATIONS, refs):
                _deinterleave_store(vals, slab, ref, dil, ts, dt)

    def grp(dil):
        return pl.BlockSpec((dil, ts // dil, ATT_W), lambda i: (0, i, 0))

    tile = pl.BlockSpec((ts, dm), lambda i: (i, 0))
    gate_tile = pl.BlockSpec((ts, gw), lambda i: (i, 0))
    const = lambda shp: pl.BlockSpec(shp, lambda i: tuple(0 for _ in shp))
    grp_shape = lambda dt: [jax.ShapeDtypeStruct((dil, s_len // dil, ATT_W), dt) for dil in DILATIONS]
    return pl.pallas_call(
        body, name="merge_and_head", grid=(nt,),
        in_specs=[grp(d) for d in DILATIONS] * 2
        + [gate_tile, tile, tile, tile, const((8, dm)), const((1, dm)), const((ATT_W, dm)), const((dm, dm))],
        out_specs=[const((8, LANES)), tile, tile, pl.BlockSpec((ts, gw), lambda i: (i, gate_off // gw))]
        + [grp(d) for d in DILATIONS] * 3
        + [const((dm, dm)), const((ATT_W, dm)), const((8, dm))],
        out_shape=[jax.ShapeDtypeStruct((8, LANES), F32), jax.ShapeDtypeStruct((s_len, dm), F32),
                   jax.ShapeDtypeStruct((s_len, dm), BF16), jax.ShapeDtypeStruct((s_len, gate_off + gw), BF16)]
        + grp_shape(BF16) + grp_shape(F32) + grp_shape(F32)
        + [jax.ShapeDtypeStruct((dm, dm), F32), jax.ShapeDtypeStruct((ATT_W, dm), F32),
           jax.ShapeDtypeStruct((8, dm), F32)],
        scratch_shapes=[pltpu.VMEM((4, ts, LANES), F32)],
        compiler_params=_params(("arbitrary",)),
    )(*o_g, *lse_g, p_gate, y_conv, x, tgt, modv, final_g, w_ao, w_o)


def _qkv_grad_to_tokens(dp, dqkv_g, rope, ts):
    s_len, dm = rope.shape[1], (dp.shape[1] - 10 * ATT_W) // 5
    qw = 3 * ATT_W
    assert (3 * dm) % qw == 0

    def body(dp_in, g0, g1, g2, rope_ref, o_ref, slab):
        del dp_in
        w = pl.program_id(1)
        rc, ra, rb = rope_ref.at[0], rope_ref.at[1], rope_ref.at[2]

        def emit(roped):
            for gi, (dil, g_ref) in enumerate(zip(DILATIONS, (g0, g1, g2))):
                vals = _interleave_load(g_ref, slab, dil, ts)
                for s in range(4):
                    t = vals[s]
                    if roped:
                        t = t * rc[...] + pltpu.roll(t * ra[...], 8, 1) + pltpu.roll(t * rb[...], LANES - 8, 1)
                    col = gi * ATT_W + s * LANES
                    o_ref[:, col:col + LANES] = t.astype(BF16)

        pl.when(w < 2)(lambda: emit(True))
        pl.when(w == 2)(lambda: emit(False))

    return pl.pallas_call(
        body, name="qkv_grad_to_tokens", grid=(s_len // ts, 3),
        in_specs=[ANY] + [pl.BlockSpec((None, dil, ts // dil, ATT_W), lambda i, w: (w, 0, i, 0)) for dil in DILATIONS]
        + [pl.BlockSpec((3, ts, LANES), lambda i, w: (0, i, 0))],
        out_specs=pl.BlockSpec((ts, qw), lambda i, w: (i, 3 * dm // qw + w)),
        out_shape=jax.ShapeDtypeStruct(dp.shape, BF16),
        input_output_aliases={0: 0},
        scratch_shapes=[pltpu.VMEM((4, ts, LANES), F32)],
        compiler_params=_params(("parallel", "arbitrary")),
    )(dp, *dqkv_g, rope)


def _wide_col_tile(cols):
    for width in (5 * COL_TILE, 2 * COL_TILE):
        if cols % width == 0:
            return width
    return COL_TILE


def _input_grad(dp, w_in, x, dx_res, modv, norm_g, ts):
    s_len, dm = x.shape
    ct = next(w * COL_TILE for w in (4, 2, 1) if dp.shape[1] % (w * COL_TILE) == 0)
    nct = dp.shape[1] // ct

    def body(p_ref, w_ref, x_ref, dxr_ref, mod_ref, g_ref, gx_ref, gs_ref, acc):
        i, j = pl.program_id(0), pl.program_id(1)

        @pl.when((i == 0) & (j == 0))
        def _():
            gs_ref[...] = jnp.zeros_like(gs_ref)

        @pl.when(j == 0)
        def _():
            acc[...] = jnp.zeros_like(acc)

        acc[...] += _dot_nt(p_ref[...], w_ref[...])

        @pl.when(j == nct - 1)
        def _():
            d_h = acc[...]
            xv = x_ref[...]
            r = lax.rsqrt(jnp.mean(xv * xv, axis=-1, keepdims=True) + EPS)
            xr = xv * r
            gs_ref[0:1, :] += jnp.sum(d_h, axis=0, keepdims=True)
            gs_ref[1:2, :] += jnp.sum(d_h * (xr * g_ref[...]), axis=0, keepdims=True)
            d_n = d_h * (1.0 + mod_ref[1:2, :])
            gs_ref[2:3, :] += jnp.sum(d_n * xr, axis=0, keepdims=True)
            dxn = d_n * g_ref[...]
            gx_ref[...] = dxr_ref[...] + r * dxn - xr * (r * jnp.mean(dxn * xr, axis=-1, keepdims=True))

    tile = pl.BlockSpec((ts, dm), lambda i, j: (i, 0))
    return pl.pallas_call(
        body, name="input_grad", grid=(s_len // ts, nct),
        in_specs=[pl.BlockSpec((ts, ct), lambda i, j: (i, j)), pl.BlockSpec((dm, ct), lambda i, j: (0, j)), tile, tile,
                  pl.BlockSpec((8, dm), lambda i, j: (0, 0)), pl.BlockSpec((1, dm), lambda i, j: (0, 0))],
        out_specs=[tile, pl.BlockSpec((8, dm), lambda i, j: (0, 0))],
        out_shape=[jax.ShapeDtypeStruct((s_len, dm), F32), jax.ShapeDtypeStruct((8, dm), F32)],
        scratch_shapes=[pltpu.VMEM((ts, dm), F32)],
        compiler_params=_params(("arbitrary", "arbitrary")),
    )(dp, w_in, x, dx_res, modv, norm_g)


def _w_in_grad(h_t, dp, ts):
    dm, s_len = h_t.shape
    ct = _wide_col_tile(dp.shape[1])

    def body(h_ref, p_ref, o_ref):
        @pl.when(pl.program_id(1) == 0)
        def _():
            o_ref[...] = jnp.zeros_like(o_ref)

        o_ref[...] += _dot(h_ref[...], p_ref[...])

    return pl.pallas_call(
        body, name="w_in_grad", grid=(dp.shape[1] // ct, s_len // ts),
        in_specs=[pl.BlockSpec((dm, ts), lambda j, i: (0, i)), pl.BlockSpec((ts, ct), lambda j, i: (i, j))],
        out_specs=pl.BlockSpec((dm, ct), lambda j, i: (0, j)),
        out_shape=jax.ShapeDtypeStruct((dm, dp.shape[1]), F32),
        compiler_params=_params(("arbitrary", "arbitrary")),
    )(h_t, dp)


def _rope_lane_table():
    l64 = jnp.arange(LANES) % 64
    half = ROT_DIM // 2
    inv_freq = ROPE_THETA ** (-(jnp.arange(half, dtype=F32) * 2.0 / ROT_DIM))
    rot = l64 < ROT_DIM
    rows = [jnp.where(rot, inv_freq[l64 % half], 0.0), (l64 < half).astype(F32),
            ((l64 >= half) & rot).astype(F32), rot.astype(F32)]
    return jnp.concatenate([jnp.stack(rows), jnp.zeros((4, LANES), F32)], axis=0)


def _tile_sizes(s_len):
    ts_big = min(1024, s_len // 2)
    ts_mid = 256
    ts_head = 256
    qt = [min(1024, s_len // dil) for dil in DILATIONS]
    return ts_big, ts_mid, ts_head, qt


def kernel(x, c, positions, norm_g, w_ada, b_ada, w_in, conv_w, conv_b, conv_ln_g, conv_ln_b, w_conv_out, w_att_out, w_o, final_g, loss_target, m_norm_g, m_w_ada, m_b_ada, m_w_in, m_conv_w, m_conv_b, m_conv_ln_g, m_conv_ln_b, m_w_conv_out, m_w_att_out, m_w_o, m_final_g, v_norm_g, v_w_ada, v_b_ada, v_w_in, v_conv_w, v_conv_b, v_conv_ln_g, v_conv_ln_b, v_w_conv_out, v_w_att_out, v_w_o, v_final_g):
    s_len, dm = x.shape[1], x.shape[2]
    ts_big, ts_mid, ts_head, qt = _tile_sizes(s_len)
    xi, yi, cidx = _my_place()
    chip = 2 * xi + yi
    batch = 4 * xi + 2 * yi + cidx
    x2d, tgt = x[0], loss_target[0]
    pos = positions.reshape(s_len, 1)
    wa_l, wi_l, cw_l = w_ada[0], w_in[0], conv_w[0]
    wco_l, wao_l, wo_l = w_conv_out[0], w_att_out[0], w_o[0]
    ada_w = wa_l.shape[1]
    cw_cols = cw_l.shape[1]

    kinds = ("col", "row", "col", "row")
    w_bufs = _cast_weights([wi_l, wco_l, wao_l, wo_l], kinds)

    cw_pad = jnp.pad(cw_l, ((0, HALO - CONV_K), (0, 0)))
    small_in = jnp.concatenate([jnp.broadcast_to(c, (8, dm)), cw_pad.reshape(8, dm)], axis=0)
    small = _allgather_small(small_in).reshape(8, 16, dm)
    c_all = small[:, 0, :]
    conv_w_full = jnp.concatenate(
        [small[2 * p, 8:16, :].reshape(HALO, cw_cols) for p in range(4)], axis=1)
    b_l = lax.dynamic_slice(b_ada, (0, chip * ada_w), (1, ada_w))
    mod_parts = _allgather_small(_mod_part(c_all, wa_l, b_l)).reshape(8, 8, ada_w)
    mod_rows = lax.dynamic_index_in_dim(mod_parts, batch, axis=1, keepdims=False)
    mod = jnp.concatenate([mod_rows[2 * p] for p in range(4)], axis=0).reshape(3, dm)
    modv = jnp.concatenate([mod, jnp.zeros((5, dm), F32)], axis=0)

    sems_a, sems_b, w_bufs, token = _gather_weights_start(w_bufs, kinds, modv)
    rope = _rope_coefficients(pos, _rope_lane_table() + token, ts_big)
    h_b, h_t = _norm_modulate(x2d, modv + token[0, 0], norm_g, ts_big)
    (w_in_b,) = _gather_weights_wait(w_bufs, kinds, sems_a, 0, (rope, h_b), "gather_weights_wait_w_in")
    w_in_b = _pass_to_sibling(w_in_b, kinds[0])

    n_conv, n_gate = 3 * dm // COL_TILE, (ATT_W + 2 * dm) // COL_TILE
    p_conv = _plain_projection(h_b, w_in_b, 0, n_conv, ts_big, "conv_projection")
    p_gate = _plain_projection(h_b, w_in_b, n_conv + 9, n_gate, ts_big, "gate_projection")
    qkv_g = [_qkv_projection(h_b, rope, w_in_b, gi, ts_big) for gi in range(3)]
    w_co_b, w_ao_b, w_o_b = _gather_weights_wait(w_bufs, kinds, sems_b, 1, (qkv_g[2], p_conv, p_gate),
                                                  "gather_weights_wait_rest")
    y_conv, u0, u1 = _conv_forward(p_conv, conv_w_full, conv_b, conv_ln_g, conv_ln_b, w_co_b, ts_mid)
    qkv_flat = [q.reshape(3, s_len, ATT_W) for q in qkv_g]
    o_g, lse_g = [], []
    for gi, dil in enumerate(DILATIONS):
        o, l = _attention_forward(qkv_flat[gi], s_len // dil, qt[gi], "attention_forward_%d" % dil)
        o_g.append(o.reshape(dil, s_len // dil, ATT_W))
        lse_g.append(l.reshape(dil, s_len // dil, ATT_W))

    (loss_p, dx_res, dyc, dp, da0, da1, da2, ls0, ls1, ls2, de0, de1, de2,
     g_wo, g_wao, head_sums) = _merge_and_head(o_g, lse_g, p_gate, y_conv, x2d, tgt, modv,
                                               final_g.reshape(1, dm), w_ao_b, w_o_b, ts_head)
    loss = lax.psum(loss_p[0, 0], ("x", "y", "c"))

    dp, g_wco, conv_sums, g_cw = _conv_backward(dp, dyc, p_conv, u0, u1, conv_w_full, conv_ln_g, conv_ln_b,
                                                w_co_b, ts_mid)
    dqkv_g = []
    for gi, (dil, da, ls, de) in enumerate(zip(DILATIONS, (da0, da1, da2), (ls0, ls1, ls2), (de0, de1, de2))):
        flat = lambda a: a.reshape(s_len, ATT_W)
        dqkv = _attention_backward(qkv_flat[gi], flat(da), flat(ls), flat(de), s_len // dil, qt[gi],
                                   "attention_backward_%d" % dil)
        dqkv_g.append(dqkv.reshape(3, dil, s_len // dil, ATT_W))
    dp = _qkv_grad_to_tokens(dp, dqkv_g, rope, 2 * ts_mid)
    g_win = _w_in_grad(h_t, dp, ts_big)

    grads = [g_win, g_wco, g_wao, g_wo]
    full_shapes = [g.shape for g in grads]
    c_arr = jnp.reshape(cidx, (1,)).astype(jnp.int32)
    recv_halves = _reduce_pair_exchange(grads, kinds)
    halves = [_reduce_pair_sum(g, pa, kind, c_arr, "reduce_pair_sum_%d" % k)
              for k, (g, pa, kind) in enumerate(zip(grads, recv_halves, kinds))]
    send_sems, recv_sems, halves, lands, token = _reduce_to_owner_start(halves, kinds, full_shapes)
    grad_x, in_sums = _input_grad(dp, w_in_b, x2d, dx_res, modv + token[0, 0], norm_g, ts_big)
    halves, recvd = _reduce_to_owner_wait(send_sems, recv_sems, halves, lands, in_sums, kinds)
    gr_win, gr_wco, gr_wao, gr_wo = _reduce_finish(halves, recvd, kinds, full_shapes)

    rows = [in_sums[2:3], conv_sums[2:3], conv_sums[0:1], conv_sums[1:2], head_sums[0:1],
            in_sums[0:1], in_sums[1:2], head_sums[1:2], g_cw, jnp.zeros((8, dm), F32)]
    part = jnp.concatenate(rows, axis=0)
    gathered = _allgather_small(part)
    tot = _sum_devices(gathered, 48)
    dmod_all = gathered.reshape(8, 48, dm)[:, 5:8, :].reshape(8, 3 * dm)
    dmod_l = lax.dynamic_slice(dmod_all, (0, chip * ada_w), (8, ada_w))
    gr_wada = _grad_w_ada(c_all.T, dmod_l)
    gr_cw = lax.dynamic_slice(tot[8:8 + HALO], (0, chip * cw_cols), (HALO, cw_cols))

    pad_cw = lambda a: jnp.pad(a[0], ((0, HALO - CONV_K), (0, 0)))
    row = lambda a: a.reshape(1, dm)
    small_upd = _adamw_small(
        [norm_g, conv_b, conv_ln_g, conv_ln_b, row(final_g), b_ada], tot[0:8],
        [m_norm_g, m_conv_b, m_conv_ln_g, m_conv_ln_b, row(m_final_g), m_b_ada],
        [v_norm_g, v_conv_b, v_conv_ln_g, v_conv_ln_b, row(v_final_g), v_b_ada])
    upd = {
        "w_ada": _adamw(wa_l, gr_wada, m_w_ada[0], v_w_ada[0], "adamw_w_ada"),
        "w_in": _adamw(wi_l, gr_win, m_w_in[0], v_w_in[0], "adamw_w_in"),
        "conv_w": _adamw(cw_pad, gr_cw, pad_cw(m_conv_w), pad_cw(v_conv_w), "adamw_conv_w"),
        "w_co": _adamw(wco_l, gr_wco, m_w_conv_out[0], v_w_conv_out[0], "adamw_w_conv_out"),
        "w_ao": _adamw(wao_l, gr_wao, m_w_att_out[0], v_w_att_out[0], "adamw_w_att_out"),
        "w_o": _adamw(wo_l, gr_wo, m_w_o[0], v_w_o[0], "adamw_w_o"),
    }

    def family(which):
        if which is None:
            sm = small_upd[0:6]
            big = {"w_ada": gr_wada, "w_in": gr_win, "conv_w": gr_cw, "w_co": gr_wco, "w_ao": gr_wao, "w_o": gr_wo}
        else:
            sm = small_upd[6 * (which + 1):6 * (which + 2)]
            big = {k: upd[k][which] for k in ("w_ada", "w_in", "conv_w", "w_co", "w_ao", "w_o")}
        return [sm[0], big["w_ada"][None], sm[5], big["w_in"][None],
                big["conv_w"][None, :CONV_K], sm[1], sm[2], sm[3], big["w_co"][None],
                big["w_ao"][None], big["w_o"][None], sm[4].reshape(dm)]

    return (loss, grad_x[None], *family(None), *family(0), *family(1), *family(2))
```

```python
import jax
import jax.numpy as jnp
from jax import lax
from jax.experimental import pallas as pl
from jax.experimental.pallas import tpu as pltpu

F32 = jnp.float32
BF16 = jnp.bfloat16
MESH = pl.DeviceIdType.MESH
ANY = pl.BlockSpec(memory_space=pl.ANY)
VM = pl.BlockSpec(memory_space=pltpu.VMEM)

EPS = 1e-6
NEG = -1e30
ATT_W = 512
DILATIONS = (1, 4, 16)
BLK = 128
CONV_K = 31
HALO = 32
CONV_ROWS_FWD = 32
CONV_ROWS_BWD = 16
ROT_DIM = 16
ROPE_THETA = 500000.0
COL_TILE = 512
LANES = 128
VMEM_LIMIT = 56 * 1024 * 1024

ADAM_LR, ADAM_B1, ADAM_B2, ADAM_EPS, ADAM_WD, ADAM_STEP = 0.001, 0.9, 0.999, 1e-08, 0.01, 10


def _params(sem=None, vmem=VMEM_LIMIT):
    return pltpu.CompilerParams(dimension_semantics=sem, vmem_limit_bytes=vmem)


def _dot(a, b):
    return jnp.dot(a, b, preferred_element_type=F32)


def _dot_nt(a, b):
    return lax.dot_general(a, b, (((1,), (1,)), ((), ())), preferred_element_type=F32)


def _dot_tn(a, b):
    return lax.dot_general(a, b, (((0,), (0,)), ((), ())), preferred_element_type=F32)


def _sig(x):
    return jax.nn.sigmoid(x)


def _dsilu(x, s):
    return s * (1.0 + x * (1.0 - s))


def _my_place():
    return lax.axis_index("x"), lax.axis_index("y"), lax.axis_index("c")


def _allgather_small(x_shard):
    m_per, n = x_shard.shape

    def body(x_ref, out_ref, send_sems, recv_sems, local_sem):
        x, y, c = _my_place()
        me, sibling = (x, y, c), (x, y, 1 - c)
        chips = [(1 - x, y), (x, 1 - y), (1 - x, 1 - y)]

        def rows(px, py, pc):
            return out_ref.at[pl.ds((4 * px + 2 * py + pc) * m_per, m_per), :]

        def copy(k, block, to, src=None):
            return pltpu.make_async_remote_copy(
                src_ref=rows(*block) if src is None else src, dst_ref=rows(*block),
                send_sem=send_sems.at[k], recv_sem=recv_sems.at[k],
                device_id=to, device_id_type=MESH)

        mine = pltpu.make_async_copy(x_ref, rows(*me), local_sem)
        mine.start()
        first = [copy(0, me, sibling, src=x_ref)]
        first += [copy(1 + j, me, (*chip, c), src=x_ref) for j, chip in enumerate(chips)]
        for cp in first:
            cp.start()
        passed = [copy(4 + j, (*chip, c), sibling) for j, chip in enumerate(chips)]
        for j, chip in enumerate(chips):
            copy(1 + j, (*chip, c), me).wait_recv()
            passed[j].start()
        copy(0, sibling, me).wait_recv()
        for j, chip in enumerate(chips):
            copy(4 + j, (*chip, 1 - c), me).wait_recv()
        for cp in first + passed:
            cp.wait_send()
        mine.wait()

    return pl.pallas_call(
        body, name="allgather_small",
        out_shape=jax.ShapeDtypeStruct((8 * m_per, n), x_shard.dtype),
        in_specs=[VM], out_specs=VM,
        scratch_shapes=[pltpu.SemaphoreType.DMA((7,)), pltpu.SemaphoreType.DMA((7,)),
                        pltpu.SemaphoreType.DMA],
    )(x_shard)


def _shard_window(ref, kind, p, n_shards=4):
    r, c = ref.shape
    if kind == "col":
        w = c // n_shards
        return ref.at[:, pl.ds(p * w, w)]
    w = r // n_shards
    return ref.at[pl.ds(p * w, w), :]


def _half_window(ref, kind, hc):
    r, c = ref.shape
    if kind == "col":
        return ref.at[pl.ds(hc * (r // 2), r // 2), :]
    return ref.at[:, pl.ds(hc * (c // 2), c // 2)]


def _landed(ref, kind, chip, hc):
    return _half_window(_shard_window(ref, kind, 2 * chip[0] + chip[1]), kind, hc)


def _cast_weights(shards, kinds):
    n = len(shards)
    full_shapes = [(s.shape[0], 4 * s.shape[1]) if kind == "col" else (4 * s.shape[0], s.shape[1])
                   for s, kind in zip(shards, kinds)]

    def body(*refs):
        w_refs, out_refs, bf_refs, sems = refs[:n], refs[n:2 * n], refs[2 * n:3 * n], refs[3 * n]
        x, y, _ = _my_place()
        cps = []
        for k in range(n):
            bf_refs[k][...] = w_refs[k][...].astype(BF16)
            cp = pltpu.make_async_copy(bf_refs[k], _shard_window(out_refs[k], kinds[k], 2 * x + y), sems.at[k])
            cp.start()
            cps.append(cp)
        for cp in cps:
            cp.wait()

    return pl.pallas_call(
        body, name="cast_weights",
        out_shape=[jax.ShapeDtypeStruct(s, BF16) for s in full_shapes],
        in_specs=[VM] * n, out_specs=[ANY] * n,
        scratch_shapes=[pltpu.VMEM(s.shape, BF16) for s in shards] + [pltpu.SemaphoreType.DMA((n,))],
        compiler_params=_params(),
    )(*shards)


def _gather_copies(refs, kinds, sems_a, sems_b):
    x, y, c = _my_place()
    chips = [(1 - x, y), (x, 1 - y), (1 - x, 1 - y)]
    set_a, set_b = [], []
    for j, chip in enumerate(chips):
        for k in range(len(refs)):
            if refs[k] is None:
                continue
            for flip in ((0,) if k == 0 else (0, 1)):
                win = _landed(refs[k], kinds[k], (x, y), c)
                sems, idx = (sems_a, j) if k == 0 else (sems_b, ((k - 1) * 3 + j) * 2 + flip)
                mine = _landed(refs[k], kinds[k], chip, (c + flip) % 2)
                (set_a if k == 0 else set_b).append((
                    pltpu.make_async_remote_copy(
                        src_ref=win, dst_ref=win, send_sem=sems[0].at[idx], recv_sem=sems[1].at[idx],
                        device_id=(*chip, (c + flip) % 2), device_id_type=MESH),
                    pltpu.make_async_remote_copy(
                        src_ref=mine, dst_ref=mine, send_sem=sems[0].at[idx], recv_sem=sems[1].at[idx],
                        device_id=(*chip, (c + flip) % 2), device_id_type=MESH)))
    return set_a, set_b


_SPLIT = dict(has_side_effects=pltpu.SideEffectType.DATAFLOW_SIDE_EFFECTING)


def _gather_weights_start(fulls, kinds, after):
    n = len(fulls)
    hbm = pl.BlockSpec(memory_space=pltpu.HBM)
    sem = pl.BlockSpec(memory_space=pltpu.SEMAPHORE)
    nb = (n - 1) * 6

    def body(*refs):
        in_refs = refs[:n]
        sa, ra, sb, rb = refs[n + 1:n + 5]
        token = refs[-1]
        set_a, set_b = _gather_copies(in_refs, kinds, (sa, ra), (sb, rb))
        for out_cp, _ in set_a + set_b:
            out_cp.start()
        token[...] = jnp.zeros_like(token)

    out = pl.pallas_call(
        body, name="gather_weights_start",
        out_shape=[pltpu.SemaphoreType.DMA((3,)), pltpu.SemaphoreType.DMA((3,)),
                   pltpu.SemaphoreType.DMA((nb,)), pltpu.SemaphoreType.DMA((nb,))]
        + [pltpu.HBM(f.shape, f.dtype) for f in fulls] + [jax.ShapeDtypeStruct((8, LANES), F32)],
        in_specs=[hbm] * n + [ANY], out_specs=[sem] * 4 + [hbm] * n + [VM],
        input_output_aliases={k: 4 + k for k in range(n)},
        compiler_params=pltpu.CompilerParams(**_SPLIT),
    )(*[pltpu.with_memory_space_constraint(f, pltpu.HBM) for f in fulls], after)
    return (out[0], out[1]), (out[2], out[3]), out[4:4 + n], out[-1]


def _gather_weights_wait(fulls, kinds, sems, which, after, name):
    n = len(fulls)
    hbm = pl.BlockSpec(memory_space=pltpu.HBM)
    sem = pl.BlockSpec(memory_space=pltpu.SEMAPHORE)
    keep = [0] if which == 0 else list(range(1, n))

    def body(*refs):
        m = len(keep)
        full_refs = [None] * n
        for pos_, k in enumerate(keep):
            full_refs[k] = refs[pos_]
        s_ref, r_ref = refs[m:m + 2]
        if which == 0:
            sets = _gather_copies([full_refs[0]], kinds[:1], (s_ref, r_ref), None)[0]
        else:
            sets = _gather_copies([None] + [full_refs[k] for k in keep], kinds, None, (s_ref, r_ref))[1]
        for out_cp, in_cp in sets:
            out_cp.wait_send()
            in_cp.wait_recv()

    out = pl.pallas_call(
        body, name=name,
        out_shape=[pltpu.HBM(fulls[k].shape, fulls[k].dtype) for k in keep],
        in_specs=[hbm] * len(keep) + [sem, sem] + [ANY] * len(after), out_specs=[hbm] * len(keep),
        input_output_aliases={i: i for i in range(len(keep))},
        compiler_params=pltpu.CompilerParams(**_SPLIT),
    )(*[fulls[k] for k in keep], sems[0], sems[1], *after)
    return list(out)


def _pass_to_sibling(w_full, kind):
    def body(w_in_ref, w_ref, send_sems, recv_sems):
        del w_in_ref
        x, y, c = _my_place()
        chips = [(1 - x, y), (x, 1 - y), (1 - x, 1 - y)]
        cps = []
        for j, chip in enumerate(chips):
            win = _landed(w_ref, kind, chip, c)
            cp = pltpu.make_async_remote_copy(
                src_ref=win, dst_ref=win, send_sem=send_sems.at[j], recv_sem=recv_sems.at[j],
                device_id=(x, y, 1 - c), device_id_type=MESH)
            cp.start()
            cps.append(cp)
        for j, chip in enumerate(chips):
            theirs = _landed(w_ref, kind, chip, 1 - c)
            pltpu.make_async_remote_copy(
                src_ref=theirs, dst_ref=theirs, send_sem=send_sems.at[j], recv_sem=recv_sems.at[j],
                device_id=(x, y, 1 - c), device_id_type=MESH).wait_recv()
        for cp in cps:
            cp.wait_send()

    return pl.pallas_call(
        body, name="pass_to_sibling",
        out_shape=jax.ShapeDtypeStruct(w_full.shape, w_full.dtype),
        in_specs=[ANY], out_specs=ANY, input_output_aliases={0: 0},
        scratch_shapes=[pltpu.SemaphoreType.DMA((3,)), pltpu.SemaphoreType.DMA((3,))],
    )(w_full)


def _reduce_pair_exchange(grads, kinds):
    n = len(grads)
    half_shapes = [(g.shape[0] // 2, g.shape[1]) if kind == "col" else (g.shape[0], g.shape[1] // 2)
                   for g, kind in zip(grads, kinds)]

    def body(*refs):
        g_refs, pa_refs = refs[:n], refs[n:2 * n]
        send_sems, recv_sems = refs[2 * n:]
        x, y, c = _my_place()
        cps = []
        for k in range(n):
            cp = pltpu.make_async_remote_copy(
                src_ref=_half_window(g_refs[k], kinds[k], 1 - c), dst_ref=pa_refs[k],
                send_sem=send_sems.at[k], recv_sem=recv_sems.at[k],
                device_id=(x, y, 1 - c), device_id_type=MESH)
            cp.start()
            cps.append(cp)
        for cp in cps:
            cp.wait()

    return pl.pallas_call(
        body, name="reduce_pair_exchange",
        out_shape=[jax.ShapeDtypeStruct(s, F32) for s in half_shapes],
        in_specs=[ANY] * n, out_specs=[ANY] * n,
        scratch_shapes=[pltpu.SemaphoreType.DMA((n,)), pltpu.SemaphoreType.DMA((n,))],
    )(*grads)


def _row_tile(rows, cols, itemsize=4, target=2 * 1024 * 1024, mult=16):
    t = rows
    while t % 2 == 0 and t // 2 >= mult and (t // 2) % mult == 0 and t * cols * itemsize > target:
        t //= 2
    return t


def _reduce_pair_sum(g, pa, kind, c_arr, name):
    hr, hc_ = pa.shape
    tr = _row_tile(hr, hc_)
    nb = hr // tr

    def body(c_ref, g_ref, pa_ref, o_ref):
        o_ref[...] = (g_ref[...] + pa_ref[...]).astype(BF16)

    if kind == "col":
        g_map = lambda i, c_ref: (c_ref[0] * nb + i, 0)
    else:
        g_map = lambda i, c_ref: (i, c_ref[0])
    return pl.pallas_call(
        body, name=name,
        grid_spec=pltpu.PrefetchScalarGridSpec(
            num_scalar_prefetch=1, grid=(nb,),
            in_specs=[pl.BlockSpec((tr, hc_), g_map), pl.BlockSpec((tr, hc_), lambda i, c_ref: (i, 0))],
            out_specs=pl.BlockSpec((tr, hc_), lambda i, c_ref: (i, 0))),
        out_shape=jax.ShapeDtypeStruct((hr, hc_), BF16),
        compiler_params=_params(("parallel",)),
    )(c_arr, g, pa)


def _half_shard_shape(full_shape, kind):
    r, c = full_shape
    return (r // 2, c // 4) if kind == "col" else (r // 4, c // 2)


def _to_owner_copies(h_refs, land_refs, send_sems, recv_sems, kinds):
    n = len(h_refs)
    x, y, c = _my_place()
    chips = [(1 - x, y), (x, 1 - y), (1 - x, 1 - y)]
    cps = []
    for j, chip in enumerate(chips):
        pj = 2 * chip[0] + chip[1]
        for k in range(n):
            cps.append(pltpu.make_async_remote_copy(
                src_ref=_shard_window(h_refs[k], kinds[k], pj), dst_ref=land_refs[k].at[j],
                send_sem=send_sems.at[j * n + k], recv_sem=recv_sems.at[j * n + k],
                device_id=(*chip, c), device_id_type=MESH))
    return cps


def _reduce_to_owner_start(halves, kinds, full_shapes):
    n = len(halves)
    hs = [_half_shard_shape(fs, kind) for fs, kind in zip(full_shapes, kinds)]
    hbm = pl.BlockSpec(memory_space=pltpu.HBM)
    sem = pl.BlockSpec(memory_space=pltpu.SEMAPHORE)

    def body(*refs):
        h_refs, land_refs = refs[:n], refs[n:2 * n]
        send_sems, recv_sems = refs[2 * n:2 * n + 2]
        token = refs[-1]
        for cp in _to_owner_copies(h_refs, land_refs, send_sems, recv_sems, kinds):
            cp.start()
        token[...] = jnp.zeros_like(token)

    lands = [pltpu.with_memory_space_constraint(lax.empty((3,) + s, BF16), pltpu.HBM) for s in hs]
    out = pl.pallas_call(
        body, name="reduce_to_owner_start",
        out_shape=[pltpu.SemaphoreType.DMA((3 * n,)), pltpu.SemaphoreType.DMA((3 * n,))]
        + [pltpu.HBM(h.shape, h.dtype) for h in halves] + [pltpu.HBM((3,) + s, BF16) for s in hs]
        + [jax.ShapeDtypeStruct((8, LANES), F32)],
        in_specs=[hbm] * (2 * n), out_specs=[sem, sem] + [hbm] * (2 * n) + [VM],
        input_output_aliases={k: 2 + k for k in range(2 * n)},
        compiler_params=pltpu.CompilerParams(has_side_effects=pltpu.SideEffectType.DATAFLOW_SIDE_EFFECTING),
    )(*[pltpu.with_memory_space_constraint(h, pltpu.HBM) for h in halves], *lands)
    return out[0], out[1], out[2:2 + n], out[2 + n:2 + 2 * n], out[-1]


def _reduce_to_owner_wait(send_sems, recv_sems, halves, lands, after, kinds):
    n = len(halves)
    hbm = pl.BlockSpec(memory_space=pltpu.HBM)
    sem = pl.BlockSpec(memory_space=pltpu.SEMAPHORE)

    def body(*refs):
        h_refs, land_refs = refs[:n], refs[n:2 * n]
        send_s, recv_s = refs[2 * n:2 * n + 2]
        for cp in _to_owner_copies(h_refs, land_refs, send_s, recv_s, kinds):
            cp.wait_send()
            cp.wait_recv()

    out = pl.pallas_call(
        body, name="reduce_to_owner_wait",
        out_shape=[pltpu.HBM(h.shape, h.dtype) for h in halves] + [pltpu.HBM(l.shape, l.dtype) for l in lands],
        in_specs=[hbm] * (2 * n) + [sem, sem, ANY], out_specs=[hbm] * (2 * n),
        input_output_aliases={k: k for k in range(2 * n)},
        compiler_params=pltpu.CompilerParams(has_side_effects=pltpu.SideEffectType.DATAFLOW_SIDE_EFFECTING),
    )(*halves, *lands, send_sems, recv_sems, after)
    return out[:n], out[n:]


def _reduce_finish(halves, recvd, kinds, full_shapes):
    n = len(halves)
    hs = [_half_shard_shape(fs, kind) for fs, kind in zip(full_shapes, kinds)]
    shard_shapes = [(fs[0], fs[1] // 4) if kind == "col" else (fs[0] // 4, fs[1])
                    for fs, kind in zip(full_shapes, kinds)]

    def body(*refs):
        h_refs, rc_refs, gs_refs = refs[:n], refs[n:2 * n], refs[2 * n:3 * n]
        own_refs, gh_refs = refs[3 * n:4 * n], refs[4 * n:5 * n]
        in_sems, loc_sems, send_sems, recv_sems = refs[5 * n:]
        x, y, c = _my_place()
        p = 2 * x + y
        loads = []
        for k in range(n):
            cp = pltpu.make_async_copy(_shard_window(h_refs[k], kinds[k], p), own_refs[k], in_sems.at[k])
            cp.start()
            loads.append(cp)
        outs = []
        for k in range(n):
            loads[k].wait()
            gh_refs[k][...] = (own_refs[k][...].astype(F32) + rc_refs[k][0].astype(F32)
                               + rc_refs[k][1].astype(F32) + rc_refs[k][2].astype(F32))
            dst = _half_window(gs_refs[k], kinds[k], c)
            lc = pltpu.make_async_copy(gh_refs[k], dst, loc_sems.at[k])
            lc.start()
            rc = pltpu.make_async_remote_copy(
                src_ref=gh_refs[k], dst_ref=dst, send_sem=send_sems.at[k], recv_sem=recv_sems.at[k],
                device_id=(x, y, 1 - c), device_id_type=MESH)
            rc.start()
            outs.append((lc, rc))
        for k, (lc, rc) in enumerate(outs):
            lc.wait()
            rc.wait_send()
            pltpu.make_async_remote_copy(
                src_ref=gh_refs[k], dst_ref=_half_window(gs_refs[k], kinds[k], 1 - c),
                send_sem=send_sems.at[k], recv_sem=recv_sems.at[k],
                device_id=(x, y, 1 - c), device_id_type=MESH).wait_recv()

    return pl.pallas_call(
        body, name="reduce_finish",
        out_shape=[jax.ShapeDtypeStruct(s, F32) for s in shard_shapes],
        in_specs=[ANY] * n + [VM] * n, out_specs=[ANY] * n,
        scratch_shapes=[pltpu.VMEM(s, BF16) for s in hs] + [pltpu.VMEM(s, F32) for s in hs]
        + [pltpu.SemaphoreType.DMA((n,)) for _ in range(4)],
        compiler_params=_params(),
    )(*halves, *recvd)


def _mod_part(c_all, w_ada_l, b_l):
    def body(c_ref, w_ref, b_ref, o_ref):
        o_ref[...] = _dot(c_ref[...].astype(BF16), w_ref[...].astype(BF16)) + b_ref[...]

    return pl.pallas_call(
        body, name="mod_part", out_shape=jax.ShapeDtypeStruct((8, w_ada_l.shape[1]), F32),
        in_specs=[VM, VM, VM], out_specs=VM, compiler_params=_params(),
    )(c_all, w_ada_l, b_l)


def _sum_devices(parts, m):
    def body(p_ref, o_ref):
        acc = p_ref[0:m, :]
        for d in range(1, 8):
            acc = acc + p_ref[d * m:(d + 1) * m, :]
        o_ref[...] = acc

    return pl.pallas_call(
        body, name="sum_devices", out_shape=jax.ShapeDtypeStruct((m, parts.shape[1]), F32),
        in_specs=[VM], out_specs=VM, compiler_params=_params(),
    )(parts)


def _grad_w_ada(c_all_t, dmod_l):
    d, w = c_all_t.shape[0], dmod_l.shape[1]

    def body(ct_ref, dm_ref, o_ref):
        acc = ct_ref[:, 0:1] * dm_ref[0:1, :]
        for b in range(1, 8):
            acc = acc + ct_ref[:, b:b + 1] * dm_ref[b:b + 1, :]
        o_ref[...] = acc

    return pl.pallas_call(
        body, name="grad_w_ada", out_shape=jax.ShapeDtypeStruct((d, w), F32),
        in_specs=[VM, VM], out_specs=VM, compiler_params=_params(),
    )(c_all_t, dmod_l)


def _adamw_math(w, g, m, v):
    nm = ADAM_B1 * m + (1.0 - ADAM_B1) * g
    nv = ADAM_B2 * v + (1.0 - ADAM_B2) * (g * g)
    m_hat = nm / (1.0 - ADAM_B1 ** ADAM_STEP)
    v_hat = nv / (1.0 - ADAM_B2 ** ADAM_STEP)
    return -ADAM_LR * (m_hat / (jnp.sqrt(v_hat) + ADAM_EPS) + ADAM_WD * w), nm, nv


def _adamw_small(ws, g8, ms, vs):
    shapes = [jax.ShapeDtypeStruct(w.shape, F32) for w in ws]

    def body(*refs):
        w_refs, g_ref, m_refs, v_refs, outs = refs[0:6], refs[6], refs[7:13], refs[13:19], refs[19:]
        for idx in range(6):
            if idx < 5:
                g = g_ref[idx:idx + 1, :]
            else:
                g = jnp.concatenate([g_ref[5:6, :], g_ref[6:7, :], g_ref[7:8, :]], axis=1)
            res = (g,) + _adamw_math(w_refs[idx][...], g, m_refs[idx][...], v_refs[idx][...])
            for fam in range(4):
                outs[6 * fam + idx][...] = res[fam]

    return pl.pallas_call(
        body, name="adamw_small", out_shape=shapes * 4,
        in_specs=[VM] * 19, out_specs=[VM] * 24, compiler_params=_params(),
    )(*ws, g8, *ms, *vs)


def _adamw(w, g, m, v, name):
    r, c = w.shape
    tr = _row_tile(r, c, target=1024 * 1024, mult=8)

    def body(w_ref, g_ref, m_ref, v_ref, d_ref, nm_ref, nv_ref):
        d_ref[...], nm_ref[...], nv_ref[...] = _adamw_math(w_ref[...], g_ref[...], m_ref[...], v_ref[...])

    spec = pl.BlockSpec((tr, c), lambda i: (i, 0))
    return pl.pallas_call(
        body, name=name, grid=(r // tr,),
        out_shape=[jax.ShapeDtypeStruct((r, c), F32)] * 3,
        in_specs=[spec] * 4, out_specs=[spec] * 3,
        compiler_params=_params(("parallel",)),
    )(w, g, m, v)


def _rope_coefficients(pos, ropetab, ts):
    s_len = pos.shape[0]

    def body(pos_ref, tab_ref, o_ref):
        ang = pos_ref[...].astype(F32) * tab_ref[0:1, :]
        cs, sn = jnp.cos(ang), jnp.sin(ang)
        o_ref[0] = jnp.where(tab_ref[3:4, :] > 0, cs, 1.0)
        o_ref[1] = -sn * tab_ref[1:2, :]
        o_ref[2] = sn * tab_ref[2:3, :]

    return pl.pallas_call(
        body, name="rope_coefficients", grid=(s_len // ts,),
        in_specs=[pl.BlockSpec((ts, 1), lambda i: (i, 0)), pl.BlockSpec((8, LANES), lambda i: (0, 0))],
        out_specs=pl.BlockSpec((3, ts, LANES), lambda i: (0, i, 0)),
        out_shape=jax.ShapeDtypeStruct((3, s_len, LANES), F32),
        compiler_params=_params(("parallel",)),
    )(pos, ropetab)


def _deinterleave_store(vals, slab, out_ref, d, ts, dtype):
    if d == 1:
        for s in range(4):
            out_ref[0, :, s * LANES:(s + 1) * LANES] = vals[s].astype(dtype)
        return
    for s in range(4):
        slab[s] = vals[s]
    if d == 16:
        q = ts // 4
        for s in range(4):
            for r1 in range(4):
                slab[4 + s, r1 * q:(r1 + 1) * q, :] = slab[s, pl.ds(r1, q, stride=4), :]
        for r in range(d):
            r1, r2 = r % 4, r // 4
            for s in range(4):
                out_ref[r, :, s * LANES:(s + 1) * LANES] = slab[
                    4 + s, pl.ds(r1 * q + r2, ts // d, stride=4), :].astype(dtype)
        return
    for r in range(d):
        for s in range(4):
            out_ref[r, :, s * LANES:(s + 1) * LANES] = slab[s, pl.ds(r, ts // d, stride=d), :].astype(dtype)


def _interleave_load(blk_ref, slab, d, ts):
    if d == 1:
        return [blk_ref[0, :, s * LANES:(s + 1) * LANES] for s in range(4)]
    if d == 16:
        q = ts // 4
        for r in range(d):
            r1, r2 = r % 4, r // 4
            for s in range(4):
                slab[4 + s, pl.ds(r1 * q + r2, ts // d, stride=4), :] = blk_ref[r, :, s * LANES:(s + 1) * LANES]
        for s in range(4):
            for r1 in range(4):
                slab[s, pl.ds(r1, q, stride=4), :] = slab[4 + s, r1 * q:(r1 + 1) * q, :]
        return [slab[s] for s in range(4)]
    for r in range(d):
        for s in range(4):
            slab[s, pl.ds(r, ts // d, stride=d), :] = blk_ref[r, :, s * LANES:(s + 1) * LANES]
    return [slab[s] for s in range(4)]


def _norm_modulate(x, modv, norm_g, ts):
    s_len, d_model = x.shape

    def body(x_ref, mod_ref, g_ref, h_ref, ht_ref):
        xv = x_ref[...]
        r = lax.rsqrt(jnp.mean(xv * xv, axis=-1, keepdims=True) + EPS)
        h = (xv * r) * g_ref[...] * (1.0 + mod_ref[1:2, :]) + mod_ref[0:1, :]
        h_ref[...] = h.astype(BF16)
        ht_ref[...] = h.T.astype(BF16)

    return pl.pallas_call(
        body, name="norm_modulate", grid=(s_len // ts,),
        in_specs=[pl.BlockSpec((ts, d_model), lambda i: (i, 0)), pl.BlockSpec((8, d_model), lambda i: (0, 0)),
                  pl.BlockSpec((1, d_model), lambda i: (0, 0))],
        out_specs=[pl.BlockSpec((ts, d_model), lambda i: (i, 0)), pl.BlockSpec((d_model, ts), lambda i: (0, i))],
        out_shape=[jax.ShapeDtypeStruct((s_len, d_model), BF16), jax.ShapeDtypeStruct((d_model, s_len), BF16)],
        compiler_params=_params(("parallel",)),
    )(x, modv, norm_g)


def _plain_projection(h, w_in, first_tile, n_tiles, ts, name):
    s_len, d_model = h.shape
    ct = COL_TILE

    def body(h_ref, w_ref, o_ref):
        o_ref[...] = _dot(h_ref[...], w_ref[...])

    return pl.pallas_call(
        body, name=name, grid=(s_len // ts, n_tiles),
        in_specs=[pl.BlockSpec((ts, d_model), lambda i, j: (i, 0)),
                  pl.BlockSpec((d_model, ct), lambda i, j: (0, first_tile + j))],
        out_specs=pl.BlockSpec((ts, ct), lambda i, j: (i, j)),
        out_shape=jax.ShapeDtypeStruct((s_len, n_tiles * ct), F32),
        compiler_params=_params(("parallel", "arbitrary")),
    )(h, w_in)


def _qkv_projection(h, rope, w_in, gi, ts):
    s_len, d_model = h.shape
    dil = DILATIONS[gi]
    nc = 3 * d_model // COL_TILE
    hr = ts // 2
    assert COL_TILE == ATT_W and hr % (16 * dil) == 0

    def body(h_ref, rope_ref, wq_ref, wk_ref, wv_ref, o_ref, slab):
        rc, ra, rb = rope_ref.at[0], rope_ref.at[1], rope_ref.at[2]
        w_refs = (wq_ref, wk_ref, wv_ref)
        units = [(w, half) for w in range(3) for half in range(2)]

        def matmul(w, half):
            return _dot(h_ref[half * hr:(half + 1) * hr, :], w_refs[w][...])

        def finish(w, half, res):
            rows = slice(half * hr, (half + 1) * hr)
            vals = []
            for s in range(4):
                t = res[:, s * LANES:(s + 1) * LANES]
                if w < 2:
                    t = (t * rc[rows, :] + pltpu.roll(t, LANES - 8, 1) * ra[rows, :]
                         + pltpu.roll(t, 8, 1) * rb[rows, :])
                vals.append(t)
            out = o_ref.at[w, :, half * (hr // dil):(half + 1) * (hr // dil), :]
            _deinterleave_store(vals, slab.at[half], out, dil, hr, BF16)

        res_next = matmul(*units[0])
        for ui, (w, half) in enumerate(units):
            res = res_next
            if ui + 1 < len(units):
                res_next = matmul(*units[ui + 1])
            finish(w, half, res)

    def w_spec(w):
        return pl.BlockSpec((d_model, COL_TILE), lambda i: (0, nc + 3 * w + gi))

    return pl.pallas_call(
        body, name="qkv_projection_%d" % dil, grid=(s_len // ts,),
        in_specs=[pl.BlockSpec((ts, d_model), lambda i: (i, 0)), pl.BlockSpec((3, ts, LANES), lambda i: (0, i, 0)),
                  w_spec(0), w_spec(1), w_spec(2)],
        out_specs=pl.BlockSpec((3, dil, ts // dil, ATT_W), lambda i: (0, 0, i, 0)),
        out_shape=jax.ShapeDtypeStruct((3, dil, s_len // dil, ATT_W), BF16),
        scratch_shapes=[pltpu.VMEM((2, 8, hr, LANES), F32)],
        compiler_params=_params(("parallel",)),
    )(h, rope, w_in, w_in, w_in)


def _layernorm_stats(u1):
    mu = jnp.mean(u1, axis=-1, keepdims=True)
    xc = u1 - mu
    rstd = lax.rsqrt(jnp.mean(xc * xc, axis=-1, keepdims=True) + EPS)
    return xc * rstd, rstd


def _shifted_copies(win, shf, ts):
    rows = ts + HALO - 8
    for b in range(1, 8):
        shf[b - 1, 0:rows, :] = win[pl.ds(b, rows), :]


def _tap(win, shf, off, r0, rows):
    a, b = divmod(off, 8)
    start = 8 * a + r0
    if b == 0:
        return win[start:start + rows, :]
    return shf[b - 1, start:start + rows, :]


def _conv_forward(p_conv, conv_w, conv_b, ln_g, ln_b, w_co, ts):
    s_len, d3 = p_conv.shape
    dm = d3 // 3

    def body(p_ref, cw_ref, cb_ref, g_ref, b_ref, w_ref, y_ref, u0_ref, u1_ref, win, shf):
        i = pl.program_id(0)

        @pl.when(i == 0)
        def _():
            win[0:HALO, :] = jnp.zeros((HALO, dm), F32)

        a, b, z = p_ref[:, 0:dm], p_ref[:, dm:2 * dm], p_ref[:, 2 * dm:3 * dm]
        u0 = a * _sig(b)
        win[HALO:HALO + ts, :] = u0
        u0_ref[...] = u0
        _shifted_copies(win, shf, ts)
        for r0 in range(0, ts, CONV_ROWS_FWD):
            acc = jnp.broadcast_to(cb_ref[...], (CONV_ROWS_FWD, dm))
            for k in range(CONV_K):
                acc = acc + cw_ref[k:k + 1, :] * _tap(win, shf, HALO - (CONV_K - 1) + k, r0, CONV_ROWS_FWD)
            u1_ref[r0:r0 + CONV_ROWS_FWD, :] = acc
        xh, _ = _layernorm_stats(u1_ref[...])
        u2 = xh * g_ref[...] + b_ref[...]
        a_conv = (u2 * _sig(u2)) * (z * _sig(z))
        y_ref[...] = _dot(a_conv.astype(BF16), w_ref[...])
        win[0:HALO, :] = win[ts:ts + HALO, :]

    row = pl.BlockSpec((1, dm), lambda i: (0, 0))
    tile = pl.BlockSpec((ts, dm), lambda i: (i, 0))
    return pl.pallas_call(
        body, name="conv_forward", grid=(s_len // ts,),
        in_specs=[pl.BlockSpec((ts, d3), lambda i: (i, 0)),
                  pl.BlockSpec((HALO, dm), lambda i: (0, 0)), row, row, row,
                  pl.BlockSpec((dm, dm), lambda i: (0, 0))],
        out_specs=[tile, tile, tile],
        out_shape=[jax.ShapeDtypeStruct((s_len, dm), F32)] * 3,
        scratch_shapes=[pltpu.VMEM((ts + HALO, dm), F32), pltpu.VMEM((7, ts + HALO - 8, dm), F32)],
        compiler_params=_params(("arbitrary",)),
    )(p_conv, conv_w, conv_b, ln_g, ln_b, w_co)


def _conv_backward(dp, dyc, p_conv, u0, u1, conv_w, ln_g, ln_b, w_co, ts):
    s_len, d3 = p_conv.shape
    dm = d3 // 3
    nt = s_len // ts
    hb = ts // HALO

    def body(dp_in, dy_ref, p_ref, u0_ref, uh_ref, u1_ref, cw_ref, g_ref, b_ref, w_ref,
             dp_ref, gw_ref, gs_ref, gcw_ref, dwin, uwin, shf):
        del dp_in
        i = pl.program_id(0)
        ti = nt - 1 - i

        @pl.when(i == 0)
        def _():
            gw_ref[...] = jnp.zeros_like(gw_ref)
            gs_ref[...] = jnp.zeros_like(gs_ref)
            gcw_ref[...] = jnp.zeros_like(gcw_ref)
            dwin[ts:ts + HALO, :] = jnp.zeros((HALO, dm), F32)

        dy = dy_ref[...]
        z = p_ref[:, 2 * dm:3 * dm]
        d_ac = _dot_nt(dy, w_ref[...])
        xh, rstd = _layernorm_stats(u1_ref[...])
        u2 = xh * g_ref[...] + b_ref[...]
        sg2, sgz = _sig(u2), _sig(z)
        u3, sz = u2 * sg2, z * sgz
        gw_ref[...] += _dot_tn((u3 * sz).astype(BF16), dy)
        d_z = d_ac * u3 * _dsilu(z, sgz)
        d_u2 = d_ac * sz * _dsilu(u2, sg2)
        gs_ref[0:1, :] += jnp.sum(d_u2 * xh, axis=0, keepdims=True)
        gs_ref[1:2, :] += jnp.sum(d_u2, axis=0, keepdims=True)
        dxh = d_u2 * g_ref[...]
        d_u1 = rstd * (dxh - jnp.mean(dxh, axis=-1, keepdims=True)
                       - xh * jnp.mean(dxh * xh, axis=-1, keepdims=True))
        gs_ref[2:3, :] += jnp.sum(d_u1, axis=0, keepdims=True)
        dwin[0:ts, :] = d_u1
        uwin[0:HALO, :] = jnp.where(ti == 0, 0.0, uh_ref[...])
        uwin[HALO:HALO + ts, :] = u0_ref[...]
        dp_ref[:, 2 * dm:3 * dm] = d_z.astype(BF16)
        _shifted_copies(uwin, shf, ts)
        for k in range(CONV_K):
            part = jnp.zeros((CONV_ROWS_BWD, dm), F32)
            for r0 in range(0, ts, CONV_ROWS_BWD):
                part = part + dwin[r0:r0 + CONV_ROWS_BWD, :] * _tap(uwin, shf, HALO - (CONV_K - 1) + k, r0,
                                                                    CONV_ROWS_BWD)
            gcw_ref[k:k + 1, :] += jnp.sum(part, axis=0, keepdims=True)
        _shifted_copies(dwin, shf, ts)
        for r0 in range(0, ts, CONV_ROWS_BWD):
            d_u0 = jnp.zeros((CONV_ROWS_BWD, dm), F32)
            for k in range(CONV_K):
                d_u0 = d_u0 + cw_ref[k:k + 1, :] * _tap(dwin, shf, CONV_K - 1 - k, r0, CONV_ROWS_BWD)
            rows = slice(r0, r0 + CONV_ROWS_BWD)
            sgb = _sig(p_ref[rows, dm:2 * dm])
            dp_ref[rows, 0:dm] = (d_u0 * sgb).astype(BF16)
            dp_ref[rows, dm:2 * dm] = (d_u0 * p_ref[rows, 0:dm] * sgb * (1.0 - sgb)).astype(BF16)
        dwin[ts:ts + HALO, :] = dwin[0:HALO, :]

    rev = lambda i: (nt - 1 - i, 0)
    row = pl.BlockSpec((1, dm), lambda i: (0, 0))
    tile = pl.BlockSpec((ts, dm), rev)
    return pl.pallas_call(
        body, name="conv_backward", grid=(nt,),
        in_specs=[ANY, tile, pl.BlockSpec((ts, d3), rev), tile,
                  pl.BlockSpec((HALO, dm), lambda i: (jnp.maximum((nt - 1 - i) * hb - 1, 0), 0)),
                  tile, pl.BlockSpec((HALO, dm), lambda i: (0, 0)), row, row,
                  pl.BlockSpec((dm, dm), lambda i: (0, 0))],
        out_specs=[pl.BlockSpec((ts, d3), rev), pl.BlockSpec((dm, dm), lambda i: (0, 0)),
                   pl.BlockSpec((8, dm), lambda i: (0, 0)), pl.BlockSpec((HALO, dm), lambda i: (0, 0))],
        out_shape=[jax.ShapeDtypeStruct(dp.shape, BF16), jax.ShapeDtypeStruct((dm, dm), F32),
                   jax.ShapeDtypeStruct((8, dm), F32), jax.ShapeDtypeStruct((HALO, dm), F32)],
        input_output_aliases={0: 0},
        scratch_shapes=[pltpu.VMEM((ts + HALO, dm), F32), pltpu.VMEM((ts + HALO, dm), F32),
                        pltpu.VMEM((7, ts + HALO - 8, dm), F32)],
        compiler_params=_params(("arbitrary",)),
    )(dp, dyc, p_conv, u0, u0, u1, conv_w, ln_g, ln_b, w_co)


def _att_masks():
    head0 = lax.broadcasted_iota(jnp.int32, (BLK, LANES), 1) < 64
    col = lax.broadcasted_iota(jnp.int32, (BLK, 4 * BLK), 1)
    row = lax.broadcasted_iota(jnp.int32, (BLK, 4 * BLK), 0)
    kj = col % BLK
    prev = jnp.where(col < 2 * BLK, 1, 0)
    band = jnp.where(col < 2 * BLK, kj - row, row - kj)
    return head0, band, prev


def _fill_block_diagonal(dst, slab, src_ref, halo_ref, head0, nq):
    sl = slice(slab * LANES, (slab + 1) * LANES)
    for b in range(nq + 1):
        blk = halo_ref[:, sl] if b == 0 else src_ref[(b - 1) * BLK:b * BLK, sl]
        base = (slab * (nq + 1) + b) * 2 * BLK
        zero = jnp.zeros_like(blk)
        dst[base:base + BLK, :] = jnp.where(head0, blk, zero)
        dst[base + BLK:base + 2 * BLK, :] = jnp.where(head0, zero, blk)


def _attention_forward(qkv, seq_len, qt, name):
    s_len = qkv.shape[1]
    nq = qt // BLK
    tiles_per_seq = seq_len // qt

    def body(q_ref, k_ref, v_ref, kh_ref, vh_ref, o_ref, lse_ref, kbd, vbd):
        first = jnp.where((pl.program_id(0) % tiles_per_seq) == 0, 4 * BLK, 0)
        head0, band, prev = _att_masks()
        band_first = band - prev * first
        for p in range(4):
            _fill_block_diagonal(kbd, p, k_ref, kh_ref, head0, nq)
            _fill_block_diagonal(vbd, p, v_ref, vh_ref, head0, nq)
        units = [(p, n) for p in range(4) for n in range(nq)]

        def keys_of(p, n):
            base = (p * (nq + 1) + n) * 2 * BLK
            return slice(base, base + 4 * BLK)

        def scores(p, n):
            q2 = q_ref[n * BLK:(n + 1) * BLK, p * LANES:(p + 1) * LANES] * 0.125
            return _dot_nt(q2, kbd[keys_of(p, n), :])

        def finish(p, n, o, den, lse):
            rows, sl = slice(n * BLK, (n + 1) * BLK), slice(p * LANES, (p + 1) * LANES)
            o_ref[rows, sl] = o / jnp.where(head0, den[0], den[1])
            lse_ref[rows, sl] = jnp.where(head0, lse[0], lse[1])

        s_next = scores(*units[0])
        pending = None
        for ui, (p, n) in enumerate(units):
            s = s_next
            if ui + 1 < len(units):
                s_next = scores(*units[ui + 1])
            s = jnp.where((band_first if n == 0 else band) >= 0, s, NEG)
            grp = [s[:, g * BLK:(g + 1) * BLK] for g in range(4)]
            ps, den, lse = [None] * 4, [], []
            for h in range(2):
                m = jnp.max(jnp.maximum(grp[h], grp[2 + h]), axis=-1, keepdims=True)
                ps[h], ps[2 + h] = jnp.exp(grp[h] - m), jnp.exp(grp[2 + h] - m)
                dn = jnp.sum(ps[h] + ps[2 + h], axis=-1, keepdims=True)
                den.append(dn)
                lse.append(m + jnp.log(dn))
            pmat = jnp.concatenate([x.astype(BF16) for x in ps], axis=1)
            o = _dot(pmat, vbd[keys_of(p, n), :])
            if pending is not None:
                finish(*pending)
            pending = (p, n, o, den, lse)
        finish(*pending)

    def which(w):
        return pl.BlockSpec((None, qt, ATT_W), lambda i: (w, i, 0))

    def halo(w):
        return pl.BlockSpec((None, BLK, ATT_W), lambda i: (w, jnp.maximum(i * nq - 1, 0), 0))

    out = pl.BlockSpec((qt, ATT_W), lambda i: (i, 0))
    bd = pltpu.VMEM((4 * (nq + 1) * 2 * BLK, LANES), BF16)
    return pl.pallas_call(
        body, name=name, grid=(s_len // qt,),
        in_specs=[which(0), which(1), which(2), halo(1), halo(2)],
        out_specs=[out, out], out_shape=[jax.ShapeDtypeStruct((s_len, ATT_W), F32)] * 2,
        scratch_shapes=[bd, bd],
        compiler_params=_params(("parallel",)),
    )(qkv, qkv, qkv, qkv, qkv)


def _attention_backward(qkv, d_att, lse, delta, seq_len, qt, name):
    s_len = qkv.shape[1]
    nq = qt // BLK
    tiles_per_seq = seq_len // qt
    nblk = s_len // BLK

    def body(q_ref, k_ref, v_ref, kh_ref, vh_ref, do_ref, lse_ref, dl_ref,
             qn_ref, don_ref, lsen_ref, dln_ref, dqkv_ref, kbd, vbd):
        i = pl.program_id(0)
        first = jnp.where((i % tiles_per_seq) == 0, 4 * BLK, 0)
        last = jnp.where((i % tiles_per_seq) == tiles_per_seq - 1, 4 * BLK, 0)
        head0, band, prev = _att_masks()
        band_first = band - prev * first
        band_tail = band[:, 0:2 * BLK] - last
        for p in range(4):
            _fill_block_diagonal(kbd, p, k_ref, kh_ref, head0, nq)
            _fill_block_diagonal(vbd, p, v_ref, vh_ref, head0, nq)
        units = [(p, n) for p in range(4) for n in range(nq + 1)]

        def stage_a(p, n):
            sl = slice(p * LANES, (p + 1) * LANES)
            base = (p * (nq + 1) + n) * 2 * BLK
            if n < nq:
                rows = slice(n * BLK, (n + 1) * BLK)
                q2, do2, lse2, dl2 = q_ref[rows, sl], do_ref[rows, sl], lse_ref[rows, sl], dl_ref[rows, sl]
                keys = slice(base, base + 4 * BLK)
            else:
                q2, do2, lse2, dl2 = qn_ref[:, sl], don_ref[:, sl], lsen_ref[:, sl], dln_ref[:, sl]
                keys = slice(base, base + 2 * BLK)
            s = _dot_nt(q2 * 0.125, kbd[keys, :])
            dp = _dot_nt(do2, vbd[keys, :])
            return q2, do2, lse2, dl2, keys, s, dp

        def stage_b(n, lse2, dl2, s, dp):
            mask = band_tail if n == nq else (band_first if n == 0 else band)
            ps, dss = [], []
            for g in range(s.shape[1] // BLK):
                h = g % 2
                cols = slice(g * BLK, (g + 1) * BLK)
                pg = jnp.exp(jnp.where(mask[:, cols] >= 0, s[:, cols] - lse2[:, h * 64:h * 64 + 1], NEG))
                ps.append(pg.astype(BF16))
                dss.append((pg * (dp[:, cols] - dl2[:, h * 64:h * 64 + 1]) * 0.125).astype(BF16))
            return jnp.concatenate(ps, axis=1), jnp.concatenate(dss, axis=1)

        def heads(r, g):
            return jnp.where(head0, r[g * BLK:(g + 1) * BLK, :], r[(g + 1) * BLK:(g + 2) * BLK, :])

        a_next = stage_a(*units[0])
        carry = None
        for ui, (p, n) in enumerate(units):
            q2, do2, lse2, dl2, keys, s, dp = a_next
            if ui + 1 < len(units):
                a_next = stage_a(*units[ui + 1])
            pmat, dsmat = stage_b(n, lse2, dl2, s, dp)
            sl = slice(p * LANES, (p + 1) * LANES)
            if n < nq:
                dqkv_ref[0, n * BLK:(n + 1) * BLK, sl] = _dot(dsmat, kbd[keys, :])
            dkbd = _dot_tn(dsmat, q2)
            dvbd = _dot_tn(pmat, do2)
            if n > 0:
                prow = slice((n - 1) * BLK, n * BLK)
                dqkv_ref[1, prow, sl] = carry[0] + heads(dkbd, 0)
                dqkv_ref[2, prow, sl] = carry[1] + heads(dvbd, 0)
            if n < nq:
                carry = (heads(dkbd, 2), heads(dvbd, 2))

    def which(w):
        return pl.BlockSpec((None, qt, ATT_W), lambda i: (w, i, 0))

    def prev(w):
        return pl.BlockSpec((None, BLK, ATT_W), lambda i: (w, jnp.maximum(i * nq - 1, 0), 0))

    tile = pl.BlockSpec((qt, ATT_W), lambda i: (i, 0))
    nxt = pl.BlockSpec((BLK, ATT_W), lambda i: (jnp.minimum((i + 1) * nq, nblk - 1), 0))
    nxt_q = pl.BlockSpec((None, BLK, ATT_W), lambda i: (0, jnp.minimum((i + 1) * nq, nblk - 1), 0))
    return pl.pallas_call(
        body, name=name, grid=(s_len // qt,),
        in_specs=[which(0), which(1), which(2), prev(1), prev(2), tile, tile, tile, nxt_q, nxt, nxt, nxt],
        out_specs=pl.BlockSpec((3, qt, ATT_W), lambda i: (0, i, 0)),
        out_shape=jax.ShapeDtypeStruct((3, s_len, ATT_W), F32),
        scratch_shapes=[pltpu.VMEM((4 * (nq + 1) * 2 * BLK, LANES), BF16)] * 2,
        compiler_params=_params(("parallel",)),
    )(qkv, qkv, qkv, qkv, qkv, d_att, lse, delta, qkv, d_att, lse, delta)


def _merge_and_head(o_g, lse_g, p_gate, y_conv, x, tgt, modv, final_g, w_ao, w_o, ts):
    s_len, dm = x.shape
    gw = ATT_W + 2 * dm
    nt = s_len // ts
    gate_off = 3 * dm + 9 * ATT_W
    assert gate_off % gw == 0

    def body(o0, o1, o2, l0, l1, l2, pg_ref, yc_ref, x_ref, t_ref, mod_ref, fg_ref, wao_ref, wo_ref,
             loss_ref, dx_ref, dyc_ref, dpg_ref, da0, da1, da2, ls0, ls1, ls2, de0, de1, de2,
             gwo_ref, gwao_ref, gs_ref, slab):
        i = pl.program_id(0)

        @pl.when(i == 0)
        def _():
            loss_ref[...] = jnp.zeros_like(loss_ref)
            gwo_ref[...] = jnp.zeros_like(gwo_ref)
            gwao_ref[...] = jnp.zeros_like(gwao_ref)
            gs_ref[...] = jnp.zeros_like(gs_ref)

        os_, ls_ = [], []
        for dil, o_ref, l_ref in zip(DILATIONS, (o0, o1, o2), (l0, l1, l2)):
            os_.append(jnp.concatenate(_interleave_load(o_ref, slab, dil, ts), axis=1))
            ls_.append(jnp.concatenate(_interleave_load(l_ref, slab, dil, ts), axis=1))
        mx = jnp.maximum(jnp.maximum(ls_[0], ls_[1]), ls_[2])
        wts = [jnp.exp(l - mx) for l in ls_]
        wsum = wts[0] + wts[1] + wts[2]
        att = (wts[0] * os_[0] + wts[1] * os_[1] + wts[2] * os_[2]) / wsum
        lse_all = mx + jnp.log(wsum)

        z_att, g_conv, g_att = pg_ref[:, 0:ATT_W], pg_ref[:, ATT_W:ATT_W + dm], pg_ref[:, ATT_W + dm:gw]
        sgz = _sig(z_att)
        sz = z_att * sgz
        a_att = (att * sz).astype(BF16)
        y_att = _dot(a_att, wao_ref[...])
        y_conv = yc_ref[...]
        sgc, sga = _sig(g_conv), _sig(g_att)
        merged = (sgc * y_conv + sga * y_att).astype(BF16)
        mo = _dot(merged, wo_ref[...])
        gate = mod_ref[2:3, :]
        x2 = x_ref[...] + gate * mo
        r = lax.rsqrt(jnp.mean(x2 * x2, axis=-1, keepdims=True) + EPS)
        xr = x2 * r
        err = xr * fg_ref[...] - t_ref[...]
        loss_ref[...] += 0.5 * jnp.sum(jnp.mean(err * err, axis=-1, keepdims=True))
        dy = err * (1.0 / dm)
        gs_ref[0:1, :] += jnp.sum(dy * xr, axis=0, keepdims=True)
        dyg = dy * fg_ref[...]
        d_x2 = r * dyg - xr * (r * jnp.mean(dyg * xr, axis=-1, keepdims=True))
        dx_ref[...] = d_x2
        gs_ref[1:2, :] += jnp.sum(d_x2 * mo, axis=0, keepdims=True)
        d_mo = (d_x2 * gate).astype(BF16)
        d_mg = _dot_nt(d_mo, wo_ref[...])
        gwo_ref[...] += _dot_tn(merged, d_mo)
        dyc_ref[...] = (d_mg * sgc).astype(BF16)
        dpg_ref[:, ATT_W:ATT_W + dm] = (d_mg * y_conv * sgc * (1.0 - sgc)).astype(BF16)
        d_ya = (d_mg * sga).astype(BF16)
        dpg_ref[:, ATT_W + dm:gw] = (d_mg * y_att * sga * (1.0 - sga)).astype(BF16)
        gwao_ref[...] += _dot_tn(a_att, d_ya)
        d_aa = _dot_nt(d_ya, wao_ref[...])
        dpg_ref[:, 0:ATT_W] = (d_aa * att * _dsilu(z_att, sgz)).astype(BF16)
        d_att = d_aa * sz
        ri = lax.broadcasted_iota(jnp.int32, (ATT_W, ATT_W), 0) // 64
        ci = lax.broadcasted_iota(jnp.int32, (ATT_W, ATT_W), 1) // 64
        ones = jnp.where(ri == ci, 1.0, 0.0).astype(BF16)
        prod = d_att * att
        hi = prod.astype(BF16)
        lo = (prod - hi.astype(F32)).astype(BF16)
        delta = _dot(hi, ones) + _dot(lo, ones)
        for val, refs, dt in ((d_att, (da0, da1, da2), BF16), (lse_all, (ls0, ls1, ls2), F32),
                              (delta, (de0, de1, de2), F32)):
            vals = [val[:, s * LANES:(s + 1) * LANES] for s in range(4)]
            for dil, ref in zip(DILATIONS, refs):
                _deinterleave_store(vals, slab, ref, dil, ts, dt)

    def grp(dil):
        return pl.BlockSpec((dil, ts // dil, ATT_W), lambda i: (0, i, 0))

    tile = pl.BlockSpec((ts, dm), lambda i: (i, 0))
    gate_tile = pl.BlockSpec((ts, gw), lambda i: (i, 0))
    const = lambda shp: pl.BlockSpec(shp, lambda i: tuple(0 for _ in shp))
    grp_shape = lambda dt: [jax.ShapeDtypeStruct((dil, s_len // dil, ATT_W), dt) for dil in DILATIONS]
    return pl.pallas_call(
        body, name="merge_and_head", grid=(nt,),
        in_specs=[grp(d) for d in DILATIONS] * 2
        + [gate_tile, tile, tile, tile, const((8, dm)), const((1, dm)), const((ATT_W, dm)), const((dm, dm))],
        out_specs=[const((8, LANES)), tile, tile, pl.BlockSpec((ts, gw), lambda i: (i, gate_off // gw))]
        + [grp(d) for d in DILATIONS] * 3
        + [const((dm, dm)), const((ATT_W, dm)), const((8, dm))],
        out_shape=[jax.ShapeDtypeStruct((8, LANES), F32), jax.ShapeDtypeStruct((s_len, dm), F32),
                   jax.ShapeDtypeStruct((s_len, dm), BF16), jax.ShapeDtypeStruct((s_len, gate_off + gw), BF16)]
        + grp_shape(BF16) + grp_shape(F32) + grp_shape(F32)
        + [jax.ShapeDtypeStruct((dm, dm), F32), jax.ShapeDtypeStruct((ATT_W, dm), F32),
           jax.ShapeDtypeStruct((8, dm), F32)],
        scratch_shapes=[pltpu.VMEM((8, ts, LANES), F32)],
        compiler_params=_params(("arbitrary",)),
    )(*o_g, *lse_g, p_gate, y_conv, x, tgt, modv, final_g, w_ao, w_o)


def _qkv_grad_to_tokens(dp, dqkv_g, rope, ts):
    s_len, dm = rope.shape[1], (dp.shape[1] - 10 * ATT_W) // 5
    qw = 3 * ATT_W
    assert (3 * dm) % qw == 0

    def body(dp_in, g0, g1, g2, rope_ref, o_ref, slab):
        del dp_in
        w = pl.program_id(1)
        rc, ra, rb = rope_ref.at[0], rope_ref.at[1], rope_ref.at[2]

        def emit(roped):
            for gi, (dil, g_ref) in enumerate(zip(DILATIONS, (g0, g1, g2))):
                vals = _interleave_load(g_ref, slab, dil, ts)
                for s in range(4):
                    t = vals[s]
                    if roped:
                        t = t * rc[...] + pltpu.roll(t * ra[...], 8, 1) + pltpu.roll(t * rb[...], LANES - 8, 1)
                    col = gi * ATT_W + s * LANES
                    o_ref[:, col:col + LANES] = t.astype(BF16)

        pl.when(w < 2)(lambda: emit(True))
        pl.when(w == 2)(lambda: emit(False))

    return pl.pallas_call(
        body, name="qkv_grad_to_tokens", grid=(s_len // ts, 3),
        in_specs=[ANY] + [pl.BlockSpec((None, dil, ts // dil, ATT_W), lambda i, w: (w, 0, i, 0)) for dil in DILATIONS]
        + [pl.BlockSpec((3, ts, LANES), lambda i, w: (0, i, 0))],
        out_specs=pl.BlockSpec((ts, qw), lambda i, w: (i, 3 * dm // qw + w)),
        out_shape=jax.ShapeDtypeStruct(dp.shape, BF16),
        input_output_aliases={0: 0},
        scratch_shapes=[pltpu.VMEM((8, ts, LANES), F32)],
        compiler_params=_params(("parallel", "arbitrary")),
    )(dp, *dqkv_g, rope)


def _wide_col_tile(cols):
    for width in (5 * COL_TILE, 2 * COL_TILE):
        if cols % width == 0:
            return width
    return COL_TILE


def _input_grad(dp, w_in, x, dx_res, modv, norm_g, ts):
    s_len, dm = x.shape
    ct = next(w * COL_TILE for w in (4, 2, 1) if dp.shape[1] % (w * COL_TILE) == 0)
    nct = dp.shape[1] // ct

    def body(p_ref, w_ref, x_ref, dxr_ref, mod_ref, g_ref, gx_ref, gs_ref, acc):
        i, j = pl.program_id(0), pl.program_id(1)

        @pl.when((i == 0) & (j == 0))
        def _():
            gs_ref[...] = jnp.zeros_like(gs_ref)

        @pl.when(j == 0)
        def _():
            acc[...] = jnp.zeros_like(acc)

        acc[...] += _dot_nt(p_ref[...], w_ref[...])

        @pl.when(j == nct - 1)
        def _():
            d_h = acc[...]
            xv = x_ref[...]
            r = lax.rsqrt(jnp.mean(xv * xv, axis=-1, keepdims=True) + EPS)
            xr = xv * r
            gs_ref[0:1, :] += jnp.sum(d_h, axis=0, keepdims=True)
            gs_ref[1:2, :] += jnp.sum(d_h * (xr * g_ref[...]), axis=0, keepdims=True)
            d_n = d_h * (1.0 + mod_ref[1:2, :])
            gs_ref[2:3, :] += jnp.sum(d_n * xr, axis=0, keepdims=True)
            dxn = d_n * g_ref[...]
            gx_ref[...] = dxr_ref[...] + r * dxn - xr * (r * jnp.mean(dxn * xr, axis=-1, keepdims=True))

    tile = pl.BlockSpec((ts, dm), lambda i, j: (i, 0))
    return pl.pallas_call(
        body, name="input_grad", grid=(s_len // ts, nct),
        in_specs=[pl.BlockSpec((ts, ct), lambda i, j: (i, j)), pl.BlockSpec((dm, ct), lambda i, j: (0, j)), tile, tile,
                  pl.BlockSpec((8, dm), lambda i, j: (0, 0)), pl.BlockSpec((1, dm), lambda i, j: (0, 0))],
        out_specs=[tile, pl.BlockSpec((8, dm), lambda i, j: (0, 0))],
        out_shape=[jax.ShapeDtypeStruct((s_len, dm), F32), jax.ShapeDtypeStruct((8, dm), F32)],
        scratch_shapes=[pltpu.VMEM((ts, dm), F32)],
        compiler_params=_params(("arbitrary", "arbitrary")),
    )(dp, w_in, x, dx_res, modv, norm_g)


def _w_in_grad(h_t, dp, ts):
    dm, s_len = h_t.shape
    ct = _wide_col_tile(dp.shape[1])

    def body(h_ref, p_ref, o_ref):
        @pl.when(pl.program_id(1) == 0)
        def _():
            o_ref[...] = jnp.zeros_like(o_ref)

        o_ref[...] += _dot(h_ref[...], p_ref[...])

    return pl.pallas_call(
        body, name="w_in_grad", grid=(dp.shape[1] // ct, s_len // ts),
        in_specs=[pl.BlockSpec((dm, ts), lambda j, i: (0, i)), pl.BlockSpec((ts, ct), lambda j, i: (i, j))],
        out_specs=pl.BlockSpec((dm, ct), lambda j, i: (0, j)),
        out_shape=jax.ShapeDtypeStruct((dm, dp.shape[1]), F32),
        compiler_params=_params(("arbitrary", "arbitrary")),
    )(h_t, dp)


def _rope_lane_table():
    l64 = jnp.arange(LANES) % 64
    half = ROT_DIM // 2
    inv_freq = ROPE_THETA ** (-(jnp.arange(half, dtype=F32) * 2.0 / ROT_DIM))
    rot = l64 < ROT_DIM
    rows = [jnp.where(rot, inv_freq[l64 % half], 0.0), (l64 < half).astype(F32),
            ((l64 >= half) & rot).astype(F32), rot.astype(F32)]
    return jnp.concatenate([jnp.stack(rows), jnp.zeros((4, LANES), F32)], axis=0)


def _tile_sizes(s_len):
    ts_big = min(1024, s_len // 2)
    ts_mid = 256
    ts_head = 256
    qt = [min(1024, s_len // dil) for dil in DILATIONS]
    return ts_big, ts_mid, ts_head, qt


def kernel(x, c, positions, norm_g, w_ada, b_ada, w_in, conv_w, conv_b, conv_ln_g, conv_ln_b, w_conv_out, w_att_out, w_o, final_g, loss_target, m_norm_g, m_w_ada, m_b_ada, m_w_in, m_conv_w, m_conv_b, m_conv_ln_g, m_conv_ln_b, m_w_conv_out, m_w_att_out, m_w_o, m_final_g, v_norm_g, v_w_ada, v_b_ada, v_w_in, v_conv_w, v_conv_b, v_conv_ln_g, v_conv_ln_b, v_w_conv_out, v_w_att_out, v_w_o, v_final_g):
    s_len, dm = x.shape[1], x.shape[2]
    ts_big, ts_mid, ts_head, qt = _tile_sizes(s_len)
    xi, yi, cidx = _my_place()
    chip = 2 * xi + yi
    batch = 4 * xi + 2 * yi + cidx
    x2d, tgt = x[0], loss_target[0]
    pos = positions.reshape(s_len, 1)
    wa_l, wi_l, cw_l = w_ada[0], w_in[0], conv_w[0]
    wco_l, wao_l, wo_l = w_conv_out[0], w_att_out[0], w_o[0]
    ada_w = wa_l.shape[1]
    cw_cols = cw_l.shape[1]

    kinds = ("col", "row", "col", "row")
    w_bufs = _cast_weights([wi_l, wco_l, wao_l, wo_l], kinds)

    cw_pad = jnp.pad(cw_l, ((0, HALO - CONV_K), (0, 0)))
    small_in = jnp.concatenate([jnp.broadcast_to(c, (8, dm)), cw_pad.reshape(8, dm)], axis=0)
    small = _allgather_small(small_in).reshape(8, 16, dm)
    c_all = small[:, 0, :]
    conv_w_full = jnp.concatenate(
        [small[2 * p, 8:16, :].reshape(HALO, cw_cols) for p in range(4)], axis=1)
    b_l = lax.dynamic_slice(b_ada, (0, chip * ada_w), (1, ada_w))
    mod_parts = _allgather_small(_mod_part(c_all, wa_l, b_l)).reshape(8, 8, ada_w)
    mod_rows = lax.dynamic_index_in_dim(mod_parts, batch, axis=1, keepdims=False)
    mod = jnp.concatenate([mod_rows[2 * p] for p in range(4)], axis=0).reshape(3, dm)
    modv = jnp.concatenate([mod, jnp.zeros((5, dm), F32)], axis=0)

    sems_a, sems_b, w_bufs, token = _gather_weights_start(w_bufs, kinds, modv)
    rope = _rope_coefficients(pos, _rope_lane_table() + token, ts_big)
    h_b, h_t = _norm_modulate(x2d, modv + token[0, 0], norm_g, ts_big)
    (w_in_b,) = _gather_weights_wait(w_bufs, kinds, sems_a, 0, (rope, h_b), "gather_weights_wait_w_in")
    w_in_b = _pass_to_sibling(w_in_b, kinds[0])

    n_conv, n_gate = 3 * dm // COL_TILE, (ATT_W + 2 * dm) // COL_TILE
    p_conv = _plain_projection(h_b, w_in_b, 0, n_conv, ts_big, "conv_projection")
    p_gate = _plain_projection(h_b, w_in_b, n_conv + 9, n_gate, ts_big, "gate_projection")
    qkv_g = [_qkv_projection(h_b, rope, w_in_b, gi, ts_big) for gi in range(3)]
    w_co_b, w_ao_b, w_o_b = _gather_weights_wait(w_bufs, kinds, sems_b, 1, (qkv_g[2], p_conv, p_gate),
                                                  "gather_weights_wait_rest")
    y_conv, u0, u1 = _conv_forward(p_conv, conv_w_full, conv_b, conv_ln_g, conv_ln_b, w_co_b, ts_mid)
    qkv_flat = [q.reshape(3, s_len, ATT_W) for q in qkv_g]
    o_g, lse_g = [], []
    for gi, dil in enumerate(DILATIONS):
        o, l = _attention_forward(qkv_flat[gi], s_len // dil, qt[gi], "attention_forward_%d" % dil)
        o_g.append(o.reshape(dil, s_len // dil, ATT_W))
        lse_g.append(l.reshape(dil, s_len // dil, ATT_W))

    (loss_p, dx_res, dyc, dp, da0, da1, da2, ls0, ls1, ls2, de0, de1, de2,
     g_wo, g_wao, head_sums) = _merge_and_head(o_g, lse_g, p_gate, y_conv, x2d, tgt, modv,
                                               final_g.reshape(1, dm), w_ao_b, w_o_b, ts_head)

    dp, g_wco, conv_sums, g_cw = _conv_backward(dp, dyc, p_conv, u0, u1, conv_w_full, conv_ln_g, conv_ln_b,
                                                w_co_b, ts_mid)
    dqkv_g = []
    for gi, (dil, da, ls, de) in enumerate(zip(DILATIONS, (da0, da1, da2), (ls0, ls1, ls2), (de0, de1, de2))):
        flat = lambda a: a.reshape(s_len, ATT_W)
        dqkv = _attention_backward(qkv_flat[gi], flat(da), flat(ls), flat(de), s_len // dil, qt[gi],
                                   "attention_backward_%d" % dil)
        dqkv_g.append(dqkv.reshape(3, dil, s_len // dil, ATT_W))
    dp = _qkv_grad_to_tokens(dp, dqkv_g, rope, 2 * ts_mid)
    g_win = _w_in_grad(h_t, dp, ts_big)

    grads = [g_win, g_wco, g_wao, g_wo]
    full_shapes = [g.shape for g in grads]
    c_arr = jnp.reshape(cidx, (1,)).astype(jnp.int32)
    recv_halves = _reduce_pair_exchange(grads, kinds)
    halves = [_reduce_pair_sum(g, pa, kind, c_arr, "reduce_pair_sum_%d" % k)
              for k, (g, pa, kind) in enumerate(zip(grads, recv_halves, kinds))]
    send_sems, recv_sems, halves, lands, token = _reduce_to_owner_start(halves, kinds, full_shapes)
    grad_x, in_sums = _input_grad(dp, w_in_b, x2d, dx_res, modv + token[0, 0], norm_g, ts_big)
    halves, recvd = _reduce_to_owner_wait(send_sems, recv_sems, halves, lands, in_sums, kinds)
    gr_win, gr_wco, gr_wao, gr_wo = _reduce_finish(halves, recvd, kinds, full_shapes)

    rows = [in_sums[2:3], conv_sums[2:3], conv_sums[0:1], conv_sums[1:2], head_sums[0:1],
            in_sums[0:1], in_sums[1:2], head_sums[1:2], g_cw, jnp.pad(loss_p, ((0, 0), (0, dm - LANES)))]
    part = jnp.concatenate(rows, axis=0)
    gathered = _allgather_small(part)
    tot = _sum_devices(gathered, 48)
    loss = tot[8 + HALO, 0]
    dmod_all = gathered.reshape(8, 48, dm)[:, 5:8, :].reshape(8, 3 * dm)
    dmod_l = lax.dynamic_slice(dmod_all, (0, chip * ada_w), (8, ada_w))
    gr_wada = _grad_w_ada(c_all.T, dmod_l)
    gr_cw = lax.dynamic_slice(tot[8:8 + HALO], (0, chip * cw_cols), (HALO, cw_cols))

    pad_cw = lambda a: jnp.pad(a[0], ((0, HALO - CONV_K), (0, 0)))
    row = lambda a: a.reshape(1, dm)
    small_upd = _adamw_small(
        [norm_g, conv_b, conv_ln_g, conv_ln_b, row(final_g), b_ada], tot[0:8],
        [m_norm_g, m_conv_b, m_conv_ln_g, m_conv_ln_b, row(m_final_g), m_b_ada],
        [v_norm_g, v_conv_b, v_conv_ln_g, v_conv_ln_b, row(v_final_g), v_b_ada])
    upd = {
        "w_ada": _adamw(wa_l, gr_wada, m_w_ada[0], v_w_ada[0], "adamw_w_ada"),
        "w_in": _adamw(wi_l, gr_win, m_w_in[0], v_w_in[0], "adamw_w_in"),
        "conv_w": _adamw(cw_pad, gr_cw, pad_cw(m_conv_w), pad_cw(v_conv_w), "adamw_conv_w"),
        "w_co": _adamw(wco_l, gr_wco, m_w_conv_out[0], v_w_conv_out[0], "adamw_w_conv_out"),
        "w_ao": _adamw(wao_l, gr_wao, m_w_att_out[0], v_w_att_out[0], "adamw_w_att_out"),
        "w_o": _adamw(wo_l, gr_wo, m_w_o[0], v_w_o[0], "adamw_w_o"),
    }

    def family(which):
        if which is None:
            sm = small_upd[0:6]
            big = {"w_ada": gr_wada, "w_in": gr_win, "conv_w": gr_cw, "w_co": gr_wco, "w_ao": gr_wao, "w_o": gr_wo}
        else:
            sm = small_upd[6 * (which + 1):6 * (which + 2)]
            big = {k: upd[k][which] for k in ("w_ada", "w_in", "conv_w", "w_co", "w_ao", "w_o")}
        return [sm[0], big["w_ada"][None], sm[5], big["w_in"][None],
                big["conv_w"][None, :CONV_K], sm[1], sm[2], sm[3], big["w_co"][None],
                big["w_ao"][None], big["w_o"][None], sm[4].reshape(dm)]

    return (loss, grad_x[None], *family(None), *family(0), *family(1), *family(2))
```

```python
import jax
import jax.numpy as jnp
from jax import lax
from jax.experimental import pallas as pl
from jax.experimental.pallas import tpu as pltpu

F32 = jnp.float32
BF16 = jnp.bfloat16
MESH = pl.DeviceIdType.MESH
ANY = pl.BlockSpec(memory_space=pl.ANY)
VM = pl.BlockSpec(memory_space=pltpu.VMEM)

EPS = 1e-6
NEG = -1e30
ATT_W = 512
DILATIONS = (1, 4, 16)
BLK = 128
CONV_K = 31
HALO = 32
CONV_ROWS_FWD = 32
CONV_ROWS_BWD = 16
ROT_DIM = 16
ROPE_THETA = 500000.0
COL_TILE = 512
LANES = 128
VMEM_LIMIT = 56 * 1024 * 1024

ADAM_LR, ADAM_B1, ADAM_B2, ADAM_EPS, ADAM_WD, ADAM_STEP = 0.001, 0.9, 0.999, 1e-08, 0.01, 10


def _params(sem=None, vmem=VMEM_LIMIT):
    return pltpu.CompilerParams(dimension_semantics=sem, vmem_limit_bytes=vmem)


def _dot(a, b):
    return jnp.dot(a, b, preferred_element_type=F32)


def _dot_nt(a, b):
    return lax.dot_general(a, b, (((1,), (1,)), ((), ())), preferred_element_type=F32)


def _dot_tn(a, b):
    return lax.dot_general(a, b, (((0,), (0,)), ((), ())), preferred_element_type=F32)


def _sig(x):
    return jax.nn.sigmoid(x)


def _dsilu(x, s):
    return s * (1.0 + x * (1.0 - s))


def _my_place():
    return lax.axis_index("x"), lax.axis_index("y"), lax.axis_index("c")


def _allgather_small(x_shard):
    m_per, n = x_shard.shape

    def body(x_ref, out_ref, send_sems, recv_sems, local_sem):
        x, y, c = _my_place()
        me, sibling = (x, y, c), (x, y, 1 - c)
        chips = [(1 - x, y), (x, 1 - y), (1 - x, 1 - y)]

        def rows(px, py, pc):
            return out_ref.at[pl.ds((4 * px + 2 * py + pc) * m_per, m_per), :]

        def copy(k, block, to, src=None):
            return pltpu.make_async_remote_copy(
                src_ref=rows(*block) if src is None else src, dst_ref=rows(*block),
                send_sem=send_sems.at[k], recv_sem=recv_sems.at[k],
                device_id=to, device_id_type=MESH)

        mine = pltpu.make_async_copy(x_ref, rows(*me), local_sem)
        mine.start()
        first = [copy(0, me, sibling, src=x_ref)]
        first += [copy(1 + j, me, (*chip, c), src=x_ref) for j, chip in enumerate(chips)]
        for cp in first:
            cp.start()
        passed = [copy(4 + j, (*chip, c), sibling) for j, chip in enumerate(chips)]
        for j, chip in enumerate(chips):
            copy(1 + j, (*chip, c), me).wait_recv()
            passed[j].start()
        copy(0, sibling, me).wait_recv()
        for j, chip in enumerate(chips):
            copy(4 + j, (*chip, 1 - c), me).wait_recv()
        for cp in first + passed:
            cp.wait_send()
        mine.wait()

    return pl.pallas_call(
        body, name="allgather_small",
        out_shape=jax.ShapeDtypeStruct((8 * m_per, n), x_shard.dtype),
        in_specs=[VM], out_specs=VM,
        scratch_shapes=[pltpu.SemaphoreType.DMA((7,)), pltpu.SemaphoreType.DMA((7,)),
                        pltpu.SemaphoreType.DMA],
    )(x_shard)


def _shard_window(ref, kind, p, n_shards=4):
    r, c = ref.shape
    if kind == "col":
        w = c // n_shards
        return ref.at[:, pl.ds(p * w, w)]
    w = r // n_shards
    return ref.at[pl.ds(p * w, w), :]


def _half_window(ref, kind, hc):
    r, c = ref.shape
    if kind == "col":
        return ref.at[pl.ds(hc * (r // 2), r // 2), :]
    return ref.at[:, pl.ds(hc * (c // 2), c // 2)]


def _landed(ref, kind, chip, hc):
    return _half_window(_shard_window(ref, kind, 2 * chip[0] + chip[1]), kind, hc)


def _cast_weights(shards, kinds):
    n = len(shards)
    full_shapes = [(s.shape[0], 4 * s.shape[1]) if kind == "col" else (4 * s.shape[0], s.shape[1])
                   for s, kind in zip(shards, kinds)]

    def body(*refs):
        w_refs, out_refs, bf_refs, sems = refs[:n], refs[n:2 * n], refs[2 * n:3 * n], refs[3 * n]
        x, y, _ = _my_place()
        cps = []
        for k in range(n):
            bf_refs[k][...] = w_refs[k][...].astype(BF16)
            cp = pltpu.make_async_copy(bf_refs[k], _shard_window(out_refs[k], kinds[k], 2 * x + y), sems.at[k])
            cp.start()
            cps.append(cp)
        for cp in cps:
            cp.wait()

    return pl.pallas_call(
        body, name="cast_weights",
        out_shape=[jax.ShapeDtypeStruct(s, BF16) for s in full_shapes],
        in_specs=[VM] * n, out_specs=[ANY] * n,
        scratch_shapes=[pltpu.VMEM(s.shape, BF16) for s in shards] + [pltpu.SemaphoreType.DMA((n,))],
        compiler_params=_params(),
    )(*shards)


def _gather_copies(refs, kinds, sems_a, sems_b):
    x, y, c = _my_place()
    chips = [(1 - x, y), (x, 1 - y), (1 - x, 1 - y)]
    set_a, set_b = [], []
    for j, chip in enumerate(chips):
        for k in range(len(refs)):
            if refs[k] is None:
                continue
            for flip in ((0,) if k == 0 else (0, 1)):
                win = _landed(refs[k], kinds[k], (x, y), c)
                sems, idx = (sems_a, j) if k == 0 else (sems_b, ((k - 1) * 3 + j) * 2 + flip)
                mine = _landed(refs[k], kinds[k], chip, (c + flip) % 2)
                (set_a if k == 0 else set_b).append((
                    pltpu.make_async_remote_copy(
                        src_ref=win, dst_ref=win, send_sem=sems[0].at[idx], recv_sem=sems[1].at[idx],
                        device_id=(*chip, (c + flip) % 2), device_id_type=MESH),
                    pltpu.make_async_remote_copy(
                        src_ref=mine, dst_ref=mine, send_sem=sems[0].at[idx], recv_sem=sems[1].at[idx],
                        device_id=(*chip, (c + flip) % 2), device_id_type=MESH)))
    return set_a, set_b


_SPLIT = dict(has_side_effects=pltpu.SideEffectType.DATAFLOW_SIDE_EFFECTING)


def _gather_weights_start(fulls, kinds, after):
    n = len(fulls)
    hbm = pl.BlockSpec(memory_space=pltpu.HBM)
    sem = pl.BlockSpec(memory_space=pltpu.SEMAPHORE)
    nb = (n - 1) * 6

    def body(*refs):
        in_refs = refs[:n]
        sa, ra, sb, rb = refs[n + 1:n + 5]
        token = refs[-1]
        set_a, set_b = _gather_copies(in_refs, kinds, (sa, ra), (sb, rb))
        for out_cp, _ in set_a + set_b:
            out_cp.start()
        token[...] = jnp.zeros_like(token)

    out = pl.pallas_call(
        body, name="gather_weights_start",
        out_shape=[pltpu.SemaphoreType.DMA((3,)), pltpu.SemaphoreType.DMA((3,)),
                   pltpu.SemaphoreType.DMA((nb,)), pltpu.SemaphoreType.DMA((nb,))]
        + [pltpu.HBM(f.shape, f.dtype) for f in fulls] + [jax.ShapeDtypeStruct((8, LANES), F32)],
        in_specs=[hbm] * n + [ANY], out_specs=[sem] * 4 + [hbm] * n + [VM],
        input_output_aliases={k: 4 + k for k in range(n)},
        compiler_params=pltpu.CompilerParams(**_SPLIT),
    )(*[pltpu.with_memory_space_constraint(f, pltpu.HBM) for f in fulls], after)
    return (out[0], out[1]), (out[2], out[3]), out[4:4 + n], out[-1]


def _gather_weights_wait(fulls, kinds, sems, which, after, name):
    n = len(fulls)
    hbm = pl.BlockSpec(memory_space=pltpu.HBM)
    sem = pl.BlockSpec(memory_space=pltpu.SEMAPHORE)
    keep = [0] if which == 0 else list(range(1, n))

    def body(*refs):
        m = len(keep)
        full_refs = [None] * n
        for pos_, k in enumerate(keep):
            full_refs[k] = refs[pos_]
        s_ref, r_ref = refs[m:m + 2]
        if which == 0:
            sets = _gather_copies([full_refs[0]], kinds[:1], (s_ref, r_ref), None)[0]
        else:
            sets = _gather_copies([None] + [full_refs[k] for k in keep], kinds, None, (s_ref, r_ref))[1]
        for out_cp, in_cp in sets:
            out_cp.wait_send()
            in_cp.wait_recv()

    out = pl.pallas_call(
        body, name=name,
        out_shape=[pltpu.HBM(fulls[k].shape, fulls[k].dtype) for k in keep],
        in_specs=[hbm] * len(keep) + [sem, sem] + [ANY] * len(after), out_specs=[hbm] * len(keep),
        input_output_aliases={i: i for i in range(len(keep))},
        compiler_params=pltpu.CompilerParams(**_SPLIT),
    )(*[fulls[k] for k in keep], sems[0], sems[1], *after)
    return list(out)


def _row_gather_copies(buf, m_per, send_sems, recv_sems):
    x, y, c = _my_place()
    pairs = []
    for idx in range(1, 8):
        dx, dy, dc = idx // 4, (idx // 2) % 2, idx % 2
        peer = ((x + dx) % 2, (y + dy) % 2, (c + dc) % 2)
        mine = buf.at[pl.ds((4 * x + 2 * y + c) * m_per, m_per), :]
        theirs = buf.at[pl.ds((4 * peer[0] + 2 * peer[1] + peer[2]) * m_per, m_per), :]
        pairs.append(tuple(
            pltpu.make_async_remote_copy(src_ref=w, dst_ref=w, send_sem=send_sems.at[idx - 1],
                                         recv_sem=recv_sems.at[idx - 1], device_id=peer, device_id_type=MESH)
            for w in (mine, theirs)))
    return pairs


def _row_gather_start(buf, m_per, after):
    hbm = pl.BlockSpec(memory_space=pltpu.HBM)
    sem = pl.BlockSpec(memory_space=pltpu.SEMAPHORE)

    def body(buf_ref, after_ref, send_sems, recv_sems, out_ref, token):
        del after_ref, out_ref
        for out_cp, _ in _row_gather_copies(buf_ref, m_per, send_sems, recv_sems):
            out_cp.start()
        token[...] = jnp.zeros_like(token)

    return pl.pallas_call(
        body, name="row_gather_start",
        out_shape=[pltpu.SemaphoreType.DMA((7,)), pltpu.SemaphoreType.DMA((7,)), pltpu.HBM(buf.shape, buf.dtype),
                   jax.ShapeDtypeStruct((8, LANES), F32)],
        in_specs=[hbm, ANY], out_specs=[sem, sem, hbm, VM], input_output_aliases={0: 2},
        compiler_params=pltpu.CompilerParams(**_SPLIT),
    )(pltpu.with_memory_space_constraint(buf, pltpu.HBM), after)


def _row_gather_wait(send_sems, recv_sems, buf, m_per, after):
    hbm = pl.BlockSpec(memory_space=pltpu.HBM)
    sem = pl.BlockSpec(memory_space=pltpu.SEMAPHORE)

    def body(buf_ref, send_s, recv_s, *rest):
        for out_cp, in_cp in _row_gather_copies(buf_ref, m_per, send_s, recv_s):
            out_cp.wait_send()
            in_cp.wait_recv()

    return pl.pallas_call(
        body, name="row_gather_wait", out_shape=pltpu.HBM(buf.shape, buf.dtype),
        in_specs=[hbm, sem, sem] + [ANY] * len(after), out_specs=hbm, input_output_aliases={0: 0},
        compiler_params=pltpu.CompilerParams(**_SPLIT),
    )(buf, send_sems, recv_sems, *after)


def _pass_to_sibling(w_full, kind):
    def body(w_in_ref, w_ref, send_sems, recv_sems):
        del w_in_ref
        x, y, c = _my_place()
        chips = [(1 - x, y), (x, 1 - y), (1 - x, 1 - y)]
        cps = []
        for j, chip in enumerate(chips):
            win = _landed(w_ref, kind, chip, c)
            cp = pltpu.make_async_remote_copy(
                src_ref=win, dst_ref=win, send_sem=send_sems.at[j], recv_sem=recv_sems.at[j],
                device_id=(x, y, 1 - c), device_id_type=MESH)
            cp.start()
            cps.append(cp)
        for j, chip in enumerate(chips):
            theirs = _landed(w_ref, kind, chip, 1 - c)
            pltpu.make_async_remote_copy(
                src_ref=theirs, dst_ref=theirs, send_sem=send_sems.at[j], recv_sem=recv_sems.at[j],
                device_id=(x, y, 1 - c), device_id_type=MESH).wait_recv()
        for cp in cps:
            cp.wait_send()

    return pl.pallas_call(
        body, name="pass_to_sibling",
        out_shape=jax.ShapeDtypeStruct(w_full.shape, w_full.dtype),
        in_specs=[ANY], out_specs=ANY, input_output_aliases={0: 0},
        scratch_shapes=[pltpu.SemaphoreType.DMA((3,)), pltpu.SemaphoreType.DMA((3,))],
    )(w_full)


def _reduce_pair_exchange(grads, kinds):
    n = len(grads)
    half_shapes = [(g.shape[0] // 2, g.shape[1]) if kind == "col" else (g.shape[0], g.shape[1] // 2)
                   for g, kind in zip(grads, kinds)]

    def body(*refs):
        g_refs, pa_refs = refs[:n], refs[n:2 * n]
        send_sems, recv_sems = refs[2 * n:]
        x, y, c = _my_place()
        cps = []
        for k in range(n):
            cp = pltpu.make_async_remote_copy(
                src_ref=_half_window(g_refs[k], kinds[k], 1 - c), dst_ref=pa_refs[k],
                send_sem=send_sems.at[k], recv_sem=recv_sems.at[k],
                device_id=(x, y, 1 - c), device_id_type=MESH)
            cp.start()
            cps.append(cp)
        for cp in cps:
            cp.wait()

    return pl.pallas_call(
        body, name="reduce_pair_exchange",
        out_shape=[jax.ShapeDtypeStruct(s, F32) for s in half_shapes],
        in_specs=[ANY] * n, out_specs=[ANY] * n,
        scratch_shapes=[pltpu.SemaphoreType.DMA((n,)), pltpu.SemaphoreType.DMA((n,))],
    )(*grads)


def _row_tile(rows, cols, itemsize=4, target=2 * 1024 * 1024, mult=16):
    t = rows
    while t % 2 == 0 and t // 2 >= mult and (t // 2) % mult == 0 and t * cols * itemsize > target:
        t //= 2
    return t


def _reduce_pair_sum(g, pa, kind, c_arr, name):
    hr, hc_ = pa.shape
    tr = _row_tile(hr, hc_)
    nb = hr // tr

    def body(c_ref, g_ref, pa_ref, o_ref):
        o_ref[...] = (g_ref[...] + pa_ref[...]).astype(BF16)

    if kind == "col":
        g_map = lambda i, c_ref: (c_ref[0] * nb + i, 0)
    else:
        g_map = lambda i, c_ref: (i, c_ref[0])
    return pl.pallas_call(
        body, name=name,
        grid_spec=pltpu.PrefetchScalarGridSpec(
            num_scalar_prefetch=1, grid=(nb,),
            in_specs=[pl.BlockSpec((tr, hc_), g_map), pl.BlockSpec((tr, hc_), lambda i, c_ref: (i, 0))],
            out_specs=pl.BlockSpec((tr, hc_), lambda i, c_ref: (i, 0))),
        out_shape=jax.ShapeDtypeStruct((hr, hc_), BF16),
        compiler_params=_params(("parallel",)),
    )(c_arr, g, pa)


def _half_shard_shape(full_shape, kind):
    r, c = full_shape
    return (r // 2, c // 4) if kind == "col" else (r // 4, c // 2)


def _to_owner_copies(h_refs, land_refs, send_sems, recv_sems, kinds):
    n = len(h_refs)
    x, y, c = _my_place()
    chips = [(1 - x, y), (x, 1 - y), (1 - x, 1 - y)]
    cps = []
    for j, chip in enumerate(chips):
        pj = 2 * chip[0] + chip[1]
        for k in range(n):
            cps.append(pltpu.make_async_remote_copy(
                src_ref=_shard_window(h_refs[k], kinds[k], pj), dst_ref=land_refs[k].at[j],
                send_sem=send_sems.at[j * n + k], recv_sem=recv_sems.at[j * n + k],
                device_id=(*chip, c), device_id_type=MESH))
    return cps


def _reduce_to_owner_start(halves, kinds, full_shapes):
    n = len(halves)
    hs = [_half_shard_shape(fs, kind) for fs, kind in zip(full_shapes, kinds)]
    hbm = pl.BlockSpec(memory_space=pltpu.HBM)
    sem = pl.BlockSpec(memory_space=pltpu.SEMAPHORE)

    def body(*refs):
        h_refs, land_refs = refs[:n], refs[n:2 * n]
        send_sems, recv_sems = refs[2 * n:2 * n + 2]
        token = refs[-1]
        for cp in _to_owner_copies(h_refs, land_refs, send_sems, recv_sems, kinds):
            cp.start()
        token[...] = jnp.zeros_like(token)

    lands = [pltpu.with_memory_space_constraint(lax.empty((3,) + s, BF16), pltpu.HBM) for s in hs]
    out = pl.pallas_call(
        body, name="reduce_to_owner_start",
        out_shape=[pltpu.SemaphoreType.DMA((3 * n,)), pltpu.SemaphoreType.DMA((3 * n,))]
        + [pltpu.HBM(h.shape, h.dtype) for h in halves] + [pltpu.HBM((3,) + s, BF16) for s in hs]
        + [jax.ShapeDtypeStruct((8, LANES), F32)],
        in_specs=[hbm] * (2 * n), out_specs=[sem, sem] + [hbm] * (2 * n) + [VM],
        input_output_aliases={k: 2 + k for k in range(2 * n)},
        compiler_params=pltpu.CompilerParams(has_side_effects=pltpu.SideEffectType.DATAFLOW_SIDE_EFFECTING),
    )(*[pltpu.with_memory_space_constraint(h, pltpu.HBM) for h in halves], *lands)
    return out[0], out[1], out[2:2 + n], out[2 + n:2 + 2 * n], out[-1]


def _reduce_to_owner_wait(send_sems, recv_sems, halves, lands, after, kinds):
    n = len(halves)
    hbm = pl.BlockSpec(memory_space=pltpu.HBM)
    sem = pl.BlockSpec(memory_space=pltpu.SEMAPHORE)

    def body(*refs):
        h_refs, land_refs = refs[:n], refs[n:2 * n]
        send_s, recv_s = refs[2 * n:2 * n + 2]
        for cp in _to_owner_copies(h_refs, land_refs, send_s, recv_s, kinds):
            cp.wait_send()
            cp.wait_recv()

    out = pl.pallas_call(
        body, name="reduce_to_owner_wait",
        out_shape=[pltpu.HBM(h.shape, h.dtype) for h in halves] + [pltpu.HBM(l.shape, l.dtype) for l in lands],
        in_specs=[hbm] * (2 * n) + [sem, sem, ANY], out_specs=[hbm] * (2 * n),
        input_output_aliases={k: k for k in range(2 * n)},
        compiler_params=pltpu.CompilerParams(has_side_effects=pltpu.SideEffectType.DATAFLOW_SIDE_EFFECTING),
    )(*halves, *lands, send_sems, recv_sems, after)
    return out[:n], out[n:]


def _reduce_finish(halves, recvd, kinds, full_shapes):
    n = len(halves)
    hs = [_half_shard_shape(fs, kind) for fs, kind in zip(full_shapes, kinds)]
    shard_shapes = [(fs[0], fs[1] // 4) if kind == "col" else (fs[0] // 4, fs[1])
                    for fs, kind in zip(full_shapes, kinds)]

    def body(*refs):
        h_refs, rc_refs, gs_refs = refs[:n], refs[n:2 * n], refs[2 * n:3 * n]
        own_refs, gh_refs = refs[3 * n:4 * n], refs[4 * n:5 * n]
        in_sems, loc_sems, send_sems, recv_sems = refs[5 * n:]
        x, y, c = _my_place()
        p = 2 * x + y
        loads = []
        for k in range(n):
            cp = pltpu.make_async_copy(_shard_window(h_refs[k], kinds[k], p), own_refs[k], in_sems.at[k])
            cp.start()
            loads.append(cp)
        outs = []
        for k in range(n):
            loads[k].wait()
            gh_refs[k][...] = (own_refs[k][...].astype(F32) + rc_refs[k][0].astype(F32)
                               + rc_refs[k][1].astype(F32) + rc_refs[k][2].astype(F32))
            dst = _half_window(gs_refs[k], kinds[k], c)
            lc = pltpu.make_async_copy(gh_refs[k], dst, loc_sems.at[k])
            lc.start()
            rc = pltpu.make_async_remote_copy(
                src_ref=gh_refs[k], dst_ref=dst, send_sem=send_sems.at[k], recv_sem=recv_sems.at[k],
                device_id=(x, y, 1 - c), device_id_type=MESH)
            rc.start()
            outs.append((lc, rc))
        for k, (lc, rc) in enumerate(outs):
            lc.wait()
            rc.wait_send()
            pltpu.make_async_remote_copy(
                src_ref=gh_refs[k], dst_ref=_half_window(gs_refs[k], kinds[k], 1 - c),
                send_sem=send_sems.at[k], recv_sem=recv_sems.at[k],
                device_id=(x, y, 1 - c), device_id_type=MESH).wait_recv()

    return pl.pallas_call(
        body, name="reduce_finish",
        out_shape=[jax.ShapeDtypeStruct(s, F32) for s in shard_shapes],
        in_specs=[ANY] * n + [VM] * n, out_specs=[ANY] * n,
        scratch_shapes=[pltpu.VMEM(s, BF16) for s in hs] + [pltpu.VMEM(s, F32) for s in hs]
        + [pltpu.SemaphoreType.DMA((n,)) for _ in range(4)],
        compiler_params=_params(),
    )(*halves, *recvd)


def _mod_part(c_all, w_ada_l, b_l):
    def body(c_ref, w_ref, b_ref, o_ref):
        o_ref[...] = _dot(c_ref[...].astype(BF16), w_ref[...].astype(BF16)) + b_ref[...]

    return pl.pallas_call(
        body, name="mod_part", out_shape=jax.ShapeDtypeStruct((8, w_ada_l.shape[1]), F32),
        in_specs=[VM, VM, VM], out_specs=VM, compiler_params=_params(),
    )(c_all, w_ada_l, b_l)


def _sum_devices(parts, m):
    def body(p_ref, o_ref):
        acc = p_ref[0:m, :]
        for d in range(1, 8):
            acc = acc + p_ref[d * m:(d + 1) * m, :]
        o_ref[...] = acc

    return pl.pallas_call(
        body, name="sum_devices", out_shape=jax.ShapeDtypeStruct((m, parts.shape[1]), F32),
        in_specs=[VM], out_specs=VM, compiler_params=_params(),
    )(parts)


def _grad_w_ada(c_all_t, dmod_l):
    d, w = c_all_t.shape[0], dmod_l.shape[1]

    def body(ct_ref, dm_ref, o_ref):
        acc = ct_ref[:, 0:1] * dm_ref[0:1, :]
        for b in range(1, 8):
            acc = acc + ct_ref[:, b:b + 1] * dm_ref[b:b + 1, :]
        o_ref[...] = acc

    return pl.pallas_call(
        body, name="grad_w_ada", out_shape=jax.ShapeDtypeStruct((d, w), F32),
        in_specs=[VM, VM], out_specs=VM, compiler_params=_params(),
    )(c_all_t, dmod_l)


def _adamw_math(w, g, m, v):
    nm = ADAM_B1 * m + (1.0 - ADAM_B1) * g
    nv = ADAM_B2 * v + (1.0 - ADAM_B2) * (g * g)
    m_hat = nm / (1.0 - ADAM_B1 ** ADAM_STEP)
    v_hat = nv / (1.0 - ADAM_B2 ** ADAM_STEP)
    return -ADAM_LR * (m_hat / (jnp.sqrt(v_hat) + ADAM_EPS) + ADAM_WD * w), nm, nv


def _adamw_small(ws, g8, ms, vs):
    shapes = [jax.ShapeDtypeStruct(w.shape, F32) for w in ws]

    def body(*refs):
        w_refs, g_ref, m_refs, v_refs, outs = refs[0:6], refs[6], refs[7:13], refs[13:19], refs[19:]
        for idx in range(6):
            if idx < 5:
                g = g_ref[idx:idx + 1, :]
            else:
                g = jnp.concatenate([g_ref[5:6, :], g_ref[6:7, :], g_ref[7:8, :]], axis=1)
            res = (g,) + _adamw_math(w_refs[idx][...], g, m_refs[idx][...], v_refs[idx][...])
            for fam in range(4):
                outs[6 * fam + idx][...] = res[fam]

    return pl.pallas_call(
        body, name="adamw_small", out_shape=shapes * 4,
        in_specs=[VM] * 19, out_specs=[VM] * 24, compiler_params=_params(),
    )(*ws, g8, *ms, *vs)


def _adamw(w, g, m, v, name, after=()):
    r, c = w.shape
    tr = _row_tile(r, c, target=1024 * 1024, mult=8)

    def body(w_ref, g_ref, m_ref, v_ref, *rest):
        d_ref, nm_ref, nv_ref = rest[len(after):]
        d_ref[...], nm_ref[...], nv_ref[...] = _adamw_math(w_ref[...], g_ref[...], m_ref[...], v_ref[...])

    spec = pl.BlockSpec((tr, c), lambda i: (i, 0))
    return pl.pallas_call(
        body, name=name, grid=(r // tr,),
        out_shape=[jax.ShapeDtypeStruct((r, c), F32)] * 3,
        in_specs=[spec] * 4 + [ANY] * len(after), out_specs=[spec] * 3,
        compiler_params=_params(("parallel",)),
    )(w, g, m, v, *after)


def _rope_coefficients(pos, ropetab, ts):
    s_len = pos.shape[0]

    def body(pos_ref, tab_ref, o_ref):
        ang = pos_ref[...].astype(F32) * tab_ref[0:1, :]
        cs, sn = jnp.cos(ang), jnp.sin(ang)
        o_ref[0] = jnp.where(tab_ref[3:4, :] > 0, cs, 1.0)
        o_ref[1] = -sn * tab_ref[1:2, :]
        o_ref[2] = sn * tab_ref[2:3, :]

    return pl.pallas_call(
        body, name="rope_coefficients", grid=(s_len // ts,),
        in_specs=[pl.BlockSpec((ts, 1), lambda i: (i, 0)), pl.BlockSpec((8, LANES), lambda i: (0, 0))],
        out_specs=pl.BlockSpec((3, ts, LANES), lambda i: (0, i, 0)),
        out_shape=jax.ShapeDtypeStruct((3, s_len, LANES), F32),
        compiler_params=_params(("parallel",)),
    )(pos, ropetab)


def _deinterleave_store(vals, slab, out_ref, d, ts, dtype):
    if d == 1:
        for s in range(4):
            out_ref[0, :, s * LANES:(s + 1) * LANES] = vals[s].astype(dtype)
        return
    for s in range(4):
        slab[s] = vals[s]
    if d == 16:
        q = ts // 4
        for s in range(4):
            for r1 in range(4):
                slab[4 + s, r1 * q:(r1 + 1) * q, :] = slab[s, pl.ds(r1, q, stride=4), :]
        for r in range(d):
            r1, r2 = r % 4, r // 4
            for s in range(4):
                out_ref[r, :, s * LANES:(s + 1) * LANES] = slab[
                    4 + s, pl.ds(r1 * q + r2, ts // d, stride=4), :].astype(dtype)
        return
    for r in range(d):
        for s in range(4):
            out_ref[r, :, s * LANES:(s + 1) * LANES] = slab[s, pl.ds(r, ts // d, stride=d), :].astype(dtype)


def _interleave_load(blk_ref, slab, d, ts):
    if d == 1:
        return [blk_ref[0, :, s * LANES:(s + 1) * LANES] for s in range(4)]
    if d == 16:
        q = ts // 4
        for r in range(d):
            r1, r2 = r % 4, r // 4
            for s in range(4):
                slab[4 + s, pl.ds(r1 * q + r2, ts // d, stride=4), :] = blk_ref[r, :, s * LANES:(s + 1) * LANES]
        for s in range(4):
            for r1 in range(4):
                slab[s, pl.ds(r1, q, stride=4), :] = slab[4 + s, r1 * q:(r1 + 1) * q, :]
        return [slab[s] for s in range(4)]
    for r in range(d):
        for s in range(4):
            slab[s, pl.ds(r, ts // d, stride=d), :] = blk_ref[r, :, s * LANES:(s + 1) * LANES]
    return [slab[s] for s in range(4)]


def _norm_modulate(x, modv, norm_g, ts):
    s_len, d_model = x.shape

    def body(x_ref, mod_ref, g_ref, h_ref, ht_ref):
        xv = x_ref[...]
        r = lax.rsqrt(jnp.mean(xv * xv, axis=-1, keepdims=True) + EPS)
        h = (xv * r) * g_ref[...] * (1.0 + mod_ref[1:2, :]) + mod_ref[0:1, :]
        h_ref[...] = h.astype(BF16)
        ht_ref[...] = h.T.astype(BF16)

    return pl.pallas_call(
        body, name="norm_modulate", grid=(s_len // ts,),
        in_specs=[pl.BlockSpec((ts, d_model), lambda i: (i, 0)), pl.BlockSpec((8, d_model), lambda i: (0, 0)),
                  pl.BlockSpec((1, d_model), lambda i: (0, 0))],
        out_specs=[pl.BlockSpec((ts, d_model), lambda i: (i, 0)), pl.BlockSpec((d_model, ts), lambda i: (0, i))],
        out_shape=[jax.ShapeDtypeStruct((s_len, d_model), BF16), jax.ShapeDtypeStruct((d_model, s_len), BF16)],
        compiler_params=_params(("parallel",)),
    )(x, modv, norm_g)


def _plain_projection(h, w_in, first_tile, n_tiles, ts, name):
    s_len, d_model = h.shape
    ct = COL_TILE

    def body(h_ref, w_ref, o_ref):
        o_ref[...] = _dot(h_ref[...], w_ref[...])

    return pl.pallas_call(
        body, name=name, grid=(s_len // ts, n_tiles),
        in_specs=[pl.BlockSpec((ts, d_model), lambda i, j: (i, 0)),
                  pl.BlockSpec((d_model, ct), lambda i, j: (0, first_tile + j))],
        out_specs=pl.BlockSpec((ts, ct), lambda i, j: (i, j)),
        out_shape=jax.ShapeDtypeStruct((s_len, n_tiles * ct), F32),
        compiler_params=_params(("parallel", "arbitrary")),
    )(h, w_in)


def _qkv_projection(h, rope, w_in, gi, ts):
    s_len, d_model = h.shape
    dil = DILATIONS[gi]
    nc = 3 * d_model // COL_TILE
    hr = ts // 2
    assert COL_TILE == ATT_W and hr % (16 * dil) == 0

    def body(h_ref, rope_ref, wq_ref, wk_ref, wv_ref, o_ref, slab):
        rc, ra, rb = rope_ref.at[0], rope_ref.at[1], rope_ref.at[2]
        w_refs = (wq_ref, wk_ref, wv_ref)
        units = [(w, half) for w in range(3) for half in range(2)]

        def matmul(w, half):
            return _dot(h_ref[half * hr:(half + 1) * hr, :], w_refs[w][...])

        def finish(w, half, res):
            rows = slice(half * hr, (half + 1) * hr)
            vals = []
            for s in range(4):
                t = res[:, s * LANES:(s + 1) * LANES]
                if w < 2:
                    t = (t * rc[rows, :] + pltpu.roll(t, LANES - 8, 1) * ra[rows, :]
                         + pltpu.roll(t, 8, 1) * rb[rows, :])
                vals.append(t)
            out = o_ref.at[w, :, half * (hr // dil):(half + 1) * (hr // dil), :]
            _deinterleave_store(vals, slab.at[half], out, dil, hr, BF16)

        res_next = matmul(*units[0])
        for ui, (w, half) in enumerate(units):
            res = res_next
            if ui + 1 < len(units):
                res_next = matmul(*units[ui + 1])
            finish(w, half, res)

    def w_spec(w):
        return pl.BlockSpec((d_model, COL_TILE), lambda i: (0, nc + 3 * w + gi))

    return pl.pallas_call(
        body, name="qkv_projection_%d" % dil, grid=(s_len // ts,),
        in_specs=[pl.BlockSpec((ts, d_model), lambda i: (i, 0)), pl.BlockSpec((3, ts, LANES), lambda i: (0, i, 0)),
                  w_spec(0), w_spec(1), w_spec(2)],
        out_specs=pl.BlockSpec((3, dil, ts // dil, ATT_W), lambda i: (0, 0, i, 0)),
        out_shape=jax.ShapeDtypeStruct((3, dil, s_len // dil, ATT_W), BF16),
        scratch_shapes=[pltpu.VMEM((2, 8, hr, LANES), F32)],
        compiler_params=_params(("parallel",)),
    )(h, rope, w_in, w_in, w_in)


def _layernorm_stats(u1):
    mu = jnp.mean(u1, axis=-1, keepdims=True)
    xc = u1 - mu
    rstd = lax.rsqrt(jnp.mean(xc * xc, axis=-1, keepdims=True) + EPS)
    return xc * rstd, rstd


def _shifted_copies(win, shf, ts):
    rows = ts + HALO - 8
    for b in range(1, 8):
        shf[b - 1, 0:rows, :] = win[pl.ds(b, rows), :]


def _tap(win, shf, off, r0, rows):
    a, b = divmod(off, 8)
    start = 8 * a + r0
    if b == 0:
        return win[start:start + rows, :]
    return shf[b - 1, start:start + rows, :]


def _conv_forward(p_conv, conv_w, conv_b, ln_g, ln_b, w_co, ts):
    s_len, d3 = p_conv.shape
    dm = d3 // 3

    def body(p_ref, cw_ref, cb_ref, g_ref, b_ref, w_ref, y_ref, u0_ref, u1_ref, win, shf):
        i = pl.program_id(0)

        @pl.when(i == 0)
        def _():
            win[0:HALO, :] = jnp.zeros((HALO, dm), F32)

        a, b, z = p_ref[:, 0:dm], p_ref[:, dm:2 * dm], p_ref[:, 2 * dm:3 * dm]
        u0 = a * _sig(b)
        win[HALO:HALO + ts, :] = u0
        u0_ref[...] = u0
        _shifted_copies(win, shf, ts)
        for r0 in range(0, ts, CONV_ROWS_FWD):
            acc = jnp.broadcast_to(cb_ref[...], (CONV_ROWS_FWD, dm))
            for k in range(CONV_K):
                acc = acc + cw_ref[k:k + 1, :] * _tap(win, shf, HALO - (CONV_K - 1) + k, r0, CONV_ROWS_FWD)
            u1_ref[r0:r0 + CONV_ROWS_FWD, :] = acc
        xh, _ = _layernorm_stats(u1_ref[...])
        u2 = xh * g_ref[...] + b_ref[...]
        a_conv = (u2 * _sig(u2)) * (z * _sig(z))
        y_ref[...] = _dot(a_conv.astype(BF16), w_ref[...])
        win[0:HALO, :] = win[ts:ts + HALO, :]

    row = pl.BlockSpec((1, dm), lambda i: (0, 0))
    tile = pl.BlockSpec((ts, dm), lambda i: (i, 0))
    return pl.pallas_call(
        body, name="conv_forward", grid=(s_len // ts,),
        in_specs=[pl.BlockSpec((ts, d3), lambda i: (i, 0)),
                  pl.BlockSpec((HALO, dm), lambda i: (0, 0)), row, row, row,
                  pl.BlockSpec((dm, dm), lambda i: (0, 0))],
        out_specs=[tile, tile, tile],
        out_shape=[jax.ShapeDtypeStruct((s_len, dm), F32)] * 3,
        scratch_shapes=[pltpu.VMEM((ts + HALO, dm), F32), pltpu.VMEM((7, ts + HALO - 8, dm), F32)],
        compiler_params=_params(("arbitrary",)),
    )(p_conv, conv_w, conv_b, ln_g, ln_b, w_co)


def _conv_backward(dp, dyc, p_conv, u0, u1, conv_w, ln_g, ln_b, w_co, ts):
    s_len, d3 = p_conv.shape
    dm = d3 // 3
    nt = s_len // ts
    hb = ts // HALO

    def body(dp_in, dy_ref, p_ref, u0_ref, uh_ref, u1_ref, cw_ref, g_ref, b_ref, w_ref,
             dp_ref, gw_ref, gs_ref, gcw_ref, dwin, uwin, shf):
        del dp_in
        i = pl.program_id(0)
        ti = nt - 1 - i

        @pl.when(i == 0)
        def _():
            gw_ref[...] = jnp.zeros_like(gw_ref)
            gs_ref[...] = jnp.zeros_like(gs_ref)
            gcw_ref[...] = jnp.zeros_like(gcw_ref)
            dwin[ts:ts + HALO, :] = jnp.zeros((HALO, dm), F32)

        dy = dy_ref[...]
        z = p_ref[:, 2 * dm:3 * dm]
        d_ac = _dot_nt(dy, w_ref[...])
        xh, rstd = _layernorm_stats(u1_ref[...])
        u2 = xh * g_ref[...] + b_ref[...]
        sg2, sgz = _sig(u2), _sig(z)
        u3, sz = u2 * sg2, z * sgz
        gw_ref[...] += _dot_tn((u3 * sz).astype(BF16), dy)
        d_z = d_ac * u3 * _dsilu(z, sgz)
        d_u2 = d_ac * sz * _dsilu(u2, sg2)
        gs_ref[0:1, :] += jnp.sum(d_u2 * xh, axis=0, keepdims=True)
        gs_ref[1:2, :] += jnp.sum(d_u2, axis=0, keepdims=True)
        dxh = d_u2 * g_ref[...]
        d_u1 = rstd * (dxh - jnp.mean(dxh, axis=-1, keepdims=True)
                       - xh * jnp.mean(dxh * xh, axis=-1, keepdims=True))
        gs_ref[2:3, :] += jnp.sum(d_u1, axis=0, keepdims=True)
        dwin[0:ts, :] = d_u1
        uwin[0:HALO, :] = jnp.where(ti == 0, 0.0, uh_ref[...])
        uwin[HALO:HALO + ts, :] = u0_ref[...]
        dp_ref[:, 2 * dm:3 * dm] = d_z.astype(BF16)
        _shifted_copies(uwin, shf, ts)
        for k in range(CONV_K):
            part = jnp.zeros((CONV_ROWS_BWD, dm), F32)
            for r0 in range(0, ts, CONV_ROWS_BWD):
                part = part + dwin[r0:r0 + CONV_ROWS_BWD, :] * _tap(uwin, shf, HALO - (CONV_K - 1) + k, r0,
                                                                    CONV_ROWS_BWD)
            gcw_ref[k:k + 1, :] += jnp.sum(part, axis=0, keepdims=True)
        _shifted_copies(dwin, shf, ts)
        for r0 in range(0, ts, CONV_ROWS_BWD):
            d_u0 = jnp.zeros((CONV_ROWS_BWD, dm), F32)
            for k in range(CONV_K):
                d_u0 = d_u0 + cw_ref[k:k + 1, :] * _tap(dwin, shf, CONV_K - 1 - k, r0, CONV_ROWS_BWD)
            rows = slice(r0, r0 + CONV_ROWS_BWD)
            sgb = _sig(p_ref[rows, dm:2 * dm])
            dp_ref[rows, 0:dm] = (d_u0 * sgb).astype(BF16)
            dp_ref[rows, dm:2 * dm] = (d_u0 * p_ref[rows, 0:dm] * sgb * (1.0 - sgb)).astype(BF16)
        dwin[ts:ts + HALO, :] = dwin[0:HALO, :]

    rev = lambda i: (nt - 1 - i, 0)
    row = pl.BlockSpec((1, dm), lambda i: (0, 0))
    tile = pl.BlockSpec((ts, dm), rev)
    return pl.pallas_call(
        body, name="conv_backward", grid=(nt,),
        in_specs=[ANY, tile, pl.BlockSpec((ts, d3), rev), tile,
                  pl.BlockSpec((HALO, dm), lambda i: (jnp.maximum((nt - 1 - i) * hb - 1, 0), 0)),
                  tile, pl.BlockSpec((HALO, dm), lambda i: (0, 0)), row, row,
                  pl.BlockSpec((dm, dm), lambda i: (0, 0))],
        out_specs=[pl.BlockSpec((ts, d3), rev), pl.BlockSpec((dm, dm), lambda i: (0, 0)),
                   pl.BlockSpec((8, dm), lambda i: (0, 0)), pl.BlockSpec((HALO, dm), lambda i: (0, 0))],
        out_shape=[jax.ShapeDtypeStruct(dp.shape, BF16), jax.ShapeDtypeStruct((dm, dm), F32),
                   jax.ShapeDtypeStruct((8, dm), F32), jax.ShapeDtypeStruct((HALO, dm), F32)],
        input_output_aliases={0: 0},
        scratch_shapes=[pltpu.VMEM((ts + HALO, dm), F32), pltpu.VMEM((ts + HALO, dm), F32),
                        pltpu.VMEM((7, ts + HALO - 8, dm), F32)],
        compiler_params=_params(("arbitrary",)),
    )(dp, dyc, p_conv, u0, u0, u1, conv_w, ln_g, ln_b, w_co)


def _att_masks():
    head0 = lax.broadcasted_iota(jnp.int32, (BLK, LANES), 1) < 64
    col = lax.broadcasted_iota(jnp.int32, (BLK, 4 * BLK), 1)
    row = lax.broadcasted_iota(jnp.int32, (BLK, 4 * BLK), 0)
    kj = col % BLK
    prev = jnp.where(col < 2 * BLK, 1, 0)
    band = jnp.where(col < 2 * BLK, kj - row, row - kj)
    return head0, band, prev


def _fill_block_diagonal(dst, slab, src_ref, halo_ref, head0, nq):
    sl = slice(slab * LANES, (slab + 1) * LANES)
    for b in range(nq + 1):
        blk = halo_ref[:, sl] if b == 0 else src_ref[(b - 1) * BLK:b * BLK, sl]
        base = (slab * (nq + 1) + b) * 2 * BLK
        zero = jnp.zeros_like(blk)
        dst[base:base + BLK, :] = jnp.where(head0, blk, zero)
        dst[base + BLK:base + 2 * BLK, :] = jnp.where(head0, zero, blk)


def _attention_forward(qkv, seq_len, qt, name):
    s_len = qkv.shape[1]
    nq = qt // BLK
    tiles_per_seq = seq_len // qt

    def body(q_ref, k_ref, v_ref, kh_ref, vh_ref, o_ref, lse_ref, kbd, vbd):
        first = jnp.where((pl.program_id(0) % tiles_per_seq) == 0, 4 * BLK, 0)
        head0, band, prev = _att_masks()
        band_first = band - prev * first
        for p in range(4):
            _fill_block_diagonal(kbd, p, k_ref, kh_ref, head0, nq)
            _fill_block_diagonal(vbd, p, v_ref, vh_ref, head0, nq)
        units = [(p, n) for p in range(4) for n in range(nq)]

        def keys_of(p, n):
            base = (p * (nq + 1) + n) * 2 * BLK
            return slice(base, base + 4 * BLK)

        def scores(p, n):
            q2 = q_ref[n * BLK:(n + 1) * BLK, p * LANES:(p + 1) * LANES] * 0.125
            return _dot_nt(q2, kbd[keys_of(p, n), :])

        def finish(p, n, o, den, lse):
            rows, sl = slice(n * BLK, (n + 1) * BLK), slice(p * LANES, (p + 1) * LANES)
            o_ref[rows, sl] = o / jnp.where(head0, den[0], den[1])
            lse_ref[rows, sl] = jnp.where(head0, lse[0], lse[1])

        s_next = scores(*units[0])
        pending = None
        for ui, (p, n) in enumerate(units):
            s = s_next
            if ui + 1 < len(units):
                s_next = scores(*units[ui + 1])
            s = jnp.where((band_first if n == 0 else band) >= 0, s, NEG)
            grp = [s[:, g * BLK:(g + 1) * BLK] for g in range(4)]
            ps, den, lse = [None] * 4, [], []
            for h in range(2):
                m = jnp.max(jnp.maximum(grp[h], grp[2 + h]), axis=-1, keepdims=True)
                ps[h], ps[2 + h] = jnp.exp(grp[h] - m), jnp.exp(grp[2 + h] - m)
                dn = jnp.sum(ps[h] + ps[2 + h], axis=-1, keepdims=True)
                den.append(dn)
                lse.append(m + jnp.log(dn))
            pmat = jnp.concatenate([x.astype(BF16) for x in ps], axis=1)
            o = _dot(pmat, vbd[keys_of(p, n), :])
            if pending is not None:
                finish(*pending)
            pending = (p, n, o, den, lse)
        finish(*pending)

    def which(w):
        return pl.BlockSpec((None, qt, ATT_W), lambda i: (w, i, 0))

    def halo(w):
        return pl.BlockSpec((None, BLK, ATT_W), lambda i: (w, jnp.maximum(i * nq - 1, 0), 0))

    out = pl.BlockSpec((qt, ATT_W), lambda i: (i, 0))
    bd = pltpu.VMEM((4 * (nq + 1) * 2 * BLK, LANES), BF16)
    return pl.pallas_call(
        body, name=name, grid=(s_len // qt,),
        in_specs=[which(0), which(1), which(2), halo(1), halo(2)],
        out_specs=[out, out], out_shape=[jax.ShapeDtypeStruct((s_len, ATT_W), F32)] * 2,
        scratch_shapes=[bd, bd],
        compiler_params=_params(("parallel",)),
    )(qkv, qkv, qkv, qkv, qkv)


def _attention_backward(qkv, d_att, lse, delta, seq_len, qt, name):
    s_len = qkv.shape[1]
    nq = qt // BLK
    tiles_per_seq = seq_len // qt
    nblk = s_len // BLK

    def body(q_ref, k_ref, v_ref, kh_ref, vh_ref, do_ref, lse_ref, dl_ref,
             qn_ref, don_ref, lsen_ref, dln_ref, dqkv_ref, kbd, vbd):
        i = pl.program_id(0)
        first = jnp.where((i % tiles_per_seq) == 0, 4 * BLK, 0)
        last = jnp.where((i % tiles_per_seq) == tiles_per_seq - 1, 4 * BLK, 0)
        head0, band, prev = _att_masks()
        band_first = band - prev * first
        band_tail = band[:, 0:2 * BLK] - last
        for p in range(4):
            _fill_block_diagonal(kbd, p, k_ref, kh_ref, head0, nq)
            _fill_block_diagonal(vbd, p, v_ref, vh_ref, head0, nq)
        units = [(p, n) for p in range(4) for n in range(nq + 1)]

        def stage_a(p, n):
            sl = slice(p * LANES, (p + 1) * LANES)
            base = (p * (nq + 1) + n) * 2 * BLK
            if n < nq:
                rows = slice(n * BLK, (n + 1) * BLK)
                q2, do2, lse2, dl2 = q_ref[rows, sl], do_ref[rows, sl], lse_ref[rows, sl], dl_ref[rows, sl]
                keys = slice(base, base + 4 * BLK)
            else:
                q2, do2, lse2, dl2 = qn_ref[:, sl], don_ref[:, sl], lsen_ref[:, sl], dln_ref[:, sl]
                keys = slice(base, base + 2 * BLK)
            s = _dot_nt(q2 * 0.125, kbd[keys, :])
            dp = _dot_nt(do2, vbd[keys, :])
            return q2, do2, lse2, dl2, keys, s, dp

        def stage_b(n, lse2, dl2, s, dp):
            mask = band_tail if n == nq else (band_first if n == 0 else band)
            ps, dss = [], []
            for g in range(s.shape[1] // BLK):
                h = g % 2
                cols = slice(g * BLK, (g + 1) * BLK)
                pg = jnp.exp(jnp.where(mask[:, cols] >= 0, s[:, cols] - lse2[:, h * 64:h * 64 + 1], NEG))
                ps.append(pg.astype(BF16))
                dss.append((pg * (dp[:, cols] - dl2[:, h * 64:h * 64 + 1]) * 0.125).astype(BF16))
            return jnp.concatenate(ps, axis=1), jnp.concatenate(dss, axis=1)

        def heads(r, g):
            return jnp.where(head0, r[g * BLK:(g + 1) * BLK, :], r[(g + 1) * BLK:(g + 2) * BLK, :])

        a_next = stage_a(*units[0])
        carry = None
        for ui, (p, n) in enumerate(units):
            q2, do2, lse2, dl2, keys, s, dp = a_next
            if ui + 1 < len(units):
                a_next = stage_a(*units[ui + 1])
            pmat, dsmat = stage_b(n, lse2, dl2, s, dp)
            sl = slice(p * LANES, (p + 1) * LANES)
            if n < nq:
                dqkv_ref[0, n * BLK:(n + 1) * BLK, sl] = _dot(dsmat, kbd[keys, :])
            dkbd = _dot_tn(dsmat, q2)
            dvbd = _dot_tn(pmat, do2)
            if n > 0:
                prow = slice((n - 1) * BLK, n * BLK)
                dqkv_ref[1, prow, sl] = carry[0] + heads(dkbd, 0)
                dqkv_ref[2, prow, sl] = carry[1] + heads(dvbd, 0)
            if n < nq:
                carry = (heads(dkbd, 2), heads(dvbd, 2))

    def which(w):
        return pl.BlockSpec((None, qt, ATT_W), lambda i: (w, i, 0))

    def prev(w):
        return pl.BlockSpec((None, BLK, ATT_W), lambda i: (w, jnp.maximum(i * nq - 1, 0), 0))

    tile = pl.BlockSpec((qt, ATT_W), lambda i: (i, 0))
    nxt = pl.BlockSpec((BLK, ATT_W), lambda i: (jnp.minimum((i + 1) * nq, nblk - 1), 0))
    nxt_q = pl.BlockSpec((None, BLK, ATT_W), lambda i: (0, jnp.minimum((i + 1) * nq, nblk - 1), 0))
    return pl.pallas_call(
        body, name=name, grid=(s_len // qt,),
        in_specs=[which(0), which(1), which(2), prev(1), prev(2), tile, tile, tile, nxt_q, nxt, nxt, nxt],
        out_specs=pl.BlockSpec((3, qt, ATT_W), lambda i: (0, i, 0)),
        out_shape=jax.ShapeDtypeStruct((3, s_len, ATT_W), F32),
        scratch_shapes=[pltpu.VMEM((4 * (nq + 1) * 2 * BLK, LANES), BF16)] * 2,
        compiler_params=_params(("parallel",)),
    )(qkv, qkv, qkv, qkv, qkv, d_att, lse, delta, qkv, d_att, lse, delta)


def _merge_and_head(o_g, lse_g, p_gate, y_conv, x, tgt, modv, final_g, w_ao, w_o, ts):
    s_len, dm = x.shape
    gw = ATT_W + 2 * dm
    nt = s_len // ts
    gate_off = 3 * dm + 9 * ATT_W
    assert gate_off % gw == 0

    def body(o0, o1, o2, l0, l1, l2, pg_ref, yc_ref, x_ref, t_ref, mod_ref, fg_ref, wao_ref, wo_ref,
             loss_ref, dx_ref, dyc_ref, dpg_ref, da0, da1, da2, ls0, ls1, ls2, de0, de1, de2,
             gwo_ref, gwao_ref, gs_ref, slab):
        i = pl.program_id(0)

        @pl.when(i == 0)
        def _():
            loss_ref[...] = jnp.zeros_like(loss_ref)
            gwo_ref[...] = jnp.zeros_like(gwo_ref)
            gwao_ref[...] = jnp.zeros_like(gwao_ref)
            gs_ref[...] = jnp.zeros_like(gs_ref)

        os_, ls_ = [], []
        for dil, o_ref, l_ref in zip(DILATIONS, (o0, o1, o2), (l0, l1, l2)):
            os_.append(jnp.concatenate(_interleave_load(o_ref, slab, dil, ts), axis=1))
            ls_.append(jnp.concatenate(_interleave_load(l_ref, slab, dil, ts), axis=1))
        mx = jnp.maximum(jnp.maximum(ls_[0], ls_[1]), ls_[2])
        wts = [jnp.exp(l - mx) for l in ls_]
        wsum = wts[0] + wts[1] + wts[2]
        att = (wts[0] * os_[0] + wts[1] * os_[1] + wts[2] * os_[2]) / wsum
        lse_all = mx + jnp.log(wsum)

        z_att, g_conv, g_att = pg_ref[:, 0:ATT_W], pg_ref[:, ATT_W:ATT_W + dm], pg_ref[:, ATT_W + dm:gw]
        sgz = _sig(z_att)
        sz = z_att * sgz
        a_att = (att * sz).astype(BF16)
        y_att = _dot(a_att, wao_ref[...])
        y_conv = yc_ref[...]
        sgc, sga = _sig(g_conv), _sig(g_att)
        merged = (sgc * y_conv + sga * y_att).astype(BF16)
        mo = _dot(merged, wo_ref[...])
        gate = mod_ref[2:3, :]
        x2 = x_ref[...] + gate * mo
        r = lax.rsqrt(jnp.mean(x2 * x2, axis=-1, keepdims=True) + EPS)
        xr = x2 * r
        err = xr * fg_ref[...] - t_ref[...]
        loss_ref[...] += 0.5 * jnp.sum(jnp.mean(err * err, axis=-1, keepdims=True))
        dy = err * (1.0 / dm)
        gs_ref[0:1, :] += jnp.sum(dy * xr, axis=0, keepdims=True)
        dyg = dy * fg_ref[...]
        d_x2 = r * dyg - xr * (r * jnp.mean(dyg * xr, axis=-1, keepdims=True))
        dx_ref[...] = d_x2
        gs_ref[1:2, :] += jnp.sum(d_x2 * mo, axis=0, keepdims=True)
        d_mo = (d_x2 * gate).astype(BF16)
        d_mg = _dot_nt(d_mo, wo_ref[...])
        gwo_ref[...] += _dot_tn(merged, d_mo)
        dyc_ref[...] = (d_mg * sgc).astype(BF16)
        dpg_ref[:, ATT_W:ATT_W + dm] = (d_mg * y_conv * sgc * (1.0 - sgc)).astype(BF16)
        d_ya = (d_mg * sga).astype(BF16)
        dpg_ref[:, ATT_W + dm:gw] = (d_mg * y_att * sga * (1.0 - sga)).astype(BF16)
        gwao_ref[...] += _dot_tn(a_att, d_ya)
        d_aa = _dot_nt(d_ya, wao_ref[...])
        dpg_ref[:, 0:ATT_W] = (d_aa * att * _dsilu(z_att, sgz)).astype(BF16)
        d_att = d_aa * sz
        ri = lax.broadcasted_iota(jnp.int32, (ATT_W, ATT_W), 0) // 64
        ci = lax.broadcasted_iota(jnp.int32, (ATT_W, ATT_W), 1) // 64
        ones = jnp.where(ri == ci, 1.0, 0.0).astype(BF16)
        prod = d_att * att
        hi = prod.astype(BF16)
        lo = (prod - hi.astype(F32)).astype(BF16)
        delta = _dot(hi, ones) + _dot(lo, ones)
        for val, refs, dt in ((d_att, (da0, da1, da2), BF16), (lse_all, (ls0, ls1, ls2), F32),
                              (delta, (de0, de1, de2), F32)):
            vals = [val[:, s * LANES:(s + 1) * LANES] for s in range(4)]
            for dil, ref in zip(DILATIONS, refs):
                _deinterleave_store(vals, slab, ref, dil, ts, dt)

    def grp(dil):
        return pl.BlockSpec((dil, ts // dil, ATT_W), lambda i: (0, i, 0))

    tile = pl.BlockSpec((ts, dm), lambda i: (i, 0))
    gate_tile = pl.BlockSpec((ts, gw), lambda i: (i, 0))
    const = lambda shp: pl.BlockSpec(shp, lambda i: tuple(0 for _ in shp))
    grp_shape = lambda dt: [jax.ShapeDtypeStruct((dil, s_len // dil, ATT_W), dt) for dil in DILATIONS]
    return pl.pallas_call(
        body, name="merge_and_head", grid=(nt,),
        in_specs=[grp(d) for d in DILATIONS] * 2
        + [gate_tile, tile, tile, tile, const((8, dm)), const((1, dm)), const((ATT_W, dm)), const((dm, dm))],
        out_specs=[const((8, LANES)), tile, tile, pl.BlockSpec((ts, gw), lambda i: (i, gate_off // gw))]
        + [grp(d) for d in DILATIONS] * 3
        + [const((dm, dm)), const((ATT_W, dm)), const((8, dm))],
        out_shape=[jax.ShapeDtypeStruct((8, LANES), F32), jax.ShapeDtypeStruct((s_len, dm), F32),
                   jax.ShapeDtypeStruct((s_len, dm), BF16), jax.ShapeDtypeStruct((s_len, gate_off + gw), BF16)]
        + grp_shape(BF16) + grp_shape(F32) + grp_shape(F32)
        + [jax.ShapeDtypeStruct((dm, dm), F32), jax.ShapeDtypeStruct((ATT_W, dm), F32),
           jax.ShapeDtypeStruct((8, dm), F32)],
        scratch_shapes=[pltpu.VMEM((8, ts, LANES), F32)],
        compiler_params=_params(("arbitrary",)),
    )(*o_g, *lse_g, p_gate, y_conv, x, tgt, modv, final_g, w_ao, w_o)


def _qkv_grad_to_tokens(dp, dqkv_g, rope, ts):
    s_len, dm = rope.shape[1], (dp.shape[1] - 10 * ATT_W) // 5
    qw = 3 * ATT_W
    assert (3 * dm) % qw == 0

    def body(dp_in, g0, g1, g2, rope_ref, o_ref, slab):
        del dp_in
        w = pl.program_id(1)
        rc, ra, rb = rope_ref.at[0], rope_ref.at[1], rope_ref.at[2]

        def emit(roped):
            for gi, (dil, g_ref) in enumerate(zip(DILATIONS, (g0, g1, g2))):
                vals = _interleave_load(g_ref, slab, dil, ts)
                for s in range(4):
                    t = vals[s]
                    if roped:
                        t = t * rc[...] + pltpu.roll(t * ra[...], 8, 1) + pltpu.roll(t * rb[...], LANES - 8, 1)
                    col = gi * ATT_W + s * LANES
                    o_ref[:, col:col + LANES] = t.astype(BF16)

        pl.when(w < 2)(lambda: emit(True))
        pl.when(w == 2)(lambda: emit(False))

    return pl.pallas_call(
        body, name="qkv_grad_to_tokens", grid=(s_len // ts, 3),
        in_specs=[ANY] + [pl.BlockSpec((None, dil, ts // dil, ATT_W), lambda i, w: (w, 0, i, 0)) for dil in DILATIONS]
        + [pl.BlockSpec((3, ts, LANES), lambda i, w: (0, i, 0))],
        out_specs=pl.BlockSpec((ts, qw), lambda i, w: (i, 3 * dm // qw + w)),
        out_shape=jax.ShapeDtypeStruct(dp.shape, BF16),
        input_output_aliases={0: 0},
        scratch_shapes=[pltpu.VMEM((8, ts, LANES), F32)],
        compiler_params=_params(("parallel", "arbitrary")),
    )(dp, *dqkv_g, rope)


def _wide_col_tile(cols):
    for width in (5 * COL_TILE, 2 * COL_TILE):
        if cols % width == 0:
            return width
    return COL_TILE


def _input_grad(dp, w_in, x, dx_res, modv, norm_g, ts):
    s_len, dm = x.shape
    ct = next(w * COL_TILE for w in (4, 2, 1) if dp.shape[1] % (w * COL_TILE) == 0)
    nct = dp.shape[1] // ct

    def body(p_ref, w_ref, x_ref, dxr_ref, mod_ref, g_ref, gx_ref, gs_ref, acc):
        i, j = pl.program_id(0), pl.program_id(1)

        @pl.when((i == 0) & (j == 0))
        def _():
            gs_ref[...] = jnp.zeros_like(gs_ref)

        @pl.when(j == 0)
        def _():
            acc[...] = jnp.zeros_like(acc)

        acc[...] += _dot_nt(p_ref[...], w_ref[...])

        @pl.when(j == nct - 1)
        def _():
            d_h = acc[...]
            xv = x_ref[...]
            r = lax.rsqrt(jnp.mean(xv * xv, axis=-1, keepdims=True) + EPS)
            xr = xv * r
            gs_ref[0:1, :] += jnp.sum(d_h, axis=0, keepdims=True)
            gs_ref[1:2, :] += jnp.sum(d_h * (xr * g_ref[...]), axis=0, keepdims=True)
            d_n = d_h * (1.0 + mod_ref[1:2, :])
            gs_ref[2:3, :] += jnp.sum(d_n * xr, axis=0, keepdims=True)
            dxn = d_n * g_ref[...]
            gx_ref[...] = dxr_ref[...] + r * dxn - xr * (r * jnp.mean(dxn * xr, axis=-1, keepdims=True))

    tile = pl.BlockSpec((ts, dm), lambda i, j: (i, 0))
    return pl.pallas_call(
        body, name="input_grad", grid=(s_len // ts, nct),
        in_specs=[pl.BlockSpec((ts, ct), lambda i, j: (i, j)), pl.BlockSpec((dm, ct), lambda i, j: (0, j)), tile, tile,
                  pl.BlockSpec((8, dm), lambda i, j: (0, 0)), pl.BlockSpec((1, dm), lambda i, j: (0, 0))],
        out_specs=[tile, pl.BlockSpec((8, dm), lambda i, j: (0, 0))],
        out_shape=[jax.ShapeDtypeStruct((s_len, dm), F32), jax.ShapeDtypeStruct((8, dm), F32)],
        scratch_shapes=[pltpu.VMEM((ts, dm), F32)],
        compiler_params=_params(("arbitrary", "arbitrary")),
    )(dp, w_in, x, dx_res, modv, norm_g)


def _w_in_grad(h_t, dp, ts):
    dm, s_len = h_t.shape
    ct = _wide_col_tile(dp.shape[1])

    def body(h_ref, p_ref, o_ref):
        @pl.when(pl.program_id(1) == 0)
        def _():
            o_ref[...] = jnp.zeros_like(o_ref)

        o_ref[...] += _dot(h_ref[...], p_ref[...])

    return pl.pallas_call(
        body, name="w_in_grad", grid=(dp.shape[1] // ct, s_len // ts),
        in_specs=[pl.BlockSpec((dm, ts), lambda j, i: (0, i)), pl.BlockSpec((ts, ct), lambda j, i: (i, j))],
        out_specs=pl.BlockSpec((dm, ct), lambda j, i: (0, j)),
        out_shape=jax.ShapeDtypeStruct((dm, dp.shape[1]), F32),
        compiler_params=_params(("arbitrary", "arbitrary")),
    )(h_t, dp)


def _rope_lane_table():
    l64 = jnp.arange(LANES) % 64
    half = ROT_DIM // 2
    inv_freq = ROPE_THETA ** (-(jnp.arange(half, dtype=F32) * 2.0 / ROT_DIM))
    rot = l64 < ROT_DIM
    rows = [jnp.where(rot, inv_freq[l64 % half], 0.0), (l64 < half).astype(F32),
            ((l64 >= half) & rot).astype(F32), rot.astype(F32)]
    return jnp.concatenate([jnp.stack(rows), jnp.zeros((4, LANES), F32)], axis=0)


def _tile_sizes(s_len):
    ts_big = min(1024, s_len // 2)
    ts_mid = 256
    ts_head = 256
    qt = [min(1024, s_len // dil) for dil in DILATIONS]
    return ts_big, ts_mid, ts_head, qt


def kernel(x, c, positions, norm_g, w_ada, b_ada, w_in, conv_w, conv_b, conv_ln_g, conv_ln_b, w_conv_out, w_att_out, w_o, final_g, loss_target, m_norm_g, m_w_ada, m_b_ada, m_w_in, m_conv_w, m_conv_b, m_conv_ln_g, m_conv_ln_b, m_w_conv_out, m_w_att_out, m_w_o, m_final_g, v_norm_g, v_w_ada, v_b_ada, v_w_in, v_conv_w, v_conv_b, v_conv_ln_g, v_conv_ln_b, v_w_conv_out, v_w_att_out, v_w_o, v_final_g):
    s_len, dm = x.shape[1], x.shape[2]
    ts_big, ts_mid, ts_head, qt = _tile_sizes(s_len)
    xi, yi, cidx = _my_place()
    chip = 2 * xi + yi
    batch = 4 * xi + 2 * yi + cidx
    x2d, tgt = x[0], loss_target[0]
    pos = positions.reshape(s_len, 1)
    wa_l, wi_l, cw_l = w_ada[0], w_in[0], conv_w[0]
    wco_l, wao_l, wo_l = w_conv_out[0], w_att_out[0], w_o[0]
    ada_w = wa_l.shape[1]
    cw_cols = cw_l.shape[1]

    kinds = ("col", "row", "col", "row")
    w_bufs = _cast_weights([wi_l, wco_l, wao_l, wo_l], kinds)

    cw_pad = jnp.pad(cw_l, ((0, HALO - CONV_K), (0, 0)))
    small_in = jnp.concatenate([jnp.broadcast_to(c, (8, dm)), cw_pad.reshape(8, dm)], axis=0)
    small = _allgather_small(small_in).reshape(8, 16, dm)
    c_all = small[:, 0, :]
    conv_w_full = jnp.concatenate(
        [small[2 * p, 8:16, :].reshape(HALO, cw_cols) for p in range(4)], axis=1)
    b_l = lax.dynamic_slice(b_ada, (0, chip * ada_w), (1, ada_w))
    mod_parts = _allgather_small(_mod_part(c_all, wa_l, b_l)).reshape(8, 8, ada_w)
    mod_rows = lax.dynamic_index_in_dim(mod_parts, batch, axis=1, keepdims=False)
    mod = jnp.concatenate([mod_rows[2 * p] for p in range(4)], axis=0).reshape(3, dm)
    modv = jnp.concatenate([mod, jnp.zeros((5, dm), F32)], axis=0)

    sems_a, sems_b, w_bufs, token = _gather_weights_start(w_bufs, kinds, modv)
    rope = _rope_coefficients(pos, _rope_lane_table() + token, ts_big)
    h_b, h_t = _norm_modulate(x2d, modv + token[0, 0], norm_g, ts_big)
    (w_in_b,) = _gather_weights_wait(w_bufs, kinds, sems_a, 0, (rope, h_b), "gather_weights_wait_w_in")
    w_in_b = _pass_to_sibling(w_in_b, kinds[0])

    n_conv, n_gate = 3 * dm // COL_TILE, (ATT_W + 2 * dm) // COL_TILE
    ts_proj = min(2 * ts_big, s_len)
    p_conv = _plain_projection(h_b, w_in_b, 0, n_conv, ts_proj, "conv_projection")
    p_gate = _plain_projection(h_b, w_in_b, n_conv + 9, n_gate, ts_proj, "gate_projection")
    qkv_g = [_qkv_projection(h_b, rope, w_in_b, gi, ts_big) for gi in range(3)]
    w_co_b, w_ao_b, w_o_b = _gather_weights_wait(w_bufs, kinds, sems_b, 1, (qkv_g[2], p_conv, p_gate),
                                                  "gather_weights_wait_rest")
    y_conv, u0, u1 = _conv_forward(p_conv, conv_w_full, conv_b, conv_ln_g, conv_ln_b, w_co_b, ts_mid)
    qkv_flat = [q.reshape(3, s_len, ATT_W) for q in qkv_g]
    o_g, lse_g = [], []
    for gi, dil in enumerate(DILATIONS):
        o, l = _attention_forward(qkv_flat[gi], s_len // dil, qt[gi], "attention_forward_%d" % dil)
        o_g.append(o.reshape(dil, s_len // dil, ATT_W))
        lse_g.append(l.reshape(dil, s_len // dil, ATT_W))

    (loss_p, dx_res, dyc, dp, da0, da1, da2, ls0, ls1, ls2, de0, de1, de2,
     g_wo, g_wao, head_sums) = _merge_and_head(o_g, lse_g, p_gate, y_conv, x2d, tgt, modv,
                                               final_g.reshape(1, dm), w_ao_b, w_o_b, ts_head)

    dp, g_wco, conv_sums, g_cw = _conv_backward(dp, dyc, p_conv, u0, u1, conv_w_full, conv_ln_g, conv_ln_b,
                                                w_co_b, ts_mid)
    dqkv_g = []
    for gi, (dil, da, ls, de) in enumerate(zip(DILATIONS, (da0, da1, da2), (ls0, ls1, ls2), (de0, de1, de2))):
        flat = lambda a: a.reshape(s_len, ATT_W)
        dqkv = _attention_backward(qkv_flat[gi], flat(da), flat(ls), flat(de), s_len // dil, qt[gi],
                                   "attention_backward_%d" % dil)
        dqkv_g.append(dqkv.reshape(3, dil, s_len // dil, ATT_W))
    dp = _qkv_grad_to_tokens(dp, dqkv_g, rope, 2 * ts_mid)
    g_win = _w_in_grad(h_t, dp, ts_big)

    grads = [g_win, g_wco, g_wao, g_wo]
    full_shapes = [g.shape for g in grads]
    c_arr = jnp.reshape(cidx, (1,)).astype(jnp.int32)
    recv_halves = _reduce_pair_exchange(grads, kinds)
    halves = [_reduce_pair_sum(g, pa, kind, c_arr, "reduce_pair_sum_%d" % k)
              for k, (g, pa, kind) in enumerate(zip(grads, recv_halves, kinds))]
    send_sems, recv_sems, halves, lands, token = _reduce_to_owner_start(halves, kinds, full_shapes)
    grad_x, in_sums = _input_grad(dp, w_in_b, x2d, dx_res, modv + token[0, 0], norm_g, ts_big)
    halves, recvd = _reduce_to_owner_wait(send_sems, recv_sems, halves, lands, in_sums, kinds)
    gr_win, gr_wco, gr_wao, gr_wo = _reduce_finish(halves, recvd, kinds, full_shapes)

    rows = [in_sums[2:3], conv_sums[2:3], conv_sums[0:1], conv_sums[1:2], head_sums[0:1],
            in_sums[0:1], in_sums[1:2], head_sums[1:2], g_cw, jnp.pad(loss_p, ((0, 0), (0, dm - LANES)))]
    part = jnp.concatenate(rows, axis=0)
    buf = lax.dynamic_update_slice(jnp.zeros((8 * 48, dm), F32), part, (batch * 48, 0))
    row_sems_s, row_sems_r, buf, row_token = _row_gather_start(buf, 48, gr_win)
    upd = {
        "w_in": _adamw(wi_l, gr_win, m_w_in[0], v_w_in[0], "adamw_w_in", (row_token,)),
        "w_co": _adamw(wco_l, gr_wco, m_w_conv_out[0], v_w_conv_out[0], "adamw_w_conv_out", (row_token,)),
        "w_ao": _adamw(wao_l, gr_wao, m_w_att_out[0], v_w_att_out[0], "adamw_w_att_out", (row_token,)),
        "w_o": _adamw(wo_l, gr_wo, m_w_o[0], v_w_o[0], "adamw_w_o", (row_token,)),
    }
    gathered = _row_gather_wait(row_sems_s, row_sems_r, buf, 48, [upd[k][0] for k in ("w_in", "w_co", "w_ao", "w_o")])
    tot = _sum_devices(gathered, 48)
    loss = tot[8 + HALO, 0]
    dmod_all = gathered.reshape(8, 48, dm)[:, 5:8, :].reshape(8, 3 * dm)
    dmod_l = lax.dynamic_slice(dmod_all, (0, chip * ada_w), (8, ada_w))
    gr_wada = _grad_w_ada(c_all.T, dmod_l)
    gr_cw = lax.dynamic_slice(tot[8:8 + HALO], (0, chip * cw_cols), (HALO, cw_cols))

    pad_cw = lambda a: jnp.pad(a[0], ((0, HALO - CONV_K), (0, 0)))
    row = lambda a: a.reshape(1, dm)
    small_upd = _adamw_small(
        [norm_g, conv_b, conv_ln_g, conv_ln_b, row(final_g), b_ada], tot[0:8],
        [m_norm_g, m_conv_b, m_conv_ln_g, m_conv_ln_b, row(m_final_g), m_b_ada],
        [v_norm_g, v_conv_b, v_conv_ln_g, v_conv_ln_b, row(v_final_g), v_b_ada])
    upd["w_ada"] = _adamw(wa_l, gr_wada, m_w_ada[0], v_w_ada[0], "adamw_w_ada")
    upd["conv_w"] = _adamw(cw_pad, gr_cw, pad_cw(m_conv_w), pad_cw(v_conv_w), "adamw_conv_w")

    def family(which):
        if which is None:
            sm = small_upd[0:6]
            big = {"w_ada": gr_wada, "w_in": gr_win, "conv_w": gr_cw, "w_co": gr_wco, "w_ao": gr_wao, "w_o": gr_wo}
        else:
            sm = small_upd[6 * (which + 1):6 * (which + 2)]
            big = {k: upd[k][which] for k in ("w_ada", "w_in", "conv_w", "w_co", "w_ao", "w_o")}
        return [sm[0], big["w_ada"][None], sm[5], big["w_in"][None],
                big["conv_w"][None, :CONV_K], sm[1], sm[2], sm[3], big["w_co"][None],
                big["w_ao"][None], big["w_o"][None], sm[4].reshape(dm)]

    return (loss, grad_x[None], *family(None), *family(0), *family(1), *family(2))
```

```python
import jax
import jax.numpy as jnp
from jax import lax
from jax.experimental import pallas as pl
from jax.experimental.pallas import tpu as pltpu

F32 = jnp.float32
BF16 = jnp.bfloat16
MESH = pl.DeviceIdType.MESH
ANY = pl.BlockSpec(memory_space=pl.ANY)
VM = pl.BlockSpec(memory_space=pltpu.VMEM)

EPS = 1e-6
NEG = -1e30
ATT_W = 512
DILATIONS = (1, 4, 16)
BLK = 128
CONV_K = 31
HALO = 32
CONV_ROWS_FWD = 32
CONV_ROWS_BWD = 16
ROT_DIM = 16
ROPE_THETA = 500000.0
COL_TILE = 512
LANES = 128
VMEM_LIMIT = 56 * 1024 * 1024

ADAM_LR, ADAM_B1, ADAM_B2, ADAM_EPS, ADAM_WD, ADAM_STEP = 0.001, 0.9, 0.999, 1e-08, 0.01, 10


def _params(sem=None, vmem=VMEM_LIMIT):
    return pltpu.CompilerParams(dimension_semantics=sem, vmem_limit_bytes=vmem)


def _dot(a, b):
    return jnp.dot(a, b, preferred_element_type=F32)


def _dot_nt(a, b):
    return lax.dot_general(a, b, (((1,), (1,)), ((), ())), preferred_element_type=F32)


def _dot_tn(a, b):
    return lax.dot_general(a, b, (((0,), (0,)), ((), ())), preferred_element_type=F32)


def _sig(x):
    return jax.nn.sigmoid(x)


def _dsilu(x, s):
    return s * (1.0 + x * (1.0 - s))


def _my_place():
    return lax.axis_index("x"), lax.axis_index("y"), lax.axis_index("c")


def _allgather_small(x_shard):
    m_per, n = x_shard.shape

    def body(x_ref, out_ref, send_sems, recv_sems, local_sem):
        x, y, c = _my_place()
        me, sibling = (x, y, c), (x, y, 1 - c)
        chips = [(1 - x, y), (x, 1 - y), (1 - x, 1 - y)]

        def rows(px, py, pc):
            return out_ref.at[pl.ds((4 * px + 2 * py + pc) * m_per, m_per), :]

        def copy(k, block, to, src=None):
            return pltpu.make_async_remote_copy(
                src_ref=rows(*block) if src is None else src, dst_ref=rows(*block),
                send_sem=send_sems.at[k], recv_sem=recv_sems.at[k],
                device_id=to, device_id_type=MESH)

        mine = pltpu.make_async_copy(x_ref, rows(*me), local_sem)
        mine.start()
        first = [copy(0, me, sibling, src=x_ref)]
        first += [copy(1 + j, me, (*chip, c), src=x_ref) for j, chip in enumerate(chips)]
        for cp in first:
            cp.start()
        passed = [copy(4 + j, (*chip, c), sibling) for j, chip in enumerate(chips)]
        for j, chip in enumerate(chips):
            copy(1 + j, (*chip, c), me).wait_recv()
            passed[j].start()
        copy(0, sibling, me).wait_recv()
        for j, chip in enumerate(chips):
            copy(4 + j, (*chip, 1 - c), me).wait_recv()
        for cp in first + passed:
            cp.wait_send()
        mine.wait()

    return pl.pallas_call(
        body, name="allgather_small",
        out_shape=jax.ShapeDtypeStruct((8 * m_per, n), x_shard.dtype),
        in_specs=[VM], out_specs=VM,
        scratch_shapes=[pltpu.SemaphoreType.DMA((7,)), pltpu.SemaphoreType.DMA((7,)),
                        pltpu.SemaphoreType.DMA],
    )(x_shard)


def _shard_window(ref, kind, p, n_shards=4):
    r, c = ref.shape
    if kind == "col":
        w = c // n_shards
        return ref.at[:, pl.ds(p * w, w)]
    w = r // n_shards
    return ref.at[pl.ds(p * w, w), :]


def _half_window(ref, kind, hc):
    r, c = ref.shape
    if kind == "col":
        return ref.at[pl.ds(hc * (r // 2), r // 2), :]
    return ref.at[:, pl.ds(hc * (c // 2), c // 2)]


def _landed(ref, kind, chip, hc):
    return _half_window(_shard_window(ref, kind, 2 * chip[0] + chip[1]), kind, hc)


def _cast_weights(shards, kinds):
    n = len(shards)
    full_shapes = [(s.shape[0], 4 * s.shape[1]) if kind == "col" else (4 * s.shape[0], s.shape[1])
                   for s, kind in zip(shards, kinds)]

    def body(*refs):
        w_refs, out_refs, bf_refs, sems = refs[:n], refs[n:2 * n], refs[2 * n:3 * n], refs[3 * n]
        x, y, _ = _my_place()
        cps = []
        for k in range(n):
            bf_refs[k][...] = w_refs[k][...].astype(BF16)
            cp = pltpu.make_async_copy(bf_refs[k], _shard_window(out_refs[k], kinds[k], 2 * x + y), sems.at[k])
            cp.start()
            cps.append(cp)
        for cp in cps:
            cp.wait()

    return pl.pallas_call(
        body, name="cast_weights",
        out_shape=[jax.ShapeDtypeStruct(s, BF16) for s in full_shapes],
        in_specs=[VM] * n, out_specs=[ANY] * n,
        scratch_shapes=[pltpu.VMEM(s.shape, BF16) for s in shards] + [pltpu.SemaphoreType.DMA((n,))],
        compiler_params=_params(),
    )(*shards)


def _gather_copies(refs, kinds, sems_a, sems_b):
    x, y, c = _my_place()
    chips = [(1 - x, y), (x, 1 - y), (1 - x, 1 - y)]
    set_a, set_b = [], []
    for j, chip in enumerate(chips):
        for k in range(len(refs)):
            if refs[k] is None:
                continue
            for flip in ((0,) if k == 0 else (0, 1)):
                win = _landed(refs[k], kinds[k], (x, y), c)
                sems, idx = (sems_a, j) if k == 0 else (sems_b, ((k - 1) * 3 + j) * 2 + flip)
                mine = _landed(refs[k], kinds[k], chip, (c + flip) % 2)
                (set_a if k == 0 else set_b).append((
                    pltpu.make_async_remote_copy(
                        src_ref=win, dst_ref=win, send_sem=sems[0].at[idx], recv_sem=sems[1].at[idx],
                        device_id=(*chip, (c + flip) % 2), device_id_type=MESH),
                    pltpu.make_async_remote_copy(
                        src_ref=mine, dst_ref=mine, send_sem=sems[0].at[idx], recv_sem=sems[1].at[idx],
                        device_id=(*chip, (c + flip) % 2), device_id_type=MESH)))
    return set_a, set_b


_SPLIT = dict(has_side_effects=pltpu.SideEffectType.DATAFLOW_SIDE_EFFECTING)


def _gather_weights_start(fulls, kinds, after):
    n = len(fulls)
    hbm = pl.BlockSpec(memory_space=pltpu.HBM)
    sem = pl.BlockSpec(memory_space=pltpu.SEMAPHORE)
    nb = (n - 1) * 6

    def body(*refs):
        in_refs = refs[:n]
        sa, ra, sb, rb = refs[n + 1:n + 5]
        token = refs[-1]
        set_a, set_b = _gather_copies(in_refs, kinds, (sa, ra), (sb, rb))
        for out_cp, _ in set_a + set_b:
            out_cp.start()
        token[...] = jnp.zeros_like(token)

    out = pl.pallas_call(
        body, name="gather_weights_start",
        out_shape=[pltpu.SemaphoreType.DMA((3,)), pltpu.SemaphoreType.DMA((3,)),
                   pltpu.SemaphoreType.DMA((nb,)), pltpu.SemaphoreType.DMA((nb,))]
        + [pltpu.HBM(f.shape, f.dtype) for f in fulls] + [jax.ShapeDtypeStruct((8, LANES), F32)],
        in_specs=[hbm] * n + [ANY], out_specs=[sem] * 4 + [hbm] * n + [VM],
        input_output_aliases={k: 4 + k for k in range(n)},
        compiler_params=pltpu.CompilerParams(**_SPLIT),
    )(*[pltpu.with_memory_space_constraint(f, pltpu.HBM) for f in fulls], after)
    return (out[0], out[1]), (out[2], out[3]), out[4:4 + n], out[-1]


def _gather_weights_wait(fulls, kinds, sems, which, after, name):
    n = len(fulls)
    hbm = pl.BlockSpec(memory_space=pltpu.HBM)
    sem = pl.BlockSpec(memory_space=pltpu.SEMAPHORE)
    keep = [0] if which == 0 else list(range(1, n))

    def body(*refs):
        m = len(keep)
        full_refs = [None] * n
        for pos_, k in enumerate(keep):
            full_refs[k] = refs[pos_]
        s_ref, r_ref = refs[m:m + 2]
        if which == 0:
            sets = _gather_copies([full_refs[0]], kinds[:1], (s_ref, r_ref), None)[0]
        else:
            sets = _gather_copies([None] + [full_refs[k] for k in keep], kinds, None, (s_ref, r_ref))[1]
        for out_cp, in_cp in sets:
            out_cp.wait_send()
            in_cp.wait_recv()

    out = pl.pallas_call(
        body, name=name,
        out_shape=[pltpu.HBM(fulls[k].shape, fulls[k].dtype) for k in keep],
        in_specs=[hbm] * len(keep) + [sem, sem] + [ANY] * len(after), out_specs=[hbm] * len(keep),
        input_output_aliases={i: i for i in range(len(keep))},
        compiler_params=pltpu.CompilerParams(**_SPLIT),
    )(*[fulls[k] for k in keep], sems[0], sems[1], *after)
    return list(out)


def _row_gather_copies(buf, m_per, send_sems, recv_sems):
    x, y, c = _my_place()
    pairs = []
    for idx in range(1, 8):
        dx, dy, dc = idx // 4, (idx // 2) % 2, idx % 2
        peer = ((x + dx) % 2, (y + dy) % 2, (c + dc) % 2)
        mine = buf.at[pl.ds((4 * x + 2 * y + c) * m_per, m_per), :]
        theirs = buf.at[pl.ds((4 * peer[0] + 2 * peer[1] + peer[2]) * m_per, m_per), :]
        pairs.append(tuple(
            pltpu.make_async_remote_copy(src_ref=w, dst_ref=w, send_sem=send_sems.at[idx - 1],
                                         recv_sem=recv_sems.at[idx - 1], device_id=peer, device_id_type=MESH)
            for w in (mine, theirs)))
    return pairs


def _row_gather_start(buf, m_per, after):
    hbm = pl.BlockSpec(memory_space=pltpu.HBM)
    sem = pl.BlockSpec(memory_space=pltpu.SEMAPHORE)

    def body(buf_ref, after_ref, send_sems, recv_sems, out_ref, token):
        del after_ref, out_ref
        for out_cp, _ in _row_gather_copies(buf_ref, m_per, send_sems, recv_sems):
            out_cp.start()
        token[...] = jnp.zeros_like(token)

    return pl.pallas_call(
        body, name="row_gather_start",
        out_shape=[pltpu.SemaphoreType.DMA((7,)), pltpu.SemaphoreType.DMA((7,)), pltpu.HBM(buf.shape, buf.dtype),
                   jax.ShapeDtypeStruct((8, LANES), F32)],
        in_specs=[hbm, ANY], out_specs=[sem, sem, hbm, VM], input_output_aliases={0: 2},
        compiler_params=pltpu.CompilerParams(**_SPLIT),
    )(pltpu.with_memory_space_constraint(buf, pltpu.HBM), after)


def _row_gather_wait(send_sems, recv_sems, buf, m_per, after):
    hbm = pl.BlockSpec(memory_space=pltpu.HBM)
    sem = pl.BlockSpec(memory_space=pltpu.SEMAPHORE)

    def body(buf_ref, send_s, recv_s, *rest):
        for out_cp, in_cp in _row_gather_copies(buf_ref, m_per, send_s, recv_s):
            out_cp.wait_send()
            in_cp.wait_recv()

    return pl.pallas_call(
        body, name="row_gather_wait", out_shape=pltpu.HBM(buf.shape, buf.dtype),
        in_specs=[hbm, sem, sem] + [ANY] * len(after), out_specs=hbm, input_output_aliases={0: 0},
        compiler_params=pltpu.CompilerParams(**_SPLIT),
    )(buf, send_sems, recv_sems, *after)


def _pass_to_sibling(w_full, kind):
    def body(w_in_ref, w_ref, send_sems, recv_sems):
        del w_in_ref
        x, y, c = _my_place()
        chips = [(1 - x, y), (x, 1 - y), (1 - x, 1 - y)]
        cps = []
        for j, chip in enumerate(chips):
            win = _landed(w_ref, kind, chip, c)
            cp = pltpu.make_async_remote_copy(
                src_ref=win, dst_ref=win, send_sem=send_sems.at[j], recv_sem=recv_sems.at[j],
                device_id=(x, y, 1 - c), device_id_type=MESH)
            cp.start()
            cps.append(cp)
        for j, chip in enumerate(chips):
            theirs = _landed(w_ref, kind, chip, 1 - c)
            pltpu.make_async_remote_copy(
                src_ref=theirs, dst_ref=theirs, send_sem=send_sems.at[j], recv_sem=recv_sems.at[j],
                device_id=(x, y, 1 - c), device_id_type=MESH).wait_recv()
        for cp in cps:
            cp.wait_send()

    return pl.pallas_call(
        body, name="pass_to_sibling",
        out_shape=jax.ShapeDtypeStruct(w_full.shape, w_full.dtype),
        in_specs=[ANY], out_specs=ANY, input_output_aliases={0: 0},
        scratch_shapes=[pltpu.SemaphoreType.DMA((3,)), pltpu.SemaphoreType.DMA((3,))],
    )(w_full)


def _reduce_pair_exchange(grads, kinds):
    n = len(grads)
    half_shapes = [(g.shape[0] // 2, g.shape[1]) if kind == "col" else (g.shape[0], g.shape[1] // 2)
                   for g, kind in zip(grads, kinds)]

    def body(*refs):
        g_refs, pa_refs = refs[:n], refs[n:2 * n]
        send_sems, recv_sems = refs[2 * n:]
        x, y, c = _my_place()
        cps = []
        for k in range(n):
            cp = pltpu.make_async_remote_copy(
                src_ref=_half_window(g_refs[k], kinds[k], 1 - c), dst_ref=pa_refs[k],
                send_sem=send_sems.at[k], recv_sem=recv_sems.at[k],
                device_id=(x, y, 1 - c), device_id_type=MESH)
            cp.start()
            cps.append(cp)
        for cp in cps:
            cp.wait()

    return pl.pallas_call(
        body, name="reduce_pair_exchange",
        out_shape=[jax.ShapeDtypeStruct(s, F32) for s in half_shapes],
        in_specs=[ANY] * n, out_specs=[ANY] * n,
        scratch_shapes=[pltpu.SemaphoreType.DMA((n,)), pltpu.SemaphoreType.DMA((n,))],
    )(*grads)


def _row_tile(rows, cols, itemsize=4, target=2 * 1024 * 1024, mult=16):
    t = rows
    while t % 2 == 0 and t // 2 >= mult and (t // 2) % mult == 0 and t * cols * itemsize > target:
        t //= 2
    return t


def _reduce_pair_sum(g, pa, kind, c_arr, name):
    hr, hc_ = pa.shape
    tr = _row_tile(hr, hc_)
    nb = hr // tr

    def body(c_ref, g_ref, pa_ref, o_ref):
        o_ref[...] = (g_ref[...] + pa_ref[...]).astype(BF16)

    if kind == "col":
        g_map = lambda i, c_ref: (c_ref[0] * nb + i, 0)
    else:
        g_map = lambda i, c_ref: (i, c_ref[0])
    return pl.pallas_call(
        body, name=name,
        grid_spec=pltpu.PrefetchScalarGridSpec(
            num_scalar_prefetch=1, grid=(nb,),
            in_specs=[pl.BlockSpec((tr, hc_), g_map), pl.BlockSpec((tr, hc_), lambda i, c_ref: (i, 0))],
            out_specs=pl.BlockSpec((tr, hc_), lambda i, c_ref: (i, 0))),
        out_shape=jax.ShapeDtypeStruct((hr, hc_), BF16),
        compiler_params=_params(("parallel",)),
    )(c_arr, g, pa)


def _half_shard_shape(full_shape, kind):
    r, c = full_shape
    return (r // 2, c // 4) if kind == "col" else (r // 4, c // 2)


def _to_owner_copies(h_refs, land_refs, send_sems, recv_sems, kinds):
    n = len(h_refs)
    x, y, c = _my_place()
    chips = [(1 - x, y), (x, 1 - y), (1 - x, 1 - y)]
    cps = []
    for j, chip in enumerate(chips):
        pj = 2 * chip[0] + chip[1]
        for k in range(n):
            cps.append(pltpu.make_async_remote_copy(
                src_ref=_shard_window(h_refs[k], kinds[k], pj), dst_ref=land_refs[k].at[j],
                send_sem=send_sems.at[j * n + k], recv_sem=recv_sems.at[j * n + k],
                device_id=(*chip, c), device_id_type=MESH))
    return cps


def _reduce_to_owner_start(halves, kinds, full_shapes):
    n = len(halves)
    hs = [_half_shard_shape(fs, kind) for fs, kind in zip(full_shapes, kinds)]
    hbm = pl.BlockSpec(memory_space=pltpu.HBM)
    sem = pl.BlockSpec(memory_space=pltpu.SEMAPHORE)

    def body(*refs):
        h_refs, land_refs = refs[:n], refs[n:2 * n]
        send_sems, recv_sems = refs[2 * n:2 * n + 2]
        token = refs[-1]
        for cp in _to_owner_copies(h_refs, land_refs, send_sems, recv_sems, kinds):
            cp.start()
        token[...] = jnp.zeros_like(token)

    lands = [pltpu.with_memory_space_constraint(lax.empty((3,) + s, BF16), pltpu.HBM) for s in hs]
    out = pl.pallas_call(
        body, name="reduce_to_owner_start",
        out_shape=[pltpu.SemaphoreType.DMA((3 * n,)), pltpu.SemaphoreType.DMA((3 * n,))]
        + [pltpu.HBM(h.shape, h.dtype) for h in halves] + [pltpu.HBM((3,) + s, BF16) for s in hs]
        + [jax.ShapeDtypeStruct((8, LANES), F32)],
        in_specs=[hbm] * (2 * n), out_specs=[sem, sem] + [hbm] * (2 * n) + [VM],
        input_output_aliases={k: 2 + k for k in range(2 * n)},
        compiler_params=pltpu.CompilerParams(has_side_effects=pltpu.SideEffectType.DATAFLOW_SIDE_EFFECTING),
    )(*[pltpu.with_memory_space_constraint(h, pltpu.HBM) for h in halves], *lands)
    return out[0], out[1], out[2:2 + n], out[2 + n:2 + 2 * n], out[-1]


def _reduce_to_owner_wait(send_sems, recv_sems, halves, lands, after, kinds):
    n = len(halves)
    hbm = pl.BlockSpec(memory_space=pltpu.HBM)
    sem = pl.BlockSpec(memory_space=pltpu.SEMAPHORE)

    def body(*refs):
        h_refs, land_refs = refs[:n], refs[n:2 * n]
        send_s, recv_s = refs[2 * n:2 * n + 2]
        for cp in _to_owner_copies(h_refs, land_refs, send_s, recv_s, kinds):
            cp.wait_send()
            cp.wait_recv()

    out = pl.pallas_call(
        body, name="reduce_to_owner_wait",
        out_shape=[pltpu.HBM(h.shape, h.dtype) for h in halves] + [pltpu.HBM(l.shape, l.dtype) for l in lands],
        in_specs=[hbm] * (2 * n) + [sem, sem, ANY], out_specs=[hbm] * (2 * n),
        input_output_aliases={k: k for k in range(2 * n)},
        compiler_params=pltpu.CompilerParams(has_side_effects=pltpu.SideEffectType.DATAFLOW_SIDE_EFFECTING),
    )(*halves, *lands, send_sems, recv_sems, after)
    return out[:n], out[n:]


def _reduce_finish(halves, recvd, kinds, full_shapes):
    n = len(halves)
    hs = [_half_shard_shape(fs, kind) for fs, kind in zip(full_shapes, kinds)]
    shard_shapes = [(fs[0], fs[1] // 4) if kind == "col" else (fs[0] // 4, fs[1])
                    for fs, kind in zip(full_shapes, kinds)]

    def body(*refs):
        h_refs, rc_refs, gs_refs = refs[:n], refs[n:2 * n], refs[2 * n:3 * n]
        own_refs, gh_refs = refs[3 * n:4 * n], refs[4 * n:5 * n]
        in_sems, loc_sems, send_sems, recv_sems = refs[5 * n:]
        x, y, c = _my_place()
        p = 2 * x + y
        loads = []
        for k in range(n):
            cp = pltpu.make_async_copy(_shard_window(h_refs[k], kinds[k], p), own_refs[k], in_sems.at[k])
            cp.start()
            loads.append(cp)
        outs = []
        for k in range(n):
            loads[k].wait()
            gh_refs[k][...] = (own_refs[k][...].astype(F32) + rc_refs[k][0].astype(F32)
                               + rc_refs[k][1].astype(F32) + rc_refs[k][2].astype(F32))
            dst = _half_window(gs_refs[k], kinds[k], c)
            lc = pltpu.make_async_copy(gh_refs[k], dst, loc_sems.at[k])
            lc.start()
            rc = pltpu.make_async_remote_copy(
                src_ref=gh_refs[k], dst_ref=dst, send_sem=send_sems.at[k], recv_sem=recv_sems.at[k],
                device_id=(x, y, 1 - c), device_id_type=MESH)
            rc.start()
            outs.append((lc, rc))
        for k, (lc, rc) in enumerate(outs):
            lc.wait()
            rc.wait_send()
            pltpu.make_async_remote_copy(
                src_ref=gh_refs[k], dst_ref=_half_window(gs_refs[k], kinds[k], 1 - c),
                send_sem=send_sems.at[k], recv_sem=recv_sems.at[k],
                device_id=(x, y, 1 - c), device_id_type=MESH).wait_recv()

    return pl.pallas_call(
        body, name="reduce_finish",
        out_shape=[jax.ShapeDtypeStruct(s, F32) for s in shard_shapes],
        in_specs=[ANY] * n + [VM] * n, out_specs=[ANY] * n,
        scratch_shapes=[pltpu.VMEM(s, BF16) for s in hs] + [pltpu.VMEM(s, F32) for s in hs]
        + [pltpu.SemaphoreType.DMA((n,)) for _ in range(4)],
        compiler_params=_params(),
    )(*halves, *recvd)


def _mod_part(c_all, w_ada_l, b_l):
    def body(c_ref, w_ref, b_ref, o_ref):
        o_ref[...] = _dot(c_ref[...].astype(BF16), w_ref[...].astype(BF16)) + b_ref[...]

    return pl.pallas_call(
        body, name="mod_part", out_shape=jax.ShapeDtypeStruct((8, w_ada_l.shape[1]), F32),
        in_specs=[VM, VM, VM], out_specs=VM, compiler_params=_params(),
    )(c_all, w_ada_l, b_l)


def _sum_devices(parts, m):
    def body(p_ref, o_ref):
        acc = p_ref[0:m, :]
        for d in range(1, 8):
            acc = acc + p_ref[d * m:(d + 1) * m, :]
        o_ref[...] = acc

    return pl.pallas_call(
        body, name="sum_devices", out_shape=jax.ShapeDtypeStruct((m, parts.shape[1]), F32),
        in_specs=[VM], out_specs=VM, compiler_params=_params(),
    )(parts)


def _grad_w_ada(c_all_t, dmod_l):
    d, w = c_all_t.shape[0], dmod_l.shape[1]

    def body(ct_ref, dm_ref, o_ref):
        acc = ct_ref[:, 0:1] * dm_ref[0:1, :]
        for b in range(1, 8):
            acc = acc + ct_ref[:, b:b + 1] * dm_ref[b:b + 1, :]
        o_ref[...] = acc

    return pl.pallas_call(
        body, name="grad_w_ada", out_shape=jax.ShapeDtypeStruct((d, w), F32),
        in_specs=[VM, VM], out_specs=VM, compiler_params=_params(),
    )(c_all_t, dmod_l)


def _adamw_math(w, g, m, v):
    nm = ADAM_B1 * m + (1.0 - ADAM_B1) * g
    nv = ADAM_B2 * v + (1.0 - ADAM_B2) * (g * g)
    m_hat = nm / (1.0 - ADAM_B1 ** ADAM_STEP)
    v_hat = nv / (1.0 - ADAM_B2 ** ADAM_STEP)
    return -ADAM_LR * (m_hat / (jnp.sqrt(v_hat) + ADAM_EPS) + ADAM_WD * w), nm, nv


def _adamw_small(ws, g8, ms, vs):
    shapes = [jax.ShapeDtypeStruct(w.shape, F32) for w in ws]

    def body(*refs):
        w_refs, g_ref, m_refs, v_refs, outs = refs[0:6], refs[6], refs[7:13], refs[13:19], refs[19:]
        for idx in range(6):
            if idx < 5:
                g = g_ref[idx:idx + 1, :]
            else:
                g = jnp.concatenate([g_ref[5:6, :], g_ref[6:7, :], g_ref[7:8, :]], axis=1)
            res = (g,) + _adamw_math(w_refs[idx][...], g, m_refs[idx][...], v_refs[idx][...])
            for fam in range(4):
                outs[6 * fam + idx][...] = res[fam]

    return pl.pallas_call(
        body, name="adamw_small", out_shape=shapes * 4,
        in_specs=[VM] * 19, out_specs=[VM] * 24, compiler_params=_params(),
    )(*ws, g8, *ms, *vs)


def _adamw(w, g, m, v, name, after=()):
    r, c = w.shape
    tr = _row_tile(r, c, target=1024 * 1024, mult=8)

    def body(w_ref, g_ref, m_ref, v_ref, *rest):
        d_ref, nm_ref, nv_ref = rest[len(after):]
        d_ref[...], nm_ref[...], nv_ref[...] = _adamw_math(w_ref[...], g_ref[...], m_ref[...], v_ref[...])

    spec = pl.BlockSpec((tr, c), lambda i: (i, 0))
    return pl.pallas_call(
        body, name=name, grid=(r // tr,),
        out_shape=[jax.ShapeDtypeStruct((r, c), F32)] * 3,
        in_specs=[spec] * 4 + [ANY] * len(after), out_specs=[spec] * 3,
        compiler_params=_params(("parallel",)),
    )(w, g, m, v, *after)


def _rope_coefficients(pos, ropetab, ts):
    s_len = pos.shape[0]

    def body(pos_ref, tab_ref, o_ref):
        ang = pos_ref[...].astype(F32) * tab_ref[0:1, :]
        cs, sn = jnp.cos(ang), jnp.sin(ang)
        o_ref[0] = jnp.where(tab_ref[3:4, :] > 0, cs, 1.0)
        o_ref[1] = -sn * tab_ref[1:2, :]
        o_ref[2] = sn * tab_ref[2:3, :]

    return pl.pallas_call(
        body, name="rope_coefficients", grid=(s_len // ts,),
        in_specs=[pl.BlockSpec((ts, 1), lambda i: (i, 0)), pl.BlockSpec((8, LANES), lambda i: (0, 0))],
        out_specs=pl.BlockSpec((3, ts, LANES), lambda i: (0, i, 0)),
        out_shape=jax.ShapeDtypeStruct((3, s_len, LANES), F32),
        compiler_params=_params(("parallel",)),
    )(pos, ropetab)


def _deinterleave_store(vals, slab, out_ref, d, ts, dtype):
    if d == 1:
        for s in range(4):
            out_ref[0, :, s * LANES:(s + 1) * LANES] = vals[s].astype(dtype)
        return
    for s in range(4):
        slab[s] = vals[s]
    if d == 16:
        q = ts // 4
        for s in range(4):
            for r1 in range(4):
                slab[4 + s, r1 * q:(r1 + 1) * q, :] = slab[s, pl.ds(r1, q, stride=4), :]
        for r in range(d):
            r1, r2 = r % 4, r // 4
            for s in range(4):
                out_ref[r, :, s * LANES:(s + 1) * LANES] = slab[
                    4 + s, pl.ds(r1 * q + r2, ts // d, stride=4), :].astype(dtype)
        return
    for r in range(d):
        for s in range(4):
            out_ref[r, :, s * LANES:(s + 1) * LANES] = slab[s, pl.ds(r, ts // d, stride=d), :].astype(dtype)


def _interleave_load(blk_ref, slab, d, ts):
    if d == 1:
        return [blk_ref[0, :, s * LANES:(s + 1) * LANES] for s in range(4)]
    if d == 16:
        q = ts // 4
        for r in range(d):
            r1, r2 = r % 4, r // 4
            for s in range(4):
                slab[4 + s, pl.ds(r1 * q + r2, ts // d, stride=4), :] = blk_ref[r, :, s * LANES:(s + 1) * LANES]
        for s in range(4):
            for r1 in range(4):
                slab[s, pl.ds(r1, q, stride=4), :] = slab[4 + s, r1 * q:(r1 + 1) * q, :]
        return [slab[s] for s in range(4)]
    for r in range(d):
        for s in range(4):
            slab[s, pl.ds(r, ts // d, stride=d), :] = blk_ref[r, :, s * LANES:(s + 1) * LANES]
    return [slab[s] for s in range(4)]


def _norm_modulate(x, modv, norm_g, ts):
    s_len, d_model = x.shape

    def body(x_ref, mod_ref, g_ref, h_ref, ht_ref):
        xv = x_ref[...]
        r = lax.rsqrt(jnp.mean(xv * xv, axis=-1, keepdims=True) + EPS)
        h = (xv * r) * g_ref[...] * (1.0 + mod_ref[1:2, :]) + mod_ref[0:1, :]
        h_ref[...] = h.astype(BF16)
        ht_ref[...] = h.T.astype(BF16)

    return pl.pallas_call(
        body, name="norm_modulate", grid=(s_len // ts,),
        in_specs=[pl.BlockSpec((ts, d_model), lambda i: (i, 0)), pl.BlockSpec((8, d_model), lambda i: (0, 0)),
                  pl.BlockSpec((1, d_model), lambda i: (0, 0))],
        out_specs=[pl.BlockSpec((ts, d_model), lambda i: (i, 0)), pl.BlockSpec((d_model, ts), lambda i: (0, i))],
        out_shape=[jax.ShapeDtypeStruct((s_len, d_model), BF16), jax.ShapeDtypeStruct((d_model, s_len), BF16)],
        compiler_params=_params(("parallel",)),
    )(x, modv, norm_g)


def _plain_projection(h, w_in, first_tile, n_tiles, ts, name):
    s_len, d_model = h.shape
    ct = COL_TILE

    def body(h_ref, w_ref, o_ref):
        o_ref[...] = _dot(h_ref[...], w_ref[...])

    return pl.pallas_call(
        body, name=name, grid=(s_len // ts, n_tiles),
        in_specs=[pl.BlockSpec((ts, d_model), lambda i, j: (i, 0)),
                  pl.BlockSpec((d_model, ct), lambda i, j: (0, first_tile + j))],
        out_specs=pl.BlockSpec((ts, ct), lambda i, j: (i, j)),
        out_shape=jax.ShapeDtypeStruct((s_len, n_tiles * ct), F32),
        compiler_params=_params(("parallel", "arbitrary")),
    )(h, w_in)


def _qkv_projection(h, rope, w_in, gi, ts):
    s_len, d_model = h.shape
    dil = DILATIONS[gi]
    nc = 3 * d_model // COL_TILE
    hr = ts // 2
    assert COL_TILE == ATT_W and hr % (16 * dil) == 0

    def body(h_ref, rope_ref, wq_ref, wk_ref, wv_ref, o_ref, slab):
        rc, ra, rb = rope_ref.at[0], rope_ref.at[1], rope_ref.at[2]
        w_refs = (wq_ref, wk_ref, wv_ref)
        units = [(w, half) for w in range(3) for half in range(2)]

        def matmul(w, half):
            return _dot(h_ref[half * hr:(half + 1) * hr, :], w_refs[w][...])

        def finish(w, half, res):
            rows = slice(half * hr, (half + 1) * hr)
            vals = []
            for s in range(4):
                t = res[:, s * LANES:(s + 1) * LANES]
                if w < 2:
                    t = (t * rc[rows, :] + pltpu.roll(t, LANES - 8, 1) * ra[rows, :]
                         + pltpu.roll(t, 8, 1) * rb[rows, :])
                vals.append(t)
            out = o_ref.at[w, :, half * (hr // dil):(half + 1) * (hr // dil), :]
            _deinterleave_store(vals, slab.at[half], out, dil, hr, BF16)

        res_next = matmul(*units[0])
        for ui, (w, half) in enumerate(units):
            res = res_next
            if ui + 1 < len(units):
                res_next = matmul(*units[ui + 1])
            finish(w, half, res)

    def w_spec(w):
        return pl.BlockSpec((d_model, COL_TILE), lambda i: (0, nc + 3 * w + gi))

    return pl.pallas_call(
        body, name="qkv_projection_%d" % dil, grid=(s_len // ts,),
        in_specs=[pl.BlockSpec((ts, d_model), lambda i: (i, 0)), pl.BlockSpec((3, ts, LANES), lambda i: (0, i, 0)),
                  w_spec(0), w_spec(1), w_spec(2)],
        out_specs=pl.BlockSpec((3, dil, ts // dil, ATT_W), lambda i: (0, 0, i, 0)),
        out_shape=jax.ShapeDtypeStruct((3, dil, s_len // dil, ATT_W), BF16),
        scratch_shapes=[pltpu.VMEM((2, 8, hr, LANES), F32)],
        compiler_params=_params(("parallel",)),
    )(h, rope, w_in, w_in, w_in)


def _layernorm_stats(u1):
    mu = jnp.mean(u1, axis=-1, keepdims=True)
    xc = u1 - mu
    rstd = lax.rsqrt(jnp.mean(xc * xc, axis=-1, keepdims=True) + EPS)
    return xc * rstd, rstd


def _shifted_copies(win, shf, ts):
    rows = ts + HALO - 8
    for b in range(1, 8):
        shf[b - 1, 0:rows, :] = win[pl.ds(b, rows), :]


def _tap(win, shf, off, r0, rows):
    a, b = divmod(off, 8)
    start = 8 * a + r0
    if b == 0:
        return win[start:start + rows, :]
    return shf[b - 1, start:start + rows, :]


def _conv_forward(p_conv, conv_w, conv_b, ln_g, ln_b, w_co, ts):
    s_len, d3 = p_conv.shape
    dm = d3 // 3

    def body(p_ref, cw_ref, cb_ref, g_ref, b_ref, w_ref, y_ref, u0_ref, u1_ref, win, shf):
        i = pl.program_id(0)

        @pl.when(i == 0)
        def _():
            win[0:HALO, :] = jnp.zeros((HALO, dm), F32)

        a, b, z = p_ref[:, 0:dm], p_ref[:, dm:2 * dm], p_ref[:, 2 * dm:3 * dm]
        u0 = a * _sig(b)
        win[HALO:HALO + ts, :] = u0
        u0_ref[...] = u0
        _shifted_copies(win, shf, ts)
        for r0 in range(0, ts, CONV_ROWS_FWD):
            acc = jnp.broadcast_to(cb_ref[...], (CONV_ROWS_FWD, dm))
            for k in range(CONV_K):
                acc = acc + cw_ref[k:k + 1, :] * _tap(win, shf, HALO - (CONV_K - 1) + k, r0, CONV_ROWS_FWD)
            u1_ref[r0:r0 + CONV_ROWS_FWD, :] = acc
        xh, _ = _layernorm_stats(u1_ref[...])
        u2 = xh * g_ref[...] + b_ref[...]
        a_conv = (u2 * _sig(u2)) * (z * _sig(z))
        y_ref[...] = _dot(a_conv.astype(BF16), w_ref[...])
        win[0:HALO, :] = win[ts:ts + HALO, :]

    row = pl.BlockSpec((1, dm), lambda i: (0, 0))
    tile = pl.BlockSpec((ts, dm), lambda i: (i, 0))
    return pl.pallas_call(
        body, name="conv_forward", grid=(s_len // ts,),
        in_specs=[pl.BlockSpec((ts, d3), lambda i: (i, 0)),
                  pl.BlockSpec((HALO, dm), lambda i: (0, 0)), row, row, row,
                  pl.BlockSpec((dm, dm), lambda i: (0, 0))],
        out_specs=[tile, tile, tile],
        out_shape=[jax.ShapeDtypeStruct((s_len, dm), F32)] * 3,
        scratch_shapes=[pltpu.VMEM((ts + HALO, dm), F32), pltpu.VMEM((7, ts + HALO - 8, dm), F32)],
        compiler_params=_params(("arbitrary",)),
    )(p_conv, conv_w, conv_b, ln_g, ln_b, w_co)


def _conv_backward(dp, dyc, p_conv, u0, u1, conv_w, ln_g, ln_b, w_co, ts):
    s_len, d3 = p_conv.shape
    dm = d3 // 3
    nt = s_len // ts
    hb = ts // HALO

    def body(dp_in, dy_ref, p_ref, u0_ref, uh_ref, u1_ref, cw_ref, g_ref, b_ref, w_ref,
             dp_ref, gw_ref, gs_ref, gcw_ref, dwin, uwin, shf):
        del dp_in
        i = pl.program_id(0)
        ti = nt - 1 - i

        @pl.when(i == 0)
        def _():
            gw_ref[...] = jnp.zeros_like(gw_ref)
            gs_ref[...] = jnp.zeros_like(gs_ref)
            gcw_ref[...] = jnp.zeros_like(gcw_ref)
            dwin[ts:ts + HALO, :] = jnp.zeros((HALO, dm), F32)

        dy = dy_ref[...]
        z = p_ref[:, 2 * dm:3 * dm]
        d_ac = _dot_nt(dy, w_ref[...])
        xh, rstd = _layernorm_stats(u1_ref[...])
        u2 = xh * g_ref[...] + b_ref[...]
        sg2, sgz = _sig(u2), _sig(z)
        u3, sz = u2 * sg2, z * sgz
        gw_ref[...] += _dot_tn((u3 * sz).astype(BF16), dy)
        d_z = d_ac * u3 * _dsilu(z, sgz)
        d_u2 = d_ac * sz * _dsilu(u2, sg2)
        gs_ref[0:1, :] += jnp.sum(d_u2 * xh, axis=0, keepdims=True)
        gs_ref[1:2, :] += jnp.sum(d_u2, axis=0, keepdims=True)
        dxh = d_u2 * g_ref[...]
        d_u1 = rstd * (dxh - jnp.mean(dxh, axis=-1, keepdims=True)
                       - xh * jnp.mean(dxh * xh, axis=-1, keepdims=True))
        gs_ref[2:3, :] += jnp.sum(d_u1, axis=0, keepdims=True)
        dwin[0:ts, :] = d_u1
        uwin[0:HALO, :] = jnp.where(ti == 0, 0.0, uh_ref[...])
        uwin[HALO:HALO + ts, :] = u0_ref[...]
        dp_ref[:, 2 * dm:3 * dm] = d_z.astype(BF16)
        _shifted_copies(uwin, shf, ts)
        for k in range(CONV_K):
            part = jnp.zeros((CONV_ROWS_BWD, dm), F32)
            for r0 in range(0, ts, CONV_ROWS_BWD):
                part = part + dwin[r0:r0 + CONV_ROWS_BWD, :] * _tap(uwin, shf, HALO - (CONV_K - 1) + k, r0,
                                                                    CONV_ROWS_BWD)
            gcw_ref[k:k + 1, :] += jnp.sum(part, axis=0, keepdims=True)
        _shifted_copies(dwin, shf, ts)
        for r0 in range(0, ts, CONV_ROWS_BWD):
            d_u0 = jnp.zeros((CONV_ROWS_BWD, dm), F32)
            for k in range(CONV_K):
                d_u0 = d_u0 + cw_ref[k:k + 1, :] * _tap(dwin, shf, CONV_K - 1 - k, r0, CONV_ROWS_BWD)
            rows = slice(r0, r0 + CONV_ROWS_BWD)
            sgb = _sig(p_ref[rows, dm:2 * dm])
            dp_ref[rows, 0:dm] = (d_u0 * sgb).astype(BF16)
            dp_ref[rows, dm:2 * dm] = (d_u0 * p_ref[rows, 0:dm] * sgb * (1.0 - sgb)).astype(BF16)
        dwin[ts:ts + HALO, :] = dwin[0:HALO, :]

    rev = lambda i: (nt - 1 - i, 0)
    row = pl.BlockSpec((1, dm), lambda i: (0, 0))
    tile = pl.BlockSpec((ts, dm), rev)
    return pl.pallas_call(
        body, name="conv_backward", grid=(nt,),
        in_specs=[ANY, tile, pl.BlockSpec((ts, d3), rev), tile,
                  pl.BlockSpec((HALO, dm), lambda i: (jnp.maximum((nt - 1 - i) * hb - 1, 0), 0)),
                  tile, pl.BlockSpec((HALO, dm), lambda i: (0, 0)), row, row,
                  pl.BlockSpec((dm, dm), lambda i: (0, 0))],
        out_specs=[pl.BlockSpec((ts, d3), rev), pl.BlockSpec((dm, dm), lambda i: (0, 0)),
                   pl.BlockSpec((8, dm), lambda i: (0, 0)), pl.BlockSpec((HALO, dm), lambda i: (0, 0))],
        out_shape=[jax.ShapeDtypeStruct(dp.shape, BF16), jax.ShapeDtypeStruct((dm, dm), F32),
                   jax.ShapeDtypeStruct((8, dm), F32), jax.ShapeDtypeStruct((HALO, dm), F32)],
        input_output_aliases={0: 0},
        scratch_shapes=[pltpu.VMEM((ts + HALO, dm), F32), pltpu.VMEM((ts + HALO, dm), F32),
                        pltpu.VMEM((7, ts + HALO - 8, dm), F32)],
        compiler_params=_params(("arbitrary",)),
    )(dp, dyc, p_conv, u0, u0, u1, conv_w, ln_g, ln_b, w_co)


def _att_masks():
    head0 = lax.broadcasted_iota(jnp.int32, (BLK, LANES), 1) < 64
    col = lax.broadcasted_iota(jnp.int32, (BLK, 4 * BLK), 1)
    row = lax.broadcasted_iota(jnp.int32, (BLK, 4 * BLK), 0)
    kj = col % BLK
    prev = jnp.where(col < 2 * BLK, 1, 0)
    band = jnp.where(col < 2 * BLK, kj - row, row - kj)
    return head0, band, prev


def _fill_block_diagonal(dst, slab, src_ref, halo_ref, head0, nq):
    sl = slice(slab * LANES, (slab + 1) * LANES)
    for b in range(nq + 1):
        blk = halo_ref[:, sl] if b == 0 else src_ref[(b - 1) * BLK:b * BLK, sl]
        base = (slab * (nq + 1) + b) * 2 * BLK
        zero = jnp.zeros_like(blk)
        dst[base:base + BLK, :] = jnp.where(head0, blk, zero)
        dst[base + BLK:base + 2 * BLK, :] = jnp.where(head0, zero, blk)


def _attention_forward(qkv, seq_len, qt, name):
    s_len = qkv.shape[1]
    nq = qt // BLK
    tiles_per_seq = seq_len // qt

    def body(q_ref, k_ref, v_ref, kh_ref, vh_ref, o_ref, lse_ref, kbd, vbd):
        first = jnp.where((pl.program_id(0) % tiles_per_seq) == 0, 4 * BLK, 0)
        head0, band, prev = _att_masks()
        band_first = band - prev * first
        for p in range(4):
            _fill_block_diagonal(kbd, p, k_ref, kh_ref, head0, nq)
            _fill_block_diagonal(vbd, p, v_ref, vh_ref, head0, nq)
        units = [(p, n) for p in range(4) for n in range(nq)]

        def keys_of(p, n):
            base = (p * (nq + 1) + n) * 2 * BLK
            return slice(base, base + 4 * BLK)

        def scores(p, n):
            q2 = q_ref[n * BLK:(n + 1) * BLK, p * LANES:(p + 1) * LANES] * 0.125
            return _dot_nt(q2, kbd[keys_of(p, n), :])

        def finish(p, n, o, den, lse):
            rows, sl = slice(n * BLK, (n + 1) * BLK), slice(p * LANES, (p + 1) * LANES)
            o_ref[rows, sl] = o / jnp.where(head0, den[0], den[1])
            lse_ref[rows, sl] = jnp.where(head0, lse[0], lse[1])

        s_next = scores(*units[0])
        pending = None
        for ui, (p, n) in enumerate(units):
            s = s_next
            if ui + 1 < len(units):
                s_next = scores(*units[ui + 1])
            s = jnp.where((band_first if n == 0 else band) >= 0, s, NEG)
            grp = [s[:, g * BLK:(g + 1) * BLK] for g in range(4)]
            ps, den, lse = [None] * 4, [], []
            for h in range(2):
                m = jnp.max(jnp.maximum(grp[h], grp[2 + h]), axis=-1, keepdims=True)
                ps[h], ps[2 + h] = jnp.exp(grp[h] - m), jnp.exp(grp[2 + h] - m)
                dn = jnp.sum(ps[h] + ps[2 + h], axis=-1, keepdims=True)
                den.append(dn)
                lse.append(m + jnp.log(dn))
            pmat = jnp.concatenate([x.astype(BF16) for x in ps], axis=1)
            o = _dot(pmat, vbd[keys_of(p, n), :])
            if pending is not None:
                finish(*pending)
            pending = (p, n, o, den, lse)
        finish(*pending)

    def which(w):
        return pl.BlockSpec((None, qt, ATT_W), lambda i: (w, i, 0))

    def halo(w):
        return pl.BlockSpec((None, BLK, ATT_W), lambda i: (w, jnp.maximum(i * nq - 1, 0), 0))

    out = pl.BlockSpec((qt, ATT_W), lambda i: (i, 0))
    bd = pltpu.VMEM((4 * (nq + 1) * 2 * BLK, LANES), BF16)
    return pl.pallas_call(
        body, name=name, grid=(s_len // qt,),
        in_specs=[which(0), which(1), which(2), halo(1), halo(2)],
        out_specs=[out, out], out_shape=[jax.ShapeDtypeStruct((s_len, ATT_W), F32)] * 2,
        scratch_shapes=[bd, bd],
        compiler_params=_params(("parallel",)),
    )(qkv, qkv, qkv, qkv, qkv)


def _attention_backward(qkv, d_att, lse, delta, seq_len, qt, name):
    s_len = qkv.shape[1]
    nq = qt // BLK
    tiles_per_seq = seq_len // qt
    nblk = s_len // BLK

    def body(q_ref, k_ref, v_ref, kh_ref, vh_ref, do_ref, lse_ref, dl_ref,
             qn_ref, don_ref, lsen_ref, dln_ref, dqkv_ref, kbd, vbd):
        i = pl.program_id(0)
        first = jnp.where((i % tiles_per_seq) == 0, 4 * BLK, 0)
        last = jnp.where((i % tiles_per_seq) == tiles_per_seq - 1, 4 * BLK, 0)
        head0, band, prev = _att_masks()
        band_first = band - prev * first
        band_tail = band[:, 0:2 * BLK] - last
        for p in range(4):
            _fill_block_diagonal(kbd, p, k_ref, kh_ref, head0, nq)
            _fill_block_diagonal(vbd, p, v_ref, vh_ref, head0, nq)
        units = [(p, n) for p in range(4) for n in range(nq + 1)]

        def stage_a(p, n):
            sl = slice(p * LANES, (p + 1) * LANES)
            base = (p * (nq + 1) + n) * 2 * BLK
            if n < nq:
                rows = slice(n * BLK, (n + 1) * BLK)
                q2, do2, lse2, dl2 = q_ref[rows, sl], do_ref[rows, sl], lse_ref[rows, sl], dl_ref[rows, sl]
                keys = slice(base, base + 4 * BLK)
            else:
                q2, do2, lse2, dl2 = qn_ref[:, sl], don_ref[:, sl], lsen_ref[:, sl], dln_ref[:, sl]
                keys = slice(base, base + 2 * BLK)
            s = _dot_nt(q2 * 0.125, kbd[keys, :])
            dp = _dot_nt(do2, vbd[keys, :])
            return q2, do2, lse2, dl2, keys, s, dp

        def stage_b(n, lse2, dl2, s, dp):
            mask = band_tail if n == nq else (band_first if n == 0 else band)
            ps, dss = [], []
            for g in range(s.shape[1] // BLK):
                h = g % 2
                cols = slice(g * BLK, (g + 1) * BLK)
                pg = jnp.exp(jnp.where(mask[:, cols] >= 0, s[:, cols] - lse2[:, h * 64:h * 64 + 1], NEG))
                ps.append(pg.astype(BF16))
                dss.append((pg * (dp[:, cols] - dl2[:, h * 64:h * 64 + 1]) * 0.125).astype(BF16))
            return jnp.concatenate(ps, axis=1), jnp.concatenate(dss, axis=1)

        def heads(r, g):
            return jnp.where(head0, r[g * BLK:(g + 1) * BLK, :], r[(g + 1) * BLK:(g + 2) * BLK, :])

        a_next = stage_a(*units[0])
        carry = None
        for ui, (p, n) in enumerate(units):
            q2, do2, lse2, dl2, keys, s, dp = a_next
            if ui + 1 < len(units):
                a_next = stage_a(*units[ui + 1])
            pmat, dsmat = stage_b(n, lse2, dl2, s, dp)
            sl = slice(p * LANES, (p + 1) * LANES)
            if n < nq:
                dqkv_ref[0, n * BLK:(n + 1) * BLK, sl] = _dot(dsmat, kbd[keys, :])
            dkbd = _dot_tn(dsmat, q2)
            dvbd = _dot_tn(pmat, do2)
            if n > 0:
                prow = slice((n - 1) * BLK, n * BLK)
                dqkv_ref[1, prow, sl] = carry[0] + heads(dkbd, 0)
                dqkv_ref[2, prow, sl] = carry[1] + heads(dvbd, 0)
            if n < nq:
                carry = (heads(dkbd, 2), heads(dvbd, 2))

    def which(w):
        return pl.BlockSpec((None, qt, ATT_W), lambda i: (w, i, 0))

    def prev(w):
        return pl.BlockSpec((None, BLK, ATT_W), lambda i: (w, jnp.maximum(i * nq - 1, 0), 0))

    tile = pl.BlockSpec((qt, ATT_W), lambda i: (i, 0))
    nxt = pl.BlockSpec((BLK, ATT_W), lambda i: (jnp.minimum((i + 1) * nq, nblk - 1), 0))
    nxt_q = pl.BlockSpec((None, BLK, ATT_W), lambda i: (0, jnp.minimum((i + 1) * nq, nblk - 1), 0))
    return pl.pallas_call(
        body, name=name, grid=(s_len // qt,),
        in_specs=[which(0), which(1), which(2), prev(1), prev(2), tile, tile, tile, nxt_q, nxt, nxt, nxt],
        out_specs=pl.BlockSpec((3, qt, ATT_W), lambda i: (0, i, 0)),
        out_shape=jax.ShapeDtypeStruct((3, s_len, ATT_W), F32),
        scratch_shapes=[pltpu.VMEM((4 * (nq + 1) * 2 * BLK, LANES), BF16)] * 2,
        compiler_params=_params(("parallel",)),
    )(qkv, qkv, qkv, qkv, qkv, d_att, lse, delta, qkv, d_att, lse, delta)


def _merge_and_head(o_g, lse_g, p_gate, y_conv, x, tgt, modv, final_g, w_ao, w_o, ts):
    s_len, dm = x.shape
    gw = ATT_W + 2 * dm
    nt = s_len // ts
    gate_off = 3 * dm + 9 * ATT_W
    assert gate_off % gw == 0

    def body(o0, o1, o2, l0, l1, l2, pg_ref, yc_ref, x_ref, t_ref, mod_ref, fg_ref, wao_ref, wo_ref,
             loss_ref, dx_ref, dyc_ref, dpg_ref, da0, da1, da2, ls0, ls1, ls2, de0, de1, de2,
             gwo_ref, gwao_ref, gs_ref, slab):
        i = pl.program_id(0)

        @pl.when(i == 0)
        def _():
            loss_ref[...] = jnp.zeros_like(loss_ref)
            gwo_ref[...] = jnp.zeros_like(gwo_ref)
            gwao_ref[...] = jnp.zeros_like(gwao_ref)
            gs_ref[...] = jnp.zeros_like(gs_ref)

        os_, ls_ = [], []
        for dil, o_ref, l_ref in zip(DILATIONS, (o0, o1, o2), (l0, l1, l2)):
            os_.append(jnp.concatenate(_interleave_load(o_ref, slab, dil, ts), axis=1))
            ls_.append(jnp.concatenate(_interleave_load(l_ref, slab, dil, ts), axis=1))
        mx = jnp.maximum(jnp.maximum(ls_[0], ls_[1]), ls_[2])
        wts = [jnp.exp(l - mx) for l in ls_]
        wsum = wts[0] + wts[1] + wts[2]
        att = (wts[0] * os_[0] + wts[1] * os_[1] + wts[2] * os_[2]) / wsum
        lse_all = mx + jnp.log(wsum)

        z_att, g_conv, g_att = pg_ref[:, 0:ATT_W], pg_ref[:, ATT_W:ATT_W + dm], pg_ref[:, ATT_W + dm:gw]
        sgz = _sig(z_att)
        sz = z_att * sgz
        a_att = (att * sz).astype(BF16)
        y_att = _dot(a_att, wao_ref[...])
        y_conv = yc_ref[...]
        sgc, sga = _sig(g_conv), _sig(g_att)
        merged = (sgc * y_conv + sga * y_att).astype(BF16)
        mo = _dot(merged, wo_ref[...])
        gate = mod_ref[2:3, :]
        x2 = x_ref[...] + gate * mo
        r = lax.rsqrt(jnp.mean(x2 * x2, axis=-1, keepdims=True) + EPS)
        xr = x2 * r
        err = xr * fg_ref[...] - t_ref[...]
        loss_ref[...] += 0.5 * jnp.sum(jnp.mean(err * err, axis=-1, keepdims=True))
        dy = err * (1.0 / dm)
        gs_ref[0:1, :] += jnp.sum(dy * xr, axis=0, keepdims=True)
        dyg = dy * fg_ref[...]
        d_x2 = r * dyg - xr * (r * jnp.mean(dyg * xr, axis=-1, keepdims=True))
        dx_ref[...] = d_x2
        gs_ref[1:2, :] += jnp.sum(d_x2 * mo, axis=0, keepdims=True)
        d_mo = (d_x2 * gate).astype(BF16)
        d_mg = _dot_nt(d_mo, wo_ref[...])
        gwo_ref[...] += _dot_tn(merged, d_mo)
        dyc_ref[...] = (d_mg * sgc).astype(BF16)
        dpg_ref[:, ATT_W:ATT_W + dm] = (d_mg * y_conv * sgc * (1.0 - sgc)).astype(BF16)
        d_ya = (d_mg * sga).astype(BF16)
        dpg_ref[:, ATT_W + dm:gw] = (d_mg * y_att * sga * (1.0 - sga)).astype(BF16)
        gwao_ref[...] += _dot_tn(a_att, d_ya)
        d_aa = _dot_nt(d_ya, wao_ref[...])
        dpg_ref[:, 0:ATT_W] = (d_aa * att * _dsilu(z_att, sgz)).astype(BF16)
        d_att = d_aa * sz
        ri = lax.broadcasted_iota(jnp.int32, (ATT_W, ATT_W), 0) // 64
        ci = lax.broadcasted_iota(jnp.int32, (ATT_W, ATT_W), 1) // 64
        ones = jnp.where(ri == ci, 1.0, 0.0).astype(BF16)
        prod = d_att * att
        hi = prod.astype(BF16)
        lo = (prod - hi.astype(F32)).astype(BF16)
        delta = _dot(hi, ones) + _dot(lo, ones)
        for val, refs, dt in ((d_att, (da0, da1, da2), BF16), (lse_all, (ls0, ls1, ls2), F32),
                              (delta, (de0, de1, de2), F32)):
            vals = [val[:, s * LANES:(s + 1) * LANES] for s in range(4)]
            for dil, ref in zip(DILATIONS, refs):
                _deinterleave_store(vals, slab, ref, dil, ts, dt)

    def grp(dil):
        return pl.BlockSpec((dil, ts // dil, ATT_W), lambda i: (0, i, 0))

    tile = pl.BlockSpec((ts, dm), lambda i: (i, 0))
    gate_tile = pl.BlockSpec((ts, gw), lambda i: (i, 0))
    const = lambda shp: pl.BlockSpec(shp, lambda i: tuple(0 for _ in shp))
    grp_shape = lambda dt: [jax.ShapeDtypeStruct((dil, s_len // dil, ATT_W), dt) for dil in DILATIONS]
    return pl.pallas_call(
        body, name="merge_and_head", grid=(nt,),
        in_specs=[grp(d) for d in DILATIONS] * 2
        + [gate_tile, tile, tile, tile, const((8, dm)), const((1, dm)), const((ATT_W, dm)), const((dm, dm))],
        out_specs=[const((8, LANES)), tile, tile, pl.BlockSpec((ts, gw), lambda i: (i, gate_off // gw))]
        + [grp(d) for d in DILATIONS] * 3
        + [const((dm, dm)), const((ATT_W, dm)), const((8, dm))],
        out_shape=[jax.ShapeDtypeStruct((8, LANES), F32), jax.ShapeDtypeStruct((s_len, dm), F32),
                   jax.ShapeDtypeStruct((s_len, dm), BF16), jax.ShapeDtypeStruct((s_len, gate_off + gw), BF16)]
        + grp_shape(BF16) + grp_shape(F32) + grp_shape(F32)
        + [jax.ShapeDtypeStruct((dm, dm), F32), jax.ShapeDtypeStruct((ATT_W, dm), F32),
           jax.ShapeDtypeStruct((8, dm), F32)],
        scratch_shapes=[pltpu.VMEM((8, ts, LANES), F32)],
        compiler_params=_params(("arbitrary",)),
    )(*o_g, *lse_g, p_gate, y_conv, x, tgt, modv, final_g, w_ao, w_o)


def _qkv_grad_to_tokens(dp, dqkv_g, rope, ts):
    s_len, dm = rope.shape[1], (dp.shape[1] - 10 * ATT_W) // 5
    qw = 3 * ATT_W
    assert (3 * dm) % qw == 0

    def body(dp_in, g0, g1, g2, rope_ref, o_ref, slab):
        del dp_in
        w = pl.program_id(1)
        rc, ra, rb = rope_ref.at[0], rope_ref.at[1], rope_ref.at[2]

        def emit(roped):
            for gi, (dil, g_ref) in enumerate(zip(DILATIONS, (g0, g1, g2))):
                vals = _interleave_load(g_ref, slab, dil, ts)
                for s in range(4):
                    t = vals[s]
                    if roped:
                        t = t * rc[...] + pltpu.roll(t * ra[...], 8, 1) + pltpu.roll(t * rb[...], LANES - 8, 1)
                    col = gi * ATT_W + s * LANES
                    o_ref[:, col:col + LANES] = t.astype(BF16)

        pl.when(w < 2)(lambda: emit(True))
        pl.when(w == 2)(lambda: emit(False))

    return pl.pallas_call(
        body, name="qkv_grad_to_tokens", grid=(s_len // ts, 3),
        in_specs=[ANY] + [pl.BlockSpec((None, dil, ts // dil, ATT_W), lambda i, w: (w, 0, i, 0)) for dil in DILATIONS]
        + [pl.BlockSpec((3, ts, LANES), lambda i, w: (0, i, 0))],
        out_specs=pl.BlockSpec((ts, qw), lambda i, w: (i, 3 * dm // qw + w)),
        out_shape=jax.ShapeDtypeStruct(dp.shape, BF16),
        input_output_aliases={0: 0},
        scratch_shapes=[pltpu.VMEM((8, ts, LANES), F32)],
        compiler_params=_params(("parallel", "arbitrary")),
    )(dp, *dqkv_g, rope)


def _wide_col_tile(cols):
    for width in (5 * COL_TILE, 2 * COL_TILE):
        if cols % width == 0:
            return width
    return COL_TILE


def _input_grad(dp, w_in, x, dx_res, modv, norm_g, ts):
    s_len, dm = x.shape
    ct = next(w * COL_TILE for w in (4, 2, 1) if dp.shape[1] % (w * COL_TILE) == 0)
    nct = dp.shape[1] // ct

    def body(p_ref, w_ref, x_ref, dxr_ref, mod_ref, g_ref, gx_ref, gs_ref, acc):
        i, j = pl.program_id(0), pl.program_id(1)

        @pl.when((i == 0) & (j == 0))
        def _():
            gs_ref[...] = jnp.zeros_like(gs_ref)

        @pl.when(j == 0)
        def _():
            acc[...] = jnp.zeros_like(acc)

        acc[...] += _dot_nt(p_ref[...], w_ref[...])

        @pl.when(j == nct - 1)
        def _():
            d_h = acc[...]
            xv = x_ref[...]
            r = lax.rsqrt(jnp.mean(xv * xv, axis=-1, keepdims=True) + EPS)
            xr = xv * r
            gs_ref[0:1, :] += jnp.sum(d_h, axis=0, keepdims=True)
            gs_ref[1:2, :] += jnp.sum(d_h * (xr * g_ref[...]), axis=0, keepdims=True)
            d_n = d_h * (1.0 + mod_ref[1:2, :])
            gs_ref[2:3, :] += jnp.sum(d_n * xr, axis=0, keepdims=True)
            dxn = d_n * g_ref[...]
            gx_ref[...] = dxr_ref[...] + r * dxn - xr * (r * jnp.mean(dxn * xr, axis=-1, keepdims=True))

    tile = pl.BlockSpec((ts, dm), lambda i, j: (i, 0))
    return pl.pallas_call(
        body, name="input_grad", grid=(s_len // ts, nct),
        in_specs=[pl.BlockSpec((ts, ct), lambda i, j: (i, j)), pl.BlockSpec((dm, ct), lambda i, j: (0, j)), tile, tile,
                  pl.BlockSpec((8, dm), lambda i, j: (0, 0)), pl.BlockSpec((1, dm), lambda i, j: (0, 0))],
        out_specs=[tile, pl.BlockSpec((8, dm), lambda i, j: (0, 0))],
        out_shape=[jax.ShapeDtypeStruct((s_len, dm), F32), jax.ShapeDtypeStruct((8, dm), F32)],
        scratch_shapes=[pltpu.VMEM((ts, dm), F32)],
        compiler_params=_params(("arbitrary", "arbitrary")),
    )(dp, w_in, x, dx_res, modv, norm_g)


def _w_in_grad(h_t, dp, ts):
    dm, s_len = h_t.shape
    ct = _wide_col_tile(dp.shape[1])

    def body(h_ref, p_ref, o_ref):
        @pl.when(pl.program_id(1) == 0)
        def _():
            o_ref[...] = jnp.zeros_like(o_ref)

        o_ref[...] += _dot(h_ref[...], p_ref[...])

    return pl.pallas_call(
        body, name="w_in_grad", grid=(dp.shape[1] // ct, s_len // ts),
        in_specs=[pl.BlockSpec((dm, ts), lambda j, i: (0, i)), pl.BlockSpec((ts, ct), lambda j, i: (i, j))],
        out_specs=pl.BlockSpec((dm, ct), lambda j, i: (0, j)),
        out_shape=jax.ShapeDtypeStruct((dm, dp.shape[1]), F32),
        compiler_params=_params(("arbitrary", "arbitrary")),
    )(h_t, dp)


def _rope_lane_table():
    l64 = jnp.arange(LANES) % 64
    half = ROT_DIM // 2
    inv_freq = ROPE_THETA ** (-(jnp.arange(half, dtype=F32) * 2.0 / ROT_DIM))
    rot = l64 < ROT_DIM
    rows = [jnp.where(rot, inv_freq[l64 % half], 0.0), (l64 < half).astype(F32),
            ((l64 >= half) & rot).astype(F32), rot.astype(F32)]
    return jnp.concatenate([jnp.stack(rows), jnp.zeros((4, LANES), F32)], axis=0)


def _tile_sizes(s_len):
    ts_big = min(1024, s_len // 2)
    ts_mid = 256
    ts_head = 256
    qt = [min(1024, s_len // dil) for dil in DILATIONS]
    return ts_big, ts_mid, ts_head, qt


def kernel(x, c, positions, norm_g, w_ada, b_ada, w_in, conv_w, conv_b, conv_ln_g, conv_ln_b, w_conv_out, w_att_out, w_o, final_g, loss_target, m_norm_g, m_w_ada, m_b_ada, m_w_in, m_conv_w, m_conv_b, m_conv_ln_g, m_conv_ln_b, m_w_conv_out, m_w_att_out, m_w_o, m_final_g, v_norm_g, v_w_ada, v_b_ada, v_w_in, v_conv_w, v_conv_b, v_conv_ln_g, v_conv_ln_b, v_w_conv_out, v_w_att_out, v_w_o, v_final_g):
    s_len, dm = x.shape[1], x.shape[2]
    ts_big, ts_mid, ts_head, qt = _tile_sizes(s_len)
    xi, yi, cidx = _my_place()
    chip = 2 * xi + yi
    batch = 4 * xi + 2 * yi + cidx
    x2d, tgt = x[0], loss_target[0]
    pos = positions.reshape(s_len, 1)
    wa_l, wi_l, cw_l = w_ada[0], w_in[0], conv_w[0]
    wco_l, wao_l, wo_l = w_conv_out[0], w_att_out[0], w_o[0]
    ada_w = wa_l.shape[1]
    cw_cols = cw_l.shape[1]

    kinds = ("col", "row", "col", "row")
    w_bufs = _cast_weights([wi_l, wco_l, wao_l, wo_l], kinds)

    cw_pad = jnp.pad(cw_l, ((0, HALO - CONV_K), (0, 0)))
    small_in = jnp.concatenate([jnp.broadcast_to(c, (8, dm)), cw_pad.reshape(8, dm)], axis=0)
    small = _allgather_small(small_in).reshape(8, 16, dm)
    c_all = small[:, 0, :]
    conv_w_full = jnp.concatenate(
        [small[2 * p, 8:16, :].reshape(HALO, cw_cols) for p in range(4)], axis=1)
    b_l = lax.dynamic_slice(b_ada, (0, chip * ada_w), (1, ada_w))
    mod_parts = _allgather_small(_mod_part(c_all, wa_l, b_l)).reshape(8, 8, ada_w)
    mod_rows = lax.dynamic_index_in_dim(mod_parts, batch, axis=1, keepdims=False)
    mod = jnp.concatenate([mod_rows[2 * p] for p in range(4)], axis=0).reshape(3, dm)
    modv = jnp.concatenate([mod, jnp.zeros((5, dm), F32)], axis=0)

    sems_a, sems_b, w_bufs, token = _gather_weights_start(w_bufs, kinds, modv)
    rope = _rope_coefficients(pos, _rope_lane_table() + token, ts_big)
    h_b, h_t = _norm_modulate(x2d, modv + token[0, 0], norm_g, ts_big)
    (w_in_b,) = _gather_weights_wait(w_bufs, kinds, sems_a, 0, (rope, h_b), "gather_weights_wait_w_in")
    w_in_b = _pass_to_sibling(w_in_b, kinds[0])

    n_conv, n_gate = 3 * dm // COL_TILE, (ATT_W + 2 * dm) // COL_TILE
    ts_proj = min(4 * ts_big, s_len)
    p_conv = _plain_projection(h_b, w_in_b, 0, n_conv, ts_proj, "conv_projection")
    p_gate = _plain_projection(h_b, w_in_b, n_conv + 9, n_gate, ts_proj, "gate_projection")
    qkv_g = [_qkv_projection(h_b, rope, w_in_b, gi, min(2 * ts_big, s_len)) for gi in range(3)]
    w_co_b, w_ao_b, w_o_b = _gather_weights_wait(w_bufs, kinds, sems_b, 1, (qkv_g[2], p_conv, p_gate),
                                                  "gather_weights_wait_rest")
    y_conv, u0, u1 = _conv_forward(p_conv, conv_w_full, conv_b, conv_ln_g, conv_ln_b, w_co_b, ts_mid)
    qkv_flat = [q.reshape(3, s_len, ATT_W) for q in qkv_g]
    o_g, lse_g = [], []
    for gi, dil in enumerate(DILATIONS):
        o, l = _attention_forward(qkv_flat[gi], s_len // dil, qt[gi], "attention_forward_%d" % dil)
        o_g.append(o.reshape(dil, s_len // dil, ATT_W))
        lse_g.append(l.reshape(dil, s_len // dil, ATT_W))

    (loss_p, dx_res, dyc, dp, da0, da1, da2, ls0, ls1, ls2, de0, de1, de2,
     g_wo, g_wao, head_sums) = _merge_and_head(o_g, lse_g, p_gate, y_conv, x2d, tgt, modv,
                                               final_g.reshape(1, dm), w_ao_b, w_o_b, ts_head)

    dp, g_wco, conv_sums, g_cw = _conv_backward(dp, dyc, p_conv, u0, u1, conv_w_full, conv_ln_g, conv_ln_b,
                                                w_co_b, ts_mid)
    dqkv_g = []
    for gi, (dil, da, ls, de) in enumerate(zip(DILATIONS, (da0, da1, da2), (ls0, ls1, ls2), (de0, de1, de2))):
        flat = lambda a: a.reshape(s_len, ATT_W)
        dqkv = _attention_backward(qkv_flat[gi], flat(da), flat(ls), flat(de), s_len // dil, qt[gi],
                                   "attention_backward_%d" % dil)
        dqkv_g.append(dqkv.reshape(3, dil, s_len // dil, ATT_W))
    dp = _qkv_grad_to_tokens(dp, dqkv_g, rope, 2 * ts_mid)
    g_win = _w_in_grad(h_t, dp, min(2 * ts_big, s_len))

    grads = [g_win, g_wco, g_wao, g_wo]
    full_shapes = [g.shape for g in grads]
    c_arr = jnp.reshape(cidx, (1,)).astype(jnp.int32)
    recv_halves = _reduce_pair_exchange(grads, kinds)
    halves = [_reduce_pair_sum(g, pa, kind, c_arr, "reduce_pair_sum_%d" % k)
              for k, (g, pa, kind) in enumerate(zip(grads, recv_halves, kinds))]
    send_sems, recv_sems, halves, lands, token = _reduce_to_owner_start(halves, kinds, full_shapes)
    grad_x, in_sums = _input_grad(dp, w_in_b, x2d, dx_res, modv + token[0, 0], norm_g, ts_big)
    halves, recvd = _reduce_to_owner_wait(send_sems, recv_sems, halves, lands, in_sums, kinds)
    gr_win, gr_wco, gr_wao, gr_wo = _reduce_finish(halves, recvd, kinds, full_shapes)

    rows = [in_sums[2:3], conv_sums[2:3], conv_sums[0:1], conv_sums[1:2], head_sums[0:1],
            in_sums[0:1], in_sums[1:2], head_sums[1:2], g_cw, jnp.pad(loss_p, ((0, 0), (0, dm - LANES)))]
    part = jnp.concatenate(rows, axis=0)
    buf = lax.dynamic_update_slice(jnp.zeros((8 * 48, dm), F32), part, (batch * 48, 0))
    row_sems_s, row_sems_r, buf, row_token = _row_gather_start(buf, 48, gr_win)
    upd = {
        "w_in": _adamw(wi_l, gr_win, m_w_in[0], v_w_in[0], "adamw_w_in", (row_token,)),
        "w_co": _adamw(wco_l, gr_wco, m_w_conv_out[0], v_w_conv_out[0], "adamw_w_conv_out", (row_token,)),
        "w_ao": _adamw(wao_l, gr_wao, m_w_att_out[0], v_w_att_out[0], "adamw_w_att_out", (row_token,)),
        "w_o": _adamw(wo_l, gr_wo, m_w_o[0], v_w_o[0], "adamw_w_o", (row_token,)),
    }
    gathered = _row_gather_wait(row_sems_s, row_sems_r, buf, 48, [upd[k][0] for k in ("w_in", "w_co", "w_ao", "w_o")])
    tot = _sum_devices(gathered, 48)
    loss = tot[8 + HALO, 0]
    dmod_all = gathered.reshape(8, 48, dm)[:, 5:8, :].reshape(8, 3 * dm)
    dmod_l = lax.dynamic_slice(dmod_all, (0, chip * ada_w), (8, ada_w))
    gr_wada = _grad_w_ada(c_all.T, dmod_l)
    gr_cw = lax.dynamic_slice(tot[8:8 + HALO], (0, chip * cw_cols), (HALO, cw_cols))

    pad_cw = lambda a: jnp.pad(a[0], ((0, HALO - CONV_K), (0, 0)))
    row = lambda a: a.reshape(1, dm)
    small_upd = _adamw_small(
        [norm_g, conv_b, conv_ln_g, conv_ln_b, row(final_g), b_ada], tot[0:8],
        [m_norm_g, m_conv_b, m_conv_ln_g, m_conv_ln_b, row(m_final_g), m_b_ada],
        [v_norm_g, v_conv_b, v_conv_ln_g, v_conv_ln_b, row(v_final_g), v_b_ada])
    upd["w_ada"] = _adamw(wa_l, gr_wada, m_w_ada[0], v_w_ada[0], "adamw_w_ada")
    upd["conv_w"] = _adamw(cw_pad, gr_cw, pad_cw(m_conv_w), pad_cw(v_conv_w), "adamw_conv_w")

    def family(which):
        if which is None:
            sm = small_upd[0:6]
            big = {"w_ada": gr_wada, "w_in": gr_win, "conv_w": gr_cw, "w_co": gr_wco, "w_ao": gr_wao, "w_o": gr_wo}
        else:
            sm = small_upd[6 * (which + 1):6 * (which + 2)]
            big = {k: upd[k][which] for k in ("w_ada", "w_in", "conv_w", "w_co", "w_ao", "w_o")}
        return [sm[0], big["w_ada"][None], sm[5], big["w_in"][None],
                big["conv_w"][None, :CONV_K], sm[1], sm[2], sm[3], big["w_co"][None],
                big["w_ao"][None], big["w_o"][None], sm[4].reshape(dm)]

    return (loss, grad_x[None], *family(None), *family(0), *family(1), *family(2))
```

```python
import jax
import jax.numpy as jnp
from jax import lax
from jax.experimental import pallas as pl
from jax.experimental.pallas import tpu as pltpu

F32 = jnp.float32
BF16 = jnp.bfloat16
MESH = pl.DeviceIdType.MESH
ANY = pl.BlockSpec(memory_space=pl.ANY)
VM = pl.BlockSpec(memory_space=pltpu.VMEM)

EPS = 1e-6
NEG = -1e30
ATT_W = 512
DILATIONS = (1, 4, 16)
BLK = 128
CONV_K = 31
HALO = 32
CONV_ROWS_FWD = 32
CONV_ROWS_BWD = 16
ROT_DIM = 16
ROPE_THETA = 500000.0
COL_TILE = 512
LANES = 128
VMEM_LIMIT = 56 * 1024 * 1024

ADAM_LR, ADAM_B1, ADAM_B2, ADAM_EPS, ADAM_WD, ADAM_STEP = 0.001, 0.9, 0.999, 1e-08, 0.01, 10


def _params(sem=None, vmem=VMEM_LIMIT):
    return pltpu.CompilerParams(dimension_semantics=sem, vmem_limit_bytes=vmem)


def _dot(a, b):
    return jnp.dot(a, b, preferred_element_type=F32)


def _dot_nt(a, b):
    return lax.dot_general(a, b, (((1,), (1,)), ((), ())), preferred_element_type=F32)


def _dot_tn(a, b):
    return lax.dot_general(a, b, (((0,), (0,)), ((), ())), preferred_element_type=F32)


def _sig(x):
    return jax.nn.sigmoid(x)


def _dsilu(x, s):
    return s * (1.0 + x * (1.0 - s))


def _my_place():
    return lax.axis_index("x"), lax.axis_index("y"), lax.axis_index("c")


def _allgather_small(x_shard):
    m_per, n = x_shard.shape

    def body(x_ref, out_ref, send_sems, recv_sems, local_sem):
        x, y, c = _my_place()
        me, sibling = (x, y, c), (x, y, 1 - c)
        chips = [(1 - x, y), (x, 1 - y), (1 - x, 1 - y)]

        def rows(px, py, pc):
            return out_ref.at[pl.ds((4 * px + 2 * py + pc) * m_per, m_per), :]

        def copy(k, block, to, src=None):
            return pltpu.make_async_remote_copy(
                src_ref=rows(*block) if src is None else src, dst_ref=rows(*block),
                send_sem=send_sems.at[k], recv_sem=recv_sems.at[k],
                device_id=to, device_id_type=MESH)

        mine = pltpu.make_async_copy(x_ref, rows(*me), local_sem)
        mine.start()
        first = [copy(0, me, sibling, src=x_ref)]
        first += [copy(1 + j, me, (*chip, c), src=x_ref) for j, chip in enumerate(chips)]
        for cp in first:
            cp.start()
        passed = [copy(4 + j, (*chip, c), sibling) for j, chip in enumerate(chips)]
        for j, chip in enumerate(chips):
            copy(1 + j, (*chip, c), me).wait_recv()
            passed[j].start()
        copy(0, sibling, me).wait_recv()
        for j, chip in enumerate(chips):
            copy(4 + j, (*chip, 1 - c), me).wait_recv()
        for cp in first + passed:
            cp.wait_send()
        mine.wait()

    return pl.pallas_call(
        body, name="allgather_small",
        out_shape=jax.ShapeDtypeStruct((8 * m_per, n), x_shard.dtype),
        in_specs=[VM], out_specs=VM,
        scratch_shapes=[pltpu.SemaphoreType.DMA((7,)), pltpu.SemaphoreType.DMA((7,)),
                        pltpu.SemaphoreType.DMA],
    )(x_shard)


def _shard_window(ref, kind, p, n_shards=4):
    r, c = ref.shape
    if kind == "col":
        w = c // n_shards
        return ref.at[:, pl.ds(p * w, w)]
    w = r // n_shards
    return ref.at[pl.ds(p * w, w), :]


def _half_window(ref, kind, hc):
    r, c = ref.shape
    if kind == "col":
        return ref.at[pl.ds(hc * (r // 2), r // 2), :]
    return ref.at[:, pl.ds(hc * (c // 2), c // 2)]


def _landed(ref, kind, chip, hc):
    return _half_window(_shard_window(ref, kind, 2 * chip[0] + chip[1]), kind, hc)


def _cast_weights(shards, kinds):
    n = len(shards)
    full_shapes = [(s.shape[0], 4 * s.shape[1]) if kind == "col" else (4 * s.shape[0], s.shape[1])
                   for s, kind in zip(shards, kinds)]

    def body(*refs):
        w_refs, out_refs, bf_refs, sems = refs[:n], refs[n:2 * n], refs[2 * n:3 * n], refs[3 * n]
        x, y, _ = _my_place()
        cps = []
        for k in range(n):
            bf_refs[k][...] = w_refs[k][...].astype(BF16)
            cp = pltpu.make_async_copy(bf_refs[k], _shard_window(out_refs[k], kinds[k], 2 * x + y), sems.at[k])
            cp.start()
            cps.append(cp)
        for cp in cps:
            cp.wait()

    return pl.pallas_call(
        body, name="cast_weights",
        out_shape=[jax.ShapeDtypeStruct(s, BF16) for s in full_shapes],
        in_specs=[VM] * n, out_specs=[ANY] * n,
        scratch_shapes=[pltpu.VMEM(s.shape, BF16) for s in shards] + [pltpu.SemaphoreType.DMA((n,))],
        compiler_params=_params(),
    )(*shards)


def _gather_copies(refs, kinds, sems_a, sems_b):
    x, y, c = _my_place()
    chips = [(1 - x, y), (x, 1 - y), (1 - x, 1 - y)]
    set_a, set_b = [], []
    for j, chip in enumerate(chips):
        for k in range(len(refs)):
            if refs[k] is None:
                continue
            for flip in ((0,) if k == 0 else (0, 1)):
                win = _landed(refs[k], kinds[k], (x, y), c)
                sems, idx = (sems_a, j) if k == 0 else (sems_b, ((k - 1) * 3 + j) * 2 + flip)
                mine = _landed(refs[k], kinds[k], chip, (c + flip) % 2)
                (set_a if k == 0 else set_b).append((
                    pltpu.make_async_remote_copy(
                        src_ref=win, dst_ref=win, send_sem=sems[0].at[idx], recv_sem=sems[1].at[idx],
                        device_id=(*chip, (c + flip) % 2), device_id_type=MESH),
                    pltpu.make_async_remote_copy(
                        src_ref=mine, dst_ref=mine, send_sem=sems[0].at[idx], recv_sem=sems[1].at[idx],
                        device_id=(*chip, (c + flip) % 2), device_id_type=MESH)))
    return set_a, set_b


_SPLIT = dict(has_side_effects=pltpu.SideEffectType.DATAFLOW_SIDE_EFFECTING)


def _gather_weights_start(fulls, kinds, after):
    n = len(fulls)
    hbm = pl.BlockSpec(memory_space=pltpu.HBM)
    sem = pl.BlockSpec(memory_space=pltpu.SEMAPHORE)
    nb = (n - 1) * 6

    def body(*refs):
        in_refs = refs[:n]
        sa, ra, sb, rb = refs[n + 1:n + 5]
        token = refs[-1]
        set_a, set_b = _gather_copies(in_refs, kinds, (sa, ra), (sb, rb))
        for out_cp, _ in set_a + set_b:
            out_cp.start()
        token[...] = jnp.zeros_like(token)

    out = pl.pallas_call(
        body, name="gather_weights_start",
        out_shape=[pltpu.SemaphoreType.DMA((3,)), pltpu.SemaphoreType.DMA((3,)),
                   pltpu.SemaphoreType.DMA((nb,)), pltpu.SemaphoreType.DMA((nb,))]
        + [pltpu.HBM(f.shape, f.dtype) for f in fulls] + [jax.ShapeDtypeStruct((8, LANES), F32)],
        in_specs=[hbm] * n + [ANY], out_specs=[sem] * 4 + [hbm] * n + [VM],
        input_output_aliases={k: 4 + k for k in range(n)},
        compiler_params=pltpu.CompilerParams(**_SPLIT),
    )(*[pltpu.with_memory_space_constraint(f, pltpu.HBM) for f in fulls], after)
    return (out[0], out[1]), (out[2], out[3]), out[4:4 + n], out[-1]


def _gather_weights_wait(fulls, kinds, sems, which, after, name):
    n = len(fulls)
    hbm = pl.BlockSpec(memory_space=pltpu.HBM)
    sem = pl.BlockSpec(memory_space=pltpu.SEMAPHORE)
    keep = [0] if which == 0 else list(range(1, n))

    def body(*refs):
        m = len(keep)
        full_refs = [None] * n
        for pos_, k in enumerate(keep):
            full_refs[k] = refs[pos_]
        s_ref, r_ref = refs[m:m + 2]
        if which == 0:
            sets = _gather_copies([full_refs[0]], kinds[:1], (s_ref, r_ref), None)[0]
        else:
            sets = _gather_copies([None] + [full_refs[k] for k in keep], kinds, None, (s_ref, r_ref))[1]
        for out_cp, in_cp in sets:
            out_cp.wait_send()
            in_cp.wait_recv()

    out = pl.pallas_call(
        body, name=name,
        out_shape=[pltpu.HBM(fulls[k].shape, fulls[k].dtype) for k in keep],
        in_specs=[hbm] * len(keep) + [sem, sem] + [ANY] * len(after), out_specs=[hbm] * len(keep),
        input_output_aliases={i: i for i in range(len(keep))},
        compiler_params=pltpu.CompilerParams(**_SPLIT),
    )(*[fulls[k] for k in keep], sems[0], sems[1], *after)
    return list(out)


def _row_gather_copies(buf, m_per, send_sems, recv_sems):
    x, y, c = _my_place()
    pairs = []
    for idx in range(1, 8):
        dx, dy, dc = idx // 4, (idx // 2) % 2, idx % 2
        peer = ((x + dx) % 2, (y + dy) % 2, (c + dc) % 2)
        mine = buf.at[pl.ds((4 * x + 2 * y + c) * m_per, m_per), :]
        theirs = buf.at[pl.ds((4 * peer[0] + 2 * peer[1] + peer[2]) * m_per, m_per), :]
        pairs.append(tuple(
            pltpu.make_async_remote_copy(src_ref=w, dst_ref=w, send_sem=send_sems.at[idx - 1],
                                         recv_sem=recv_sems.at[idx - 1], device_id=peer, device_id_type=MESH)
            for w in (mine, theirs)))
    return pairs


def _row_gather_start(buf, m_per, after):
    hbm = pl.BlockSpec(memory_space=pltpu.HBM)
    sem = pl.BlockSpec(memory_space=pltpu.SEMAPHORE)

    def body(buf_ref, after_ref, send_sems, recv_sems, out_ref, token):
        del after_ref, out_ref
        for out_cp, _ in _row_gather_copies(buf_ref, m_per, send_sems, recv_sems):
            out_cp.start()
        token[...] = jnp.zeros_like(token)

    return pl.pallas_call(
        body, name="row_gather_start",
        out_shape=[pltpu.SemaphoreType.DMA((7,)), pltpu.SemaphoreType.DMA((7,)), pltpu.HBM(buf.shape, buf.dtype),
                   jax.ShapeDtypeStruct((8, LANES), F32)],
        in_specs=[hbm, ANY], out_specs=[sem, sem, hbm, VM], input_output_aliases={0: 2},
        compiler_params=pltpu.CompilerParams(**_SPLIT),
    )(pltpu.with_memory_space_constraint(buf, pltpu.HBM), after)


def _row_gather_wait(send_sems, recv_sems, buf, m_per, after):
    hbm = pl.BlockSpec(memory_space=pltpu.HBM)
    sem = pl.BlockSpec(memory_space=pltpu.SEMAPHORE)

    def body(buf_ref, send_s, recv_s, *rest):
        for out_cp, in_cp in _row_gather_copies(buf_ref, m_per, send_s, recv_s):
            out_cp.wait_send()
            in_cp.wait_recv()

    return pl.pallas_call(
        body, name="row_gather_wait", out_shape=pltpu.HBM(buf.shape, buf.dtype),
        in_specs=[hbm, sem, sem] + [ANY] * len(after), out_specs=hbm, input_output_aliases={0: 0},
        compiler_params=pltpu.CompilerParams(**_SPLIT),
    )(buf, send_sems, recv_sems, *after)


def _pass_to_sibling(w_full, kind):
    def body(w_in_ref, w_ref, send_sems, recv_sems):
        del w_in_ref
        x, y, c = _my_place()
        chips = [(1 - x, y), (x, 1 - y), (1 - x, 1 - y)]
        cps = []
        for j, chip in enumerate(chips):
            win = _landed(w_ref, kind, chip, c)
            cp = pltpu.make_async_remote_copy(
                src_ref=win, dst_ref=win, send_sem=send_sems.at[j], recv_sem=recv_sems.at[j],
                device_id=(x, y, 1 - c), device_id_type=MESH)
            cp.start()
            cps.append(cp)
        for j, chip in enumerate(chips):
            theirs = _landed(w_ref, kind, chip, 1 - c)
            pltpu.make_async_remote_copy(
                src_ref=theirs, dst_ref=theirs, send_sem=send_sems.at[j], recv_sem=recv_sems.at[j],
                device_id=(x, y, 1 - c), device_id_type=MESH).wait_recv()
        for cp in cps:
            cp.wait_send()

    return pl.pallas_call(
        body, name="pass_to_sibling",
        out_shape=jax.ShapeDtypeStruct(w_full.shape, w_full.dtype),
        in_specs=[ANY], out_specs=ANY, input_output_aliases={0: 0},
        scratch_shapes=[pltpu.SemaphoreType.DMA((3,)), pltpu.SemaphoreType.DMA((3,))],
    )(w_full)


def _reduce_pair_exchange(grads, kinds):
    n = len(grads)
    half_shapes = [(g.shape[0] // 2, g.shape[1]) if kind == "col" else (g.shape[0], g.shape[1] // 2)
                   for g, kind in zip(grads, kinds)]

    def body(*refs):
        g_refs, pa_refs = refs[:n], refs[n:2 * n]
        send_sems, recv_sems = refs[2 * n:]
        x, y, c = _my_place()
        cps = []
        for k in range(n):
            cp = pltpu.make_async_remote_copy(
                src_ref=_half_window(g_refs[k], kinds[k], 1 - c), dst_ref=pa_refs[k],
                send_sem=send_sems.at[k], recv_sem=recv_sems.at[k],
                device_id=(x, y, 1 - c), device_id_type=MESH)
            cp.start()
            cps.append(cp)
        for cp in cps:
            cp.wait()

    return pl.pallas_call(
        body, name="reduce_pair_exchange",
        out_shape=[jax.ShapeDtypeStruct(s, F32) for s in half_shapes],
        in_specs=[ANY] * n, out_specs=[ANY] * n,
        scratch_shapes=[pltpu.SemaphoreType.DMA((n,)), pltpu.SemaphoreType.DMA((n,))],
    )(*grads)


def _row_tile(rows, cols, itemsize=4, target=2 * 1024 * 1024, mult=16):
    t = rows
    while t % 2 == 0 and t // 2 >= mult and (t // 2) % mult == 0 and t * cols * itemsize > target:
        t //= 2
    return t


def _reduce_pair_sum(g, pa, kind, c_arr, name):
    hr, hc_ = pa.shape
    tr = _row_tile(hr, hc_)
    nb = hr // tr

    def body(c_ref, g_ref, pa_ref, o_ref):
        o_ref[...] = (g_ref[...] + pa_ref[...]).astype(BF16)

    if kind == "col":
        g_map = lambda i, c_ref: (c_ref[0] * nb + i, 0)
    else:
        g_map = lambda i, c_ref: (i, c_ref[0])
    return pl.pallas_call(
        body, name=name,
        grid_spec=pltpu.PrefetchScalarGridSpec(
            num_scalar_prefetch=1, grid=(nb,),
            in_specs=[pl.BlockSpec((tr, hc_), g_map), pl.BlockSpec((tr, hc_), lambda i, c_ref: (i, 0))],
            out_specs=pl.BlockSpec((tr, hc_), lambda i, c_ref: (i, 0))),
        out_shape=jax.ShapeDtypeStruct((hr, hc_), BF16),
        compiler_params=_params(("parallel",)),
    )(c_arr, g, pa)


def _half_shard_shape(full_shape, kind):
    r, c = full_shape
    return (r // 2, c // 4) if kind == "col" else (r // 4, c // 2)


def _to_owner_copies(h_refs, land_refs, send_sems, recv_sems, kinds):
    n = len(h_refs)
    x, y, c = _my_place()
    chips = [(1 - x, y), (x, 1 - y), (1 - x, 1 - y)]
    cps = []
    for j, chip in enumerate(chips):
        pj = 2 * chip[0] + chip[1]
        for k in range(n):
            cps.append(pltpu.make_async_remote_copy(
                src_ref=_shard_window(h_refs[k], kinds[k], pj), dst_ref=land_refs[k].at[j],
                send_sem=send_sems.at[j * n + k], recv_sem=recv_sems.at[j * n + k],
                device_id=(*chip, c), device_id_type=MESH))
    return cps


def _reduce_to_owner_start(halves, kinds, full_shapes):
    n = len(halves)
    hs = [_half_shard_shape(fs, kind) for fs, kind in zip(full_shapes, kinds)]
    hbm = pl.BlockSpec(memory_space=pltpu.HBM)
    sem = pl.BlockSpec(memory_space=pltpu.SEMAPHORE)

    def body(*refs):
        h_refs, land_refs = refs[:n], refs[n:2 * n]
        send_sems, recv_sems = refs[2 * n:2 * n + 2]
        token = refs[-1]
        for cp in _to_owner_copies(h_refs, land_refs, send_sems, recv_sems, kinds):
            cp.start()
        token[...] = jnp.zeros_like(token)

    lands = [pltpu.with_memory_space_constraint(lax.empty((3,) + s, BF16), pltpu.HBM) for s in hs]
    out = pl.pallas_call(
        body, name="reduce_to_owner_start",
        out_shape=[pltpu.SemaphoreType.DMA((3 * n,)), pltpu.SemaphoreType.DMA((3 * n,))]
        + [pltpu.HBM(h.shape, h.dtype) for h in halves] + [pltpu.HBM((3,) + s, BF16) for s in hs]
        + [jax.ShapeDtypeStruct((8, LANES), F32)],
        in_specs=[hbm] * (2 * n), out_specs=[sem, sem] + [hbm] * (2 * n) + [VM],
        input_output_aliases={k: 2 + k for k in range(2 * n)},
        compiler_params=pltpu.CompilerParams(has_side_effects=pltpu.SideEffectType.DATAFLOW_SIDE_EFFECTING),
    )(*[pltpu.with_memory_space_constraint(h, pltpu.HBM) for h in halves], *lands)
    return out[0], out[1], out[2:2 + n], out[2 + n:2 + 2 * n], out[-1]


def _reduce_to_owner_wait(send_sems, recv_sems, halves, lands, after, kinds):
    n = len(halves)
    hbm = pl.BlockSpec(memory_space=pltpu.HBM)
    sem = pl.BlockSpec(memory_space=pltpu.SEMAPHORE)

    def body(*refs):
        h_refs, land_refs = refs[:n], refs[n:2 * n]
        send_s, recv_s = refs[2 * n:2 * n + 2]
        for cp in _to_owner_copies(h_refs, land_refs, send_s, recv_s, kinds):
            cp.wait_send()
            cp.wait_recv()

    out = pl.pallas_call(
        body, name="reduce_to_owner_wait",
        out_shape=[pltpu.HBM(h.shape, h.dtype) for h in halves] + [pltpu.HBM(l.shape, l.dtype) for l in lands],
        in_specs=[hbm] * (2 * n) + [sem, sem, ANY], out_specs=[hbm] * (2 * n),
        input_output_aliases={k: k for k in range(2 * n)},
        compiler_params=pltpu.CompilerParams(has_side_effects=pltpu.SideEffectType.DATAFLOW_SIDE_EFFECTING),
    )(*halves, *lands, send_sems, recv_sems, after)
    return out[:n], out[n:]


def _reduce_finish(halves, recvd, kinds, full_shapes):
    n = len(halves)
    hs = [_half_shard_shape(fs, kind) for fs, kind in zip(full_shapes, kinds)]
    shard_shapes = [(fs[0], fs[1] // 4) if kind == "col" else (fs[0] // 4, fs[1])
                    for fs, kind in zip(full_shapes, kinds)]

    def body(*refs):
        h_refs, rc_refs, gs_refs = refs[:n], refs[n:2 * n], refs[2 * n:3 * n]
        own_refs, gh_refs = refs[3 * n:4 * n], refs[4 * n:5 * n]
        in_sems, loc_sems, send_sems, recv_sems = refs[5 * n:]
        x, y, c = _my_place()
        p = 2 * x + y
        loads = []
        for k in range(n):
            cp = pltpu.make_async_copy(_shard_window(h_refs[k], kinds[k], p), own_refs[k], in_sems.at[k])
            cp.start()
            loads.append(cp)
        outs = []
        for k in range(n):
            loads[k].wait()
            gh_refs[k][...] = (own_refs[k][...].astype(F32) + rc_refs[k][0].astype(F32)
                               + rc_refs[k][1].astype(F32) + rc_refs[k][2].astype(F32))
            dst = _half_window(gs_refs[k], kinds[k], c)
            lc = pltpu.make_async_copy(gh_refs[k], dst, loc_sems.at[k])
            lc.start()
            rc = pltpu.make_async_remote_copy(
                src_ref=gh_refs[k], dst_ref=dst, send_sem=send_sems.at[k], recv_sem=recv_sems.at[k],
                device_id=(x, y, 1 - c), device_id_type=MESH)
            rc.start()
            outs.append((lc, rc))
        for k, (lc, rc) in enumerate(outs):
            lc.wait()
            rc.wait_send()
            pltpu.make_async_remote_copy(
                src_ref=gh_refs[k], dst_ref=_half_window(gs_refs[k], kinds[k], 1 - c),
                send_sem=send_sems.at[k], recv_sem=recv_sems.at[k],
                device_id=(x, y, 1 - c), device_id_type=MESH).wait_recv()

    return pl.pallas_call(
        body, name="reduce_finish",
        out_shape=[jax.ShapeDtypeStruct(s, F32) for s in shard_shapes],
        in_specs=[ANY] * n + [VM] * n, out_specs=[ANY] * n,
        scratch_shapes=[pltpu.VMEM(s, BF16) for s in hs] + [pltpu.VMEM(s, F32) for s in hs]
        + [pltpu.SemaphoreType.DMA((n,)) for _ in range(4)],
        compiler_params=_params(),
    )(*halves, *recvd)


def _mod_part(c_all, w_ada_l, b_l):
    def body(c_ref, w_ref, b_ref, o_ref):
        o_ref[...] = _dot(c_ref[...].astype(BF16), w_ref[...].astype(BF16)) + b_ref[...]

    return pl.pallas_call(
        body, name="mod_part", out_shape=jax.ShapeDtypeStruct((8, w_ada_l.shape[1]), F32),
        in_specs=[VM, VM, VM], out_specs=VM, compiler_params=_params(),
    )(c_all, w_ada_l, b_l)


def _sum_devices(parts, m):
    def body(p_ref, o_ref):
        acc = p_ref[0:m, :]
        for d in range(1, 8):
            acc = acc + p_ref[d * m:(d + 1) * m, :]
        o_ref[...] = acc

    return pl.pallas_call(
        body, name="sum_devices", out_shape=jax.ShapeDtypeStruct((m, parts.shape[1]), F32),
        in_specs=[VM], out_specs=VM, compiler_params=_params(),
    )(parts)


def _grad_w_ada(c_all_t, dmod_l):
    d, w = c_all_t.shape[0], dmod_l.shape[1]

    def body(ct_ref, dm_ref, o_ref):
        acc = ct_ref[:, 0:1] * dm_ref[0:1, :]
        for b in range(1, 8):
            acc = acc + ct_ref[:, b:b + 1] * dm_ref[b:b + 1, :]
        o_ref[...] = acc

    return pl.pallas_call(
        body, name="grad_w_ada", out_shape=jax.ShapeDtypeStruct((d, w), F32),
        in_specs=[VM, VM], out_specs=VM, compiler_params=_params(),
    )(c_all_t, dmod_l)


def _adamw_math(w, g, m, v):
    nm = ADAM_B1 * m + (1.0 - ADAM_B1) * g
    nv = ADAM_B2 * v + (1.0 - ADAM_B2) * (g * g)
    m_hat = nm / (1.0 - ADAM_B1 ** ADAM_STEP)
    v_hat = nv / (1.0 - ADAM_B2 ** ADAM_STEP)
    return -ADAM_LR * (m_hat / (jnp.sqrt(v_hat) + ADAM_EPS) + ADAM_WD * w), nm, nv


def _adamw_small(ws, g8, ms, vs):
    shapes = [jax.ShapeDtypeStruct(w.shape, F32) for w in ws]

    def body(*refs):
        w_refs, g_ref, m_refs, v_refs, outs = refs[0:6], refs[6], refs[7:13], refs[13:19], refs[19:]
        for idx in range(6):
            if idx < 5:
                g = g_ref[idx:idx + 1, :]
            else:
                g = jnp.concatenate([g_ref[5:6, :], g_ref[6:7, :], g_ref[7:8, :]], axis=1)
            res = (g,) + _adamw_math(w_refs[idx][...], g, m_refs[idx][...], v_refs[idx][...])
            for fam in range(4):
                outs[6 * fam + idx][...] = res[fam]

    return pl.pallas_call(
        body, name="adamw_small", out_shape=shapes * 4,
        in_specs=[VM] * 19, out_specs=[VM] * 24, compiler_params=_params(),
    )(*ws, g8, *ms, *vs)


def _adamw(w, g, m, v, name, after=()):
    r, c = w.shape
    tr = _row_tile(r, c, target=1024 * 1024, mult=8)

    def body(w_ref, g_ref, m_ref, v_ref, *rest):
        d_ref, nm_ref, nv_ref = rest[len(after):]
        d_ref[...], nm_ref[...], nv_ref[...] = _adamw_math(w_ref[...], g_ref[...], m_ref[...], v_ref[...])

    spec = pl.BlockSpec((tr, c), lambda i: (i, 0))
    return pl.pallas_call(
        body, name=name, grid=(r // tr,),
        out_shape=[jax.ShapeDtypeStruct((r, c), F32)] * 3,
        in_specs=[spec] * 4 + [ANY] * len(after), out_specs=[spec] * 3,
        compiler_params=_params(("parallel",)),
    )(w, g, m, v, *after)


def _rope_coefficients(pos, ropetab, ts):
    s_len = pos.shape[0]

    def body(pos_ref, tab_ref, o_ref):
        ang = pos_ref[...].astype(F32) * tab_ref[0:1, :]
        cs, sn = jnp.cos(ang), jnp.sin(ang)
        o_ref[0] = jnp.where(tab_ref[3:4, :] > 0, cs, 1.0)
        o_ref[1] = -sn * tab_ref[1:2, :]
        o_ref[2] = sn * tab_ref[2:3, :]

    return pl.pallas_call(
        body, name="rope_coefficients", grid=(s_len // ts,),
        in_specs=[pl.BlockSpec((ts, 1), lambda i: (i, 0)), pl.BlockSpec((8, LANES), lambda i: (0, 0))],
        out_specs=pl.BlockSpec((3, ts, LANES), lambda i: (0, i, 0)),
        out_shape=jax.ShapeDtypeStruct((3, s_len, LANES), F32),
        compiler_params=_params(("parallel",)),
    )(pos, ropetab)


def _deinterleave_store(vals, slab, out_ref, d, ts, dtype):
    if d == 1:
        for s in range(4):
            out_ref[0, :, s * LANES:(s + 1) * LANES] = vals[s].astype(dtype)
        return
    for s in range(4):
        slab[s] = vals[s]
    if d == 16:
        q = ts // 4
        for s in range(4):
            for r1 in range(4):
                slab[4 + s, r1 * q:(r1 + 1) * q, :] = slab[s, pl.ds(r1, q, stride=4), :]
        for r in range(d):
            r1, r2 = r % 4, r // 4
            for s in range(4):
                out_ref[r, :, s * LANES:(s + 1) * LANES] = slab[
                    4 + s, pl.ds(r1 * q + r2, ts // d, stride=4), :].astype(dtype)
        return
    for r in range(d):
        for s in range(4):
            out_ref[r, :, s * LANES:(s + 1) * LANES] = slab[s, pl.ds(r, ts // d, stride=d), :].astype(dtype)


def _interleave_load(blk_ref, slab, d, ts):
    if d == 1:
        return [blk_ref[0, :, s * LANES:(s + 1) * LANES] for s in range(4)]
    if d == 16:
        q = ts // 4
        for r in range(d):
            r1, r2 = r % 4, r // 4
            for s in range(4):
                slab[4 + s, pl.ds(r1 * q + r2, ts // d, stride=4), :] = blk_ref[r, :, s * LANES:(s + 1) * LANES]
        for s in range(4):
            for r1 in range(4):
                slab[s, pl.ds(r1, q, stride=4), :] = slab[4 + s, r1 * q:(r1 + 1) * q, :]
        return [slab[s] for s in range(4)]
    for r in range(d):
        for s in range(4):
            slab[s, pl.ds(r, ts // d, stride=d), :] = blk_ref[r, :, s * LANES:(s + 1) * LANES]
    return [slab[s] for s in range(4)]


def _norm_modulate(x, modv, norm_g, ts):
    s_len, d_model = x.shape

    def body(x_ref, mod_ref, g_ref, h_ref, ht_ref):
        xv = x_ref[...]
        r = lax.rsqrt(jnp.mean(xv * xv, axis=-1, keepdims=True) + EPS)
        h = (xv * r) * g_ref[...] * (1.0 + mod_ref[1:2, :]) + mod_ref[0:1, :]
        h_ref[...] = h.astype(BF16)
        ht_ref[...] = h.T.astype(BF16)

    return pl.pallas_call(
        body, name="norm_modulate", grid=(s_len // ts,),
        in_specs=[pl.BlockSpec((ts, d_model), lambda i: (i, 0)), pl.BlockSpec((8, d_model), lambda i: (0, 0)),
                  pl.BlockSpec((1, d_model), lambda i: (0, 0))],
        out_specs=[pl.BlockSpec((ts, d_model), lambda i: (i, 0)), pl.BlockSpec((d_model, ts), lambda i: (0, i))],
        out_shape=[jax.ShapeDtypeStruct((s_len, d_model), BF16), jax.ShapeDtypeStruct((d_model, s_len), BF16)],
        compiler_params=_params(("parallel",)),
    )(x, modv, norm_g)


def _plain_projection(h, w_in, first_tile, n_tiles, ts, name):
    s_len, d_model = h.shape
    ct = COL_TILE

    def body(h_ref, w_ref, o_ref):
        o_ref[...] = _dot(h_ref[...], w_ref[...])

    return pl.pallas_call(
        body, name=name, grid=(s_len // ts, n_tiles),
        in_specs=[pl.BlockSpec((ts, d_model), lambda i, j: (i, 0)),
                  pl.BlockSpec((d_model, ct), lambda i, j: (0, first_tile + j))],
        out_specs=pl.BlockSpec((ts, ct), lambda i, j: (i, j)),
        out_shape=jax.ShapeDtypeStruct((s_len, n_tiles * ct), F32),
        compiler_params=_params(("parallel", "arbitrary")),
    )(h, w_in)


def _qkv_projection(h, rope, w_in, gi, ts):
    s_len, d_model = h.shape
    dil = DILATIONS[gi]
    nc = 3 * d_model // COL_TILE
    hr = ts // 2
    assert COL_TILE == ATT_W and hr % (16 * dil) == 0

    def body(h_ref, rope_ref, wq_ref, wk_ref, wv_ref, o_ref, slab):
        rc, ra, rb = rope_ref.at[0], rope_ref.at[1], rope_ref.at[2]
        w_refs = (wq_ref, wk_ref, wv_ref)
        units = [(w, half) for w in range(3) for half in range(2)]

        def matmul(w, half):
            return _dot(h_ref[half * hr:(half + 1) * hr, :], w_refs[w][...])

        def finish(w, half, res):
            rows = slice(half * hr, (half + 1) * hr)
            vals = []
            for s in range(4):
                t = res[:, s * LANES:(s + 1) * LANES]
                if w < 2:
                    t = (t * rc[rows, :] + pltpu.roll(t, LANES - 8, 1) * ra[rows, :]
                         + pltpu.roll(t, 8, 1) * rb[rows, :])
                vals.append(t)
            out = o_ref.at[w, :, half * (hr // dil):(half + 1) * (hr // dil), :]
            _deinterleave_store(vals, slab.at[half], out, dil, hr, BF16)

        res_next = matmul(*units[0])
        for ui, (w, half) in enumerate(units):
            res = res_next
            if ui + 1 < len(units):
                res_next = matmul(*units[ui + 1])
            finish(w, half, res)

    def w_spec(w):
        return pl.BlockSpec((d_model, COL_TILE), lambda i: (0, nc + 3 * w + gi))

    return pl.pallas_call(
        body, name="qkv_projection_%d" % dil, grid=(s_len // ts,),
        in_specs=[pl.BlockSpec((ts, d_model), lambda i: (i, 0)), pl.BlockSpec((3, ts, LANES), lambda i: (0, i, 0)),
                  w_spec(0), w_spec(1), w_spec(2)],
        out_specs=pl.BlockSpec((3, dil, ts // dil, ATT_W), lambda i: (0, 0, i, 0)),
        out_shape=jax.ShapeDtypeStruct((3, dil, s_len // dil, ATT_W), BF16),
        scratch_shapes=[pltpu.VMEM((2, 8, hr, LANES), F32)],
        compiler_params=_params(("parallel",)),
    )(h, rope, w_in, w_in, w_in)


def _layernorm_stats(u1):
    mu = jnp.mean(u1, axis=-1, keepdims=True)
    xc = u1 - mu
    rstd = lax.rsqrt(jnp.mean(xc * xc, axis=-1, keepdims=True) + EPS)
    return xc * rstd, rstd


def _shifted_copies(win, shf, ts):
    rows = ts + HALO - 8
    for b in range(1, 8):
        shf[b - 1, 0:rows, :] = win[pl.ds(b, rows), :]


def _tap(win, shf, off, r0, rows):
    a, b = divmod(off, 8)
    start = 8 * a + r0
    if b == 0:
        return win[start:start + rows, :]
    return shf[b - 1, start:start + rows, :]


def _conv_forward(p_conv, conv_w, conv_b, ln_g, ln_b, w_co, ts):
    s_len, d3 = p_conv.shape
    dm = d3 // 3

    def body(p_ref, cw_ref, cb_ref, g_ref, b_ref, w_ref, y_ref, u0_ref, u1_ref, win, shf):
        i = pl.program_id(0)

        @pl.when(i == 0)
        def _():
            win[0:HALO, :] = jnp.zeros((HALO, dm), F32)

        a, b, z = p_ref[:, 0:dm], p_ref[:, dm:2 * dm], p_ref[:, 2 * dm:3 * dm]
        u0 = a * _sig(b)
        win[HALO:HALO + ts, :] = u0
        u0_ref[...] = u0
        _shifted_copies(win, shf, ts)
        for r0 in range(0, ts, CONV_ROWS_FWD):
            acc = jnp.broadcast_to(cb_ref[...], (CONV_ROWS_FWD, dm))
            for k in range(CONV_K):
                acc = acc + cw_ref[k:k + 1, :] * _tap(win, shf, HALO - (CONV_K - 1) + k, r0, CONV_ROWS_FWD)
            u1_ref[r0:r0 + CONV_ROWS_FWD, :] = acc
        xh, _ = _layernorm_stats(u1_ref[...])
        u2 = xh * g_ref[...] + b_ref[...]
        a_conv = (u2 * _sig(u2)) * (z * _sig(z))
        y_ref[...] = _dot(a_conv.astype(BF16), w_ref[...])
        win[0:HALO, :] = win[ts:ts + HALO, :]

    row = pl.BlockSpec((1, dm), lambda i: (0, 0))
    tile = pl.BlockSpec((ts, dm), lambda i: (i, 0))
    return pl.pallas_call(
        body, name="conv_forward", grid=(s_len // ts,),
        in_specs=[pl.BlockSpec((ts, d3), lambda i: (i, 0)),
                  pl.BlockSpec((HALO, dm), lambda i: (0, 0)), row, row, row,
                  pl.BlockSpec((dm, dm), lambda i: (0, 0))],
        out_specs=[tile, tile, tile],
        out_shape=[jax.ShapeDtypeStruct((s_len, dm), F32)] * 3,
        scratch_shapes=[pltpu.VMEM((ts + HALO, dm), F32), pltpu.VMEM((7, ts + HALO - 8, dm), F32)],
        compiler_params=_params(("arbitrary",)),
    )(p_conv, conv_w, conv_b, ln_g, ln_b, w_co)


def _conv_backward(dp, dyc, p_conv, u0, u1, conv_w, ln_g, ln_b, w_co, ts):
    s_len, d3 = p_conv.shape
    dm = d3 // 3
    nt = s_len // ts
    hb = ts // HALO

    def body(dp_in, dy_ref, p_ref, u0_ref, uh_ref, u1_ref, cw_ref, g_ref, b_ref, w_ref,
             dp_ref, gw_ref, gs_ref, gcw_ref, dwin, uwin, shf):
        del dp_in
        i = pl.program_id(0)
        ti = nt - 1 - i

        @pl.when(i == 0)
        def _():
            gw_ref[...] = jnp.zeros_like(gw_ref)
            gs_ref[...] = jnp.zeros_like(gs_ref)
            gcw_ref[...] = jnp.zeros_like(gcw_ref)
            dwin[ts:ts + HALO, :] = jnp.zeros((HALO, dm), F32)

        dy = dy_ref[...]
        z = p_ref[:, 2 * dm:3 * dm]
        d_ac = _dot_nt(dy, w_ref[...])
        xh, rstd = _layernorm_stats(u1_ref[...])
        u2 = xh * g_ref[...] + b_ref[...]
        sg2, sgz = _sig(u2), _sig(z)
        u3, sz = u2 * sg2, z * sgz
        gw_ref[...] += _dot_tn((u3 * sz).astype(BF16), dy)
        d_z = d_ac * u3 * _dsilu(z, sgz)
        d_u2 = d_ac * sz * _dsilu(u2, sg2)
        gs_ref[0:1, :] += jnp.sum(d_u2 * xh, axis=0, keepdims=True)
        gs_ref[1:2, :] += jnp.sum(d_u2, axis=0, keepdims=True)
        dxh = d_u2 * g_ref[...]
        d_u1 = rstd * (dxh - jnp.mean(dxh, axis=-1, keepdims=True)
                       - xh * jnp.mean(dxh * xh, axis=-1, keepdims=True))
        gs_ref[2:3, :] += jnp.sum(d_u1, axis=0, keepdims=True)
        dwin[0:ts, :] = d_u1
        uwin[0:HALO, :] = jnp.where(ti == 0, 0.0, uh_ref[...])
        uwin[HALO:HALO + ts, :] = u0_ref[...]
        dp_ref[:, 2 * dm:3 * dm] = d_z.astype(BF16)
        _shifted_copies(uwin, shf, ts)
        for k in range(CONV_K):
            part = jnp.zeros((CONV_ROWS_BWD, dm), F32)
            for r0 in range(0, ts, CONV_ROWS_BWD):
                part = part + dwin[r0:r0 + CONV_ROWS_BWD, :] * _tap(uwin, shf, HALO - (CONV_K - 1) + k, r0,
                                                                    CONV_ROWS_BWD)
            gcw_ref[k:k + 1, :] += jnp.sum(part, axis=0, keepdims=True)
        _shifted_copies(dwin, shf, ts)
        for r0 in range(0, ts, CONV_ROWS_BWD):
            d_u0 = jnp.zeros((CONV_ROWS_BWD, dm), F32)
            for k in range(CONV_K):
                d_u0 = d_u0 + cw_ref[k:k + 1, :] * _tap(dwin, shf, CONV_K - 1 - k, r0, CONV_ROWS_BWD)
            rows = slice(r0, r0 + CONV_ROWS_BWD)
            sgb = _sig(p_ref[rows, dm:2 * dm])
            dp_ref[rows, 0:dm] = (d_u0 * sgb).astype(BF16)
            dp_ref[rows, dm:2 * dm] = (d_u0 * p_ref[rows, 0:dm] * sgb * (1.0 - sgb)).astype(BF16)
        dwin[ts:ts + HALO, :] = dwin[0:HALO, :]

    rev = lambda i: (nt - 1 - i, 0)
    row = pl.BlockSpec((1, dm), lambda i: (0, 0))
    tile = pl.BlockSpec((ts, dm), rev)
    return pl.pallas_call(
        body, name="conv_backward", grid=(nt,),
        in_specs=[ANY, tile, pl.BlockSpec((ts, d3), rev), tile,
                  pl.BlockSpec((HALO, dm), lambda i: (jnp.maximum((nt - 1 - i) * hb - 1, 0), 0)),
                  tile, pl.BlockSpec((HALO, dm), lambda i: (0, 0)), row, row,
                  pl.BlockSpec((dm, dm), lambda i: (0, 0))],
        out_specs=[pl.BlockSpec((ts, d3), rev), pl.BlockSpec((dm, dm), lambda i: (0, 0)),
                   pl.BlockSpec((8, dm), lambda i: (0, 0)), pl.BlockSpec((HALO, dm), lambda i: (0, 0))],
        out_shape=[jax.ShapeDtypeStruct(dp.shape, BF16), jax.ShapeDtypeStruct((dm, dm), F32),
                   jax.ShapeDtypeStruct((8, dm), F32), jax.ShapeDtypeStruct((HALO, dm), F32)],
        input_output_aliases={0: 0},
        scratch_shapes=[pltpu.VMEM((ts + HALO, dm), F32), pltpu.VMEM((ts + HALO, dm), F32),
                        pltpu.VMEM((7, ts + HALO - 8, dm), F32)],
        compiler_params=_params(("arbitrary",)),
    )(dp, dyc, p_conv, u0, u0, u1, conv_w, ln_g, ln_b, w_co)


def _att_masks():
    head0 = lax.broadcasted_iota(jnp.int32, (BLK, LANES), 1) < 64
    col = lax.broadcasted_iota(jnp.int32, (BLK, 4 * BLK), 1)
    row = lax.broadcasted_iota(jnp.int32, (BLK, 4 * BLK), 0)
    kj = col % BLK
    prev = jnp.where(col < 2 * BLK, 1, 0)
    band = jnp.where(col < 2 * BLK, kj - row, row - kj)
    return head0, band, prev


def _fill_block_diagonal(dst, slab, src_ref, halo_ref, head0, nq):
    sl = slice(slab * LANES, (slab + 1) * LANES)
    for b in range(nq + 1):
        blk = halo_ref[:, sl] if b == 0 else src_ref[(b - 1) * BLK:b * BLK, sl]
        base = (slab * (nq + 1) + b) * 2 * BLK
        zero = jnp.zeros_like(blk)
        dst[base:base + BLK, :] = jnp.where(head0, blk, zero)
        dst[base + BLK:base + 2 * BLK, :] = jnp.where(head0, zero, blk)


def _attention_forward(qkv, seq_len, qt, name):
    s_len = qkv.shape[1]
    nq = qt // BLK
    tiles_per_seq = seq_len // qt

    def body(q_ref, k_ref, v_ref, kh_ref, vh_ref, o_ref, lse_ref, kbd, vbd):
        first = jnp.where((pl.program_id(0) % tiles_per_seq) == 0, 4 * BLK, 0)
        head0, band, prev = _att_masks()
        band_first = band - prev * first
        for p in range(4):
            _fill_block_diagonal(kbd, p, k_ref, kh_ref, head0, nq)
            _fill_block_diagonal(vbd, p, v_ref, vh_ref, head0, nq)
        units = [(p, n) for p in range(4) for n in range(nq)]

        def keys_of(p, n):
            base = (p * (nq + 1) + n) * 2 * BLK
            return slice(base, base + 4 * BLK)

        def scores(p, n):
            q2 = q_ref[n * BLK:(n + 1) * BLK, p * LANES:(p + 1) * LANES] * 0.125
            return _dot_nt(q2, kbd[keys_of(p, n), :])

        def finish(p, n, o, den, lse):
            rows, sl = slice(n * BLK, (n + 1) * BLK), slice(p * LANES, (p + 1) * LANES)
            o_ref[rows, sl] = o / jnp.where(head0, den[0], den[1])
            lse_ref[rows, sl] = jnp.where(head0, lse[0], lse[1])

        s_next = scores(*units[0])
        pending = None
        for ui, (p, n) in enumerate(units):
            s = s_next
            if ui + 1 < len(units):
                s_next = scores(*units[ui + 1])
            s = jnp.where((band_first if n == 0 else band) >= 0, s, NEG)
            grp = [s[:, g * BLK:(g + 1) * BLK] for g in range(4)]
            ps, den, lse = [None] * 4, [], []
            for h in range(2):
                m = jnp.max(jnp.maximum(grp[h], grp[2 + h]), axis=-1, keepdims=True)
                ps[h], ps[2 + h] = jnp.exp(grp[h] - m), jnp.exp(grp[2 + h] - m)
                dn = jnp.sum(ps[h] + ps[2 + h], axis=-1, keepdims=True)
                den.append(dn)
                lse.append(m + jnp.log(dn))
            pmat = jnp.concatenate([x.astype(BF16) for x in ps], axis=1)
            o = _dot(pmat, vbd[keys_of(p, n), :])
            if pending is not None:
                finish(*pending)
            pending = (p, n, o, den, lse)
        finish(*pending)

    def which(w):
        return pl.BlockSpec((None, qt, ATT_W), lambda i: (w, i, 0))

    def halo(w):
        return pl.BlockSpec((None, BLK, ATT_W), lambda i: (w, jnp.maximum(i * nq - 1, 0), 0))

    out = pl.BlockSpec((qt, ATT_W), lambda i: (i, 0))
    bd = pltpu.VMEM((4 * (nq + 1) * 2 * BLK, LANES), BF16)
    return pl.pallas_call(
        body, name=name, grid=(s_len // qt,),
        in_specs=[which(0), which(1), which(2), halo(1), halo(2)],
        out_specs=[out, out], out_shape=[jax.ShapeDtypeStruct((s_len, ATT_W), F32)] * 2,
        scratch_shapes=[bd, bd],
        compiler_params=_params(("parallel",)),
    )(qkv, qkv, qkv, qkv, qkv)


def _attention_backward(qkv, d_att, lse, delta, seq_len, qt, name):
    s_len = qkv.shape[1]
    nq = qt // BLK
    tiles_per_seq = seq_len // qt
    nblk = s_len // BLK

    def body(q_ref, k_ref, v_ref, kh_ref, vh_ref, do_ref, lse_ref, dl_ref,
             qn_ref, don_ref, lsen_ref, dln_ref, dqkv_ref, kbd, vbd):
        i = pl.program_id(0)
        first = jnp.where((i % tiles_per_seq) == 0, 4 * BLK, 0)
        last = jnp.where((i % tiles_per_seq) == tiles_per_seq - 1, 4 * BLK, 0)
        head0, band, prev = _att_masks()
        band_first = band - prev * first
        band_tail = band[:, 0:2 * BLK] - last
        for p in range(4):
            _fill_block_diagonal(kbd, p, k_ref, kh_ref, head0, nq)
            _fill_block_diagonal(vbd, p, v_ref, vh_ref, head0, nq)
        units = [(p, n) for p in range(4) for n in range(nq + 1)]

        def stage_a(p, n):
            sl = slice(p * LANES, (p + 1) * LANES)
            base = (p * (nq + 1) + n) * 2 * BLK
            if n < nq:
                rows = slice(n * BLK, (n + 1) * BLK)
                q2, do2, lse2, dl2 = q_ref[rows, sl], do_ref[rows, sl], lse_ref[rows, sl], dl_ref[rows, sl]
                keys = slice(base, base + 4 * BLK)
            else:
                q2, do2, lse2, dl2 = qn_ref[:, sl], don_ref[:, sl], lsen_ref[:, sl], dln_ref[:, sl]
                keys = slice(base, base + 2 * BLK)
            s = _dot_nt(q2 * 0.125, kbd[keys, :])
            dp = _dot_nt(do2, vbd[keys, :])
            return q2, do2, lse2, dl2, keys, s, dp

        def stage_b(n, lse2, dl2, s, dp):
            mask = band_tail if n == nq else (band_first if n == 0 else band)
            ps, dss = [], []
            for g in range(s.shape[1] // BLK):
                h = g % 2
                cols = slice(g * BLK, (g + 1) * BLK)
                pg = jnp.exp(jnp.where(mask[:, cols] >= 0, s[:, cols] - lse2[:, h * 64:h * 64 + 1], NEG))
                ps.append(pg.astype(BF16))
                dss.append((pg * (dp[:, cols] - dl2[:, h * 64:h * 64 + 1]) * 0.125).astype(BF16))
            return jnp.concatenate(ps, axis=1), jnp.concatenate(dss, axis=1)

        def heads(r, g):
            return jnp.where(head0, r[g * BLK:(g + 1) * BLK, :], r[(g + 1) * BLK:(g + 2) * BLK, :])

        a_next = stage_a(*units[0])
        carry = None
        for ui, (p, n) in enumerate(units):
            q2, do2, lse2, dl2, keys, s, dp = a_next
            if ui + 1 < len(units):
                a_next = stage_a(*units[ui + 1])
            pmat, dsmat = stage_b(n, lse2, dl2, s, dp)
            sl = slice(p * LANES, (p + 1) * LANES)
            if n < nq:
                dqkv_ref[0, n * BLK:(n + 1) * BLK, sl] = _dot(dsmat, kbd[keys, :])
            dkbd = _dot_tn(dsmat, q2)
            dvbd = _dot_tn(pmat, do2)
            if n > 0:
                prow = slice((n - 1) * BLK, n * BLK)
                dqkv_ref[1, prow, sl] = carry[0] + heads(dkbd, 0)
                dqkv_ref[2, prow, sl] = carry[1] + heads(dvbd, 0)
            if n < nq:
                carry = (heads(dkbd, 2), heads(dvbd, 2))

    def which(w):
        return pl.BlockSpec((None, qt, ATT_W), lambda i: (w, i, 0))

    def prev(w):
        return pl.BlockSpec((None, BLK, ATT_W), lambda i: (w, jnp.maximum(i * nq - 1, 0), 0))

    tile = pl.BlockSpec((qt, ATT_W), lambda i: (i, 0))
    nxt = pl.BlockSpec((BLK, ATT_W), lambda i: (jnp.minimum((i + 1) * nq, nblk - 1), 0))
    nxt_q = pl.BlockSpec((None, BLK, ATT_W), lambda i: (0, jnp.minimum((i + 1) * nq, nblk - 1), 0))
    return pl.pallas_call(
        body, name=name, grid=(s_len // qt,),
        in_specs=[which(0), which(1), which(2), prev(1), prev(2), tile, tile, tile, nxt_q, nxt, nxt, nxt],
        out_specs=pl.BlockSpec((3, qt, ATT_W), lambda i: (0, i, 0)),
        out_shape=jax.ShapeDtypeStruct((3, s_len, ATT_W), F32),
        scratch_shapes=[pltpu.VMEM((4 * (nq + 1) * 2 * BLK, LANES), BF16)] * 2,
        compiler_params=_params(("parallel",)),
    )(qkv, qkv, qkv, qkv, qkv, d_att, lse, delta, qkv, d_att, lse, delta)


def _merge_and_head(o_g, lse_g, p_gate, y_conv, x, tgt, modv, final_g, w_ao, w_o, ts):
    s_len, dm = x.shape
    gw = ATT_W + 2 * dm
    nt = s_len // ts
    gate_off = 3 * dm + 9 * ATT_W
    assert gate_off % gw == 0

    def body(o0, o1, o2, l0, l1, l2, pg_ref, yc_ref, x_ref, t_ref, mod_ref, fg_ref, wao_ref, wo_ref,
             loss_ref, dx_ref, dyc_ref, dpg_ref, da0, da1, da2, ls0, ls1, ls2, de0, de1, de2,
             gwo_ref, gwao_ref, gs_ref, slab):
        i = pl.program_id(0)

        @pl.when(i == 0)
        def _():
            loss_ref[...] = jnp.zeros_like(loss_ref)
            gwo_ref[...] = jnp.zeros_like(gwo_ref)
            gwao_ref[...] = jnp.zeros_like(gwao_ref)
            gs_ref[...] = jnp.zeros_like(gs_ref)

        os_, ls_ = [], []
        for dil, o_ref, l_ref in zip(DILATIONS, (o0, o1, o2), (l0, l1, l2)):
            os_.append(jnp.concatenate(_interleave_load(o_ref, slab, dil, ts), axis=1))
            ls_.append(jnp.concatenate(_interleave_load(l_ref, slab, dil, ts), axis=1))
        mx = jnp.maximum(jnp.maximum(ls_[0], ls_[1]), ls_[2])
        wts = [jnp.exp(l - mx) for l in ls_]
        wsum = wts[0] + wts[1] + wts[2]
        att = (wts[0] * os_[0] + wts[1] * os_[1] + wts[2] * os_[2]) / wsum
        lse_all = mx + jnp.log(wsum)

        z_att, g_conv, g_att = pg_ref[:, 0:ATT_W], pg_ref[:, ATT_W:ATT_W + dm], pg_ref[:, ATT_W + dm:gw]
        sgz = _sig(z_att)
        sz = z_att * sgz
        a_att = (att * sz).astype(BF16)
        y_att = _dot(a_att, wao_ref[...])
        y_conv = yc_ref[...]
        sgc, sga = _sig(g_conv), _sig(g_att)
        merged = (sgc * y_conv + sga * y_att).astype(BF16)
        mo = _dot(merged, wo_ref[...])
        gate = mod_ref[2:3, :]
        x2 = x_ref[...] + gate * mo
        r = lax.rsqrt(jnp.mean(x2 * x2, axis=-1, keepdims=True) + EPS)
        xr = x2 * r
        err = xr * fg_ref[...] - t_ref[...]
        loss_ref[...] += 0.5 * jnp.sum(jnp.mean(err * err, axis=-1, keepdims=True))
        dy = err * (1.0 / dm)
        gs_ref[0:1, :] += jnp.sum(dy * xr, axis=0, keepdims=True)
        dyg = dy * fg_ref[...]
        d_x2 = r * dyg - xr * (r * jnp.mean(dyg * xr, axis=-1, keepdims=True))
        dx_ref[...] = d_x2
        gs_ref[1:2, :] += jnp.sum(d_x2 * mo, axis=0, keepdims=True)
        d_mo = (d_x2 * gate).astype(BF16)
        d_mg = _dot_nt(d_mo, wo_ref[...])
        gwo_ref[...] += _dot_tn(merged, d_mo)
        dyc_ref[...] = (d_mg * sgc).astype(BF16)
        dpg_ref[:, ATT_W:ATT_W + dm] = (d_mg * y_conv * sgc * (1.0 - sgc)).astype(BF16)
        d_ya = (d_mg * sga).astype(BF16)
        dpg_ref[:, ATT_W + dm:gw] = (d_mg * y_att * sga * (1.0 - sga)).astype(BF16)
        gwao_ref[...] += _dot_tn(a_att, d_ya)
        d_aa = _dot_nt(d_ya, wao_ref[...])
        dpg_ref[:, 0:ATT_W] = (d_aa * att * _dsilu(z_att, sgz)).astype(BF16)
        d_att = d_aa * sz
        ri = lax.broadcasted_iota(jnp.int32, (ATT_W, ATT_W), 0) // 64
        ci = lax.broadcasted_iota(jnp.int32, (ATT_W, ATT_W), 1) // 64
        ones = jnp.where(ri == ci, 1.0, 0.0).astype(BF16)
        prod = d_att * att
        hi = prod.astype(BF16)
        lo = (prod - hi.astype(F32)).astype(BF16)
        delta = _dot(hi, ones) + _dot(lo, ones)
        for val, refs, dt in ((d_att, (da0, da1, da2), BF16), (lse_all, (ls0, ls1, ls2), F32),
                              (delta, (de0, de1, de2), F32)):
            vals = [val[:, s * LANES:(s + 1) * LANES] for s in range(4)]
            for dil, ref in zip(DILATIONS, refs):
                _deinterleave_store(vals, slab, ref, dil, ts, dt)

    def grp(dil):
        return pl.BlockSpec((dil, ts // dil, ATT_W), lambda i: (0, i, 0))

    tile = pl.BlockSpec((ts, dm), lambda i: (i, 0))
    gate_tile = pl.BlockSpec((ts, gw), lambda i: (i, 0))
    const = lambda shp: pl.BlockSpec(shp, lambda i: tuple(0 for _ in shp))
    grp_shape = lambda dt: [jax.ShapeDtypeStruct((dil, s_len // dil, ATT_W), dt) for dil in DILATIONS]
    return pl.pallas_call(
        body, name="merge_and_head", grid=(nt,),
        in_specs=[grp(d) for d in DILATIONS] * 2
        + [gate_tile, tile, tile, tile, const((8, dm)), const((1, dm)), const((ATT_W, dm)), const((dm, dm))],
        out_specs=[const((8, LANES)), tile, tile, pl.BlockSpec((ts, gw), lambda i: (i, gate_off // gw))]
        + [grp(d) for d in DILATIONS] * 3
        + [const((dm, dm)), const((ATT_W, dm)), const((8, dm))],
        out_shape=[jax.ShapeDtypeStruct((8, LANES), F32), jax.ShapeDtypeStruct((s_len, dm), F32),
                   jax.ShapeDtypeStruct((s_len, dm), BF16), jax.ShapeDtypeStruct((s_len, gate_off + gw), BF16)]
        + grp_shape(BF16) + grp_shape(F32) + grp_shape(F32)
        + [jax.ShapeDtypeStruct((dm, dm), F32), jax.ShapeDtypeStruct((ATT_W, dm), F32),
           jax.ShapeDtypeStruct((8, dm), F32)],
        scratch_shapes=[pltpu.VMEM((8, ts, LANES), F32)],
        compiler_params=_params(("arbitrary",)),
    )(*o_g, *lse_g, p_gate, y_conv, x, tgt, modv, final_g, w_ao, w_o)


def _qkv_grad_to_tokens(dp, dqkv_g, rope, ts):
    s_len, dm = rope.shape[1], (dp.shape[1] - 10 * ATT_W) // 5
    qw = 3 * ATT_W
    assert (3 * dm) % qw == 0

    def body(dp_in, g0, g1, g2, rope_ref, o_ref, slab):
        del dp_in
        w = pl.program_id(1)
        rc, ra, rb = rope_ref.at[0], rope_ref.at[1], rope_ref.at[2]

        def emit(roped):
            for gi, (dil, g_ref) in enumerate(zip(DILATIONS, (g0, g1, g2))):
                vals = _interleave_load(g_ref, slab, dil, ts)
                for s in range(4):
                    t = vals[s]
                    if roped:
                        t = t * rc[...] + pltpu.roll(t * ra[...], 8, 1) + pltpu.roll(t * rb[...], LANES - 8, 1)
                    col = gi * ATT_W + s * LANES
                    o_ref[:, col:col + LANES] = t.astype(BF16)

        pl.when(w < 2)(lambda: emit(True))
        pl.when(w == 2)(lambda: emit(False))

    return pl.pallas_call(
        body, name="qkv_grad_to_tokens", grid=(s_len // ts, 3),
        in_specs=[ANY] + [pl.BlockSpec((None, dil, ts // dil, ATT_W), lambda i, w: (w, 0, i, 0)) for dil in DILATIONS]
        + [pl.BlockSpec((3, ts, LANES), lambda i, w: (0, i, 0))],
        out_specs=pl.BlockSpec((ts, qw), lambda i, w: (i, 3 * dm // qw + w)),
        out_shape=jax.ShapeDtypeStruct(dp.shape, BF16),
        input_output_aliases={0: 0},
        scratch_shapes=[pltpu.VMEM((8, ts, LANES), F32)],
        compiler_params=_params(("parallel", "arbitrary")),
    )(dp, *dqkv_g, rope)


def _wide_col_tile(cols):
    for width in (5 * COL_TILE, 2 * COL_TILE):
        if cols % width == 0:
            return width
    return COL_TILE


def _input_grad(dp, w_in, x, dx_res, modv, norm_g, ts):
    s_len, dm = x.shape
    ct = _wide_col_tile(dp.shape[1])
    nct = dp.shape[1] // ct

    def body(p_ref, w_ref, x_ref, dxr_ref, mod_ref, g_ref, gx_ref, gs_ref, acc):
        i, j = pl.program_id(0), pl.program_id(1)

        @pl.when((i == 0) & (j == 0))
        def _():
            gs_ref[...] = jnp.zeros_like(gs_ref)

        @pl.when(j == 0)
        def _():
            acc[...] = jnp.zeros_like(acc)

        acc[...] += _dot_nt(p_ref[...], w_ref[...])

        @pl.when(j == nct - 1)
        def _():
            d_h = acc[...]
            xv = x_ref[...]
            r = lax.rsqrt(jnp.mean(xv * xv, axis=-1, keepdims=True) + EPS)
            xr = xv * r
            gs_ref[0:1, :] += jnp.sum(d_h, axis=0, keepdims=True)
            gs_ref[1:2, :] += jnp.sum(d_h * (xr * g_ref[...]), axis=0, keepdims=True)
            d_n = d_h * (1.0 + mod_ref[1:2, :])
            gs_ref[2:3, :] += jnp.sum(d_n * xr, axis=0, keepdims=True)
            dxn = d_n * g_ref[...]
            gx_ref[...] = dxr_ref[...] + r * dxn - xr * (r * jnp.mean(dxn * xr, axis=-1, keepdims=True))

    tile = pl.BlockSpec((ts, dm), lambda i, j: (i, 0))
    return pl.pallas_call(
        body, name="input_grad", grid=(s_len // ts, nct),
        in_specs=[pl.BlockSpec((ts, ct), lambda i, j: (i, j)), pl.BlockSpec((dm, ct), lambda i, j: (0, j)), tile, tile,
                  pl.BlockSpec((8, dm), lambda i, j: (0, 0)), pl.BlockSpec((1, dm), lambda i, j: (0, 0))],
        out_specs=[tile, pl.BlockSpec((8, dm), lambda i, j: (0, 0))],
        out_shape=[jax.ShapeDtypeStruct((s_len, dm), F32), jax.ShapeDtypeStruct((8, dm), F32)],
        scratch_shapes=[pltpu.VMEM((ts, dm), F32)],
        compiler_params=_params(("arbitrary", "arbitrary")),
    )(dp, w_in, x, dx_res, modv, norm_g)


def _w_in_grad(h_t, dp, ts):
    dm, s_len = h_t.shape
    ct = _wide_col_tile(dp.shape[1])

    def body(h_ref, p_ref, o_ref):
        @pl.when(pl.program_id(1) == 0)
        def _():
            o_ref[...] = jnp.zeros_like(o_ref)

        o_ref[...] += _dot(h_ref[...], p_ref[...])

    return pl.pallas_call(
        body, name="w_in_grad", grid=(dp.shape[1] // ct, s_len // ts),
        in_specs=[pl.BlockSpec((dm, ts), lambda j, i: (0, i)), pl.BlockSpec((ts, ct), lambda j, i: (i, j))],
        out_specs=pl.BlockSpec((dm, ct), lambda j, i: (0, j)),
        out_shape=jax.ShapeDtypeStruct((dm, dp.shape[1]), F32),
        compiler_params=_params(("arbitrary", "arbitrary")),
    )(h_t, dp)


def _rope_lane_table():
    l64 = jnp.arange(LANES) % 64
    half = ROT_DIM // 2
    inv_freq = ROPE_THETA ** (-(jnp.arange(half, dtype=F32) * 2.0 / ROT_DIM))
    rot = l64 < ROT_DIM
    rows = [jnp.where(rot, inv_freq[l64 % half], 0.0), (l64 < half).astype(F32),
            ((l64 >= half) & rot).astype(F32), rot.astype(F32)]
    return jnp.concatenate([jnp.stack(rows), jnp.zeros((4, LANES), F32)], axis=0)


def _tile_sizes(s_len):
    ts_big = min(1024, s_len // 2)
    ts_mid = 256
    ts_head = 256
    qt = [min(1024, s_len // dil) for dil in DILATIONS]
    return ts_big, ts_mid, ts_head, qt


def kernel(x, c, positions, norm_g, w_ada, b_ada, w_in, conv_w, conv_b, conv_ln_g, conv_ln_b, w_conv_out, w_att_out, w_o, final_g, loss_target, m_norm_g, m_w_ada, m_b_ada, m_w_in, m_conv_w, m_conv_b, m_conv_ln_g, m_conv_ln_b, m_w_conv_out, m_w_att_out, m_w_o, m_final_g, v_norm_g, v_w_ada, v_b_ada, v_w_in, v_conv_w, v_conv_b, v_conv_ln_g, v_conv_ln_b, v_w_conv_out, v_w_att_out, v_w_o, v_final_g):
    s_len, dm = x.shape[1], x.shape[2]
    ts_big, ts_mid, ts_head, qt = _tile_sizes(s_len)
    xi, yi, cidx = _my_place()
    chip = 2 * xi + yi
    batch = 4 * xi + 2 * yi + cidx
    x2d, tgt = x[0], loss_target[0]
    pos = positions.reshape(s_len, 1)
    wa_l, wi_l, cw_l = w_ada[0], w_in[0], conv_w[0]
    wco_l, wao_l, wo_l = w_conv_out[0], w_att_out[0], w_o[0]
    ada_w = wa_l.shape[1]
    cw_cols = cw_l.shape[1]

    kinds = ("col", "row", "col", "row")
    w_bufs = _cast_weights([wi_l, wco_l, wao_l, wo_l], kinds)

    cw_pad = jnp.pad(cw_l, ((0, HALO - CONV_K), (0, 0)))
    small_in = jnp.concatenate([jnp.broadcast_to(c, (8, dm)), cw_pad.reshape(8, dm)], axis=0)
    small = _allgather_small(small_in).reshape(8, 16, dm)
    c_all = small[:, 0, :]
    conv_w_full = jnp.concatenate(
        [small[2 * p, 8:16, :].reshape(HALO, cw_cols) for p in range(4)], axis=1)
    b_l = lax.dynamic_slice(b_ada, (0, chip * ada_w), (1, ada_w))
    mod_parts = _allgather_small(_mod_part(c_all, wa_l, b_l)).reshape(8, 8, ada_w)
    mod_rows = lax.dynamic_index_in_dim(mod_parts, batch, axis=1, keepdims=False)
    mod = jnp.concatenate([mod_rows[2 * p] for p in range(4)], axis=0).reshape(3, dm)
    modv = jnp.concatenate([mod, jnp.zeros((5, dm), F32)], axis=0)

    sems_a, sems_b, w_bufs, token = _gather_weights_start(w_bufs, kinds, modv)
    rope = _rope_coefficients(pos, _rope_lane_table() + token, ts_big)
    h_b, h_t = _norm_modulate(x2d, modv + token[0, 0], norm_g, ts_big)
    (w_in_b,) = _gather_weights_wait(w_bufs, kinds, sems_a, 0, (rope, h_b), "gather_weights_wait_w_in")
    w_in_b = _pass_to_sibling(w_in_b, kinds[0])

    n_conv, n_gate = 3 * dm // COL_TILE, (ATT_W + 2 * dm) // COL_TILE
    ts_proj = min(4 * ts_big, s_len)
    p_conv = _plain_projection(h_b, w_in_b, 0, n_conv, ts_proj, "conv_projection")
    p_gate = _plain_projection(h_b, w_in_b, n_conv + 9, n_gate, ts_proj, "gate_projection")
    qkv_g = [_qkv_projection(h_b, rope, w_in_b, gi, min(2 * ts_big, s_len)) for gi in range(3)]
    w_co_b, w_ao_b, w_o_b = _gather_weights_wait(w_bufs, kinds, sems_b, 1, (qkv_g[2], p_conv, p_gate),
                                                  "gather_weights_wait_rest")
    y_conv, u0, u1 = _conv_forward(p_conv, conv_w_full, conv_b, conv_ln_g, conv_ln_b, w_co_b, ts_mid)
    qkv_flat = [q.reshape(3, s_len, ATT_W) for q in qkv_g]
    o_g, lse_g = [], []
    for gi, dil in enumerate(DILATIONS):
        o, l = _attention_forward(qkv_flat[gi], s_len // dil, qt[gi], "attention_forward_%d" % dil)
        o_g.append(o.reshape(dil, s_len // dil, ATT_W))
        lse_g.append(l.reshape(dil, s_len // dil, ATT_W))

    (loss_p, dx_res, dyc, dp, da0, da1, da2, ls0, ls1, ls2, de0, de1, de2,
     g_wo, g_wao, head_sums) = _merge_and_head(o_g, lse_g, p_gate, y_conv, x2d, tgt, modv,
                                               final_g.reshape(1, dm), w_ao_b, w_o_b, ts_head)

    dp, g_wco, conv_sums, g_cw = _conv_backward(dp, dyc, p_conv, u0, u1, conv_w_full, conv_ln_g, conv_ln_b,
                                                w_co_b, ts_mid)
    dqkv_g = []
    for gi, (dil, da, ls, de) in enumerate(zip(DILATIONS, (da0, da1, da2), (ls0, ls1, ls2), (de0, de1, de2))):
        flat = lambda a: a.reshape(s_len, ATT_W)
        dqkv = _attention_backward(qkv_flat[gi], flat(da), flat(ls), flat(de), s_len // dil, qt[gi],
                                   "attention_backward_%d" % dil)
        dqkv_g.append(dqkv.reshape(3, dil, s_len // dil, ATT_W))
    dp = _qkv_grad_to_tokens(dp, dqkv_g, rope, ts_big)
    g_win = _w_in_grad(h_t, dp, min(2 * ts_big, s_len))

    grads = [g_win, g_wco, g_wao, g_wo]
    full_shapes = [g.shape for g in grads]
    c_arr = jnp.reshape(cidx, (1,)).astype(jnp.int32)
    recv_halves = _reduce_pair_exchange(grads, kinds)
    halves = [_reduce_pair_sum(g, pa, kind, c_arr, "reduce_pair_sum_%d" % k)
              for k, (g, pa, kind) in enumerate(zip(grads, recv_halves, kinds))]
    send_sems, recv_sems, halves, lands, token = _reduce_to_owner_start(halves, kinds, full_shapes)
    grad_x, in_sums = _input_grad(dp, w_in_b, x2d, dx_res, modv + token[0, 0], norm_g, ts_big)
    halves, recvd = _reduce_to_owner_wait(send_sems, recv_sems, halves, lands, in_sums, kinds)
    gr_win, gr_wco, gr_wao, gr_wo = _reduce_finish(halves, recvd, kinds, full_shapes)

    rows = [in_sums[2:3], conv_sums[2:3], conv_sums[0:1], conv_sums[1:2], head_sums[0:1],
            in_sums[0:1], in_sums[1:2], head_sums[1:2], g_cw, jnp.pad(loss_p, ((0, 0), (0, dm - LANES)))]
    part = jnp.concatenate(rows, axis=0)
    buf = lax.dynamic_update_slice(jnp.zeros((8 * 48, dm), F32), part, (batch * 48, 0))
    row_sems_s, row_sems_r, buf, row_token = _row_gather_start(buf, 48, gr_win)
    upd = {
        "w_in": _adamw(wi_l, gr_win, m_w_in[0], v_w_in[0], "adamw_w_in", (row_token,)),
        "w_co": _adamw(wco_l, gr_wco, m_w_conv_out[0], v_w_conv_out[0], "adamw_w_conv_out", (row_token,)),
        "w_ao": _adamw(wao_l, gr_wao, m_w_att_out[0], v_w_att_out[0], "adamw_w_att_out", (row_token,)),
        "w_o": _adamw(wo_l, gr_wo, m_w_o[0], v_w_o[0], "adamw_w_o", (row_token,)),
    }
    gathered = _row_gather_wait(row_sems_s, row_sems_r, buf, 48, [upd[k][0] for k in ("w_in", "w_co", "w_ao", "w_o")])
    tot = _sum_devices(gathered, 48)
    loss = tot[8 + HALO, 0]
    dmod_all = gathered.reshape(8, 48, dm)[:, 5:8, :].reshape(8, 3 * dm)
    dmod_l = lax.dynamic_slice(dmod_all, (0, chip * ada_w), (8, ada_w))
    gr_wada = _grad_w_ada(c_all.T, dmod_l)
    gr_cw = lax.dynamic_slice(tot[8:8 + HALO], (0, chip * cw_cols), (HALO, cw_cols))

    pad_cw = lambda a: jnp.pad(a[0], ((0, HALO - CONV_K), (0, 0)))
    row = lambda a: a.reshape(1, dm)
    small_upd = _adamw_small(
        [norm_g, conv_b, conv_ln_g, conv_ln_b, row(final_g), b_ada], tot[0:8],
        [m_norm_g, m_conv_b, m_conv_ln_g, m_conv_ln_b, row(m_final_g), m_b_ada],
        [v_norm_g, v_conv_b, v_conv_ln_g, v_conv_ln_b, row(v_final_g), v_b_ada])
    upd["w_ada"] = _adamw(wa_l, gr_wada, m_w_ada[0], v_w_ada[0], "adamw_w_ada")
    upd["conv_w"] = _adamw(cw_pad, gr_cw, pad_cw(m_conv_w), pad_cw(v_conv_w), "adamw_conv_w")

    def family(which):
        if which is None:
            sm = small_upd[0:6]
            big = {"w_ada": gr_wada, "w_in": gr_win, "conv_w": gr_cw, "w_co": gr_wco, "w_ao": gr_wao, "w_o": gr_wo}
        else:
            sm = small_upd[6 * (which + 1):6 * (which + 2)]
            big = {k: upd[k][which] for k in ("w_ada", "w_in", "conv_w", "w_co", "w_ao", "w_o")}
        return [sm[0], big["w_ada"][None], sm[5], big["w_in"][None],
                big["conv_w"][None, :CONV_K], sm[1], sm[2], sm[3], big["w_co"][None],
                big["w_ao"][None], big["w_o"][None], sm[4].reshape(dm)]

    return (loss, grad_x[None], *family(None), *family(0), *family(1), *family(2))
```

```python
import jax
import jax.numpy as jnp
from jax import lax
from jax.experimental import pallas as pl
from jax.experimental.pallas import tpu as pltpu

F32 = jnp.float32
BF16 = jnp.bfloat16
MESH = pl.DeviceIdType.MESH
ANY = pl.BlockSpec(memory_space=pl.ANY)
VM = pl.BlockSpec(memory_space=pltpu.VMEM)

EPS = 1e-6
NEG = -1e30
ATT_W = 512
DILATIONS = (1, 4, 16)
BLK = 128
CONV_K = 31
HALO = 32
CONV_ROWS_FWD = 32
CONV_ROWS_BWD = 16
ROT_DIM = 16
ROPE_THETA = 500000.0
COL_TILE = 512
LANES = 128
VMEM_LIMIT = 56 * 1024 * 1024

ADAM_LR, ADAM_B1, ADAM_B2, ADAM_EPS, ADAM_WD, ADAM_STEP = 0.001, 0.9, 0.999, 1e-08, 0.01, 10


def _params(sem=None, vmem=VMEM_LIMIT):
    return pltpu.CompilerParams(dimension_semantics=sem, vmem_limit_bytes=vmem)


def _dot(a, b):
    return jnp.dot(a, b, preferred_element_type=F32)


def _dot_nt(a, b):
    return lax.dot_general(a, b, (((1,), (1,)), ((), ())), preferred_element_type=F32)


def _dot_tn(a, b):
    return lax.dot_general(a, b, (((0,), (0,)), ((), ())), preferred_element_type=F32)


def _sig(x):
    return jax.nn.sigmoid(x)


def _dsilu(x, s):
    return s * (1.0 + x * (1.0 - s))


def _my_place():
    return lax.axis_index("x"), lax.axis_index("y"), lax.axis_index("c")


def _allgather_small(x_shard):
    m_per, n = x_shard.shape

    def body(x_ref, out_ref, send_sems, recv_sems, local_sem):
        x, y, c = _my_place()
        me, sibling = (x, y, c), (x, y, 1 - c)
        chips = [(1 - x, y), (x, 1 - y), (1 - x, 1 - y)]

        def rows(px, py, pc):
            return out_ref.at[pl.ds((4 * px + 2 * py + pc) * m_per, m_per), :]

        def copy(k, block, to, src=None):
            return pltpu.make_async_remote_copy(
                src_ref=rows(*block) if src is None else src, dst_ref=rows(*block),
                send_sem=send_sems.at[k], recv_sem=recv_sems.at[k],
                device_id=to, device_id_type=MESH)

        mine = pltpu.make_async_copy(x_ref, rows(*me), local_sem)
        mine.start()
        first = [copy(0, me, sibling, src=x_ref)]
        first += [copy(1 + j, me, (*chip, c), src=x_ref) for j, chip in enumerate(chips)]
        for cp in first:
            cp.start()
        passed = [copy(4 + j, (*chip, c), sibling) for j, chip in enumerate(chips)]
        for j, chip in enumerate(chips):
            copy(1 + j, (*chip, c), me).wait_recv()
            passed[j].start()
        copy(0, sibling, me).wait_recv()
        for j, chip in enumerate(chips):
            copy(4 + j, (*chip, 1 - c), me).wait_recv()
        for cp in first + passed:
            cp.wait_send()
        mine.wait()

    return pl.pallas_call(
        body, name="allgather_small",
        out_shape=jax.ShapeDtypeStruct((8 * m_per, n), x_shard.dtype),
        in_specs=[VM], out_specs=VM,
        scratch_shapes=[pltpu.SemaphoreType.DMA((7,)), pltpu.SemaphoreType.DMA((7,)),
                        pltpu.SemaphoreType.DMA],
    )(x_shard)


def _shard_window(ref, kind, p, n_shards=4):
    r, c = ref.shape
    if kind == "col":
        w = c // n_shards
        return ref.at[:, pl.ds(p * w, w)]
    w = r // n_shards
    return ref.at[pl.ds(p * w, w), :]


def _half_window(ref, kind, hc):
    r, c = ref.shape
    if kind == "col":
        return ref.at[pl.ds(hc * (r // 2), r // 2), :]
    return ref.at[:, pl.ds(hc * (c // 2), c // 2)]


def _landed(ref, kind, chip, hc):
    return _half_window(_shard_window(ref, kind, 2 * chip[0] + chip[1]), kind, hc)


def _cast_weights(shards, kinds):
    n = len(shards)
    full_shapes = [(s.shape[0], 4 * s.shape[1]) if kind == "col" else (4 * s.shape[0], s.shape[1])
                   for s, kind in zip(shards, kinds)]

    def body(*refs):
        w_refs, out_refs, bf_refs, sems = refs[:n], refs[n:2 * n], refs[2 * n:3 * n], refs[3 * n]
        x, y, _ = _my_place()
        cps = []
        for k in range(n):
            bf_refs[k][...] = w_refs[k][...].astype(BF16)
            cp = pltpu.make_async_copy(bf_refs[k], _shard_window(out_refs[k], kinds[k], 2 * x + y), sems.at[k])
            cp.start()
            cps.append(cp)
        for cp in cps:
            cp.wait()

    return pl.pallas_call(
        body, name="cast_weights",
        out_shape=[jax.ShapeDtypeStruct(s, BF16) for s in full_shapes],
        in_specs=[VM] * n, out_specs=[ANY] * n,
        scratch_shapes=[pltpu.VMEM(s.shape, BF16) for s in shards] + [pltpu.SemaphoreType.DMA((n,))],
        compiler_params=_params(),
    )(*shards)


def _gather_copies(refs, kinds, sems_a, sems_b):
    x, y, c = _my_place()
    chips = [(1 - x, y), (x, 1 - y), (1 - x, 1 - y)]
    set_a, set_b = [], []
    for j, chip in enumerate(chips):
        for k in range(len(refs)):
            if refs[k] is None:
                continue
            for flip in ((0,) if k == 0 else (0, 1)):
                win = _landed(refs[k], kinds[k], (x, y), c)
                sems, idx = (sems_a, j) if k == 0 else (sems_b, ((k - 1) * 3 + j) * 2 + flip)
                mine = _landed(refs[k], kinds[k], chip, (c + flip) % 2)
                (set_a if k == 0 else set_b).append((
                    pltpu.make_async_remote_copy(
                        src_ref=win, dst_ref=win, send_sem=sems[0].at[idx], recv_sem=sems[1].at[idx],
                        device_id=(*chip, (c + flip) % 2), device_id_type=MESH),
                    pltpu.make_async_remote_copy(
                        src_ref=mine, dst_ref=mine, send_sem=sems[0].at[idx], recv_sem=sems[1].at[idx],
                        device_id=(*chip, (c + flip) % 2), device_id_type=MESH)))
    return set_a, set_b


_SPLIT = dict(has_side_effects=pltpu.SideEffectType.DATAFLOW_SIDE_EFFECTING)


def _gather_weights_start(fulls, kinds, after):
    n = len(fulls)
    hbm = pl.BlockSpec(memory_space=pltpu.HBM)
    sem = pl.BlockSpec(memory_space=pltpu.SEMAPHORE)
    nb = (n - 1) * 6

    def body(*refs):
        in_refs = refs[:n]
        sa, ra, sb, rb = refs[n + 1:n + 5]
        token = refs[-1]
        set_a, set_b = _gather_copies(in_refs, kinds, (sa, ra), (sb, rb))
        for out_cp, _ in set_a + set_b:
            out_cp.start()
        token[...] = jnp.zeros_like(token)

    out = pl.pallas_call(
        body, name="gather_weights_start",
        out_shape=[pltpu.SemaphoreType.DMA((3,)), pltpu.SemaphoreType.DMA((3,)),
                   pltpu.SemaphoreType.DMA((nb,)), pltpu.SemaphoreType.DMA((nb,))]
        + [pltpu.HBM(f.shape, f.dtype) for f in fulls] + [jax.ShapeDtypeStruct((8, LANES), F32)],
        in_specs=[hbm] * n + [ANY], out_specs=[sem] * 4 + [hbm] * n + [VM],
        input_output_aliases={k: 4 + k for k in range(n)},
        compiler_params=pltpu.CompilerParams(**_SPLIT),
    )(*[pltpu.with_memory_space_constraint(f, pltpu.HBM) for f in fulls], after)
    return (out[0], out[1]), (out[2], out[3]), out[4:4 + n], out[-1]


def _gather_weights_wait(fulls, kinds, sems, which, after, name):
    n = len(fulls)
    hbm = pl.BlockSpec(memory_space=pltpu.HBM)
    sem = pl.BlockSpec(memory_space=pltpu.SEMAPHORE)
    keep = [0] if which == 0 else list(range(1, n))

    def body(*refs):
        m = len(keep)
        full_refs = [None] * n
        for pos_, k in enumerate(keep):
            full_refs[k] = refs[pos_]
        s_ref, r_ref = refs[m:m + 2]
        if which == 0:
            sets = _gather_copies([full_refs[0]], kinds[:1], (s_ref, r_ref), None)[0]
        else:
            sets = _gather_copies([None] + [full_refs[k] for k in keep], kinds, None, (s_ref, r_ref))[1]
        for out_cp, in_cp in sets:
            out_cp.wait_send()
            in_cp.wait_recv()

    out = pl.pallas_call(
        body, name=name,
        out_shape=[pltpu.HBM(fulls[k].shape, fulls[k].dtype) for k in keep],
        in_specs=[hbm] * len(keep) + [sem, sem] + [ANY] * len(after), out_specs=[hbm] * len(keep),
        input_output_aliases={i: i for i in range(len(keep))},
        compiler_params=pltpu.CompilerParams(**_SPLIT),
    )(*[fulls[k] for k in keep], sems[0], sems[1], *after)
    return list(out)


def _row_gather_copies(buf, m_per, send_sems, recv_sems):
    x, y, c = _my_place()
    pairs = []
    for idx in range(1, 8):
        dx, dy, dc = idx // 4, (idx // 2) % 2, idx % 2
        peer = ((x + dx) % 2, (y + dy) % 2, (c + dc) % 2)
        mine = buf.at[pl.ds((4 * x + 2 * y + c) * m_per, m_per), :]
        theirs = buf.at[pl.ds((4 * peer[0] + 2 * peer[1] + peer[2]) * m_per, m_per), :]
        pairs.append(tuple(
            pltpu.make_async_remote_copy(src_ref=w, dst_ref=w, send_sem=send_sems.at[idx - 1],
                                         recv_sem=recv_sems.at[idx - 1], device_id=peer, device_id_type=MESH)
            for w in (mine, theirs)))
    return pairs


def _row_gather_start(buf, m_per, after):
    hbm = pl.BlockSpec(memory_space=pltpu.HBM)
    sem = pl.BlockSpec(memory_space=pltpu.SEMAPHORE)

    def body(buf_ref, after_ref, send_sems, recv_sems, out_ref, token):
        del after_ref, out_ref
        for out_cp, _ in _row_gather_copies(buf_ref, m_per, send_sems, recv_sems):
            out_cp.start()
        token[...] = jnp.zeros_like(token)

    return pl.pallas_call(
        body, name="row_gather_start",
        out_shape=[pltpu.SemaphoreType.DMA((7,)), pltpu.SemaphoreType.DMA((7,)), pltpu.HBM(buf.shape, buf.dtype),
                   jax.ShapeDtypeStruct((8, LANES), F32)],
        in_specs=[hbm, ANY], out_specs=[sem, sem, hbm, VM], input_output_aliases={0: 2},
        compiler_params=pltpu.CompilerParams(**_SPLIT),
    )(pltpu.with_memory_space_constraint(buf, pltpu.HBM), after)


def _row_gather_wait(send_sems, recv_sems, buf, m_per, after):
    hbm = pl.BlockSpec(memory_space=pltpu.HBM)
    sem = pl.BlockSpec(memory_space=pltpu.SEMAPHORE)

    def body(buf_ref, send_s, recv_s, *rest):
        for out_cp, in_cp in _row_gather_copies(buf_ref, m_per, send_s, recv_s):
            out_cp.wait_send()
            in_cp.wait_recv()

    return pl.pallas_call(
        body, name="row_gather_wait", out_shape=pltpu.HBM(buf.shape, buf.dtype),
        in_specs=[hbm, sem, sem] + [ANY] * len(after), out_specs=hbm, input_output_aliases={0: 0},
        compiler_params=pltpu.CompilerParams(**_SPLIT),
    )(buf, send_sems, recv_sems, *after)


def _pass_to_sibling(w_full, kind):
    def body(w_in_ref, w_ref, send_sems, recv_sems):
        del w_in_ref
        x, y, c = _my_place()
        chips = [(1 - x, y), (x, 1 - y), (1 - x, 1 - y)]
        cps = []
        for j, chip in enumerate(chips):
            win = _landed(w_ref, kind, chip, c)
            cp = pltpu.make_async_remote_copy(
                src_ref=win, dst_ref=win, send_sem=send_sems.at[j], recv_sem=recv_sems.at[j],
                device_id=(x, y, 1 - c), device_id_type=MESH)
            cp.start()
            cps.append(cp)
        for j, chip in enumerate(chips):
            theirs = _landed(w_ref, kind, chip, 1 - c)
            pltpu.make_async_remote_copy(
                src_ref=theirs, dst_ref=theirs, send_sem=send_sems.at[j], recv_sem=recv_sems.at[j],
                device_id=(x, y, 1 - c), device_id_type=MESH).wait_recv()
        for cp in cps:
            cp.wait_send()

    return pl.pallas_call(
        body, name="pass_to_sibling",
        out_shape=jax.ShapeDtypeStruct(w_full.shape, w_full.dtype),
        in_specs=[ANY], out_specs=ANY, input_output_aliases={0: 0},
        scratch_shapes=[pltpu.SemaphoreType.DMA((3,)), pltpu.SemaphoreType.DMA((3,))],
    )(w_full)


def _reduce_pair_exchange(grads, kinds):
    n = len(grads)
    half_shapes = [(g.shape[0] // 2, g.shape[1]) if kind == "col" else (g.shape[0], g.shape[1] // 2)
                   for g, kind in zip(grads, kinds)]

    def body(*refs):
        g_refs, pa_refs = refs[:n], refs[n:2 * n]
        send_sems, recv_sems = refs[2 * n:]
        x, y, c = _my_place()
        cps = []
        for k in range(n):
            cp = pltpu.make_async_remote_copy(
                src_ref=_half_window(g_refs[k], kinds[k], 1 - c), dst_ref=pa_refs[k],
                send_sem=send_sems.at[k], recv_sem=recv_sems.at[k],
                device_id=(x, y, 1 - c), device_id_type=MESH)
            cp.start()
            cps.append(cp)
        for cp in cps:
            cp.wait()

    return pl.pallas_call(
        body, name="reduce_pair_exchange",
        out_shape=[jax.ShapeDtypeStruct(s, F32) for s in half_shapes],
        in_specs=[ANY] * n, out_specs=[ANY] * n,
        scratch_shapes=[pltpu.SemaphoreType.DMA((n,)), pltpu.SemaphoreType.DMA((n,))],
    )(*grads)


def _row_tile(rows, cols, itemsize=4, target=2 * 1024 * 1024, mult=16):
    t = rows
    while t % 2 == 0 and t // 2 >= mult and (t // 2) % mult == 0 and t * cols * itemsize > target:
        t //= 2
    return t


def _reduce_pair_sum(g, pa, kind, c_arr, name):
    hr, hc_ = pa.shape
    tr = _row_tile(hr, hc_)
    nb = hr // tr

    def body(c_ref, g_ref, pa_ref, o_ref):
        o_ref[...] = (g_ref[...] + pa_ref[...]).astype(BF16)

    if kind == "col":
        g_map = lambda i, c_ref: (c_ref[0] * nb + i, 0)
    else:
        g_map = lambda i, c_ref: (i, c_ref[0])
    return pl.pallas_call(
        body, name=name,
        grid_spec=pltpu.PrefetchScalarGridSpec(
            num_scalar_prefetch=1, grid=(nb,),
            in_specs=[pl.BlockSpec((tr, hc_), g_map), pl.BlockSpec((tr, hc_), lambda i, c_ref: (i, 0))],
            out_specs=pl.BlockSpec((tr, hc_), lambda i, c_ref: (i, 0))),
        out_shape=jax.ShapeDtypeStruct((hr, hc_), BF16),
        compiler_params=_params(("parallel",)),
    )(c_arr, g, pa)


def _half_shard_shape(full_shape, kind):
    r, c = full_shape
    return (r // 2, c // 4) if kind == "col" else (r // 4, c // 2)


def _to_owner_copies(h_refs, land_refs, send_sems, recv_sems, kinds):
    n = len(h_refs)
    x, y, c = _my_place()
    chips = [(1 - x, y), (x, 1 - y), (1 - x, 1 - y)]
    cps = []
    for j, chip in enumerate(chips):
        pj = 2 * chip[0] + chip[1]
        for k in range(n):
            cps.append(pltpu.make_async_remote_copy(
                src_ref=_shard_window(h_refs[k], kinds[k], pj), dst_ref=land_refs[k].at[j],
                send_sem=send_sems.at[j * n + k], recv_sem=recv_sems.at[j * n + k],
                device_id=(*chip, c), device_id_type=MESH))
    return cps


def _reduce_to_owner_start(halves, kinds, full_shapes):
    n = len(halves)
    hs = [_half_shard_shape(fs, kind) for fs, kind in zip(full_shapes, kinds)]
    hbm = pl.BlockSpec(memory_space=pltpu.HBM)
    sem = pl.BlockSpec(memory_space=pltpu.SEMAPHORE)

    def body(*refs):
        h_refs, land_refs = refs[:n], refs[n:2 * n]
        send_sems, recv_sems = refs[2 * n:2 * n + 2]
        token = refs[-1]
        for cp in _to_owner_copies(h_refs, land_refs, send_sems, recv_sems, kinds):
            cp.start()
        token[...] = jnp.zeros_like(token)

    lands = [pltpu.with_memory_space_constraint(lax.empty((3,) + s, BF16), pltpu.HBM) for s in hs]
    out = pl.pallas_call(
        body, name="reduce_to_owner_start",
        out_shape=[pltpu.SemaphoreType.DMA((3 * n,)), pltpu.SemaphoreType.DMA((3 * n,))]
        + [pltpu.HBM(h.shape, h.dtype) for h in halves] + [pltpu.HBM((3,) + s, BF16) for s in hs]
        + [jax.ShapeDtypeStruct((8, LANES), F32)],
        in_specs=[hbm] * (2 * n), out_specs=[sem, sem] + [hbm] * (2 * n) + [VM],
        input_output_aliases={k: 2 + k for k in range(2 * n)},
        compiler_params=pltpu.CompilerParams(has_side_effects=pltpu.SideEffectType.DATAFLOW_SIDE_EFFECTING),
    )(*[pltpu.with_memory_space_constraint(h, pltpu.HBM) for h in halves], *lands)
    return out[0], out[1], out[2:2 + n], out[2 + n:2 + 2 * n], out[-1]


def _reduce_to_owner_wait(send_sems, recv_sems, halves, lands, after, kinds):
    n = len(halves)
    hbm = pl.BlockSpec(memory_space=pltpu.HBM)
    sem = pl.BlockSpec(memory_space=pltpu.SEMAPHORE)

    def body(*refs):
        h_refs, land_refs = refs[:n], refs[n:2 * n]
        send_s, recv_s = refs[2 * n:2 * n + 2]
        for cp in _to_owner_copies(h_refs, land_refs, send_s, recv_s, kinds):
            cp.wait_send()
            cp.wait_recv()

    out = pl.pallas_call(
        body, name="reduce_to_owner_wait",
        out_shape=[pltpu.HBM(h.shape, h.dtype) for h in halves] + [pltpu.HBM(l.shape, l.dtype) for l in lands],
        in_specs=[hbm] * (2 * n) + [sem, sem, ANY], out_specs=[hbm] * (2 * n),
        input_output_aliases={k: k for k in range(2 * n)},
        compiler_params=pltpu.CompilerParams(has_side_effects=pltpu.SideEffectType.DATAFLOW_SIDE_EFFECTING),
    )(*halves, *lands, send_sems, recv_sems, after)
    return out[:n], out[n:]


def _reduce_finish(halves, recvd, kinds, full_shapes):
    n = len(halves)
    hs = [_half_shard_shape(fs, kind) for fs, kind in zip(full_shapes, kinds)]
    shard_shapes = [(fs[0], fs[1] // 4) if kind == "col" else (fs[0] // 4, fs[1])
                    for fs, kind in zip(full_shapes, kinds)]

    def body(*refs):
        h_refs, rc_refs, gs_refs = refs[:n], refs[n:2 * n], refs[2 * n:3 * n]
        own_refs, gh_refs = refs[3 * n:4 * n], refs[4 * n:5 * n]
        in_sems, loc_sems, send_sems, recv_sems = refs[5 * n:]
        x, y, c = _my_place()
        p = 2 * x + y
        loads = []
        for k in range(n):
            cp = pltpu.make_async_copy(_shard_window(h_refs[k], kinds[k], p), own_refs[k], in_sems.at[k])
            cp.start()
            loads.append(cp)
        outs = []
        for k in range(n):
            loads[k].wait()
            gh_refs[k][...] = (own_refs[k][...].astype(F32) + rc_refs[k][0].astype(F32)
                               + rc_refs[k][1].astype(F32) + rc_refs[k][2].astype(F32))
            dst = _half_window(gs_refs[k], kinds[k], c)
            lc = pltpu.make_async_copy(gh_refs[k], dst, loc_sems.at[k])
            lc.start()
            rc = pltpu.make_async_remote_copy(
                src_ref=gh_refs[k], dst_ref=dst, send_sem=send_sems.at[k], recv_sem=recv_sems.at[k],
                device_id=(x, y, 1 - c), device_id_type=MESH)
            rc.start()
            outs.append((lc, rc))
        for k, (lc, rc) in enumerate(outs):
            lc.wait()
            rc.wait_send()
            pltpu.make_async_remote_copy(
                src_ref=gh_refs[k], dst_ref=_half_window(gs_refs[k], kinds[k], 1 - c),
                send_sem=send_sems.at[k], recv_sem=recv_sems.at[k],
                device_id=(x, y, 1 - c), device_id_type=MESH).wait_recv()

    return pl.pallas_call(
        body, name="reduce_finish",
        out_shape=[jax.ShapeDtypeStruct(s, F32) for s in shard_shapes],
        in_specs=[ANY] * n + [VM] * n, out_specs=[ANY] * n,
        scratch_shapes=[pltpu.VMEM(s, BF16) for s in hs] + [pltpu.VMEM(s, F32) for s in hs]
        + [pltpu.SemaphoreType.DMA((n,)) for _ in range(4)],
        compiler_params=_params(),
    )(*halves, *recvd)


def _mod_part(c_all, w_ada_l, b_l):
    def body(c_ref, w_ref, b_ref, o_ref):
        o_ref[...] = _dot(c_ref[...].astype(BF16), w_ref[...].astype(BF16)) + b_ref[...]

    return pl.pallas_call(
        body, name="mod_part", out_shape=jax.ShapeDtypeStruct((8, w_ada_l.shape[1]), F32),
        in_specs=[VM, VM, VM], out_specs=VM, compiler_params=_params(),
    )(c_all, w_ada_l, b_l)


def _sum_devices(parts, m):
    def body(p_ref, o_ref):
        acc = p_ref[0:m, :]
        for d in range(1, 8):
            acc = acc + p_ref[d * m:(d + 1) * m, :]
        o_ref[...] = acc

    return pl.pallas_call(
        body, name="sum_devices", out_shape=jax.ShapeDtypeStruct((m, parts.shape[1]), F32),
        in_specs=[VM], out_specs=VM, compiler_params=_params(),
    )(parts)


def _grad_w_ada(c_all_t, dmod_l):
    d, w = c_all_t.shape[0], dmod_l.shape[1]

    def body(ct_ref, dm_ref, o_ref):
        acc = ct_ref[:, 0:1] * dm_ref[0:1, :]
        for b in range(1, 8):
            acc = acc + ct_ref[:, b:b + 1] * dm_ref[b:b + 1, :]
        o_ref[...] = acc

    return pl.pallas_call(
        body, name="grad_w_ada", out_shape=jax.ShapeDtypeStruct((d, w), F32),
        in_specs=[VM, VM], out_specs=VM, compiler_params=_params(),
    )(c_all_t, dmod_l)


def _adamw_math(w, g, m, v):
    nm = ADAM_B1 * m + (1.0 - ADAM_B1) * g
    nv = ADAM_B2 * v + (1.0 - ADAM_B2) * (g * g)
    m_hat = nm / (1.0 - ADAM_B1 ** ADAM_STEP)
    v_hat = nv / (1.0 - ADAM_B2 ** ADAM_STEP)
    return -ADAM_LR * (m_hat / (jnp.sqrt(v_hat) + ADAM_EPS) + ADAM_WD * w), nm, nv


def _adamw_small(ws, g8, ms, vs):
    shapes = [jax.ShapeDtypeStruct(w.shape, F32) for w in ws]

    def body(*refs):
        w_refs, g_ref, m_refs, v_refs, outs = refs[0:6], refs[6], refs[7:13], refs[13:19], refs[19:]
        for idx in range(6):
            if idx < 5:
                g = g_ref[idx:idx + 1, :]
            else:
                g = jnp.concatenate([g_ref[5:6, :], g_ref[6:7, :], g_ref[7:8, :]], axis=1)
            res = (g,) + _adamw_math(w_refs[idx][...], g, m_refs[idx][...], v_refs[idx][...])
            for fam in range(4):
                outs[6 * fam + idx][...] = res[fam]

    return pl.pallas_call(
        body, name="adamw_small", out_shape=shapes * 4,
        in_specs=[VM] * 19, out_specs=[VM] * 24, compiler_params=_params(),
    )(*ws, g8, *ms, *vs)


def _adamw(w, g, m, v, name, after=()):
    r, c = w.shape
    tr = _row_tile(r, c, target=1024 * 1024, mult=8)

    def body(w_ref, g_ref, m_ref, v_ref, *rest):
        d_ref, nm_ref, nv_ref = rest[len(after):]
        d_ref[...], nm_ref[...], nv_ref[...] = _adamw_math(w_ref[...], g_ref[...], m_ref[...], v_ref[...])

    spec = pl.BlockSpec((tr, c), lambda i: (i, 0))
    return pl.pallas_call(
        body, name=name, grid=(r // tr,),
        out_shape=[jax.ShapeDtypeStruct((r, c), F32)] * 3,
        in_specs=[spec] * 4 + [ANY] * len(after), out_specs=[spec] * 3,
        compiler_params=_params(("parallel",)),
    )(w, g, m, v, *after)


def _rope_coefficients(pos, ropetab, ts):
    s_len = pos.shape[0]

    def body(pos_ref, tab_ref, o_ref):
        ang = pos_ref[...].astype(F32) * tab_ref[0:1, :]
        cs, sn = jnp.cos(ang), jnp.sin(ang)
        o_ref[0] = jnp.where(tab_ref[3:4, :] > 0, cs, 1.0)
        o_ref[1] = -sn * tab_ref[1:2, :]
        o_ref[2] = sn * tab_ref[2:3, :]

    return pl.pallas_call(
        body, name="rope_coefficients", grid=(s_len // ts,),
        in_specs=[pl.BlockSpec((ts, 1), lambda i: (i, 0)), pl.BlockSpec((8, LANES), lambda i: (0, 0))],
        out_specs=pl.BlockSpec((3, ts, LANES), lambda i: (0, i, 0)),
        out_shape=jax.ShapeDtypeStruct((3, s_len, LANES), F32),
        compiler_params=_params(("parallel",)),
    )(pos, ropetab)


def _deinterleave_store(vals, slab, out_ref, d, ts, dtype):
    if d == 1:
        for s in range(4):
            out_ref[0, :, s * LANES:(s + 1) * LANES] = vals[s].astype(dtype)
        return
    for s in range(4):
        slab[s] = vals[s]
    if d == 16:
        q = ts // 4
        for s in range(4):
            for r1 in range(4):
                slab[4 + s, r1 * q:(r1 + 1) * q, :] = slab[s, pl.ds(r1, q, stride=4), :]
        for r in range(d):
            r1, r2 = r % 4, r // 4
            for s in range(4):
                out_ref[r, :, s * LANES:(s + 1) * LANES] = slab[
                    4 + s, pl.ds(r1 * q + r2, ts // d, stride=4), :].astype(dtype)
        return
    for r in range(d):
        for s in range(4):
            out_ref[r, :, s * LANES:(s + 1) * LANES] = slab[s, pl.ds(r, ts // d, stride=d), :].astype(dtype)


def _interleave_load(blk_ref, slab, d, ts):
    if d == 1:
        return [blk_ref[0, :, s * LANES:(s + 1) * LANES] for s in range(4)]
    if d == 16:
        q = ts // 4
        for r in range(d):
            r1, r2 = r % 4, r // 4
            for s in range(4):
                slab[4 + s, pl.ds(r1 * q + r2, ts // d, stride=4), :] = blk_ref[r, :, s * LANES:(s + 1) * LANES]
        for s in range(4):
            for r1 in range(4):
                slab[s, pl.ds(r1, q, stride=4), :] = slab[4 + s, r1 * q:(r1 + 1) * q, :]
        return [slab[s] for s in range(4)]
    for r in range(d):
        for s in range(4):
            slab[s, pl.ds(r, ts // d, stride=d), :] = blk_ref[r, :, s * LANES:(s + 1) * LANES]
    return [slab[s] for s in range(4)]


def _norm_modulate(x, modv, norm_g, ts):
    s_len, d_model = x.shape

    def body(x_ref, mod_ref, g_ref, h_ref, ht_ref):
        xv = x_ref[...]
        r = lax.rsqrt(jnp.mean(xv * xv, axis=-1, keepdims=True) + EPS)
        h = (xv * r) * g_ref[...] * (1.0 + mod_ref[1:2, :]) + mod_ref[0:1, :]
        h_ref[...] = h.astype(BF16)
        ht_ref[...] = h.T.astype(BF16)

    return pl.pallas_call(
        body, name="norm_modulate", grid=(s_len // ts,),
        in_specs=[pl.BlockSpec((ts, d_model), lambda i: (i, 0)), pl.BlockSpec((8, d_model), lambda i: (0, 0)),
                  pl.BlockSpec((1, d_model), lambda i: (0, 0))],
        out_specs=[pl.BlockSpec((ts, d_model), lambda i: (i, 0)), pl.BlockSpec((d_model, ts), lambda i: (0, i))],
        out_shape=[jax.ShapeDtypeStruct((s_len, d_model), BF16), jax.ShapeDtypeStruct((d_model, s_len), BF16)],
        compiler_params=_params(("parallel",)),
    )(x, modv, norm_g)


def _plain_projection(h, w_in, first_tile, n_tiles, ts, name):
    s_len, d_model = h.shape
    ct = COL_TILE

    def body(h_ref, w_ref, o_ref):
        o_ref[...] = _dot(h_ref[...], w_ref[...])

    return pl.pallas_call(
        body, name=name, grid=(s_len // ts, n_tiles),
        in_specs=[pl.BlockSpec((ts, d_model), lambda i, j: (i, 0)),
                  pl.BlockSpec((d_model, ct), lambda i, j: (0, first_tile + j))],
        out_specs=pl.BlockSpec((ts, ct), lambda i, j: (i, j)),
        out_shape=jax.ShapeDtypeStruct((s_len, n_tiles * ct), F32),
        compiler_params=_params(("parallel", "arbitrary")),
    )(h, w_in)


def _qkv_projection(h, rope, w_in, gi, ts):
    s_len, d_model = h.shape
    dil = DILATIONS[gi]
    nc = 3 * d_model // COL_TILE
    hr = ts // 2
    assert COL_TILE == ATT_W and hr % (16 * dil) == 0

    def body(h_ref, rope_ref, wq_ref, wk_ref, wv_ref, o_ref, slab):
        rc, ra, rb = rope_ref.at[0], rope_ref.at[1], rope_ref.at[2]
        w_refs = (wq_ref, wk_ref, wv_ref)
        units = [(w, half) for w in range(3) for half in range(2)]

        def matmul(w, half):
            return _dot(h_ref[half * hr:(half + 1) * hr, :], w_refs[w][...])

        def finish(w, half, res):
            rows = slice(half * hr, (half + 1) * hr)
            vals = []
            for s in range(4):
                t = res[:, s * LANES:(s + 1) * LANES]
                if w < 2:
                    t = (t * rc[rows, :] + pltpu.roll(t, LANES - 8, 1) * ra[rows, :]
                         + pltpu.roll(t, 8, 1) * rb[rows, :])
                vals.append(t)
            out = o_ref.at[w, :, half * (hr // dil):(half + 1) * (hr // dil), :]
            _deinterleave_store(vals, slab.at[half], out, dil, hr, BF16)

        res_next = matmul(*units[0])
        for ui, (w, half) in enumerate(units):
            res = res_next
            if ui + 1 < len(units):
                res_next = matmul(*units[ui + 1])
            finish(w, half, res)

    def w_spec(w):
        return pl.BlockSpec((d_model, COL_TILE), lambda i: (0, nc + 3 * w + gi))

    return pl.pallas_call(
        body, name="qkv_projection_%d" % dil, grid=(s_len // ts,),
        in_specs=[pl.BlockSpec((ts, d_model), lambda i: (i, 0)), pl.BlockSpec((3, ts, LANES), lambda i: (0, i, 0)),
                  w_spec(0), w_spec(1), w_spec(2)],
        out_specs=pl.BlockSpec((3, dil, ts // dil, ATT_W), lambda i: (0, 0, i, 0)),
        out_shape=jax.ShapeDtypeStruct((3, dil, s_len // dil, ATT_W), BF16),
        scratch_shapes=[pltpu.VMEM((2, 8, hr, LANES), F32)],
        compiler_params=_params(("parallel",)),
    )(h, rope, w_in, w_in, w_in)


def _layernorm_stats(u1):
    mu = jnp.mean(u1, axis=-1, keepdims=True)
    xc = u1 - mu
    rstd = lax.rsqrt(jnp.mean(xc * xc, axis=-1, keepdims=True) + EPS)
    return xc * rstd, rstd


def _shifted_copies(win, shf, ts):
    rows = ts + HALO - 8
    for b in range(1, 8):
        shf[b - 1, 0:rows, :] = win[pl.ds(b, rows), :]


def _tap(win, shf, off, r0, rows):
    a, b = divmod(off, 8)
    start = 8 * a + r0
    if b == 0:
        return win[start:start + rows, :]
    return shf[b - 1, start:start + rows, :]


def _conv_forward(p_conv, conv_w, conv_b, ln_g, ln_b, w_co, ts):
    s_len, d3 = p_conv.shape
    dm = d3 // 3

    def body(p_ref, cw_ref, cb_ref, g_ref, b_ref, w_ref, y_ref, u0_ref, xh_ref, rstd_ref, win, shf, u1_ref):
        i = pl.program_id(0)

        @pl.when(i == 0)
        def _():
            win[0:HALO, :] = jnp.zeros((HALO, dm), F32)

        a, b, z = p_ref[:, 0:dm], p_ref[:, dm:2 * dm], p_ref[:, 2 * dm:3 * dm]
        u0 = a * _sig(b)
        win[HALO:HALO + ts, :] = u0
        u0_ref[...] = u0
        _shifted_copies(win, shf, ts)
        for r0 in range(0, ts, CONV_ROWS_FWD):
            acc = jnp.broadcast_to(cb_ref[...], (CONV_ROWS_FWD, dm))
            for k in range(CONV_K):
                acc = acc + cw_ref[k:k + 1, :] * _tap(win, shf, HALO - (CONV_K - 1) + k, r0, CONV_ROWS_FWD)
            u1_ref[r0:r0 + CONV_ROWS_FWD, :] = acc
        xh, rstd = _layernorm_stats(u1_ref[...])
        xh_ref[...] = xh
        rstd_ref[...] = rstd
        u2 = xh * g_ref[...] + b_ref[...]
        a_conv = (u2 * _sig(u2)) * (z * _sig(z))
        y_ref[...] = _dot(a_conv.astype(BF16), w_ref[...])
        win[0:HALO, :] = win[ts:ts + HALO, :]

    row = pl.BlockSpec((1, dm), lambda i: (0, 0))
    tile = pl.BlockSpec((ts, dm), lambda i: (i, 0))
    return pl.pallas_call(
        body, name="conv_forward", grid=(s_len // ts,),
        in_specs=[pl.BlockSpec((ts, d3), lambda i: (i, 0)),
                  pl.BlockSpec((HALO, dm), lambda i: (0, 0)), row, row, row,
                  pl.BlockSpec((dm, dm), lambda i: (0, 0))],
        out_specs=[tile, tile, tile, pl.BlockSpec((ts, 1), lambda i: (i, 0))],
        out_shape=[jax.ShapeDtypeStruct((s_len, dm), F32)] * 3 + [jax.ShapeDtypeStruct((s_len, 1), F32)],
        scratch_shapes=[pltpu.VMEM((ts + HALO, dm), F32), pltpu.VMEM((7, ts + HALO - 8, dm), F32),
                        pltpu.VMEM((ts, dm), F32)],
        compiler_params=_params(("arbitrary",)),
    )(p_conv, conv_w, conv_b, ln_g, ln_b, w_co)


def _conv_backward(dp, dyc, p_conv, u0, xh, rstd, conv_w, ln_g, ln_b, w_co, ts):
    s_len, d3 = p_conv.shape
    dm = d3 // 3
    nt = s_len // ts
    hb = ts // HALO

    def body(dp_in, dy_ref, p_ref, u0_ref, uh_ref, xh_ref, rstd_ref, cw_ref, g_ref, b_ref, w_ref,
             dp_ref, gw_ref, gs_ref, gcw_ref, dwin, uwin, shf):
        del dp_in
        i = pl.program_id(0)
        ti = nt - 1 - i

        @pl.when(i == 0)
        def _():
            gw_ref[...] = jnp.zeros_like(gw_ref)
            gs_ref[...] = jnp.zeros_like(gs_ref)
            gcw_ref[...] = jnp.zeros_like(gcw_ref)
            dwin[ts:ts + HALO, :] = jnp.zeros((HALO, dm), F32)

        dy = dy_ref[...]
        z = p_ref[:, 2 * dm:3 * dm]
        d_ac = _dot_nt(dy, w_ref[...])
        xh, rstd = xh_ref[...], rstd_ref[...]
        u2 = xh * g_ref[...] + b_ref[...]
        sg2, sgz = _sig(u2), _sig(z)
        u3, sz = u2 * sg2, z * sgz
        gw_ref[...] += _dot_tn((u3 * sz).astype(BF16), dy)
        d_z = d_ac * u3 * _dsilu(z, sgz)
        d_u2 = d_ac * sz * _dsilu(u2, sg2)
        gs_ref[0:1, :] += jnp.sum(d_u2 * xh, axis=0, keepdims=True)
        gs_ref[1:2, :] += jnp.sum(d_u2, axis=0, keepdims=True)
        dxh = d_u2 * g_ref[...]
        d_u1 = rstd * (dxh - jnp.mean(dxh, axis=-1, keepdims=True)
                       - xh * jnp.mean(dxh * xh, axis=-1, keepdims=True))
        gs_ref[2:3, :] += jnp.sum(d_u1, axis=0, keepdims=True)
        dwin[0:ts, :] = d_u1
        uwin[0:HALO, :] = jnp.where(ti == 0, 0.0, uh_ref[...])
        uwin[HALO:HALO + ts, :] = u0_ref[...]
        dp_ref[:, 2 * dm:3 * dm] = d_z.astype(BF16)
        _shifted_copies(uwin, shf, ts)
        for k in range(CONV_K):
            part = jnp.zeros((CONV_ROWS_BWD, dm), F32)
            for r0 in range(0, ts, CONV_ROWS_BWD):
                part = part + dwin[r0:r0 + CONV_ROWS_BWD, :] * _tap(uwin, shf, HALO - (CONV_K - 1) + k, r0,
                                                                    CONV_ROWS_BWD)
            gcw_ref[k:k + 1, :] += jnp.sum(part, axis=0, keepdims=True)
        _shifted_copies(dwin, shf, ts)
        for r0 in range(0, ts, CONV_ROWS_BWD):
            d_u0 = jnp.zeros((CONV_ROWS_BWD, dm), F32)
            for k in range(CONV_K):
                d_u0 = d_u0 + cw_ref[k:k + 1, :] * _tap(dwin, shf, CONV_K - 1 - k, r0, CONV_ROWS_BWD)
            rows = slice(r0, r0 + CONV_ROWS_BWD)
            sgb = _sig(p_ref[rows, dm:2 * dm])
            dp_ref[rows, 0:dm] = (d_u0 * sgb).astype(BF16)
            dp_ref[rows, dm:2 * dm] = (d_u0 * p_ref[rows, 0:dm] * sgb * (1.0 - sgb)).astype(BF16)
        dwin[ts:ts + HALO, :] = dwin[0:HALO, :]

    rev = lambda i: (nt - 1 - i, 0)
    row = pl.BlockSpec((1, dm), lambda i: (0, 0))
    tile = pl.BlockSpec((ts, dm), rev)
    return pl.pallas_call(
        body, name="conv_backward", grid=(nt,),
        in_specs=[ANY, tile, pl.BlockSpec((ts, d3), rev), tile,
                  pl.BlockSpec((HALO, dm), lambda i: (jnp.maximum((nt - 1 - i) * hb - 1, 0), 0)),
                  tile, pl.BlockSpec((ts, 1), rev), pl.BlockSpec((HALO, dm), lambda i: (0, 0)), row, row,
                  pl.BlockSpec((dm, dm), lambda i: (0, 0))],
        out_specs=[pl.BlockSpec((ts, d3), rev), pl.BlockSpec((dm, dm), lambda i: (0, 0)),
                   pl.BlockSpec((8, dm), lambda i: (0, 0)), pl.BlockSpec((HALO, dm), lambda i: (0, 0))],
        out_shape=[jax.ShapeDtypeStruct(dp.shape, BF16), jax.ShapeDtypeStruct((dm, dm), F32),
                   jax.ShapeDtypeStruct((8, dm), F32), jax.ShapeDtypeStruct((HALO, dm), F32)],
        input_output_aliases={0: 0},
        scratch_shapes=[pltpu.VMEM((ts + HALO, dm), F32), pltpu.VMEM((ts + HALO, dm), F32),
                        pltpu.VMEM((7, ts + HALO - 8, dm), F32)],
        compiler_params=_params(("arbitrary",)),
    )(dp, dyc, p_conv, u0, u0, xh, rstd, conv_w, ln_g, ln_b, w_co)


def _att_masks():
    head0 = lax.broadcasted_iota(jnp.int32, (BLK, LANES), 1) < 64
    col = lax.broadcasted_iota(jnp.int32, (BLK, 4 * BLK), 1)
    row = lax.broadcasted_iota(jnp.int32, (BLK, 4 * BLK), 0)
    kj = col % BLK
    prev = jnp.where(col < 2 * BLK, 1, 0)
    band = jnp.where(col < 2 * BLK, kj - row, row - kj)
    return head0, band, prev


def _fill_block_diagonal(dst, slab, src_ref, halo_ref, head0, nq):
    sl = slice(slab * LANES, (slab + 1) * LANES)
    for b in range(nq + 1):
        blk = halo_ref[:, sl] if b == 0 else src_ref[(b - 1) * BLK:b * BLK, sl]
        base = (slab * (nq + 1) + b) * 2 * BLK
        zero = jnp.zeros_like(blk)
        dst[base:base + BLK, :] = jnp.where(head0, blk, zero)
        dst[base + BLK:base + 2 * BLK, :] = jnp.where(head0, zero, blk)


def _attention_forward(qkv, seq_len, qt, name):
    s_len = qkv.shape[1]
    nq = qt // BLK
    tiles_per_seq = seq_len // qt

    def body(q_ref, k_ref, v_ref, kh_ref, vh_ref, o_ref, lse_ref, kbd, vbd):
        first = jnp.where((pl.program_id(0) % tiles_per_seq) == 0, 4 * BLK, 0)
        head0, band, prev = _att_masks()
        band_first = band - prev * first
        for p in range(4):
            _fill_block_diagonal(kbd, p, k_ref, kh_ref, head0, nq)
            _fill_block_diagonal(vbd, p, v_ref, vh_ref, head0, nq)
        units = [(p, n) for p in range(4) for n in range(nq)]

        def keys_of(p, n):
            base = (p * (nq + 1) + n) * 2 * BLK
            return slice(base, base + 4 * BLK)

        def scores(p, n):
            q2 = q_ref[n * BLK:(n + 1) * BLK, p * LANES:(p + 1) * LANES] * 0.125
            return _dot_nt(q2, kbd[keys_of(p, n), :])

        def finish(p, n, o, den, lse):
            rows, sl = slice(n * BLK, (n + 1) * BLK), slice(p * LANES, (p + 1) * LANES)
            o_ref[rows, sl] = o / jnp.where(head0, den[0], den[1])
            lse_ref[rows, sl] = jnp.where(head0, lse[0], lse[1])

        s_next = scores(*units[0])
        pending = None
        for ui, (p, n) in enumerate(units):
            s = s_next
            if ui + 1 < len(units):
                s_next = scores(*units[ui + 1])
            s = jnp.where((band_first if n == 0 else band) >= 0, s, NEG)
            grp = [s[:, g * BLK:(g + 1) * BLK] for g in range(4)]
            ps, den, lse = [None] * 4, [], []
            for h in range(2):
                m = jnp.max(jnp.maximum(grp[h], grp[2 + h]), axis=-1, keepdims=True)
                ps[h], ps[2 + h] = jnp.exp(grp[h] - m), jnp.exp(grp[2 + h] - m)
                dn = jnp.sum(ps[h] + ps[2 + h], axis=-1, keepdims=True)
                den.append(dn)
                lse.append(m + jnp.log(dn))
            pmat = jnp.concatenate([x.astype(BF16) for x in ps], axis=1)
            o = _dot(pmat, vbd[keys_of(p, n), :])
            if pending is not None:
                finish(*pending)
            pending = (p, n, o, den, lse)
        finish(*pending)

    def which(w):
        return pl.BlockSpec((None, qt, ATT_W), lambda i: (w, i, 0))

    def halo(w):
        return pl.BlockSpec((None, BLK, ATT_W), lambda i: (w, jnp.maximum(i * nq - 1, 0), 0))

    out = pl.BlockSpec((qt, ATT_W), lambda i: (i, 0))
    bd = pltpu.VMEM((4 * (nq + 1) * 2 * BLK, LANES), BF16)
    return pl.pallas_call(
        body, name=name, grid=(s_len // qt,),
        in_specs=[which(0), which(1), which(2), halo(1), halo(2)],
        out_specs=[out, out], out_shape=[jax.ShapeDtypeStruct((s_len, ATT_W), F32)] * 2,
        scratch_shapes=[bd, bd],
        compiler_params=_params(("parallel",)),
    )(qkv, qkv, qkv, qkv, qkv)


def _attention_backward(qkv, d_att, lse, delta, seq_len, qt, name):
    s_len = qkv.shape[1]
    nq = qt // BLK
    tiles_per_seq = seq_len // qt
    nblk = s_len // BLK

    def body(q_ref, k_ref, v_ref, kh_ref, vh_ref, do_ref, lse_ref, dl_ref,
             qn_ref, don_ref, lsen_ref, dln_ref, dqkv_ref, kbd, vbd):
        i = pl.program_id(0)
        first = jnp.where((i % tiles_per_seq) == 0, 4 * BLK, 0)
        last = jnp.where((i % tiles_per_seq) == tiles_per_seq - 1, 4 * BLK, 0)
        head0, band, prev = _att_masks()
        band_first = band - prev * first
        band_tail = band[:, 0:2 * BLK] - last
        for p in range(4):
            _fill_block_diagonal(kbd, p, k_ref, kh_ref, head0, nq)
            _fill_block_diagonal(vbd, p, v_ref, vh_ref, head0, nq)
        units = [(p, n) for p in range(4) for n in range(nq + 1)]

        def stage_a(p, n):
            sl = slice(p * LANES, (p + 1) * LANES)
            base = (p * (nq + 1) + n) * 2 * BLK
            if n < nq:
                rows = slice(n * BLK, (n + 1) * BLK)
                q2, do2, lse2, dl2 = q_ref[rows, sl], do_ref[rows, sl], lse_ref[rows, sl], dl_ref[rows, sl]
                keys = slice(base, base + 4 * BLK)
            else:
                q2, do2, lse2, dl2 = qn_ref[:, sl], don_ref[:, sl], lsen_ref[:, sl], dln_ref[:, sl]
                keys = slice(base, base + 2 * BLK)
            s = _dot_nt(q2 * 0.125, kbd[keys, :])
            dp = _dot_nt(do2, vbd[keys, :])
            return q2, do2, lse2, dl2, keys, s, dp

        def stage_b(n, lse2, dl2, s, dp):
            mask = band_tail if n == nq else (band_first if n == 0 else band)
            ps, dss = [], []
            for g in range(s.shape[1] // BLK):
                h = g % 2
                cols = slice(g * BLK, (g + 1) * BLK)
                pg = jnp.exp(jnp.where(mask[:, cols] >= 0, s[:, cols] - lse2[:, h * 64:h * 64 + 1], NEG))
                ps.append(pg.astype(BF16))
                dss.append((pg * (dp[:, cols] - dl2[:, h * 64:h * 64 + 1]) * 0.125).astype(BF16))
            return jnp.concatenate(ps, axis=1), jnp.concatenate(dss, axis=1)

        def heads(r, g):
            return jnp.where(head0, r[g * BLK:(g + 1) * BLK, :], r[(g + 1) * BLK:(g + 2) * BLK, :])

        a_next = stage_a(*units[0])
        carry = None
        for ui, (p, n) in enumerate(units):
            q2, do2, lse2, dl2, keys, s, dp = a_next
            if ui + 1 < len(units):
                a_next = stage_a(*units[ui + 1])
            pmat, dsmat = stage_b(n, lse2, dl2, s, dp)
            sl = slice(p * LANES, (p + 1) * LANES)
            if n < nq:
                dqkv_ref[0, n * BLK:(n + 1) * BLK, sl] = _dot(dsmat, kbd[keys, :])
            dkbd = _dot_tn(dsmat, q2)
            dvbd = _dot_tn(pmat, do2)
            if n > 0:
                prow = slice((n - 1) * BLK, n * BLK)
                dqkv_ref[1, prow, sl] = carry[0] + heads(dkbd, 0)
                dqkv_ref[2, prow, sl] = carry[1] + heads(dvbd, 0)
            if n < nq:
                carry = (heads(dkbd, 2), heads(dvbd, 2))

    def which(w):
        return pl.BlockSpec((None, qt, ATT_W), lambda i: (w, i, 0))

    def prev(w):
        return pl.BlockSpec((None, BLK, ATT_W), lambda i: (w, jnp.maximum(i * nq - 1, 0), 0))

    tile = pl.BlockSpec((qt, ATT_W), lambda i: (i, 0))
    nxt = pl.BlockSpec((BLK, ATT_W), lambda i: (jnp.minimum((i + 1) * nq, nblk - 1), 0))
    nxt_q = pl.BlockSpec((None, BLK, ATT_W), lambda i: (0, jnp.minimum((i + 1) * nq, nblk - 1), 0))
    return pl.pallas_call(
        body, name=name, grid=(s_len // qt,),
        in_specs=[which(0), which(1), which(2), prev(1), prev(2), tile, tile, tile, nxt_q, nxt, nxt, nxt],
        out_specs=pl.BlockSpec((3, qt, ATT_W), lambda i: (0, i, 0)),
        out_shape=jax.ShapeDtypeStruct((3, s_len, ATT_W), F32),
        scratch_shapes=[pltpu.VMEM((4 * (nq + 1) * 2 * BLK, LANES), BF16)] * 2,
        compiler_params=_params(("parallel",)),
    )(qkv, qkv, qkv, qkv, qkv, d_att, lse, delta, qkv, d_att, lse, delta)


def _merge_and_head(o_g, lse_g, p_gate, y_conv, x, tgt, modv, final_g, w_ao, w_o, ts):
    s_len, dm = x.shape
    gw = ATT_W + 2 * dm
    nt = s_len // ts
    gate_off = 3 * dm + 9 * ATT_W
    assert gate_off % gw == 0

    def body(o0, o1, o2, l0, l1, l2, pg_ref, yc_ref, x_ref, t_ref, mod_ref, fg_ref, wao_ref, wo_ref,
             loss_ref, dx_ref, dyc_ref, dpg_ref, da0, da1, da2, ls0, ls1, ls2, de0, de1, de2,
             gwo_ref, gwao_ref, gs_ref, slab, ones_scr):
        i = pl.program_id(0)

        @pl.when(i == 0)
        def _():
            loss_ref[...] = jnp.zeros_like(loss_ref)
            gwo_ref[...] = jnp.zeros_like(gwo_ref)
            gwao_ref[...] = jnp.zeros_like(gwao_ref)
            gs_ref[...] = jnp.zeros_like(gs_ref)
            ri = lax.broadcasted_iota(jnp.int32, (ATT_W, ATT_W), 0) // 64
            ci = lax.broadcasted_iota(jnp.int32, (ATT_W, ATT_W), 1) // 64
            ones_scr[...] = jnp.where(ri == ci, 1.0, 0.0).astype(BF16)

        os_, ls_ = [], []
        for dil, o_ref, l_ref in zip(DILATIONS, (o0, o1, o2), (l0, l1, l2)):
            os_.append(jnp.concatenate(_interleave_load(o_ref, slab, dil, ts), axis=1))
            ls_.append(jnp.concatenate(_interleave_load(l_ref, slab, dil, ts), axis=1))
        mx = jnp.maximum(jnp.maximum(ls_[0], ls_[1]), ls_[2])
        wts = [jnp.exp(l - mx) for l in ls_]
        wsum = wts[0] + wts[1] + wts[2]
        att = (wts[0] * os_[0] + wts[1] * os_[1] + wts[2] * os_[2]) / wsum
        lse_all = mx + jnp.log(wsum)

        z_att, g_conv, g_att = pg_ref[:, 0:ATT_W], pg_ref[:, ATT_W:ATT_W + dm], pg_ref[:, ATT_W + dm:gw]
        sgz = _sig(z_att)
        sz = z_att * sgz
        a_att = (att * sz).astype(BF16)
        y_att = _dot(a_att, wao_ref[...])
        y_conv = yc_ref[...]
        sgc, sga = _sig(g_conv), _sig(g_att)
        merged = (sgc * y_conv + sga * y_att).astype(BF16)
        mo = _dot(merged, wo_ref[...])
        gate = mod_ref[2:3, :]
        x2 = x_ref[...] + gate * mo
        r = lax.rsqrt(jnp.mean(x2 * x2, axis=-1, keepdims=True) + EPS)
        xr = x2 * r
        err = xr * fg_ref[...] - t_ref[...]
        loss_ref[...] += 0.5 * jnp.sum(jnp.mean(err * err, axis=-1, keepdims=True))
        dy = err * (1.0 / dm)
        gs_ref[0:1, :] += jnp.sum(dy * xr, axis=0, keepdims=True)
        dyg = dy * fg_ref[...]
        d_x2 = r * dyg - xr * (r * jnp.mean(dyg * xr, axis=-1, keepdims=True))
        dx_ref[...] = d_x2
        gs_ref[1:2, :] += jnp.sum(d_x2 * mo, axis=0, keepdims=True)
        d_mo = (d_x2 * gate).astype(BF16)
        d_mg = _dot_nt(d_mo, wo_ref[...])
        gwo_ref[...] += _dot_tn(merged, d_mo)
        dyc_ref[...] = (d_mg * sgc).astype(BF16)
        dpg_ref[:, ATT_W:ATT_W + dm] = (d_mg * y_conv * sgc * (1.0 - sgc)).astype(BF16)
        d_ya = (d_mg * sga).astype(BF16)
        dpg_ref[:, ATT_W + dm:gw] = (d_mg * y_att * sga * (1.0 - sga)).astype(BF16)
        gwao_ref[...] += _dot_tn(a_att, d_ya)
        d_aa = _dot_nt(d_ya, wao_ref[...])
        dpg_ref[:, 0:ATT_W] = (d_aa * att * _dsilu(z_att, sgz)).astype(BF16)
        d_att = d_aa * sz
        ones = ones_scr[...]
        prod = d_att * att
        hi = prod.astype(BF16)
        lo = (prod - hi.astype(F32)).astype(BF16)
        delta = _dot(hi, ones) + _dot(lo, ones)
        for val, refs, dt in ((d_att, (da0, da1, da2), BF16), (lse_all, (ls0, ls1, ls2), F32),
                              (delta, (de0, de1, de2), F32)):
            vals = [val[:, s * LANES:(s + 1) * LANES] for s in range(4)]
            for dil, ref in zip(DILATIONS, refs):
                _deinterleave_store(vals, slab, ref, dil, ts, dt)

    def grp(dil):
        return pl.BlockSpec((dil, ts // dil, ATT_W), lambda i: (0, i, 0))

    tile = pl.BlockSpec((ts, dm), lambda i: (i, 0))
    gate_tile = pl.BlockSpec((ts, gw), lambda i: (i, 0))
    const = lambda shp: pl.BlockSpec(shp, lambda i: tuple(0 for _ in shp))
    grp_shape = lambda dt: [jax.ShapeDtypeStruct((dil, s_len // dil, ATT_W), dt) for dil in DILATIONS]
    return pl.pallas_call(
        body, name="merge_and_head", grid=(nt,),
        in_specs=[grp(d) for d in DILATIONS] * 2
        + [gate_tile, tile, tile, tile, const((8, dm)), const((1, dm)), const((ATT_W, dm)), const((dm, dm))],
        out_specs=[const((8, LANES)), tile, tile, pl.BlockSpec((ts, gw), lambda i: (i, gate_off // gw))]
        + [grp(d) for d in DILATIONS] * 3
        + [const((dm, dm)), const((ATT_W, dm)), const((8, dm))],
        out_shape=[jax.ShapeDtypeStruct((8, LANES), F32), jax.ShapeDtypeStruct((s_len, dm), F32),
                   jax.ShapeDtypeStruct((s_len, dm), BF16), jax.ShapeDtypeStruct((s_len, gate_off + gw), BF16)]
        + grp_shape(BF16) + grp_shape(F32) + grp_shape(F32)
        + [jax.ShapeDtypeStruct((dm, dm), F32), jax.ShapeDtypeStruct((ATT_W, dm), F32),
           jax.ShapeDtypeStruct((8, dm), F32)],
        scratch_shapes=[pltpu.VMEM((8, ts, LANES), F32), pltpu.VMEM((ATT_W, ATT_W), BF16)],
        compiler_params=_params(("arbitrary",)),
    )(*o_g, *lse_g, p_gate, y_conv, x, tgt, modv, final_g, w_ao, w_o)


def _qkv_grad_to_tokens(dp, dqkv_g, rope, ts):
    s_len, dm = rope.shape[1], (dp.shape[1] - 10 * ATT_W) // 5
    qw = 3 * ATT_W
    assert (3 * dm) % qw == 0

    def body(dp_in, g0, g1, g2, rope_ref, o_ref, slab):
        del dp_in
        w = pl.program_id(1)
        rc, ra, rb = rope_ref.at[0], rope_ref.at[1], rope_ref.at[2]

        def emit(roped):
            for gi, (dil, g_ref) in enumerate(zip(DILATIONS, (g0, g1, g2))):
                vals = _interleave_load(g_ref, slab, dil, ts)
                for s in range(4):
                    t = vals[s]
                    if roped:
                        t = t * rc[...] + pltpu.roll(t * ra[...], 8, 1) + pltpu.roll(t * rb[...], LANES - 8, 1)
                    col = gi * ATT_W + s * LANES
                    o_ref[:, col:col + LANES] = t.astype(BF16)

        pl.when(w < 2)(lambda: emit(True))
        pl.when(w == 2)(lambda: emit(False))

    return pl.pallas_call(
        body, name="qkv_grad_to_tokens", grid=(s_len // ts, 3),
        in_specs=[ANY] + [pl.BlockSpec((None, dil, ts // dil, ATT_W), lambda i, w: (w, 0, i, 0)) for dil in DILATIONS]
        + [pl.BlockSpec((3, ts, LANES), lambda i, w: (0, i, 0))],
        out_specs=pl.BlockSpec((ts, qw), lambda i, w: (i, 3 * dm // qw + w)),
        out_shape=jax.ShapeDtypeStruct(dp.shape, BF16),
        input_output_aliases={0: 0},
        scratch_shapes=[pltpu.VMEM((8, ts, LANES), F32)],
        compiler_params=_params(("parallel", "arbitrary")),
    )(dp, *dqkv_g, rope)


def _wide_col_tile(cols):
    for width in (5 * COL_TILE, 2 * COL_TILE):
        if cols % width == 0:
            return width
    return COL_TILE


def _input_grad(dp, w_in, x, dx_res, modv, norm_g, ts):
    s_len, dm = x.shape
    ct = _wide_col_tile(dp.shape[1])
    nct = dp.shape[1] // ct

    def body(p_ref, w_ref, x_ref, dxr_ref, mod_ref, g_ref, gx_ref, gs_ref, acc):
        i, j = pl.program_id(0), pl.program_id(1)

        @pl.when((i == 0) & (j == 0))
        def _():
            gs_ref[...] = jnp.zeros_like(gs_ref)

        @pl.when(j == 0)
        def _():
            acc[...] = jnp.zeros_like(acc)

        acc[...] += _dot_nt(p_ref[...], w_ref[...])

        @pl.when(j == nct - 1)
        def _():
            d_h = acc[...]
            xv = x_ref[...]
            r = lax.rsqrt(jnp.mean(xv * xv, axis=-1, keepdims=True) + EPS)
            xr = xv * r
            gs_ref[0:1, :] += jnp.sum(d_h, axis=0, keepdims=True)
            gs_ref[1:2, :] += jnp.sum(d_h * (xr * g_ref[...]), axis=0, keepdims=True)
            d_n = d_h * (1.0 + mod_ref[1:2, :])
            gs_ref[2:3, :] += jnp.sum(d_n * xr, axis=0, keepdims=True)
            dxn = d_n * g_ref[...]
            gx_ref[...] = dxr_ref[...] + r * dxn - xr * (r * jnp.mean(dxn * xr, axis=-1, keepdims=True))

    tile = pl.BlockSpec((ts, dm), lambda i, j: (i, 0))
    return pl.pallas_call(
        body, name="input_grad", grid=(s_len // ts, nct),
        in_specs=[pl.BlockSpec((ts, ct), lambda i, j: (i, j)), pl.BlockSpec((dm, ct), lambda i, j: (0, j)), tile, tile,
                  pl.BlockSpec((8, dm), lambda i, j: (0, 0)), pl.BlockSpec((1, dm), lambda i, j: (0, 0))],
        out_specs=[tile, pl.BlockSpec((8, dm), lambda i, j: (0, 0))],
        out_shape=[jax.ShapeDtypeStruct((s_len, dm), F32), jax.ShapeDtypeStruct((8, dm), F32)],
        scratch_shapes=[pltpu.VMEM((ts, dm), F32)],
        compiler_params=_params(("arbitrary", "arbitrary")),
    )(dp, w_in, x, dx_res, modv, norm_g)


def _w_in_grad(h_t, dp, ts):
    dm, s_len = h_t.shape
    ct = _wide_col_tile(dp.shape[1])

    def body(h_ref, p_ref, o_ref):
        @pl.when(pl.program_id(1) == 0)
        def _():
            o_ref[...] = jnp.zeros_like(o_ref)

        o_ref[...] += _dot(h_ref[...], p_ref[...])

    return pl.pallas_call(
        body, name="w_in_grad", grid=(dp.shape[1] // ct, s_len // ts),
        in_specs=[pl.BlockSpec((dm, ts), lambda j, i: (0, i)), pl.BlockSpec((ts, ct), lambda j, i: (i, j))],
        out_specs=pl.BlockSpec((dm, ct), lambda j, i: (0, j)),
        out_shape=jax.ShapeDtypeStruct((dm, dp.shape[1]), F32),
        compiler_params=_params(("arbitrary", "arbitrary")),
    )(h_t, dp)


def _rope_lane_table():
    l64 = jnp.arange(LANES) % 64
    half = ROT_DIM // 2
    inv_freq = ROPE_THETA ** (-(jnp.arange(half, dtype=F32) * 2.0 / ROT_DIM))
    rot = l64 < ROT_DIM
    rows = [jnp.where(rot, inv_freq[l64 % half], 0.0), (l64 < half).astype(F32),
            ((l64 >= half) & rot).astype(F32), rot.astype(F32)]
    return jnp.concatenate([jnp.stack(rows), jnp.zeros((4, LANES), F32)], axis=0)


def _tile_sizes(s_len):
    ts_big = min(1024, s_len // 2)
    ts_mid = 256
    ts_head = 256
    qt = [min(1024, s_len // dil) for dil in DILATIONS]
    return ts_big, ts_mid, ts_head, qt


def kernel(x, c, positions, norm_g, w_ada, b_ada, w_in, conv_w, conv_b, conv_ln_g, conv_ln_b, w_conv_out, w_att_out, w_o, final_g, loss_target, m_norm_g, m_w_ada, m_b_ada, m_w_in, m_conv_w, m_conv_b, m_conv_ln_g, m_conv_ln_b, m_w_conv_out, m_w_att_out, m_w_o, m_final_g, v_norm_g, v_w_ada, v_b_ada, v_w_in, v_conv_w, v_conv_b, v_conv_ln_g, v_conv_ln_b, v_w_conv_out, v_w_att_out, v_w_o, v_final_g):
    s_len, dm = x.shape[1], x.shape[2]
    ts_big, ts_mid, ts_head, qt = _tile_sizes(s_len)
    xi, yi, cidx = _my_place()
    chip = 2 * xi + yi
    batch = 4 * xi + 2 * yi + cidx
    x2d, tgt = x[0], loss_target[0]
    pos = positions.reshape(s_len, 1)
    wa_l, wi_l, cw_l = w_ada[0], w_in[0], conv_w[0]
    wco_l, wao_l, wo_l = w_conv_out[0], w_att_out[0], w_o[0]
    ada_w = wa_l.shape[1]
    cw_cols = cw_l.shape[1]

    kinds = ("col", "row", "col", "row")
    w_bufs = _cast_weights([wi_l, wco_l, wao_l, wo_l], kinds)

    cw_pad = jnp.pad(cw_l, ((0, HALO - CONV_K), (0, 0)))
    small_in = jnp.concatenate([jnp.broadcast_to(c, (8, dm)), cw_pad.reshape(8, dm)], axis=0)
    small = _allgather_small(small_in).reshape(8, 16, dm)
    c_all = small[:, 0, :]
    conv_w_full = jnp.concatenate(
        [small[2 * p, 8:16, :].reshape(HALO, cw_cols) for p in range(4)], axis=1)
    b_l = lax.dynamic_slice(b_ada, (0, chip * ada_w), (1, ada_w))
    mod_parts = _allgather_small(_mod_part(c_all, wa_l, b_l)).reshape(8, 8, ada_w)
    mod_rows = lax.dynamic_index_in_dim(mod_parts, batch, axis=1, keepdims=False)
    mod = jnp.concatenate([mod_rows[2 * p] for p in range(4)], axis=0).reshape(3, dm)
    modv = jnp.concatenate([mod, jnp.zeros((5, dm), F32)], axis=0)

    sems_a, sems_b, w_bufs, token = _gather_weights_start(w_bufs, kinds, modv)
    rope = _rope_coefficients(pos, _rope_lane_table() + token, ts_big)
    h_b, h_t = _norm_modulate(x2d, modv + token[0, 0], norm_g, ts_big)
    (w_in_b,) = _gather_weights_wait(w_bufs, kinds, sems_a, 0, (rope, h_b), "gather_weights_wait_w_in")
    w_in_b = _pass_to_sibling(w_in_b, kinds[0])

    n_conv, n_gate = 3 * dm // COL_TILE, (ATT_W + 2 * dm) // COL_TILE
    ts_proj = min(4 * ts_big, s_len)
    p_conv = _plain_projection(h_b, w_in_b, 0, n_conv, ts_proj, "conv_projection")
    p_gate = _plain_projection(h_b, w_in_b, n_conv + 9, n_gate, ts_proj, "gate_projection")
    qkv_g = [_qkv_projection(h_b, rope, w_in_b, gi, min(2 * ts_big, s_len)) for gi in range(3)]
    w_co_b, w_ao_b, w_o_b = _gather_weights_wait(w_bufs, kinds, sems_b, 1, (qkv_g[2], p_conv, p_gate),
                                                  "gather_weights_wait_rest")
    y_conv, u0, xh, rstd = _conv_forward(p_conv, conv_w_full, conv_b, conv_ln_g, conv_ln_b, w_co_b, ts_mid)
    qkv_flat = [q.reshape(3, s_len, ATT_W) for q in qkv_g]
    o_g, lse_g = [], []
    for gi, dil in enumerate(DILATIONS):
        o, l = _attention_forward(qkv_flat[gi], s_len // dil, qt[gi], "attention_forward_%d" % dil)
        o_g.append(o.reshape(dil, s_len // dil, ATT_W))
        lse_g.append(l.reshape(dil, s_len // dil, ATT_W))

    (loss_p, dx_res, dyc, dp, da0, da1, da2, ls0, ls1, ls2, de0, de1, de2,
     g_wo, g_wao, head_sums) = _merge_and_head(o_g, lse_g, p_gate, y_conv, x2d, tgt, modv,
                                               final_g.reshape(1, dm), w_ao_b, w_o_b, ts_head)

    dp, g_wco, conv_sums, g_cw = _conv_backward(dp, dyc, p_conv, u0, xh, rstd, conv_w_full, conv_ln_g, conv_ln_b,
                                                w_co_b, ts_mid)
    dqkv_g = []
    for gi, (dil, da, ls, de) in enumerate(zip(DILATIONS, (da0, da1, da2), (ls0, ls1, ls2), (de0, de1, de2))):
        flat = lambda a: a.reshape(s_len, ATT_W)
        dqkv = _attention_backward(qkv_flat[gi], flat(da), flat(ls), flat(de), s_len // dil, qt[gi],
                                   "attention_backward_%d" % dil)
        dqkv_g.append(dqkv.reshape(3, dil, s_len // dil, ATT_W))
    dp = _qkv_grad_to_tokens(dp, dqkv_g, rope, ts_big)
    g_win = _w_in_grad(h_t, dp, min(2 * ts_big, s_len))

    grads = [g_win, g_wco, g_wao, g_wo]
    full_shapes = [g.shape for g in grads]
    c_arr = jnp.reshape(cidx, (1,)).astype(jnp.int32)
    recv_halves = _reduce_pair_exchange(grads, kinds)
    halves = [_reduce_pair_sum(g, pa, kind, c_arr, "reduce_pair_sum_%d" % k)
              for k, (g, pa, kind) in enumerate(zip(grads, recv_halves, kinds))]
    send_sems, recv_sems, halves, lands, token = _reduce_to_owner_start(halves, kinds, full_shapes)
    grad_x, in_sums = _input_grad(dp, w_in_b, x2d, dx_res, modv + token[0, 0], norm_g, ts_big)
    halves, recvd = _reduce_to_owner_wait(send_sems, recv_sems, halves, lands, in_sums, kinds)
    gr_win, gr_wco, gr_wao, gr_wo = _reduce_finish(halves, recvd, kinds, full_shapes)

    rows = [in_sums[2:3], conv_sums[2:3], conv_sums[0:1], conv_sums[1:2], head_sums[0:1],
            in_sums[0:1], in_sums[1:2], head_sums[1:2], g_cw, jnp.pad(loss_p, ((0, 0), (0, dm - LANES)))]
    part = jnp.concatenate(rows, axis=0)
    buf = lax.dynamic_update_slice(jnp.zeros((8 * 48, dm), F32), part, (batch * 48, 0))
    row_sems_s, row_sems_r, buf, row_token = _row_gather_start(buf, 48, gr_win)
    upd = {
        "w_in": _adamw(wi_l, gr_win, m_w_in[0], v_w_in[0], "adamw_w_in", (row_token,)),
        "w_co": _adamw(wco_l, gr_wco, m_w_conv_out[0], v_w_conv_out[0], "adamw_w_conv_out", (row_token,)),
        "w_ao": _adamw(wao_l, gr_wao, m_w_att_out[0], v_w_att_out[0], "adamw_w_att_out", (row_token,)),
        "w_o": _adamw(wo_l, gr_wo, m_w_o[0], v_w_o[0], "adamw_w_o", (row_token,)),
    }
    gathered = _row_gather_wait(row_sems_s, row_sems_r, buf, 48, [upd[k][0] for k in ("w_in", "w_co", "w_ao", "w_o")])
    tot = _sum_devices(gathered, 48)
    loss = tot[8 + HALO, 0]
    dmod_all = gathered.reshape(8, 48, dm)[:, 5:8, :].reshape(8, 3 * dm)
    dmod_l = lax.dynamic_slice(dmod_all, (0, chip * ada_w), (8, ada_w))
    gr_wada = _grad_w_ada(c_all.T, dmod_l)
    gr_cw = lax.dynamic_slice(tot[8:8 + HALO], (0, chip * cw_cols), (HALO, cw_cols))

    pad_cw = lambda a: jnp.pad(a[0], ((0, HALO - CONV_K), (0, 0)))
    row = lambda a: a.reshape(1, dm)
    small_upd = _adamw_small(
        [norm_g, conv_b, conv_ln_g, conv_ln_b, row(final_g), b_ada], tot[0:8],
        [m_norm_g, m_conv_b, m_conv_ln_g, m_conv_ln_b, row(m_final_g), m_b_ada],
        [v_norm_g, v_conv_b, v_conv_ln_g, v_conv_ln_b, row(v_final_g), v_b_ada])
    upd["w_ada"] = _adamw(wa_l, gr_wada, m_w_ada[0], v_w_ada[0], "adamw_w_ada")
    upd["conv_w"] = _adamw(cw_pad, gr_cw, pad_cw(m_conv_w), pad_cw(v_conv_w), "adamw_conv_w")

    def family(which):
        if which is None:
            sm = small_upd[0:6]
            big = {"w_ada": gr_wada, "w_in": gr_win, "conv_w": gr_cw, "w_co": gr_wco, "w_ao": gr_wao, "w_o": gr_wo}
        else:
            sm = small_upd[6 * (which + 1):6 * (which + 2)]
            big = {k: upd[k][which] for k in ("w_ada", "w_in", "conv_w", "w_co", "w_ao", "w_o")}
        return [sm[0], big["w_ada"][None], sm[5], big["w_in"][None],
                big["conv_w"][None, :CONV_K], sm[1], sm[2], sm[3], big["w_co"][None],
                big["w_ao"][None], big["w_o"][None], sm[4].reshape(dm)]

    return (loss, grad_x[None], *family(None), *family(0), *family(1), *family(2))
```

```python
import jax
import jax.numpy as jnp
from jax import lax
from jax.experimental import pallas as pl
from jax.experimental.pallas import tpu as pltpu

F32 = jnp.float32
BF16 = jnp.bfloat16
MESH = pl.DeviceIdType.MESH
ANY = pl.BlockSpec(memory_space=pl.ANY)
VM = pl.BlockSpec(memory_space=pltpu.VMEM)

EPS = 1e-6
NEG = -1e30
ATT_W = 512
DILATIONS = (1, 4, 16)
BLK = 128
CONV_K = 31
HALO = 32
CONV_ROWS_FWD = 32
CONV_ROWS_BWD = 16
ROT_DIM = 16
ROPE_THETA = 500000.0
COL_TILE = 512
LANES = 128
VMEM_LIMIT = 56 * 1024 * 1024

ADAM_LR, ADAM_B1, ADAM_B2, ADAM_EPS, ADAM_WD, ADAM_STEP = 0.001, 0.9, 0.999, 1e-08, 0.01, 10


def _params(sem=None, vmem=VMEM_LIMIT):
    return pltpu.CompilerParams(dimension_semantics=sem, vmem_limit_bytes=vmem)


def _dot(a, b):
    return jnp.dot(a, b, preferred_element_type=F32)


def _dot_nt(a, b):
    return lax.dot_general(a, b, (((1,), (1,)), ((), ())), preferred_element_type=F32)


def _dot_tn(a, b):
    return lax.dot_general(a, b, (((0,), (0,)), ((), ())), preferred_element_type=F32)


def _sig(x):
    return jax.nn.sigmoid(x)


def _dsilu(x, s):
    return s * (1.0 + x * (1.0 - s))


def _my_place():
    return lax.axis_index("x"), lax.axis_index("y"), lax.axis_index("c")


def _allgather_small(x_shard):
    m_per, n = x_shard.shape

    def body(x_ref, out_ref, send_sems, recv_sems, local_sem):
        x, y, c = _my_place()
        me, sibling = (x, y, c), (x, y, 1 - c)
        chips = [(1 - x, y), (x, 1 - y), (1 - x, 1 - y)]

        def rows(px, py, pc):
            return out_ref.at[pl.ds((4 * px + 2 * py + pc) * m_per, m_per), :]

        def copy(k, block, to, src=None):
            return pltpu.make_async_remote_copy(
                src_ref=rows(*block) if src is None else src, dst_ref=rows(*block),
                send_sem=send_sems.at[k], recv_sem=recv_sems.at[k],
                device_id=to, device_id_type=MESH)

        mine = pltpu.make_async_copy(x_ref, rows(*me), local_sem)
        mine.start()
        first = [copy(0, me, sibling, src=x_ref)]
        first += [copy(1 + j, me, (*chip, c), src=x_ref) for j, chip in enumerate(chips)]
        for cp in first:
            cp.start()
        passed = [copy(4 + j, (*chip, c), sibling) for j, chip in enumerate(chips)]
        for j, chip in enumerate(chips):
            copy(1 + j, (*chip, c), me).wait_recv()
            passed[j].start()
        copy(0, sibling, me).wait_recv()
        for j, chip in enumerate(chips):
            copy(4 + j, (*chip, 1 - c), me).wait_recv()
        for cp in first + passed:
            cp.wait_send()
        mine.wait()

    return pl.pallas_call(
        body, name="allgather_small",
        out_shape=jax.ShapeDtypeStruct((8 * m_per, n), x_shard.dtype),
        in_specs=[VM], out_specs=VM,
        scratch_shapes=[pltpu.SemaphoreType.DMA((7,)), pltpu.SemaphoreType.DMA((7,)),
                        pltpu.SemaphoreType.DMA],
    )(x_shard)


def _shard_window(ref, kind, p, n_shards=4):
    r, c = ref.shape
    if kind == "col":
        w = c // n_shards
        return ref.at[:, pl.ds(p * w, w)]
    w = r // n_shards
    return ref.at[pl.ds(p * w, w), :]


def _half_window(ref, kind, hc):
    r, c = ref.shape
    if kind == "col":
        return ref.at[pl.ds(hc * (r // 2), r // 2), :]
    return ref.at[:, pl.ds(hc * (c // 2), c // 2)]


def _landed(ref, kind, chip, hc):
    return _half_window(_shard_window(ref, kind, 2 * chip[0] + chip[1]), kind, hc)


def _cast_weights(shards, kinds):
    n = len(shards)
    full_shapes = [(s.shape[0], 4 * s.shape[1]) if kind == "col" else (4 * s.shape[0], s.shape[1])
                   for s, kind in zip(shards, kinds)]

    def body(*refs):
        w_refs, out_refs, bf_refs, sems = refs[:n], refs[n:2 * n], refs[2 * n:3 * n], refs[3 * n]
        x, y, _ = _my_place()
        cps = []
        for k in range(n):
            bf_refs[k][...] = w_refs[k][...].astype(BF16)
            cp = pltpu.make_async_copy(bf_refs[k], _shard_window(out_refs[k], kinds[k], 2 * x + y), sems.at[k])
            cp.start()
            cps.append(cp)
        for cp in cps:
            cp.wait()

    return pl.pallas_call(
        body, name="cast_weights",
        out_shape=[jax.ShapeDtypeStruct(s, BF16) for s in full_shapes],
        in_specs=[VM] * n, out_specs=[ANY] * n,
        scratch_shapes=[pltpu.VMEM(s.shape, BF16) for s in shards] + [pltpu.SemaphoreType.DMA((n,))],
        compiler_params=_params(),
    )(*shards)


def _gather_copies(refs, kinds, sems_a, sems_b):
    x, y, c = _my_place()
    chips = [(1 - x, y), (x, 1 - y), (1 - x, 1 - y)]
    set_a, set_b = [], []
    for j, chip in enumerate(chips):
        for k in range(len(refs)):
            if refs[k] is None:
                continue
            for flip in ((0,) if k == 0 else (0, 1)):
                win = _landed(refs[k], kinds[k], (x, y), c)
                sems, idx = (sems_a, j) if k == 0 else (sems_b, ((k - 1) * 3 + j) * 2 + flip)
                mine = _landed(refs[k], kinds[k], chip, (c + flip) % 2)
                (set_a if k == 0 else set_b).append((
                    pltpu.make_async_remote_copy(
                        src_ref=win, dst_ref=win, send_sem=sems[0].at[idx], recv_sem=sems[1].at[idx],
                        device_id=(*chip, (c + flip) % 2), device_id_type=MESH),
                    pltpu.make_async_remote_copy(
                        src_ref=mine, dst_ref=mine, send_sem=sems[0].at[idx], recv_sem=sems[1].at[idx],
                        device_id=(*chip, (c + flip) % 2), device_id_type=MESH)))
    return set_a, set_b


_SPLIT = dict(has_side_effects=pltpu.SideEffectType.DATAFLOW_SIDE_EFFECTING)


def _gather_weights_start(fulls, kinds, after):
    n = len(fulls)
    hbm = pl.BlockSpec(memory_space=pltpu.HBM)
    sem = pl.BlockSpec(memory_space=pltpu.SEMAPHORE)
    nb = (n - 1) * 6

    def body(*refs):
        in_refs = refs[:n]
        sa, ra, sb, rb = refs[n + 1:n + 5]
        token = refs[-1]
        set_a, set_b = _gather_copies(in_refs, kinds, (sa, ra), (sb, rb))
        for out_cp, _ in set_a + set_b:
            out_cp.start()
        token[...] = jnp.zeros_like(token)

    out = pl.pallas_call(
        body, name="gather_weights_start",
        out_shape=[pltpu.SemaphoreType.DMA((3,)), pltpu.SemaphoreType.DMA((3,)),
                   pltpu.SemaphoreType.DMA((nb,)), pltpu.SemaphoreType.DMA((nb,))]
        + [pltpu.HBM(f.shape, f.dtype) for f in fulls] + [jax.ShapeDtypeStruct((8, LANES), F32)],
        in_specs=[hbm] * n + [ANY], out_specs=[sem] * 4 + [hbm] * n + [VM],
        input_output_aliases={k: 4 + k for k in range(n)},
        compiler_params=pltpu.CompilerParams(**_SPLIT),
    )(*[pltpu.with_memory_space_constraint(f, pltpu.HBM) for f in fulls], after)
    return (out[0], out[1]), (out[2], out[3]), out[4:4 + n], out[-1]


def _gather_weights_wait(fulls, kinds, sems, which, after, name):
    n = len(fulls)
    hbm = pl.BlockSpec(memory_space=pltpu.HBM)
    sem = pl.BlockSpec(memory_space=pltpu.SEMAPHORE)
    keep = [0] if which == 0 else list(range(1, n))

    def body(*refs):
        m = len(keep)
        full_refs = [None] * n
        for pos_, k in enumerate(keep):
            full_refs[k] = refs[pos_]
        s_ref, r_ref = refs[m:m + 2]
        if which == 0:
            sets = _gather_copies([full_refs[0]], kinds[:1], (s_ref, r_ref), None)[0]
        else:
            sets = _gather_copies([None] + [full_refs[k] for k in keep], kinds, None, (s_ref, r_ref))[1]
        for out_cp, in_cp in sets:
            out_cp.wait_send()
            in_cp.wait_recv()

    out = pl.pallas_call(
        body, name=name,
        out_shape=[pltpu.HBM(fulls[k].shape, fulls[k].dtype) for k in keep],
        in_specs=[hbm] * len(keep) + [sem, sem] + [ANY] * len(after), out_specs=[hbm] * len(keep),
        input_output_aliases={i: i for i in range(len(keep))},
        compiler_params=pltpu.CompilerParams(**_SPLIT),
    )(*[fulls[k] for k in keep], sems[0], sems[1], *after)
    return list(out)


def _row_gather_copies(buf, m_per, send_sems, recv_sems):
    x, y, c = _my_place()
    pairs = []
    for idx in range(1, 8):
        dx, dy, dc = idx // 4, (idx // 2) % 2, idx % 2
        peer = ((x + dx) % 2, (y + dy) % 2, (c + dc) % 2)
        mine = buf.at[pl.ds((4 * x + 2 * y + c) * m_per, m_per), :]
        theirs = buf.at[pl.ds((4 * peer[0] + 2 * peer[1] + peer[2]) * m_per, m_per), :]
        pairs.append(tuple(
            pltpu.make_async_remote_copy(src_ref=w, dst_ref=w, send_sem=send_sems.at[idx - 1],
                                         recv_sem=recv_sems.at[idx - 1], device_id=peer, device_id_type=MESH)
            for w in (mine, theirs)))
    return pairs


def _row_gather_start(buf, m_per, after):
    hbm = pl.BlockSpec(memory_space=pltpu.HBM)
    sem = pl.BlockSpec(memory_space=pltpu.SEMAPHORE)

    def body(buf_ref, after_ref, send_sems, recv_sems, out_ref, token):
        del after_ref, out_ref
        for out_cp, _ in _row_gather_copies(buf_ref, m_per, send_sems, recv_sems):
            out_cp.start()
        token[...] = jnp.zeros_like(token)

    return pl.pallas_call(
        body, name="row_gather_start",
        out_shape=[pltpu.SemaphoreType.DMA((7,)), pltpu.SemaphoreType.DMA((7,)), pltpu.HBM(buf.shape, buf.dtype),
                   jax.ShapeDtypeStruct((8, LANES), F32)],
        in_specs=[hbm, ANY], out_specs=[sem, sem, hbm, VM], input_output_aliases={0: 2},
        compiler_params=pltpu.CompilerParams(**_SPLIT),
    )(pltpu.with_memory_space_constraint(buf, pltpu.HBM), after)


def _row_gather_wait(send_sems, recv_sems, buf, m_per, after):
    hbm = pl.BlockSpec(memory_space=pltpu.HBM)
    sem = pl.BlockSpec(memory_space=pltpu.SEMAPHORE)

    def body(buf_ref, send_s, recv_s, *rest):
        for out_cp, in_cp in _row_gather_copies(buf_ref, m_per, send_s, recv_s):
            out_cp.wait_send()
            in_cp.wait_recv()

    return pl.pallas_call(
        body, name="row_gather_wait", out_shape=pltpu.HBM(buf.shape, buf.dtype),
        in_specs=[hbm, sem, sem] + [ANY] * len(after), out_specs=hbm, input_output_aliases={0: 0},
        compiler_params=pltpu.CompilerParams(**_SPLIT),
    )(buf, send_sems, recv_sems, *after)


def _pass_to_sibling(w_full, kind):
    def body(w_in_ref, w_ref, send_sems, recv_sems):
        del w_in_ref
        x, y, c = _my_place()
        chips = [(1 - x, y), (x, 1 - y), (1 - x, 1 - y)]
        cps = []
        for j, chip in enumerate(chips):
            win = _landed(w_ref, kind, chip, c)
            cp = pltpu.make_async_remote_copy(
                src_ref=win, dst_ref=win, send_sem=send_sems.at[j], recv_sem=recv_sems.at[j],
                device_id=(x, y, 1 - c), device_id_type=MESH)
            cp.start()
            cps.append(cp)
        for j, chip in enumerate(chips):
            theirs = _landed(w_ref, kind, chip, 1 - c)
            pltpu.make_async_remote_copy(
                src_ref=theirs, dst_ref=theirs, send_sem=send_sems.at[j], recv_sem=recv_sems.at[j],
                device_id=(x, y, 1 - c), device_id_type=MESH).wait_recv()
        for cp in cps:
            cp.wait_send()

    return pl.pallas_call(
        body, name="pass_to_sibling",
        out_shape=jax.ShapeDtypeStruct(w_full.shape, w_full.dtype),
        in_specs=[ANY], out_specs=ANY, input_output_aliases={0: 0},
        scratch_shapes=[pltpu.SemaphoreType.DMA((3,)), pltpu.SemaphoreType.DMA((3,))],
    )(w_full)


def _pair_copies(g_refs, pa_refs, kinds, send_sems, recv_sems, cols, with_rest):
    x, y, c = _my_place()
    lo, hi = cols

    def copy(k, src, dst):
        return pltpu.make_async_remote_copy(src_ref=src, dst_ref=dst, send_sem=send_sems.at[k], recv_sem=recv_sems.at[k],
                                            device_id=(x, y, 1 - c), device_id_type=MESH)

    cps = [copy(0, _half_window(g_refs[0], kinds[0], 1 - c).at[:, pl.ds(lo, hi - lo)],
                pa_refs[0].at[:, pl.ds(lo, hi - lo)])]
    if with_rest:
        cps += [copy(k, _half_window(g_refs[k], kinds[k], 1 - c), pa_refs[k]) for k in range(1, len(g_refs))]
    return cps


def _pair_exchange_start(grads, pas, kinds, cols):
    n = len(grads)
    hbm = pl.BlockSpec(memory_space=pltpu.HBM)
    sem = pl.BlockSpec(memory_space=pltpu.SEMAPHORE)

    def body(*refs):
        g_refs, pa_refs = refs[:n], refs[n:2 * n]
        send_sems, recv_sems = refs[2 * n:2 * n + 2]
        for cp in _pair_copies(g_refs, pa_refs, kinds, send_sems, recv_sems, cols, True):
            cp.start()
        refs[-1][...] = jnp.zeros_like(refs[-1])

    out = pl.pallas_call(
        body, name="pair_exchange_start",
        out_shape=[pltpu.SemaphoreType.DMA((n,)), pltpu.SemaphoreType.DMA((n,))]
        + [pltpu.HBM(a.shape, a.dtype) for a in list(grads) + list(pas)] + [jax.ShapeDtypeStruct((8, LANES), F32)],
        in_specs=[hbm] * (2 * n), out_specs=[sem, sem] + [hbm] * (2 * n) + [VM],
        input_output_aliases={k: 2 + k for k in range(2 * n)},
        compiler_params=pltpu.CompilerParams(**_SPLIT),
    )(*[pltpu.with_memory_space_constraint(a, pltpu.HBM) for a in list(grads) + list(pas)])
    return out[0], out[1], list(out[2:2 + n]), list(out[2 + n:2 + 2 * n]), out[-1]


def _pair_exchange_wait(send_sems, recv_sems, grads, pas, kinds, cols):
    n = len(grads)
    hbm = pl.BlockSpec(memory_space=pltpu.HBM)
    sem = pl.BlockSpec(memory_space=pltpu.SEMAPHORE)

    def body(*refs):
        g_refs, pa_refs = refs[:n], refs[n:2 * n]
        send_s, recv_s = refs[2 * n:2 * n + 2]
        for cp in _pair_copies(g_refs, pa_refs, kinds, send_s, recv_s, cols, True):
            cp.wait_send()
            cp.wait_recv()

    out = pl.pallas_call(
        body, name="pair_exchange_wait",
        out_shape=[pltpu.HBM(a.shape, a.dtype) for a in list(grads) + list(pas)],
        in_specs=[hbm] * (2 * n) + [sem, sem], out_specs=[hbm] * (2 * n),
        input_output_aliases={k: k for k in range(2 * n)},
        compiler_params=pltpu.CompilerParams(**_SPLIT),
    )(*grads, *pas, send_sems, recv_sems)
    return list(out[:n]), list(out[n:])


def _pair_exchange_rest(g, pa, kind, cols):
    def body(g_ref, pa_in, pa_ref, send_sems, recv_sems):
        del pa_in
        (cp,) = _pair_copies([g_ref], [pa_ref], [kind], send_sems, recv_sems, cols, False)
        cp.start()
        cp.wait()

    return pl.pallas_call(
        body, name="pair_exchange_rest", out_shape=jax.ShapeDtypeStruct(pa.shape, pa.dtype),
        in_specs=[ANY, ANY], out_specs=ANY, input_output_aliases={1: 0},
        scratch_shapes=[pltpu.SemaphoreType.DMA((1,)), pltpu.SemaphoreType.DMA((1,))],
    )(g, pa)


def _row_tile(rows, cols, itemsize=4, target=2 * 1024 * 1024, mult=16):
    t = rows
    while t % 2 == 0 and t // 2 >= mult and (t // 2) % mult == 0 and t * cols * itemsize > target:
        t //= 2
    return t


def _reduce_pair_sum(g, pa, kind, c_arr, name):
    hr, hc_ = pa.shape
    tr = _row_tile(hr, hc_)
    nb = hr // tr

    def body(c_ref, g_ref, pa_ref, o_ref):
        o_ref[...] = (g_ref[...] + pa_ref[...]).astype(BF16)

    if kind == "col":
        g_map = lambda i, c_ref: (c_ref[0] * nb + i, 0)
    else:
        g_map = lambda i, c_ref: (i, c_ref[0])
    return pl.pallas_call(
        body, name=name,
        grid_spec=pltpu.PrefetchScalarGridSpec(
            num_scalar_prefetch=1, grid=(nb,),
            in_specs=[pl.BlockSpec((tr, hc_), g_map), pl.BlockSpec((tr, hc_), lambda i, c_ref: (i, 0))],
            out_specs=pl.BlockSpec((tr, hc_), lambda i, c_ref: (i, 0))),
        out_shape=jax.ShapeDtypeStruct((hr, hc_), BF16),
        compiler_params=_params(("parallel",)),
    )(c_arr, g, pa)


def _half_shard_shape(full_shape, kind):
    r, c = full_shape
    return (r // 2, c // 4) if kind == "col" else (r // 4, c // 2)


def _to_owner_copies(h_refs, land_refs, send_sems, recv_sems, kinds):
    n = len(h_refs)
    x, y, c = _my_place()
    chips = [(1 - x, y), (x, 1 - y), (1 - x, 1 - y)]
    cps = []
    for j, chip in enumerate(chips):
        pj = 2 * chip[0] + chip[1]
        for k in range(n):
            cps.append(pltpu.make_async_remote_copy(
                src_ref=_shard_window(h_refs[k], kinds[k], pj), dst_ref=land_refs[k].at[j],
                send_sem=send_sems.at[j * n + k], recv_sem=recv_sems.at[j * n + k],
                device_id=(*chip, c), device_id_type=MESH))
    return cps


def _reduce_to_owner_start(halves, kinds, full_shapes):
    n = len(halves)
    hs = [_half_shard_shape(fs, kind) for fs, kind in zip(full_shapes, kinds)]
    hbm = pl.BlockSpec(memory_space=pltpu.HBM)
    sem = pl.BlockSpec(memory_space=pltpu.SEMAPHORE)

    def body(*refs):
        h_refs, land_refs = refs[:n], refs[n:2 * n]
        send_sems, recv_sems = refs[2 * n:2 * n + 2]
        token = refs[-1]
        for cp in _to_owner_copies(h_refs, land_refs, send_sems, recv_sems, kinds):
            cp.start()
        token[...] = jnp.zeros_like(token)

    lands = [pltpu.with_memory_space_constraint(lax.empty((3,) + s, BF16), pltpu.HBM) for s in hs]
    out = pl.pallas_call(
        body, name="reduce_to_owner_start",
        out_shape=[pltpu.SemaphoreType.DMA((3 * n,)), pltpu.SemaphoreType.DMA((3 * n,))]
        + [pltpu.HBM(h.shape, h.dtype) for h in halves] + [pltpu.HBM((3,) + s, BF16) for s in hs]
        + [jax.ShapeDtypeStruct((8, LANES), F32)],
        in_specs=[hbm] * (2 * n), out_specs=[sem, sem] + [hbm] * (2 * n) + [VM],
        input_output_aliases={k: 2 + k for k in range(2 * n)},
        compiler_params=pltpu.CompilerParams(has_side_effects=pltpu.SideEffectType.DATAFLOW_SIDE_EFFECTING),
    )(*[pltpu.with_memory_space_constraint(h, pltpu.HBM) for h in halves], *lands)
    return out[0], out[1], out[2:2 + n], out[2 + n:2 + 2 * n], out[-1]


def _reduce_to_owner_wait(send_sems, recv_sems, halves, lands, after, kinds):
    n = len(halves)
    hbm = pl.BlockSpec(memory_space=pltpu.HBM)
    sem = pl.BlockSpec(memory_space=pltpu.SEMAPHORE)

    def body(*refs):
        h_refs, land_refs = refs[:n], refs[n:2 * n]
        send_s, recv_s = refs[2 * n:2 * n + 2]
        for cp in _to_owner_copies(h_refs, land_refs, send_s, recv_s, kinds):
            cp.wait_send()
            cp.wait_recv()

    out = pl.pallas_call(
        body, name="reduce_to_owner_wait",
        out_shape=[pltpu.HBM(h.shape, h.dtype) for h in halves] + [pltpu.HBM(l.shape, l.dtype) for l in lands],
        in_specs=[hbm] * (2 * n) + [sem, sem, ANY], out_specs=[hbm] * (2 * n),
        input_output_aliases={k: k for k in range(2 * n)},
        compiler_params=pltpu.CompilerParams(has_side_effects=pltpu.SideEffectType.DATAFLOW_SIDE_EFFECTING),
    )(*halves, *lands, send_sems, recv_sems, after)
    return out[:n], out[n:]


def _reduce_finish(halves, recvd, kinds, full_shapes):
    n = len(halves)
    hs = [_half_shard_shape(fs, kind) for fs, kind in zip(full_shapes, kinds)]
    shard_shapes = [(fs[0], fs[1] // 4) if kind == "col" else (fs[0] // 4, fs[1])
                    for fs, kind in zip(full_shapes, kinds)]

    def body(*refs):
        h_refs, rc_refs, gs_refs = refs[:n], refs[n:2 * n], refs[2 * n:3 * n]
        own_refs, gh_refs = refs[3 * n:4 * n], refs[4 * n:5 * n]
        in_sems, loc_sems, send_sems, recv_sems = refs[5 * n:]
        x, y, c = _my_place()
        p = 2 * x + y
        loads = []
        for k in range(n):
            cp = pltpu.make_async_copy(_shard_window(h_refs[k], kinds[k], p), own_refs[k], in_sems.at[k])
            cp.start()
            loads.append(cp)
        outs = []
        for k in range(n):
            loads[k].wait()
            gh_refs[k][...] = (own_refs[k][...].astype(F32) + rc_refs[k][0].astype(F32)
                               + rc_refs[k][1].astype(F32) + rc_refs[k][2].astype(F32))
            dst = _half_window(gs_refs[k], kinds[k], c)
            lc = pltpu.make_async_copy(gh_refs[k], dst, loc_sems.at[k])
            lc.start()
            rc = pltpu.make_async_remote_copy(
                src_ref=gh_refs[k], dst_ref=dst, send_sem=send_sems.at[k], recv_sem=recv_sems.at[k],
                device_id=(x, y, 1 - c), device_id_type=MESH)
            rc.start()
            outs.append((lc, rc))
        for k, (lc, rc) in enumerate(outs):
            lc.wait()
            rc.wait_send()
            pltpu.make_async_remote_copy(
                src_ref=gh_refs[k], dst_ref=_half_window(gs_refs[k], kinds[k], 1 - c),
                send_sem=send_sems.at[k], recv_sem=recv_sems.at[k],
                device_id=(x, y, 1 - c), device_id_type=MESH).wait_recv()

    return pl.pallas_call(
        body, name="reduce_finish",
        out_shape=[jax.ShapeDtypeStruct(s, F32) for s in shard_shapes],
        in_specs=[ANY] * n + [VM] * n, out_specs=[ANY] * n,
        scratch_shapes=[pltpu.VMEM(s, BF16) for s in hs] + [pltpu.VMEM(s, F32) for s in hs]
        + [pltpu.SemaphoreType.DMA((n,)) for _ in range(4)],
        compiler_params=_params(),
    )(*halves, *recvd)


def _mod_part(c_all, w_ada_l, b_l):
    def body(c_ref, w_ref, b_ref, o_ref):
        o_ref[...] = _dot(c_ref[...].astype(BF16), w_ref[...].astype(BF16)) + b_ref[...]

    return pl.pallas_call(
        body, name="mod_part", out_shape=jax.ShapeDtypeStruct((8, w_ada_l.shape[1]), F32),
        in_specs=[VM, VM, VM], out_specs=VM, compiler_params=_params(),
    )(c_all, w_ada_l, b_l)


def _sum_devices(parts, m):
    def body(p_ref, o_ref):
        acc = p_ref[0:m, :]
        for d in range(1, 8):
            acc = acc + p_ref[d * m:(d + 1) * m, :]
        o_ref[...] = acc

    return pl.pallas_call(
        body, name="sum_devices", out_shape=jax.ShapeDtypeStruct((m, parts.shape[1]), F32),
        in_specs=[VM], out_specs=VM, compiler_params=_params(),
    )(parts)


def _grad_w_ada(c_all_t, dmod_l):
    d, w = c_all_t.shape[0], dmod_l.shape[1]

    def body(ct_ref, dm_ref, o_ref):
        acc = ct_ref[:, 0:1] * dm_ref[0:1, :]
        for b in range(1, 8):
            acc = acc + ct_ref[:, b:b + 1] * dm_ref[b:b + 1, :]
        o_ref[...] = acc

    return pl.pallas_call(
        body, name="grad_w_ada", out_shape=jax.ShapeDtypeStruct((d, w), F32),
        in_specs=[VM, VM], out_specs=VM, compiler_params=_params(),
    )(c_all_t, dmod_l)


def _adamw_math(w, g, m, v):
    nm = ADAM_B1 * m + (1.0 - ADAM_B1) * g
    nv = ADAM_B2 * v + (1.0 - ADAM_B2) * (g * g)
    m_hat = nm / (1.0 - ADAM_B1 ** ADAM_STEP)
    v_hat = nv / (1.0 - ADAM_B2 ** ADAM_STEP)
    return -ADAM_LR * (m_hat / (jnp.sqrt(v_hat) + ADAM_EPS) + ADAM_WD * w), nm, nv


def _adamw_small(ws, g8, ms, vs):
    shapes = [jax.ShapeDtypeStruct(w.shape, F32) for w in ws]

    def body(*refs):
        w_refs, g_ref, m_refs, v_refs, outs = refs[0:6], refs[6], refs[7:13], refs[13:19], refs[19:]
        for idx in range(6):
            if idx < 5:
                g = g_ref[idx:idx + 1, :]
            else:
                g = jnp.concatenate([g_ref[5:6, :], g_ref[6:7, :], g_ref[7:8, :]], axis=1)
            res = (g,) + _adamw_math(w_refs[idx][...], g, m_refs[idx][...], v_refs[idx][...])
            for fam in range(4):
                outs[6 * fam + idx][...] = res[fam]

    return pl.pallas_call(
        body, name="adamw_small", out_shape=shapes * 4,
        in_specs=[VM] * 19, out_specs=[VM] * 24, compiler_params=_params(),
    )(*ws, g8, *ms, *vs)


def _adamw(w, g, m, v, name, after=()):
    r, c = w.shape
    tr = _row_tile(r, c, target=1024 * 1024, mult=8)

    def body(w_ref, g_ref, m_ref, v_ref, *rest):
        d_ref, nm_ref, nv_ref = rest[len(after):]
        d_ref[...], nm_ref[...], nv_ref[...] = _adamw_math(w_ref[...], g_ref[...], m_ref[...], v_ref[...])

    spec = pl.BlockSpec((tr, c), lambda i: (i, 0))
    return pl.pallas_call(
        body, name=name, grid=(r // tr,),
        out_shape=[jax.ShapeDtypeStruct((r, c), F32)] * 3,
        in_specs=[spec] * 4 + [ANY] * len(after), out_specs=[spec] * 3,
        compiler_params=_params(("parallel",)),
    )(w, g, m, v, *after)


def _rope_coefficients(pos, ropetab, ts):
    s_len = pos.shape[0]

    def body(pos_ref, tab_ref, o_ref):
        ang = pos_ref[...].astype(F32) * tab_ref[0:1, :]
        cs, sn = jnp.cos(ang), jnp.sin(ang)
        o_ref[0] = jnp.where(tab_ref[3:4, :] > 0, cs, 1.0)
        o_ref[1] = -sn * tab_ref[1:2, :]
        o_ref[2] = sn * tab_ref[2:3, :]

    return pl.pallas_call(
        body, name="rope_coefficients", grid=(s_len // ts,),
        in_specs=[pl.BlockSpec((ts, 1), lambda i: (i, 0)), pl.BlockSpec((8, LANES), lambda i: (0, 0))],
        out_specs=pl.BlockSpec((3, ts, LANES), lambda i: (0, i, 0)),
        out_shape=jax.ShapeDtypeStruct((3, s_len, LANES), F32),
        compiler_params=_params(("parallel",)),
    )(pos, ropetab)


def _deinterleave_store(vals, slab, out_ref, d, ts, dtype):
    if d == 1:
        for s in range(4):
            out_ref[0, :, s * LANES:(s + 1) * LANES] = vals[s].astype(dtype)
        return
    for s in range(4):
        slab[s] = vals[s]
    if d == 16:
        q = ts // 4
        for s in range(4):
            for r1 in range(4):
                slab[4 + s, r1 * q:(r1 + 1) * q, :] = slab[s, pl.ds(r1, q, stride=4), :]
        for r in range(d):
            r1, r2 = r % 4, r // 4
            for s in range(4):
                out_ref[r, :, s * LANES:(s + 1) * LANES] = slab[
                    4 + s, pl.ds(r1 * q + r2, ts // d, stride=4), :].astype(dtype)
        return
    for r in range(d):
        for s in range(4):
            out_ref[r, :, s * LANES:(s + 1) * LANES] = slab[s, pl.ds(r, ts // d, stride=d), :].astype(dtype)


def _interleave_load(blk_ref, slab, d, ts):
    if d == 1:
        return [blk_ref[0, :, s * LANES:(s + 1) * LANES] for s in range(4)]
    if d == 16:
        q = ts // 4
        for r in range(d):
            r1, r2 = r % 4, r // 4
            for s in range(4):
                slab[4 + s, pl.ds(r1 * q + r2, ts // d, stride=4), :] = blk_ref[r, :, s * LANES:(s + 1) * LANES]
        for s in range(4):
            for r1 in range(4):
                slab[s, pl.ds(r1, q, stride=4), :] = slab[4 + s, r1 * q:(r1 + 1) * q, :]
        return [slab[s] for s in range(4)]
    for r in range(d):
        for s in range(4):
            slab[s, pl.ds(r, ts // d, stride=d), :] = blk_ref[r, :, s * LANES:(s + 1) * LANES]
    return [slab[s] for s in range(4)]


def _norm_modulate(x, modv, norm_g, ts):
    s_len, d_model = x.shape

    def body(x_ref, mod_ref, g_ref, h_ref, ht_ref):
        xv = x_ref[...]
        r = lax.rsqrt(jnp.mean(xv * xv, axis=-1, keepdims=True) + EPS)
        h = (xv * r) * g_ref[...] * (1.0 + mod_ref[1:2, :]) + mod_ref[0:1, :]
        h_ref[...] = h.astype(BF16)
        ht_ref[...] = h.T.astype(BF16)

    return pl.pallas_call(
        body, name="norm_modulate", grid=(s_len // ts,),
        in_specs=[pl.BlockSpec((ts, d_model), lambda i: (i, 0)), pl.BlockSpec((8, d_model), lambda i: (0, 0)),
                  pl.BlockSpec((1, d_model), lambda i: (0, 0))],
        out_specs=[pl.BlockSpec((ts, d_model), lambda i: (i, 0)), pl.BlockSpec((d_model, ts), lambda i: (0, i))],
        out_shape=[jax.ShapeDtypeStruct((s_len, d_model), BF16), jax.ShapeDtypeStruct((d_model, s_len), BF16)],
        compiler_params=_params(("parallel",)),
    )(x, modv, norm_g)


def _plain_projection(h, w_in, first_tile, n_tiles, ts, name):
    s_len, d_model = h.shape
    ct = COL_TILE

    def body(h_ref, w_ref, o_ref):
        o_ref[...] = _dot(h_ref[...], w_ref[...])

    return pl.pallas_call(
        body, name=name, grid=(s_len // ts, n_tiles),
        in_specs=[pl.BlockSpec((ts, d_model), lambda i, j: (i, 0)),
                  pl.BlockSpec((d_model, ct), lambda i, j: (0, first_tile + j))],
        out_specs=pl.BlockSpec((ts, ct), lambda i, j: (i, j)),
        out_shape=jax.ShapeDtypeStruct((s_len, n_tiles * ct), F32),
        compiler_params=_params(("parallel", "arbitrary")),
    )(h, w_in)


def _qkv_projection(h, rope, w_in, gi, ts):
    s_len, d_model = h.shape
    dil = DILATIONS[gi]
    nc = 3 * d_model // COL_TILE
    hr = ts // 2
    assert COL_TILE == ATT_W and hr % (16 * dil) == 0

    def body(h_ref, rope_ref, wq_ref, wk_ref, wv_ref, o_ref, slab):
        rc, ra, rb = rope_ref.at[0], rope_ref.at[1], rope_ref.at[2]
        w_refs = (wq_ref, wk_ref, wv_ref)
        units = [(w, half) for w in range(3) for half in range(2)]

        def matmul(w, half):
            return _dot(h_ref[half * hr:(half + 1) * hr, :], w_refs[w][...])

        def finish(w, half, res):
            rows = slice(half * hr, (half + 1) * hr)
            vals = []
            for s in range(4):
                t = res[:, s * LANES:(s + 1) * LANES]
                if w < 2:
                    t = (t * rc[rows, :] + pltpu.roll(t, LANES - 8, 1) * ra[rows, :]
                         + pltpu.roll(t, 8, 1) * rb[rows, :])
                vals.append(t)
            out = o_ref.at[w, :, half * (hr // dil):(half + 1) * (hr // dil), :]
            _deinterleave_store(vals, slab.at[half], out, dil, hr, BF16)

        res_next = matmul(*units[0])
        for ui, (w, half) in enumerate(units):
            res = res_next
            if ui + 1 < len(units):
                res_next = matmul(*units[ui + 1])
            finish(w, half, res)

    def w_spec(w):
        return pl.BlockSpec((d_model, COL_TILE), lambda i: (0, nc + 3 * w + gi))

    return pl.pallas_call(
        body, name="qkv_projection_%d" % dil, grid=(s_len // ts,),
        in_specs=[pl.BlockSpec((ts, d_model), lambda i: (i, 0)), pl.BlockSpec((3, ts, LANES), lambda i: (0, i, 0)),
                  w_spec(0), w_spec(1), w_spec(2)],
        out_specs=pl.BlockSpec((3, dil, ts // dil, ATT_W), lambda i: (0, 0, i, 0)),
        out_shape=jax.ShapeDtypeStruct((3, dil, s_len // dil, ATT_W), BF16),
        scratch_shapes=[pltpu.VMEM((2, 8, hr, LANES), F32)],
        compiler_params=_params(("parallel",)),
    )(h, rope, w_in, w_in, w_in)


def _layernorm_stats(u1):
    mu = jnp.mean(u1, axis=-1, keepdims=True)
    xc = u1 - mu
    rstd = lax.rsqrt(jnp.mean(xc * xc, axis=-1, keepdims=True) + EPS)
    return xc * rstd, rstd


def _shifted_copies(win, shf, ts):
    rows = ts + HALO - 8
    for b in range(1, 8):
        shf[b - 1, 0:rows, :] = win[pl.ds(b, rows), :]


def _tap(win, shf, off, r0, rows):
    a, b = divmod(off, 8)
    start = 8 * a + r0
    if b == 0:
        return win[start:start + rows, :]
    return shf[b - 1, start:start + rows, :]


def _conv_forward(p_conv, conv_w, conv_b, ln_g, ln_b, w_co, ts):
    s_len, d3 = p_conv.shape
    dm = d3 // 3

    def body(p_ref, cw_ref, cb_ref, g_ref, b_ref, w_ref, y_ref, u0_ref, xh_ref, rstd_ref, win, shf, u1_ref):
        i = pl.program_id(0)

        @pl.when(i == 0)
        def _():
            win[0:HALO, :] = jnp.zeros((HALO, dm), F32)

        a, b, z = p_ref[:, 0:dm], p_ref[:, dm:2 * dm], p_ref[:, 2 * dm:3 * dm]
        u0 = a * _sig(b)
        win[HALO:HALO + ts, :] = u0
        u0_ref[...] = u0
        _shifted_copies(win, shf, ts)
        for r0 in range(0, ts, CONV_ROWS_FWD):
            acc = jnp.broadcast_to(cb_ref[...], (CONV_ROWS_FWD, dm))
            for k in range(CONV_K):
                acc = acc + cw_ref[k:k + 1, :] * _tap(win, shf, HALO - (CONV_K - 1) + k, r0, CONV_ROWS_FWD)
            u1_ref[r0:r0 + CONV_ROWS_FWD, :] = acc
        xh, rstd = _layernorm_stats(u1_ref[...])
        xh_ref[...] = xh
        rstd_ref[...] = rstd
        u2 = xh * g_ref[...] + b_ref[...]
        a_conv = (u2 * _sig(u2)) * (z * _sig(z))
        y_ref[...] = _dot(a_conv.astype(BF16), w_ref[...])
        win[0:HALO, :] = win[ts:ts + HALO, :]

    row = pl.BlockSpec((1, dm), lambda i: (0, 0))
    tile = pl.BlockSpec((ts, dm), lambda i: (i, 0))
    return pl.pallas_call(
        body, name="conv_forward", grid=(s_len // ts,),
        in_specs=[pl.BlockSpec((ts, d3), lambda i: (i, 0)),
                  pl.BlockSpec((HALO, dm), lambda i: (0, 0)), row, row, row,
                  pl.BlockSpec((dm, dm), lambda i: (0, 0))],
        out_specs=[tile, tile, tile, pl.BlockSpec((ts, 1), lambda i: (i, 0))],
        out_shape=[jax.ShapeDtypeStruct((s_len, dm), F32)] * 3 + [jax.ShapeDtypeStruct((s_len, 1), F32)],
        scratch_shapes=[pltpu.VMEM((ts + HALO, dm), F32), pltpu.VMEM((7, ts + HALO - 8, dm), F32),
                        pltpu.VMEM((ts, dm), F32)],
        compiler_params=_params(("arbitrary",)),
    )(p_conv, conv_w, conv_b, ln_g, ln_b, w_co)


def _conv_backward(dp, dyc, p_conv, u0, xh, rstd, conv_w, ln_g, ln_b, w_co, ts):
    s_len, d3 = p_conv.shape
    dm = d3 // 3
    nt = s_len // ts
    hb = ts // HALO

    def body(dp_in, dy_ref, p_ref, u0_ref, uh_ref, xh_ref, rstd_ref, cw_ref, g_ref, b_ref, w_ref,
             dp_ref, gw_ref, gs_ref, gcw_ref, dwin, uwin, shf):
        del dp_in
        i = pl.program_id(0)
        ti = nt - 1 - i

        @pl.when(i == 0)
        def _():
            gw_ref[...] = jnp.zeros_like(gw_ref)
            gs_ref[...] = jnp.zeros_like(gs_ref)
            gcw_ref[...] = jnp.zeros_like(gcw_ref)
            dwin[ts:ts + HALO, :] = jnp.zeros((HALO, dm), F32)

        dy = dy_ref[...]
        z = p_ref[:, 2 * dm:3 * dm]
        d_ac = _dot_nt(dy, w_ref[...])
        xh, rstd = xh_ref[...], rstd_ref[...]
        u2 = xh * g_ref[...] + b_ref[...]
        sg2, sgz = _sig(u2), _sig(z)
        u3, sz = u2 * sg2, z * sgz
        gw_ref[...] += _dot_tn((u3 * sz).astype(BF16), dy)
        d_z = d_ac * u3 * _dsilu(z, sgz)
        d_u2 = d_ac * sz * _dsilu(u2, sg2)
        gs_ref[0:1, :] += jnp.sum(d_u2 * xh, axis=0, keepdims=True)
        gs_ref[1:2, :] += jnp.sum(d_u2, axis=0, keepdims=True)
        dxh = d_u2 * g_ref[...]
        d_u1 = rstd * (dxh - jnp.mean(dxh, axis=-1, keepdims=True)
                       - xh * jnp.mean(dxh * xh, axis=-1, keepdims=True))
        gs_ref[2:3, :] += jnp.sum(d_u1, axis=0, keepdims=True)
        dwin[0:ts, :] = d_u1
        uwin[0:HALO, :] = jnp.where(ti == 0, 0.0, uh_ref[...])
        uwin[HALO:HALO + ts, :] = u0_ref[...]
        dp_ref[:, 2 * dm:3 * dm] = d_z.astype(BF16)
        _shifted_copies(uwin, shf, ts)
        for k in range(CONV_K):
            part = jnp.zeros((CONV_ROWS_BWD, dm), F32)
            for r0 in range(0, ts, CONV_ROWS_BWD):
                part = part + dwin[r0:r0 + CONV_ROWS_BWD, :] * _tap(uwin, shf, HALO - (CONV_K - 1) + k, r0,
                                                                    CONV_ROWS_BWD)
            gcw_ref[k:k + 1, :] += jnp.sum(part, axis=0, keepdims=True)
        _shifted_copies(dwin, shf, ts)
        for r0 in range(0, ts, CONV_ROWS_BWD):
            d_u0 = jnp.zeros((CONV_ROWS_BWD, dm), F32)
            for k in range(CONV_K):
                d_u0 = d_u0 + cw_ref[k:k + 1, :] * _tap(dwin, shf, CONV_K - 1 - k, r0, CONV_ROWS_BWD)
            rows = slice(r0, r0 + CONV_ROWS_BWD)
            sgb = _sig(p_ref[rows, dm:2 * dm])
            dp_ref[rows, 0:dm] = (d_u0 * sgb).astype(BF16)
            dp_ref[rows, dm:2 * dm] = (d_u0 * p_ref[rows, 0:dm] * sgb * (1.0 - sgb)).astype(BF16)
        dwin[ts:ts + HALO, :] = dwin[0:HALO, :]

    rev = lambda i: (nt - 1 - i, 0)
    row = pl.BlockSpec((1, dm), lambda i: (0, 0))
    tile = pl.BlockSpec((ts, dm), rev)
    return pl.pallas_call(
        body, name="conv_backward", grid=(nt,),
        in_specs=[ANY, tile, pl.BlockSpec((ts, d3), rev), tile,
                  pl.BlockSpec((HALO, dm), lambda i: (jnp.maximum((nt - 1 - i) * hb - 1, 0), 0)),
                  tile, pl.BlockSpec((ts, 1), rev), pl.BlockSpec((HALO, dm), lambda i: (0, 0)), row, row,
                  pl.BlockSpec((dm, dm), lambda i: (0, 0))],
        out_specs=[pl.BlockSpec((ts, d3), rev), pl.BlockSpec((dm, dm), lambda i: (0, 0)),
                   pl.BlockSpec((8, dm), lambda i: (0, 0)), pl.BlockSpec((HALO, dm), lambda i: (0, 0))],
        out_shape=[jax.ShapeDtypeStruct(dp.shape, BF16), jax.ShapeDtypeStruct((dm, dm), F32),
                   jax.ShapeDtypeStruct((8, dm), F32), jax.ShapeDtypeStruct((HALO, dm), F32)],
        input_output_aliases={0: 0},
        scratch_shapes=[pltpu.VMEM((ts + HALO, dm), F32), pltpu.VMEM((ts + HALO, dm), F32),
                        pltpu.VMEM((7, ts + HALO - 8, dm), F32)],
        compiler_params=_params(("arbitrary",)),
    )(dp, dyc, p_conv, u0, u0, xh, rstd, conv_w, ln_g, ln_b, w_co)


def _att_masks():
    head0 = lax.broadcasted_iota(jnp.int32, (BLK, LANES), 1) < 64
    col = lax.broadcasted_iota(jnp.int32, (BLK, 4 * BLK), 1)
    row = lax.broadcasted_iota(jnp.int32, (BLK, 4 * BLK), 0)
    kj = col % BLK
    prev = jnp.where(col < 2 * BLK, 1, 0)
    band = jnp.where(col < 2 * BLK, kj - row, row - kj)
    return head0, band, prev


def _fill_block_diagonal(dst, slab, src_ref, halo_ref, head0, nq):
    sl = slice(slab * LANES, (slab + 1) * LANES)
    for b in range(nq + 1):
        blk = halo_ref[:, sl] if b == 0 else src_ref[(b - 1) * BLK:b * BLK, sl]
        base = (slab * (nq + 1) + b) * 2 * BLK
        zero = jnp.zeros_like(blk)
        dst[base:base + BLK, :] = jnp.where(head0, blk, zero)
        dst[base + BLK:base + 2 * BLK, :] = jnp.where(head0, zero, blk)


def _attention_forward(qkv, seq_len, qt, name):
    s_len = qkv.shape[1]
    nq = qt // BLK
    tiles_per_seq = seq_len // qt

    def body(q_ref, k_ref, v_ref, kh_ref, vh_ref, o_ref, lse_ref, kbd, vbd):
        first = jnp.where((pl.program_id(0) % tiles_per_seq) == 0, 4 * BLK, 0)
        head0, band, prev = _att_masks()
        band_first = band - prev * first
        for p in range(4):
            _fill_block_diagonal(kbd, p, k_ref, kh_ref, head0, nq)
            _fill_block_diagonal(vbd, p, v_ref, vh_ref, head0, nq)
        units = [(p, n) for p in range(4) for n in range(nq)]

        def keys_of(p, n):
            base = (p * (nq + 1) + n) * 2 * BLK
            return slice(base, base + 4 * BLK)

        def scores(p, n):
            q2 = q_ref[n * BLK:(n + 1) * BLK, p * LANES:(p + 1) * LANES] * 0.125
            return _dot_nt(q2, kbd[keys_of(p, n), :])

        def finish(p, n, o, den, lse):
            rows, sl = slice(n * BLK, (n + 1) * BLK), slice(p * LANES, (p + 1) * LANES)
            o_ref[rows, sl] = o / jnp.where(head0, den[0], den[1])
            lse_ref[rows, sl] = jnp.where(head0, lse[0], lse[1])

        s_next = scores(*units[0])
        pending = None
        for ui, (p, n) in enumerate(units):
            s = s_next
            if ui + 1 < len(units):
                s_next = scores(*units[ui + 1])
            s = jnp.where((band_first if n == 0 else band) >= 0, s, NEG)
            grp = [s[:, g * BLK:(g + 1) * BLK] for g in range(4)]
            ps, den, lse = [None] * 4, [], []
            for h in range(2):
                m = jnp.max(jnp.maximum(grp[h], grp[2 + h]), axis=-1, keepdims=True)
                ps[h], ps[2 + h] = jnp.exp(grp[h] - m), jnp.exp(grp[2 + h] - m)
                dn = jnp.sum(ps[h] + ps[2 + h], axis=-1, keepdims=True)
                den.append(dn)
                lse.append(m + jnp.log(dn))
            pmat = jnp.concatenate([x.astype(BF16) for x in ps], axis=1)
            o = _dot(pmat, vbd[keys_of(p, n), :])
            if pending is not None:
                finish(*pending)
            pending = (p, n, o, den, lse)
        finish(*pending)

    def which(w):
        return pl.BlockSpec((None, qt, ATT_W), lambda i: (w, i, 0))

    def halo(w):
        return pl.BlockSpec((None, BLK, ATT_W), lambda i: (w, jnp.maximum(i * nq - 1, 0), 0))

    out = pl.BlockSpec((qt, ATT_W), lambda i: (i, 0))
    bd = pltpu.VMEM((4 * (nq + 1) * 2 * BLK, LANES), BF16)
    return pl.pallas_call(
        body, name=name, grid=(s_len // qt,),
        in_specs=[which(0), which(1), which(2), halo(1), halo(2)],
        out_specs=[out, out], out_shape=[jax.ShapeDtypeStruct((s_len, ATT_W), F32)] * 2,
        scratch_shapes=[bd, bd],
        compiler_params=_params(("parallel",)),
    )(qkv, qkv, qkv, qkv, qkv)


def _attention_backward(qkv, d_att, lse, delta, seq_len, qt, name):
    s_len = qkv.shape[1]
    nq = qt // BLK
    tiles_per_seq = seq_len // qt
    nblk = s_len // BLK

    def body(q_ref, k_ref, v_ref, kh_ref, vh_ref, do_ref, lse_ref, dl_ref,
             qn_ref, don_ref, lsen_ref, dln_ref, dqkv_ref, kbd, vbd):
        i = pl.program_id(0)
        first = jnp.where((i % tiles_per_seq) == 0, 4 * BLK, 0)
        last = jnp.where((i % tiles_per_seq) == tiles_per_seq - 1, 4 * BLK, 0)
        head0, band, prev = _att_masks()
        band_first = band - prev * first
        band_tail = band[:, 0:2 * BLK] - last
        for p in range(4):
            _fill_block_diagonal(kbd, p, k_ref, kh_ref, head0, nq)
            _fill_block_diagonal(vbd, p, v_ref, vh_ref, head0, nq)
        units = [(p, n) for p in range(4) for n in range(nq + 1)]

        def stage_a(p, n):
            sl = slice(p * LANES, (p + 1) * LANES)
            base = (p * (nq + 1) + n) * 2 * BLK
            if n < nq:
                rows = slice(n * BLK, (n + 1) * BLK)
                q2, do2, lse2, dl2 = q_ref[rows, sl], do_ref[rows, sl], lse_ref[rows, sl], dl_ref[rows, sl]
                keys = slice(base, base + 4 * BLK)
            else:
                q2, do2, lse2, dl2 = qn_ref[:, sl], don_ref[:, sl], lsen_ref[:, sl], dln_ref[:, sl]
                keys = slice(base, base + 2 * BLK)
            s = _dot_nt(q2 * 0.125, kbd[keys, :])
            dp = _dot_nt(do2, vbd[keys, :])
            return q2, do2, lse2, dl2, keys, s, dp

        def stage_b(n, lse2, dl2, s, dp):
            mask = band_tail if n == nq else (band_first if n == 0 else band)
            ps, dss = [], []
            for g in range(s.shape[1] // BLK):
                h = g % 2
                cols = slice(g * BLK, (g + 1) * BLK)
                pg = jnp.exp(jnp.where(mask[:, cols] >= 0, s[:, cols] - lse2[:, h * 64:h * 64 + 1], NEG))
                ps.append(pg.astype(BF16))
                dss.append((pg * (dp[:, cols] - dl2[:, h * 64:h * 64 + 1]) * 0.125).astype(BF16))
            return jnp.concatenate(ps, axis=1), jnp.concatenate(dss, axis=1)

        def heads(r, g):
            return jnp.where(head0, r[g * BLK:(g + 1) * BLK, :], r[(g + 1) * BLK:(g + 2) * BLK, :])

        a_next = stage_a(*units[0])
        carry = None
        for ui, (p, n) in enumerate(units):
            q2, do2, lse2, dl2, keys, s, dp = a_next
            if ui + 1 < len(units):
                a_next = stage_a(*units[ui + 1])
            pmat, dsmat = stage_b(n, lse2, dl2, s, dp)
            sl = slice(p * LANES, (p + 1) * LANES)
            if n < nq:
                dqkv_ref[0, n * BLK:(n + 1) * BLK, sl] = _dot(dsmat, kbd[keys, :])
            dkbd = _dot_tn(dsmat, q2)
            dvbd = _dot_tn(pmat, do2)
            if n > 0:
                prow = slice((n - 1) * BLK, n * BLK)
                dqkv_ref[1, prow, sl] = carry[0] + heads(dkbd, 0)
                dqkv_ref[2, prow, sl] = carry[1] + heads(dvbd, 0)
            if n < nq:
                carry = (heads(dkbd, 2), heads(dvbd, 2))

    def which(w):
        return pl.BlockSpec((None, qt, ATT_W), lambda i: (w, i, 0))

    def prev(w):
        return pl.BlockSpec((None, BLK, ATT_W), lambda i: (w, jnp.maximum(i * nq - 1, 0), 0))

    tile = pl.BlockSpec((qt, ATT_W), lambda i: (i, 0))
    nxt = pl.BlockSpec((BLK, ATT_W), lambda i: (jnp.minimum((i + 1) * nq, nblk - 1), 0))
    nxt_q = pl.BlockSpec((None, BLK, ATT_W), lambda i: (0, jnp.minimum((i + 1) * nq, nblk - 1), 0))
    return pl.pallas_call(
        body, name=name, grid=(s_len // qt,),
        in_specs=[which(0), which(1), which(2), prev(1), prev(2), tile, tile, tile, nxt_q, nxt, nxt, nxt],
        out_specs=pl.BlockSpec((3, qt, ATT_W), lambda i: (0, i, 0)),
        out_shape=jax.ShapeDtypeStruct((3, s_len, ATT_W), F32),
        scratch_shapes=[pltpu.VMEM((4 * (nq + 1) * 2 * BLK, LANES), BF16)] * 2,
        compiler_params=_params(("parallel",)),
    )(qkv, qkv, qkv, qkv, qkv, d_att, lse, delta, qkv, d_att, lse, delta)


def _merge_and_head(o_g, lse_g, p_gate, y_conv, x, tgt, modv, final_g, w_ao, w_o, ts):
    s_len, dm = x.shape
    gw = ATT_W + 2 * dm
    nt = s_len // ts
    gate_off = 3 * dm + 9 * ATT_W
    assert gate_off % gw == 0

    def body(o0, o1, o2, l0, l1, l2, pg_ref, yc_ref, x_ref, t_ref, mod_ref, fg_ref, wao_ref, wo_ref,
             loss_ref, dx_ref, dyc_ref, dpg_ref, da0, da1, da2, ls0, ls1, ls2, de0, de1, de2,
             gwo_ref, gwao_ref, gs_ref, slab, ones_scr):
        i = pl.program_id(0)

        @pl.when(i == 0)
        def _():
            loss_ref[...] = jnp.zeros_like(loss_ref)
            gwo_ref[...] = jnp.zeros_like(gwo_ref)
            gwao_ref[...] = jnp.zeros_like(gwao_ref)
            gs_ref[...] = jnp.zeros_like(gs_ref)
            ri = lax.broadcasted_iota(jnp.int32, (ATT_W, ATT_W), 0) // 64
            ci = lax.broadcasted_iota(jnp.int32, (ATT_W, ATT_W), 1) // 64
            ones_scr[...] = jnp.where(ri == ci, 1.0, 0.0).astype(BF16)

        os_, ls_ = [], []
        for dil, o_ref, l_ref in zip(DILATIONS, (o0, o1, o2), (l0, l1, l2)):
            os_.append(jnp.concatenate(_interleave_load(o_ref, slab, dil, ts), axis=1))
            ls_.append(jnp.concatenate(_interleave_load(l_ref, slab, dil, ts), axis=1))
        mx = jnp.maximum(jnp.maximum(ls_[0], ls_[1]), ls_[2])
        wts = [jnp.exp(l - mx) for l in ls_]
        wsum = wts[0] + wts[1] + wts[2]
        att = (wts[0] * os_[0] + wts[1] * os_[1] + wts[2] * os_[2]) / wsum
        lse_all = mx + jnp.log(wsum)

        z_att, g_conv, g_att = pg_ref[:, 0:ATT_W], pg_ref[:, ATT_W:ATT_W + dm], pg_ref[:, ATT_W + dm:gw]
        sgz = _sig(z_att)
        sz = z_att * sgz
        a_att = (att * sz).astype(BF16)
        y_att = _dot(a_att, wao_ref[...])
        y_conv = yc_ref[...]
        sgc, sga = _sig(g_conv), _sig(g_att)
        merged = (sgc * y_conv + sga * y_att).astype(BF16)
        mo = _dot(merged, wo_ref[...])
        gate = mod_ref[2:3, :]
        x2 = x_ref[...] + gate * mo
        r = lax.rsqrt(jnp.mean(x2 * x2, axis=-1, keepdims=True) + EPS)
        xr = x2 * r
        err = xr * fg_ref[...] - t_ref[...]
        loss_ref[...] += 0.5 * jnp.sum(jnp.mean(err * err, axis=-1, keepdims=True))
        dy = err * (1.0 / dm)
        gs_ref[0:1, :] += jnp.sum(dy * xr, axis=0, keepdims=True)
        dyg = dy * fg_ref[...]
        d_x2 = r * dyg - xr * (r * jnp.mean(dyg * xr, axis=-1, keepdims=True))
        dx_ref[...] = d_x2
        gs_ref[1:2, :] += jnp.sum(d_x2 * mo, axis=0, keepdims=True)
        d_mo = (d_x2 * gate).astype(BF16)
        d_mg = _dot_nt(d_mo, wo_ref[...])
        gwo_ref[...] += _dot_tn(merged, d_mo)
        dyc_ref[...] = (d_mg * sgc).astype(BF16)
        dpg_ref[:, ATT_W:ATT_W + dm] = (d_mg * y_conv * sgc * (1.0 - sgc)).astype(BF16)
        d_ya = (d_mg * sga).astype(BF16)
        dpg_ref[:, ATT_W + dm:gw] = (d_mg * y_att * sga * (1.0 - sga)).astype(BF16)
        gwao_ref[...] += _dot_tn(a_att, d_ya)
        d_aa = _dot_nt(d_ya, wao_ref[...])
        dpg_ref[:, 0:ATT_W] = (d_aa * att * _dsilu(z_att, sgz)).astype(BF16)
        d_att = d_aa * sz
        ones = ones_scr[...]
        prod = d_att * att
        hi = prod.astype(BF16)
        lo = (prod - hi.astype(F32)).astype(BF16)
        delta = _dot(hi, ones) + _dot(lo, ones)
        for val, refs, dt in ((d_att, (da0, da1, da2), BF16), (lse_all, (ls0, ls1, ls2), F32),
                              (delta, (de0, de1, de2), F32)):
            vals = [val[:, s * LANES:(s + 1) * LANES] for s in range(4)]
            for dil, ref in zip(DILATIONS, refs):
                _deinterleave_store(vals, slab, ref, dil, ts, dt)

    def grp(dil):
        return pl.BlockSpec((dil, ts // dil, ATT_W), lambda i: (0, i, 0))

    tile = pl.BlockSpec((ts, dm), lambda i: (i, 0))
    gate_tile = pl.BlockSpec((ts, gw), lambda i: (i, 0))
    const = lambda shp: pl.BlockSpec(shp, lambda i: tuple(0 for _ in shp))
    grp_shape = lambda dt: [jax.ShapeDtypeStruct((dil, s_len // dil, ATT_W), dt) for dil in DILATIONS]
    return pl.pallas_call(
        body, name="merge_and_head", grid=(nt,),
        in_specs=[grp(d) for d in DILATIONS] * 2
        + [gate_tile, tile, tile, tile, const((8, dm)), const((1, dm)), const((ATT_W, dm)), const((dm, dm))],
        out_specs=[const((8, LANES)), tile, tile, pl.BlockSpec((ts, gw), lambda i: (i, gate_off // gw))]
        + [grp(d) for d in DILATIONS] * 3
        + [const((dm, dm)), const((ATT_W, dm)), const((8, dm))],
        out_shape=[jax.ShapeDtypeStruct((8, LANES), F32), jax.ShapeDtypeStruct((s_len, dm), F32),
                   jax.ShapeDtypeStruct((s_len, dm), BF16), jax.ShapeDtypeStruct((s_len, gate_off + gw), BF16)]
        + grp_shape(BF16) + grp_shape(F32) + grp_shape(F32)
        + [jax.ShapeDtypeStruct((dm, dm), F32), jax.ShapeDtypeStruct((ATT_W, dm), F32),
           jax.ShapeDtypeStruct((8, dm), F32)],
        scratch_shapes=[pltpu.VMEM((8, ts, LANES), F32), pltpu.VMEM((ATT_W, ATT_W), BF16)],
        compiler_params=_params(("arbitrary",)),
    )(*o_g, *lse_g, p_gate, y_conv, x, tgt, modv, final_g, w_ao, w_o)


def _qkv_grad_to_tokens(dp, dqkv_g, rope, ts):
    s_len, dm = rope.shape[1], (dp.shape[1] - 10 * ATT_W) // 5
    qw = 3 * ATT_W
    assert (3 * dm) % qw == 0

    def body(dp_in, g0, g1, g2, rope_ref, o_ref, slab):
        del dp_in
        w = pl.program_id(1)
        rc, ra, rb = rope_ref.at[0], rope_ref.at[1], rope_ref.at[2]

        def emit(roped):
            for gi, (dil, g_ref) in enumerate(zip(DILATIONS, (g0, g1, g2))):
                vals = _interleave_load(g_ref, slab, dil, ts)
                for s in range(4):
                    t = vals[s]
                    if roped:
                        t = t * rc[...] + pltpu.roll(t * ra[...], 8, 1) + pltpu.roll(t * rb[...], LANES - 8, 1)
                    col = gi * ATT_W + s * LANES
                    o_ref[:, col:col + LANES] = t.astype(BF16)

        pl.when(w < 2)(lambda: emit(True))
        pl.when(w == 2)(lambda: emit(False))

    return pl.pallas_call(
        body, name="qkv_grad_to_tokens", grid=(s_len // ts, 3),
        in_specs=[ANY] + [pl.BlockSpec((None, dil, ts // dil, ATT_W), lambda i, w: (w, 0, i, 0)) for dil in DILATIONS]
        + [pl.BlockSpec((3, ts, LANES), lambda i, w: (0, i, 0))],
        out_specs=pl.BlockSpec((ts, qw), lambda i, w: (i, 3 * dm // qw + w)),
        out_shape=jax.ShapeDtypeStruct(dp.shape, BF16),
        input_output_aliases={0: 0},
        scratch_shapes=[pltpu.VMEM((8, ts, LANES), F32)],
        compiler_params=_params(("parallel", "arbitrary")),
    )(dp, *dqkv_g, rope)


def _wide_col_tile(cols):
    for width in (5 * COL_TILE, 2 * COL_TILE):
        if cols % width == 0:
            return width
    return COL_TILE


def _input_grad(dp, w_in, x, dx_res, modv, norm_g, ts):
    s_len, dm = x.shape
    ct = _wide_col_tile(dp.shape[1])
    nct = dp.shape[1] // ct

    def body(p_ref, w_ref, x_ref, dxr_ref, mod_ref, g_ref, gx_ref, gs_ref, acc):
        i, j = pl.program_id(0), pl.program_id(1)

        @pl.when((i == 0) & (j == 0))
        def _():
            gs_ref[...] = jnp.zeros_like(gs_ref)

        @pl.when(j == 0)
        def _():
            acc[...] = jnp.zeros_like(acc)

        acc[...] += _dot_nt(p_ref[...], w_ref[...])

        @pl.when(j == nct - 1)
        def _():
            d_h = acc[...]
            xv = x_ref[...]
            r = lax.rsqrt(jnp.mean(xv * xv, axis=-1, keepdims=True) + EPS)
            xr = xv * r
            gs_ref[0:1, :] += jnp.sum(d_h, axis=0, keepdims=True)
            gs_ref[1:2, :] += jnp.sum(d_h * (xr * g_ref[...]), axis=0, keepdims=True)
            d_n = d_h * (1.0 + mod_ref[1:2, :])
            gs_ref[2:3, :] += jnp.sum(d_n * xr, axis=0, keepdims=True)
            dxn = d_n * g_ref[...]
            gx_ref[...] = dxr_ref[...] + r * dxn - xr * (r * jnp.mean(dxn * xr, axis=-1, keepdims=True))

    tile = pl.BlockSpec((ts, dm), lambda i, j: (i, 0))
    return pl.pallas_call(
        body, name="input_grad", grid=(s_len // ts, nct),
        in_specs=[pl.BlockSpec((ts, ct), lambda i, j: (i, j)), pl.BlockSpec((dm, ct), lambda i, j: (0, j)), tile, tile,
                  pl.BlockSpec((8, dm), lambda i, j: (0, 0)), pl.BlockSpec((1, dm), lambda i, j: (0, 0))],
        out_specs=[tile, pl.BlockSpec((8, dm), lambda i, j: (0, 0))],
        out_shape=[jax.ShapeDtypeStruct((s_len, dm), F32), jax.ShapeDtypeStruct((8, dm), F32)],
        scratch_shapes=[pltpu.VMEM((ts, dm), F32)],
        compiler_params=_params(("arbitrary", "arbitrary")),
    )(dp, w_in, x, dx_res, modv, norm_g)


def _w_in_grad(h_t, dp, ts, tiles, name, prev=None, after=()):
    dm, s_len = h_t.shape
    ct = _wide_col_tile(dp.shape[1])
    lo, hi = tiles
    extra = ([prev] if prev is not None else []) + list(after)

    def body(h_ref, p_ref, *rest):
        o_ref = rest[-1]

        @pl.when(pl.program_id(1) == 0)
        def _():
            o_ref[...] = jnp.zeros_like(o_ref)

        o_ref[...] += _dot(h_ref[...], p_ref[...])

    return pl.pallas_call(
        body, name=name, grid=(hi - lo, s_len // ts),
        in_specs=[pl.BlockSpec((dm, ts), lambda j, i: (0, i)), pl.BlockSpec((ts, ct), lambda j, i: (i, lo + j))]
        + [ANY] * len(extra),
        out_specs=pl.BlockSpec((dm, ct), lambda j, i: (0, lo + j)),
        out_shape=jax.ShapeDtypeStruct((dm, dp.shape[1]), F32),
        input_output_aliases={2: 0} if prev is not None else {},
        compiler_params=_params(("arbitrary", "arbitrary")),
    )(h_t, dp, *extra)


def _rope_lane_table():
    l64 = jnp.arange(LANES) % 64
    half = ROT_DIM // 2
    inv_freq = ROPE_THETA ** (-(jnp.arange(half, dtype=F32) * 2.0 / ROT_DIM))
    rot = l64 < ROT_DIM
    rows = [jnp.where(rot, inv_freq[l64 % half], 0.0), (l64 < half).astype(F32),
            ((l64 >= half) & rot).astype(F32), rot.astype(F32)]
    return jnp.concatenate([jnp.stack(rows), jnp.zeros((4, LANES), F32)], axis=0)


def _tile_sizes(s_len):
    ts_big = min(1024, s_len // 2)
    ts_mid = 256
    ts_head = 256
    qt = [min(1024, s_len // dil) for dil in DILATIONS]
    return ts_big, ts_mid, ts_head, qt


def kernel(x, c, positions, norm_g, w_ada, b_ada, w_in, conv_w, conv_b, conv_ln_g, conv_ln_b, w_conv_out, w_att_out, w_o, final_g, loss_target, m_norm_g, m_w_ada, m_b_ada, m_w_in, m_conv_w, m_conv_b, m_conv_ln_g, m_conv_ln_b, m_w_conv_out, m_w_att_out, m_w_o, m_final_g, v_norm_g, v_w_ada, v_b_ada, v_w_in, v_conv_w, v_conv_b, v_conv_ln_g, v_conv_ln_b, v_w_conv_out, v_w_att_out, v_w_o, v_final_g):
    s_len, dm = x.shape[1], x.shape[2]
    ts_big, ts_mid, ts_head, qt = _tile_sizes(s_len)
    xi, yi, cidx = _my_place()
    chip = 2 * xi + yi
    batch = 4 * xi + 2 * yi + cidx
    x2d, tgt = x[0], loss_target[0]
    pos = positions.reshape(s_len, 1)
    wa_l, wi_l, cw_l = w_ada[0], w_in[0], conv_w[0]
    wco_l, wao_l, wo_l = w_conv_out[0], w_att_out[0], w_o[0]
    ada_w = wa_l.shape[1]
    cw_cols = cw_l.shape[1]

    kinds = ("col", "row", "col", "row")
    w_bufs = _cast_weights([wi_l, wco_l, wao_l, wo_l], kinds)

    cw_pad = jnp.pad(cw_l, ((0, HALO - CONV_K), (0, 0)))
    small_in = jnp.concatenate([jnp.broadcast_to(c, (8, dm)), cw_pad.reshape(8, dm)], axis=0)
    small = _allgather_small(small_in).reshape(8, 16, dm)
    c_all = small[:, 0, :]
    conv_w_full = jnp.concatenate(
        [small[2 * p, 8:16, :].reshape(HALO, cw_cols) for p in range(4)], axis=1)
    b_l = lax.dynamic_slice(b_ada, (0, chip * ada_w), (1, ada_w))
    mod_parts = _allgather_small(_mod_part(c_all, wa_l, b_l)).reshape(8, 8, ada_w)
    mod_rows = lax.dynamic_index_in_dim(mod_parts, batch, axis=1, keepdims=False)
    mod = jnp.concatenate([mod_rows[2 * p] for p in range(4)], axis=0).reshape(3, dm)
    modv = jnp.concatenate([mod, jnp.zeros((5, dm), F32)], axis=0)

    sems_a, sems_b, w_bufs, token = _gather_weights_start(w_bufs, kinds, modv)
    rope = _rope_coefficients(pos, _rope_lane_table() + token, ts_big)
    h_b, h_t = _norm_modulate(x2d, modv + token[0, 0], norm_g, ts_big)
    (w_in_b,) = _gather_weights_wait(w_bufs, kinds, sems_a, 0, (rope, h_b), "gather_weights_wait_w_in")
    w_in_b = _pass_to_sibling(w_in_b, kinds[0])

    n_conv, n_gate = 3 * dm // COL_TILE, (ATT_W + 2 * dm) // COL_TILE
    ts_proj = min(4 * ts_big, s_len)
    p_conv = _plain_projection(h_b, w_in_b, 0, n_conv, ts_proj, "conv_projection")
    p_gate = _plain_projection(h_b, w_in_b, n_conv + 9, n_gate, ts_proj, "gate_projection")
    qkv_g = [_qkv_projection(h_b, rope, w_in_b, gi, min(2 * ts_big, s_len)) for gi in range(3)]
    w_co_b, w_ao_b, w_o_b = _gather_weights_wait(w_bufs, kinds, sems_b, 1, (qkv_g[2], p_conv, p_gate),
                                                  "gather_weights_wait_rest")
    y_conv, u0, xh, rstd = _conv_forward(p_conv, conv_w_full, conv_b, conv_ln_g, conv_ln_b, w_co_b, ts_mid)
    qkv_flat = [q.reshape(3, s_len, ATT_W) for q in qkv_g]
    o_g, lse_g = [], []
    for gi, dil in enumerate(DILATIONS):
        o, l = _attention_forward(qkv_flat[gi], s_len // dil, qt[gi], "attention_forward_%d" % dil)
        o_g.append(o.reshape(dil, s_len // dil, ATT_W))
        lse_g.append(l.reshape(dil, s_len // dil, ATT_W))

    (loss_p, dx_res, dyc, dp, da0, da1, da2, ls0, ls1, ls2, de0, de1, de2,
     g_wo, g_wao, head_sums) = _merge_and_head(o_g, lse_g, p_gate, y_conv, x2d, tgt, modv,
                                               final_g.reshape(1, dm), w_ao_b, w_o_b, ts_head)

    dp, g_wco, conv_sums, g_cw = _conv_backward(dp, dyc, p_conv, u0, xh, rstd, conv_w_full, conv_ln_g, conv_ln_b,
                                                w_co_b, ts_mid)
    dqkv_g = []
    for gi, (dil, da, ls, de) in enumerate(zip(DILATIONS, (da0, da1, da2), (ls0, ls1, ls2), (de0, de1, de2))):
        flat = lambda a: a.reshape(s_len, ATT_W)
        dqkv = _attention_backward(qkv_flat[gi], flat(da), flat(ls), flat(de), s_len // dil, qt[gi],
                                   "attention_backward_%d" % dil)
        dqkv_g.append(dqkv.reshape(3, dil, s_len // dil, ATT_W))
    dp = _qkv_grad_to_tokens(dp, dqkv_g, rope, ts_big)
    ts_wg = min(2 * ts_big, s_len)
    n_wt = dp.shape[1] // _wide_col_tile(dp.shape[1])
    split_col = (n_wt // 2) * _wide_col_tile(dp.shape[1])
    g_first = _w_in_grad(h_t, dp, ts_wg, (0, n_wt // 2), "w_in_grad_first")
    grads = [g_first, g_wco, g_wao, g_wo]
    full_shapes = [g.shape for g in grads]
    pas = [pltpu.with_memory_space_constraint(
        lax.empty((r // 2, c_) if kind == "col" else (r, c_ // 2), F32), pltpu.HBM)
        for (r, c_), kind in zip(full_shapes, kinds)]
    pe_s, pe_r, grads, pas, pe_token = _pair_exchange_start(grads, pas, kinds, (0, split_col))
    grads[0] = _w_in_grad(h_t, dp, ts_wg, (n_wt // 2, n_wt), "w_in_grad_second", prev=grads[0], after=(pe_token,))
    grads, recv_halves = _pair_exchange_wait(pe_s, pe_r, grads, pas, kinds, (0, split_col))
    recv_halves[0] = _pair_exchange_rest(grads[0], recv_halves[0], kinds[0], (split_col, dp.shape[1]))

    c_arr = jnp.reshape(cidx, (1,)).astype(jnp.int32)
    halves = [_reduce_pair_sum(g, pa, kind, c_arr, "reduce_pair_sum_%d" % k)
              for k, (g, pa, kind) in enumerate(zip(grads, recv_halves, kinds))]
    send_sems, recv_sems, halves, lands, token = _reduce_to_owner_start(halves, kinds, full_shapes)
    grad_x, in_sums = _input_grad(dp, w_in_b, x2d, dx_res, modv + token[0, 0], norm_g, ts_big)
    halves, recvd = _reduce_to_owner_wait(send_sems, recv_sems, halves, lands, in_sums, kinds)
    gr_win, gr_wco, gr_wao, gr_wo = _reduce_finish(halves, recvd, kinds, full_shapes)

    rows = [in_sums[2:3], conv_sums[2:3], conv_sums[0:1], conv_sums[1:2], head_sums[0:1],
            in_sums[0:1], in_sums[1:2], head_sums[1:2], g_cw, jnp.pad(loss_p, ((0, 0), (0, dm - LANES)))]
    part = jnp.concatenate(rows, axis=0)
    buf = lax.dynamic_update_slice(jnp.zeros((8 * 48, dm), F32), part, (batch * 48, 0))
    row_sems_s, row_sems_r, buf, row_token = _row_gather_start(buf, 48, gr_win)
    upd = {
        "w_in": _adamw(wi_l, gr_win, m_w_in[0], v_w_in[0], "adamw_w_in", (row_token,)),
        "w_co": _adamw(wco_l, gr_wco, m_w_conv_out[0], v_w_conv_out[0], "adamw_w_conv_out", (row_token,)),
        "w_ao": _adamw(wao_l, gr_wao, m_w_att_out[0], v_w_att_out[0], "adamw_w_att_out", (row_token,)),
        "w_o": _adamw(wo_l, gr_wo, m_w_o[0], v_w_o[0], "adamw_w_o", (row_token,)),
    }
    gathered = _row_gather_wait(row_sems_s, row_sems_r, buf, 48, [upd[k][0] for k in ("w_in", "w_co", "w_ao", "w_o")])
    tot = _sum_devices(gathered, 48)
    loss = tot[8 + HALO, 0]
    dmod_all = gathered.reshape(8, 48, dm)[:, 5:8, :].reshape(8, 3 * dm)
    dmod_l = lax.dynamic_slice(dmod_all, (0, chip * ada_w), (8, ada_w))
    gr_wada = _grad_w_ada(c_all.T, dmod_l)
    gr_cw = lax.dynamic_slice(tot[8:8 + HALO], (0, chip * cw_cols), (HALO, cw_cols))

    pad_cw = lambda a: jnp.pad(a[0], ((0, HALO - CONV_K), (0, 0)))
    row = lambda a: a.reshape(1, dm)
    small_upd = _adamw_small(
        [norm_g, conv_b, conv_ln_g, conv_ln_b, row(final_g), b_ada], tot[0:8],
        [m_norm_g, m_conv_b, m_conv_ln_g, m_conv_ln_b, row(m_final_g), m_b_ada],
        [v_norm_g, v_conv_b, v_conv_ln_g, v_conv_ln_b, row(v_final_g), v_b_ada])
    upd["w_ada"] = _adamw(wa_l, gr_wada, m_w_ada[0], v_w_ada[0], "adamw_w_ada")
    upd["conv_w"] = _adamw(cw_pad, gr_cw, pad_cw(m_conv_w), pad_cw(v_conv_w), "adamw_conv_w")

    def family(which):
        if which is None:
            sm = small_upd[0:6]
            big = {"w_ada": gr_wada, "w_in": gr_win, "conv_w": gr_cw, "w_co": gr_wco, "w_ao": gr_wao, "w_o": gr_wo}
        else:
            sm = small_upd[6 * (which + 1):6 * (which + 2)]
            big = {k: upd[k][which] for k in ("w_ada", "w_in", "conv_w", "w_co", "w_ao", "w_o")}
        return [sm[0], big["w_ada"][None], sm[5], big["w_in"][None],
                big["conv_w"][None, :CONV_K], sm[1], sm[2], sm[3], big["w_co"][None],
                big["w_ao"][None], big["w_o"][None], sm[4].reshape(dm)]

    return (loss, grad_x[None], *family(None), *family(0), *family(1), *family(2))
```

```python
import jax
import jax.numpy as jnp
from jax import lax
from jax.experimental import pallas as pl
from jax.experimental.pallas import tpu as pltpu

F32 = jnp.float32
BF16 = jnp.bfloat16
MESH = pl.DeviceIdType.MESH
ANY = pl.BlockSpec(memory_space=pl.ANY)
VM = pl.BlockSpec(memory_space=pltpu.VMEM)

EPS = 1e-6
NEG = -1e30
ATT_W = 512
DILATIONS = (1, 4, 16)
BLK = 128
CONV_K = 31
HALO = 32
CONV_ROWS_FWD = 32
CONV_ROWS_BWD = 16
ROT_DIM = 16
ROPE_THETA = 500000.0
COL_TILE = 512
LANES = 128
VMEM_LIMIT = 56 * 1024 * 1024

ADAM_LR, ADAM_B1, ADAM_B2, ADAM_EPS, ADAM_WD, ADAM_STEP = 0.001, 0.9, 0.999, 1e-08, 0.01, 10


def _params(sem=None, vmem=VMEM_LIMIT):
    return pltpu.CompilerParams(dimension_semantics=sem, vmem_limit_bytes=vmem)


def _dot(a, b):
    return jnp.dot(a, b, preferred_element_type=F32)


def _dot_nt(a, b):
    return lax.dot_general(a, b, (((1,), (1,)), ((), ())), preferred_element_type=F32)


def _dot_tn(a, b):
    return lax.dot_general(a, b, (((0,), (0,)), ((), ())), preferred_element_type=F32)


def _sig(x):
    return jax.nn.sigmoid(x)


def _dsilu(x, s):
    return s * (1.0 + x * (1.0 - s))


def _my_place():
    return lax.axis_index("x"), lax.axis_index("y"), lax.axis_index("c")


def _allgather_small(x_shard):
    m_per, n = x_shard.shape

    def body(x_ref, out_ref, send_sems, recv_sems, local_sem):
        x, y, c = _my_place()
        me, sibling = (x, y, c), (x, y, 1 - c)
        chips = [(1 - x, y), (x, 1 - y), (1 - x, 1 - y)]

        def rows(px, py, pc):
            return out_ref.at[pl.ds((4 * px + 2 * py + pc) * m_per, m_per), :]

        def copy(k, block, to, src=None):
            return pltpu.make_async_remote_copy(
                src_ref=rows(*block) if src is None else src, dst_ref=rows(*block),
                send_sem=send_sems.at[k], recv_sem=recv_sems.at[k],
                device_id=to, device_id_type=MESH)

        mine = pltpu.make_async_copy(x_ref, rows(*me), local_sem)
        mine.start()
        first = [copy(0, me, sibling, src=x_ref)]
        first += [copy(1 + j, me, (*chip, c), src=x_ref) for j, chip in enumerate(chips)]
        for cp in first:
            cp.start()
        passed = [copy(4 + j, (*chip, c), sibling) for j, chip in enumerate(chips)]
        for j, chip in enumerate(chips):
            copy(1 + j, (*chip, c), me).wait_recv()
            passed[j].start()
        copy(0, sibling, me).wait_recv()
        for j, chip in enumerate(chips):
            copy(4 + j, (*chip, 1 - c), me).wait_recv()
        for cp in first + passed:
            cp.wait_send()
        mine.wait()

    return pl.pallas_call(
        body, name="allgather_small",
        out_shape=jax.ShapeDtypeStruct((8 * m_per, n), x_shard.dtype),
        in_specs=[VM], out_specs=VM,
        scratch_shapes=[pltpu.SemaphoreType.DMA((7,)), pltpu.SemaphoreType.DMA((7,)),
                        pltpu.SemaphoreType.DMA],
    )(x_shard)


def _shard_window(ref, kind, p, n_shards=4):
    r, c = ref.shape
    if kind == "col":
        w = c // n_shards
        return ref.at[:, pl.ds(p * w, w)]
    w = r // n_shards
    return ref.at[pl.ds(p * w, w), :]


def _half_window(ref, kind, hc):
    r, c = ref.shape
    if kind == "col":
        return ref.at[pl.ds(hc * (r // 2), r // 2), :]
    return ref.at[:, pl.ds(hc * (c // 2), c // 2)]


def _landed(ref, kind, chip, hc):
    return _half_window(_shard_window(ref, kind, 2 * chip[0] + chip[1]), kind, hc)


def _cast_weights(shards, kinds):
    n = len(shards)
    full_shapes = [(s.shape[0], 4 * s.shape[1]) if kind == "col" else (4 * s.shape[0], s.shape[1])
                   for s, kind in zip(shards, kinds)]

    def body(*refs):
        w_refs, out_refs, bf_refs, sems = refs[:n], refs[n:2 * n], refs[2 * n:3 * n], refs[3 * n]
        x, y, _ = _my_place()
        cps = []
        for k in range(n):
            bf_refs[k][...] = w_refs[k][...].astype(BF16)
            cp = pltpu.make_async_copy(bf_refs[k], _shard_window(out_refs[k], kinds[k], 2 * x + y), sems.at[k])
            cp.start()
            cps.append(cp)
        for cp in cps:
            cp.wait()

    return pl.pallas_call(
        body, name="cast_weights",
        out_shape=[jax.ShapeDtypeStruct(s, BF16) for s in full_shapes],
        in_specs=[VM] * n, out_specs=[ANY] * n,
        scratch_shapes=[pltpu.VMEM(s.shape, BF16) for s in shards] + [pltpu.SemaphoreType.DMA((n,))],
        compiler_params=_params(),
    )(*shards)


def _gather_copies(refs, kinds, sems_a, sems_b):
    x, y, c = _my_place()
    chips = [(1 - x, y), (x, 1 - y), (1 - x, 1 - y)]
    set_a, set_b = [], []
    for j, chip in enumerate(chips):
        for k in range(len(refs)):
            if refs[k] is None:
                continue
            for flip in ((0,) if k == 0 else (0, 1)):
                win = _landed(refs[k], kinds[k], (x, y), c)
                sems, idx = (sems_a, j) if k == 0 else (sems_b, ((k - 1) * 3 + j) * 2 + flip)
                mine = _landed(refs[k], kinds[k], chip, (c + flip) % 2)
                (set_a if k == 0 else set_b).append((
                    pltpu.make_async_remote_copy(
                        src_ref=win, dst_ref=win, send_sem=sems[0].at[idx], recv_sem=sems[1].at[idx],
                        device_id=(*chip, (c + flip) % 2), device_id_type=MESH),
                    pltpu.make_async_remote_copy(
                        src_ref=mine, dst_ref=mine, send_sem=sems[0].at[idx], recv_sem=sems[1].at[idx],
                        device_id=(*chip, (c + flip) % 2), device_id_type=MESH)))
    return set_a, set_b


_SPLIT = dict(has_side_effects=pltpu.SideEffectType.DATAFLOW_SIDE_EFFECTING)


def _gather_weights_start(fulls, kinds, after):
    n = len(fulls)
    hbm = pl.BlockSpec(memory_space=pltpu.HBM)
    sem = pl.BlockSpec(memory_space=pltpu.SEMAPHORE)
    nb = (n - 1) * 6

    def body(*refs):
        in_refs = refs[:n]
        sa, ra, sb, rb = refs[n + 1:n + 5]
        token = refs[-1]
        set_a, set_b = _gather_copies(in_refs, kinds, (sa, ra), (sb, rb))
        for out_cp, _ in set_a + set_b:
            out_cp.start()
        token[...] = jnp.zeros_like(token)

    out = pl.pallas_call(
        body, name="gather_weights_start",
        out_shape=[pltpu.SemaphoreType.DMA((3,)), pltpu.SemaphoreType.DMA((3,)),
                   pltpu.SemaphoreType.DMA((nb,)), pltpu.SemaphoreType.DMA((nb,))]
        + [pltpu.HBM(f.shape, f.dtype) for f in fulls] + [jax.ShapeDtypeStruct((8, LANES), F32)],
        in_specs=[hbm] * n + [ANY], out_specs=[sem] * 4 + [hbm] * n + [VM],
        input_output_aliases={k: 4 + k for k in range(n)},
        compiler_params=pltpu.CompilerParams(**_SPLIT),
    )(*[pltpu.with_memory_space_constraint(f, pltpu.HBM) for f in fulls], after)
    return (out[0], out[1]), (out[2], out[3]), out[4:4 + n], out[-1]


def _gather_weights_wait(fulls, kinds, sems, which, after, name):
    n = len(fulls)
    hbm = pl.BlockSpec(memory_space=pltpu.HBM)
    sem = pl.BlockSpec(memory_space=pltpu.SEMAPHORE)
    keep = [0] if which == 0 else list(range(1, n))

    def body(*refs):
        m = len(keep)
        full_refs = [None] * n
        for pos_, k in enumerate(keep):
            full_refs[k] = refs[pos_]
        s_ref, r_ref = refs[m:m + 2]
        if which == 0:
            sets = _gather_copies([full_refs[0]], kinds[:1], (s_ref, r_ref), None)[0]
        else:
            sets = _gather_copies([None] + [full_refs[k] for k in keep], kinds, None, (s_ref, r_ref))[1]
        for out_cp, in_cp in sets:
            out_cp.wait_send()
            in_cp.wait_recv()

    out = pl.pallas_call(
        body, name=name,
        out_shape=[pltpu.HBM(fulls[k].shape, fulls[k].dtype) for k in keep],
        in_specs=[hbm] * len(keep) + [sem, sem] + [ANY] * len(after), out_specs=[hbm] * len(keep),
        input_output_aliases={i: i for i in range(len(keep))},
        compiler_params=pltpu.CompilerParams(**_SPLIT),
    )(*[fulls[k] for k in keep], sems[0], sems[1], *after)
    return list(out)


def _row_gather_copies(buf, m_per, send_sems, recv_sems):
    x, y, c = _my_place()
    pairs = []
    for idx in range(1, 8):
        dx, dy, dc = idx // 4, (idx // 2) % 2, idx % 2
        peer = ((x + dx) % 2, (y + dy) % 2, (c + dc) % 2)
        mine = buf.at[pl.ds((4 * x + 2 * y + c) * m_per, m_per), :]
        theirs = buf.at[pl.ds((4 * peer[0] + 2 * peer[1] + peer[2]) * m_per, m_per), :]
        pairs.append(tuple(
            pltpu.make_async_remote_copy(src_ref=w, dst_ref=w, send_sem=send_sems.at[idx - 1],
                                         recv_sem=recv_sems.at[idx - 1], device_id=peer, device_id_type=MESH)
            for w in (mine, theirs)))
    return pairs


def _row_gather_start(buf, m_per, after):
    hbm = pl.BlockSpec(memory_space=pltpu.HBM)
    sem = pl.BlockSpec(memory_space=pltpu.SEMAPHORE)

    def body(buf_ref, after_ref, send_sems, recv_sems, out_ref, token):
        del after_ref, out_ref
        for out_cp, _ in _row_gather_copies(buf_ref, m_per, send_sems, recv_sems):
            out_cp.start()
        token[...] = jnp.zeros_like(token)

    return pl.pallas_call(
        body, name="row_gather_start",
        out_shape=[pltpu.SemaphoreType.DMA((7,)), pltpu.SemaphoreType.DMA((7,)), pltpu.HBM(buf.shape, buf.dtype),
                   jax.ShapeDtypeStruct((8, LANES), F32)],
        in_specs=[hbm, ANY], out_specs=[sem, sem, hbm, VM], input_output_aliases={0: 2},
        compiler_params=pltpu.CompilerParams(**_SPLIT),
    )(pltpu.with_memory_space_constraint(buf, pltpu.HBM), after)


def _row_gather_wait(send_sems, recv_sems, buf, m_per, after):
    hbm = pl.BlockSpec(memory_space=pltpu.HBM)
    sem = pl.BlockSpec(memory_space=pltpu.SEMAPHORE)

    def body(buf_ref, send_s, recv_s, *rest):
        for out_cp, in_cp in _row_gather_copies(buf_ref, m_per, send_s, recv_s):
            out_cp.wait_send()
            in_cp.wait_recv()

    return pl.pallas_call(
        body, name="row_gather_wait", out_shape=pltpu.HBM(buf.shape, buf.dtype),
        in_specs=[hbm, sem, sem] + [ANY] * len(after), out_specs=hbm, input_output_aliases={0: 0},
        compiler_params=pltpu.CompilerParams(**_SPLIT),
    )(buf, send_sems, recv_sems, *after)


def _pass_to_sibling(w_full, kind):
    def body(w_in_ref, w_ref, send_sems, recv_sems):
        del w_in_ref
        x, y, c = _my_place()
        chips = [(1 - x, y), (x, 1 - y), (1 - x, 1 - y)]
        cps = []
        for j, chip in enumerate(chips):
            win = _landed(w_ref, kind, chip, c)
            cp = pltpu.make_async_remote_copy(
                src_ref=win, dst_ref=win, send_sem=send_sems.at[j], recv_sem=recv_sems.at[j],
                device_id=(x, y, 1 - c), device_id_type=MESH)
            cp.start()
            cps.append(cp)
        for j, chip in enumerate(chips):
            theirs = _landed(w_ref, kind, chip, 1 - c)
            pltpu.make_async_remote_copy(
                src_ref=theirs, dst_ref=theirs, send_sem=send_sems.at[j], recv_sem=recv_sems.at[j],
                device_id=(x, y, 1 - c), device_id_type=MESH).wait_recv()
        for cp in cps:
            cp.wait_send()

    return pl.pallas_call(
        body, name="pass_to_sibling",
        out_shape=jax.ShapeDtypeStruct(w_full.shape, w_full.dtype),
        in_specs=[ANY], out_specs=ANY, input_output_aliases={0: 0},
        scratch_shapes=[pltpu.SemaphoreType.DMA((3,)), pltpu.SemaphoreType.DMA((3,))],
    )(w_full)


def _pair_copies(g_refs, pa_refs, kinds, send_sems, recv_sems, cols, with_rest):
    x, y, c = _my_place()
    lo, hi = cols

    def copy(k, src, dst):
        return pltpu.make_async_remote_copy(src_ref=src, dst_ref=dst, send_sem=send_sems.at[k], recv_sem=recv_sems.at[k],
                                            device_id=(x, y, 1 - c), device_id_type=MESH)

    cps = [copy(0, _half_window(g_refs[0], kinds[0], 1 - c).at[:, pl.ds(lo, hi - lo)],
                pa_refs[0].at[:, pl.ds(lo, hi - lo)])]
    if with_rest:
        cps += [copy(k, _half_window(g_refs[k], kinds[k], 1 - c), pa_refs[k]) for k in range(1, len(g_refs))]
    return cps


def _pair_exchange_start(grads, pas, kinds, cols):
    n = len(grads)
    hbm = pl.BlockSpec(memory_space=pltpu.HBM)
    sem = pl.BlockSpec(memory_space=pltpu.SEMAPHORE)

    def body(*refs):
        g_refs, pa_refs = refs[:n], refs[n:2 * n]
        send_sems, recv_sems = refs[2 * n:2 * n + 2]
        for cp in _pair_copies(g_refs, pa_refs, kinds, send_sems, recv_sems, cols, True):
            cp.start()
        refs[-1][...] = jnp.zeros_like(refs[-1])

    out = pl.pallas_call(
        body, name="pair_exchange_start",
        out_shape=[pltpu.SemaphoreType.DMA((n,)), pltpu.SemaphoreType.DMA((n,))]
        + [pltpu.HBM(a.shape, a.dtype) for a in list(grads) + list(pas)] + [jax.ShapeDtypeStruct((8, LANES), F32)],
        in_specs=[hbm] * (2 * n), out_specs=[sem, sem] + [hbm] * (2 * n) + [VM],
        input_output_aliases={k: 2 + k for k in range(2 * n)},
        compiler_params=pltpu.CompilerParams(**_SPLIT),
    )(*[pltpu.with_memory_space_constraint(a, pltpu.HBM) for a in list(grads) + list(pas)])
    return out[0], out[1], list(out[2:2 + n]), list(out[2 + n:2 + 2 * n]), out[-1]


def _pair_exchange_wait(send_sems, recv_sems, grads, pas, kinds, cols):
    n = len(grads)
    hbm = pl.BlockSpec(memory_space=pltpu.HBM)
    sem = pl.BlockSpec(memory_space=pltpu.SEMAPHORE)

    def body(*refs):
        g_refs, pa_refs = refs[:n], refs[n:2 * n]
        send_s, recv_s = refs[2 * n:2 * n + 2]
        for cp in _pair_copies(g_refs, pa_refs, kinds, send_s, recv_s, cols, True):
            cp.wait_send()
            cp.wait_recv()

    out = pl.pallas_call(
        body, name="pair_exchange_wait",
        out_shape=[pltpu.HBM(a.shape, a.dtype) for a in list(grads) + list(pas)],
        in_specs=[hbm] * (2 * n) + [sem, sem], out_specs=[hbm] * (2 * n),
        input_output_aliases={k: k for k in range(2 * n)},
        compiler_params=pltpu.CompilerParams(**_SPLIT),
    )(*grads, *pas, send_sems, recv_sems)
    return list(out[:n]), list(out[n:])


def _pair_exchange_rest(g, pa, kind, cols):
    def body(g_ref, pa_in, pa_ref, send_sems, recv_sems):
        del pa_in
        (cp,) = _pair_copies([g_ref], [pa_ref], [kind], send_sems, recv_sems, cols, False)
        cp.start()
        cp.wait()

    return pl.pallas_call(
        body, name="pair_exchange_rest", out_shape=jax.ShapeDtypeStruct(pa.shape, pa.dtype),
        in_specs=[ANY, ANY], out_specs=ANY, input_output_aliases={1: 0},
        scratch_shapes=[pltpu.SemaphoreType.DMA((1,)), pltpu.SemaphoreType.DMA((1,))],
    )(g, pa)


def _row_tile(rows, cols, itemsize=4, target=2 * 1024 * 1024, mult=16):
    t = rows
    while t % 2 == 0 and t // 2 >= mult and (t // 2) % mult == 0 and t * cols * itemsize > target:
        t //= 2
    return t


def _reduce_pair_sum(g, pa, kind, c_arr, name):
    hr, hc_ = pa.shape
    tr = _row_tile(hr, hc_)
    nb = hr // tr

    def body(c_ref, g_ref, pa_ref, o_ref):
        o_ref[...] = (g_ref[...] + pa_ref[...]).astype(BF16)

    if kind == "col":
        g_map = lambda i, c_ref: (c_ref[0] * nb + i, 0)
    else:
        g_map = lambda i, c_ref: (i, c_ref[0])
    return pl.pallas_call(
        body, name=name,
        grid_spec=pltpu.PrefetchScalarGridSpec(
            num_scalar_prefetch=1, grid=(nb,),
            in_specs=[pl.BlockSpec((tr, hc_), g_map), pl.BlockSpec((tr, hc_), lambda i, c_ref: (i, 0))],
            out_specs=pl.BlockSpec((tr, hc_), lambda i, c_ref: (i, 0))),
        out_shape=jax.ShapeDtypeStruct((hr, hc_), BF16),
        compiler_params=_params(("parallel",)),
    )(c_arr, g, pa)


def _half_shard_shape(full_shape, kind):
    r, c = full_shape
    return (r // 2, c // 4) if kind == "col" else (r // 4, c // 2)


def _to_owner_copies(h_refs, land_refs, send_sems, recv_sems, kinds):
    n = len(h_refs)
    x, y, c = _my_place()
    chips = [(1 - x, y), (x, 1 - y), (1 - x, 1 - y)]
    cps = []
    for j, chip in enumerate(chips):
        pj = 2 * chip[0] + chip[1]
        for k in range(n):
            cps.append(pltpu.make_async_remote_copy(
                src_ref=_shard_window(h_refs[k], kinds[k], pj), dst_ref=land_refs[k].at[j],
                send_sem=send_sems.at[j * n + k], recv_sem=recv_sems.at[j * n + k],
                device_id=(*chip, c), device_id_type=MESH))
    return cps


def _reduce_to_owner_start(halves, kinds, full_shapes):
    n = len(halves)
    hs = [_half_shard_shape(fs, kind) for fs, kind in zip(full_shapes, kinds)]
    hbm = pl.BlockSpec(memory_space=pltpu.HBM)
    sem = pl.BlockSpec(memory_space=pltpu.SEMAPHORE)

    def body(*refs):
        h_refs, land_refs = refs[:n], refs[n:2 * n]
        send_sems, recv_sems = refs[2 * n:2 * n + 2]
        token = refs[-1]
        for cp in _to_owner_copies(h_refs, land_refs, send_sems, recv_sems, kinds):
            cp.start()
        token[...] = jnp.zeros_like(token)

    lands = [pltpu.with_memory_space_constraint(lax.empty((3,) + s, BF16), pltpu.HBM) for s in hs]
    out = pl.pallas_call(
        body, name="reduce_to_owner_start",
        out_shape=[pltpu.SemaphoreType.DMA((3 * n,)), pltpu.SemaphoreType.DMA((3 * n,))]
        + [pltpu.HBM(h.shape, h.dtype) for h in halves] + [pltpu.HBM((3,) + s, BF16) for s in hs]
        + [jax.ShapeDtypeStruct((8, LANES), F32)],
        in_specs=[hbm] * (2 * n), out_specs=[sem, sem] + [hbm] * (2 * n) + [VM],
        input_output_aliases={k: 2 + k for k in range(2 * n)},
        compiler_params=pltpu.CompilerParams(has_side_effects=pltpu.SideEffectType.DATAFLOW_SIDE_EFFECTING),
    )(*[pltpu.with_memory_space_constraint(h, pltpu.HBM) for h in halves], *lands)
    return out[0], out[1], out[2:2 + n], out[2 + n:2 + 2 * n], out[-1]


def _reduce_to_owner_wait(send_sems, recv_sems, halves, lands, after, kinds):
    n = len(halves)
    hbm = pl.BlockSpec(memory_space=pltpu.HBM)
    sem = pl.BlockSpec(memory_space=pltpu.SEMAPHORE)

    def body(*refs):
        h_refs, land_refs = refs[:n], refs[n:2 * n]
        send_s, recv_s = refs[2 * n:2 * n + 2]
        for cp in _to_owner_copies(h_refs, land_refs, send_s, recv_s, kinds):
            cp.wait_send()
            cp.wait_recv()

    out = pl.pallas_call(
        body, name="reduce_to_owner_wait",
        out_shape=[pltpu.HBM(h.shape, h.dtype) for h in halves] + [pltpu.HBM(l.shape, l.dtype) for l in lands],
        in_specs=[hbm] * (2 * n) + [sem, sem, ANY], out_specs=[hbm] * (2 * n),
        input_output_aliases={k: k for k in range(2 * n)},
        compiler_params=pltpu.CompilerParams(has_side_effects=pltpu.SideEffectType.DATAFLOW_SIDE_EFFECTING),
    )(*halves, *lands, send_sems, recv_sems, after)
    return out[:n], out[n:]


def _reduce_finish(halves, recvd, kinds, full_shapes):
    n = len(halves)
    hs = [_half_shard_shape(fs, kind) for fs, kind in zip(full_shapes, kinds)]
    shard_shapes = [(fs[0], fs[1] // 4) if kind == "col" else (fs[0] // 4, fs[1])
                    for fs, kind in zip(full_shapes, kinds)]

    def body(*refs):
        h_refs, rc_refs, gs_refs = refs[:n], refs[n:2 * n], refs[2 * n:3 * n]
        own_refs, gh_refs = refs[3 * n:4 * n], refs[4 * n:5 * n]
        in_sems, loc_sems, send_sems, recv_sems = refs[5 * n:]
        x, y, c = _my_place()
        p = 2 * x + y
        loads = []
        for k in range(n):
            cp = pltpu.make_async_copy(_shard_window(h_refs[k], kinds[k], p), own_refs[k], in_sems.at[k])
            cp.start()
            loads.append(cp)
        outs = []
        for k in range(n):
            loads[k].wait()
            gh_refs[k][...] = (own_refs[k][...].astype(F32) + rc_refs[k][0].astype(F32)
                               + rc_refs[k][1].astype(F32) + rc_refs[k][2].astype(F32))
            dst = _half_window(gs_refs[k], kinds[k], c)
            lc = pltpu.make_async_copy(gh_refs[k], dst, loc_sems.at[k])
            lc.start()
            rc = pltpu.make_async_remote_copy(
                src_ref=gh_refs[k], dst_ref=dst, send_sem=send_sems.at[k], recv_sem=recv_sems.at[k],
                device_id=(x, y, 1 - c), device_id_type=MESH)
            rc.start()
            outs.append((lc, rc))
        for k, (lc, rc) in enumerate(outs):
            lc.wait()
            rc.wait_send()
            pltpu.make_async_remote_copy(
                src_ref=gh_refs[k], dst_ref=_half_window(gs_refs[k], kinds[k], 1 - c),
                send_sem=send_sems.at[k], recv_sem=recv_sems.at[k],
                device_id=(x, y, 1 - c), device_id_type=MESH).wait_recv()

    return pl.pallas_call(
        body, name="reduce_finish",
        out_shape=[jax.ShapeDtypeStruct(s, F32) for s in shard_shapes],
        in_specs=[ANY] * n + [VM] * n, out_specs=[ANY] * n,
        scratch_shapes=[pltpu.VMEM(s, BF16) for s in hs] + [pltpu.VMEM(s, F32) for s in hs]
        + [pltpu.SemaphoreType.DMA((n,)) for _ in range(4)],
        compiler_params=_params(),
    )(*halves, *recvd)


def _mod_part(c_all, w_ada_l, b_l):
    def body(c_ref, w_ref, b_ref, o_ref):
        o_ref[...] = _dot(c_ref[...].astype(BF16), w_ref[...].astype(BF16)) + b_ref[...]

    return pl.pallas_call(
        body, name="mod_part", out_shape=jax.ShapeDtypeStruct((8, w_ada_l.shape[1]), F32),
        in_specs=[VM, VM, VM], out_specs=VM, compiler_params=_params(),
    )(c_all, w_ada_l, b_l)


def _sum_devices(parts, m):
    def body(p_ref, o_ref):
        acc = p_ref[0:m, :]
        for d in range(1, 8):
            acc = acc + p_ref[d * m:(d + 1) * m, :]
        o_ref[...] = acc

    return pl.pallas_call(
        body, name="sum_devices", out_shape=jax.ShapeDtypeStruct((m, parts.shape[1]), F32),
        in_specs=[VM], out_specs=VM, compiler_params=_params(),
    )(parts)


def _grad_w_ada(c_all_t, dmod_l):
    d, w = c_all_t.shape[0], dmod_l.shape[1]

    def body(ct_ref, dm_ref, o_ref):
        acc = ct_ref[:, 0:1] * dm_ref[0:1, :]
        for b in range(1, 8):
            acc = acc + ct_ref[:, b:b + 1] * dm_ref[b:b + 1, :]
        o_ref[...] = acc

    return pl.pallas_call(
        body, name="grad_w_ada", out_shape=jax.ShapeDtypeStruct((d, w), F32),
        in_specs=[VM, VM], out_specs=VM, compiler_params=_params(),
    )(c_all_t, dmod_l)


def _adamw_math(w, g, m, v):
    nm = ADAM_B1 * m + (1.0 - ADAM_B1) * g
    nv = ADAM_B2 * v + (1.0 - ADAM_B2) * (g * g)
    m_hat = nm / (1.0 - ADAM_B1 ** ADAM_STEP)
    v_hat = nv / (1.0 - ADAM_B2 ** ADAM_STEP)
    return -ADAM_LR * (m_hat / (jnp.sqrt(v_hat) + ADAM_EPS) + ADAM_WD * w), nm, nv


def _adamw_small(ws, g8, ms, vs):
    shapes = [jax.ShapeDtypeStruct(w.shape, F32) for w in ws]

    def body(*refs):
        w_refs, g_ref, m_refs, v_refs, outs = refs[0:6], refs[6], refs[7:13], refs[13:19], refs[19:]
        for idx in range(6):
            if idx < 5:
                g = g_ref[idx:idx + 1, :]
            else:
                g = jnp.concatenate([g_ref[5:6, :], g_ref[6:7, :], g_ref[7:8, :]], axis=1)
            res = (g,) + _adamw_math(w_refs[idx][...], g, m_refs[idx][...], v_refs[idx][...])
            for fam in range(4):
                outs[6 * fam + idx][...] = res[fam]

    return pl.pallas_call(
        body, name="adamw_small", out_shape=shapes * 4,
        in_specs=[VM] * 19, out_specs=[VM] * 24, compiler_params=_params(),
    )(*ws, g8, *ms, *vs)


def _adamw(w, g, m, v, name, after=()):
    r, c = w.shape
    tr = _row_tile(r, c, target=1024 * 1024, mult=8)

    def body(w_ref, g_ref, m_ref, v_ref, *rest):
        d_ref, nm_ref, nv_ref = rest[len(after):]
        d_ref[...], nm_ref[...], nv_ref[...] = _adamw_math(w_ref[...], g_ref[...], m_ref[...], v_ref[...])

    spec = pl.BlockSpec((tr, c), lambda i: (i, 0))
    return pl.pallas_call(
        body, name=name, grid=(r // tr,),
        out_shape=[jax.ShapeDtypeStruct((r, c), F32)] * 3,
        in_specs=[spec] * 4 + [ANY] * len(after), out_specs=[spec] * 3,
        compiler_params=_params(("parallel",)),
    )(w, g, m, v, *after)


def _rope_coefficients(pos, ropetab, ts):
    s_len = pos.shape[0]

    def body(pos_ref, tab_ref, o_ref):
        ang = pos_ref[...].astype(F32) * tab_ref[0:1, :]
        cs, sn = jnp.cos(ang), jnp.sin(ang)
        o_ref[0] = jnp.where(tab_ref[3:4, :] > 0, cs, 1.0)
        o_ref[1] = -sn * tab_ref[1:2, :]
        o_ref[2] = sn * tab_ref[2:3, :]

    return pl.pallas_call(
        body, name="rope_coefficients", grid=(s_len // ts,),
        in_specs=[pl.BlockSpec((ts, 1), lambda i: (i, 0)), pl.BlockSpec((8, LANES), lambda i: (0, 0))],
        out_specs=pl.BlockSpec((3, ts, LANES), lambda i: (0, i, 0)),
        out_shape=jax.ShapeDtypeStruct((3, s_len, LANES), F32),
        compiler_params=_params(("parallel",)),
    )(pos, ropetab)


def _deinterleave_store(vals, slab, out_ref, d, ts, dtype):
    if d == 1:
        for s in range(4):
            out_ref[0, :, s * LANES:(s + 1) * LANES] = vals[s].astype(dtype)
        return
    for s in range(4):
        slab[s] = vals[s]
    if d == 16:
        q = ts // 4
        for s in range(4):
            for r1 in range(4):
                slab[4 + s, r1 * q:(r1 + 1) * q, :] = slab[s, pl.ds(r1, q, stride=4), :]
        for r in range(d):
            r1, r2 = r % 4, r // 4
            for s in range(4):
                out_ref[r, :, s * LANES:(s + 1) * LANES] = slab[
                    4 + s, pl.ds(r1 * q + r2, ts // d, stride=4), :].astype(dtype)
        return
    for r in range(d):
        for s in range(4):
            out_ref[r, :, s * LANES:(s + 1) * LANES] = slab[s, pl.ds(r, ts // d, stride=d), :].astype(dtype)


def _interleave_load(blk_ref, slab, d, ts):
    if d == 1:
        return [blk_ref[0, :, s * LANES:(s + 1) * LANES] for s in range(4)]
    if d == 16:
        q = ts // 4
        for r in range(d):
            r1, r2 = r % 4, r // 4
            for s in range(4):
                slab[4 + s, pl.ds(r1 * q + r2, ts // d, stride=4), :] = blk_ref[r, :, s * LANES:(s + 1) * LANES]
        for s in range(4):
            for r1 in range(4):
                slab[s, pl.ds(r1, q, stride=4), :] = slab[4 + s, r1 * q:(r1 + 1) * q, :]
        return [slab[s] for s in range(4)]
    for r in range(d):
        for s in range(4):
            slab[s, pl.ds(r, ts // d, stride=d), :] = blk_ref[r, :, s * LANES:(s + 1) * LANES]
    return [slab[s] for s in range(4)]


def _norm_modulate(x, modv, norm_g, ts):
    s_len, d_model = x.shape

    def body(x_ref, mod_ref, g_ref, h_ref, ht_ref):
        xv = x_ref[...]
        r = lax.rsqrt(jnp.mean(xv * xv, axis=-1, keepdims=True) + EPS)
        h = (xv * r) * g_ref[...] * (1.0 + mod_ref[1:2, :]) + mod_ref[0:1, :]
        h_ref[...] = h.astype(BF16)
        ht_ref[...] = h.T.astype(BF16)

    return pl.pallas_call(
        body, name="norm_modulate", grid=(s_len // ts,),
        in_specs=[pl.BlockSpec((ts, d_model), lambda i: (i, 0)), pl.BlockSpec((8, d_model), lambda i: (0, 0)),
                  pl.BlockSpec((1, d_model), lambda i: (0, 0))],
        out_specs=[pl.BlockSpec((ts, d_model), lambda i: (i, 0)), pl.BlockSpec((d_model, ts), lambda i: (0, i))],
        out_shape=[jax.ShapeDtypeStruct((s_len, d_model), BF16), jax.ShapeDtypeStruct((d_model, s_len), BF16)],
        compiler_params=_params(("parallel",)),
    )(x, modv, norm_g)


def _plain_projection(h, w_in, first_tile, n_tiles, ts, name):
    s_len, d_model = h.shape
    ct = COL_TILE

    def body(h_ref, w_ref, o_ref):
        o_ref[...] = _dot(h_ref[...], w_ref[...])

    return pl.pallas_call(
        body, name=name, grid=(s_len // ts, n_tiles),
        in_specs=[pl.BlockSpec((ts, d_model), lambda i, j: (i, 0)),
                  pl.BlockSpec((d_model, ct), lambda i, j: (0, first_tile + j))],
        out_specs=pl.BlockSpec((ts, ct), lambda i, j: (i, j)),
        out_shape=jax.ShapeDtypeStruct((s_len, n_tiles * ct), F32),
        compiler_params=_params(("parallel", "arbitrary")),
    )(h, w_in)


def _qkv_projection(h, rope, w_in, gi, ts):
    s_len, d_model = h.shape
    dil = DILATIONS[gi]
    nc = 3 * d_model // COL_TILE
    hr = ts // 2
    assert COL_TILE == ATT_W and hr % (16 * dil) == 0

    def body(h_ref, rope_ref, wq_ref, wk_ref, wv_ref, o_ref, slab):
        rc, ra, rb = rope_ref.at[0], rope_ref.at[1], rope_ref.at[2]
        w_refs = (wq_ref, wk_ref, wv_ref)
        units = [(w, half) for w in range(3) for half in range(2)]

        def matmul(w, half):
            return _dot(h_ref[half * hr:(half + 1) * hr, :], w_refs[w][...])

        def finish(w, half, res):
            rows = slice(half * hr, (half + 1) * hr)
            vals = []
            for s in range(4):
                t = res[:, s * LANES:(s + 1) * LANES]
                if w < 2:
                    t = (t * rc[rows, :] + pltpu.roll(t, LANES - 8, 1) * ra[rows, :]
                         + pltpu.roll(t, 8, 1) * rb[rows, :])
                vals.append(t)
            out = o_ref.at[w, :, half * (hr // dil):(half + 1) * (hr // dil), :]
            _deinterleave_store(vals, slab.at[half], out, dil, hr, BF16)

        res_next = matmul(*units[0])
        for ui, (w, half) in enumerate(units):
            res = res_next
            if ui + 1 < len(units):
                res_next = matmul(*units[ui + 1])
            finish(w, half, res)

    def w_spec(w):
        return pl.BlockSpec((d_model, COL_TILE), lambda i: (0, nc + 3 * w + gi))

    return pl.pallas_call(
        body, name="qkv_projection_%d" % dil, grid=(s_len // ts,),
        in_specs=[pl.BlockSpec((ts, d_model), lambda i: (i, 0)), pl.BlockSpec((3, ts, LANES), lambda i: (0, i, 0)),
                  w_spec(0), w_spec(1), w_spec(2)],
        out_specs=pl.BlockSpec((3, dil, ts // dil, ATT_W), lambda i: (0, 0, i, 0)),
        out_shape=jax.ShapeDtypeStruct((3, dil, s_len // dil, ATT_W), BF16),
        scratch_shapes=[pltpu.VMEM((2, 8, hr, LANES), F32)],
        compiler_params=_params(("parallel",)),
    )(h, rope, w_in, w_in, w_in)


def _layernorm_stats(u1):
    mu = jnp.mean(u1, axis=-1, keepdims=True)
    xc = u1 - mu
    rstd = lax.rsqrt(jnp.mean(xc * xc, axis=-1, keepdims=True) + EPS)
    return xc * rstd, rstd


def _shifted_copies(win, shf, ts):
    rows = ts + HALO - 8
    for b in range(1, 8):
        shf[b - 1, 0:rows, :] = win[pl.ds(b, rows), :]


def _tap(win, shf, off, r0, rows):
    a, b = divmod(off, 8)
    start = 8 * a + r0
    if b == 0:
        return win[start:start + rows, :]
    return shf[b - 1, start:start + rows, :]


def _conv_forward(p_conv, conv_w, conv_b, ln_g, ln_b, w_co, ts):
    s_len, d3 = p_conv.shape
    dm = d3 // 3

    def body(p_ref, cw_ref, cb_ref, g_ref, b_ref, w_ref, y_ref, u0_ref, xh_ref, rstd_ref, win, shf, u1_ref):
        i = pl.program_id(0)

        @pl.when(i == 0)
        def _():
            win[0:HALO, :] = jnp.zeros((HALO, dm), F32)

        a, b, z = p_ref[:, 0:dm], p_ref[:, dm:2 * dm], p_ref[:, 2 * dm:3 * dm]
        u0 = a * _sig(b)
        win[HALO:HALO + ts, :] = u0
        u0_ref[...] = u0
        _shifted_copies(win, shf, ts)
        for r0 in range(0, ts, CONV_ROWS_FWD):
            acc = jnp.broadcast_to(cb_ref[...], (CONV_ROWS_FWD, dm))
            for k in range(CONV_K):
                acc = acc + cw_ref[k:k + 1, :] * _tap(win, shf, HALO - (CONV_K - 1) + k, r0, CONV_ROWS_FWD)
            u1_ref[r0:r0 + CONV_ROWS_FWD, :] = acc
        xh, rstd = _layernorm_stats(u1_ref[...])
        xh_ref[...] = xh
        rstd_ref[...] = rstd
        u2 = xh * g_ref[...] + b_ref[...]
        a_conv = (u2 * _sig(u2)) * (z * _sig(z))
        y_ref[...] = _dot(a_conv.astype(BF16), w_ref[...])
        win[0:HALO, :] = win[ts:ts + HALO, :]

    row = pl.BlockSpec((1, dm), lambda i: (0, 0))
    tile = pl.BlockSpec((ts, dm), lambda i: (i, 0))
    return pl.pallas_call(
        body, name="conv_forward", grid=(s_len // ts,),
        in_specs=[pl.BlockSpec((ts, d3), lambda i: (i, 0)),
                  pl.BlockSpec((HALO, dm), lambda i: (0, 0)), row, row, row,
                  pl.BlockSpec((dm, dm), lambda i: (0, 0))],
        out_specs=[tile, tile, tile, pl.BlockSpec((ts, 1), lambda i: (i, 0))],
        out_shape=[jax.ShapeDtypeStruct((s_len, dm), F32)] * 3 + [jax.ShapeDtypeStruct((s_len, 1), F32)],
        scratch_shapes=[pltpu.VMEM((ts + HALO, dm), F32), pltpu.VMEM((7, ts + HALO - 8, dm), F32),
                        pltpu.VMEM((ts, dm), F32)],
        compiler_params=_params(("arbitrary",)),
    )(p_conv, conv_w, conv_b, ln_g, ln_b, w_co)


def _conv_backward(dp, dyc, p_conv, u0, xh, rstd, conv_w, ln_g, ln_b, w_co, ts):
    s_len, d3 = p_conv.shape
    dm = d3 // 3
    nt = s_len // ts
    hb = ts // HALO

    def body(dp_in, dy_ref, p_ref, u0_ref, uh_ref, xh_ref, rstd_ref, cw_ref, g_ref, b_ref, w_ref,
             dp_ref, gw_ref, gs_ref, gcw_ref, dwin, uwin, shf):
        del dp_in
        i = pl.program_id(0)
        ti = nt - 1 - i

        @pl.when(i == 0)
        def _():
            gw_ref[...] = jnp.zeros_like(gw_ref)
            gs_ref[...] = jnp.zeros_like(gs_ref)
            gcw_ref[...] = jnp.zeros_like(gcw_ref)
            dwin[ts:ts + HALO, :] = jnp.zeros((HALO, dm), F32)

        dy = dy_ref[...]
        z = p_ref[:, 2 * dm:3 * dm]
        d_ac = _dot_nt(dy, w_ref[...])
        xh, rstd = xh_ref[...], rstd_ref[...]
        u2 = xh * g_ref[...] + b_ref[...]
        sg2, sgz = _sig(u2), _sig(z)
        u3, sz = u2 * sg2, z * sgz
        gw_ref[...] += _dot_tn((u3 * sz).astype(BF16), dy)
        d_z = d_ac * u3 * _dsilu(z, sgz)
        d_u2 = d_ac * sz * _dsilu(u2, sg2)
        gs_ref[0:1, :] += jnp.sum(d_u2 * xh, axis=0, keepdims=True)
        gs_ref[1:2, :] += jnp.sum(d_u2, axis=0, keepdims=True)
        dxh = d_u2 * g_ref[...]
        d_u1 = rstd * (dxh - jnp.mean(dxh, axis=-1, keepdims=True)
                       - xh * jnp.mean(dxh * xh, axis=-1, keepdims=True))
        gs_ref[2:3, :] += jnp.sum(d_u1, axis=0, keepdims=True)
        dwin[0:ts, :] = d_u1
        uwin[0:HALO, :] = jnp.where(ti == 0, 0.0, uh_ref[...])
        uwin[HALO:HALO + ts, :] = u0_ref[...]
        dp_ref[:, 2 * dm:3 * dm] = d_z.astype(BF16)
        _shifted_copies(uwin, shf, ts)
        for k in range(CONV_K):
            part = jnp.zeros((CONV_ROWS_BWD, dm), F32)
            for r0 in range(0, ts, CONV_ROWS_BWD):
                part = part + dwin[r0:r0 + CONV_ROWS_BWD, :] * _tap(uwin, shf, HALO - (CONV_K - 1) + k, r0,
                                                                    CONV_ROWS_BWD)
            gcw_ref[k:k + 1, :] += jnp.sum(part, axis=0, keepdims=True)
        _shifted_copies(dwin, shf, ts)
        for r0 in range(0, ts, CONV_ROWS_BWD):
            d_u0 = jnp.zeros((CONV_ROWS_BWD, dm), F32)
            for k in range(CONV_K):
                d_u0 = d_u0 + cw_ref[k:k + 1, :] * _tap(dwin, shf, CONV_K - 1 - k, r0, CONV_ROWS_BWD)
            rows = slice(r0, r0 + CONV_ROWS_BWD)
            sgb = _sig(p_ref[rows, dm:2 * dm])
            dp_ref[rows, 0:dm] = (d_u0 * sgb).astype(BF16)
            dp_ref[rows, dm:2 * dm] = (d_u0 * p_ref[rows, 0:dm] * sgb * (1.0 - sgb)).astype(BF16)
        dwin[ts:ts + HALO, :] = dwin[0:HALO, :]

    rev = lambda i: (nt - 1 - i, 0)
    row = pl.BlockSpec((1, dm), lambda i: (0, 0))
    tile = pl.BlockSpec((ts, dm), rev)
    return pl.pallas_call(
        body, name="conv_backward", grid=(nt,),
        in_specs=[ANY, tile, pl.BlockSpec((ts, d3), rev), tile,
                  pl.BlockSpec((HALO, dm), lambda i: (jnp.maximum((nt - 1 - i) * hb - 1, 0), 0)),
                  tile, pl.BlockSpec((ts, 1), rev), pl.BlockSpec((HALO, dm), lambda i: (0, 0)), row, row,
                  pl.BlockSpec((dm, dm), lambda i: (0, 0))],
        out_specs=[pl.BlockSpec((ts, d3), rev), pl.BlockSpec((dm, dm), lambda i: (0, 0)),
                   pl.BlockSpec((8, dm), lambda i: (0, 0)), pl.BlockSpec((HALO, dm), lambda i: (0, 0))],
        out_shape=[jax.ShapeDtypeStruct(dp.shape, BF16), jax.ShapeDtypeStruct((dm, dm), F32),
                   jax.ShapeDtypeStruct((8, dm), F32), jax.ShapeDtypeStruct((HALO, dm), F32)],
        input_output_aliases={0: 0},
        scratch_shapes=[pltpu.VMEM((ts + HALO, dm), F32), pltpu.VMEM((ts + HALO, dm), F32),
                        pltpu.VMEM((7, ts + HALO - 8, dm), F32)],
        compiler_params=_params(("arbitrary",)),
    )(dp, dyc, p_conv, u0, u0, xh, rstd, conv_w, ln_g, ln_b, w_co)


def _att_masks():
    head0 = lax.broadcasted_iota(jnp.int32, (BLK, LANES), 1) < 64
    col = lax.broadcasted_iota(jnp.int32, (BLK, 4 * BLK), 1)
    row = lax.broadcasted_iota(jnp.int32, (BLK, 4 * BLK), 0)
    kj = col % BLK
    prev = jnp.where(col < 2 * BLK, 1, 0)
    band = jnp.where(col < 2 * BLK, kj - row, row - kj)
    return head0, band, prev


def _fill_block_diagonal(dst, slab, src_ref, halo_ref, head0, nq):
    sl = slice(slab * LANES, (slab + 1) * LANES)
    for b in range(nq + 1):
        blk = halo_ref[:, sl] if b == 0 else src_ref[(b - 1) * BLK:b * BLK, sl]
        base = (slab * (nq + 1) + b) * 2 * BLK
        zero = jnp.zeros_like(blk)
        dst[base:base + BLK, :] = jnp.where(head0, blk, zero)
        dst[base + BLK:base + 2 * BLK, :] = jnp.where(head0, zero, blk)


def _attention_forward(qkv, seq_len, qt, name):
    s_len = qkv.shape[1]
    nq = qt // BLK
    tiles_per_seq = seq_len // qt

    def body(q_ref, k_ref, v_ref, kh_ref, vh_ref, o_ref, lse_ref, kbd, vbd):
        first = jnp.where((pl.program_id(0) % tiles_per_seq) == 0, 4 * BLK, 0)
        head0, band, prev = _att_masks()
        band_first = band - prev * first
        for p in range(4):
            _fill_block_diagonal(kbd, p, k_ref, kh_ref, head0, nq)
            _fill_block_diagonal(vbd, p, v_ref, vh_ref, head0, nq)
        units = [(p, n) for p in range(4) for n in range(nq)]

        def keys_of(p, n):
            base = (p * (nq + 1) + n) * 2 * BLK
            return slice(base, base + 4 * BLK)

        def scores(p, n):
            q2 = q_ref[n * BLK:(n + 1) * BLK, p * LANES:(p + 1) * LANES] * 0.125
            return _dot_nt(q2, kbd[keys_of(p, n), :])

        def finish(p, n, o, den, lse):
            rows, sl = slice(n * BLK, (n + 1) * BLK), slice(p * LANES, (p + 1) * LANES)
            o_ref[rows, sl] = o / jnp.where(head0, den[0], den[1])
            lse_ref[rows, sl] = jnp.where(head0, lse[0], lse[1])

        s_next = scores(*units[0])
        pending = None
        for ui, (p, n) in enumerate(units):
            s = s_next
            if ui + 1 < len(units):
                s_next = scores(*units[ui + 1])
            s = jnp.where((band_first if n == 0 else band) >= 0, s, NEG)
            grp = [s[:, g * BLK:(g + 1) * BLK] for g in range(4)]
            ps, den, lse = [None] * 4, [], []
            for h in range(2):
                m = jnp.max(jnp.maximum(grp[h], grp[2 + h]), axis=-1, keepdims=True)
                ps[h], ps[2 + h] = jnp.exp(grp[h] - m), jnp.exp(grp[2 + h] - m)
                dn = jnp.sum(ps[h] + ps[2 + h], axis=-1, keepdims=True)
                den.append(dn)
                lse.append(m + jnp.log(dn))
            pmat = jnp.concatenate([x.astype(BF16) for x in ps], axis=1)
            o = _dot(pmat, vbd[keys_of(p, n), :])
            if pending is not None:
                finish(*pending)
            pending = (p, n, o, den, lse)
        finish(*pending)

    def which(w):
        return pl.BlockSpec((None, qt, ATT_W), lambda i: (w, i, 0))

    def halo(w):
        return pl.BlockSpec((None, BLK, ATT_W), lambda i: (w, jnp.maximum(i * nq - 1, 0), 0))

    out = pl.BlockSpec((qt, ATT_W), lambda i: (i, 0))
    bd = pltpu.VMEM((4 * (nq + 1) * 2 * BLK, LANES), BF16)
    return pl.pallas_call(
        body, name=name, grid=(s_len // qt,),
        in_specs=[which(0), which(1), which(2), halo(1), halo(2)],
        out_specs=[out, out], out_shape=[jax.ShapeDtypeStruct((s_len, ATT_W), F32)] * 2,
        scratch_shapes=[bd, bd],
        compiler_params=_params(("parallel",)),
    )(qkv, qkv, qkv, qkv, qkv)


def _attention_backward(qkv, d_att, lse, delta, seq_len, qt, name):
    s_len = qkv.shape[1]
    nq = qt // BLK
    tiles_per_seq = seq_len // qt
    nblk = s_len // BLK

    def body(q_ref, k_ref, v_ref, kh_ref, vh_ref, do_ref, lse_ref, dl_ref,
             qn_ref, don_ref, lsen_ref, dln_ref, dqkv_ref, kbd, vbd):
        i = pl.program_id(0)
        first = jnp.where((i % tiles_per_seq) == 0, 4 * BLK, 0)
        last = jnp.where((i % tiles_per_seq) == tiles_per_seq - 1, 4 * BLK, 0)
        head0, band, prev = _att_masks()
        band_first = band - prev * first
        band_tail = band[:, 0:2 * BLK] - last
        for p in range(4):
            _fill_block_diagonal(kbd, p, k_ref, kh_ref, head0, nq)
            _fill_block_diagonal(vbd, p, v_ref, vh_ref, head0, nq)
        units = [(p, n) for p in range(4) for n in range(nq + 1)]

        def stage_a(p, n):
            sl = slice(p * LANES, (p + 1) * LANES)
            base = (p * (nq + 1) + n) * 2 * BLK
            if n < nq:
                rows = slice(n * BLK, (n + 1) * BLK)
                q2, do2, lse2, dl2 = q_ref[rows, sl], do_ref[rows, sl], lse_ref[rows, sl], dl_ref[rows, sl]
                keys = slice(base, base + 4 * BLK)
            else:
                q2, do2, lse2, dl2 = qn_ref[:, sl], don_ref[:, sl], lsen_ref[:, sl], dln_ref[:, sl]
                keys = slice(base, base + 2 * BLK)
            s = _dot_nt(q2 * 0.125, kbd[keys, :])
            dp = _dot_nt(do2, vbd[keys, :])
            return q2, do2, lse2, dl2, keys, s, dp

        def stage_b(n, lse2, dl2, s, dp):
            mask = band_tail if n == nq else (band_first if n == 0 else band)
            ps, dss = [], []
            for g in range(s.shape[1] // BLK):
                h = g % 2
                cols = slice(g * BLK, (g + 1) * BLK)
                pg = jnp.exp(jnp.where(mask[:, cols] >= 0, s[:, cols] - lse2[:, h * 64:h * 64 + 1], NEG))
                ps.append(pg.astype(BF16))
                dss.append((pg * (dp[:, cols] - dl2[:, h * 64:h * 64 + 1]) * 0.125).astype(BF16))
            return jnp.concatenate(ps, axis=1), jnp.concatenate(dss, axis=1)

        def heads(r, g):
            return jnp.where(head0, r[g * BLK:(g + 1) * BLK, :], r[(g + 1) * BLK:(g + 2) * BLK, :])

        a_next = stage_a(*units[0])
        carry = None
        for ui, (p, n) in enumerate(units):
            q2, do2, lse2, dl2, keys, s, dp = a_next
            if ui + 1 < len(units):
                a_next = stage_a(*units[ui + 1])
            pmat, dsmat = stage_b(n, lse2, dl2, s, dp)
            sl = slice(p * LANES, (p + 1) * LANES)
            if n < nq:
                dqkv_ref[0, n * BLK:(n + 1) * BLK, sl] = _dot(dsmat, kbd[keys, :])
            dkbd = _dot_tn(dsmat, q2)
            dvbd = _dot_tn(pmat, do2)
            if n > 0:
                prow = slice((n - 1) * BLK, n * BLK)
                dqkv_ref[1, prow, sl] = carry[0] + heads(dkbd, 0)
                dqkv_ref[2, prow, sl] = carry[1] + heads(dvbd, 0)
            if n < nq:
                carry = (heads(dkbd, 2), heads(dvbd, 2))

    def which(w):
        return pl.BlockSpec((None, qt, ATT_W), lambda i: (w, i, 0))

    def prev(w):
        return pl.BlockSpec((None, BLK, ATT_W), lambda i: (w, jnp.maximum(i * nq - 1, 0), 0))

    tile = pl.BlockSpec((qt, ATT_W), lambda i: (i, 0))
    nxt = pl.BlockSpec((BLK, ATT_W), lambda i: (jnp.minimum((i + 1) * nq, nblk - 1), 0))
    nxt_q = pl.BlockSpec((None, BLK, ATT_W), lambda i: (0, jnp.minimum((i + 1) * nq, nblk - 1), 0))
    return pl.pallas_call(
        body, name=name, grid=(s_len // qt,),
        in_specs=[which(0), which(1), which(2), prev(1), prev(2), tile, tile, tile, nxt_q, nxt, nxt, nxt],
        out_specs=pl.BlockSpec((3, qt, ATT_W), lambda i: (0, i, 0)),
        out_shape=jax.ShapeDtypeStruct((3, s_len, ATT_W), F32),
        scratch_shapes=[pltpu.VMEM((4 * (nq + 1) * 2 * BLK, LANES), BF16)] * 2,
        compiler_params=_params(("parallel",)),
    )(qkv, qkv, qkv, qkv, qkv, d_att, lse, delta, qkv, d_att, lse, delta)


def _merge_and_head(o_g, lse_g, p_gate, y_conv, x, tgt, modv, final_g, w_ao, w_o, ts):
    s_len, dm = x.shape
    gw = ATT_W + 2 * dm
    nt = s_len // ts
    gate_off = 3 * dm + 9 * ATT_W
    assert gate_off % gw == 0

    def body(o0, o1, o2, l0, l1, l2, pg_ref, yc_ref, x_ref, t_ref, mod_ref, fg_ref, wao_ref, wo_ref,
             loss_ref, dx_ref, dyc_ref, dpg_ref, da0, da1, da2, ls0, ls1, ls2, de0, de1, de2,
             gwo_ref, gwao_ref, gs_ref, slab, ones_scr):
        i = pl.program_id(0)

        @pl.when(i == 0)
        def _():
            loss_ref[...] = jnp.zeros_like(loss_ref)
            gwo_ref[...] = jnp.zeros_like(gwo_ref)
            gwao_ref[...] = jnp.zeros_like(gwao_ref)
            gs_ref[...] = jnp.zeros_like(gs_ref)
            ri = lax.broadcasted_iota(jnp.int32, (ATT_W, ATT_W), 0) // 64
            ci = lax.broadcasted_iota(jnp.int32, (ATT_W, ATT_W), 1) // 64
            ones_scr[...] = jnp.where(ri == ci, 1.0, 0.0).astype(BF16)

        os_, ls_ = [], []
        for dil, o_ref, l_ref in zip(DILATIONS, (o0, o1, o2), (l0, l1, l2)):
            os_.append(jnp.concatenate(_interleave_load(o_ref, slab, dil, ts), axis=1))
            ls_.append(jnp.concatenate(_interleave_load(l_ref, slab, dil, ts), axis=1))
        mx = jnp.maximum(jnp.maximum(ls_[0], ls_[1]), ls_[2])
        wts = [jnp.exp(l - mx) for l in ls_]
        wsum = wts[0] + wts[1] + wts[2]
        att = (wts[0] * os_[0] + wts[1] * os_[1] + wts[2] * os_[2]) / wsum
        lse_all = mx + jnp.log(wsum)

        z_att, g_conv, g_att = pg_ref[:, 0:ATT_W], pg_ref[:, ATT_W:ATT_W + dm], pg_ref[:, ATT_W + dm:gw]
        sgz = _sig(z_att)
        sz = z_att * sgz
        a_att = (att * sz).astype(BF16)
        y_att = _dot(a_att, wao_ref[...])
        y_conv = yc_ref[...]
        sgc, sga = _sig(g_conv), _sig(g_att)
        merged = (sgc * y_conv + sga * y_att).astype(BF16)
        mo = _dot(merged, wo_ref[...])
        gate = mod_ref[2:3, :]
        x2 = x_ref[...] + gate * mo
        r = lax.rsqrt(jnp.mean(x2 * x2, axis=-1, keepdims=True) + EPS)
        xr = x2 * r
        err = xr * fg_ref[...] - t_ref[...]
        loss_ref[...] += 0.5 * jnp.sum(jnp.mean(err * err, axis=-1, keepdims=True))
        dy = err * (1.0 / dm)
        gs_ref[0:1, :] += jnp.sum(dy * xr, axis=0, keepdims=True)
        dyg = dy * fg_ref[...]
        d_x2 = r * dyg - xr * (r * jnp.mean(dyg * xr, axis=-1, keepdims=True))
        dx_ref[...] = d_x2
        gs_ref[1:2, :] += jnp.sum(d_x2 * mo, axis=0, keepdims=True)
        d_mo = (d_x2 * gate).astype(BF16)
        d_mg = _dot_nt(d_mo, wo_ref[...])
        gwo_ref[...] += _dot_tn(merged, d_mo)
        dyc_ref[...] = (d_mg * sgc).astype(BF16)
        dpg_ref[:, ATT_W:ATT_W + dm] = (d_mg * y_conv * sgc * (1.0 - sgc)).astype(BF16)
        d_ya = (d_mg * sga).astype(BF16)
        dpg_ref[:, ATT_W + dm:gw] = (d_mg * y_att * sga * (1.0 - sga)).astype(BF16)
        gwao_ref[...] += _dot_tn(a_att, d_ya)
        d_aa = _dot_nt(d_ya, wao_ref[...])
        dpg_ref[:, 0:ATT_W] = (d_aa * att * _dsilu(z_att, sgz)).astype(BF16)
        d_att = d_aa * sz
        ones = ones_scr[...]
        prod = d_att * att
        hi = prod.astype(BF16)
        lo = (prod - hi.astype(F32)).astype(BF16)
        delta = _dot(hi, ones) + _dot(lo, ones)
        for val, refs, dt in ((d_att, (da0, da1, da2), BF16), (lse_all, (ls0, ls1, ls2), F32),
                              (delta, (de0, de1, de2), F32)):
            vals = [val[:, s * LANES:(s + 1) * LANES] for s in range(4)]
            for dil, ref in zip(DILATIONS, refs):
                _deinterleave_store(vals, slab, ref, dil, ts, dt)

    def grp(dil):
        return pl.BlockSpec((dil, ts // dil, ATT_W), lambda i: (0, i, 0))

    tile = pl.BlockSpec((ts, dm), lambda i: (i, 0))
    gate_tile = pl.BlockSpec((ts, gw), lambda i: (i, 0))
    const = lambda shp: pl.BlockSpec(shp, lambda i: tuple(0 for _ in shp))
    grp_shape = lambda dt: [jax.ShapeDtypeStruct((dil, s_len // dil, ATT_W), dt) for dil in DILATIONS]
    return pl.pallas_call(
        body, name="merge_and_head", grid=(nt,),
        in_specs=[grp(d) for d in DILATIONS] * 2
        + [gate_tile, tile, tile, tile, const((8, dm)), const((1, dm)), const((ATT_W, dm)), const((dm, dm))],
        out_specs=[const((8, LANES)), tile, tile, pl.BlockSpec((ts, gw), lambda i: (i, gate_off // gw))]
        + [grp(d) for d in DILATIONS] * 3
        + [const((dm, dm)), const((ATT_W, dm)), const((8, dm))],
        out_shape=[jax.ShapeDtypeStruct((8, LANES), F32), jax.ShapeDtypeStruct((s_len, dm), F32),
                   jax.ShapeDtypeStruct((s_len, dm), BF16), jax.ShapeDtypeStruct((s_len, gate_off + gw), BF16)]
        + grp_shape(BF16) + grp_shape(F32) + grp_shape(F32)
        + [jax.ShapeDtypeStruct((dm, dm), F32), jax.ShapeDtypeStruct((ATT_W, dm), F32),
           jax.ShapeDtypeStruct((8, dm), F32)],
        scratch_shapes=[pltpu.VMEM((8, ts, LANES), F32), pltpu.VMEM((ATT_W, ATT_W), BF16)],
        compiler_params=_params(("arbitrary",)),
    )(*o_g, *lse_g, p_gate, y_conv, x, tgt, modv, final_g, w_ao, w_o)


def _qkv_grad_to_tokens(dp, dqkv_g, rope, ts):
    s_len, dm = rope.shape[1], (dp.shape[1] - 10 * ATT_W) // 5
    qw = 3 * ATT_W
    assert (3 * dm) % qw == 0

    def body(dp_in, g0, g1, g2, rope_ref, o_ref, slab):
        del dp_in
        w = pl.program_id(1)
        rc, ra, rb = rope_ref.at[0], rope_ref.at[1], rope_ref.at[2]

        def emit(roped):
            for gi, (dil, g_ref) in enumerate(zip(DILATIONS, (g0, g1, g2))):
                vals = _interleave_load(g_ref, slab, dil, ts)
                for s in range(4):
                    t = vals[s]
                    if roped:
                        t = t * rc[...] + pltpu.roll(t * ra[...], 8, 1) + pltpu.roll(t * rb[...], LANES - 8, 1)
                    col = gi * ATT_W + s * LANES
                    o_ref[:, col:col + LANES] = t.astype(BF16)

        pl.when(w < 2)(lambda: emit(True))
        pl.when(w == 2)(lambda: emit(False))

    return pl.pallas_call(
        body, name="qkv_grad_to_tokens", grid=(s_len // ts, 3),
        in_specs=[ANY] + [pl.BlockSpec((None, dil, ts // dil, ATT_W), lambda i, w: (w, 0, i, 0)) for dil in DILATIONS]
        + [pl.BlockSpec((3, ts, LANES), lambda i, w: (0, i, 0))],
        out_specs=pl.BlockSpec((ts, qw), lambda i, w: (i, 3 * dm // qw + w)),
        out_shape=jax.ShapeDtypeStruct(dp.shape, BF16),
        input_output_aliases={0: 0},
        scratch_shapes=[pltpu.VMEM((8, ts, LANES), F32)],
        compiler_params=_params(("parallel", "arbitrary")),
    )(dp, *dqkv_g, rope)


def _wide_col_tile(cols):
    for width in (5 * COL_TILE, 2 * COL_TILE):
        if cols % width == 0:
            return width
    return COL_TILE


def _input_grad(dp, w_in, x, dx_res, modv, norm_g, ts):
    s_len, dm = x.shape
    ct = _wide_col_tile(dp.shape[1])
    nct = dp.shape[1] // ct

    def body(p_ref, w_ref, x_ref, dxr_ref, mod_ref, g_ref, gx_ref, gs_ref, acc):
        i, j = pl.program_id(0), pl.program_id(1)

        @pl.when((i == 0) & (j == 0))
        def _():
            gs_ref[...] = jnp.zeros_like(gs_ref)

        @pl.when(j == 0)
        def _():
            acc[...] = jnp.zeros_like(acc)

        acc[...] += _dot_nt(p_ref[...], w_ref[...])

        @pl.when(j == nct - 1)
        def _():
            d_h = acc[...]
            xv = x_ref[...]
            r = lax.rsqrt(jnp.mean(xv * xv, axis=-1, keepdims=True) + EPS)
            xr = xv * r
            gs_ref[0:1, :] += jnp.sum(d_h, axis=0, keepdims=True)
            gs_ref[1:2, :] += jnp.sum(d_h * (xr * g_ref[...]), axis=0, keepdims=True)
            d_n = d_h * (1.0 + mod_ref[1:2, :])
            gs_ref[2:3, :] += jnp.sum(d_n * xr, axis=0, keepdims=True)
            dxn = d_n * g_ref[...]
            gx_ref[...] = dxr_ref[...] + r * dxn - xr * (r * jnp.mean(dxn * xr, axis=-1, keepdims=True))

    tile = pl.BlockSpec((ts, dm), lambda i, j: (i, 0))
    return pl.pallas_call(
        body, name="input_grad", grid=(s_len // ts, nct),
        in_specs=[pl.BlockSpec((ts, ct), lambda i, j: (i, j)), pl.BlockSpec((dm, ct), lambda i, j: (0, j)), tile, tile,
                  pl.BlockSpec((8, dm), lambda i, j: (0, 0)), pl.BlockSpec((1, dm), lambda i, j: (0, 0))],
        out_specs=[tile, pl.BlockSpec((8, dm), lambda i, j: (0, 0))],
        out_shape=[jax.ShapeDtypeStruct((s_len, dm), F32), jax.ShapeDtypeStruct((8, dm), F32)],
        scratch_shapes=[pltpu.VMEM((ts, dm), F32)],
        compiler_params=_params(("arbitrary", "arbitrary")),
    )(dp, w_in, x, dx_res, modv, norm_g)


def _w_in_grad(h_t, dp, ts, tiles, name, prev=None, after=()):
    dm, s_len = h_t.shape
    ct = _wide_col_tile(dp.shape[1])
    lo, hi = tiles
    extra = ([prev] if prev is not None else []) + list(after)

    def body(h_ref, p_ref, *rest):
        o_ref = rest[-1]

        @pl.when(pl.program_id(1) == 0)
        def _():
            o_ref[...] = jnp.zeros_like(o_ref)

        o_ref[...] += _dot(h_ref[...], p_ref[...])

    return pl.pallas_call(
        body, name=name, grid=(hi - lo, s_len // ts),
        in_specs=[pl.BlockSpec((dm, ts), lambda j, i: (0, i)), pl.BlockSpec((ts, ct), lambda j, i: (i, lo + j))]
        + [ANY] * len(extra),
        out_specs=pl.BlockSpec((dm, ct), lambda j, i: (0, lo + j)),
        out_shape=jax.ShapeDtypeStruct((dm, dp.shape[1]), F32),
        input_output_aliases={2: 0} if prev is not None else {},
        compiler_params=_params(("arbitrary", "arbitrary")),
    )(h_t, dp, *extra)


def _rope_lane_table():
    l64 = jnp.arange(LANES) % 64
    half = ROT_DIM // 2
    inv_freq = ROPE_THETA ** (-(jnp.arange(half, dtype=F32) * 2.0 / ROT_DIM))
    rot = l64 < ROT_DIM
    rows = [jnp.where(rot, inv_freq[l64 % half], 0.0), (l64 < half).astype(F32),
            ((l64 >= half) & rot).astype(F32), rot.astype(F32)]
    return jnp.concatenate([jnp.stack(rows), jnp.zeros((4, LANES), F32)], axis=0)


def _tile_sizes(s_len):
    ts_big = min(1024, s_len // 2)
    ts_mid = 256
    ts_head = 256
    qt = [min(1024, s_len // dil) for dil in DILATIONS]
    return ts_big, ts_mid, ts_head, qt


def kernel(x, c, positions, norm_g, w_ada, b_ada, w_in, conv_w, conv_b, conv_ln_g, conv_ln_b, w_conv_out, w_att_out, w_o, final_g, loss_target, m_norm_g, m_w_ada, m_b_ada, m_w_in, m_conv_w, m_conv_b, m_conv_ln_g, m_conv_ln_b, m_w_conv_out, m_w_att_out, m_w_o, m_final_g, v_norm_g, v_w_ada, v_b_ada, v_w_in, v_conv_w, v_conv_b, v_conv_ln_g, v_conv_ln_b, v_w_conv_out, v_w_att_out, v_w_o, v_final_g):
    s_len, dm = x.shape[1], x.shape[2]
    ts_big, ts_mid, ts_head, qt = _tile_sizes(s_len)
    xi, yi, cidx = _my_place()
    chip = 2 * xi + yi
    batch = 4 * xi + 2 * yi + cidx
    x2d, tgt = x[0], loss_target[0]
    pos = positions.reshape(s_len, 1)
    wa_l, wi_l, cw_l = w_ada[0], w_in[0], conv_w[0]
    wco_l, wao_l, wo_l = w_conv_out[0], w_att_out[0], w_o[0]
    ada_w = wa_l.shape[1]
    cw_cols = cw_l.shape[1]

    kinds = ("col", "row", "col", "row")
    w_bufs = _cast_weights([wi_l, wco_l, wao_l, wo_l], kinds)

    cw_pad = jnp.pad(cw_l, ((0, HALO - CONV_K), (0, 0)))
    small_in = jnp.concatenate([jnp.broadcast_to(c, (8, dm)), cw_pad.reshape(8, dm)], axis=0)
    small = _allgather_small(small_in).reshape(8, 16, dm)
    c_all = small[:, 0, :]
    conv_w_full = jnp.concatenate(
        [small[2 * p, 8:16, :].reshape(HALO, cw_cols) for p in range(4)], axis=1)
    b_l = lax.dynamic_slice(b_ada, (0, chip * ada_w), (1, ada_w))
    mod_parts = _allgather_small(_mod_part(c_all, wa_l, b_l)).reshape(8, 8, ada_w)
    mod_rows = lax.dynamic_index_in_dim(mod_parts, batch, axis=1, keepdims=False)
    mod = jnp.concatenate([mod_rows[2 * p] for p in range(4)], axis=0).reshape(3, dm)
    modv = jnp.concatenate([mod, jnp.zeros((5, dm), F32)], axis=0)

    sems_a, sems_b, w_bufs, token = _gather_weights_start(w_bufs, kinds, modv)
    rope = _rope_coefficients(pos, _rope_lane_table() + token, ts_big)
    h_b, h_t = _norm_modulate(x2d, modv + token[0, 0], norm_g, ts_big)
    (w_in_b,) = _gather_weights_wait(w_bufs, kinds, sems_a, 0, (rope, h_b), "gather_weights_wait_w_in")
    w_in_b = _pass_to_sibling(w_in_b, kinds[0])

    n_conv, n_gate = 3 * dm // COL_TILE, (ATT_W + 2 * dm) // COL_TILE
    ts_proj = min(4 * ts_big, s_len)
    p_conv = _plain_projection(h_b, w_in_b, 0, n_conv, ts_proj, "conv_projection")
    p_gate = _plain_projection(h_b, w_in_b, n_conv + 9, n_gate, ts_proj, "gate_projection")
    qkv_g = [_qkv_projection(h_b, rope, w_in_b, gi, min(2 * ts_big, s_len)) for gi in range(3)]
    w_co_b, w_ao_b, w_o_b = _gather_weights_wait(w_bufs, kinds, sems_b, 1, (qkv_g[2], p_conv, p_gate),
                                                  "gather_weights_wait_rest")
    y_conv, u0, xh, rstd = _conv_forward(p_conv, conv_w_full, conv_b, conv_ln_g, conv_ln_b, w_co_b, ts_mid)
    qkv_flat = [q.reshape(3, s_len, ATT_W) for q in qkv_g]
    o_g, lse_g = [], []
    for gi, dil in enumerate(DILATIONS):
        o, l = _attention_forward(qkv_flat[gi], s_len // dil, qt[gi], "attention_forward_%d" % dil)
        o_g.append(o.reshape(dil, s_len // dil, ATT_W))
        lse_g.append(l.reshape(dil, s_len // dil, ATT_W))

    (loss_p, dx_res, dyc, dp, da0, da1, da2, ls0, ls1, ls2, de0, de1, de2,
     g_wo, g_wao, head_sums) = _merge_and_head(o_g, lse_g, p_gate, y_conv, x2d, tgt, modv,
                                               final_g.reshape(1, dm), w_ao_b, w_o_b, ts_head)

    dp, g_wco, conv_sums, g_cw = _conv_backward(dp, dyc, p_conv, u0, xh, rstd, conv_w_full, conv_ln_g, conv_ln_b,
                                                w_co_b, ts_mid)
    dqkv_g = []
    for gi, (dil, da, ls, de) in enumerate(zip(DILATIONS, (da0, da1, da2), (ls0, ls1, ls2), (de0, de1, de2))):
        flat = lambda a: a.reshape(s_len, ATT_W)
        dqkv = _attention_backward(qkv_flat[gi], flat(da), flat(ls), flat(de), s_len // dil, qt[gi],
                                   "attention_backward_%d" % dil)
        dqkv_g.append(dqkv.reshape(3, dil, s_len // dil, ATT_W))
    dp = _qkv_grad_to_tokens(dp, dqkv_g, rope, ts_big)
    ts_wg = min(2 * ts_big, s_len)
    n_wt = dp.shape[1] // _wide_col_tile(dp.shape[1])
    n_first = max(1, (3 * n_wt) // 4)
    split_col = n_first * _wide_col_tile(dp.shape[1])
    g_first = _w_in_grad(h_t, dp, ts_wg, (0, n_first), "w_in_grad_first")
    grads = [g_first, g_wco, g_wao, g_wo]
    full_shapes = [g.shape for g in grads]
    pas = [pltpu.with_memory_space_constraint(
        lax.empty((r // 2, c_) if kind == "col" else (r, c_ // 2), F32), pltpu.HBM)
        for (r, c_), kind in zip(full_shapes, kinds)]
    pe_s, pe_r, grads, pas, pe_token = _pair_exchange_start(grads, pas, kinds, (0, split_col))
    grads[0] = _w_in_grad(h_t, dp, ts_wg, (n_first, n_wt), "w_in_grad_second", prev=grads[0], after=(pe_token,))
    grads, recv_halves = _pair_exchange_wait(pe_s, pe_r, grads, pas, kinds, (0, split_col))
    recv_halves[0] = _pair_exchange_rest(grads[0], recv_halves[0], kinds[0], (split_col, dp.shape[1]))

    c_arr = jnp.reshape(cidx, (1,)).astype(jnp.int32)
    halves = [_reduce_pair_sum(g, pa, kind, c_arr, "reduce_pair_sum_%d" % k)
              for k, (g, pa, kind) in enumerate(zip(grads, recv_halves, kinds))]
    send_sems, recv_sems, halves, lands, token = _reduce_to_owner_start(halves, kinds, full_shapes)
    grad_x, in_sums = _input_grad(dp, w_in_b, x2d, dx_res, modv + token[0, 0], norm_g, ts_big)
    halves, recvd = _reduce_to_owner_wait(send_sems, recv_sems, halves, lands, in_sums, kinds)
    gr_win, gr_wco, gr_wao, gr_wo = _reduce_finish(halves, recvd, kinds, full_shapes)

    rows = [in_sums[2:3], conv_sums[2:3], conv_sums[0:1], conv_sums[1:2], head_sums[0:1],
            in_sums[0:1], in_sums[1:2], head_sums[1:2], g_cw, jnp.pad(loss_p, ((0, 0), (0, dm - LANES)))]
    part = jnp.concatenate(rows, axis=0)
    buf = lax.dynamic_update_slice(jnp.zeros((8 * 48, dm), F32), part, (batch * 48, 0))
    row_sems_s, row_sems_r, buf, row_token = _row_gather_start(buf, 48, gr_win)
    upd = {
        "w_in": _adamw(wi_l, gr_win, m_w_in[0], v_w_in[0], "adamw_w_in", (row_token,)),
        "w_co": _adamw(wco_l, gr_wco, m_w_conv_out[0], v_w_conv_out[0], "adamw_w_conv_out", (row_token,)),
        "w_ao": _adamw(wao_l, gr_wao, m_w_att_out[0], v_w_att_out[0], "adamw_w_att_out", (row_token,)),
        "w_o": _adamw(wo_l, gr_wo, m_w_o[0], v_w_o[0], "adamw_w_o", (row_token,)),
    }
    gathered = _row_gather_wait(row_sems_s, row_sems_r, buf, 48, [upd[k][0] for k in ("w_in", "w_co", "w_ao", "w_o")])
    tot = _sum_devices(gathered, 48)
    loss = tot[8 + HALO, 0]
    dmod_all = gathered.reshape(8, 48, dm)[:, 5:8, :].reshape(8, 3 * dm)
    dmod_l = lax.dynamic_slice(dmod_all, (0, chip * ada_w), (8, ada_w))
    gr_wada = _grad_w_ada(c_all.T, dmod_l)
    gr_cw = lax.dynamic_slice(tot[8:8 + HALO], (0, chip * cw_cols), (HALO, cw_cols))

    pad_cw = lambda a: jnp.pad(a[0], ((0, HALO - CONV_K), (0, 0)))
    row = lambda a: a.reshape(1, dm)
    small_upd = _adamw_small(
        [norm_g, conv_b, conv_ln_g, conv_ln_b, row(final_g), b_ada], tot[0:8],
        [m_norm_g, m_conv_b, m_conv_ln_g, m_conv_ln_b, row(m_final_g), m_b_ada],
        [v_norm_g, v_conv_b, v_conv_ln_g, v_conv_ln_b, row(v_final_g), v_b_ada])
    upd["w_ada"] = _adamw(wa_l, gr_wada, m_w_ada[0], v_w_ada[0], "adamw_w_ada")
    upd["conv_w"] = _adamw(cw_pad, gr_cw, pad_cw(m_conv_w), pad_cw(v_conv_w), "adamw_conv_w")

    def family(which):
        if which is None:
            sm = small_upd[0:6]
            big = {"w_ada": gr_wada, "w_in": gr_win, "conv_w": gr_cw, "w_co": gr_wco, "w_ao": gr_wao, "w_o": gr_wo}
        else:
            sm = small_upd[6 * (which + 1):6 * (which + 2)]
            big = {k: upd[k][which] for k in ("w_ada", "w_in", "conv_w", "w_co", "w_ao", "w_o")}
        return [sm[0], big["w_ada"][None], sm[5], big["w_in"][None],
                big["conv_w"][None, :CONV_K], sm[1], sm[2], sm[3], big["w_co"][None],
                big["w_ao"][None], big["w_o"][None], sm[4].reshape(dm)]

    return (loss, grad_x[None], *family(None), *family(0), *family(1), *family(2))
```

```python
import jax
import jax.numpy as jnp
from jax import lax
from jax.experimental import pallas as pl
from jax.experimental.pallas import tpu as pltpu

F32 = jnp.float32
BF16 = jnp.bfloat16
MESH = pl.DeviceIdType.MESH
ANY = pl.BlockSpec(memory_space=pl.ANY)
VM = pl.BlockSpec(memory_space=pltpu.VMEM)

EPS = 1e-6
NEG = -1e30
ATT_W = 512
DILATIONS = (1, 4, 16)
BLK = 128
CONV_K = 31
HALO = 32
CONV_ROWS_FWD = 32
CONV_ROWS_BWD = 16
ROT_DIM = 16
ROPE_THETA = 500000.0
COL_TILE = 512
LANES = 128
VMEM_LIMIT = 56 * 1024 * 1024

ADAM_LR, ADAM_B1, ADAM_B2, ADAM_EPS, ADAM_WD, ADAM_STEP = 0.001, 0.9, 0.999, 1e-08, 0.01, 10


def _params(sem=None, vmem=VMEM_LIMIT):
    return pltpu.CompilerParams(dimension_semantics=sem, vmem_limit_bytes=vmem)


def _dot(a, b):
    return jnp.dot(a, b, preferred_element_type=F32)


def _dot_nt(a, b):
    return lax.dot_general(a, b, (((1,), (1,)), ((), ())), preferred_element_type=F32)


def _dot_tn(a, b):
    return lax.dot_general(a, b, (((0,), (0,)), ((), ())), preferred_element_type=F32)


def _sig(x):
    return jax.nn.sigmoid(x)


def _dsilu(x, s):
    return s * (1.0 + x * (1.0 - s))


def _my_place():
    return lax.axis_index("x"), lax.axis_index("y"), lax.axis_index("c")


def _allgather_small(x_shard):
    m_per, n = x_shard.shape

    def body(x_ref, out_ref, send_sems, recv_sems, local_sem):
        x, y, c = _my_place()
        me, sibling = (x, y, c), (x, y, 1 - c)
        chips = [(1 - x, y), (x, 1 - y), (1 - x, 1 - y)]

        def rows(px, py, pc):
            return out_ref.at[pl.ds((4 * px + 2 * py + pc) * m_per, m_per), :]

        def copy(k, block, to, src=None):
            return pltpu.make_async_remote_copy(
                src_ref=rows(*block) if src is None else src, dst_ref=rows(*block),
                send_sem=send_sems.at[k], recv_sem=recv_sems.at[k],
                device_id=to, device_id_type=MESH)

        mine = pltpu.make_async_copy(x_ref, rows(*me), local_sem)
        mine.start()
        first = [copy(0, me, sibling, src=x_ref)]
        first += [copy(1 + j, me, (*chip, c), src=x_ref) for j, chip in enumerate(chips)]
        for cp in first:
            cp.start()
        passed = [copy(4 + j, (*chip, c), sibling) for j, chip in enumerate(chips)]
        for j, chip in enumerate(chips):
            copy(1 + j, (*chip, c), me).wait_recv()
            passed[j].start()
        copy(0, sibling, me).wait_recv()
        for j, chip in enumerate(chips):
            copy(4 + j, (*chip, 1 - c), me).wait_recv()
        for cp in first + passed:
            cp.wait_send()
        mine.wait()

    return pl.pallas_call(
        body, name="allgather_small",
        out_shape=jax.ShapeDtypeStruct((8 * m_per, n), x_shard.dtype),
        in_specs=[VM], out_specs=VM,
        scratch_shapes=[pltpu.SemaphoreType.DMA((7,)), pltpu.SemaphoreType.DMA((7,)),
                        pltpu.SemaphoreType.DMA],
    )(x_shard)


def _shard_window(ref, kind, p, n_shards=4):
    r, c = ref.shape
    if kind == "col":
        w = c // n_shards
        return ref.at[:, pl.ds(p * w, w)]
    w = r // n_shards
    return ref.at[pl.ds(p * w, w), :]


def _half_window(ref, kind, hc):
    r, c = ref.shape
    if kind == "col":
        return ref.at[pl.ds(hc * (r // 2), r // 2), :]
    return ref.at[:, pl.ds(hc * (c // 2), c // 2)]


def _landed(ref, kind, chip, hc):
    return _half_window(_shard_window(ref, kind, 2 * chip[0] + chip[1]), kind, hc)


def _cast_weights(shards, kinds):
    n = len(shards)
    full_shapes = [(s.shape[0], 4 * s.shape[1]) if kind == "col" else (4 * s.shape[0], s.shape[1])
                   for s, kind in zip(shards, kinds)]

    def body(*refs):
        w_refs, out_refs, bf_refs, sems = refs[:n], refs[n:2 * n], refs[2 * n:3 * n], refs[3 * n]
        x, y, _ = _my_place()
        cps = []
        for k in range(n):
            bf_refs[k][...] = w_refs[k][...].astype(BF16)
            cp = pltpu.make_async_copy(bf_refs[k], _shard_window(out_refs[k], kinds[k], 2 * x + y), sems.at[k])
            cp.start()
            cps.append(cp)
        for cp in cps:
            cp.wait()

    return pl.pallas_call(
        body, name="cast_weights",
        out_shape=[jax.ShapeDtypeStruct(s, BF16) for s in full_shapes],
        in_specs=[VM] * n, out_specs=[ANY] * n,
        scratch_shapes=[pltpu.VMEM(s.shape, BF16) for s in shards] + [pltpu.SemaphoreType.DMA((n,))],
        compiler_params=_params(),
    )(*shards)


def _gather_copies(refs, kinds, sems_a, sems_b):
    x, y, c = _my_place()
    chips = [(1 - x, y), (x, 1 - y), (1 - x, 1 - y)]
    set_a, set_b = [], []
    for j, chip in enumerate(chips):
        for k in range(len(refs)):
            if refs[k] is None:
                continue
            for flip in ((0,) if k == 0 else (0, 1)):
                win = _landed(refs[k], kinds[k], (x, y), c)
                sems, idx = (sems_a, j) if k == 0 else (sems_b, ((k - 1) * 3 + j) * 2 + flip)
                mine = _landed(refs[k], kinds[k], chip, (c + flip) % 2)
                (set_a if k == 0 else set_b).append((
                    pltpu.make_async_remote_copy(
                        src_ref=win, dst_ref=win, send_sem=sems[0].at[idx], recv_sem=sems[1].at[idx],
                        device_id=(*chip, (c + flip) % 2), device_id_type=MESH),
                    pltpu.make_async_remote_copy(
                        src_ref=mine, dst_ref=mine, send_sem=sems[0].at[idx], recv_sem=sems[1].at[idx],
                        device_id=(*chip, (c + flip) % 2), device_id_type=MESH)))
    return set_a, set_b


_SPLIT = dict(has_side_effects=pltpu.SideEffectType.DATAFLOW_SIDE_EFFECTING)


def _gather_weights_start(fulls, kinds, after):
    n = len(fulls)
    hbm = pl.BlockSpec(memory_space=pltpu.HBM)
    sem = pl.BlockSpec(memory_space=pltpu.SEMAPHORE)
    nb = (n - 1) * 6

    def body(*refs):
        in_refs = refs[:n]
        sa, ra, sb, rb = refs[n + 1:n + 5]
        token = refs[-1]
        set_a, set_b = _gather_copies(in_refs, kinds, (sa, ra), (sb, rb))
        for out_cp, _ in set_a + set_b:
            out_cp.start()
        token[...] = jnp.zeros_like(token)

    out = pl.pallas_call(
        body, name="gather_weights_start",
        out_shape=[pltpu.SemaphoreType.DMA((3,)), pltpu.SemaphoreType.DMA((3,)),
                   pltpu.SemaphoreType.DMA((nb,)), pltpu.SemaphoreType.DMA((nb,))]
        + [pltpu.HBM(f.shape, f.dtype) for f in fulls] + [jax.ShapeDtypeStruct((8, LANES), F32)],
        in_specs=[hbm] * n + [ANY], out_specs=[sem] * 4 + [hbm] * n + [VM],
        input_output_aliases={k: 4 + k for k in range(n)},
        compiler_params=pltpu.CompilerParams(**_SPLIT),
    )(*[pltpu.with_memory_space_constraint(f, pltpu.HBM) for f in fulls], after)
    return (out[0], out[1]), (out[2], out[3]), out[4:4 + n], out[-1]


def _gather_weights_wait(fulls, kinds, sems, which, after, name):
    n = len(fulls)
    hbm = pl.BlockSpec(memory_space=pltpu.HBM)
    sem = pl.BlockSpec(memory_space=pltpu.SEMAPHORE)
    keep = [0] if which == 0 else list(range(1, n))

    def body(*refs):
        m = len(keep)
        full_refs = [None] * n
        for pos_, k in enumerate(keep):
            full_refs[k] = refs[pos_]
        s_ref, r_ref = refs[m:m + 2]
        if which == 0:
            sets = _gather_copies([full_refs[0]], kinds[:1], (s_ref, r_ref), None)[0]
        else:
            sets = _gather_copies([None] + [full_refs[k] for k in keep], kinds, None, (s_ref, r_ref))[1]
        for out_cp, in_cp in sets:
            out_cp.wait_send()
            in_cp.wait_recv()

    out = pl.pallas_call(
        body, name=name,
        out_shape=[pltpu.HBM(fulls[k].shape, fulls[k].dtype) for k in keep],
        in_specs=[hbm] * len(keep) + [sem, sem] + [ANY] * len(after), out_specs=[hbm] * len(keep),
        input_output_aliases={i: i for i in range(len(keep))},
        compiler_params=pltpu.CompilerParams(**_SPLIT),
    )(*[fulls[k] for k in keep], sems[0], sems[1], *after)
    return list(out)


def _row_gather_copies(buf, m_per, send_sems, recv_sems):
    x, y, c = _my_place()
    pairs = []
    for idx in range(1, 8):
        dx, dy, dc = idx // 4, (idx // 2) % 2, idx % 2
        peer = ((x + dx) % 2, (y + dy) % 2, (c + dc) % 2)
        mine = buf.at[pl.ds((4 * x + 2 * y + c) * m_per, m_per), :]
        theirs = buf.at[pl.ds((4 * peer[0] + 2 * peer[1] + peer[2]) * m_per, m_per), :]
        pairs.append(tuple(
            pltpu.make_async_remote_copy(src_ref=w, dst_ref=w, send_sem=send_sems.at[idx - 1],
                                         recv_sem=recv_sems.at[idx - 1], device_id=peer, device_id_type=MESH)
            for w in (mine, theirs)))
    return pairs


def _row_gather_start(buf, m_per, after):
    hbm = pl.BlockSpec(memory_space=pltpu.HBM)
    sem = pl.BlockSpec(memory_space=pltpu.SEMAPHORE)

    def body(buf_ref, after_ref, send_sems, recv_sems, out_ref, token):
        del after_ref, out_ref
        for out_cp, _ in _row_gather_copies(buf_ref, m_per, send_sems, recv_sems):
            out_cp.start()
        token[...] = jnp.zeros_like(token)

    return pl.pallas_call(
        body, name="row_gather_start",
        out_shape=[pltpu.SemaphoreType.DMA((7,)), pltpu.SemaphoreType.DMA((7,)), pltpu.HBM(buf.shape, buf.dtype),
                   jax.ShapeDtypeStruct((8, LANES), F32)],
        in_specs=[hbm, ANY], out_specs=[sem, sem, hbm, VM], input_output_aliases={0: 2},
        compiler_params=pltpu.CompilerParams(**_SPLIT),
    )(pltpu.with_memory_space_constraint(buf, pltpu.HBM), after)


def _row_gather_wait(send_sems, recv_sems, buf, m_per, after):
    hbm = pl.BlockSpec(memory_space=pltpu.HBM)
    sem = pl.BlockSpec(memory_space=pltpu.SEMAPHORE)

    def body(buf_ref, send_s, recv_s, *rest):
        for out_cp, in_cp in _row_gather_copies(buf_ref, m_per, send_s, recv_s):
            out_cp.wait_send()
            in_cp.wait_recv()

    return pl.pallas_call(
        body, name="row_gather_wait", out_shape=pltpu.HBM(buf.shape, buf.dtype),
        in_specs=[hbm, sem, sem] + [ANY] * len(after), out_specs=hbm, input_output_aliases={0: 0},
        compiler_params=pltpu.CompilerParams(**_SPLIT),
    )(buf, send_sems, recv_sems, *after)


def _pass_to_sibling(w_full, kind):
    def body(w_in_ref, w_ref, send_sems, recv_sems):
        del w_in_ref
        x, y, c = _my_place()
        chips = [(1 - x, y), (x, 1 - y), (1 - x, 1 - y)]
        cps = []
        for j, chip in enumerate(chips):
            win = _landed(w_ref, kind, chip, c)
            cp = pltpu.make_async_remote_copy(
                src_ref=win, dst_ref=win, send_sem=send_sems.at[j], recv_sem=recv_sems.at[j],
                device_id=(x, y, 1 - c), device_id_type=MESH)
            cp.start()
            cps.append(cp)
        for j, chip in enumerate(chips):
            theirs = _landed(w_ref, kind, chip, 1 - c)
            pltpu.make_async_remote_copy(
                src_ref=theirs, dst_ref=theirs, send_sem=send_sems.at[j], recv_sem=recv_sems.at[j],
                device_id=(x, y, 1 - c), device_id_type=MESH).wait_recv()
        for cp in cps:
            cp.wait_send()

    return pl.pallas_call(
        body, name="pass_to_sibling",
        out_shape=jax.ShapeDtypeStruct(w_full.shape, w_full.dtype),
        in_specs=[ANY], out_specs=ANY, input_output_aliases={0: 0},
        scratch_shapes=[pltpu.SemaphoreType.DMA((3,)), pltpu.SemaphoreType.DMA((3,))],
    )(w_full)


def _pair_copies(g_refs, pa_refs, kinds, send_sems, recv_sems, cols, with_rest):
    x, y, c = _my_place()
    lo, hi = cols

    def copy(k, src, dst):
        return pltpu.make_async_remote_copy(src_ref=src, dst_ref=dst, send_sem=send_sems.at[k], recv_sem=recv_sems.at[k],
                                            device_id=(x, y, 1 - c), device_id_type=MESH)

    cps = [copy(0, _half_window(g_refs[0], kinds[0], 1 - c).at[:, pl.ds(lo, hi - lo)],
                pa_refs[0].at[:, pl.ds(lo, hi - lo)])]
    if with_rest:
        cps += [copy(k, _half_window(g_refs[k], kinds[k], 1 - c), pa_refs[k]) for k in range(1, len(g_refs))]
    return cps


def _pair_exchange_start(grads, pas, kinds, cols):
    n = len(grads)
    hbm = pl.BlockSpec(memory_space=pltpu.HBM)
    sem = pl.BlockSpec(memory_space=pltpu.SEMAPHORE)

    def body(*refs):
        g_refs, pa_refs = refs[:n], refs[n:2 * n]
        send_sems, recv_sems = refs[2 * n:2 * n + 2]
        for cp in _pair_copies(g_refs, pa_refs, kinds, send_sems, recv_sems, cols, True):
            cp.start()
        refs[-1][...] = jnp.zeros_like(refs[-1])

    out = pl.pallas_call(
        body, name="pair_exchange_start",
        out_shape=[pltpu.SemaphoreType.DMA((n,)), pltpu.SemaphoreType.DMA((n,))]
        + [pltpu.HBM(a.shape, a.dtype) for a in list(grads) + list(pas)] + [jax.ShapeDtypeStruct((8, LANES), F32)],
        in_specs=[hbm] * (2 * n), out_specs=[sem, sem] + [hbm] * (2 * n) + [VM],
        input_output_aliases={k: 2 + k for k in range(2 * n)},
        compiler_params=pltpu.CompilerParams(**_SPLIT),
    )(*[pltpu.with_memory_space_constraint(a, pltpu.HBM) for a in list(grads) + list(pas)])
    return out[0], out[1], list(out[2:2 + n]), list(out[2 + n:2 + 2 * n]), out[-1]


def _pair_exchange_wait(send_sems, recv_sems, grads, pas, kinds, cols):
    n = len(grads)
    hbm = pl.BlockSpec(memory_space=pltpu.HBM)
    sem = pl.BlockSpec(memory_space=pltpu.SEMAPHORE)

    def body(*refs):
        g_refs, pa_refs = refs[:n], refs[n:2 * n]
        send_s, recv_s = refs[2 * n:2 * n + 2]
        for cp in _pair_copies(g_refs, pa_refs, kinds, send_s, recv_s, cols, True):
            cp.wait_send()
            cp.wait_recv()

    out = pl.pallas_call(
        body, name="pair_exchange_wait",
        out_shape=[pltpu.HBM(a.shape, a.dtype) for a in list(grads) + list(pas)],
        in_specs=[hbm] * (2 * n) + [sem, sem], out_specs=[hbm] * (2 * n),
        input_output_aliases={k: k for k in range(2 * n)},
        compiler_params=pltpu.CompilerParams(**_SPLIT),
    )(*grads, *pas, send_sems, recv_sems)
    return list(out[:n]), list(out[n:])


def _pair_exchange_rest(g, pa, kind, cols):
    def body(g_ref, pa_in, pa_ref, send_sems, recv_sems):
        del pa_in
        (cp,) = _pair_copies([g_ref], [pa_ref], [kind], send_sems, recv_sems, cols, False)
        cp.start()
        cp.wait()

    return pl.pallas_call(
        body, name="pair_exchange_rest", out_shape=jax.ShapeDtypeStruct(pa.shape, pa.dtype),
        in_specs=[ANY, ANY], out_specs=ANY, input_output_aliases={1: 0},
        scratch_shapes=[pltpu.SemaphoreType.DMA((1,)), pltpu.SemaphoreType.DMA((1,))],
    )(g, pa)


def _row_tile(rows, cols, itemsize=4, target=2 * 1024 * 1024, mult=16):
    t = rows
    while t % 2 == 0 and t // 2 >= mult and (t // 2) % mult == 0 and t * cols * itemsize > target:
        t //= 2
    return t


def _reduce_pair_sum(g, pa, kind, c_arr, name):
    hr, hc_ = pa.shape
    tr = _row_tile(hr, hc_)
    nb = hr // tr

    def body(c_ref, g_ref, pa_ref, o_ref):
        o_ref[...] = (g_ref[...] + pa_ref[...]).astype(BF16)

    if kind == "col":
        g_map = lambda i, c_ref: (c_ref[0] * nb + i, 0)
    else:
        g_map = lambda i, c_ref: (i, c_ref[0])
    return pl.pallas_call(
        body, name=name,
        grid_spec=pltpu.PrefetchScalarGridSpec(
            num_scalar_prefetch=1, grid=(nb,),
            in_specs=[pl.BlockSpec((tr, hc_), g_map), pl.BlockSpec((tr, hc_), lambda i, c_ref: (i, 0))],
            out_specs=pl.BlockSpec((tr, hc_), lambda i, c_ref: (i, 0))),
        out_shape=jax.ShapeDtypeStruct((hr, hc_), BF16),
        compiler_params=_params(("parallel",)),
    )(c_arr, g, pa)


def _half_shard_shape(full_shape, kind):
    r, c = full_shape
    return (r // 2, c // 4) if kind == "col" else (r // 4, c // 2)


def _to_owner_copies(h_refs, land_refs, send_sems, recv_sems, kinds):
    n = len(h_refs)
    x, y, c = _my_place()
    chips = [(1 - x, y), (x, 1 - y), (1 - x, 1 - y)]
    cps = []
    for j, chip in enumerate(chips):
        pj = 2 * chip[0] + chip[1]
        for k in range(n):
            cps.append(pltpu.make_async_remote_copy(
                src_ref=_shard_window(h_refs[k], kinds[k], pj), dst_ref=land_refs[k].at[j],
                send_sem=send_sems.at[j * n + k], recv_sem=recv_sems.at[j * n + k],
                device_id=(*chip, c), device_id_type=MESH))
    return cps


def _reduce_to_owner_start(halves, kinds, full_shapes):
    n = len(halves)
    hs = [_half_shard_shape(fs, kind) for fs, kind in zip(full_shapes, kinds)]
    hbm = pl.BlockSpec(memory_space=pltpu.HBM)
    sem = pl.BlockSpec(memory_space=pltpu.SEMAPHORE)

    def body(*refs):
        h_refs, land_refs = refs[:n], refs[n:2 * n]
        send_sems, recv_sems = refs[2 * n:2 * n + 2]
        token = refs[-1]
        for cp in _to_owner_copies(h_refs, land_refs, send_sems, recv_sems, kinds):
            cp.start()
        token[...] = jnp.zeros_like(token)

    lands = [pltpu.with_memory_space_constraint(lax.empty((3,) + s, BF16), pltpu.HBM) for s in hs]
    out = pl.pallas_call(
        body, name="reduce_to_owner_start",
        out_shape=[pltpu.SemaphoreType.DMA((3 * n,)), pltpu.SemaphoreType.DMA((3 * n,))]
        + [pltpu.HBM(h.shape, h.dtype) for h in halves] + [pltpu.HBM((3,) + s, BF16) for s in hs]
        + [jax.ShapeDtypeStruct((8, LANES), F32)],
        in_specs=[hbm] * (2 * n), out_specs=[sem, sem] + [hbm] * (2 * n) + [VM],
        input_output_aliases={k: 2 + k for k in range(2 * n)},
        compiler_params=pltpu.CompilerParams(has_side_effects=pltpu.SideEffectType.DATAFLOW_SIDE_EFFECTING),
    )(*[pltpu.with_memory_space_constraint(h, pltpu.HBM) for h in halves], *lands)
    return out[0], out[1], out[2:2 + n], out[2 + n:2 + 2 * n], out[-1]


def _reduce_to_owner_wait(send_sems, recv_sems, halves, lands, after, kinds):
    n = len(halves)
    hbm = pl.BlockSpec(memory_space=pltpu.HBM)
    sem = pl.BlockSpec(memory_space=pltpu.SEMAPHORE)

    def body(*refs):
        h_refs, land_refs = refs[:n], refs[n:2 * n]
        send_s, recv_s = refs[2 * n:2 * n + 2]
        for cp in _to_owner_copies(h_refs, land_refs, send_s, recv_s, kinds):
            cp.wait_send()
            cp.wait_recv()

    out = pl.pallas_call(
        body, name="reduce_to_owner_wait",
        out_shape=[pltpu.HBM(h.shape, h.dtype) for h in halves] + [pltpu.HBM(l.shape, l.dtype) for l in lands],
        in_specs=[hbm] * (2 * n) + [sem, sem, ANY], out_specs=[hbm] * (2 * n),
        input_output_aliases={k: k for k in range(2 * n)},
        compiler_params=pltpu.CompilerParams(has_side_effects=pltpu.SideEffectType.DATAFLOW_SIDE_EFFECTING),
    )(*halves, *lands, send_sems, recv_sems, after)
    return out[:n], out[n:]


def _reduce_finish(halves, recvd, kinds, full_shapes):
    n = len(halves)
    hs = [_half_shard_shape(fs, kind) for fs, kind in zip(full_shapes, kinds)]
    shard_shapes = [(fs[0], fs[1] // 4) if kind == "col" else (fs[0] // 4, fs[1])
                    for fs, kind in zip(full_shapes, kinds)]

    def body(*refs):
        h_refs, rc_refs, gs_refs = refs[:n], refs[n:2 * n], refs[2 * n:3 * n]
        own_refs, gh_refs = refs[3 * n:4 * n], refs[4 * n:5 * n]
        in_sems, loc_sems, send_sems, recv_sems = refs[5 * n:]
        x, y, c = _my_place()
        p = 2 * x + y
        loads = []
        for k in range(n):
            cp = pltpu.make_async_copy(_shard_window(h_refs[k], kinds[k], p), own_refs[k], in_sems.at[k])
            cp.start()
            loads.append(cp)
        outs = []
        for k in range(n):
            loads[k].wait()
            gh_refs[k][...] = (own_refs[k][...].astype(F32) + rc_refs[k][0].astype(F32)
                               + rc_refs[k][1].astype(F32) + rc_refs[k][2].astype(F32))
            dst = _half_window(gs_refs[k], kinds[k], c)
            lc = pltpu.make_async_copy(gh_refs[k], dst, loc_sems.at[k])
            lc.start()
            rc = pltpu.make_async_remote_copy(
                src_ref=gh_refs[k], dst_ref=dst, send_sem=send_sems.at[k], recv_sem=recv_sems.at[k],
                device_id=(x, y, 1 - c), device_id_type=MESH)
            rc.start()
            outs.append((lc, rc))
        for k, (lc, rc) in enumerate(outs):
            lc.wait()
            rc.wait_send()
            pltpu.make_async_remote_copy(
                src_ref=gh_refs[k], dst_ref=_half_window(gs_refs[k], kinds[k], 1 - c),
                send_sem=send_sems.at[k], recv_sem=recv_sems.at[k],
                device_id=(x, y, 1 - c), device_id_type=MESH).wait_recv()

    return pl.pallas_call(
        body, name="reduce_finish",
        out_shape=[jax.ShapeDtypeStruct(s, F32) for s in shard_shapes],
        in_specs=[ANY] * n + [VM] * n, out_specs=[ANY] * n,
        scratch_shapes=[pltpu.VMEM(s, BF16) for s in hs] + [pltpu.VMEM(s, F32) for s in hs]
        + [pltpu.SemaphoreType.DMA((n,)) for _ in range(4)],
        compiler_params=_params(),
    )(*halves, *recvd)


def _mod_part(c_all, w_ada_l, b_l):
    def body(c_ref, w_ref, b_ref, o_ref):
        o_ref[...] = _dot(c_ref[...].astype(BF16), w_ref[...].astype(BF16)) + b_ref[...]

    return pl.pallas_call(
        body, name="mod_part", out_shape=jax.ShapeDtypeStruct((8, w_ada_l.shape[1]), F32),
        in_specs=[VM, VM, VM], out_specs=VM, compiler_params=_params(),
    )(c_all, w_ada_l, b_l)


def _sum_devices(parts, m):
    def body(p_ref, o_ref):
        acc = p_ref[0:m, :]
        for d in range(1, 8):
            acc = acc + p_ref[d * m:(d + 1) * m, :]
        o_ref[...] = acc

    return pl.pallas_call(
        body, name="sum_devices", out_shape=jax.ShapeDtypeStruct((m, parts.shape[1]), F32),
        in_specs=[VM], out_specs=VM, compiler_params=_params(),
    )(parts)


def _grad_w_ada(c_all_t, dmod_l):
    d, w = c_all_t.shape[0], dmod_l.shape[1]

    def body(ct_ref, dm_ref, o_ref):
        acc = ct_ref[:, 0:1] * dm_ref[0:1, :]
        for b in range(1, 8):
            acc = acc + ct_ref[:, b:b + 1] * dm_ref[b:b + 1, :]
        o_ref[...] = acc

    return pl.pallas_call(
        body, name="grad_w_ada", out_shape=jax.ShapeDtypeStruct((d, w), F32),
        in_specs=[VM, VM], out_specs=VM, compiler_params=_params(),
    )(c_all_t, dmod_l)


def _adamw_math(w, g, m, v):
    nm = ADAM_B1 * m + (1.0 - ADAM_B1) * g
    nv = ADAM_B2 * v + (1.0 - ADAM_B2) * (g * g)
    m_hat = nm / (1.0 - ADAM_B1 ** ADAM_STEP)
    v_hat = nv / (1.0 - ADAM_B2 ** ADAM_STEP)
    return -ADAM_LR * (m_hat / (jnp.sqrt(v_hat) + ADAM_EPS) + ADAM_WD * w), nm, nv


def _adamw_small(ws, g8, ms, vs):
    shapes = [jax.ShapeDtypeStruct(w.shape, F32) for w in ws]

    def body(*refs):
        w_refs, g_ref, m_refs, v_refs, outs = refs[0:6], refs[6], refs[7:13], refs[13:19], refs[19:]
        for idx in range(6):
            if idx < 5:
                g = g_ref[idx:idx + 1, :]
            else:
                g = jnp.concatenate([g_ref[5:6, :], g_ref[6:7, :], g_ref[7:8, :]], axis=1)
            res = (g,) + _adamw_math(w_refs[idx][...], g, m_refs[idx][...], v_refs[idx][...])
            for fam in range(4):
                outs[6 * fam + idx][...] = res[fam]

    return pl.pallas_call(
        body, name="adamw_small", out_shape=shapes * 4,
        in_specs=[VM] * 19, out_specs=[VM] * 24, compiler_params=_params(),
    )(*ws, g8, *ms, *vs)


def _adamw(w, g, m, v, name, after=()):
    r, c = w.shape
    tr = _row_tile(r, c, target=1024 * 1024, mult=8)

    def body(w_ref, g_ref, m_ref, v_ref, *rest):
        d_ref, nm_ref, nv_ref = rest[len(after):]
        d_ref[...], nm_ref[...], nv_ref[...] = _adamw_math(w_ref[...], g_ref[...], m_ref[...], v_ref[...])

    spec = pl.BlockSpec((tr, c), lambda i: (i, 0))
    return pl.pallas_call(
        body, name=name, grid=(r // tr,),
        out_shape=[jax.ShapeDtypeStruct((r, c), F32)] * 3,
        in_specs=[spec] * 4 + [ANY] * len(after), out_specs=[spec] * 3,
        compiler_params=_params(("parallel",)),
    )(w, g, m, v, *after)


def _rope_coefficients(pos, ropetab, ts):
    s_len = pos.shape[0]

    def body(pos_ref, tab_ref, o_ref):
        ang = pos_ref[...].astype(F32) * tab_ref[0:1, :]
        cs, sn = jnp.cos(ang), jnp.sin(ang)
        o_ref[0] = jnp.where(tab_ref[3:4, :] > 0, cs, 1.0)
        o_ref[1] = -sn * tab_ref[1:2, :]
        o_ref[2] = sn * tab_ref[2:3, :]

    return pl.pallas_call(
        body, name="rope_coefficients", grid=(s_len // ts,),
        in_specs=[pl.BlockSpec((ts, 1), lambda i: (i, 0)), pl.BlockSpec((8, LANES), lambda i: (0, 0))],
        out_specs=pl.BlockSpec((3, ts, LANES), lambda i: (0, i, 0)),
        out_shape=jax.ShapeDtypeStruct((3, s_len, LANES), F32),
        compiler_params=_params(("parallel",)),
    )(pos, ropetab)


def _deinterleave_store(vals, slab, out_ref, d, ts, dtype):
    if d == 1:
        for s in range(4):
            out_ref[0, :, s * LANES:(s + 1) * LANES] = vals[s].astype(dtype)
        return
    for s in range(4):
        slab[s] = vals[s]
    if d == 16:
        q = ts // 4
        for s in range(4):
            for r1 in range(4):
                slab[4 + s, r1 * q:(r1 + 1) * q, :] = slab[s, pl.ds(r1, q, stride=4), :]
        for r in range(d):
            r1, r2 = r % 4, r // 4
            for s in range(4):
                out_ref[r, :, s * LANES:(s + 1) * LANES] = slab[
                    4 + s, pl.ds(r1 * q + r2, ts // d, stride=4), :].astype(dtype)
        return
    for r in range(d):
        for s in range(4):
            out_ref[r, :, s * LANES:(s + 1) * LANES] = slab[s, pl.ds(r, ts // d, stride=d), :].astype(dtype)


def _interleave_load(blk_ref, slab, d, ts):
    if d == 1:
        return [blk_ref[0, :, s * LANES:(s + 1) * LANES] for s in range(4)]
    if d == 16:
        q = ts // 4
        for r in range(d):
            r1, r2 = r % 4, r // 4
            for s in range(4):
                slab[4 + s, pl.ds(r1 * q + r2, ts // d, stride=4), :] = blk_ref[r, :, s * LANES:(s + 1) * LANES]
        for s in range(4):
            for r1 in range(4):
                slab[s, pl.ds(r1, q, stride=4), :] = slab[4 + s, r1 * q:(r1 + 1) * q, :]
        return [slab[s] for s in range(4)]
    for r in range(d):
        for s in range(4):
            slab[s, pl.ds(r, ts // d, stride=d), :] = blk_ref[r, :, s * LANES:(s + 1) * LANES]
    return [slab[s] for s in range(4)]


def _norm_modulate(x, modv, norm_g, ts):
    s_len, d_model = x.shape

    def body(x_ref, mod_ref, g_ref, h_ref, ht_ref):
        xv = x_ref[...]
        r = lax.rsqrt(jnp.mean(xv * xv, axis=-1, keepdims=True) + EPS)
        h = (xv * r) * g_ref[...] * (1.0 + mod_ref[1:2, :]) + mod_ref[0:1, :]
        h_ref[...] = h.astype(BF16)
        ht_ref[...] = h.T.astype(BF16)

    return pl.pallas_call(
        body, name="norm_modulate", grid=(s_len // ts,),
        in_specs=[pl.BlockSpec((ts, d_model), lambda i: (i, 0)), pl.BlockSpec((8, d_model), lambda i: (0, 0)),
                  pl.BlockSpec((1, d_model), lambda i: (0, 0))],
        out_specs=[pl.BlockSpec((ts, d_model), lambda i: (i, 0)), pl.BlockSpec((d_model, ts), lambda i: (0, i))],
        out_shape=[jax.ShapeDtypeStruct((s_len, d_model), BF16), jax.ShapeDtypeStruct((d_model, s_len), BF16)],
        compiler_params=_params(("parallel",)),
    )(x, modv, norm_g)


def _plain_projection(h, w_in, first_tile, n_tiles, ts, name):
    s_len, d_model = h.shape
    ct = COL_TILE

    def body(h_ref, w_ref, o_ref):
        o_ref[...] = _dot(h_ref[...], w_ref[...])

    return pl.pallas_call(
        body, name=name, grid=(s_len // ts, n_tiles),
        in_specs=[pl.BlockSpec((ts, d_model), lambda i, j: (i, 0)),
                  pl.BlockSpec((d_model, ct), lambda i, j: (0, first_tile + j))],
        out_specs=pl.BlockSpec((ts, ct), lambda i, j: (i, j)),
        out_shape=jax.ShapeDtypeStruct((s_len, n_tiles * ct), F32),
        compiler_params=_params(("parallel", "arbitrary")),
    )(h, w_in)


def _qkv_projection(h, rope, w_in, gi, ts):
    s_len, d_model = h.shape
    dil = DILATIONS[gi]
    nc = 3 * d_model // COL_TILE
    hr = ts // 2
    assert COL_TILE == ATT_W and hr % (16 * dil) == 0

    def body(h_ref, rope_ref, wq_ref, wk_ref, wv_ref, o_ref, slab):
        rc, ra, rb = rope_ref.at[0], rope_ref.at[1], rope_ref.at[2]
        w_refs = (wq_ref, wk_ref, wv_ref)
        units = [(w, half) for w in range(3) for half in range(2)]

        def matmul(w, half):
            return _dot(h_ref[half * hr:(half + 1) * hr, :], w_refs[w][...])

        def finish(w, half, res):
            rows = slice(half * hr, (half + 1) * hr)
            vals = []
            for s in range(4):
                t = res[:, s * LANES:(s + 1) * LANES]
                if w < 2:
                    t = (t * rc[rows, :] + pltpu.roll(t, LANES - 8, 1) * ra[rows, :]
                         + pltpu.roll(t, 8, 1) * rb[rows, :])
                vals.append(t)
            out = o_ref.at[w, :, half * (hr // dil):(half + 1) * (hr // dil), :]
            _deinterleave_store(vals, slab.at[half], out, dil, hr, BF16)

        res_next = matmul(*units[0])
        for ui, (w, half) in enumerate(units):
            res = res_next
            if ui + 1 < len(units):
                res_next = matmul(*units[ui + 1])
            finish(w, half, res)

    def w_spec(w):
        return pl.BlockSpec((d_model, COL_TILE), lambda i: (0, nc + 3 * w + gi))

    return pl.pallas_call(
        body, name="qkv_projection_%d" % dil, grid=(s_len // ts,),
        in_specs=[pl.BlockSpec((ts, d_model), lambda i: (i, 0)), pl.BlockSpec((3, ts, LANES), lambda i: (0, i, 0)),
                  w_spec(0), w_spec(1), w_spec(2)],
        out_specs=pl.BlockSpec((3, dil, ts // dil, ATT_W), lambda i: (0, 0, i, 0)),
        out_shape=jax.ShapeDtypeStruct((3, dil, s_len // dil, ATT_W), BF16),
        scratch_shapes=[pltpu.VMEM((2, 8, hr, LANES), F32)],
        compiler_params=_params(("parallel",)),
    )(h, rope, w_in, w_in, w_in)


def _layernorm_stats(u1):
    mu = jnp.mean(u1, axis=-1, keepdims=True)
    xc = u1 - mu
    rstd = lax.rsqrt(jnp.mean(xc * xc, axis=-1, keepdims=True) + EPS)
    return xc * rstd, rstd


def _shifted_copies(win, shf, ts):
    rows = ts + HALO - 8
    for b in range(1, 8):
        shf[b - 1, 0:rows, :] = win[pl.ds(b, rows), :]


def _tap(win, shf, off, r0, rows):
    a, b = divmod(off, 8)
    start = 8 * a + r0
    if b == 0:
        return win[start:start + rows, :]
    return shf[b - 1, start:start + rows, :]


def _conv_forward(p_conv, conv_w, conv_b, ln_g, ln_b, w_co, ts):
    s_len, d3 = p_conv.shape
    dm = d3 // 3

    def body(p_ref, cw_ref, cb_ref, g_ref, b_ref, w_ref, y_ref, u0_ref, xh_ref, rstd_ref, win, shf, u1_ref):
        i = pl.program_id(0)

        @pl.when(i == 0)
        def _():
            win[0:HALO, :] = jnp.zeros((HALO, dm), F32)

        a, b, z = p_ref[:, 0:dm], p_ref[:, dm:2 * dm], p_ref[:, 2 * dm:3 * dm]
        u0 = a * _sig(b)
        win[HALO:HALO + ts, :] = u0
        u0_ref[...] = u0
        _shifted_copies(win, shf, ts)
        for r0 in range(0, ts, CONV_ROWS_FWD):
            acc = jnp.broadcast_to(cb_ref[...], (CONV_ROWS_FWD, dm))
            for k in range(CONV_K):
                acc = acc + cw_ref[k:k + 1, :] * _tap(win, shf, HALO - (CONV_K - 1) + k, r0, CONV_ROWS_FWD)
            u1_ref[r0:r0 + CONV_ROWS_FWD, :] = acc
        xh, rstd = _layernorm_stats(u1_ref[...])
        xh_ref[...] = xh
        rstd_ref[...] = rstd
        u2 = xh * g_ref[...] + b_ref[...]
        a_conv = (u2 * _sig(u2)) * (z * _sig(z))
        y_ref[...] = _dot(a_conv.astype(BF16), w_ref[...])
        win[0:HALO, :] = win[ts:ts + HALO, :]

    row = pl.BlockSpec((1, dm), lambda i: (0, 0))
    tile = pl.BlockSpec((ts, dm), lambda i: (i, 0))
    return pl.pallas_call(
        body, name="conv_forward", grid=(s_len // ts,),
        in_specs=[pl.BlockSpec((ts, d3), lambda i: (i, 0)),
                  pl.BlockSpec((HALO, dm), lambda i: (0, 0)), row, row, row,
                  pl.BlockSpec((dm, dm), lambda i: (0, 0))],
        out_specs=[tile, tile, tile, pl.BlockSpec((ts, 1), lambda i: (i, 0))],
        out_shape=[jax.ShapeDtypeStruct((s_len, dm), F32)] * 3 + [jax.ShapeDtypeStruct((s_len, 1), F32)],
        scratch_shapes=[pltpu.VMEM((ts + HALO, dm), F32), pltpu.VMEM((7, ts + HALO - 8, dm), F32),
                        pltpu.VMEM((ts, dm), F32)],
        compiler_params=_params(("arbitrary",)),
    )(p_conv, conv_w, conv_b, ln_g, ln_b, w_co)


def _conv_backward(dp, dyc, p_conv, u0, xh, rstd, conv_w, ln_g, ln_b, w_co, ts):
    s_len, d3 = p_conv.shape
    dm = d3 // 3
    nt = s_len // ts
    hb = ts // HALO

    def body(dp_in, dy_ref, p_ref, u0_ref, uh_ref, xh_ref, rstd_ref, cw_ref, g_ref, b_ref, w_ref,
             dp_ref, gw_ref, gs_ref, gcw_ref, dwin, uwin, shf):
        del dp_in
        i = pl.program_id(0)
        ti = nt - 1 - i

        @pl.when(i == 0)
        def _():
            gw_ref[...] = jnp.zeros_like(gw_ref)
            gs_ref[...] = jnp.zeros_like(gs_ref)
            gcw_ref[...] = jnp.zeros_like(gcw_ref)
            dwin[ts:ts + HALO, :] = jnp.zeros((HALO, dm), F32)

        dy = dy_ref[...]
        z = p_ref[:, 2 * dm:3 * dm]
        d_ac = _dot_nt(dy, w_ref[...])
        xh, rstd = xh_ref[...], rstd_ref[...]
        u2 = xh * g_ref[...] + b_ref[...]
        sg2, sgz = _sig(u2), _sig(z)
        u3, sz = u2 * sg2, z * sgz
        gw_ref[...] += _dot_tn((u3 * sz).astype(BF16), dy)
        d_z = d_ac * u3 * _dsilu(z, sgz)
        d_u2 = d_ac * sz * _dsilu(u2, sg2)
        gs_ref[0:1, :] += jnp.sum(d_u2 * xh, axis=0, keepdims=True)
        gs_ref[1:2, :] += jnp.sum(d_u2, axis=0, keepdims=True)
        dxh = d_u2 * g_ref[...]
        d_u1 = rstd * (dxh - jnp.mean(dxh, axis=-1, keepdims=True)
                       - xh * jnp.mean(dxh * xh, axis=-1, keepdims=True))
        gs_ref[2:3, :] += jnp.sum(d_u1, axis=0, keepdims=True)
        dwin[0:ts, :] = d_u1
        uwin[0:HALO, :] = jnp.where(ti == 0, 0.0, uh_ref[...])
        uwin[HALO:HALO + ts, :] = u0_ref[...]
        dp_ref[:, 2 * dm:3 * dm] = d_z.astype(BF16)
        _shifted_copies(uwin, shf, ts)
        for k in range(CONV_K):
            part = jnp.zeros((CONV_ROWS_BWD, dm), F32)
            for r0 in range(0, ts, CONV_ROWS_BWD):
                part = part + dwin[r0:r0 + CONV_ROWS_BWD, :] * _tap(uwin, shf, HALO - (CONV_K - 1) + k, r0,
                                                                    CONV_ROWS_BWD)
            gcw_ref[k:k + 1, :] += jnp.sum(part, axis=0, keepdims=True)
        _shifted_copies(dwin, shf, ts)
        for r0 in range(0, ts, CONV_ROWS_BWD):
            d_u0 = jnp.zeros((CONV_ROWS_BWD, dm), F32)
            for k in range(CONV_K):
                d_u0 = d_u0 + cw_ref[k:k + 1, :] * _tap(dwin, shf, CONV_K - 1 - k, r0, CONV_ROWS_BWD)
            rows = slice(r0, r0 + CONV_ROWS_BWD)
            sgb = _sig(p_ref[rows, dm:2 * dm])
            dp_ref[rows, 0:dm] = (d_u0 * sgb).astype(BF16)
            dp_ref[rows, dm:2 * dm] = (d_u0 * p_ref[rows, 0:dm] * sgb * (1.0 - sgb)).astype(BF16)
        dwin[ts:ts + HALO, :] = dwin[0:HALO, :]

    rev = lambda i: (nt - 1 - i, 0)
    row = pl.BlockSpec((1, dm), lambda i: (0, 0))
    tile = pl.BlockSpec((ts, dm), rev)
    return pl.pallas_call(
        body, name="conv_backward", grid=(nt,),
        in_specs=[ANY, tile, pl.BlockSpec((ts, d3), rev), tile,
                  pl.BlockSpec((HALO, dm), lambda i: (jnp.maximum((nt - 1 - i) * hb - 1, 0), 0)),
                  tile, pl.BlockSpec((ts, 1), rev), pl.BlockSpec((HALO, dm), lambda i: (0, 0)), row, row,
                  pl.BlockSpec((dm, dm), lambda i: (0, 0))],
        out_specs=[pl.BlockSpec((ts, d3), rev), pl.BlockSpec((dm, dm), lambda i: (0, 0)),
                   pl.BlockSpec((8, dm), lambda i: (0, 0)), pl.BlockSpec((HALO, dm), lambda i: (0, 0))],
        out_shape=[jax.ShapeDtypeStruct(dp.shape, BF16), jax.ShapeDtypeStruct((dm, dm), F32),
                   jax.ShapeDtypeStruct((8, dm), F32), jax.ShapeDtypeStruct((HALO, dm), F32)],
        input_output_aliases={0: 0},
        scratch_shapes=[pltpu.VMEM((ts + HALO, dm), F32), pltpu.VMEM((ts + HALO, dm), F32),
                        pltpu.VMEM((7, ts + HALO - 8, dm), F32)],
        compiler_params=_params(("arbitrary",)),
    )(dp, dyc, p_conv, u0, u0, xh, rstd, conv_w, ln_g, ln_b, w_co)


def _att_masks():
    head0 = lax.broadcasted_iota(jnp.int32, (BLK, LANES), 1) < 64
    col = lax.broadcasted_iota(jnp.int32, (BLK, 4 * BLK), 1)
    row = lax.broadcasted_iota(jnp.int32, (BLK, 4 * BLK), 0)
    kj = col % BLK
    prev = jnp.where(col < 2 * BLK, 1, 0)
    band = jnp.where(col < 2 * BLK, kj - row, row - kj)
    return head0, band, prev


def _fill_block_diagonal(dst, slab, src_ref, halo_ref, head0, nq):
    sl = slice(slab * LANES, (slab + 1) * LANES)
    for b in range(nq + 1):
        blk = halo_ref[:, sl] if b == 0 else src_ref[(b - 1) * BLK:b * BLK, sl]
        base = (slab * (nq + 1) + b) * 2 * BLK
        zero = jnp.zeros_like(blk)
        dst[base:base + BLK, :] = jnp.where(head0, blk, zero)
        dst[base + BLK:base + 2 * BLK, :] = jnp.where(head0, zero, blk)


def _attention_forward(qkv, seq_len, qt, name):
    s_len = qkv.shape[1]
    nq = qt // BLK
    tiles_per_seq = seq_len // qt

    def body(q_ref, k_ref, v_ref, kh_ref, vh_ref, o_ref, lse_ref, kbd, vbd):
        first = jnp.where((pl.program_id(0) % tiles_per_seq) == 0, 4 * BLK, 0)
        head0, band, prev = _att_masks()
        band_first = band - prev * first
        for p in range(4):
            _fill_block_diagonal(kbd, p, k_ref, kh_ref, head0, nq)
            _fill_block_diagonal(vbd, p, v_ref, vh_ref, head0, nq)
        units = [(p, n) for p in range(4) for n in range(nq)]

        def keys_of(p, n):
            base = (p * (nq + 1) + n) * 2 * BLK
            return slice(base, base + 4 * BLK)

        def scores(p, n):
            q2 = q_ref[n * BLK:(n + 1) * BLK, p * LANES:(p + 1) * LANES] * 0.125
            return _dot_nt(q2, kbd[keys_of(p, n), :])

        def finish(p, n, o, den, lse):
            rows, sl = slice(n * BLK, (n + 1) * BLK), slice(p * LANES, (p + 1) * LANES)
            o_ref[rows, sl] = o / jnp.where(head0, den[0], den[1])
            lse_ref[rows, sl] = jnp.where(head0, lse[0], lse[1])

        s_next = scores(*units[0])
        pending = None
        for ui, (p, n) in enumerate(units):
            s = s_next
            if ui + 1 < len(units):
                s_next = scores(*units[ui + 1])
            s = jnp.where((band_first if n == 0 else band) >= 0, s, NEG)
            grp = [s[:, g * BLK:(g + 1) * BLK] for g in range(4)]
            ps, den, lse = [None] * 4, [], []
            for h in range(2):
                m = jnp.max(jnp.maximum(grp[h], grp[2 + h]), axis=-1, keepdims=True)
                ps[h], ps[2 + h] = jnp.exp(grp[h] - m), jnp.exp(grp[2 + h] - m)
                dn = jnp.sum(ps[h] + ps[2 + h], axis=-1, keepdims=True)
                den.append(dn)
                lse.append(m + jnp.log(dn))
            pmat = jnp.concatenate([x.astype(BF16) for x in ps], axis=1)
            o = _dot(pmat, vbd[keys_of(p, n), :])
            if pending is not None:
                finish(*pending)
            pending = (p, n, o, den, lse)
        finish(*pending)

    def which(w):
        return pl.BlockSpec((None, qt, ATT_W), lambda i: (w, i, 0))

    def halo(w):
        return pl.BlockSpec((None, BLK, ATT_W), lambda i: (w, jnp.maximum(i * nq - 1, 0), 0))

    out = pl.BlockSpec((qt, ATT_W), lambda i: (i, 0))
    bd = pltpu.VMEM((4 * (nq + 1) * 2 * BLK, LANES), BF16)
    return pl.pallas_call(
        body, name=name, grid=(s_len // qt,),
        in_specs=[which(0), which(1), which(2), halo(1), halo(2)],
        out_specs=[out, out], out_shape=[jax.ShapeDtypeStruct((s_len, ATT_W), F32)] * 2,
        scratch_shapes=[bd, bd],
        compiler_params=_params(("parallel",)),
    )(qkv, qkv, qkv, qkv, qkv)


def _attention_backward(qkv, d_att, lse, delta, seq_len, qt, name):
    s_len = qkv.shape[1]
    nq = qt // BLK
    tiles_per_seq = seq_len // qt
    nblk = s_len // BLK

    def body(q_ref, k_ref, v_ref, kh_ref, vh_ref, do_ref, lse_ref, dl_ref,
             qn_ref, don_ref, lsen_ref, dln_ref, dqkv_ref, kbd, vbd):
        i = pl.program_id(0)
        first = jnp.where((i % tiles_per_seq) == 0, 4 * BLK, 0)
        last = jnp.where((i % tiles_per_seq) == tiles_per_seq - 1, 4 * BLK, 0)
        head0, band, prev = _att_masks()
        band_first = band - prev * first
        band_tail = band[:, 0:2 * BLK] - last
        for p in range(4):
            _fill_block_diagonal(kbd, p, k_ref, kh_ref, head0, nq)
            _fill_block_diagonal(vbd, p, v_ref, vh_ref, head0, nq)
        units = [(p, n) for p in range(4) for n in range(nq + 1)]

        def stage_a(p, n):
            sl = slice(p * LANES, (p + 1) * LANES)
            base = (p * (nq + 1) + n) * 2 * BLK
            if n < nq:
                rows = slice(n * BLK, (n + 1) * BLK)
                q2, do2, lse2, dl2 = q_ref[rows, sl], do_ref[rows, sl], lse_ref[rows, sl], dl_ref[rows, sl]
                keys = slice(base, base + 4 * BLK)
            else:
                q2, do2, lse2, dl2 = qn_ref[:, sl], don_ref[:, sl], lsen_ref[:, sl], dln_ref[:, sl]
                keys = slice(base, base + 2 * BLK)
            s = _dot_nt(q2 * 0.125, kbd[keys, :])
            dp = _dot_nt(do2, vbd[keys, :])
            return q2, do2, lse2, dl2, keys, s, dp

        def stage_b(n, lse2, dl2, s, dp):
            mask = band_tail if n == nq else (band_first if n == 0 else band)
            ps, dss = [], []
            for g in range(s.shape[1] // BLK):
                h = g % 2
                cols = slice(g * BLK, (g + 1) * BLK)
                pg = jnp.exp(jnp.where(mask[:, cols] >= 0, s[:, cols] - lse2[:, h * 64:h * 64 + 1], NEG))
                ps.append(pg.astype(BF16))
                dss.append((pg * (dp[:, cols] - dl2[:, h * 64:h * 64 + 1]) * 0.125).astype(BF16))
            return jnp.concatenate(ps, axis=1), jnp.concatenate(dss, axis=1)

        def heads(r, g):
            return jnp.where(head0, r[g * BLK:(g + 1) * BLK, :], r[(g + 1) * BLK:(g + 2) * BLK, :])

        a_next = stage_a(*units[0])
        carry = None
        for ui, (p, n) in enumerate(units):
            q2, do2, lse2, dl2, keys, s, dp = a_next
            if ui + 1 < len(units):
                a_next = stage_a(*units[ui + 1])
            pmat, dsmat = stage_b(n, lse2, dl2, s, dp)
            sl = slice(p * LANES, (p + 1) * LANES)
            if n < nq:
                dqkv_ref[0, n * BLK:(n + 1) * BLK, sl] = _dot(dsmat, kbd[keys, :])
            dkbd = _dot_tn(dsmat, q2)
            dvbd = _dot_tn(pmat, do2)
            if n > 0:
                prow = slice((n - 1) * BLK, n * BLK)
                dqkv_ref[1, prow, sl] = carry[0] + heads(dkbd, 0)
                dqkv_ref[2, prow, sl] = carry[1] + heads(dvbd, 0)
            if n < nq:
                carry = (heads(dkbd, 2), heads(dvbd, 2))

    def which(w):
        return pl.BlockSpec((None, qt, ATT_W), lambda i: (w, i, 0))

    def prev(w):
        return pl.BlockSpec((None, BLK, ATT_W), lambda i: (w, jnp.maximum(i * nq - 1, 0), 0))

    tile = pl.BlockSpec((qt, ATT_W), lambda i: (i, 0))
    nxt = pl.BlockSpec((BLK, ATT_W), lambda i: (jnp.minimum((i + 1) * nq, nblk - 1), 0))
    nxt_q = pl.BlockSpec((None, BLK, ATT_W), lambda i: (0, jnp.minimum((i + 1) * nq, nblk - 1), 0))
    return pl.pallas_call(
        body, name=name, grid=(s_len // qt,),
        in_specs=[which(0), which(1), which(2), prev(1), prev(2), tile, tile, tile, nxt_q, nxt, nxt, nxt],
        out_specs=pl.BlockSpec((3, qt, ATT_W), lambda i: (0, i, 0)),
        out_shape=jax.ShapeDtypeStruct((3, s_len, ATT_W), F32),
        scratch_shapes=[pltpu.VMEM((4 * (nq + 1) * 2 * BLK, LANES), BF16)] * 2,
        compiler_params=_params(("parallel",)),
    )(qkv, qkv, qkv, qkv, qkv, d_att, lse, delta, qkv, d_att, lse, delta)


def _merge_and_head(o_g, lse_g, p_gate, y_conv, x, tgt, modv, final_g, w_ao, w_o, ts):
    s_len, dm = x.shape
    gw = ATT_W + 2 * dm
    nt = s_len // ts
    gate_off = 3 * dm + 9 * ATT_W
    assert gate_off % gw == 0

    def body(o0, o1, o2, l0, l1, l2, pg_ref, yc_ref, x_ref, t_ref, mod_ref, fg_ref, wao_ref, wo_ref,
             loss_ref, dx_ref, dyc_ref, dpg_ref, da0, da1, da2, ls0, ls1, ls2, de0, de1, de2,
             gwo_ref, gwao_ref, gs_ref, slab, ones_scr):
        i = pl.program_id(0)

        @pl.when(i == 0)
        def _():
            loss_ref[...] = jnp.zeros_like(loss_ref)
            gwo_ref[...] = jnp.zeros_like(gwo_ref)
            gwao_ref[...] = jnp.zeros_like(gwao_ref)
            gs_ref[...] = jnp.zeros_like(gs_ref)
            ri = lax.broadcasted_iota(jnp.int32, (ATT_W, ATT_W), 0) // 64
            ci = lax.broadcasted_iota(jnp.int32, (ATT_W, ATT_W), 1) // 64
            ones_scr[...] = jnp.where(ri == ci, 1.0, 0.0).astype(BF16)

        os_, ls_ = [], []
        for dil, o_ref, l_ref in zip(DILATIONS, (o0, o1, o2), (l0, l1, l2)):
            os_.append(jnp.concatenate(_interleave_load(o_ref, slab, dil, ts), axis=1))
            ls_.append(jnp.concatenate(_interleave_load(l_ref, slab, dil, ts), axis=1))
        mx = jnp.maximum(jnp.maximum(ls_[0], ls_[1]), ls_[2])
        wts = [jnp.exp(l - mx) for l in ls_]
        wsum = wts[0] + wts[1] + wts[2]
        att = (wts[0] * os_[0] + wts[1] * os_[1] + wts[2] * os_[2]) / wsum
        lse_all = mx + jnp.log(wsum)

        z_att, g_conv, g_att = pg_ref[:, 0:ATT_W], pg_ref[:, ATT_W:ATT_W + dm], pg_ref[:, ATT_W + dm:gw]
        sgz = _sig(z_att)
        sz = z_att * sgz
        a_att = (att * sz).astype(BF16)
        y_att = _dot(a_att, wao_ref[...])
        y_conv = yc_ref[...]
        sgc, sga = _sig(g_conv), _sig(g_att)
        merged = (sgc * y_conv + sga * y_att).astype(BF16)
        mo = _dot(merged, wo_ref[...])
        gate = mod_ref[2:3, :]
        x2 = x_ref[...] + gate * mo
        r = lax.rsqrt(jnp.mean(x2 * x2, axis=-1, keepdims=True) + EPS)
        xr = x2 * r
        err = xr * fg_ref[...] - t_ref[...]
        loss_ref[...] += 0.5 * jnp.sum(jnp.mean(err * err, axis=-1, keepdims=True))
        dy = err * (1.0 / dm)
        gs_ref[0:1, :] += jnp.sum(dy * xr, axis=0, keepdims=True)
        dyg = dy * fg_ref[...]
        d_x2 = r * dyg - xr * (r * jnp.mean(dyg * xr, axis=-1, keepdims=True))
        dx_ref[...] = d_x2
        gs_ref[1:2, :] += jnp.sum(d_x2 * mo, axis=0, keepdims=True)
        d_mo = (d_x2 * gate).astype(BF16)
        d_mg = _dot_nt(d_mo, wo_ref[...])
        gwo_ref[...] += _dot_tn(merged, d_mo)
        dyc_ref[...] = (d_mg * sgc).astype(BF16)
        dpg_ref[:, ATT_W:ATT_W + dm] = (d_mg * y_conv * sgc * (1.0 - sgc)).astype(BF16)
        d_ya = (d_mg * sga).astype(BF16)
        dpg_ref[:, ATT_W + dm:gw] = (d_mg * y_att * sga * (1.0 - sga)).astype(BF16)
        gwao_ref[...] += _dot_tn(a_att, d_ya)
        d_aa = _dot_nt(d_ya, wao_ref[...])
        dpg_ref[:, 0:ATT_W] = (d_aa * att * _dsilu(z_att, sgz)).astype(BF16)
        d_att = d_aa * sz
        ones = ones_scr[...]
        prod = d_att * att
        hi = prod.astype(BF16)
        lo = (prod - hi.astype(F32)).astype(BF16)
        delta = _dot(hi, ones) + _dot(lo, ones)
        for val, refs, dt in ((d_att, (da0, da1, da2), BF16), (lse_all, (ls0, ls1, ls2), F32),
                              (delta, (de0, de1, de2), F32)):
            vals = [val[:, s * LANES:(s + 1) * LANES] for s in range(4)]
            for dil, ref in zip(DILATIONS, refs):
                _deinterleave_store(vals, slab, ref, dil, ts, dt)

    def grp(dil):
        return pl.BlockSpec((dil, ts // dil, ATT_W), lambda i: (0, i, 0))

    tile = pl.BlockSpec((ts, dm), lambda i: (i, 0))
    gate_tile = pl.BlockSpec((ts, gw), lambda i: (i, 0))
    const = lambda shp: pl.BlockSpec(shp, lambda i: tuple(0 for _ in shp))
    grp_shape = lambda dt: [jax.ShapeDtypeStruct((dil, s_len // dil, ATT_W), dt) for dil in DILATIONS]
    return pl.pallas_call(
        body, name="merge_and_head", grid=(nt,),
        in_specs=[grp(d) for d in DILATIONS] * 2
        + [gate_tile, tile, tile, tile, const((8, dm)), const((1, dm)), const((ATT_W, dm)), const((dm, dm))],
        out_specs=[const((8, LANES)), tile, tile, pl.BlockSpec((ts, gw), lambda i: (i, gate_off // gw))]
        + [grp(d) for d in DILATIONS] * 3
        + [const((dm, dm)), const((ATT_W, dm)), const((8, dm))],
        out_shape=[jax.ShapeDtypeStruct((8, LANES), F32), jax.ShapeDtypeStruct((s_len, dm), F32),
                   jax.ShapeDtypeStruct((s_len, dm), BF16), jax.ShapeDtypeStruct((s_len, gate_off + gw), BF16)]
        + grp_shape(BF16) + grp_shape(F32) + grp_shape(F32)
        + [jax.ShapeDtypeStruct((dm, dm), F32), jax.ShapeDtypeStruct((ATT_W, dm), F32),
           jax.ShapeDtypeStruct((8, dm), F32)],
        scratch_shapes=[pltpu.VMEM((8, ts, LANES), F32), pltpu.VMEM((ATT_W, ATT_W), BF16)],
        compiler_params=_params(("arbitrary",)),
    )(*o_g, *lse_g, p_gate, y_conv, x, tgt, modv, final_g, w_ao, w_o)


def _qkv_grad_to_tokens(dp, dqkv_g, rope, ts):
    s_len, dm = rope.shape[1], (dp.shape[1] - 10 * ATT_W) // 5
    qw = 3 * ATT_W
    assert (3 * dm) % qw == 0

    def body(dp_in, g0, g1, g2, rope_ref, o_ref, slab):
        del dp_in
        w = pl.program_id(1)
        rc, ra, rb = rope_ref.at[0], rope_ref.at[1], rope_ref.at[2]

        def emit(roped):
            for gi, (dil, g_ref) in enumerate(zip(DILATIONS, (g0, g1, g2))):
                vals = _interleave_load(g_ref, slab, dil, ts)
                for s in range(4):
                    t = vals[s]
                    if roped:
                        t = t * rc[...] + pltpu.roll(t * ra[...], 8, 1) + pltpu.roll(t * rb[...], LANES - 8, 1)
                    col = gi * ATT_W + s * LANES
                    o_ref[:, col:col + LANES] = t.astype(BF16)

        pl.when(w < 2)(lambda: emit(True))
        pl.when(w == 2)(lambda: emit(False))

    return pl.pallas_call(
        body, name="qkv_grad_to_tokens", grid=(s_len // ts, 3),
        in_specs=[ANY] + [pl.BlockSpec((None, dil, ts // dil, ATT_W), lambda i, w: (w, 0, i, 0)) for dil in DILATIONS]
        + [pl.BlockSpec((3, ts, LANES), lambda i, w: (0, i, 0))],
        out_specs=pl.BlockSpec((ts, qw), lambda i, w: (i, 3 * dm // qw + w)),
        out_shape=jax.ShapeDtypeStruct(dp.shape, BF16),
        input_output_aliases={0: 0},
        scratch_shapes=[pltpu.VMEM((8, ts, LANES), F32)],
        compiler_params=_params(("parallel", "arbitrary")),
    )(dp, *dqkv_g, rope)


def _wide_col_tile(cols):
    for width in (5 * COL_TILE, 2 * COL_TILE):
        if cols % width == 0:
            return width
    return COL_TILE


def _input_grad(dp, w_in, x, dx_res, modv, norm_g, ts):
    s_len, dm = x.shape
    ct = _wide_col_tile(dp.shape[1])
    nct = dp.shape[1] // ct

    def body(p_ref, w_ref, x_ref, dxr_ref, mod_ref, g_ref, gx_ref, gs_ref, acc):
        i, j = pl.program_id(0), pl.program_id(1)

        @pl.when((i == 0) & (j == 0))
        def _():
            gs_ref[...] = jnp.zeros_like(gs_ref)

        @pl.when(j == 0)
        def _():
            acc[...] = jnp.zeros_like(acc)

        acc[...] += _dot_nt(p_ref[...], w_ref[...])

        @pl.when(j == nct - 1)
        def _():
            d_h = acc[...]
            xv = x_ref[...]
            r = lax.rsqrt(jnp.mean(xv * xv, axis=-1, keepdims=True) + EPS)
            xr = xv * r
            gs_ref[0:1, :] += jnp.sum(d_h, axis=0, keepdims=True)
            gs_ref[1:2, :] += jnp.sum(d_h * (xr * g_ref[...]), axis=0, keepdims=True)
            d_n = d_h * (1.0 + mod_ref[1:2, :])
            gs_ref[2:3, :] += jnp.sum(d_n * xr, axis=0, keepdims=True)
            dxn = d_n * g_ref[...]
            gx_ref[...] = dxr_ref[...] + r * dxn - xr * (r * jnp.mean(dxn * xr, axis=-1, keepdims=True))

    tile = pl.BlockSpec((ts, dm), lambda i, j: (i, 0))
    return pl.pallas_call(
        body, name="input_grad", grid=(s_len // ts, nct),
        in_specs=[pl.BlockSpec((ts, ct), lambda i, j: (i, j)), pl.BlockSpec((dm, ct), lambda i, j: (0, j)), tile, tile,
                  pl.BlockSpec((8, dm), lambda i, j: (0, 0)), pl.BlockSpec((1, dm), lambda i, j: (0, 0))],
        out_specs=[tile, pl.BlockSpec((8, dm), lambda i, j: (0, 0))],
        out_shape=[jax.ShapeDtypeStruct((s_len, dm), F32), jax.ShapeDtypeStruct((8, dm), F32)],
        scratch_shapes=[pltpu.VMEM((ts, dm), F32)],
        compiler_params=_params(("arbitrary", "arbitrary")),
    )(dp, w_in, x, dx_res, modv, norm_g)


def _w_in_grad(h_t, dp, ts, tiles, name, prev=None, after=()):
    dm, s_len = h_t.shape
    ct = _wide_col_tile(dp.shape[1])
    lo, hi = tiles
    extra = ([prev] if prev is not None else []) + list(after)

    def body(h_ref, p_ref, *rest):
        o_ref = rest[-1]

        @pl.when(pl.program_id(1) == 0)
        def _():
            o_ref[...] = jnp.zeros_like(o_ref)

        o_ref[...] += _dot(h_ref[...], p_ref[...])

    return pl.pallas_call(
        body, name=name, grid=(hi - lo, s_len // ts),
        in_specs=[pl.BlockSpec((dm, ts), lambda j, i: (0, i)), pl.BlockSpec((ts, ct), lambda j, i: (i, lo + j))]
        + [ANY] * len(extra),
        out_specs=pl.BlockSpec((dm, ct), lambda j, i: (0, lo + j)),
        out_shape=jax.ShapeDtypeStruct((dm, dp.shape[1]), F32),
        input_output_aliases={2: 0} if prev is not None else {},
        compiler_params=_params(("arbitrary", "arbitrary")),
    )(h_t, dp, *extra)


def _rope_lane_table():
    l64 = jnp.arange(LANES) % 64
    half = ROT_DIM // 2
    inv_freq = ROPE_THETA ** (-(jnp.arange(half, dtype=F32) * 2.0 / ROT_DIM))
    rot = l64 < ROT_DIM
    rows = [jnp.where(rot, inv_freq[l64 % half], 0.0), (l64 < half).astype(F32),
            ((l64 >= half) & rot).astype(F32), rot.astype(F32)]
    return jnp.concatenate([jnp.stack(rows), jnp.zeros((4, LANES), F32)], axis=0)


def _tile_sizes(s_len):
    ts_big = min(1024, s_len // 2)
    ts_mid = 256
    ts_head = 256
    qt = [min(1024, s_len // dil) for dil in DILATIONS]
    return ts_big, ts_mid, ts_head, qt


def kernel(x, c, positions, norm_g, w_ada, b_ada, w_in, conv_w, conv_b, conv_ln_g, conv_ln_b, w_conv_out, w_att_out, w_o, final_g, loss_target, m_norm_g, m_w_ada, m_b_ada, m_w_in, m_conv_w, m_conv_b, m_conv_ln_g, m_conv_ln_b, m_w_conv_out, m_w_att_out, m_w_o, m_final_g, v_norm_g, v_w_ada, v_b_ada, v_w_in, v_conv_w, v_conv_b, v_conv_ln_g, v_conv_ln_b, v_w_conv_out, v_w_att_out, v_w_o, v_final_g):
    s_len, dm = x.shape[1], x.shape[2]
    ts_big, ts_mid, ts_head, qt = _tile_sizes(s_len)
    xi, yi, cidx = _my_place()
    chip = 2 * xi + yi
    batch = 4 * xi + 2 * yi + cidx
    x2d, tgt = x[0], loss_target[0]
    pos = positions.reshape(s_len, 1)
    wa_l, wi_l, cw_l = w_ada[0], w_in[0], conv_w[0]
    wco_l, wao_l, wo_l = w_conv_out[0], w_att_out[0], w_o[0]
    ada_w = wa_l.shape[1]
    cw_cols = cw_l.shape[1]

    kinds = ("col", "row", "col", "row")
    w_bufs = _cast_weights([wi_l, wco_l, wao_l, wo_l], kinds)

    cw_pad = jnp.pad(cw_l, ((0, HALO - CONV_K), (0, 0)))
    small_in = jnp.concatenate([jnp.broadcast_to(c, (8, dm)), cw_pad.reshape(8, dm)], axis=0)
    small = _allgather_small(small_in).reshape(8, 16, dm)
    c_all = small[:, 0, :]
    conv_w_full = jnp.concatenate(
        [small[2 * p, 8:16, :].reshape(HALO, cw_cols) for p in range(4)], axis=1)
    b_l = lax.dynamic_slice(b_ada, (0, chip * ada_w), (1, ada_w))
    mod_parts = _allgather_small(_mod_part(c_all, wa_l, b_l)).reshape(8, 8, ada_w)
    mod_rows = lax.dynamic_index_in_dim(mod_parts, batch, axis=1, keepdims=False)
    mod = jnp.concatenate([mod_rows[2 * p] for p in range(4)], axis=0).reshape(3, dm)
    modv = jnp.concatenate([mod, jnp.zeros((5, dm), F32)], axis=0)

    sems_a, sems_b, w_bufs, token = _gather_weights_start(w_bufs, kinds, modv)
    rope = _rope_coefficients(pos, _rope_lane_table() + token, ts_big)
    h_b, h_t = _norm_modulate(x2d, modv + token[0, 0], norm_g, ts_big)
    (w_in_b,) = _gather_weights_wait(w_bufs, kinds, sems_a, 0, (rope, h_b), "gather_weights_wait_w_in")
    w_in_b = _pass_to_sibling(w_in_b, kinds[0])

    n_conv, n_gate = 3 * dm // COL_TILE, (ATT_W + 2 * dm) // COL_TILE
    ts_proj = min(4 * ts_big, s_len)
    p_conv = _plain_projection(h_b, w_in_b, 0, n_conv, ts_proj, "conv_projection")
    p_gate = _plain_projection(h_b, w_in_b, n_conv + 9, n_gate, ts_proj, "gate_projection")
    qkv_g = [_qkv_projection(h_b, rope, w_in_b, gi, min(2 * ts_big, s_len)) for gi in range(3)]
    w_co_b, w_ao_b, w_o_b = _gather_weights_wait(w_bufs, kinds, sems_b, 1, (qkv_g[2], p_conv, p_gate),
                                                  "gather_weights_wait_rest")
    y_conv, u0, xh, rstd = _conv_forward(p_conv, conv_w_full, conv_b, conv_ln_g, conv_ln_b, w_co_b, 2 * ts_mid)
    qkv_flat = [q.reshape(3, s_len, ATT_W) for q in qkv_g]
    o_g, lse_g = [], []
    for gi, dil in enumerate(DILATIONS):
        o, l = _attention_forward(qkv_flat[gi], s_len // dil, qt[gi], "attention_forward_%d" % dil)
        o_g.append(o.reshape(dil, s_len // dil, ATT_W))
        lse_g.append(l.reshape(dil, s_len // dil, ATT_W))

    (loss_p, dx_res, dyc, dp, da0, da1, da2, ls0, ls1, ls2, de0, de1, de2,
     g_wo, g_wao, head_sums) = _merge_and_head(o_g, lse_g, p_gate, y_conv, x2d, tgt, modv,
                                               final_g.reshape(1, dm), w_ao_b, w_o_b, ts_head)

    dp, g_wco, conv_sums, g_cw = _conv_backward(dp, dyc, p_conv, u0, xh, rstd, conv_w_full, conv_ln_g, conv_ln_b,
                                                w_co_b, 2 * ts_mid)
    dqkv_g = []
    for gi, (dil, da, ls, de) in enumerate(zip(DILATIONS, (da0, da1, da2), (ls0, ls1, ls2), (de0, de1, de2))):
        flat = lambda a: a.reshape(s_len, ATT_W)
        dqkv = _attention_backward(qkv_flat[gi], flat(da), flat(ls), flat(de), s_len // dil, qt[gi],
                                   "attention_backward_%d" % dil)
        dqkv_g.append(dqkv.reshape(3, dil, s_len // dil, ATT_W))
    dp = _qkv_grad_to_tokens(dp, dqkv_g, rope, ts_big)
    ts_wg = min(2 * ts_big, s_len)
    n_wt = dp.shape[1] // _wide_col_tile(dp.shape[1])
    n_first = max(1, (3 * n_wt) // 4)
    split_col = n_first * _wide_col_tile(dp.shape[1])
    g_first = _w_in_grad(h_t, dp, ts_wg, (0, n_first), "w_in_grad_first")
    grads = [g_first, g_wco, g_wao, g_wo]
    full_shapes = [g.shape for g in grads]
    pas = [pltpu.with_memory_space_constraint(
        lax.empty((r // 2, c_) if kind == "col" else (r, c_ // 2), F32), pltpu.HBM)
        for (r, c_), kind in zip(full_shapes, kinds)]
    pe_s, pe_r, grads, pas, pe_token = _pair_exchange_start(grads, pas, kinds, (0, split_col))
    grads[0] = _w_in_grad(h_t, dp, ts_wg, (n_first, n_wt), "w_in_grad_second", prev=grads[0], after=(pe_token,))
    grads, recv_halves = _pair_exchange_wait(pe_s, pe_r, grads, pas, kinds, (0, split_col))
    recv_halves[0] = _pair_exchange_rest(grads[0], recv_halves[0], kinds[0], (split_col, dp.shape[1]))

    c_arr = jnp.reshape(cidx, (1,)).astype(jnp.int32)
    halves = [_reduce_pair_sum(g, pa, kind, c_arr, "reduce_pair_sum_%d" % k)
              for k, (g, pa, kind) in enumerate(zip(grads, recv_halves, kinds))]
    send_sems, recv_sems, halves, lands, token = _reduce_to_owner_start(halves, kinds, full_shapes)
    grad_x, in_sums = _input_grad(dp, w_in_b, x2d, dx_res, modv + token[0, 0], norm_g, ts_big)
    halves, recvd = _reduce_to_owner_wait(send_sems, recv_sems, halves, lands, in_sums, kinds)
    gr_win, gr_wco, gr_wao, gr_wo = _reduce_finish(halves, recvd, kinds, full_shapes)

    rows = [in_sums[2:3], conv_sums[2:3], conv_sums[0:1], conv_sums[1:2], head_sums[0:1],
            in_sums[0:1], in_sums[1:2], head_sums[1:2], g_cw, jnp.pad(loss_p, ((0, 0), (0, dm - LANES)))]
    part = jnp.concatenate(rows, axis=0)
    buf = lax.dynamic_update_slice(jnp.zeros((8 * 48, dm), F32), part, (batch * 48, 0))
    row_sems_s, row_sems_r, buf, row_token = _row_gather_start(buf, 48, gr_win)
    upd = {
        "w_in": _adamw(wi_l, gr_win, m_w_in[0], v_w_in[0], "adamw_w_in", (row_token,)),
        "w_co": _adamw(wco_l, gr_wco, m_w_conv_out[0], v_w_conv_out[0], "adamw_w_conv_out", (row_token,)),
        "w_ao": _adamw(wao_l, gr_wao, m_w_att_out[0], v_w_att_out[0], "adamw_w_att_out", (row_token,)),
        "w_o": _adamw(wo_l, gr_wo, m_w_o[0], v_w_o[0], "adamw_w_o", (row_token,)),
    }
    gathered = _row_gather_wait(row_sems_s, row_sems_r, buf, 48, [upd[k][0] for k in ("w_in", "w_co", "w_ao", "w_o")])
    tot = _sum_devices(gathered, 48)
    loss = tot[8 + HALO, 0]
    dmod_all = gathered.reshape(8, 48, dm)[:, 5:8, :].reshape(8, 3 * dm)
    dmod_l = lax.dynamic_slice(dmod_all, (0, chip * ada_w), (8, ada_w))
    gr_wada = _grad_w_ada(c_all.T, dmod_l)
    gr_cw = lax.dynamic_slice(tot[8:8 + HALO], (0, chip * cw_cols), (HALO, cw_cols))

    pad_cw = lambda a: jnp.pad(a[0], ((0, HALO - CONV_K), (0, 0)))
    row = lambda a: a.reshape(1, dm)
    small_upd = _adamw_small(
        [norm_g, conv_b, conv_ln_g, conv_ln_b, row(final_g), b_ada], tot[0:8],
        [m_norm_g, m_conv_b, m_conv_ln_g, m_conv_ln_b, row(m_final_g), m_b_ada],
        [v_norm_g, v_conv_b, v_conv_ln_g, v_conv_ln_b, row(v_final_g), v_b_ada])
    upd["w_ada"] = _adamw(wa_l, gr_wada, m_w_ada[0], v_w_ada[0], "adamw_w_ada")
    upd["conv_w"] = _adamw(cw_pad, gr_cw, pad_cw(m_conv_w), pad_cw(v_conv_w), "adamw_conv_w")

    def family(which):
        if which is None:
            sm = small_upd[0:6]
            big = {"w_ada": gr_wada, "w_in": gr_win, "conv_w": gr_cw, "w_co": gr_wco, "w_ao": gr_wao, "w_o": gr_wo}
        else:
            sm = small_upd[6 * (which + 1):6 * (which + 2)]
            big = {k: upd[k][which] for k in ("w_ada", "w_in", "conv_w", "w_co", "w_ao", "w_o")}
        return [sm[0], big["w_ada"][None], sm[5], big["w_in"][None],
                big["conv_w"][None, :CONV_K], sm[1], sm[2], sm[3], big["w_co"][None],
                big["w_ao"][None], big["w_o"][None], sm[4].reshape(dm)]

    return (loss, grad_x[None], *family(None), *family(0), *family(1), *family(2))
```

```python
import jax
import jax.numpy as jnp
from jax import lax
from jax.experimental import pallas as pl
from jax.experimental.pallas import tpu as pltpu

F32 = jnp.float32
BF16 = jnp.bfloat16
MESH = pl.DeviceIdType.MESH
ANY = pl.BlockSpec(memory_space=pl.ANY)
VM = pl.BlockSpec(memory_space=pltpu.VMEM)

EPS = 1e-6
NEG = -1e30
ATT_W = 512
DILATIONS = (1, 4, 16)
BLK = 128
CONV_K = 31
HALO = 32
CONV_ROWS_FWD = 32
CONV_ROWS_BWD = 16
ROT_DIM = 16
ROPE_THETA = 500000.0
COL_TILE = 512
LANES = 128
VMEM_LIMIT = 56 * 1024 * 1024

ADAM_LR, ADAM_B1, ADAM_B2, ADAM_EPS, ADAM_WD, ADAM_STEP = 0.001, 0.9, 0.999, 1e-08, 0.01, 10


def _params(sem=None, vmem=VMEM_LIMIT):
    return pltpu.CompilerParams(dimension_semantics=sem, vmem_limit_bytes=vmem)


def _dot(a, b):
    return jnp.dot(a, b, preferred_element_type=F32)


def _dot_nt(a, b):
    return lax.dot_general(a, b, (((1,), (1,)), ((), ())), preferred_element_type=F32)


def _dot_tn(a, b):
    return lax.dot_general(a, b, (((0,), (0,)), ((), ())), preferred_element_type=F32)


def _sig(x):
    return jax.nn.sigmoid(x)


def _dsilu(x, s):
    return s * (1.0 + x * (1.0 - s))


def _my_place():
    return lax.axis_index("x"), lax.axis_index("y"), lax.axis_index("c")


def _allgather_small(x_shard):
    m_per, n = x_shard.shape

    def body(x_ref, out_ref, send_sems, recv_sems, local_sem):
        x, y, c = _my_place()
        me, sibling = (x, y, c), (x, y, 1 - c)
        chips = [(1 - x, y), (x, 1 - y), (1 - x, 1 - y)]

        def rows(px, py, pc):
            return out_ref.at[pl.ds((4 * px + 2 * py + pc) * m_per, m_per), :]

        def copy(k, block, to, src=None):
            return pltpu.make_async_remote_copy(
                src_ref=rows(*block) if src is None else src, dst_ref=rows(*block),
                send_sem=send_sems.at[k], recv_sem=recv_sems.at[k],
                device_id=to, device_id_type=MESH)

        mine = pltpu.make_async_copy(x_ref, rows(*me), local_sem)
        mine.start()
        first = [copy(0, me, sibling, src=x_ref)]
        first += [copy(1 + j, me, (*chip, c), src=x_ref) for j, chip in enumerate(chips)]
        for cp in first:
            cp.start()
        passed = [copy(4 + j, (*chip, c), sibling) for j, chip in enumerate(chips)]
        for j, chip in enumerate(chips):
            copy(1 + j, (*chip, c), me).wait_recv()
            passed[j].start()
        copy(0, sibling, me).wait_recv()
        for j, chip in enumerate(chips):
            copy(4 + j, (*chip, 1 - c), me).wait_recv()
        for cp in first + passed:
            cp.wait_send()
        mine.wait()

    return pl.pallas_call(
        body, name="allgather_small",
        out_shape=jax.ShapeDtypeStruct((8 * m_per, n), x_shard.dtype),
        in_specs=[VM], out_specs=VM,
        scratch_shapes=[pltpu.SemaphoreType.DMA((7,)), pltpu.SemaphoreType.DMA((7,)),
                        pltpu.SemaphoreType.DMA],
    )(x_shard)


def _shard_window(ref, kind, p, n_shards=4):
    r, c = ref.shape
    if kind == "col":
        w = c // n_shards
        return ref.at[:, pl.ds(p * w, w)]
    w = r // n_shards
    return ref.at[pl.ds(p * w, w), :]


def _half_window(ref, kind, hc):
    r, c = ref.shape
    if kind == "col":
        return ref.at[pl.ds(hc * (r // 2), r // 2), :]
    return ref.at[:, pl.ds(hc * (c // 2), c // 2)]


def _landed(ref, kind, chip, hc):
    return _half_window(_shard_window(ref, kind, 2 * chip[0] + chip[1]), kind, hc)


def _cast_weights(shards, kinds):
    n = len(shards)
    full_shapes = [(s.shape[0], 4 * s.shape[1]) if kind == "col" else (4 * s.shape[0], s.shape[1])
                   for s, kind in zip(shards, kinds)]

    def body(*refs):
        w_refs, out_refs, bf_refs, sems = refs[:n], refs[n:2 * n], refs[2 * n:3 * n], refs[3 * n]
        x, y, _ = _my_place()
        cps = []
        for k in range(n):
            bf_refs[k][...] = w_refs[k][...].astype(BF16)
            cp = pltpu.make_async_copy(bf_refs[k], _shard_window(out_refs[k], kinds[k], 2 * x + y), sems.at[k])
            cp.start()
            cps.append(cp)
        for cp in cps:
            cp.wait()

    return pl.pallas_call(
        body, name="cast_weights",
        out_shape=[jax.ShapeDtypeStruct(s, BF16) for s in full_shapes],
        in_specs=[VM] * n, out_specs=[ANY] * n,
        scratch_shapes=[pltpu.VMEM(s.shape, BF16) for s in shards] + [pltpu.SemaphoreType.DMA((n,))],
        compiler_params=_params(),
    )(*shards)


def _gather_copies(refs, kinds, sems_a, sems_b):
    x, y, c = _my_place()
    chips = [(1 - x, y), (x, 1 - y), (1 - x, 1 - y)]
    set_a, set_b = [], []
    for j, chip in enumerate(chips):
        for k in range(len(refs)):
            if refs[k] is None:
                continue
            for flip in ((0,) if k == 0 else (0, 1)):
                win = _landed(refs[k], kinds[k], (x, y), c)
                sems, idx = (sems_a, j) if k == 0 else (sems_b, ((k - 1) * 3 + j) * 2 + flip)
                mine = _landed(refs[k], kinds[k], chip, (c + flip) % 2)
                (set_a if k == 0 else set_b).append((
                    pltpu.make_async_remote_copy(
                        src_ref=win, dst_ref=win, send_sem=sems[0].at[idx], recv_sem=sems[1].at[idx],
                        device_id=(*chip, (c + flip) % 2), device_id_type=MESH),
                    pltpu.make_async_remote_copy(
                        src_ref=mine, dst_ref=mine, send_sem=sems[0].at[idx], recv_sem=sems[1].at[idx],
                        device_id=(*chip, (c + flip) % 2), device_id_type=MESH)))
    return set_a, set_b


_SPLIT = dict(has_side_effects=pltpu.SideEffectType.DATAFLOW_SIDE_EFFECTING)


def _gather_weights_start(fulls, kinds, after):
    n = len(fulls)
    hbm = pl.BlockSpec(memory_space=pltpu.HBM)
    sem = pl.BlockSpec(memory_space=pltpu.SEMAPHORE)
    nb = (n - 1) * 6

    def body(*refs):
        in_refs = refs[:n]
        sa, ra, sb, rb = refs[n + 1:n + 5]
        token = refs[-1]
        set_a, set_b = _gather_copies(in_refs, kinds, (sa, ra), (sb, rb))
        for out_cp, _ in set_a + set_b:
            out_cp.start()
        token[...] = jnp.zeros_like(token)

    out = pl.pallas_call(
        body, name="gather_weights_start",
        out_shape=[pltpu.SemaphoreType.DMA((3,)), pltpu.SemaphoreType.DMA((3,)),
                   pltpu.SemaphoreType.DMA((nb,)), pltpu.SemaphoreType.DMA((nb,))]
        + [pltpu.HBM(f.shape, f.dtype) for f in fulls] + [jax.ShapeDtypeStruct((8, LANES), F32)],
        in_specs=[hbm] * n + [ANY], out_specs=[sem] * 4 + [hbm] * n + [VM],
        input_output_aliases={k: 4 + k for k in range(n)},
        compiler_params=pltpu.CompilerParams(**_SPLIT),
    )(*[pltpu.with_memory_space_constraint(f, pltpu.HBM) for f in fulls], after)
    return (out[0], out[1]), (out[2], out[3]), out[4:4 + n], out[-1]


def _gather_weights_wait(fulls, kinds, sems, which, after, name):
    n = len(fulls)
    hbm = pl.BlockSpec(memory_space=pltpu.HBM)
    sem = pl.BlockSpec(memory_space=pltpu.SEMAPHORE)
    keep = [0] if which == 0 else list(range(1, n))

    def body(*refs):
        m = len(keep)
        full_refs = [None] * n
        for pos_, k in enumerate(keep):
            full_refs[k] = refs[pos_]
        s_ref, r_ref = refs[m:m + 2]
        if which == 0:
            sets = _gather_copies([full_refs[0]], kinds[:1], (s_ref, r_ref), None)[0]
        else:
            sets = _gather_copies([None] + [full_refs[k] for k in keep], kinds, None, (s_ref, r_ref))[1]
        for out_cp, in_cp in sets:
            out_cp.wait_send()
            in_cp.wait_recv()

    out = pl.pallas_call(
        body, name=name,
        out_shape=[pltpu.HBM(fulls[k].shape, fulls[k].dtype) for k in keep],
        in_specs=[hbm] * len(keep) + [sem, sem] + [ANY] * len(after), out_specs=[hbm] * len(keep),
        input_output_aliases={i: i for i in range(len(keep))},
        compiler_params=pltpu.CompilerParams(**_SPLIT),
    )(*[fulls[k] for k in keep], sems[0], sems[1], *after)
    return list(out)


def _row_gather_copies(buf, m_per, send_sems, recv_sems):
    x, y, c = _my_place()
    pairs = []
    for idx in range(1, 8):
        dx, dy, dc = idx // 4, (idx // 2) % 2, idx % 2
        peer = ((x + dx) % 2, (y + dy) % 2, (c + dc) % 2)
        mine = buf.at[pl.ds((4 * x + 2 * y + c) * m_per, m_per), :]
        theirs = buf.at[pl.ds((4 * peer[0] + 2 * peer[1] + peer[2]) * m_per, m_per), :]
        pairs.append(tuple(
            pltpu.make_async_remote_copy(src_ref=w, dst_ref=w, send_sem=send_sems.at[idx - 1],
                                         recv_sem=recv_sems.at[idx - 1], device_id=peer, device_id_type=MESH)
            for w in (mine, theirs)))
    return pairs


def _row_gather_start(buf, m_per, after):
    hbm = pl.BlockSpec(memory_space=pltpu.HBM)
    sem = pl.BlockSpec(memory_space=pltpu.SEMAPHORE)

    def body(buf_ref, after_ref, send_sems, recv_sems, out_ref, token):
        del after_ref, out_ref
        for out_cp, _ in _row_gather_copies(buf_ref, m_per, send_sems, recv_sems):
            out_cp.start()
        token[...] = jnp.zeros_like(token)

    return pl.pallas_call(
        body, name="row_gather_start",
        out_shape=[pltpu.SemaphoreType.DMA((7,)), pltpu.SemaphoreType.DMA((7,)), pltpu.HBM(buf.shape, buf.dtype),
                   jax.ShapeDtypeStruct((8, LANES), F32)],
        in_specs=[hbm, ANY], out_specs=[sem, sem, hbm, VM], input_output_aliases={0: 2},
        compiler_params=pltpu.CompilerParams(**_SPLIT),
    )(pltpu.with_memory_space_constraint(buf, pltpu.HBM), after)


def _row_gather_wait(send_sems, recv_sems, buf, m_per, after):
    hbm = pl.BlockSpec(memory_space=pltpu.HBM)
    sem = pl.BlockSpec(memory_space=pltpu.SEMAPHORE)

    def body(buf_ref, send_s, recv_s, *rest):
        for out_cp, in_cp in _row_gather_copies(buf_ref, m_per, send_s, recv_s):
            out_cp.wait_send()
            in_cp.wait_recv()

    return pl.pallas_call(
        body, name="row_gather_wait", out_shape=pltpu.HBM(buf.shape, buf.dtype),
        in_specs=[hbm, sem, sem] + [ANY] * len(after), out_specs=hbm, input_output_aliases={0: 0},
        compiler_params=pltpu.CompilerParams(**_SPLIT),
    )(buf, send_sems, recv_sems, *after)


def _pass_to_sibling(w_full, kind):
    def body(w_in_ref, w_ref, send_sems, recv_sems):
        del w_in_ref
        x, y, c = _my_place()
        chips = [(1 - x, y), (x, 1 - y), (1 - x, 1 - y)]
        cps = []
        for j, chip in enumerate(chips):
            win = _landed(w_ref, kind, chip, c)
            cp = pltpu.make_async_remote_copy(
                src_ref=win, dst_ref=win, send_sem=send_sems.at[j], recv_sem=recv_sems.at[j],
                device_id=(x, y, 1 - c), device_id_type=MESH)
            cp.start()
            cps.append(cp)
        for j, chip in enumerate(chips):
            theirs = _landed(w_ref, kind, chip, 1 - c)
            pltpu.make_async_remote_copy(
                src_ref=theirs, dst_ref=theirs, send_sem=send_sems.at[j], recv_sem=recv_sems.at[j],
                device_id=(x, y, 1 - c), device_id_type=MESH).wait_recv()
        for cp in cps:
            cp.wait_send()

    return pl.pallas_call(
        body, name="pass_to_sibling",
        out_shape=jax.ShapeDtypeStruct(w_full.shape, w_full.dtype),
        in_specs=[ANY], out_specs=ANY, input_output_aliases={0: 0},
        scratch_shapes=[pltpu.SemaphoreType.DMA((3,)), pltpu.SemaphoreType.DMA((3,))],
    )(w_full)


def _pair_copies(g_refs, pa_refs, kinds, send_sems, recv_sems, cols, with_rest):
    x, y, c = _my_place()
    lo, hi = cols

    def copy(k, src, dst):
        return pltpu.make_async_remote_copy(src_ref=src, dst_ref=dst, send_sem=send_sems.at[k], recv_sem=recv_sems.at[k],
                                            device_id=(x, y, 1 - c), device_id_type=MESH)

    cps = [copy(0, _half_window(g_refs[0], kinds[0], 1 - c).at[:, pl.ds(lo, hi - lo)],
                pa_refs[0].at[:, pl.ds(lo, hi - lo)])]
    if with_rest:
        cps += [copy(k, _half_window(g_refs[k], kinds[k], 1 - c), pa_refs[k]) for k in range(1, len(g_refs))]
    return cps


def _pair_exchange_start(grads, pas, kinds, cols):
    n = len(grads)
    hbm = pl.BlockSpec(memory_space=pltpu.HBM)
    sem = pl.BlockSpec(memory_space=pltpu.SEMAPHORE)

    def body(*refs):
        g_refs, pa_refs = refs[:n], refs[n:2 * n]
        send_sems, recv_sems = refs[2 * n:2 * n + 2]
        for cp in _pair_copies(g_refs, pa_refs, kinds, send_sems, recv_sems, cols, True):
            cp.start()
        refs[-1][...] = jnp.zeros_like(refs[-1])

    out = pl.pallas_call(
        body, name="pair_exchange_start",
        out_shape=[pltpu.SemaphoreType.DMA((n,)), pltpu.SemaphoreType.DMA((n,))]
        + [pltpu.HBM(a.shape, a.dtype) for a in list(grads) + list(pas)] + [jax.ShapeDtypeStruct((8, LANES), F32)],
        in_specs=[hbm] * (2 * n), out_specs=[sem, sem] + [hbm] * (2 * n) + [VM],
        input_output_aliases={k: 2 + k for k in range(2 * n)},
        compiler_params=pltpu.CompilerParams(**_SPLIT),
    )(*[pltpu.with_memory_space_constraint(a, pltpu.HBM) for a in list(grads) + list(pas)])
    return out[0], out[1], list(out[2:2 + n]), list(out[2 + n:2 + 2 * n]), out[-1]


def _pair_exchange_wait(send_sems, recv_sems, grads, pas, kinds, cols):
    n = len(grads)
    hbm = pl.BlockSpec(memory_space=pltpu.HBM)
    sem = pl.BlockSpec(memory_space=pltpu.SEMAPHORE)

    def body(*refs):
        g_refs, pa_refs = refs[:n], refs[n:2 * n]
        send_s, recv_s = refs[2 * n:2 * n + 2]
        for cp in _pair_copies(g_refs, pa_refs, kinds, send_s, recv_s, cols, True):
            cp.wait_send()
            cp.wait_recv()

    out = pl.pallas_call(
        body, name="pair_exchange_wait",
        out_shape=[pltpu.HBM(a.shape, a.dtype) for a in list(grads) + list(pas)],
        in_specs=[hbm] * (2 * n) + [sem, sem], out_specs=[hbm] * (2 * n),
        input_output_aliases={k: k for k in range(2 * n)},
        compiler_params=pltpu.CompilerParams(**_SPLIT),
    )(*grads, *pas, send_sems, recv_sems)
    return list(out[:n]), list(out[n:])


def _pair_exchange_rest(g, pa, kind, cols):
    def body(g_ref, pa_in, pa_ref, send_sems, recv_sems):
        del pa_in
        (cp,) = _pair_copies([g_ref], [pa_ref], [kind], send_sems, recv_sems, cols, False)
        cp.start()
        cp.wait()

    return pl.pallas_call(
        body, name="pair_exchange_rest", out_shape=jax.ShapeDtypeStruct(pa.shape, pa.dtype),
        in_specs=[ANY, ANY], out_specs=ANY, input_output_aliases={1: 0},
        scratch_shapes=[pltpu.SemaphoreType.DMA((1,)), pltpu.SemaphoreType.DMA((1,))],
    )(g, pa)


def _row_tile(rows, cols, itemsize=4, target=2 * 1024 * 1024, mult=16):
    t = rows
    while t % 2 == 0 and t // 2 >= mult and (t // 2) % mult == 0 and t * cols * itemsize > target:
        t //= 2
    return t


def _reduce_pair_sum(g, pa, kind, c_arr, name):
    hr, hc_ = pa.shape
    tr = _row_tile(hr, hc_)
    nb = hr // tr

    def body(c_ref, g_ref, pa_ref, o_ref):
        o_ref[...] = (g_ref[...] + pa_ref[...]).astype(BF16)

    if kind == "col":
        g_map = lambda i, c_ref: (c_ref[0] * nb + i, 0)
    else:
        g_map = lambda i, c_ref: (i, c_ref[0])
    return pl.pallas_call(
        body, name=name,
        grid_spec=pltpu.PrefetchScalarGridSpec(
            num_scalar_prefetch=1, grid=(nb,),
            in_specs=[pl.BlockSpec((tr, hc_), g_map), pl.BlockSpec((tr, hc_), lambda i, c_ref: (i, 0))],
            out_specs=pl.BlockSpec((tr, hc_), lambda i, c_ref: (i, 0))),
        out_shape=jax.ShapeDtypeStruct((hr, hc_), BF16),
        compiler_params=_params(("parallel",)),
    )(c_arr, g, pa)


def _half_shard_shape(full_shape, kind):
    r, c = full_shape
    return (r // 2, c // 4) if kind == "col" else (r // 4, c // 2)


def _to_owner_copies(h_refs, land_refs, send_sems, recv_sems, kinds):
    n = len(h_refs)
    x, y, c = _my_place()
    chips = [(1 - x, y), (x, 1 - y), (1 - x, 1 - y)]
    cps = []
    for j, chip in enumerate(chips):
        pj = 2 * chip[0] + chip[1]
        for k in range(n):
            cps.append(pltpu.make_async_remote_copy(
                src_ref=_shard_window(h_refs[k], kinds[k], pj), dst_ref=land_refs[k].at[j],
                send_sem=send_sems.at[j * n + k], recv_sem=recv_sems.at[j * n + k],
                device_id=(*chip, c), device_id_type=MESH))
    return cps


def _reduce_to_owner_start(halves, kinds, full_shapes):
    n = len(halves)
    hs = [_half_shard_shape(fs, kind) for fs, kind in zip(full_shapes, kinds)]
    hbm = pl.BlockSpec(memory_space=pltpu.HBM)
    sem = pl.BlockSpec(memory_space=pltpu.SEMAPHORE)

    def body(*refs):
        h_refs, land_refs = refs[:n], refs[n:2 * n]
        send_sems, recv_sems = refs[2 * n:2 * n + 2]
        token = refs[-1]
        for cp in _to_owner_copies(h_refs, land_refs, send_sems, recv_sems, kinds):
            cp.start()
        token[...] = jnp.zeros_like(token)

    lands = [pltpu.with_memory_space_constraint(lax.empty((3,) + s, BF16), pltpu.HBM) for s in hs]
    out = pl.pallas_call(
        body, name="reduce_to_owner_start",
        out_shape=[pltpu.SemaphoreType.DMA((3 * n,)), pltpu.SemaphoreType.DMA((3 * n,))]
        + [pltpu.HBM(h.shape, h.dtype) for h in halves] + [pltpu.HBM((3,) + s, BF16) for s in hs]
        + [jax.ShapeDtypeStruct((8, LANES), F32)],
        in_specs=[hbm] * (2 * n), out_specs=[sem, sem] + [hbm] * (2 * n) + [VM],
        input_output_aliases={k: 2 + k for k in range(2 * n)},
        compiler_params=pltpu.CompilerParams(has_side_effects=pltpu.SideEffectType.DATAFLOW_SIDE_EFFECTING),
    )(*[pltpu.with_memory_space_constraint(h, pltpu.HBM) for h in halves], *lands)
    return out[0], out[1], out[2:2 + n], out[2 + n:2 + 2 * n], out[-1]


def _reduce_to_owner_wait(send_sems, recv_sems, halves, lands, after, kinds):
    n = len(halves)
    hbm = pl.BlockSpec(memory_space=pltpu.HBM)
    sem = pl.BlockSpec(memory_space=pltpu.SEMAPHORE)

    def body(*refs):
        h_refs, land_refs = refs[:n], refs[n:2 * n]
        send_s, recv_s = refs[2 * n:2 * n + 2]
        for cp in _to_owner_copies(h_refs, land_refs, send_s, recv_s, kinds):
            cp.wait_send()
            cp.wait_recv()

    out = pl.pallas_call(
        body, name="reduce_to_owner_wait",
        out_shape=[pltpu.HBM(h.shape, h.dtype) for h in halves] + [pltpu.HBM(l.shape, l.dtype) for l in lands],
        in_specs=[hbm] * (2 * n) + [sem, sem, ANY], out_specs=[hbm] * (2 * n),
        input_output_aliases={k: k for k in range(2 * n)},
        compiler_params=pltpu.CompilerParams(has_side_effects=pltpu.SideEffectType.DATAFLOW_SIDE_EFFECTING),
    )(*halves, *lands, send_sems, recv_sems, after)
    return out[:n], out[n:]


def _reduce_finish(halves, recvd, kinds, full_shapes):
    n = len(halves)
    hs = [_half_shard_shape(fs, kind) for fs, kind in zip(full_shapes, kinds)]
    shard_shapes = [(fs[0], fs[1] // 4) if kind == "col" else (fs[0] // 4, fs[1])
                    for fs, kind in zip(full_shapes, kinds)]

    def body(*refs):
        h_refs, rc_refs, gs_refs = refs[:n], refs[n:2 * n], refs[2 * n:3 * n]
        own_refs, gh_refs = refs[3 * n:4 * n], refs[4 * n:5 * n]
        in_sems, loc_sems, send_sems, recv_sems = refs[5 * n:]
        x, y, c = _my_place()
        p = 2 * x + y
        loads = []
        for k in range(n):
            cp = pltpu.make_async_copy(_shard_window(h_refs[k], kinds[k], p), own_refs[k], in_sems.at[k])
            cp.start()
            loads.append(cp)
        outs = []
        for k in range(n):
            loads[k].wait()
            gh_refs[k][...] = (own_refs[k][...].astype(F32) + rc_refs[k][0].astype(F32)
                               + rc_refs[k][1].astype(F32) + rc_refs[k][2].astype(F32))
            dst = _half_window(gs_refs[k], kinds[k], c)
            lc = pltpu.make_async_copy(gh_refs[k], dst, loc_sems.at[k])
            lc.start()
            rc = pltpu.make_async_remote_copy(
                src_ref=gh_refs[k], dst_ref=dst, send_sem=send_sems.at[k], recv_sem=recv_sems.at[k],
                device_id=(x, y, 1 - c), device_id_type=MESH)
            rc.start()
            outs.append((lc, rc))
        for k, (lc, rc) in enumerate(outs):
            lc.wait()
            rc.wait_send()
            pltpu.make_async_remote_copy(
                src_ref=gh_refs[k], dst_ref=_half_window(gs_refs[k], kinds[k], 1 - c),
                send_sem=send_sems.at[k], recv_sem=recv_sems.at[k],
                device_id=(x, y, 1 - c), device_id_type=MESH).wait_recv()

    return pl.pallas_call(
        body, name="reduce_finish",
        out_shape=[jax.ShapeDtypeStruct(s, F32) for s in shard_shapes],
        in_specs=[ANY] * n + [VM] * n, out_specs=[ANY] * n,
        scratch_shapes=[pltpu.VMEM(s, BF16) for s in hs] + [pltpu.VMEM(s, F32) for s in hs]
        + [pltpu.SemaphoreType.DMA((n,)) for _ in range(4)],
        compiler_params=_params(),
    )(*halves, *recvd)


def _mod_part(c_all, w_ada_l, b_l):
    def body(c_ref, w_ref, b_ref, o_ref):
        o_ref[...] = _dot(c_ref[...].astype(BF16), w_ref[...].astype(BF16)) + b_ref[...]

    return pl.pallas_call(
        body, name="mod_part", out_shape=jax.ShapeDtypeStruct((8, w_ada_l.shape[1]), F32),
        in_specs=[VM, VM, VM], out_specs=VM, compiler_params=_params(),
    )(c_all, w_ada_l, b_l)


def _sum_devices(parts, m):
    def body(p_ref, o_ref):
        acc = p_ref[0:m, :]
        for d in range(1, 8):
            acc = acc + p_ref[d * m:(d + 1) * m, :]
        o_ref[...] = acc

    return pl.pallas_call(
        body, name="sum_devices", out_shape=jax.ShapeDtypeStruct((m, parts.shape[1]), F32),
        in_specs=[VM], out_specs=VM, compiler_params=_params(),
    )(parts)


def _grad_w_ada(c_all_t, dmod_l):
    d, w = c_all_t.shape[0], dmod_l.shape[1]

    def body(ct_ref, dm_ref, o_ref):
        acc = ct_ref[:, 0:1] * dm_ref[0:1, :]
        for b in range(1, 8):
            acc = acc + ct_ref[:, b:b + 1] * dm_ref[b:b + 1, :]
        o_ref[...] = acc

    return pl.pallas_call(
        body, name="grad_w_ada", out_shape=jax.ShapeDtypeStruct((d, w), F32),
        in_specs=[VM, VM], out_specs=VM, compiler_params=_params(),
    )(c_all_t, dmod_l)


def _adamw_math(w, g, m, v):
    nm = ADAM_B1 * m + (1.0 - ADAM_B1) * g
    nv = ADAM_B2 * v + (1.0 - ADAM_B2) * (g * g)
    m_hat = nm / (1.0 - ADAM_B1 ** ADAM_STEP)
    v_hat = nv / (1.0 - ADAM_B2 ** ADAM_STEP)
    return -ADAM_LR * (m_hat / (jnp.sqrt(v_hat) + ADAM_EPS) + ADAM_WD * w), nm, nv


def _adamw_small(ws, g8, ms, vs):
    shapes = [jax.ShapeDtypeStruct(w.shape, F32) for w in ws]

    def body(*refs):
        w_refs, g_ref, m_refs, v_refs, outs = refs[0:6], refs[6], refs[7:13], refs[13:19], refs[19:]
        for idx in range(6):
            if idx < 5:
                g = g_ref[idx:idx + 1, :]
            else:
                g = jnp.concatenate([g_ref[5:6, :], g_ref[6:7, :], g_ref[7:8, :]], axis=1)
            res = (g,) + _adamw_math(w_refs[idx][...], g, m_refs[idx][...], v_refs[idx][...])
            for fam in range(4):
                outs[6 * fam + idx][...] = res[fam]

    return pl.pallas_call(
        body, name="adamw_small", out_shape=shapes * 4,
        in_specs=[VM] * 19, out_specs=[VM] * 24, compiler_params=_params(),
    )(*ws, g8, *ms, *vs)


def _adamw(w, g, m, v, name, after=()):
    r, c = w.shape
    tr = _row_tile(r, c, target=1024 * 1024, mult=8)

    def body(w_ref, g_ref, m_ref, v_ref, *rest):
        d_ref, nm_ref, nv_ref = rest[len(after):]
        d_ref[...], nm_ref[...], nv_ref[...] = _adamw_math(w_ref[...], g_ref[...], m_ref[...], v_ref[...])

    spec = pl.BlockSpec((tr, c), lambda i: (i, 0))
    return pl.pallas_call(
        body, name=name, grid=(r // tr,),
        out_shape=[jax.ShapeDtypeStruct((r, c), F32)] * 3,
        in_specs=[spec] * 4 + [ANY] * len(after), out_specs=[spec] * 3,
        compiler_params=_params(("parallel",)),
    )(w, g, m, v, *after)


def _rope_coefficients(pos, ropetab, ts):
    s_len = pos.shape[0]

    def body(pos_ref, tab_ref, o_ref):
        ang = pos_ref[...].astype(F32) * tab_ref[0:1, :]
        cs, sn = jnp.cos(ang), jnp.sin(ang)
        o_ref[0] = jnp.where(tab_ref[3:4, :] > 0, cs, 1.0)
        o_ref[1] = -sn * tab_ref[1:2, :]
        o_ref[2] = sn * tab_ref[2:3, :]

    return pl.pallas_call(
        body, name="rope_coefficients", grid=(s_len // ts,),
        in_specs=[pl.BlockSpec((ts, 1), lambda i: (i, 0)), pl.BlockSpec((8, LANES), lambda i: (0, 0))],
        out_specs=pl.BlockSpec((3, ts, LANES), lambda i: (0, i, 0)),
        out_shape=jax.ShapeDtypeStruct((3, s_len, LANES), F32),
        compiler_params=_params(("parallel",)),
    )(pos, ropetab)


def _deinterleave_store(vals, slab, out_ref, d, ts, dtype):
    if d == 1:
        for s in range(4):
            out_ref[0, :, s * LANES:(s + 1) * LANES] = vals[s].astype(dtype)
        return
    for s in range(4):
        slab[s] = vals[s]
    if d == 16:
        q = ts // 4
        for s in range(4):
            for r1 in range(4):
                slab[4 + s, r1 * q:(r1 + 1) * q, :] = slab[s, pl.ds(r1, q, stride=4), :]
        for r in range(d):
            r1, r2 = r % 4, r // 4
            for s in range(4):
                out_ref[r, :, s * LANES:(s + 1) * LANES] = slab[
                    4 + s, pl.ds(r1 * q + r2, ts // d, stride=4), :].astype(dtype)
        return
    for r in range(d):
        for s in range(4):
            out_ref[r, :, s * LANES:(s + 1) * LANES] = slab[s, pl.ds(r, ts // d, stride=d), :].astype(dtype)


def _interleave_load(blk_ref, slab, d, ts):
    if d == 1:
        return [blk_ref[0, :, s * LANES:(s + 1) * LANES] for s in range(4)]
    if d == 16:
        q = ts // 4
        for r in range(d):
            r1, r2 = r % 4, r // 4
            for s in range(4):
                slab[4 + s, pl.ds(r1 * q + r2, ts // d, stride=4), :] = blk_ref[r, :, s * LANES:(s + 1) * LANES]
        for s in range(4):
            for r1 in range(4):
                slab[s, pl.ds(r1, q, stride=4), :] = slab[4 + s, r1 * q:(r1 + 1) * q, :]
        return [slab[s] for s in range(4)]
    for r in range(d):
        for s in range(4):
            slab[s, pl.ds(r, ts // d, stride=d), :] = blk_ref[r, :, s * LANES:(s + 1) * LANES]
    return [slab[s] for s in range(4)]


def _norm_modulate(x, modv, norm_g, ts):
    s_len, d_model = x.shape

    def body(x_ref, mod_ref, g_ref, h_ref, ht_ref):
        xv = x_ref[...]
        r = lax.rsqrt(jnp.mean(xv * xv, axis=-1, keepdims=True) + EPS)
        h = (xv * r) * g_ref[...] * (1.0 + mod_ref[1:2, :]) + mod_ref[0:1, :]
        h_ref[...] = h.astype(BF16)
        ht_ref[...] = h.T.astype(BF16)

    return pl.pallas_call(
        body, name="norm_modulate", grid=(s_len // ts,),
        in_specs=[pl.BlockSpec((ts, d_model), lambda i: (i, 0)), pl.BlockSpec((8, d_model), lambda i: (0, 0)),
                  pl.BlockSpec((1, d_model), lambda i: (0, 0))],
        out_specs=[pl.BlockSpec((ts, d_model), lambda i: (i, 0)), pl.BlockSpec((d_model, ts), lambda i: (0, i))],
        out_shape=[jax.ShapeDtypeStruct((s_len, d_model), BF16), jax.ShapeDtypeStruct((d_model, s_len), BF16)],
        compiler_params=_params(("parallel",)),
    )(x, modv, norm_g)


def _plain_projection(h, w_in, first_tile, n_tiles, ts, name):
    s_len, d_model = h.shape
    width = n_tiles * COL_TILE
    assert (first_tile * COL_TILE) % width == 0

    def body(h_ref, w_ref, o_ref):
        for t in range(n_tiles):
            cols = slice(t * COL_TILE, (t + 1) * COL_TILE)
            o_ref[:, cols] = _dot(h_ref[...], w_ref[:, cols])

    return pl.pallas_call(
        body, name=name, grid=(s_len // ts,),
        in_specs=[pl.BlockSpec((ts, d_model), lambda i: (i, 0)),
                  pl.BlockSpec((d_model, width), lambda i: (0, first_tile * COL_TILE // width),
                               pipeline_mode=pl.Buffered(1))],
        out_specs=pl.BlockSpec((ts, width), lambda i: (i, 0)),
        out_shape=jax.ShapeDtypeStruct((s_len, width), F32),
        compiler_params=_params(("parallel",)),
    )(h, w_in)


def _qkv_projection(h, rope, w_in, gi, ts):
    s_len, d_model = h.shape
    dil = DILATIONS[gi]
    nc = 3 * d_model // COL_TILE
    hr = ts // 2
    assert COL_TILE == ATT_W and hr % (16 * dil) == 0

    def body(h_ref, rope_ref, wq_ref, wk_ref, wv_ref, o_ref, slab):
        rc, ra, rb = rope_ref.at[0], rope_ref.at[1], rope_ref.at[2]
        w_refs = (wq_ref, wk_ref, wv_ref)
        units = [(w, half) for w in range(3) for half in range(2)]

        def matmul(w, half):
            return _dot(h_ref[half * hr:(half + 1) * hr, :], w_refs[w][...])

        def finish(w, half, res):
            rows = slice(half * hr, (half + 1) * hr)
            vals = []
            for s in range(4):
                t = res[:, s * LANES:(s + 1) * LANES]
                if w < 2:
                    t = (t * rc[rows, :] + pltpu.roll(t, LANES - 8, 1) * ra[rows, :]
                         + pltpu.roll(t, 8, 1) * rb[rows, :])
                vals.append(t)
            out = o_ref.at[w, :, half * (hr // dil):(half + 1) * (hr // dil), :]
            _deinterleave_store(vals, slab.at[half], out, dil, hr, BF16)

        res_next = matmul(*units[0])
        for ui, (w, half) in enumerate(units):
            res = res_next
            if ui + 1 < len(units):
                res_next = matmul(*units[ui + 1])
            finish(w, half, res)

    def w_spec(w):
        return pl.BlockSpec((d_model, COL_TILE), lambda i: (0, nc + 3 * w + gi))

    return pl.pallas_call(
        body, name="qkv_projection_%d" % dil, grid=(s_len // ts,),
        in_specs=[pl.BlockSpec((ts, d_model), lambda i: (i, 0)), pl.BlockSpec((3, ts, LANES), lambda i: (0, i, 0)),
                  w_spec(0), w_spec(1), w_spec(2)],
        out_specs=pl.BlockSpec((3, dil, ts // dil, ATT_W), lambda i: (0, 0, i, 0)),
        out_shape=jax.ShapeDtypeStruct((3, dil, s_len // dil, ATT_W), BF16),
        scratch_shapes=[pltpu.VMEM((2, 8, hr, LANES), F32)],
        compiler_params=_params(("parallel",)),
    )(h, rope, w_in, w_in, w_in)


def _layernorm_stats(u1):
    mu = jnp.mean(u1, axis=-1, keepdims=True)
    xc = u1 - mu
    rstd = lax.rsqrt(jnp.mean(xc * xc, axis=-1, keepdims=True) + EPS)
    return xc * rstd, rstd


def _shifted_copies(win, shf, ts):
    rows = ts + HALO - 8
    for b in range(1, 8):
        shf[b - 1, 0:rows, :] = win[pl.ds(b, rows), :]


def _tap(win, shf, off, r0, rows):
    a, b = divmod(off, 8)
    start = 8 * a + r0
    if b == 0:
        return win[start:start + rows, :]
    return shf[b - 1, start:start + rows, :]


def _conv_forward(p_conv, conv_w, conv_b, ln_g, ln_b, w_co, ts):
    s_len, d3 = p_conv.shape
    dm = d3 // 3

    def body(p_ref, cw_ref, cb_ref, g_ref, b_ref, w_ref, y_ref, u0_ref, xh_ref, rstd_ref, win, shf, u1_ref):
        i = pl.program_id(0)

        @pl.when(i == 0)
        def _():
            win[0:HALO, :] = jnp.zeros((HALO, dm), F32)

        a, b, z = p_ref[:, 0:dm], p_ref[:, dm:2 * dm], p_ref[:, 2 * dm:3 * dm]
        u0 = a * _sig(b)
        win[HALO:HALO + ts, :] = u0
        u0_ref[...] = u0
        _shifted_copies(win, shf, ts)
        for r0 in range(0, ts, CONV_ROWS_FWD):
            acc = jnp.broadcast_to(cb_ref[...], (CONV_ROWS_FWD, dm))
            for k in range(CONV_K):
                acc = acc + cw_ref[k:k + 1, :] * _tap(win, shf, HALO - (CONV_K - 1) + k, r0, CONV_ROWS_FWD)
            u1_ref[r0:r0 + CONV_ROWS_FWD, :] = acc
        xh, rstd = _layernorm_stats(u1_ref[...])
        xh_ref[...] = xh
        rstd_ref[...] = rstd
        u2 = xh * g_ref[...] + b_ref[...]
        a_conv = (u2 * _sig(u2)) * (z * _sig(z))
        y_ref[...] = _dot(a_conv.astype(BF16), w_ref[...])
        win[0:HALO, :] = win[ts:ts + HALO, :]

    row = pl.BlockSpec((1, dm), lambda i: (0, 0))
    tile = pl.BlockSpec((ts, dm), lambda i: (i, 0))
    return pl.pallas_call(
        body, name="conv_forward", grid=(s_len // ts,),
        in_specs=[pl.BlockSpec((ts, d3), lambda i: (i, 0)),
                  pl.BlockSpec((HALO, dm), lambda i: (0, 0)), row, row, row,
                  pl.BlockSpec((dm, dm), lambda i: (0, 0))],
        out_specs=[tile, tile, tile, pl.BlockSpec((ts, 1), lambda i: (i, 0))],
        out_shape=[jax.ShapeDtypeStruct((s_len, dm), F32)] * 3 + [jax.ShapeDtypeStruct((s_len, 1), F32)],
        scratch_shapes=[pltpu.VMEM((ts + HALO, dm), F32), pltpu.VMEM((7, ts + HALO - 8, dm), F32),
                        pltpu.VMEM((ts, dm), F32)],
        compiler_params=_params(("arbitrary",)),
    )(p_conv, conv_w, conv_b, ln_g, ln_b, w_co)


def _conv_backward(dp, dyc, p_conv, u0, xh, rstd, conv_w, ln_g, ln_b, w_co, ts):
    s_len, d3 = p_conv.shape
    dm = d3 // 3
    nt = s_len // ts
    hb = ts // HALO

    def body(dp_in, dy_ref, p_ref, u0_ref, uh_ref, xh_ref, rstd_ref, cw_ref, g_ref, b_ref, w_ref,
             dp_ref, gw_ref, gs_ref, gcw_ref, dwin, uwin, shf):
        del dp_in
        i = pl.program_id(0)
        ti = nt - 1 - i

        @pl.when(i == 0)
        def _():
            gw_ref[...] = jnp.zeros_like(gw_ref)
            gs_ref[...] = jnp.zeros_like(gs_ref)
            gcw_ref[...] = jnp.zeros_like(gcw_ref)
            dwin[ts:ts + HALO, :] = jnp.zeros((HALO, dm), F32)

        dy = dy_ref[...]
        z = p_ref[:, 2 * dm:3 * dm]
        d_ac = _dot_nt(dy, w_ref[...])
        xh, rstd = xh_ref[...], rstd_ref[...]
        u2 = xh * g_ref[...] + b_ref[...]
        sg2, sgz = _sig(u2), _sig(z)
        u3, sz = u2 * sg2, z * sgz
        gw_ref[...] += _dot_tn((u3 * sz).astype(BF16), dy)
        d_z = d_ac * u3 * _dsilu(z, sgz)
        d_u2 = d_ac * sz * _dsilu(u2, sg2)
        gs_ref[0:1, :] += jnp.sum(d_u2 * xh, axis=0, keepdims=True)
        gs_ref[1:2, :] += jnp.sum(d_u2, axis=0, keepdims=True)
        dxh = d_u2 * g_ref[...]
        d_u1 = rstd * (dxh - jnp.mean(dxh, axis=-1, keepdims=True)
                       - xh * jnp.mean(dxh * xh, axis=-1, keepdims=True))
        gs_ref[2:3, :] += jnp.sum(d_u1, axis=0, keepdims=True)
        dwin[0:ts, :] = d_u1
        uwin[0:HALO, :] = jnp.where(ti == 0, 0.0, uh_ref[...])
        uwin[HALO:HALO + ts, :] = u0_ref[...]
        dp_ref[:, 2 * dm:3 * dm] = d_z.astype(BF16)
        _shifted_copies(uwin, shf, ts)
        for k in range(CONV_K):
            part = jnp.zeros((CONV_ROWS_BWD, dm), F32)
            for r0 in range(0, ts, CONV_ROWS_BWD):
                part = part + dwin[r0:r0 + CONV_ROWS_BWD, :] * _tap(uwin, shf, HALO - (CONV_K - 1) + k, r0,
                                                                    CONV_ROWS_BWD)
            gcw_ref[k:k + 1, :] += jnp.sum(part, axis=0, keepdims=True)
        _shifted_copies(dwin, shf, ts)
        for r0 in range(0, ts, CONV_ROWS_BWD):
            d_u0 = jnp.zeros((CONV_ROWS_BWD, dm), F32)
            for k in range(CONV_K):
                d_u0 = d_u0 + cw_ref[k:k + 1, :] * _tap(dwin, shf, CONV_K - 1 - k, r0, CONV_ROWS_BWD)
            rows = slice(r0, r0 + CONV_ROWS_BWD)
            sgb = _sig(p_ref[rows, dm:2 * dm])
            dp_ref[rows, 0:dm] = (d_u0 * sgb).astype(BF16)
            dp_ref[rows, dm:2 * dm] = (d_u0 * p_ref[rows, 0:dm] * sgb * (1.0 - sgb)).astype(BF16)
        dwin[ts:ts + HALO, :] = dwin[0:HALO, :]

    rev = lambda i: (nt - 1 - i, 0)
    row = pl.BlockSpec((1, dm), lambda i: (0, 0))
    tile = pl.BlockSpec((ts, dm), rev)
    return pl.pallas_call(
        body, name="conv_backward", grid=(nt,),
        in_specs=[ANY, tile, pl.BlockSpec((ts, d3), rev), tile,
                  pl.BlockSpec((HALO, dm), lambda i: (jnp.maximum((nt - 1 - i) * hb - 1, 0), 0)),
                  tile, pl.BlockSpec((ts, 1), rev), pl.BlockSpec((HALO, dm), lambda i: (0, 0)), row, row,
                  pl.BlockSpec((dm, dm), lambda i: (0, 0))],
        out_specs=[pl.BlockSpec((ts, d3), rev), pl.BlockSpec((dm, dm), lambda i: (0, 0)),
                   pl.BlockSpec((8, dm), lambda i: (0, 0)), pl.BlockSpec((HALO, dm), lambda i: (0, 0))],
        out_shape=[jax.ShapeDtypeStruct(dp.shape, BF16), jax.ShapeDtypeStruct((dm, dm), F32),
                   jax.ShapeDtypeStruct((8, dm), F32), jax.ShapeDtypeStruct((HALO, dm), F32)],
        input_output_aliases={0: 0},
        scratch_shapes=[pltpu.VMEM((ts + HALO, dm), F32), pltpu.VMEM((ts + HALO, dm), F32),
                        pltpu.VMEM((7, ts + HALO - 8, dm), F32)],
        compiler_params=_params(("arbitrary",)),
    )(dp, dyc, p_conv, u0, u0, xh, rstd, conv_w, ln_g, ln_b, w_co)


def _att_masks():
    head0 = lax.broadcasted_iota(jnp.int32, (BLK, LANES), 1) < 64
    col = lax.broadcasted_iota(jnp.int32, (BLK, 4 * BLK), 1)
    row = lax.broadcasted_iota(jnp.int32, (BLK, 4 * BLK), 0)
    kj = col % BLK
    prev = jnp.where(col < 2 * BLK, 1, 0)
    band = jnp.where(col < 2 * BLK, kj - row, row - kj)
    return head0, band, prev


def _fill_block_diagonal(dst, slab, src_ref, halo_ref, head0, nq):
    sl = slice(slab * LANES, (slab + 1) * LANES)
    for b in range(nq + 1):
        blk = halo_ref[:, sl] if b == 0 else src_ref[(b - 1) * BLK:b * BLK, sl]
        base = (slab * (nq + 1) + b) * 2 * BLK
        zero = jnp.zeros_like(blk)
        dst[base:base + BLK, :] = jnp.where(head0, blk, zero)
        dst[base + BLK:base + 2 * BLK, :] = jnp.where(head0, zero, blk)


def _attention_forward(qkv, seq_len, qt, name):
    s_len = qkv.shape[1]
    nq = qt // BLK
    tiles_per_seq = seq_len // qt

    def body(q_ref, k_ref, v_ref, kh_ref, vh_ref, o_ref, lse_ref, kbd, vbd):
        first = jnp.where((pl.program_id(0) % tiles_per_seq) == 0, 4 * BLK, 0)
        head0, band, prev = _att_masks()
        band_first = band - prev * first
        for p in range(4):
            _fill_block_diagonal(kbd, p, k_ref, kh_ref, head0, nq)
            _fill_block_diagonal(vbd, p, v_ref, vh_ref, head0, nq)
        units = [(p, n) for p in range(4) for n in range(nq)]

        def keys_of(p, n):
            base = (p * (nq + 1) + n) * 2 * BLK
            return slice(base, base + 4 * BLK)

        def scores(p, n):
            q2 = q_ref[n * BLK:(n + 1) * BLK, p * LANES:(p + 1) * LANES] * 0.125
            return _dot_nt(q2, kbd[keys_of(p, n), :])

        def finish(p, n, o, den, lse):
            rows, sl = slice(n * BLK, (n + 1) * BLK), slice(p * LANES, (p + 1) * LANES)
            o_ref[rows, sl] = o / jnp.where(head0, den[0], den[1])
            lse_ref[rows, sl] = jnp.where(head0, lse[0], lse[1])

        s_next = scores(*units[0])
        pending = None
        for ui, (p, n) in enumerate(units):
            s = s_next
            if ui + 1 < len(units):
                s_next = scores(*units[ui + 1])
            s = jnp.where((band_first if n == 0 else band) >= 0, s, NEG)
            grp = [s[:, g * BLK:(g + 1) * BLK] for g in range(4)]
            ps, den, lse = [None] * 4, [], []
            for h in range(2):
                m = jnp.max(jnp.maximum(grp[h], grp[2 + h]), axis=-1, keepdims=True)
                ps[h], ps[2 + h] = jnp.exp(grp[h] - m), jnp.exp(grp[2 + h] - m)
                dn = jnp.sum(ps[h] + ps[2 + h], axis=-1, keepdims=True)
                den.append(dn)
                lse.append(m + jnp.log(dn))
            pmat = jnp.concatenate([x.astype(BF16) for x in ps], axis=1)
            o = _dot(pmat, vbd[keys_of(p, n), :])
            if pending is not None:
                finish(*pending)
            pending = (p, n, o, den, lse)
        finish(*pending)

    def which(w):
        return pl.BlockSpec((None, qt, ATT_W), lambda i: (w, i, 0))

    def halo(w):
        return pl.BlockSpec((None, BLK, ATT_W), lambda i: (w, jnp.maximum(i * nq - 1, 0), 0))

    out = pl.BlockSpec((qt, ATT_W), lambda i: (i, 0))
    bd = pltpu.VMEM((4 * (nq + 1) * 2 * BLK, LANES), BF16)
    return pl.pallas_call(
        body, name=name, grid=(s_len // qt,),
        in_specs=[which(0), which(1), which(2), halo(1), halo(2)],
        out_specs=[out, out], out_shape=[jax.ShapeDtypeStruct((s_len, ATT_W), F32)] * 2,
        scratch_shapes=[bd, bd],
        compiler_params=_params(("parallel",)),
    )(qkv, qkv, qkv, qkv, qkv)


def _attention_backward(qkv, d_att, lse, delta, seq_len, qt, name):
    s_len = qkv.shape[1]
    nq = qt // BLK
    tiles_per_seq = seq_len // qt
    nblk = s_len // BLK

    def body(q_ref, k_ref, v_ref, kh_ref, vh_ref, do_ref, lse_ref, dl_ref,
             qn_ref, don_ref, lsen_ref, dln_ref, dqkv_ref, kbd, vbd):
        i = pl.program_id(0)
        first = jnp.where((i % tiles_per_seq) == 0, 4 * BLK, 0)
        last = jnp.where((i % tiles_per_seq) == tiles_per_seq - 1, 4 * BLK, 0)
        head0, band, prev = _att_masks()
        band_first = band - prev * first
        band_tail = band[:, 0:2 * BLK] - last
        for p in range(4):
            _fill_block_diagonal(kbd, p, k_ref, kh_ref, head0, nq)
            _fill_block_diagonal(vbd, p, v_ref, vh_ref, head0, nq)
        units = [(p, n) for p in range(4) for n in range(nq + 1)]

        def stage_a(p, n):
            sl = slice(p * LANES, (p + 1) * LANES)
            base = (p * (nq + 1) + n) * 2 * BLK
            if n < nq:
                rows = slice(n * BLK, (n + 1) * BLK)
                q2, do2, lse2, dl2 = q_ref[rows, sl], do_ref[rows, sl], lse_ref[rows, sl], dl_ref[rows, sl]
                keys = slice(base, base + 4 * BLK)
            else:
                q2, do2, lse2, dl2 = qn_ref[:, sl], don_ref[:, sl], lsen_ref[:, sl], dln_ref[:, sl]
                keys = slice(base, base + 2 * BLK)
            s = _dot_nt(q2 * 0.125, kbd[keys, :])
            dp = _dot_nt(do2, vbd[keys, :])
            return q2, do2, lse2, dl2, keys, s, dp

        def stage_b(n, lse2, dl2, s, dp):
            mask = band_tail if n == nq else (band_first if n == 0 else band)
            ps, dss = [], []
            for g in range(s.shape[1] // BLK):
                h = g % 2
                cols = slice(g * BLK, (g + 1) * BLK)
                pg = jnp.exp(jnp.where(mask[:, cols] >= 0, s[:, cols] - lse2[:, h * 64:h * 64 + 1], NEG))
                ps.append(pg.astype(BF16))
                dss.append((pg * (dp[:, cols] - dl2[:, h * 64:h * 64 + 1]) * 0.125).astype(BF16))
            return jnp.concatenate(ps, axis=1), jnp.concatenate(dss, axis=1)

        def heads(r, g):
            return jnp.where(head0, r[g * BLK:(g + 1) * BLK, :], r[(g + 1) * BLK:(g + 2) * BLK, :])

        a_next = stage_a(*units[0])
        carry = None
        for ui, (p, n) in enumerate(units):
            q2, do2, lse2, dl2, keys, s, dp = a_next
            if ui + 1 < len(units):
                a_next = stage_a(*units[ui + 1])
            pmat, dsmat = stage_b(n, lse2, dl2, s, dp)
            sl = slice(p * LANES, (p + 1) * LANES)
            if n < nq:
                dqkv_ref[0, n * BLK:(n + 1) * BLK, sl] = _dot(dsmat, kbd[keys, :])
            dkbd = _dot_tn(dsmat, q2)
            dvbd = _dot_tn(pmat, do2)
            if n > 0:
                prow = slice((n - 1) * BLK, n * BLK)
                dqkv_ref[1, prow, sl] = carry[0] + heads(dkbd, 0)
                dqkv_ref[2, prow, sl] = carry[1] + heads(dvbd, 0)
            if n < nq:
                carry = (heads(dkbd, 2), heads(dvbd, 2))

    def which(w):
        return pl.BlockSpec((None, qt, ATT_W), lambda i: (w, i, 0))

    def prev(w):
        return pl.BlockSpec((None, BLK, ATT_W), lambda i: (w, jnp.maximum(i * nq - 1, 0), 0))

    tile = pl.BlockSpec((qt, ATT_W), lambda i: (i, 0))
    nxt = pl.BlockSpec((BLK, ATT_W), lambda i: (jnp.minimum((i + 1) * nq, nblk - 1), 0))
    nxt_q = pl.BlockSpec((None, BLK, ATT_W), lambda i: (0, jnp.minimum((i + 1) * nq, nblk - 1), 0))
    return pl.pallas_call(
        body, name=name, grid=(s_len // qt,),
        in_specs=[which(0), which(1), which(2), prev(1), prev(2), tile, tile, tile, nxt_q, nxt, nxt, nxt],
        out_specs=pl.BlockSpec((3, qt, ATT_W), lambda i: (0, i, 0)),
        out_shape=jax.ShapeDtypeStruct((3, s_len, ATT_W), F32),
        scratch_shapes=[pltpu.VMEM((4 * (nq + 1) * 2 * BLK, LANES), BF16)] * 2,
        compiler_params=_params(("parallel",)),
    )(qkv, qkv, qkv, qkv, qkv, d_att, lse, delta, qkv, d_att, lse, delta)


def _merge_and_head(o_g, lse_g, p_gate, y_conv, x, tgt, modv, final_g, w_ao, w_o, ts):
    s_len, dm = x.shape
    gw = ATT_W + 2 * dm
    nt = s_len // ts
    gate_off = 3 * dm + 9 * ATT_W
    assert gate_off % gw == 0

    def body(o0, o1, o2, l0, l1, l2, pg_ref, yc_ref, x_ref, t_ref, mod_ref, fg_ref, wao_ref, wo_ref,
             loss_ref, dx_ref, dyc_ref, dpg_ref, da0, da1, da2, ls0, ls1, ls2, de0, de1, de2,
             gwo_ref, gwao_ref, gs_ref, slab, ones_scr):
        i = pl.program_id(0)

        @pl.when(i == 0)
        def _():
            loss_ref[...] = jnp.zeros_like(loss_ref)
            gwo_ref[...] = jnp.zeros_like(gwo_ref)
            gwao_ref[...] = jnp.zeros_like(gwao_ref)
            gs_ref[...] = jnp.zeros_like(gs_ref)
            ri = lax.broadcasted_iota(jnp.int32, (ATT_W, ATT_W), 0) // 64
            ci = lax.broadcasted_iota(jnp.int32, (ATT_W, ATT_W), 1) // 64
            ones_scr[...] = jnp.where(ri == ci, 1.0, 0.0).astype(BF16)

        os_, ls_ = [], []
        for dil, o_ref, l_ref in zip(DILATIONS, (o0, o1, o2), (l0, l1, l2)):
            os_.append(jnp.concatenate(_interleave_load(o_ref, slab, dil, ts), axis=1))
            ls_.append(jnp.concatenate(_interleave_load(l_ref, slab, dil, ts), axis=1))
        mx = jnp.maximum(jnp.maximum(ls_[0], ls_[1]), ls_[2])
        wts = [jnp.exp(l - mx) for l in ls_]
        wsum = wts[0] + wts[1] + wts[2]
        att = (wts[0] * os_[0] + wts[1] * os_[1] + wts[2] * os_[2]) / wsum
        lse_all = mx + jnp.log(wsum)

        z_att, g_conv, g_att = pg_ref[:, 0:ATT_W], pg_ref[:, ATT_W:ATT_W + dm], pg_ref[:, ATT_W + dm:gw]
        sgz = _sig(z_att)
        sz = z_att * sgz
        a_att = (att * sz).astype(BF16)
        y_att = _dot(a_att, wao_ref[...])
        y_conv = yc_ref[...]
        sgc, sga = _sig(g_conv), _sig(g_att)
        merged = (sgc * y_conv + sga * y_att).astype(BF16)
        mo = _dot(merged, wo_ref[...])
        gate = mod_ref[2:3, :]
        x2 = x_ref[...] + gate * mo
        r = lax.rsqrt(jnp.mean(x2 * x2, axis=-1, keepdims=True) + EPS)
        xr = x2 * r
        err = xr * fg_ref[...] - t_ref[...]
        loss_ref[...] += 0.5 * jnp.sum(jnp.mean(err * err, axis=-1, keepdims=True))
        dy = err * (1.0 / dm)
        gs_ref[0:1, :] += jnp.sum(dy * xr, axis=0, keepdims=True)
        dyg = dy * fg_ref[...]
        d_x2 = r * dyg - xr * (r * jnp.mean(dyg * xr, axis=-1, keepdims=True))
        dx_ref[...] = d_x2
        gs_ref[1:2, :] += jnp.sum(d_x2 * mo, axis=0, keepdims=True)
        d_mo = (d_x2 * gate).astype(BF16)
        d_mg = _dot_nt(d_mo, wo_ref[...])
        gwo_ref[...] += _dot_tn(merged, d_mo)
        dyc_ref[...] = (d_mg * sgc).astype(BF16)
        dpg_ref[:, ATT_W:ATT_W + dm] = (d_mg * y_conv * sgc * (1.0 - sgc)).astype(BF16)
        d_ya = (d_mg * sga).astype(BF16)
        dpg_ref[:, ATT_W + dm:gw] = (d_mg * y_att * sga * (1.0 - sga)).astype(BF16)
        gwao_ref[...] += _dot_tn(a_att, d_ya)
        d_aa = _dot_nt(d_ya, wao_ref[...])
        dpg_ref[:, 0:ATT_W] = (d_aa * att * _dsilu(z_att, sgz)).astype(BF16)
        d_att = d_aa * sz
        ones = ones_scr[...]
        prod = d_att * att
        hi = prod.astype(BF16)
        lo = (prod - hi.astype(F32)).astype(BF16)
        delta = _dot(hi, ones) + _dot(lo, ones)
        for val, refs, dt in ((d_att, (da0, da1, da2), BF16), (lse_all, (ls0, ls1, ls2), F32),
                              (delta, (de0, de1, de2), F32)):
            vals = [val[:, s * LANES:(s + 1) * LANES] for s in range(4)]
            for dil, ref in zip(DILATIONS, refs):
                _deinterleave_store(vals, slab, ref, dil, ts, dt)

    def grp(dil):
        return pl.BlockSpec((dil, ts // dil, ATT_W), lambda i: (0, i, 0))

    tile = pl.BlockSpec((ts, dm), lambda i: (i, 0))
    gate_tile = pl.BlockSpec((ts, gw), lambda i: (i, 0))
    const = lambda shp: pl.BlockSpec(shp, lambda i: tuple(0 for _ in shp))
    grp_shape = lambda dt: [jax.ShapeDtypeStruct((dil, s_len // dil, ATT_W), dt) for dil in DILATIONS]
    return pl.pallas_call(
        body, name="merge_and_head", grid=(nt,),
        in_specs=[grp(d) for d in DILATIONS] * 2
        + [gate_tile, tile, tile, tile, const((8, dm)), const((1, dm)), const((ATT_W, dm)), const((dm, dm))],
        out_specs=[const((8, LANES)), tile, tile, pl.BlockSpec((ts, gw), lambda i: (i, gate_off // gw))]
        + [grp(d) for d in DILATIONS] * 3
        + [const((dm, dm)), const((ATT_W, dm)), const((8, dm))],
        out_shape=[jax.ShapeDtypeStruct((8, LANES), F32), jax.ShapeDtypeStruct((s_len, dm), F32),
                   jax.ShapeDtypeStruct((s_len, dm), BF16), jax.ShapeDtypeStruct((s_len, gate_off + gw), BF16)]
        + grp_shape(BF16) + grp_shape(F32) + grp_shape(F32)
        + [jax.ShapeDtypeStruct((dm, dm), F32), jax.ShapeDtypeStruct((ATT_W, dm), F32),
           jax.ShapeDtypeStruct((8, dm), F32)],
        scratch_shapes=[pltpu.VMEM((8, ts, LANES), F32), pltpu.VMEM((ATT_W, ATT_W), BF16)],
        compiler_params=_params(("arbitrary",)),
    )(*o_g, *lse_g, p_gate, y_conv, x, tgt, modv, final_g, w_ao, w_o)


def _qkv_grad_to_tokens(dp, dqkv_g, rope, ts):
    s_len, dm = rope.shape[1], (dp.shape[1] - 10 * ATT_W) // 5
    qw = 3 * ATT_W
    assert (3 * dm) % qw == 0

    def body(dp_in, g0, g1, g2, rope_ref, o_ref, slab):
        del dp_in
        w = pl.program_id(1)
        rc, ra, rb = rope_ref.at[0], rope_ref.at[1], rope_ref.at[2]

        def emit(roped):
            for gi, (dil, g_ref) in enumerate(zip(DILATIONS, (g0, g1, g2))):
                vals = _interleave_load(g_ref, slab, dil, ts)
                for s in range(4):
                    t = vals[s]
                    if roped:
                        t = t * rc[...] + pltpu.roll(t * ra[...], 8, 1) + pltpu.roll(t * rb[...], LANES - 8, 1)
                    col = gi * ATT_W + s * LANES
                    o_ref[:, col:col + LANES] = t.astype(BF16)

        pl.when(w < 2)(lambda: emit(True))
        pl.when(w == 2)(lambda: emit(False))

    return pl.pallas_call(
        body, name="qkv_grad_to_tokens", grid=(s_len // ts, 3),
        in_specs=[ANY] + [pl.BlockSpec((None, dil, ts // dil, ATT_W), lambda i, w: (w, 0, i, 0)) for dil in DILATIONS]
        + [pl.BlockSpec((3, ts, LANES), lambda i, w: (0, i, 0))],
        out_specs=pl.BlockSpec((ts, qw), lambda i, w: (i, 3 * dm // qw + w)),
        out_shape=jax.ShapeDtypeStruct(dp.shape, BF16),
        input_output_aliases={0: 0},
        scratch_shapes=[pltpu.VMEM((8, ts, LANES), F32)],
        compiler_params=_params(("parallel", "arbitrary")),
    )(dp, *dqkv_g, rope)


def _wide_col_tile(cols):
    for width in (5 * COL_TILE, 2 * COL_TILE):
        if cols % width == 0:
            return width
    return COL_TILE


def _input_grad(dp, w_in, x, dx_res, modv, norm_g, ts):
    s_len, dm = x.shape
    ct = _wide_col_tile(dp.shape[1])
    nct = dp.shape[1] // ct

    def body(p_ref, w_ref, x_ref, dxr_ref, mod_ref, g_ref, gx_ref, gs_ref, acc):
        i, j = pl.program_id(0), pl.program_id(1)

        @pl.when((i == 0) & (j == 0))
        def _():
            gs_ref[...] = jnp.zeros_like(gs_ref)

        @pl.when(j == 0)
        def _():
            acc[...] = jnp.zeros_like(acc)

        acc[...] += _dot_nt(p_ref[...], w_ref[...])

        @pl.when(j == nct - 1)
        def _():
            d_h = acc[...]
            xv = x_ref[...]
            r = lax.rsqrt(jnp.mean(xv * xv, axis=-1, keepdims=True) + EPS)
            xr = xv * r
            gs_ref[0:1, :] += jnp.sum(d_h, axis=0, keepdims=True)
            gs_ref[1:2, :] += jnp.sum(d_h * (xr * g_ref[...]), axis=0, keepdims=True)
            d_n = d_h * (1.0 + mod_ref[1:2, :])
            gs_ref[2:3, :] += jnp.sum(d_n * xr, axis=0, keepdims=True)
            dxn = d_n * g_ref[...]
            gx_ref[...] = dxr_ref[...] + r * dxn - xr * (r * jnp.mean(dxn * xr, axis=-1, keepdims=True))

    tile = pl.BlockSpec((ts, dm), lambda i, j: (i, 0))
    return pl.pallas_call(
        body, name="input_grad", grid=(s_len // ts, nct),
        in_specs=[pl.BlockSpec((ts, ct), lambda i, j: (i, j)), pl.BlockSpec((dm, ct), lambda i, j: (0, j)), tile, tile,
                  pl.BlockSpec((8, dm), lambda i, j: (0, 0)), pl.BlockSpec((1, dm), lambda i, j: (0, 0))],
        out_specs=[tile, pl.BlockSpec((8, dm), lambda i, j: (0, 0))],
        out_shape=[jax.ShapeDtypeStruct((s_len, dm), F32), jax.ShapeDtypeStruct((8, dm), F32)],
        scratch_shapes=[pltpu.VMEM((ts, dm), F32)],
        compiler_params=_params(("arbitrary", "arbitrary")),
    )(dp, w_in, x, dx_res, modv, norm_g)


def _w_in_grad(h_t, dp, ts, tiles, name, prev=None, after=()):
    dm, s_len = h_t.shape
    ct = _wide_col_tile(dp.shape[1])
    lo, hi = tiles
    extra = ([prev] if prev is not None else []) + list(after)

    def body(h_ref, p_ref, *rest):
        o_ref = rest[-1]

        @pl.when(pl.program_id(1) == 0)
        def _():
            o_ref[...] = jnp.zeros_like(o_ref)

        o_ref[...] += _dot(h_ref[...], p_ref[...])

    return pl.pallas_call(
        body, name=name, grid=(hi - lo, s_len // ts),
        in_specs=[pl.BlockSpec((dm, ts), lambda j, i: (0, i)), pl.BlockSpec((ts, ct), lambda j, i: (i, lo + j))]
        + [ANY] * len(extra),
        out_specs=pl.BlockSpec((dm, ct), lambda j, i: (0, lo + j)),
        out_shape=jax.ShapeDtypeStruct((dm, dp.shape[1]), F32),
        input_output_aliases={2: 0} if prev is not None else {},
        compiler_params=_params(("arbitrary", "arbitrary")),
    )(h_t, dp, *extra)


def _rope_lane_table():
    l64 = jnp.arange(LANES) % 64
    half = ROT_DIM // 2
    inv_freq = ROPE_THETA ** (-(jnp.arange(half, dtype=F32) * 2.0 / ROT_DIM))
    rot = l64 < ROT_DIM
    rows = [jnp.where(rot, inv_freq[l64 % half], 0.0), (l64 < half).astype(F32),
            ((l64 >= half) & rot).astype(F32), rot.astype(F32)]
    return jnp.concatenate([jnp.stack(rows), jnp.zeros((4, LANES), F32)], axis=0)


def _tile_sizes(s_len):
    ts_big = min(1024, s_len // 2)
    ts_mid = 256
    ts_head = 256
    qt = [min(1024, s_len // dil) for dil in DILATIONS]
    return ts_big, ts_mid, ts_head, qt


def kernel(x, c, positions, norm_g, w_ada, b_ada, w_in, conv_w, conv_b, conv_ln_g, conv_ln_b, w_conv_out, w_att_out, w_o, final_g, loss_target, m_norm_g, m_w_ada, m_b_ada, m_w_in, m_conv_w, m_conv_b, m_conv_ln_g, m_conv_ln_b, m_w_conv_out, m_w_att_out, m_w_o, m_final_g, v_norm_g, v_w_ada, v_b_ada, v_w_in, v_conv_w, v_conv_b, v_conv_ln_g, v_conv_ln_b, v_w_conv_out, v_w_att_out, v_w_o, v_final_g):
    s_len, dm = x.shape[1], x.shape[2]
    ts_big, ts_mid, ts_head, qt = _tile_sizes(s_len)
    xi, yi, cidx = _my_place()
    chip = 2 * xi + yi
    batch = 4 * xi + 2 * yi + cidx
    x2d, tgt = x[0], loss_target[0]
    pos = positions.reshape(s_len, 1)
    wa_l, wi_l, cw_l = w_ada[0], w_in[0], conv_w[0]
    wco_l, wao_l, wo_l = w_conv_out[0], w_att_out[0], w_o[0]
    ada_w = wa_l.shape[1]
    cw_cols = cw_l.shape[1]

    kinds = ("col", "row", "col", "row")
    w_bufs = _cast_weights([wi_l, wco_l, wao_l, wo_l], kinds)

    cw_pad = jnp.pad(cw_l, ((0, HALO - CONV_K), (0, 0)))
    small_in = jnp.concatenate([jnp.broadcast_to(c, (8, dm)), cw_pad.reshape(8, dm)], axis=0)
    small = _allgather_small(small_in).reshape(8, 16, dm)
    c_all = small[:, 0, :]
    conv_w_full = jnp.concatenate(
        [small[2 * p, 8:16, :].reshape(HALO, cw_cols) for p in range(4)], axis=1)
    b_l = lax.dynamic_slice(b_ada, (0, chip * ada_w), (1, ada_w))
    mod_parts = _allgather_small(_mod_part(c_all, wa_l, b_l)).reshape(8, 8, ada_w)
    mod_rows = lax.dynamic_index_in_dim(mod_parts, batch, axis=1, keepdims=False)
    mod = jnp.concatenate([mod_rows[2 * p] for p in range(4)], axis=0).reshape(3, dm)
    modv = jnp.concatenate([mod, jnp.zeros((5, dm), F32)], axis=0)

    sems_a, sems_b, w_bufs, token = _gather_weights_start(w_bufs, kinds, modv)
    rope = _rope_coefficients(pos, _rope_lane_table() + token, ts_big)
    h_b, h_t = _norm_modulate(x2d, modv + token[0, 0], norm_g, ts_big)
    (w_in_b,) = _gather_weights_wait(w_bufs, kinds, sems_a, 0, (rope, h_b), "gather_weights_wait_w_in")
    w_in_b = _pass_to_sibling(w_in_b, kinds[0])

    n_conv, n_gate = 3 * dm // COL_TILE, (ATT_W + 2 * dm) // COL_TILE
    ts_proj = ts_big
    p_conv = _plain_projection(h_b, w_in_b, 0, n_conv, ts_proj, "conv_projection")
    p_gate = _plain_projection(h_b, w_in_b, n_conv + 9, n_gate, ts_proj, "gate_projection")
    qkv_g = [_qkv_projection(h_b, rope, w_in_b, gi, min(2 * ts_big, s_len)) for gi in range(3)]
    w_co_b, w_ao_b, w_o_b = _gather_weights_wait(w_bufs, kinds, sems_b, 1, (qkv_g[2], p_conv, p_gate),
                                                  "gather_weights_wait_rest")
    y_conv, u0, xh, rstd = _conv_forward(p_conv, conv_w_full, conv_b, conv_ln_g, conv_ln_b, w_co_b, 2 * ts_mid)
    qkv_flat = [q.reshape(3, s_len, ATT_W) for q in qkv_g]
    o_g, lse_g = [], []
    for gi, dil in enumerate(DILATIONS):
        o, l = _attention_forward(qkv_flat[gi], s_len // dil, qt[gi], "attention_forward_%d" % dil)
        o_g.append(o.reshape(dil, s_len // dil, ATT_W))
        lse_g.append(l.reshape(dil, s_len // dil, ATT_W))

    (loss_p, dx_res, dyc, dp, da0, da1, da2, ls0, ls1, ls2, de0, de1, de2,
     g_wo, g_wao, head_sums) = _merge_and_head(o_g, lse_g, p_gate, y_conv, x2d, tgt, modv,
                                               final_g.reshape(1, dm), w_ao_b, w_o_b, ts_head)

    dp, g_wco, conv_sums, g_cw = _conv_backward(dp, dyc, p_conv, u0, xh, rstd, conv_w_full, conv_ln_g, conv_ln_b,
                                                w_co_b, 2 * ts_mid)
    dqkv_g = []
    for gi, (dil, da, ls, de) in enumerate(zip(DILATIONS, (da0, da1, da2), (ls0, ls1, ls2), (de0, de1, de2))):
        flat = lambda a: a.reshape(s_len, ATT_W)
        dqkv = _attention_backward(qkv_flat[gi], flat(da), flat(ls), flat(de), s_len // dil, qt[gi],
                                   "attention_backward_%d" % dil)
        dqkv_g.append(dqkv.reshape(3, dil, s_len // dil, ATT_W))
    dp = _qkv_grad_to_tokens(dp, dqkv_g, rope, ts_big)
    ts_wg = min(2 * ts_big, s_len)
    n_wt = dp.shape[1] // _wide_col_tile(dp.shape[1])
    n_first = max(1, (3 * n_wt) // 4)
    split_col = n_first * _wide_col_tile(dp.shape[1])
    g_first = _w_in_grad(h_t, dp, ts_wg, (0, n_first), "w_in_grad_first")
    grads = [g_first, g_wco, g_wao, g_wo]
    full_shapes = [g.shape for g in grads]
    pas = [pltpu.with_memory_space_constraint(
        lax.empty((r // 2, c_) if kind == "col" else (r, c_ // 2), F32), pltpu.HBM)
        for (r, c_), kind in zip(full_shapes, kinds)]
    pe_s, pe_r, grads, pas, pe_token = _pair_exchange_start(grads, pas, kinds, (0, split_col))
    grads[0] = _w_in_grad(h_t, dp, ts_wg, (n_first, n_wt), "w_in_grad_second", prev=grads[0], after=(pe_token,))
    grads, recv_halves = _pair_exchange_wait(pe_s, pe_r, grads, pas, kinds, (0, split_col))
    recv_halves[0] = _pair_exchange_rest(grads[0], recv_halves[0], kinds[0], (split_col, dp.shape[1]))

    c_arr = jnp.reshape(cidx, (1,)).astype(jnp.int32)
    halves = [_reduce_pair_sum(g, pa, kind, c_arr, "reduce_pair_sum_%d" % k)
              for k, (g, pa, kind) in enumerate(zip(grads, recv_halves, kinds))]
    send_sems, recv_sems, halves, lands, token = _reduce_to_owner_start(halves, kinds, full_shapes)
    grad_x, in_sums = _input_grad(dp, w_in_b, x2d, dx_res, modv + token[0, 0], norm_g, ts_big)
    halves, recvd = _reduce_to_owner_wait(send_sems, recv_sems, halves, lands, in_sums, kinds)
    gr_win, gr_wco, gr_wao, gr_wo = _reduce_finish(halves, recvd, kinds, full_shapes)

    rows = [in_sums[2:3], conv_sums[2:3], conv_sums[0:1], conv_sums[1:2], head_sums[0:1],
            in_sums[0:1], in_sums[1:2], head_sums[1:2], g_cw, jnp.pad(loss_p, ((0, 0), (0, dm - LANES)))]
    part = jnp.concatenate(rows, axis=0)
    buf = lax.dynamic_update_slice(jnp.zeros((8 * 48, dm), F32), part, (batch * 48, 0))
    row_sems_s, row_sems_r, buf, row_token = _row_gather_start(buf, 48, gr_win)
    upd = {
        "w_in": _adamw(wi_l, gr_win, m_w_in[0], v_w_in[0], "adamw_w_in", (row_token,)),
        "w_co": _adamw(wco_l, gr_wco, m_w_conv_out[0], v_w_conv_out[0], "adamw_w_conv_out", (row_token,)),
        "w_ao": _adamw(wao_l, gr_wao, m_w_att_out[0], v_w_att_out[0], "adamw_w_att_out", (row_token,)),
        "w_o": _adamw(wo_l, gr_wo, m_w_o[0], v_w_o[0], "adamw_w_o", (row_token,)),
    }
    gathered = _row_gather_wait(row_sems_s, row_sems_r, buf, 48, [upd[k][0] for k in ("w_in", "w_co", "w_ao", "w_o")])
    tot = _sum_devices(gathered, 48)
    loss = tot[8 + HALO, 0]
    dmod_all = gathered.reshape(8, 48, dm)[:, 5:8, :].reshape(8, 3 * dm)
    dmod_l = lax.dynamic_slice(dmod_all, (0, chip * ada_w), (8, ada_w))
    gr_wada = _grad_w_ada(c_all.T, dmod_l)
    gr_cw = lax.dynamic_slice(tot[8:8 + HALO], (0, chip * cw_cols), (HALO, cw_cols))

    pad_cw = lambda a: jnp.pad(a[0], ((0, HALO - CONV_K), (0, 0)))
    row = lambda a: a.reshape(1, dm)
    small_upd = _adamw_small(
        [norm_g, conv_b, conv_ln_g, conv_ln_b, row(final_g), b_ada], tot[0:8],
        [m_norm_g, m_conv_b, m_conv_ln_g, m_conv_ln_b, row(m_final_g), m_b_ada],
        [v_norm_g, v_conv_b, v_conv_ln_g, v_conv_ln_b, row(v_final_g), v_b_ada])
    upd["w_ada"] = _adamw(wa_l, gr_wada, m_w_ada[0], v_w_ada[0], "adamw_w_ada")
    upd["conv_w"] = _adamw(cw_pad, gr_cw, pad_cw(m_conv_w), pad_cw(v_conv_w), "adamw_conv_w")

    def family(which):
        if which is None:
            sm = small_upd[0:6]
            big = {"w_ada": gr_wada, "w_in": gr_win, "conv_w": gr_cw, "w_co": gr_wco, "w_ao": gr_wao, "w_o": gr_wo}
        else:
            sm = small_upd[6 * (which + 1):6 * (which + 2)]
            big = {k: upd[k][which] for k in ("w_ada", "w_in", "conv_w", "w_co", "w_ao", "w_o")}
        return [sm[0], big["w_ada"][None], sm[5], big["w_in"][None],
                big["conv_w"][None, :CONV_K], sm[1], sm[2], sm[3], big["w_co"][None],
                big["w_ao"][None], big["w_o"][None], sm[4].reshape(dm)]

    return (loss, grad_x[None], *family(None), *family(0), *family(1), *family(2))
```

```python
import jax
import jax.numpy as jnp
from jax import lax
from jax.experimental import pallas as pl
from jax.experimental.pallas import tpu as pltpu

F32 = jnp.float32
BF16 = jnp.bfloat16
MESH = pl.DeviceIdType.MESH
ANY = pl.BlockSpec(memory_space=pl.ANY)
VM = pl.BlockSpec(memory_space=pltpu.VMEM)

EPS = 1e-6
NEG = -1e30
ATT_W = 512
DILATIONS = (1, 4, 16)
BLK = 128
CONV_K = 31
HALO = 32
CONV_ROWS_FWD = 32
CONV_ROWS_BWD = 16
ROT_DIM = 16
ROPE_THETA = 500000.0
COL_TILE = 512
LANES = 128
VMEM_LIMIT = 56 * 1024 * 1024

ADAM_LR, ADAM_B1, ADAM_B2, ADAM_EPS, ADAM_WD, ADAM_STEP = 0.001, 0.9, 0.999, 1e-08, 0.01, 10


def _params(sem=None, vmem=VMEM_LIMIT):
    return pltpu.CompilerParams(dimension_semantics=sem, vmem_limit_bytes=vmem)


def _dot(a, b):
    return jnp.dot(a, b, preferred_element_type=F32)


def _dot_nt(a, b):
    return lax.dot_general(a, b, (((1,), (1,)), ((), ())), preferred_element_type=F32)


def _dot_tn(a, b):
    return lax.dot_general(a, b, (((0,), (0,)), ((), ())), preferred_element_type=F32)


def _sig(x):
    return jax.nn.sigmoid(x)


def _dsilu(x, s):
    return s * (1.0 + x * (1.0 - s))


def _my_place():
    return lax.axis_index("x"), lax.axis_index("y"), lax.axis_index("c")


def _allgather_small(x_shard):
    m_per, n = x_shard.shape

    def body(x_ref, out_ref, send_sems, recv_sems, local_sem):
        x, y, c = _my_place()
        me, sibling = (x, y, c), (x, y, 1 - c)
        chips = [(1 - x, y), (x, 1 - y), (1 - x, 1 - y)]

        def rows(px, py, pc):
            return out_ref.at[pl.ds((4 * px + 2 * py + pc) * m_per, m_per), :]

        def copy(k, block, to, src=None):
            return pltpu.make_async_remote_copy(
                src_ref=rows(*block) if src is None else src, dst_ref=rows(*block),
                send_sem=send_sems.at[k], recv_sem=recv_sems.at[k],
                device_id=to, device_id_type=MESH)

        mine = pltpu.make_async_copy(x_ref, rows(*me), local_sem)
        mine.start()
        first = [copy(0, me, sibling, src=x_ref)]
        first += [copy(1 + j, me, (*chip, c), src=x_ref) for j, chip in enumerate(chips)]
        for cp in first:
            cp.start()
        passed = [copy(4 + j, (*chip, c), sibling) for j, chip in enumerate(chips)]
        for j, chip in enumerate(chips):
            copy(1 + j, (*chip, c), me).wait_recv()
            passed[j].start()
        copy(0, sibling, me).wait_recv()
        for j, chip in enumerate(chips):
            copy(4 + j, (*chip, 1 - c), me).wait_recv()
        for cp in first + passed:
            cp.wait_send()
        mine.wait()

    return pl.pallas_call(
        body, name="allgather_small",
        out_shape=jax.ShapeDtypeStruct((8 * m_per, n), x_shard.dtype),
        in_specs=[VM], out_specs=VM,
        scratch_shapes=[pltpu.SemaphoreType.DMA((7,)), pltpu.SemaphoreType.DMA((7,)),
                        pltpu.SemaphoreType.DMA],
    )(x_shard)


def _shard_window(ref, kind, p, n_shards=4):
    r, c = ref.shape
    if kind == "col":
        w = c // n_shards
        return ref.at[:, pl.ds(p * w, w)]
    w = r // n_shards
    return ref.at[pl.ds(p * w, w), :]


def _half_window(ref, kind, hc):
    r, c = ref.shape
    if kind == "col":
        return ref.at[pl.ds(hc * (r // 2), r // 2), :]
    return ref.at[:, pl.ds(hc * (c // 2), c // 2)]


def _landed(ref, kind, chip, hc):
    return _half_window(_shard_window(ref, kind, 2 * chip[0] + chip[1]), kind, hc)


def _cast_weights(shards, kinds):
    n = len(shards)
    full_shapes = [(s.shape[0], 4 * s.shape[1]) if kind == "col" else (4 * s.shape[0], s.shape[1])
                   for s, kind in zip(shards, kinds)]

    def body(*refs):
        w_refs, out_refs, bf_refs, sems = refs[:n], refs[n:2 * n], refs[2 * n:3 * n], refs[3 * n]
        x, y, _ = _my_place()
        cps = []
        for k in range(n):
            bf_refs[k][...] = w_refs[k][...].astype(BF16)
            cp = pltpu.make_async_copy(bf_refs[k], _shard_window(out_refs[k], kinds[k], 2 * x + y), sems.at[k])
            cp.start()
            cps.append(cp)
        for cp in cps:
            cp.wait()

    return pl.pallas_call(
        body, name="cast_weights",
        out_shape=[jax.ShapeDtypeStruct(s, BF16) for s in full_shapes],
        in_specs=[VM] * n, out_specs=[ANY] * n,
        scratch_shapes=[pltpu.VMEM(s.shape, BF16) for s in shards] + [pltpu.SemaphoreType.DMA((n,))],
        compiler_params=_params(),
    )(*shards)


def _gather_copies(refs, kinds, sems_a, sems_b):
    x, y, c = _my_place()
    chips = [(1 - x, y), (x, 1 - y), (1 - x, 1 - y)]
    set_a, set_b = [], []
    for j, chip in enumerate(chips):
        for k in range(len(refs)):
            if refs[k] is None:
                continue
            for flip in ((0,) if k == 0 else (0, 1)):
                win = _landed(refs[k], kinds[k], (x, y), c)
                sems, idx = (sems_a, j) if k == 0 else (sems_b, ((k - 1) * 3 + j) * 2 + flip)
                mine = _landed(refs[k], kinds[k], chip, (c + flip) % 2)
                (set_a if k == 0 else set_b).append((
                    pltpu.make_async_remote_copy(
                        src_ref=win, dst_ref=win, send_sem=sems[0].at[idx], recv_sem=sems[1].at[idx],
                        device_id=(*chip, (c + flip) % 2), device_id_type=MESH),
                    pltpu.make_async_remote_copy(
                        src_ref=mine, dst_ref=mine, send_sem=sems[0].at[idx], recv_sem=sems[1].at[idx],
                        device_id=(*chip, (c + flip) % 2), device_id_type=MESH)))
    return set_a, set_b


_SPLIT = dict(has_side_effects=pltpu.SideEffectType.DATAFLOW_SIDE_EFFECTING)


def _gather_weights_start(fulls, kinds, after):
    n = len(fulls)
    hbm = pl.BlockSpec(memory_space=pltpu.HBM)
    sem = pl.BlockSpec(memory_space=pltpu.SEMAPHORE)
    nb = (n - 1) * 6

    def body(*refs):
        in_refs = refs[:n]
        sa, ra, sb, rb = refs[n + 1:n + 5]
        token = refs[-1]
        set_a, set_b = _gather_copies(in_refs, kinds, (sa, ra), (sb, rb))
        for out_cp, _ in set_a + set_b:
            out_cp.start()
        token[...] = jnp.zeros_like(token)

    out = pl.pallas_call(
        body, name="gather_weights_start",
        out_shape=[pltpu.SemaphoreType.DMA((3,)), pltpu.SemaphoreType.DMA((3,)),
                   pltpu.SemaphoreType.DMA((nb,)), pltpu.SemaphoreType.DMA((nb,))]
        + [pltpu.HBM(f.shape, f.dtype) for f in fulls] + [jax.ShapeDtypeStruct((8, LANES), F32)],
        in_specs=[hbm] * n + [ANY], out_specs=[sem] * 4 + [hbm] * n + [VM],
        input_output_aliases={k: 4 + k for k in range(n)},
        compiler_params=pltpu.CompilerParams(**_SPLIT),
    )(*[pltpu.with_memory_space_constraint(f, pltpu.HBM) for f in fulls], after)
    return (out[0], out[1]), (out[2], out[3]), out[4:4 + n], out[-1]


def _gather_weights_wait(fulls, kinds, sems, which, after, name):
    n = len(fulls)
    hbm = pl.BlockSpec(memory_space=pltpu.HBM)
    sem = pl.BlockSpec(memory_space=pltpu.SEMAPHORE)
    keep = [0] if which == 0 else list(range(1, n))

    def body(*refs):
        m = len(keep)
        full_refs = [None] * n
        for pos_, k in enumerate(keep):
            full_refs[k] = refs[pos_]
        s_ref, r_ref = refs[m:m + 2]
        if which == 0:
            sets = _gather_copies([full_refs[0]], kinds[:1], (s_ref, r_ref), None)[0]
        else:
            sets = _gather_copies([None] + [full_refs[k] for k in keep], kinds, None, (s_ref, r_ref))[1]
        for out_cp, in_cp in sets:
            out_cp.wait_send()
            in_cp.wait_recv()

    out = pl.pallas_call(
        body, name=name,
        out_shape=[pltpu.HBM(fulls[k].shape, fulls[k].dtype) for k in keep],
        in_specs=[hbm] * len(keep) + [sem, sem] + [ANY] * len(after), out_specs=[hbm] * len(keep),
        input_output_aliases={i: i for i in range(len(keep))},
        compiler_params=pltpu.CompilerParams(**_SPLIT),
    )(*[fulls[k] for k in keep], sems[0], sems[1], *after)
    return list(out)


def _row_gather_copies(buf, m_per, send_sems, recv_sems):
    x, y, c = _my_place()
    pairs = []
    for idx in range(1, 8):
        dx, dy, dc = idx // 4, (idx // 2) % 2, idx % 2
        peer = ((x + dx) % 2, (y + dy) % 2, (c + dc) % 2)
        mine = buf.at[pl.ds((4 * x + 2 * y + c) * m_per, m_per), :]
        theirs = buf.at[pl.ds((4 * peer[0] + 2 * peer[1] + peer[2]) * m_per, m_per), :]
        pairs.append(tuple(
            pltpu.make_async_remote_copy(src_ref=w, dst_ref=w, send_sem=send_sems.at[idx - 1],
                                         recv_sem=recv_sems.at[idx - 1], device_id=peer, device_id_type=MESH)
            for w in (mine, theirs)))
    return pairs


def _row_gather_start(buf, m_per, after):
    hbm = pl.BlockSpec(memory_space=pltpu.HBM)
    sem = pl.BlockSpec(memory_space=pltpu.SEMAPHORE)

    def body(buf_ref, after_ref, send_sems, recv_sems, out_ref, token):
        del after_ref, out_ref
        for out_cp, _ in _row_gather_copies(buf_ref, m_per, send_sems, recv_sems):
            out_cp.start()
        token[...] = jnp.zeros_like(token)

    return pl.pallas_call(
        body, name="row_gather_start",
        out_shape=[pltpu.SemaphoreType.DMA((7,)), pltpu.SemaphoreType.DMA((7,)), pltpu.HBM(buf.shape, buf.dtype),
                   jax.ShapeDtypeStruct((8, LANES), F32)],
        in_specs=[hbm, ANY], out_specs=[sem, sem, hbm, VM], input_output_aliases={0: 2},
        compiler_params=pltpu.CompilerParams(**_SPLIT),
    )(pltpu.with_memory_space_constraint(buf, pltpu.HBM), after)


def _row_gather_wait(send_sems, recv_sems, buf, m_per, after):
    hbm = pl.BlockSpec(memory_space=pltpu.HBM)
    sem = pl.BlockSpec(memory_space=pltpu.SEMAPHORE)

    def body(buf_ref, send_s, recv_s, *rest):
        for out_cp, in_cp in _row_gather_copies(buf_ref, m_per, send_s, recv_s):
            out_cp.wait_send()
            in_cp.wait_recv()

    return pl.pallas_call(
        body, name="row_gather_wait", out_shape=pltpu.HBM(buf.shape, buf.dtype),
        in_specs=[hbm, sem, sem] + [ANY] * len(after), out_specs=hbm, input_output_aliases={0: 0},
        compiler_params=pltpu.CompilerParams(**_SPLIT),
    )(buf, send_sems, recv_sems, *after)


def _pass_to_sibling(w_full, kind):
    def body(w_in_ref, w_ref, send_sems, recv_sems):
        del w_in_ref
        x, y, c = _my_place()
        chips = [(1 - x, y), (x, 1 - y), (1 - x, 1 - y)]
        cps = []
        for j, chip in enumerate(chips):
            win = _landed(w_ref, kind, chip, c)
            cp = pltpu.make_async_remote_copy(
                src_ref=win, dst_ref=win, send_sem=send_sems.at[j], recv_sem=recv_sems.at[j],
                device_id=(x, y, 1 - c), device_id_type=MESH)
            cp.start()
            cps.append(cp)
        for j, chip in enumerate(chips):
            theirs = _landed(w_ref, kind, chip, 1 - c)
            pltpu.make_async_remote_copy(
                src_ref=theirs, dst_ref=theirs, send_sem=send_sems.at[j], recv_sem=recv_sems.at[j],
                device_id=(x, y, 1 - c), device_id_type=MESH).wait_recv()
        for cp in cps:
            cp.wait_send()

    return pl.pallas_call(
        body, name="pass_to_sibling",
        out_shape=jax.ShapeDtypeStruct(w_full.shape, w_full.dtype),
        in_specs=[ANY], out_specs=ANY, input_output_aliases={0: 0},
        scratch_shapes=[pltpu.SemaphoreType.DMA((3,)), pltpu.SemaphoreType.DMA((3,))],
    )(w_full)


def _pair_copies(g_refs, pa_refs, kinds, send_sems, recv_sems, cols, with_rest):
    x, y, c = _my_place()
    lo, hi = cols

    def copy(k, src, dst):
        return pltpu.make_async_remote_copy(src_ref=src, dst_ref=dst, send_sem=send_sems.at[k], recv_sem=recv_sems.at[k],
                                            device_id=(x, y, 1 - c), device_id_type=MESH)

    cps = [copy(0, _half_window(g_refs[0], kinds[0], 1 - c).at[:, pl.ds(lo, hi - lo)],
                pa_refs[0].at[:, pl.ds(lo, hi - lo)])]
    if with_rest:
        cps += [copy(k, _half_window(g_refs[k], kinds[k], 1 - c), pa_refs[k]) for k in range(1, len(g_refs))]
    return cps


def _pair_exchange_start(grads, pas, kinds, cols):
    n = len(grads)
    hbm = pl.BlockSpec(memory_space=pltpu.HBM)
    sem = pl.BlockSpec(memory_space=pltpu.SEMAPHORE)

    def body(*refs):
        g_refs, pa_refs = refs[:n], refs[n:2 * n]
        send_sems, recv_sems = refs[2 * n:2 * n + 2]
        for cp in _pair_copies(g_refs, pa_refs, kinds, send_sems, recv_sems, cols, True):
            cp.start()
        refs[-1][...] = jnp.zeros_like(refs[-1])

    out = pl.pallas_call(
        body, name="pair_exchange_start",
        out_shape=[pltpu.SemaphoreType.DMA((n,)), pltpu.SemaphoreType.DMA((n,))]
        + [pltpu.HBM(a.shape, a.dtype) for a in list(grads) + list(pas)] + [jax.ShapeDtypeStruct((8, LANES), F32)],
        in_specs=[hbm] * (2 * n), out_specs=[sem, sem] + [hbm] * (2 * n) + [VM],
        input_output_aliases={k: 2 + k for k in range(2 * n)},
        compiler_params=pltpu.CompilerParams(**_SPLIT),
    )(*[pltpu.with_memory_space_constraint(a, pltpu.HBM) for a in list(grads) + list(pas)])
    return out[0], out[1], list(out[2:2 + n]), list(out[2 + n:2 + 2 * n]), out[-1]


def _pair_exchange_wait(send_sems, recv_sems, grads, pas, kinds, cols):
    n = len(grads)
    hbm = pl.BlockSpec(memory_space=pltpu.HBM)
    sem = pl.BlockSpec(memory_space=pltpu.SEMAPHORE)

    def body(*refs):
        g_refs, pa_refs = refs[:n], refs[n:2 * n]
        send_s, recv_s = refs[2 * n:2 * n + 2]
        for cp in _pair_copies(g_refs, pa_refs, kinds, send_s, recv_s, cols, True):
            cp.wait_send()
            cp.wait_recv()

    out = pl.pallas_call(
        body, name="pair_exchange_wait",
        out_shape=[pltpu.HBM(a.shape, a.dtype) for a in list(grads) + list(pas)],
        in_specs=[hbm] * (2 * n) + [sem, sem], out_specs=[hbm] * (2 * n),
        input_output_aliases={k: k for k in range(2 * n)},
        compiler_params=pltpu.CompilerParams(**_SPLIT),
    )(*grads, *pas, send_sems, recv_sems)
    return list(out[:n]), list(out[n:])


def _pair_exchange_rest(g, pa, kind, cols):
    def body(g_ref, pa_in, pa_ref, send_sems, recv_sems):
        del pa_in
        (cp,) = _pair_copies([g_ref], [pa_ref], [kind], send_sems, recv_sems, cols, False)
        cp.start()
        cp.wait()

    return pl.pallas_call(
        body, name="pair_exchange_rest", out_shape=jax.ShapeDtypeStruct(pa.shape, pa.dtype),
        in_specs=[ANY, ANY], out_specs=ANY, input_output_aliases={1: 0},
        scratch_shapes=[pltpu.SemaphoreType.DMA((1,)), pltpu.SemaphoreType.DMA((1,))],
    )(g, pa)


def _row_tile(rows, cols, itemsize=4, target=2 * 1024 * 1024, mult=16):
    t = rows
    while t % 2 == 0 and t // 2 >= mult and (t // 2) % mult == 0 and t * cols * itemsize > target:
        t //= 2
    return t


def _reduce_pair_sum(g, pa, kind, c_arr, name):
    hr, hc_ = pa.shape
    tr = _row_tile(hr, hc_)
    nb = hr // tr

    def body(c_ref, g_ref, pa_ref, o_ref):
        o_ref[...] = (g_ref[...] + pa_ref[...]).astype(BF16)

    if kind == "col":
        g_map = lambda i, c_ref: (c_ref[0] * nb + i, 0)
    else:
        g_map = lambda i, c_ref: (i, c_ref[0])
    return pl.pallas_call(
        body, name=name,
        grid_spec=pltpu.PrefetchScalarGridSpec(
            num_scalar_prefetch=1, grid=(nb,),
            in_specs=[pl.BlockSpec((tr, hc_), g_map), pl.BlockSpec((tr, hc_), lambda i, c_ref: (i, 0))],
            out_specs=pl.BlockSpec((tr, hc_), lambda i, c_ref: (i, 0))),
        out_shape=jax.ShapeDtypeStruct((hr, hc_), BF16),
        compiler_params=_params(("parallel",)),
    )(c_arr, g, pa)


def _half_shard_shape(full_shape, kind):
    r, c = full_shape
    return (r // 2, c // 4) if kind == "col" else (r // 4, c // 2)


def _to_owner_copies(h_refs, land_refs, send_sems, recv_sems, kinds):
    n = len(h_refs)
    x, y, c = _my_place()
    chips = [(1 - x, y), (x, 1 - y), (1 - x, 1 - y)]
    cps = []
    for j, chip in enumerate(chips):
        pj = 2 * chip[0] + chip[1]
        for k in range(n):
            cps.append(pltpu.make_async_remote_copy(
                src_ref=_shard_window(h_refs[k], kinds[k], pj), dst_ref=land_refs[k].at[j],
                send_sem=send_sems.at[j * n + k], recv_sem=recv_sems.at[j * n + k],
                device_id=(*chip, c), device_id_type=MESH))
    return cps


def _reduce_to_owner_start(halves, kinds, full_shapes):
    n = len(halves)
    hs = [_half_shard_shape(fs, kind) for fs, kind in zip(full_shapes, kinds)]
    hbm = pl.BlockSpec(memory_space=pltpu.HBM)
    sem = pl.BlockSpec(memory_space=pltpu.SEMAPHORE)

    def body(*refs):
        h_refs, land_refs = refs[:n], refs[n:2 * n]
        send_sems, recv_sems = refs[2 * n:2 * n + 2]
        token = refs[-1]
        for cp in _to_owner_copies(h_refs, land_refs, send_sems, recv_sems, kinds):
            cp.start()
        token[...] = jnp.zeros_like(token)

    lands = [pltpu.with_memory_space_constraint(lax.empty((3,) + s, BF16), pltpu.HBM) for s in hs]
    out = pl.pallas_call(
        body, name="reduce_to_owner_start",
        out_shape=[pltpu.SemaphoreType.DMA((3 * n,)), pltpu.SemaphoreType.DMA((3 * n,))]
        + [pltpu.HBM(h.shape, h.dtype) for h in halves] + [pltpu.HBM((3,) + s, BF16) for s in hs]
        + [jax.ShapeDtypeStruct((8, LANES), F32)],
        in_specs=[hbm] * (2 * n), out_specs=[sem, sem] + [hbm] * (2 * n) + [VM],
        input_output_aliases={k: 2 + k for k in range(2 * n)},
        compiler_params=pltpu.CompilerParams(has_side_effects=pltpu.SideEffectType.DATAFLOW_SIDE_EFFECTING),
    )(*[pltpu.with_memory_space_constraint(h, pltpu.HBM) for h in halves], *lands)
    return out[0], out[1], out[2:2 + n], out[2 + n:2 + 2 * n], out[-1]


def _reduce_to_owner_wait(send_sems, recv_sems, halves, lands, after, kinds):
    n = len(halves)
    hbm = pl.BlockSpec(memory_space=pltpu.HBM)
    sem = pl.BlockSpec(memory_space=pltpu.SEMAPHORE)

    def body(*refs):
        h_refs, land_refs = refs[:n], refs[n:2 * n]
        send_s, recv_s = refs[2 * n:2 * n + 2]
        for cp in _to_owner_copies(h_refs, land_refs, send_s, recv_s, kinds):
            cp.wait_send()
            cp.wait_recv()

    out = pl.pallas_call(
        body, name="reduce_to_owner_wait",
        out_shape=[pltpu.HBM(h.shape, h.dtype) for h in halves] + [pltpu.HBM(l.shape, l.dtype) for l in lands],
        in_specs=[hbm] * (2 * n) + [sem, sem, ANY], out_specs=[hbm] * (2 * n),
        input_output_aliases={k: k for k in range(2 * n)},
        compiler_params=pltpu.CompilerParams(has_side_effects=pltpu.SideEffectType.DATAFLOW_SIDE_EFFECTING),
    )(*halves, *lands, send_sems, recv_sems, after)
    return out[:n], out[n:]


def _reduce_finish(halves, recvd, kinds, full_shapes):
    n = len(halves)
    hs = [_half_shard_shape(fs, kind) for fs, kind in zip(full_shapes, kinds)]
    shard_shapes = [(fs[0], fs[1] // 4) if kind == "col" else (fs[0] // 4, fs[1])
                    for fs, kind in zip(full_shapes, kinds)]

    def body(*refs):
        h_refs, rc_refs, gs_refs = refs[:n], refs[n:2 * n], refs[2 * n:3 * n]
        own_refs, gh_refs = refs[3 * n:4 * n], refs[4 * n:5 * n]
        in_sems, loc_sems, send_sems, recv_sems = refs[5 * n:]
        x, y, c = _my_place()
        p = 2 * x + y
        loads = []
        for k in range(n):
            cp = pltpu.make_async_copy(_shard_window(h_refs[k], kinds[k], p), own_refs[k], in_sems.at[k])
            cp.start()
            loads.append(cp)
        outs = []
        for k in range(n):
            loads[k].wait()
            gh_refs[k][...] = (own_refs[k][...].astype(F32) + rc_refs[k][0].astype(F32)
                               + rc_refs[k][1].astype(F32) + rc_refs[k][2].astype(F32))
            dst = _half_window(gs_refs[k], kinds[k], c)
            lc = pltpu.make_async_copy(gh_refs[k], dst, loc_sems.at[k])
            lc.start()
            rc = pltpu.make_async_remote_copy(
                src_ref=gh_refs[k], dst_ref=dst, send_sem=send_sems.at[k], recv_sem=recv_sems.at[k],
                device_id=(x, y, 1 - c), device_id_type=MESH)
            rc.start()
            outs.append((lc, rc))
        for k, (lc, rc) in enumerate(outs):
            lc.wait()
            rc.wait_send()
            pltpu.make_async_remote_copy(
                src_ref=gh_refs[k], dst_ref=_half_window(gs_refs[k], kinds[k], 1 - c),
                send_sem=send_sems.at[k], recv_sem=recv_sems.at[k],
                device_id=(x, y, 1 - c), device_id_type=MESH).wait_recv()

    return pl.pallas_call(
        body, name="reduce_finish",
        out_shape=[jax.ShapeDtypeStruct(s, F32) for s in shard_shapes],
        in_specs=[ANY] * n + [VM] * n, out_specs=[ANY] * n,
        scratch_shapes=[pltpu.VMEM(s, BF16) for s in hs] + [pltpu.VMEM(s, F32) for s in hs]
        + [pltpu.SemaphoreType.DMA((n,)) for _ in range(4)],
        compiler_params=_params(),
    )(*halves, *recvd)


def _mod_part(c_all, w_ada_l, b_l):
    def body(c_ref, w_ref, b_ref, o_ref):
        o_ref[...] = _dot(c_ref[...].astype(BF16), w_ref[...].astype(BF16)) + b_ref[...]

    return pl.pallas_call(
        body, name="mod_part", out_shape=jax.ShapeDtypeStruct((8, w_ada_l.shape[1]), F32),
        in_specs=[VM, VM, VM], out_specs=VM, compiler_params=_params(),
    )(c_all, w_ada_l, b_l)


def _sum_devices(parts, m):
    def body(p_ref, o_ref):
        acc = p_ref[0:m, :]
        for d in range(1, 8):
            acc = acc + p_ref[d * m:(d + 1) * m, :]
        o_ref[...] = acc

    return pl.pallas_call(
        body, name="sum_devices", out_shape=jax.ShapeDtypeStruct((m, parts.shape[1]), F32),
        in_specs=[VM], out_specs=VM, compiler_params=_params(),
    )(parts)


def _grad_w_ada(c_all_t, dmod_l):
    d, w = c_all_t.shape[0], dmod_l.shape[1]

    def body(ct_ref, dm_ref, o_ref):
        acc = ct_ref[:, 0:1] * dm_ref[0:1, :]
        for b in range(1, 8):
            acc = acc + ct_ref[:, b:b + 1] * dm_ref[b:b + 1, :]
        o_ref[...] = acc

    return pl.pallas_call(
        body, name="grad_w_ada", out_shape=jax.ShapeDtypeStruct((d, w), F32),
        in_specs=[VM, VM], out_specs=VM, compiler_params=_params(),
    )(c_all_t, dmod_l)


def _adamw_math(w, g, m, v):
    nm = ADAM_B1 * m + (1.0 - ADAM_B1) * g
    nv = ADAM_B2 * v + (1.0 - ADAM_B2) * (g * g)
    m_hat = nm / (1.0 - ADAM_B1 ** ADAM_STEP)
    v_hat = nv / (1.0 - ADAM_B2 ** ADAM_STEP)
    return -ADAM_LR * (m_hat / (jnp.sqrt(v_hat) + ADAM_EPS) + ADAM_WD * w), nm, nv


def _adamw_small(ws, g8, ms, vs):
    shapes = [jax.ShapeDtypeStruct(w.shape, F32) for w in ws]

    def body(*refs):
        w_refs, g_ref, m_refs, v_refs, outs = refs[0:6], refs[6], refs[7:13], refs[13:19], refs[19:]
        for idx in range(6):
            if idx < 5:
                g = g_ref[idx:idx + 1, :]
            else:
                g = jnp.concatenate([g_ref[5:6, :], g_ref[6:7, :], g_ref[7:8, :]], axis=1)
            res = (g,) + _adamw_math(w_refs[idx][...], g, m_refs[idx][...], v_refs[idx][...])
            for fam in range(4):
                outs[6 * fam + idx][...] = res[fam]

    return pl.pallas_call(
        body, name="adamw_small", out_shape=shapes * 4,
        in_specs=[VM] * 19, out_specs=[VM] * 24, compiler_params=_params(),
    )(*ws, g8, *ms, *vs)


def _adamw(w, g, m, v, name, after=()):
    r, c = w.shape
    tr = _row_tile(r, c, target=1024 * 1024, mult=8)

    def body(w_ref, g_ref, m_ref, v_ref, *rest):
        d_ref, nm_ref, nv_ref = rest[len(after):]
        d_ref[...], nm_ref[...], nv_ref[...] = _adamw_math(w_ref[...], g_ref[...], m_ref[...], v_ref[...])

    spec = pl.BlockSpec((tr, c), lambda i: (i, 0))
    return pl.pallas_call(
        body, name=name, grid=(r // tr,),
        out_shape=[jax.ShapeDtypeStruct((r, c), F32)] * 3,
        in_specs=[spec] * 4 + [ANY] * len(after), out_specs=[spec] * 3,
        compiler_params=_params(("parallel",)),
    )(w, g, m, v, *after)


def _rope_coefficients(pos, ropetab, ts):
    s_len = pos.shape[0]

    def body(pos_ref, tab_ref, o_ref):
        ang = pos_ref[...].astype(F32) * tab_ref[0:1, :]
        cs, sn = jnp.cos(ang), jnp.sin(ang)
        o_ref[0] = jnp.where(tab_ref[3:4, :] > 0, cs, 1.0)
        o_ref[1] = -sn * tab_ref[1:2, :]
        o_ref[2] = sn * tab_ref[2:3, :]

    return pl.pallas_call(
        body, name="rope_coefficients", grid=(s_len // ts,),
        in_specs=[pl.BlockSpec((ts, 1), lambda i: (i, 0)), pl.BlockSpec((8, LANES), lambda i: (0, 0))],
        out_specs=pl.BlockSpec((3, ts, LANES), lambda i: (0, i, 0)),
        out_shape=jax.ShapeDtypeStruct((3, s_len, LANES), F32),
        compiler_params=_params(("parallel",)),
    )(pos, ropetab)


def _deinterleave_store(vals, slab, out_ref, d, ts, dtype):
    if d == 1:
        for s in range(4):
            out_ref[0, :, s * LANES:(s + 1) * LANES] = vals[s].astype(dtype)
        return
    for s in range(4):
        slab[s] = vals[s]
    if d == 16:
        q = ts // 4
        for s in range(4):
            for r1 in range(4):
                slab[4 + s, r1 * q:(r1 + 1) * q, :] = slab[s, pl.ds(r1, q, stride=4), :]
        for r in range(d):
            r1, r2 = r % 4, r // 4
            for s in range(4):
                out_ref[r, :, s * LANES:(s + 1) * LANES] = slab[
                    4 + s, pl.ds(r1 * q + r2, ts // d, stride=4), :].astype(dtype)
        return
    for r in range(d):
        for s in range(4):
            out_ref[r, :, s * LANES:(s + 1) * LANES] = slab[s, pl.ds(r, ts // d, stride=d), :].astype(dtype)


def _interleave_load(blk_ref, slab, d, ts):
    if d == 1:
        return [blk_ref[0, :, s * LANES:(s + 1) * LANES] for s in range(4)]
    if d == 16:
        q = ts // 4
        for r in range(d):
            r1, r2 = r % 4, r // 4
            for s in range(4):
                slab[4 + s, pl.ds(r1 * q + r2, ts // d, stride=4), :] = blk_ref[r, :, s * LANES:(s + 1) * LANES]
        for s in range(4):
            for r1 in range(4):
                slab[s, pl.ds(r1, q, stride=4), :] = slab[4 + s, r1 * q:(r1 + 1) * q, :]
        return [slab[s] for s in range(4)]
    for r in range(d):
        for s in range(4):
            slab[s, pl.ds(r, ts // d, stride=d), :] = blk_ref[r, :, s * LANES:(s + 1) * LANES]
    return [slab[s] for s in range(4)]


def _norm_modulate(x, modv, norm_g, ts):
    s_len, d_model = x.shape

    def body(x_ref, mod_ref, g_ref, h_ref, ht_ref):
        xv = x_ref[...]
        r = lax.rsqrt(jnp.mean(xv * xv, axis=-1, keepdims=True) + EPS)
        h = (xv * r) * g_ref[...] * (1.0 + mod_ref[1:2, :]) + mod_ref[0:1, :]
        h_ref[...] = h.astype(BF16)
        ht_ref[...] = h.T.astype(BF16)

    return pl.pallas_call(
        body, name="norm_modulate", grid=(s_len // ts,),
        in_specs=[pl.BlockSpec((ts, d_model), lambda i: (i, 0)), pl.BlockSpec((8, d_model), lambda i: (0, 0)),
                  pl.BlockSpec((1, d_model), lambda i: (0, 0))],
        out_specs=[pl.BlockSpec((ts, d_model), lambda i: (i, 0)), pl.BlockSpec((d_model, ts), lambda i: (0, i))],
        out_shape=[jax.ShapeDtypeStruct((s_len, d_model), BF16), jax.ShapeDtypeStruct((d_model, s_len), BF16)],
        compiler_params=_params(("parallel",)),
    )(x, modv, norm_g)


def _plain_projection(h, w_in, first_tile, n_tiles, ts, name):
    s_len, d_model = h.shape
    width = n_tiles * COL_TILE
    assert (first_tile * COL_TILE) % width == 0

    def body(h_ref, w_ref, o_ref):
        for t in range(n_tiles):
            cols = slice(t * COL_TILE, (t + 1) * COL_TILE)
            o_ref[:, cols] = _dot(h_ref[...], w_ref[:, cols])

    return pl.pallas_call(
        body, name=name, grid=(s_len // ts,),
        in_specs=[pl.BlockSpec((ts, d_model), lambda i: (i, 0)),
                  pl.BlockSpec((d_model, width), lambda i: (0, first_tile * COL_TILE // width),
                               pipeline_mode=pl.Buffered(1))],
        out_specs=pl.BlockSpec((ts, width), lambda i: (i, 0)),
        out_shape=jax.ShapeDtypeStruct((s_len, width), F32),
        compiler_params=_params(("parallel",)),
    )(h, w_in)


def _qkv_projection(h, rope, w_in, gi, ts):
    s_len, d_model = h.shape
    dil = DILATIONS[gi]
    nc = 3 * d_model // COL_TILE
    hr = ts // 2
    assert COL_TILE == ATT_W and hr % (16 * dil) == 0

    def body(h_ref, rope_ref, wq_ref, wk_ref, wv_ref, o_ref, slab):
        rc, ra, rb = rope_ref.at[0], rope_ref.at[1], rope_ref.at[2]
        w_refs = (wq_ref, wk_ref, wv_ref)
        units = [(w, half) for w in range(3) for half in range(2)]

        def matmul(w, half):
            return _dot(h_ref[half * hr:(half + 1) * hr, :], w_refs[w][...])

        def finish(w, half, res):
            rows = slice(half * hr, (half + 1) * hr)
            vals = []
            for s in range(4):
                t = res[:, s * LANES:(s + 1) * LANES]
                if w < 2:
                    t = (t * rc[rows, :] + pltpu.roll(t, LANES - 8, 1) * ra[rows, :]
                         + pltpu.roll(t, 8, 1) * rb[rows, :])
                vals.append(t)
            out = o_ref.at[w, :, half * (hr // dil):(half + 1) * (hr // dil), :]
            _deinterleave_store(vals, slab.at[half], out, dil, hr, BF16)

        res_next = matmul(*units[0])
        for ui, (w, half) in enumerate(units):
            res = res_next
            if ui + 1 < len(units):
                res_next = matmul(*units[ui + 1])
            finish(w, half, res)

    def w_spec(w):
        return pl.BlockSpec((d_model, COL_TILE), lambda i: (0, nc + 3 * w + gi))

    return pl.pallas_call(
        body, name="qkv_projection_%d" % dil, grid=(s_len // ts,),
        in_specs=[pl.BlockSpec((ts, d_model), lambda i: (i, 0)), pl.BlockSpec((3, ts, LANES), lambda i: (0, i, 0)),
                  w_spec(0), w_spec(1), w_spec(2)],
        out_specs=pl.BlockSpec((3, dil, ts // dil, ATT_W), lambda i: (0, 0, i, 0)),
        out_shape=jax.ShapeDtypeStruct((3, dil, s_len // dil, ATT_W), BF16),
        scratch_shapes=[pltpu.VMEM((2, 8, hr, LANES), F32)],
        compiler_params=_params(("parallel",)),
    )(h, rope, w_in, w_in, w_in)


def _layernorm_stats(u1):
    mu = jnp.mean(u1, axis=-1, keepdims=True)
    xc = u1 - mu
    rstd = lax.rsqrt(jnp.mean(xc * xc, axis=-1, keepdims=True) + EPS)
    return xc * rstd, rstd


def _shifted_copies(win, shf, ts):
    rows = ts + HALO - 8
    for b in range(1, 8):
        shf[b - 1, 0:rows, :] = win[pl.ds(b, rows), :]


def _tap(win, shf, off, r0, rows):
    a, b = divmod(off, 8)
    start = 8 * a + r0
    if b == 0:
        return win[start:start + rows, :]
    return shf[b - 1, start:start + rows, :]


def _conv_forward(p_conv, conv_w, conv_b, ln_g, ln_b, w_co, ts):
    s_len, d3 = p_conv.shape
    dm = d3 // 3

    def body(p_ref, cw_ref, cb_ref, g_ref, b_ref, w_ref, y_ref, u0_ref, xh_ref, rstd_ref, win, shf, u1_ref):
        i = pl.program_id(0)

        @pl.when(i == 0)
        def _():
            win[0:HALO, :] = jnp.zeros((HALO, dm), F32)

        a, b, z = p_ref[:, 0:dm], p_ref[:, dm:2 * dm], p_ref[:, 2 * dm:3 * dm]
        u0 = a * _sig(b)
        win[HALO:HALO + ts, :] = u0
        u0_ref[...] = u0
        _shifted_copies(win, shf, ts)
        for r0 in range(0, ts, CONV_ROWS_FWD):
            acc = jnp.broadcast_to(cb_ref[...], (CONV_ROWS_FWD, dm))
            for k in range(CONV_K):
                acc = acc + cw_ref[k:k + 1, :] * _tap(win, shf, HALO - (CONV_K - 1) + k, r0, CONV_ROWS_FWD)
            u1_ref[r0:r0 + CONV_ROWS_FWD, :] = acc
        xh, rstd = _layernorm_stats(u1_ref[...])
        xh_ref[...] = xh
        rstd_ref[...] = rstd
        u2 = xh * g_ref[...] + b_ref[...]
        a_conv = (u2 * _sig(u2)) * (z * _sig(z))
        y_ref[...] = _dot(a_conv.astype(BF16), w_ref[...])
        win[0:HALO, :] = win[ts:ts + HALO, :]

    row = pl.BlockSpec((1, dm), lambda i: (0, 0))
    tile = pl.BlockSpec((ts, dm), lambda i: (i, 0))
    return pl.pallas_call(
        body, name="conv_forward", grid=(s_len // ts,),
        in_specs=[pl.BlockSpec((ts, d3), lambda i: (i, 0)),
                  pl.BlockSpec((HALO, dm), lambda i: (0, 0)), row, row, row,
                  pl.BlockSpec((dm, dm), lambda i: (0, 0))],
        out_specs=[tile, tile, tile, pl.BlockSpec((ts, 1), lambda i: (i, 0))],
        out_shape=[jax.ShapeDtypeStruct((s_len, dm), F32)] * 3 + [jax.ShapeDtypeStruct((s_len, 1), F32)],
        scratch_shapes=[pltpu.VMEM((ts + HALO, dm), F32), pltpu.VMEM((7, ts + HALO - 8, dm), F32),
                        pltpu.VMEM((ts, dm), F32)],
        compiler_params=_params(("arbitrary",)),
    )(p_conv, conv_w, conv_b, ln_g, ln_b, w_co)


def _conv_backward(dp, dyc, p_conv, u0, xh, rstd, conv_w, ln_g, ln_b, w_co, ts):
    s_len, d3 = p_conv.shape
    dm = d3 // 3
    nt = s_len // ts
    hb = ts // HALO

    def body(dp_in, dy_ref, p_ref, u0_ref, uh_ref, xh_ref, rstd_ref, cw_ref, g_ref, b_ref, w_ref,
             dp_ref, gw_ref, gs_ref, gcw_ref, dwin, uwin, shf):
        del dp_in
        i = pl.program_id(0)
        ti = nt - 1 - i

        @pl.when(i == 0)
        def _():
            gw_ref[...] = jnp.zeros_like(gw_ref)
            gs_ref[...] = jnp.zeros_like(gs_ref)
            gcw_ref[...] = jnp.zeros_like(gcw_ref)
            dwin[ts:ts + HALO, :] = jnp.zeros((HALO, dm), F32)

        dy = dy_ref[...]
        z = p_ref[:, 2 * dm:3 * dm]
        d_ac = _dot_nt(dy, w_ref[...])
        xh, rstd = xh_ref[...], rstd_ref[...]
        u2 = xh * g_ref[...] + b_ref[...]
        sg2, sgz = _sig(u2), _sig(z)
        u3, sz = u2 * sg2, z * sgz
        gw_ref[...] += _dot_tn((u3 * sz).astype(BF16), dy)
        d_z = d_ac * u3 * _dsilu(z, sgz)
        d_u2 = d_ac * sz * _dsilu(u2, sg2)
        gs_ref[0:1, :] += jnp.sum(d_u2 * xh, axis=0, keepdims=True)
        gs_ref[1:2, :] += jnp.sum(d_u2, axis=0, keepdims=True)
        dxh = d_u2 * g_ref[...]
        d_u1 = rstd * (dxh - jnp.mean(dxh, axis=-1, keepdims=True)
                       - xh * jnp.mean(dxh * xh, axis=-1, keepdims=True))
        gs_ref[2:3, :] += jnp.sum(d_u1, axis=0, keepdims=True)
        dwin[0:ts, :] = d_u1
        uwin[0:HALO, :] = jnp.where(ti == 0, 0.0, uh_ref[...])
        uwin[HALO:HALO + ts, :] = u0_ref[...]
        dp_ref[:, 2 * dm:3 * dm] = d_z.astype(BF16)
        _shifted_copies(uwin, shf, ts)
        for k in range(CONV_K):
            part = jnp.zeros((CONV_ROWS_BWD, dm), F32)
            for r0 in range(0, ts, CONV_ROWS_BWD):
                part = part + dwin[r0:r0 + CONV_ROWS_BWD, :] * _tap(uwin, shf, HALO - (CONV_K - 1) + k, r0,
                                                                    CONV_ROWS_BWD)
            gcw_ref[k:k + 1, :] += jnp.sum(part, axis=0, keepdims=True)
        _shifted_copies(dwin, shf, ts)
        for r0 in range(0, ts, CONV_ROWS_BWD):
            d_u0 = jnp.zeros((CONV_ROWS_BWD, dm), F32)
            for k in range(CONV_K):
                d_u0 = d_u0 + cw_ref[k:k + 1, :] * _tap(dwin, shf, CONV_K - 1 - k, r0, CONV_ROWS_BWD)
            rows = slice(r0, r0 + CONV_ROWS_BWD)
            sgb = _sig(p_ref[rows, dm:2 * dm])
            dp_ref[rows, 0:dm] = (d_u0 * sgb).astype(BF16)
            dp_ref[rows, dm:2 * dm] = (d_u0 * p_ref[rows, 0:dm] * sgb * (1.0 - sgb)).astype(BF16)
        dwin[ts:ts + HALO, :] = dwin[0:HALO, :]

    rev = lambda i: (nt - 1 - i, 0)
    row = pl.BlockSpec((1, dm), lambda i: (0, 0))
    tile = pl.BlockSpec((ts, dm), rev)
    return pl.pallas_call(
        body, name="conv_backward", grid=(nt,),
        in_specs=[ANY, tile, pl.BlockSpec((ts, d3), rev), tile,
                  pl.BlockSpec((HALO, dm), lambda i: (jnp.maximum((nt - 1 - i) * hb - 1, 0), 0)),
                  tile, pl.BlockSpec((ts, 1), rev), pl.BlockSpec((HALO, dm), lambda i: (0, 0)), row, row,
                  pl.BlockSpec((dm, dm), lambda i: (0, 0))],
        out_specs=[pl.BlockSpec((ts, d3), rev), pl.BlockSpec((dm, dm), lambda i: (0, 0)),
                   pl.BlockSpec((8, dm), lambda i: (0, 0)), pl.BlockSpec((HALO, dm), lambda i: (0, 0))],
        out_shape=[jax.ShapeDtypeStruct(dp.shape, BF16), jax.ShapeDtypeStruct((dm, dm), F32),
                   jax.ShapeDtypeStruct((8, dm), F32), jax.ShapeDtypeStruct((HALO, dm), F32)],
        input_output_aliases={0: 0},
        scratch_shapes=[pltpu.VMEM((ts + HALO, dm), F32), pltpu.VMEM((ts + HALO, dm), F32),
                        pltpu.VMEM((7, ts + HALO - 8, dm), F32)],
        compiler_params=_params(("arbitrary",)),
    )(dp, dyc, p_conv, u0, u0, xh, rstd, conv_w, ln_g, ln_b, w_co)


def _att_masks():
    head0 = lax.broadcasted_iota(jnp.int32, (BLK, LANES), 1) < 64
    col = lax.broadcasted_iota(jnp.int32, (BLK, 4 * BLK), 1)
    row = lax.broadcasted_iota(jnp.int32, (BLK, 4 * BLK), 0)
    kj = col % BLK
    prev = jnp.where(col < 2 * BLK, 1, 0)
    band = jnp.where(col < 2 * BLK, kj - row, row - kj)
    return head0, band, prev


def _fill_block_diagonal(dst, slab, src_ref, halo_ref, head0, nq):
    sl = slice(slab * LANES, (slab + 1) * LANES)
    for b in range(nq + 1):
        blk = halo_ref[:, sl] if b == 0 else src_ref[(b - 1) * BLK:b * BLK, sl]
        base = (slab * (nq + 1) + b) * 2 * BLK
        zero = jnp.zeros_like(blk)
        dst[base:base + BLK, :] = jnp.where(head0, blk, zero)
        dst[base + BLK:base + 2 * BLK, :] = jnp.where(head0, zero, blk)


def _attention_forward(qkv, seq_len, qt, name):
    s_len = qkv.shape[1]
    nq = qt // BLK
    tiles_per_seq = seq_len // qt

    def body(q_ref, k_ref, v_ref, kh_ref, vh_ref, o_ref, lse_ref, kbd, vbd):
        first = jnp.where((pl.program_id(0) % tiles_per_seq) == 0, 4 * BLK, 0)
        head0, band, prev = _att_masks()
        band_first = band - prev * first
        for p in range(4):
            _fill_block_diagonal(kbd, p, k_ref, kh_ref, head0, nq)
            _fill_block_diagonal(vbd, p, v_ref, vh_ref, head0, nq)
        units = [(p, n) for p in range(4) for n in range(nq)]

        def keys_of(p, n):
            base = (p * (nq + 1) + n) * 2 * BLK
            return slice(base, base + 4 * BLK)

        def scores(p, n):
            q2 = q_ref[n * BLK:(n + 1) * BLK, p * LANES:(p + 1) * LANES] * 0.125
            return _dot_nt(q2, kbd[keys_of(p, n), :])

        def finish(p, n, o, den, lse):
            rows, sl = slice(n * BLK, (n + 1) * BLK), slice(p * LANES, (p + 1) * LANES)
            o_ref[rows, sl] = o / jnp.where(head0, den[0], den[1])
            lse_ref[rows, sl] = jnp.where(head0, lse[0], lse[1])

        s_next = scores(*units[0])
        pending = None
        for ui, (p, n) in enumerate(units):
            s = s_next
            if ui + 1 < len(units):
                s_next = scores(*units[ui + 1])
            s = jnp.where((band_first if n == 0 else band) >= 0, s, NEG)
            grp = [s[:, g * BLK:(g + 1) * BLK] for g in range(4)]
            ps, den, lse = [None] * 4, [], []
            for h in range(2):
                m = jnp.max(jnp.maximum(grp[h], grp[2 + h]), axis=-1, keepdims=True)
                ps[h], ps[2 + h] = jnp.exp(grp[h] - m), jnp.exp(grp[2 + h] - m)
                dn = jnp.sum(ps[h] + ps[2 + h], axis=-1, keepdims=True)
                den.append(dn)
                lse.append(m + jnp.log(dn))
            pmat = jnp.concatenate([x.astype(BF16) for x in ps], axis=1)
            o = _dot(pmat, vbd[keys_of(p, n), :])
            if pending is not None:
                finish(*pending)
            pending = (p, n, o, den, lse)
        finish(*pending)

    def which(w):
        return pl.BlockSpec((None, qt, ATT_W), lambda i: (w, i, 0))

    def halo(w):
        return pl.BlockSpec((None, BLK, ATT_W), lambda i: (w, jnp.maximum(i * nq - 1, 0), 0))

    out = pl.BlockSpec((qt, ATT_W), lambda i: (i, 0))
    bd = pltpu.VMEM((4 * (nq + 1) * 2 * BLK, LANES), BF16)
    return pl.pallas_call(
        body, name=name, grid=(s_len // qt,),
        in_specs=[which(0), which(1), which(2), halo(1), halo(2)],
        out_specs=[out, out], out_shape=[jax.ShapeDtypeStruct((s_len, ATT_W), F32)] * 2,
        scratch_shapes=[bd, bd],
        compiler_params=_params(("parallel",)),
    )(qkv, qkv, qkv, qkv, qkv)


def _attention_backward(qkv, d_att, lse, delta, seq_len, qt, name):
    s_len = qkv.shape[1]
    nq = qt // BLK
    tiles_per_seq = seq_len // qt
    nblk = s_len // BLK

    def body(q_ref, k_ref, v_ref, kh_ref, vh_ref, do_ref, lse_ref, dl_ref,
             qn_ref, don_ref, lsen_ref, dln_ref, dqkv_ref, kbd, vbd):
        i = pl.program_id(0)
        first = jnp.where((i % tiles_per_seq) == 0, 4 * BLK, 0)
        last = jnp.where((i % tiles_per_seq) == tiles_per_seq - 1, 4 * BLK, 0)
        head0, band, prev = _att_masks()
        band_first = band - prev * first
        band_tail = band[:, 0:2 * BLK] - last
        for p in range(4):
            _fill_block_diagonal(kbd, p, k_ref, kh_ref, head0, nq)
            _fill_block_diagonal(vbd, p, v_ref, vh_ref, head0, nq)
        units = [(p, n) for p in range(4) for n in range(nq + 1)]

        def stage_a(p, n):
            sl = slice(p * LANES, (p + 1) * LANES)
            base = (p * (nq + 1) + n) * 2 * BLK
            if n < nq:
                rows = slice(n * BLK, (n + 1) * BLK)
                q2, do2, lse2, dl2 = q_ref[rows, sl], do_ref[rows, sl], lse_ref[rows, sl], dl_ref[rows, sl]
                keys = slice(base, base + 4 * BLK)
            else:
                q2, do2, lse2, dl2 = qn_ref[:, sl], don_ref[:, sl], lsen_ref[:, sl], dln_ref[:, sl]
                keys = slice(base, base + 2 * BLK)
            s = _dot_nt(q2 * 0.125, kbd[keys, :])
            dp = _dot_nt(do2, vbd[keys, :])
            return q2, do2, lse2, dl2, keys, s, dp

        def stage_b(n, lse2, dl2, s, dp):
            mask = band_tail if n == nq else (band_first if n == 0 else band)
            ps, dss = [], []
            for g in range(s.shape[1] // BLK):
                h = g % 2
                cols = slice(g * BLK, (g + 1) * BLK)
                pg = jnp.exp(jnp.where(mask[:, cols] >= 0, s[:, cols] - lse2[:, h * 64:h * 64 + 1], NEG))
                ps.append(pg.astype(BF16))
                dss.append((pg * (dp[:, cols] - dl2[:, h * 64:h * 64 + 1]) * 0.125).astype(BF16))
            return jnp.concatenate(ps, axis=1), jnp.concatenate(dss, axis=1)

        def heads(r, g):
            return jnp.where(head0, r[g * BLK:(g + 1) * BLK, :], r[(g + 1) * BLK:(g + 2) * BLK, :])

        a_next = stage_a(*units[0])
        carry = None
        for ui, (p, n) in enumerate(units):
            q2, do2, lse2, dl2, keys, s, dp = a_next
            if ui + 1 < len(units):
                a_next = stage_a(*units[ui + 1])
            pmat, dsmat = stage_b(n, lse2, dl2, s, dp)
            sl = slice(p * LANES, (p + 1) * LANES)
            if n < nq:
                dqkv_ref[0, n * BLK:(n + 1) * BLK, sl] = _dot(dsmat, kbd[keys, :])
            dkbd = _dot_tn(dsmat, q2)
            dvbd = _dot_tn(pmat, do2)
            if n > 0:
                prow = slice((n - 1) * BLK, n * BLK)
                dqkv_ref[1, prow, sl] = carry[0] + heads(dkbd, 0)
                dqkv_ref[2, prow, sl] = carry[1] + heads(dvbd, 0)
            if n < nq:
                carry = (heads(dkbd, 2), heads(dvbd, 2))

    def which(w):
        return pl.BlockSpec((None, qt, ATT_W), lambda i: (w, i, 0))

    def prev(w):
        return pl.BlockSpec((None, BLK, ATT_W), lambda i: (w, jnp.maximum(i * nq - 1, 0), 0))

    tile = pl.BlockSpec((qt, ATT_W), lambda i: (i, 0))
    nxt = pl.BlockSpec((BLK, ATT_W), lambda i: (jnp.minimum((i + 1) * nq, nblk - 1), 0))
    nxt_q = pl.BlockSpec((None, BLK, ATT_W), lambda i: (0, jnp.minimum((i + 1) * nq, nblk - 1), 0))
    return pl.pallas_call(
        body, name=name, grid=(s_len // qt,),
        in_specs=[which(0), which(1), which(2), prev(1), prev(2), tile, tile, tile, nxt_q, nxt, nxt, nxt],
        out_specs=pl.BlockSpec((3, qt, ATT_W), lambda i: (0, i, 0)),
        out_shape=jax.ShapeDtypeStruct((3, s_len, ATT_W), F32),
        scratch_shapes=[pltpu.VMEM((4 * (nq + 1) * 2 * BLK, LANES), BF16)] * 2,
        compiler_params=_params(("parallel",)),
    )(qkv, qkv, qkv, qkv, qkv, d_att, lse, delta, qkv, d_att, lse, delta)


def _merge_and_head(o_g, lse_g, p_gate, y_conv, x, tgt, modv, final_g, w_ao, w_o, ts):
    s_len, dm = x.shape
    gw = ATT_W + 2 * dm
    nt = s_len // ts
    gate_off = 3 * dm + 9 * ATT_W
    assert gate_off % gw == 0

    def body(o0, o1, o2, l0, l1, l2, pg_ref, yc_ref, x_ref, t_ref, mod_ref, fg_ref, wao_ref, wo_ref,
             loss_ref, dx_ref, dyc_ref, dpg_ref, da0, da1, da2, ls0, ls1, ls2, de0, de1, de2,
             gwo_ref, gwao_ref, gs_ref, slab, ones_scr):
        i = pl.program_id(0)

        @pl.when(i == 0)
        def _():
            loss_ref[...] = jnp.zeros_like(loss_ref)
            gwo_ref[...] = jnp.zeros_like(gwo_ref)
            gwao_ref[...] = jnp.zeros_like(gwao_ref)
            gs_ref[...] = jnp.zeros_like(gs_ref)
            ri = lax.broadcasted_iota(jnp.int32, (ATT_W, ATT_W), 0) // 64
            ci = lax.broadcasted_iota(jnp.int32, (ATT_W, ATT_W), 1) // 64
            ones_scr[...] = jnp.where(ri == ci, 1.0, 0.0).astype(BF16)

        os_, ls_ = [], []
        for dil, o_ref, l_ref in zip(DILATIONS, (o0, o1, o2), (l0, l1, l2)):
            os_.append(jnp.concatenate(_interleave_load(o_ref, slab, dil, ts), axis=1))
            ls_.append(jnp.concatenate(_interleave_load(l_ref, slab, dil, ts), axis=1))
        mx = jnp.maximum(jnp.maximum(ls_[0], ls_[1]), ls_[2])
        wts = [jnp.exp(l - mx) for l in ls_]
        wsum = wts[0] + wts[1] + wts[2]
        att = (wts[0] * os_[0] + wts[1] * os_[1] + wts[2] * os_[2]) / wsum
        lse_all = mx + jnp.log(wsum)

        z_att, g_conv, g_att = pg_ref[:, 0:ATT_W], pg_ref[:, ATT_W:ATT_W + dm], pg_ref[:, ATT_W + dm:gw]
        sgz = _sig(z_att)
        sz = z_att * sgz
        a_att = (att * sz).astype(BF16)
        y_att = _dot(a_att, wao_ref[...])
        y_conv = yc_ref[...]
        sgc, sga = _sig(g_conv), _sig(g_att)
        merged = (sgc * y_conv + sga * y_att).astype(BF16)
        mo = _dot(merged, wo_ref[...])
        gate = mod_ref[2:3, :]
        x2 = x_ref[...] + gate * mo
        r = lax.rsqrt(jnp.mean(x2 * x2, axis=-1, keepdims=True) + EPS)
        xr = x2 * r
        err = xr * fg_ref[...] - t_ref[...]
        loss_ref[...] += 0.5 * jnp.sum(jnp.mean(err * err, axis=-1, keepdims=True))
        dy = err * (1.0 / dm)
        gs_ref[0:1, :] += jnp.sum(dy * xr, axis=0, keepdims=True)
        dyg = dy * fg_ref[...]
        d_x2 = r * dyg - xr * (r * jnp.mean(dyg * xr, axis=-1, keepdims=True))
        dx_ref[...] = d_x2
        gs_ref[1:2, :] += jnp.sum(d_x2 * mo, axis=0, keepdims=True)
        d_mo = (d_x2 * gate).astype(BF16)
        d_mg = _dot_nt(d_mo, wo_ref[...])
        gwo_ref[...] += _dot_tn(merged, d_mo)
        dyc_ref[...] = (d_mg * sgc).astype(BF16)
        dpg_ref[:, ATT_W:ATT_W + dm] = (d_mg * y_conv * sgc * (1.0 - sgc)).astype(BF16)
        d_ya = (d_mg * sga).astype(BF16)
        dpg_ref[:, ATT_W + dm:gw] = (d_mg * y_att * sga * (1.0 - sga)).astype(BF16)
        gwao_ref[...] += _dot_tn(a_att, d_ya)
        d_aa = _dot_nt(d_ya, wao_ref[...])
        dpg_ref[:, 0:ATT_W] = (d_aa * att * _dsilu(z_att, sgz)).astype(BF16)
        d_att = d_aa * sz
        ones = ones_scr[...]
        prod = d_att * att
        hi = prod.astype(BF16)
        lo = (prod - hi.astype(F32)).astype(BF16)
        delta = _dot(hi, ones) + _dot(lo, ones)
        for val, refs, dt in ((d_att, (da0, da1, da2), BF16), (lse_all, (ls0, ls1, ls2), F32),
                              (delta, (de0, de1, de2), F32)):
            vals = [val[:, s * LANES:(s + 1) * LANES] for s in range(4)]
            for dil, ref in zip(DILATIONS, refs):
                _deinterleave_store(vals, slab, ref, dil, ts, dt)

    def grp(dil):
        return pl.BlockSpec((dil, ts // dil, ATT_W), lambda i: (0, i, 0))

    tile = pl.BlockSpec((ts, dm), lambda i: (i, 0))
    gate_tile = pl.BlockSpec((ts, gw), lambda i: (i, 0))
    const = lambda shp: pl.BlockSpec(shp, lambda i: tuple(0 for _ in shp))
    grp_shape = lambda dt: [jax.ShapeDtypeStruct((dil, s_len // dil, ATT_W), dt) for dil in DILATIONS]
    return pl.pallas_call(
        body, name="merge_and_head", grid=(nt,),
        in_specs=[grp(d) for d in DILATIONS] * 2
        + [gate_tile, tile, tile, tile, const((8, dm)), const((1, dm)), const((ATT_W, dm)), const((dm, dm))],
        out_specs=[const((8, LANES)), tile, tile, pl.BlockSpec((ts, gw), lambda i: (i, gate_off // gw))]
        + [grp(d) for d in DILATIONS] * 3
        + [const((dm, dm)), const((ATT_W, dm)), const((8, dm))],
        out_shape=[jax.ShapeDtypeStruct((8, LANES), F32), jax.ShapeDtypeStruct((s_len, dm), F32),
                   jax.ShapeDtypeStruct((s_len, dm), BF16), jax.ShapeDtypeStruct((s_len, gate_off + gw), BF16)]
        + grp_shape(BF16) + grp_shape(F32) + grp_shape(F32)
        + [jax.ShapeDtypeStruct((dm, dm), F32), jax.ShapeDtypeStruct((ATT_W, dm), F32),
           jax.ShapeDtypeStruct((8, dm), F32)],
        scratch_shapes=[pltpu.VMEM((8, ts, LANES), F32), pltpu.VMEM((ATT_W, ATT_W), BF16)],
        compiler_params=_params(("arbitrary",)),
    )(*o_g, *lse_g, p_gate, y_conv, x, tgt, modv, final_g, w_ao, w_o)


def _qkv_grad_to_tokens(dp, dqkv_g, rope, ts):
    s_len, dm = rope.shape[1], (dp.shape[1] - 10 * ATT_W) // 5
    qw = 3 * ATT_W
    assert (3 * dm) % qw == 0

    def body(dp_in, g0, g1, g2, rope_ref, o_ref, slab):
        del dp_in
        w = pl.program_id(1)
        rc, ra, rb = rope_ref.at[0], rope_ref.at[1], rope_ref.at[2]

        def emit(roped):
            for gi, (dil, g_ref) in enumerate(zip(DILATIONS, (g0, g1, g2))):
                vals = _interleave_load(g_ref, slab, dil, ts)
                for s in range(4):
                    t = vals[s]
                    if roped:
                        t = t * rc[...] + pltpu.roll(t * ra[...], 8, 1) + pltpu.roll(t * rb[...], LANES - 8, 1)
                    col = gi * ATT_W + s * LANES
                    o_ref[:, col:col + LANES] = t.astype(BF16)

        pl.when(w < 2)(lambda: emit(True))
        pl.when(w == 2)(lambda: emit(False))

    return pl.pallas_call(
        body, name="qkv_grad_to_tokens", grid=(s_len // ts, 3),
        in_specs=[ANY] + [pl.BlockSpec((None, dil, ts // dil, ATT_W), lambda i, w: (w, 0, i, 0)) for dil in DILATIONS]
        + [pl.BlockSpec((3, ts, LANES), lambda i, w: (0, i, 0))],
        out_specs=pl.BlockSpec((ts, qw), lambda i, w: (i, 3 * dm // qw + w)),
        out_shape=jax.ShapeDtypeStruct(dp.shape, BF16),
        input_output_aliases={0: 0},
        scratch_shapes=[pltpu.VMEM((8, ts, LANES), F32)],
        compiler_params=_params(("parallel", "arbitrary")),
    )(dp, *dqkv_g, rope)


def _wide_col_tile(cols):
    for width in (5 * COL_TILE, 2 * COL_TILE):
        if cols % width == 0:
            return width
    return COL_TILE


def _input_grad(dp, w_in, x, dx_res, modv, norm_g, ts):
    s_len, dm = x.shape
    ct = _wide_col_tile(dp.shape[1])
    nct = dp.shape[1] // ct

    def body(p_ref, w_ref, x_ref, dxr_ref, mod_ref, g_ref, gx_ref, gs_ref):
        @pl.when(pl.program_id(0) == 0)
        def _():
            gs_ref[...] = jnp.zeros_like(gs_ref)

        d_h = _dot_nt(p_ref[:, 0:ct], w_ref[:, 0:ct])
        for j in range(1, nct):
            d_h = d_h + _dot_nt(p_ref[:, j * ct:(j + 1) * ct], w_ref[:, j * ct:(j + 1) * ct])
        xv = x_ref[...]
        r = lax.rsqrt(jnp.mean(xv * xv, axis=-1, keepdims=True) + EPS)
        xr = xv * r
        gs_ref[0:1, :] += jnp.sum(d_h, axis=0, keepdims=True)
        gs_ref[1:2, :] += jnp.sum(d_h * (xr * g_ref[...]), axis=0, keepdims=True)
        d_n = d_h * (1.0 + mod_ref[1:2, :])
        gs_ref[2:3, :] += jnp.sum(d_n * xr, axis=0, keepdims=True)
        dxn = d_n * g_ref[...]
        gx_ref[...] = dxr_ref[...] + r * dxn - xr * (r * jnp.mean(dxn * xr, axis=-1, keepdims=True))

    tile = pl.BlockSpec((ts, dm), lambda i: (i, 0))
    return pl.pallas_call(
        body, name="input_grad", grid=(s_len // ts,),
        in_specs=[pl.BlockSpec((ts, dp.shape[1]), lambda i: (i, 0)),
                  pl.BlockSpec((dm, dp.shape[1]), lambda i: (0, 0), pipeline_mode=pl.Buffered(1)), tile, tile,
                  pl.BlockSpec((8, dm), lambda i: (0, 0)), pl.BlockSpec((1, dm), lambda i: (0, 0))],
        out_specs=[tile, pl.BlockSpec((8, dm), lambda i: (0, 0))],
        out_shape=[jax.ShapeDtypeStruct((s_len, dm), F32), jax.ShapeDtypeStruct((8, dm), F32)],
        compiler_params=_params(("arbitrary",)),
    )(dp, w_in, x, dx_res, modv, norm_g)


def _w_in_grad(h_t, dp, ts, tiles, name, prev=None, after=()):
    dm, s_len = h_t.shape
    ct = _wide_col_tile(dp.shape[1])
    lo, hi = tiles
    extra = ([prev] if prev is not None else []) + list(after)

    def body(h_ref, p_ref, *rest):
        o_ref = rest[-1]

        @pl.when(pl.program_id(1) == 0)
        def _():
            o_ref[...] = jnp.zeros_like(o_ref)

        o_ref[...] += _dot(h_ref[...], p_ref[...])

    return pl.pallas_call(
        body, name=name, grid=(hi - lo, s_len // ts),
        in_specs=[pl.BlockSpec((dm, ts), lambda j, i: (0, i)), pl.BlockSpec((ts, ct), lambda j, i: (i, lo + j))]
        + [ANY] * len(extra),
        out_specs=pl.BlockSpec((dm, ct), lambda j, i: (0, lo + j)),
        out_shape=jax.ShapeDtypeStruct((dm, dp.shape[1]), F32),
        input_output_aliases={2: 0} if prev is not None else {},
        compiler_params=_params(("arbitrary", "arbitrary")),
    )(h_t, dp, *extra)


def _rope_lane_table():
    l64 = jnp.arange(LANES) % 64
    half = ROT_DIM // 2
    inv_freq = ROPE_THETA ** (-(jnp.arange(half, dtype=F32) * 2.0 / ROT_DIM))
    rot = l64 < ROT_DIM
    rows = [jnp.where(rot, inv_freq[l64 % half], 0.0), (l64 < half).astype(F32),
            ((l64 >= half) & rot).astype(F32), rot.astype(F32)]
    return jnp.concatenate([jnp.stack(rows), jnp.zeros((4, LANES), F32)], axis=0)


def _tile_sizes(s_len):
    ts_big = min(1024, s_len // 2)
    ts_mid = 256
    ts_head = 256
    qt = [min(1024, s_len // dil) for dil in DILATIONS]
    return ts_big, ts_mid, ts_head, qt


def kernel(x, c, positions, norm_g, w_ada, b_ada, w_in, conv_w, conv_b, conv_ln_g, conv_ln_b, w_conv_out, w_att_out, w_o, final_g, loss_target, m_norm_g, m_w_ada, m_b_ada, m_w_in, m_conv_w, m_conv_b, m_conv_ln_g, m_conv_ln_b, m_w_conv_out, m_w_att_out, m_w_o, m_final_g, v_norm_g, v_w_ada, v_b_ada, v_w_in, v_conv_w, v_conv_b, v_conv_ln_g, v_conv_ln_b, v_w_conv_out, v_w_att_out, v_w_o, v_final_g):
    s_len, dm = x.shape[1], x.shape[2]
    ts_big, ts_mid, ts_head, qt = _tile_sizes(s_len)
    xi, yi, cidx = _my_place()
    chip = 2 * xi + yi
    batch = 4 * xi + 2 * yi + cidx
    x2d, tgt = x[0], loss_target[0]
    pos = positions.reshape(s_len, 1)
    wa_l, wi_l, cw_l = w_ada[0], w_in[0], conv_w[0]
    wco_l, wao_l, wo_l = w_conv_out[0], w_att_out[0], w_o[0]
    ada_w = wa_l.shape[1]
    cw_cols = cw_l.shape[1]

    kinds = ("col", "row", "col", "row")
    w_bufs = _cast_weights([wi_l, wco_l, wao_l, wo_l], kinds)

    cw_pad = jnp.pad(cw_l, ((0, HALO - CONV_K), (0, 0)))
    small_in = jnp.concatenate([jnp.broadcast_to(c, (8, dm)), cw_pad.reshape(8, dm)], axis=0)
    small = _allgather_small(small_in).reshape(8, 16, dm)
    c_all = small[:, 0, :]
    conv_w_full = jnp.concatenate(
        [small[2 * p, 8:16, :].reshape(HALO, cw_cols) for p in range(4)], axis=1)
    b_l = lax.dynamic_slice(b_ada, (0, chip * ada_w), (1, ada_w))
    mod_parts = _allgather_small(_mod_part(c_all, wa_l, b_l)).reshape(8, 8, ada_w)
    mod_rows = lax.dynamic_index_in_dim(mod_parts, batch, axis=1, keepdims=False)
    mod = jnp.concatenate([mod_rows[2 * p] for p in range(4)], axis=0).reshape(3, dm)
    modv = jnp.concatenate([mod, jnp.zeros((5, dm), F32)], axis=0)

    sems_a, sems_b, w_bufs, token = _gather_weights_start(w_bufs, kinds, modv)
    rope = _rope_coefficients(pos, _rope_lane_table() + token, ts_big)
    h_b, h_t = _norm_modulate(x2d, modv + token[0, 0], norm_g, ts_big)
    (w_in_b,) = _gather_weights_wait(w_bufs, kinds, sems_a, 0, (rope, h_b), "gather_weights_wait_w_in")
    w_in_b = _pass_to_sibling(w_in_b, kinds[0])

    n_conv, n_gate = 3 * dm // COL_TILE, (ATT_W + 2 * dm) // COL_TILE
    ts_proj = ts_big
    p_conv = _plain_projection(h_b, w_in_b, 0, n_conv, ts_proj, "conv_projection")
    p_gate = _plain_projection(h_b, w_in_b, n_conv + 9, n_gate, ts_proj, "gate_projection")
    qkv_g = [_qkv_projection(h_b, rope, w_in_b, gi, min(2 * ts_big, s_len)) for gi in range(3)]
    w_co_b, w_ao_b, w_o_b = _gather_weights_wait(w_bufs, kinds, sems_b, 1, (qkv_g[2], p_conv, p_gate),
                                                  "gather_weights_wait_rest")
    y_conv, u0, xh, rstd = _conv_forward(p_conv, conv_w_full, conv_b, conv_ln_g, conv_ln_b, w_co_b, 2 * ts_mid)
    qkv_flat = [q.reshape(3, s_len, ATT_W) for q in qkv_g]
    o_g, lse_g = [], []
    for gi, dil in enumerate(DILATIONS):
        o, l = _attention_forward(qkv_flat[gi], s_len // dil, qt[gi], "attention_forward_%d" % dil)
        o_g.append(o.reshape(dil, s_len // dil, ATT_W))
        lse_g.append(l.reshape(dil, s_len // dil, ATT_W))

    (loss_p, dx_res, dyc, dp, da0, da1, da2, ls0, ls1, ls2, de0, de1, de2,
     g_wo, g_wao, head_sums) = _merge_and_head(o_g, lse_g, p_gate, y_conv, x2d, tgt, modv,
                                               final_g.reshape(1, dm), w_ao_b, w_o_b, ts_head)

    dp, g_wco, conv_sums, g_cw = _conv_backward(dp, dyc, p_conv, u0, xh, rstd, conv_w_full, conv_ln_g, conv_ln_b,
                                                w_co_b, 2 * ts_mid)
    dqkv_g = []
    for gi, (dil, da, ls, de) in enumerate(zip(DILATIONS, (da0, da1, da2), (ls0, ls1, ls2), (de0, de1, de2))):
        flat = lambda a: a.reshape(s_len, ATT_W)
        dqkv = _attention_backward(qkv_flat[gi], flat(da), flat(ls), flat(de), s_len // dil, qt[gi],
                                   "attention_backward_%d" % dil)
        dqkv_g.append(dqkv.reshape(3, dil, s_len // dil, ATT_W))
    dp = _qkv_grad_to_tokens(dp, dqkv_g, rope, ts_big)
    ts_wg = min(2 * ts_big, s_len)
    n_wt = dp.shape[1] // _wide_col_tile(dp.shape[1])
    n_first = max(1, (3 * n_wt) // 4)
    split_col = n_first * _wide_col_tile(dp.shape[1])
    g_first = _w_in_grad(h_t, dp, ts_wg, (0, n_first), "w_in_grad_first")
    grads = [g_first, g_wco, g_wao, g_wo]
    full_shapes = [g.shape for g in grads]
    pas = [pltpu.with_memory_space_constraint(
        lax.empty((r // 2, c_) if kind == "col" else (r, c_ // 2), F32), pltpu.HBM)
        for (r, c_), kind in zip(full_shapes, kinds)]
    pe_s, pe_r, grads, pas, pe_token = _pair_exchange_start(grads, pas, kinds, (0, split_col))
    grads[0] = _w_in_grad(h_t, dp, ts_wg, (n_first, n_wt), "w_in_grad_second", prev=grads[0], after=(pe_token,))
    grads, recv_halves = _pair_exchange_wait(pe_s, pe_r, grads, pas, kinds, (0, split_col))
    recv_halves[0] = _pair_exchange_rest(grads[0], recv_halves[0], kinds[0], (split_col, dp.shape[1]))

    c_arr = jnp.reshape(cidx, (1,)).astype(jnp.int32)
    halves = [_reduce_pair_sum(g, pa, kind, c_arr, "reduce_pair_sum_%d" % k)
              for k, (g, pa, kind) in enumerate(zip(grads, recv_halves, kinds))]
    send_sems, recv_sems, halves, lands, token = _reduce_to_owner_start(halves, kinds, full_shapes)
    grad_x, in_sums = _input_grad(dp, w_in_b, x2d, dx_res, modv + token[0, 0], norm_g, ts_mid)
    halves, recvd = _reduce_to_owner_wait(send_sems, recv_sems, halves, lands, in_sums, kinds)
    gr_win, gr_wco, gr_wao, gr_wo = _reduce_finish(halves, recvd, kinds, full_shapes)

    rows = [in_sums[2:3], conv_sums[2:3], conv_sums[0:1], conv_sums[1:2], head_sums[0:1],
            in_sums[0:1], in_sums[1:2], head_sums[1:2], g_cw, jnp.pad(loss_p, ((0, 0), (0, dm - LANES)))]
    part = jnp.concatenate(rows, axis=0)
    buf = lax.dynamic_update_slice(jnp.zeros((8 * 48, dm), F32), part, (batch * 48, 0))
    row_sems_s, row_sems_r, buf, row_token = _row_gather_start(buf, 48, gr_win)
    upd = {
        "w_in": _adamw(wi_l, gr_win, m_w_in[0], v_w_in[0], "adamw_w_in", (row_token,)),
        "w_co": _adamw(wco_l, gr_wco, m_w_conv_out[0], v_w_conv_out[0], "adamw_w_conv_out", (row_token,)),
        "w_ao": _adamw(wao_l, gr_wao, m_w_att_out[0], v_w_att_out[0], "adamw_w_att_out", (row_token,)),
        "w_o": _adamw(wo_l, gr_wo, m_w_o[0], v_w_o[0], "adamw_w_o", (row_token,)),
    }
    gathered = _row_gather_wait(row_sems_s, row_sems_r, buf, 48, [upd[k][0] for k in ("w_in", "w_co", "w_ao", "w_o")])
    tot = _sum_devices(gathered, 48)
    loss = tot[8 + HALO, 0]
    dmod_all = gathered.reshape(8, 48, dm)[:, 5:8, :].reshape(8, 3 * dm)
    dmod_l = lax.dynamic_slice(dmod_all, (0, chip * ada_w), (8, ada_w))
    gr_wada = _grad_w_ada(c_all.T, dmod_l)
    gr_cw = lax.dynamic_slice(tot[8:8 + HALO], (0, chip * cw_cols), (HALO, cw_cols))

    pad_cw = lambda a: jnp.pad(a[0], ((0, HALO - CONV_K), (0, 0)))
    row = lambda a: a.reshape(1, dm)
    small_upd = _adamw_small(
        [norm_g, conv_b, conv_ln_g, conv_ln_b, row(final_g), b_ada], tot[0:8],
        [m_norm_g, m_conv_b, m_conv_ln_g, m_conv_ln_b, row(m_final_g), m_b_ada],
        [v_norm_g, v_conv_b, v_conv_ln_g, v_conv_ln_b, row(v_final_g), v_b_ada])
    upd["w_ada"] = _adamw(wa_l, gr_wada, m_w_ada[0], v_w_ada[0], "adamw_w_ada")
    upd["conv_w"] = _adamw(cw_pad, gr_cw, pad_cw(m_conv_w), pad_cw(v_conv_w), "adamw_conv_w")

    def family(which):
        if which is None:
            sm = small_upd[0:6]
            big = {"w_ada": gr_wada, "w_in": gr_win, "conv_w": gr_cw, "w_co": gr_wco, "w_ao": gr_wao, "w_o": gr_wo}
        else:
            sm = small_upd[6 * (which + 1):6 * (which + 2)]
            big = {k: upd[k][which] for k in ("w_ada", "w_in", "conv_w", "w_co", "w_ao", "w_o")}
        return [sm[0], big["w_ada"][None], sm[5], big["w_in"][None],
                big["conv_w"][None, :CONV_K], sm[1], sm[2], sm[3], big["w_co"][None],
                big["w_ao"][None], big["w_o"][None], sm[4].reshape(dm)]

    return (loss, grad_x[None], *family(None), *family(0), *family(1), *family(2))
```
